```python
import math
import jax
import jax.numpy as jnp
from jax import lax
import numpy as np

D_MODEL = 1024
BATCH = 8
SEQ = 8192
DEPTH = 2

GRID_W = 64
CTX_LEN = 256
HEAD_DIM = 64
D_MIX = D_MODEL
D_NA = D_MIX // 2
D_ML = D_MIX // 4
D_HY = D_MIX // 4
H_NA = D_NA // HEAD_DIM
H_ML = D_ML // HEAD_DIM
NA_ROWS = 8
NA_COLS = 16
NA_QBLK = 16
NA_KBLK = 2 * NA_COLS
ML_CHUNK = 64
HY_ORDER = 2
HY_EMB = 33
HY_FFN = 64
HY_TARGET = 1e-2
HY_FAST = 0.3
HY_SLOW = 1.5
ROPE_BASE = 10000.0
EPS = 1e-6
SPLIT_SIZES = (D_NA,) * 4 + (D_ML,) * 5 + (4 * H_ML,) + (D_HY,) * 4
N_IN = 4 * D_NA + 5 * D_ML + 4 * H_ML + 4 * D_HY

kernel_name = 'hybrid_natten_mlstm_hyena_block'


def _rmsnorm(x, g):
    xf = x.astype(jnp.float32)
    y = xf * lax.rsqrt(jnp.mean(xf * xf, axis=-1, keepdims=True) + EPS)
    return (y * g.astype(jnp.float32)).astype(x.dtype)


def _split_cols(p):
    idx = np.cumsum(SPLIT_SIZES)[:-1].tolist()
    return jnp.split(p, idx, axis=-1)


def _heads(a, n_heads):
    return a.reshape(a.shape[:-1] + (n_heads, a.shape[-1] // n_heads))


def _dwconv3(u, w):
    up = jnp.pad(u, ((0, 0), (1, 1), (0, 0)))
    return up[:, :-2] * w[0] + up[:, 1:-1] * w[1] + up[:, 2:] * w[2]


def _axial_rope(x):
    L, d = x.shape[1], x.shape[-1]
    n = d // 4
    t = jnp.arange(L)
    inv = ROPE_BASE ** (-jnp.arange(n, dtype=jnp.float32) / n)
    pos = jnp.stack([t // GRID_W, t % GRID_W], axis=-1).astype(jnp.float32)
    ang = pos[:, :, None] * inv
    cos = jnp.cos(ang)[None, :, None]
    sin = jnp.sin(ang)[None, :, None]
    xr = x.astype(jnp.float32).reshape(x.shape[:-1] + (2, 2, n))
    x1, x2 = xr[..., 0, :], xr[..., 1, :]
    out = jnp.stack([x1 * cos - x2 * sin, x1 * sin + x2 * cos], axis=-2).reshape(x.shape)
    return out.astype(x.dtype)


def _neighbourhood_attention(q, k, v, kc, vc, rpb, rows):
    nb, L, nh, d = q.shape
    kr = min(NA_ROWS, rows)
    scale = d ** -0.5
    qg = q.reshape(nb, rows, GRID_W, nh, d)
    kg = k.reshape(nb, rows, GRID_W, nh, d)
    vg = v.reshape(nb, rows, GRID_W, nh, d)
    n_cb = GRID_W // NA_QBLK
    cb_start = np.clip(np.arange(n_cb) * NA_QBLK - NA_COLS // 2, 0, GRID_W - NA_KBLK)
    col_idx = cb_start[:, None] + np.arange(NA_KBLK)
    qcol = np.arange(GRID_W).reshape(n_cb, NA_QBLK)
    qstart = np.clip(qcol - NA_COLS // 2, 0, GRID_W - NA_COLS)
    kcol = col_idx[:, None, :]
    col_mask = (kcol >= qstart[..., None]) & (kcol < qstart[..., None] + NA_COLS)
    col_off = np.clip(kcol - qcol[..., None], -(NA_COLS - 1), NA_COLS - 1) + NA_COLS - 1
    rpb_col = rpb[:, :, col_off]
    row_start = jnp.clip(jnp.arange(rows) - NA_ROWS // 2, 0, rows - kr)
    mask = jnp.asarray(col_mask)[None, None, :, :, None, :]

    def row_block(r):
        rs = row_start[r]
        qr = qg[:, r].reshape(nb, n_cb, NA_QBLK, nh, d)
        kw = lax.dynamic_slice_in_dim(kg, rs, kr, axis=1)[:, :, col_idx]
        vw = lax.dynamic_slice_in_dim(vg, rs, kr, axis=1)[:, :, col_idx]
        s_lat = jnp.einsum('bjqhd,bijchd->bhjqic', qr, kw).astype(jnp.float32) * scale
        roff = rs + jnp.arange(kr) - r + (NA_ROWS - 1)
        bias = jnp.transpose(rpb_col[:, roff], (0, 2, 3, 1, 4)).astype(jnp.float32)
        s_lat = jnp.where(mask, s_lat + bias[None], -jnp.inf)
        s_ctx = jnp.einsum('bjqhd,bnhd->bhjqn', qr, kc).astype(jnp.float32) * scale
        s = jnp.concatenate([s_lat.reshape(s_lat.shape[:4] + (kr * NA_KBLK,)), s_ctx], axis=-1)
        p = jax.nn.softmax(s, axis=-1).astype(v.dtype)
        p_lat = p[..., :kr * NA_KBLK].reshape(s_lat.shape)
        p_ctx = p[..., kr * NA_KBLK:]
        o = jnp.einsum('bhjqic,bijchd->bjqhd', p_lat, vw) + jnp.einsum('bhjqn,bnhd->bjqhd', p_ctx, vc)
        return o.reshape(nb, GRID_W, nh * d)

    out = lax.map(row_block, jnp.arange(rows))
    return jnp.moveaxis(out, 0, 1).reshape(nb, L, nh * d)


def _ctx_attention(q, k, v):
    nb, n, nh, d = q.shape
    s = jnp.einsum('bnhd,bmhd->bhnm', q, k).astype(jnp.float32) * d ** -0.5
    p = jax.nn.softmax(s, axis=-1).astype(v.dtype)
    return jnp.einsum('bhnm,bmhd->bnhd', p, v).reshape(nb, n, nh * d)


def _mlstm_scan(q, k, v, li, lf, state, emit):
    nb, nh, L, d = q.shape
    nc = L // ML_CHUNK

    def chunks(a):
        a = a.reshape(a.shape[:2] + (nc, ML_CHUNK) + a.shape[3:])
        return jnp.moveaxis(a, 2, 0)

    tril = jnp.tril(jnp.ones((ML_CHUNK, ML_CHUNK), dtype=bool))

    def step(carry, inp):
        C, n, m = carry
        qc, kc, vc, ic, fc = inp
        b = jnp.cumsum(fc, axis=-1)
        dlog = jnp.where(tril, b[..., :, None] - b[..., None, :] + ic[..., None, :], -jnp.inf)
        inter = b + m[..., None]
        m_t = jnp.maximum(inter, jnp.max(dlog, axis=-1))
        s = jnp.einsum('bhtk,bhsk->bhts', qc, kc) * jnp.exp(dlog - m_t[..., None])
        a = jnp.exp(inter - m_t)
        num = a[..., None] * jnp.einsum('bhvk,bhtk->bhtv', C, qc) + jnp.einsum('bhts,bhsv->bhtv', s, vc)
        den = a * jnp.einsum('bhk,bhtk->bht', n, qc) + jnp.sum(s, axis=-1)
        h = num / jnp.maximum(jnp.abs(den), jnp.exp(-m_t))[..., None]
        b_end = b[..., -1]
        g = b_end[..., None] - b + ic
        m_new = jnp.maximum(b_end + m, jnp.max(g, axis=-1))
        decay = jnp.exp(b_end + m - m_new)
        w = jnp.exp(g - m_new[..., None])
        C = decay[..., None, None] * C + jnp.einsum('bhsv,bhsk->bhvk', w[..., None] * vc, kc)
        n = decay[..., None] * n + jnp.einsum('bhs,bhsk->bhk', w, kc)
        return (C, n, m_new), (h if emit else None)

    state, h = lax.scan(step, state, tuple(chunks(a) for a in (q, k, v, li, lf)))
    if emit:
        h = jnp.moveaxis(h, 0, 2).reshape(nb, nh, L, d)
    return h, state


def _mlstm_bidirectional(qc, kc, vc, gc, ql, kl, vl, gl, emit_ctx):
    nb, nh, _, d = ql.shape
    zero = (jnp.zeros((nb, nh, d, d), jnp.float32), jnp.zeros((nb, nh, d), jnp.float32),
            jnp.zeros((nb, nh), jnp.float32))
    outs_lat, outs_ctx = [], []
    for direction in range(2):
        flip = (lambda a: jnp.flip(a, axis=2)) if direction else (lambda a: a)
        hc, st = _mlstm_scan(flip(qc), flip(kc), flip(vc), flip(gc[:, direction, 0]),
                             flip(jax.nn.log_sigmoid(gc[:, direction, 1])), zero, emit_ctx)
        hl, _ = _mlstm_scan(flip(ql), flip(kl), flip(vl), flip(gl[:, direction, 0]),
                            flip(jax.nn.log_sigmoid(gl[:, direction, 1])), st, True)
        outs_lat.append(flip(hl))
        if emit_ctx:
            outs_ctx.append(flip(hc))
    h_ctx = (outs_ctx[0] + outs_ctx[1]) if emit_ctx else None
    return outs_lat[0] + outs_lat[1], h_ctx


def _mlstm_inputs(qm, km, vm, gm, conv_ml, b_if, rope):
    qk = jax.nn.silu(_dwconv3(jnp.concatenate([qm, km], axis=-1), conv_ml))
    q, k = jnp.split(qk, 2, axis=-1)
    q, k, v = _heads(q, H_ML), _heads(k, H_ML), _heads(vm, H_ML)
    if rope:
        q, k = _axial_rope(q), _axial_rope(k)
    to = lambda a: jnp.swapaxes(a, 1, 2).astype(jnp.float32)
    g = (gm.reshape(gm.shape[:-1] + (2, 2, H_ML)) + b_if).astype(jnp.float32)
    g = jnp.transpose(g, (0, 2, 3, 4, 1))
    return to(q), to(k) * HEAD_DIM ** -0.5, to(v), g


def _merge_heads(h):
    nb, nh, L, d = h.shape
    return jnp.swapaxes(h, 1, 2).reshape(nb, L, nh * d)


def _hyena_filters(L, w1, b1, w2, b2, w3, freq):
    f32 = jnp.float32
    t = jnp.arange(L, dtype=f32)
    tn = t / (L - 1)
    bands = (HY_EMB - 1) // 2
    fr = jnp.linspace(1e-4, bands - 1, bands, dtype=f32)
    ang = (2.0 * math.pi / L) * t[:, None] * fr[None, :]
    feat = jnp.concatenate([tn[:, None], jnp.cos(ang), -jnp.sin(ang)], axis=-1)
    a = jnp.sin(freq[0].astype(f32) * (feat @ w1.astype(f32) + b1.astype(f32)))
    a = jnp.sin(freq[1].astype(f32) * (a @ w2.astype(f32) + b2.astype(f32)))
    filt = (a @ w3.astype(f32)).reshape(L, HY_ORDER, 2, D_HY)
    deltas = jnp.abs(jnp.linspace(math.log(HY_TARGET) / HY_SLOW, math.log(HY_TARGET) / HY_FAST, D_HY, dtype=f32))
    filt = filt * jnp.exp(-tn[:, None] * deltas)[:, None, None, :]
    fwd, bwd = filt[:, :, 0], filt[:, :, 1]
    k2 = jnp.concatenate([fwd, jnp.zeros((1, HY_ORDER, D_HY), f32), jnp.flip(bwd[1:], axis=0)], axis=0)
    k2 = k2 * lax.rsqrt(jnp.sum(k2 * k2, axis=0, keepdims=True) + EPS)
    return jnp.fft.rfft(k2, axis=0)


def _fftconv(z, kf, bias):
    L = z.shape[1]
    y = jnp.fft.irfft(jnp.fft.rfft(z, n=2 * L, axis=1) * kf, n=2 * L, axis=1)[:, :L]
    return y + z * bias


def _hyena(vh, x1h, x2h, conv_hy, hf_w1, hf_b1, hf_w2, hf_b2, hf_w3, hf_freq, hy_bias):
    u = _dwconv3(jnp.concatenate([vh, x1h, x2h], axis=-1), conv_hy).astype(jnp.float32)
    v, x1, x2 = jnp.split(u, 3, axis=-1)
    kf = _hyena_filters(u.shape[1], hf_w1, hf_b1, hf_w2, hf_b2, hf_w3, hf_freq)
    bias = hy_bias.astype(jnp.float32)
    z = x1 * _fftconv(v, kf[:, 0], bias[0])
    return x2 * _fftconv(z, kf[:, 1], bias[1])


def _layer(x, ctx, c, c_ctx, w_ada, b_ada, g_pre, g_post, w_in, b_if, conv_ml, conv_hy, rpb,
           hf_w1, hf_b1, hf_w2, hf_b2, hf_w3, hf_freq, hy_bias, w_out, update_ctx):
    rows = x.shape[1] // GRID_W
    mod = jax.nn.silu(c) @ w_ada + b_ada
    mod_c = jax.nn.silu(c_ctx) @ w_ada + b_ada
    shift, scale, gate = jnp.split(mod, 3, axis=-1)
    shift_c, scale_c, gate_c = jnp.split(mod_c, 3, axis=-1)
    h = _rmsnorm(x, g_pre) * (1.0 + scale[:, None]) + shift[:, None]
    hc = _rmsnorm(ctx, g_pre) * (1.0 + scale_c) + shift_c
    qa, ka, va, za, qm, km, vm, om, zm, gm, vh, x1h, x2h, zh = _split_cols(h @ w_in)
    qac, kac, vac, zac, qmc, kmc, vmc, omc, zmc, gmc, vhc, x1hc, x2hc, zhc = _split_cols(hc @ w_in)

    kac_h, vac_h = _heads(kac, H_NA), _heads(vac, H_NA)
    o_na = _neighbourhood_attention(_heads(qa, H_NA), _heads(ka, H_NA), _heads(va, H_NA), kac_h, vac_h, rpb, rows)

    ql, kl, vl, gl = _mlstm_inputs(qm, km, vm, gm, conv_ml, b_if, True)
    qc, kc, vc, gc = _mlstm_inputs(qmc, kmc, vmc, gmc, conv_ml, b_if, False)
    h_lat, h_ctx = _mlstm_bidirectional(qc, kc, vc, gc, ql, kl, vl, gl, update_ctx)
    o_ml = jax.nn.sigmoid(om) * _merge_heads(h_lat).astype(om.dtype)

    o_hy = _hyena(vh, x1h, x2h, conv_hy, hf_w1, hf_b1, hf_w2, hf_b2, hf_w3, hf_freq, hy_bias).astype(zh.dtype)

    y = jnp.concatenate([o_na * jax.nn.silu(za), o_ml * jax.nn.silu(zm), o_hy * jax.nn.silu(zh)], axis=-1) @ w_out
    x = x + gate[:, None] * _rmsnorm(y, g_post)

    if update_ctx:
        o_na_c = _ctx_attention(_heads(qac, H_NA), kac_h, vac_h)
        o_ml_c = jax.nn.sigmoid(omc) * _merge_heads(h_ctx).astype(omc.dtype)
        o_hy_c = _hyena(vhc, x1hc, x2hc, conv_hy, hf_w1, hf_b1, hf_w2, hf_b2, hf_w3, hf_freq, hy_bias).astype(zhc.dtype)
        yc = jnp.concatenate([o_na_c * jax.nn.silu(zac), o_ml_c * jax.nn.silu(zmc), o_hy_c * jax.nn.silu(zhc)], axis=-1) @ w_out
        ctx = ctx + gate_c * _rmsnorm(yc, g_post)
    return x, ctx


def setup_inputs(seed: int = 0) -> dict:
    key = jax.random.key(seed)
    ks = jax.random.split(key, 24)
    f32 = jnp.float32
    nrm = lambda k, shape, s: s * jax.random.normal(k, shape, f32)
    x = nrm(ks[0], (BATCH, SEQ, D_MODEL), 1.0)
    c = nrm(ks[1], (BATCH, D_MODEL), 1.0)
    ctx = nrm(ks[2], (BATCH, CTX_LEN, D_MODEL), 1.0)
    c_ctx = nrm(ks[3], (D_MODEL,), 0.5)
    w_ada = nrm(ks[4], (DEPTH, D_MODEL, 3 * D_MODEL), 0.5 * D_MODEL ** -0.5)
    b_ada = nrm(ks[5], (DEPTH, 3 * D_MODEL), 0.02)
    g_pre = 1.0 + nrm(ks[6], (DEPTH, D_MODEL), 0.02)
    g_post = 1.0 + nrm(ks[7], (DEPTH, D_MODEL), 0.02)
    w_in = nrm(ks[8], (DEPTH, D_MODEL, N_IN), D_MODEL ** -0.5)
    f_bias = jnp.linspace(3.0, 6.0, H_ML, dtype=f32) + nrm(ks[9], (DEPTH, 2, H_ML), 0.1)
    i_bias = nrm(ks[10], (DEPTH, 2, H_ML), 0.1)
    b_if = jnp.stack([i_bias, f_bias], axis=2)
    conv_ml = nrm(ks[11], (DEPTH, 3, 2 * D_ML), 0.5)
    conv_hy = nrm(ks[12], (DEPTH, 3, 3 * D_HY), 0.5)
    rpb = nrm(ks[13], (DEPTH, H_NA, 2 * NA_ROWS - 1, 2 * NA_COLS - 1), 0.1)
    hf_w1 = nrm(ks[14], (DEPTH, HY_EMB, HY_FFN), HY_EMB ** -0.5)
    hf_b1 = nrm(ks[15], (DEPTH, HY_FFN), 0.1)
    hf_w2 = nrm(ks[16], (DEPTH, HY_FFN, HY_FFN), HY_FFN ** -0.5)
    hf_b2 = nrm(ks[17], (DEPTH, HY_FFN), 0.1)
    hf_w3 = nrm(ks[18], (DEPTH, HY_FFN, HY_ORDER * 2 * D_HY), HY_FFN ** -0.5)
    hf_freq = 1.0 + nrm(ks[19], (DEPTH, 2, HY_FFN), 0.1)
    hy_bias = nrm(ks[20], (DEPTH, HY_ORDER, D_HY), 0.5)
    w_out = nrm(ks[21], (DEPTH, D_MIX, D_MODEL), D_MIX ** -0.5)
    return {'x': x, 'c': c, 'ctx': ctx, 'c_ctx': c_ctx, 'w_ada': w_ada, 'b_ada': b_ada,
            'g_pre': g_pre, 'g_post': g_post, 'w_in': w_in, 'b_if': b_if, 'conv_ml': conv_ml,
            'conv_hy': conv_hy, 'rpb': rpb, 'hf_w1': hf_w1, 'hf_b1': hf_b1, 'hf_w2': hf_w2,
            'hf_b2': hf_b2, 'hf_w3': hf_w3, 'hf_freq': hf_freq, 'hy_bias': hy_bias, 'w_out': w_out}


def reference(x, c, ctx, c_ctx, w_ada, b_ada, g_pre, g_post, w_in, b_if, conv_ml, conv_hy, rpb,
              hf_w1, hf_b1, hf_w2, hf_b2, hf_w3, hf_freq, hy_bias, w_out):
    for l in range(DEPTH):
        x, ctx = _layer(x, ctx, c, c_ctx, w_ada[l], b_ada[l], g_pre[l], g_post[l], w_in[l], b_if[l],
                        conv_ml[l], conv_hy[l], rpb[l], hf_w1[l], hf_b1[l], hf_w2[l], hf_b2[l], hf_w3[l],
                        hf_freq[l], hy_bias[l], w_out[l], l < DEPTH - 1)
    return x
```

```python
import functools
import math

import numpy as np
import jax
import jax.numpy as jnp
from jax import lax
from jax.experimental import pallas as pl
from jax.experimental.pallas import tpu as pltpu

F32 = jnp.float32
BF16 = jnp.bfloat16

D_MODEL = 1024
GRID_W = 64
HEAD_DIM = 64
D_NA = 512
D_ML = 256
D_HY = 256
H_NA = 8
H_ML = 4
NA_ROWS = 8
NA_COLS = 16
HY_EMB = 33
HY_FFN = 64
HY_TARGET = 1e-2
HY_FAST = 0.3
HY_SLOW = 1.5
ROPE_BASE = 10000.0
EPS = 1e-6

TT = 256
BAND_ROWS = TT // GRID_W
FFT_N2 = 128
NEG = -1e30
VMEM_LIMIT = 56 * 1024 * 1024

PF_ZA = 0
PF_QK = 1
PF_VM, PF_OM, PF_ZM, PF_ZH, PF_GM = 4, 5, 6, 7, 8
PF_HY = 3
PF_COLS = 3072


def _dot(a, b):
    return jnp.dot(a, b, preferred_element_type=F32)


def _dot_nt(a, b):
    return lax.dot_general(a, b, (((1,), (1,)), ((), ())), preferred_element_type=F32)


def _dot_tn(a, b):
    return lax.dot_general(a, b, (((0,), (0,)), ((), ())), preferred_element_type=F32)


def _split3(x):
    h = x.astype(BF16)
    r = x - h.astype(F32)
    m = r.astype(BF16)
    l = (r - m.astype(F32)).astype(BF16)
    return h, m, l


def _dot_f32(a, b):
    ah, am, al = _split3(a)
    bh, bm, bl = _split3(b)
    return (_dot(ah, bh) + (_dot(ah, bm) + _dot(am, bh))
            + (_dot(ah, bl) + _dot(al, bh) + _dot(am, bm)))


def _dot_exact_lhs(a_bf16, b):
    bh, bm, bl = _split3(b)
    return _dot(a_bf16, bh) + _dot(a_bf16, bm) + _dot(a_bf16, bl)


def _silu(x):
    return x * jax.nn.sigmoid(x)


def _params(sem, vmem=VMEM_LIMIT):
    return pltpu.CompilerParams(dimension_semantics=sem, vmem_limit_bytes=vmem)


def _mod_kernel(c_ref, w_ref, b_ref, o_ref):
    o_ref[...] = _dot_f32(_silu(c_ref[...]), w_ref[...]) + b_ref[...]


def _modulation(cc, w_ada, b_ada):
    n = w_ada.shape[1]
    bn = 512
    return pl.pallas_call(
        _mod_kernel, grid=(n // bn,),
        in_specs=[pl.BlockSpec(cc.shape, lambda j: (0, 0)),
                  pl.BlockSpec((w_ada.shape[0], bn), lambda j: (0, j)),
                  pl.BlockSpec((1, bn), lambda j: (0, j))],
        out_specs=pl.BlockSpec((cc.shape[0], bn), lambda j: (0, j)),
        out_shape=jax.ShapeDtypeStruct((cc.shape[0], n), F32),
        compiler_params=_params(("arbitrary",)), name="adaln_mod",
    )(cc, w_ada, b_ada.reshape(1, n))


def _proj_kernel(x_ref, mod_ref, g_ref, wna_ref, wpf_ref, ona_ref, opf_ref):
    x = x_ref[0]
    ms = jnp.mean(x * x, axis=-1, keepdims=True)
    y = x * lax.rsqrt(ms + EPS) * g_ref[...]
    mod = mod_ref[0]
    h = (y * (1.0 + mod[1:2]) + mod[0:1]).astype(BF16)
    ona_ref[0] = _dot(h, wna_ref[...]).astype(BF16)
    opf_ref[0] = _dot(h, wpf_ref[...])


def _projection(xc, mod3, g_pre, wna, wpf, nt):
    nb, ltot, d = xc.shape
    ctx_row = nb

    def mod_idx(b, t):
        return (jnp.where(t == nt, ctx_row, b), 0, 0)

    return pl.pallas_call(
        _proj_kernel, grid=(nb, nt + 1),
        in_specs=[pl.BlockSpec((1, TT, d), lambda b, t: (b, t, 0)),
                  pl.BlockSpec((1, 3, d), mod_idx),
                  pl.BlockSpec((1, d), lambda b, t: (0, 0)),
                  pl.BlockSpec(wna.shape, lambda b, t: (0, 0), pipeline_mode=pl.Buffered(1)),
                  pl.BlockSpec(wpf.shape, lambda b, t: (0, 0), pipeline_mode=pl.Buffered(1))],
        out_specs=[pl.BlockSpec((1, TT, wna.shape[1]), lambda b, t: (b, t, 0)),
                   pl.BlockSpec((1, TT, wpf.shape[1]), lambda b, t: (b, t, 0))],
        out_shape=[jax.ShapeDtypeStruct((nb, ltot, wna.shape[1]), BF16),
                   jax.ShapeDtypeStruct((nb, ltot, wpf.shape[1]), F32)],
        compiler_params=_params(("arbitrary", "arbitrary")), name="norm_in_proj",
    )(xc, mod3, g_pre.reshape(1, d), wna, wpf)


def _pre_kernel(nt, qk_ref, qkp_ref, qkn_ref, hy_ref, hyp_ref, hyn_ref, gm_ref, cml_ref, chy_ref,
                bif_ref, cos_ref, sin_ref, tril_ref, triu_ref, oqk_ref, ohv_ref, og_ref):
    t = pl.program_id(0)
    pm = jnp.where(jnp.logical_and(t != 0, t != nt), 1.0, 0.0)
    nm = jnp.where(jnp.logical_and(t != nt - 1, t != nt), 1.0, 0.0)
    row = lax.broadcasted_iota(jnp.int32, (TT, 1), 0)

    def conv3(u, up_row, un_row, w):
        prev = jnp.where(row == 0, up_row * pm, pltpu.roll(u, 1, 0))
        nxt = jnp.where(row == TT - 1, un_row * nm, pltpu.roll(u, TT - 1, 0))
        return prev * w[0:1] + u * w[1:2] + nxt * w[2:3]

    ohv_ref[0] = conv3(hy_ref[0], hyp_ref[0][7:8], hyn_ref[0][0:1], chy_ref[...])

    x = _silu(conv3(qk_ref[0], qkp_ref[0][7:8], qkn_ref[0][0:1], cml_ref[...]))
    lane = lax.broadcasted_iota(jnp.int32, (1, 2 * D_ML), 1)
    first = (lane % 32) < 16
    partner = jnp.where(first, pltpu.roll(x, 2 * D_ML - 16, 1), pltpu.roll(x, 16, 1))
    oqk_ref[0] = (x * cos_ref[...] + partner * sin_ref[...]).astype(BF16)

    g = gm_ref[0][:, 0:128] + bif_ref[...]
    gl = lax.broadcasted_iota(jnp.int32, (1, 128), 1)
    is_f = jnp.logical_and((gl % 8) >= 4, gl < 16)
    lf = jnp.where(is_f, jnp.minimum(g, 0.0) - jnp.log1p(jnp.exp(-jnp.abs(g))), 0.0)
    bcum = jnp.where(gl < 8, _dot_exact_lhs(tril_ref[...], lf), _dot_exact_lhs(triu_ref[...], lf))
    bi = pltpu.roll(bcum, 128 - H_ML, 1)
    a = g - bi
    keep = gl < H_ML
    og_ref[0, 0] = jnp.concatenate([jnp.where(keep, a, 0.0), jnp.where(keep, bi, 0.0)], axis=1)
    og_ref[0, 1] = jnp.concatenate([jnp.where(keep, pltpu.roll(a, 120, 1), 0.0),
                                    jnp.where(keep, pltpu.roll(bi, 120, 1), 0.0)], axis=1)


def _pre(pf, conv_ml, conv_hy, bif_row, cos_t, sin_t, tril, triu, nt):
    nb, ltot, _ = pf.shape
    last8 = ltot // 8 - 1
    r8 = TT // 8

    def cur(col):
        return lambda t, b: (b, t, col)

    def prv(col):
        return lambda t, b: (b, jnp.maximum(t * r8 - 1, 0), col)

    def nxt(col):
        return lambda t, b: (b, jnp.minimum((t + 1) * r8, last8), col)

    const = lambda t, b: (0, 0)
    return pl.pallas_call(
        functools.partial(_pre_kernel, nt), grid=(nt + 1, nb),
        in_specs=[pl.BlockSpec((1, TT, 512), cur(PF_QK)), pl.BlockSpec((1, 8, 512), prv(PF_QK)),
                  pl.BlockSpec((1, 8, 512), nxt(PF_QK)),
                  pl.BlockSpec((1, TT, 768), cur(PF_HY)), pl.BlockSpec((1, 8, 768), prv(PF_HY)),
                  pl.BlockSpec((1, 8, 768), nxt(PF_HY)),
                  pl.BlockSpec((1, TT, 256), cur(PF_GM)),
                  pl.BlockSpec((3, 512), const), pl.BlockSpec((3, 768), const), pl.BlockSpec((1, 128), const),
                  pl.BlockSpec((TT, 512), lambda t, b: (t, 0)), pl.BlockSpec((TT, 512), lambda t, b: (t, 0)),
                  pl.BlockSpec((TT, TT), const), pl.BlockSpec((TT, TT), const)],
        out_specs=[pl.BlockSpec((1, TT, 512), lambda t, b: (b, t, 0)),
                   pl.BlockSpec((1, TT, 768), lambda t, b: (b, t, 0)),
                   pl.BlockSpec((1, 2, TT, 256), lambda t, b: (b, 0, t, 0))],
        out_shape=[jax.ShapeDtypeStruct((nb, ltot, 512), BF16),
                   jax.ShapeDtypeStruct((nb, ltot, 768), F32),
                   jax.ShapeDtypeStruct((nb, 2, ltot, 256), F32)],
        compiler_params=_params(("arbitrary", "arbitrary")), name="conv_rope_gates",
    )(pf, pf, pf, pf, pf, pf, pf, conv_ml, conv_hy, bif_row, cos_t, sin_t, tril, triu)


def _na_kernel(q_ref, k0_ref, k1_ref, k2_ref, v0_ref, v1_ref, v2_ref, kc_ref, vc_ref, za_ref, tab_ref, o_ref):
    lane = lax.broadcasted_iota(jnp.int32, (1, 128), 1)
    for hp in range(H_NA // 2):
        sl = slice(hp * 128, (hp + 1) * 128)
        q2 = q_ref[0, :, sl]
        kl = jnp.concatenate([k0_ref[0, :, sl], k1_ref[0, :, sl], k2_ref[0, :, sl]], axis=0)
        vl = jnp.concatenate([v0_ref[0, :, sl], v1_ref[0, :, sl], v2_ref[0, :, sl]], axis=0)
        kc = kc_ref[0, :, sl]
        vc = vc_ref[0, :, sl]
        acc = None
        for e in range(2):
            hm = (lane >= HEAD_DIM) if e else (lane < HEAD_DIM)
            qm = jnp.where(hm, q2, jnp.zeros_like(q2))
            s_lat = _dot_nt(qm, kl) + tab_ref[0, 2 * hp + e]
            s_ctx = _dot_nt(qm, kc)
            m = jnp.maximum(jnp.max(s_lat, axis=-1, keepdims=True), jnp.max(s_ctx, axis=-1, keepdims=True))
            p_lat = jnp.exp(s_lat - m)
            p_ctx = jnp.exp(s_ctx - m)
            den = jnp.sum(p_lat, axis=-1, keepdims=True) + jnp.sum(p_ctx, axis=-1, keepdims=True)
            o = (_dot(p_lat.astype(BF16), vl) + _dot(p_ctx.astype(BF16), vc)) / den
            acc = o if e == 0 else jnp.where(hm, o, acc)
        o_ref[0, :, sl] = (acc * _silu(za_ref[0, :, sl])).astype(BF16)


def _attention(ona, pf, table, nt, with_ctx_queries):
    nb, ltot, _ = ona.shape
    nj = nt + 1 if with_ctx_queries else nt

    def kv(col, dj):
        return lambda j, b: (b, jnp.clip(j + dj, 0, nt - 1), col)

    def pat(j, b):
        p = jnp.where(j == 0, 0, jnp.where(j == nt - 1, 2, jnp.where(j == nt, 3, 1)))
        return (p, 0, 0, 0)

    return pl.pallas_call(
        _na_kernel, grid=(nj, nb),
        in_specs=[pl.BlockSpec((1, TT, 512), lambda j, b: (b, j, 0)),
                  pl.BlockSpec((1, TT, 512), kv(1, -1)), pl.BlockSpec((1, TT, 512), kv(1, 0)),
                  pl.BlockSpec((1, TT, 512), kv(1, 1)),
                  pl.BlockSpec((1, TT, 512), kv(2, -1)), pl.BlockSpec((1, TT, 512), kv(2, 0)),
                  pl.BlockSpec((1, TT, 512), kv(2, 1)),
                  pl.BlockSpec((1, TT, 512), lambda j, b: (b, nt, 1)),
                  pl.BlockSpec((1, TT, 512), lambda j, b: (b, nt, 2)),
                  pl.BlockSpec((1, TT, 512), lambda j, b: (b, j, PF_ZA)),
                  pl.BlockSpec((1, H_NA, TT, 3 * TT), pat)],
        out_specs=pl.BlockSpec((1, TT, 512), lambda j, b: (b, j, 0)),
        out_shape=jax.ShapeDtypeStruct((nb, ltot, 512), BF16),
        compiler_params=_params(("arbitrary", "arbitrary")), name="nbr_attention",
    )(ona, ona, ona, ona, ona, ona, ona, ona, ona, pf, table)


def _na_bias_table(rpb):
    a = np.arange(TT) // GRID_W
    qc = np.arange(TT) % GRID_W
    kidx = np.arange(3 * TT)
    kr = (kidx // TT - 1) * BAND_ROWS + (kidx % TT) // GRID_W
    kcol = kidx % GRID_W
    qstart = np.clip(qc - NA_COLS // 2, 0, GRID_W - NA_COLS)
    col_ok = (kcol[None, :] >= qstart[:, None]) & (kcol[None, :] < qstart[:, None] + NA_COLS)
    coff = np.clip(kcol[None, :] - qc[:, None], -(NA_COLS - 1), NA_COLS - 1) + NA_COLS - 1
    roff = np.clip(kr[None, :] - a[:, None] + NA_ROWS - 1, 0, 2 * NA_ROWS - 2)
    half = NA_ROWS // 2
    row_ok = [
        (kr[None, :] >= 0) & (kr[None, :] < NA_ROWS) & (a[:, None] >= 0),
        (kr[None, :] >= a[:, None] - half) & (kr[None, :] < a[:, None] + half),
        (kr[None, :] >= BAND_ROWS - NA_ROWS) & (kr[None, :] < BAND_ROWS) & (a[:, None] >= 0),
    ]
    bias = rpb[:, roff, coff].astype(F32)
    tabs = [jnp.where(jnp.asarray(r & col_ok)[None], bias, NEG) for r in row_ok]
    tabs.append(jnp.full_like(bias, NEG))
    return jnp.stack(tabs, axis=0)


def _scan_kernel(q_ref, k_ref, v_ref, g_ref, e_ref, bm_ref, o_ref, w_ref, m_ref):
    s = pl.program_id(2)
    sgn = 1 - 2 * pl.program_id(1)

    @pl.when(s == 0)
    def _():
        w_ref[...] = jnp.zeros_like(w_ref)
        m_ref[...] = jnp.zeros_like(m_ref)

    a = g_ref[0, 0, :, 0:128]
    bb = g_ref[0, 0, :, 128:256]
    m_prev = m_ref[...]
    mc = jnp.maximum(m_prev, jnp.max(a, axis=0, keepdims=True))
    small = jnp.concatenate([jnp.exp(a - mc), jnp.exp(-bb - mc),
                             jnp.broadcast_to(jnp.exp(m_prev - mc), (8, 128))], axis=0)
    sh = small.astype(BF16)
    sl = (small - sh.astype(F32)).astype(BF16)
    ex = _dot(sh, e_ref[...]) + _dot(sl, e_ref[...])
    w_e = ex[0:TT]
    eb_e = ex[TT:2 * TT]
    beta = ex[2 * TT:2 * TT + 1]
    beta2 = jnp.concatenate([beta, beta], axis=1)

    q = q_ref[0]
    kw = (k_ref[0].astype(F32) * w_e).astype(BF16)
    vb = v_ref[0].astype(BF16)
    vext = jnp.concatenate([vb, jnp.ones_like(vb)], axis=1)
    lane_h = lax.broadcasted_iota(jnp.int32, (1, D_ML), 1) // HEAD_DIM
    lane_h2 = jnp.concatenate([lane_h, lane_h], axis=1)
    ri = lax.broadcasted_iota(jnp.int32, (TT, TT), 0)
    ci = lax.broadcasted_iota(jnp.int32, (TT, TT), 1)
    tri = ((ci - ri) * sgn) <= 0

    intra = None
    for h in range(H_ML):
        qh = jnp.where(lane_h == h, q, jnp.zeros_like(q))
        sc = jnp.where(tri, _dot_nt(qh, kw), 0.0).astype(BF16)
        vh = jnp.where(lane_h2 == h, vext, jnp.zeros_like(vext))
        part = _dot(sc, vh)
        intra = part if intra is None else intra + part
    inter = _dot(q, w_ref[...].astype(BF16))
    tot = beta2 * inter + intra
    num = tot[:, 0:D_ML]
    den = tot[:, D_ML:2 * D_ML]
    o_ref[0, 0] = num / jnp.maximum(jnp.abs(den), eb_e)

    w_ref[...] = beta2 * w_ref[...] + bm_ref[...] * _dot_tn(kw, vext)
    m_ref[...] = jnp.min(bb, axis=0, keepdims=True) + mc


def _mlstm_scan(qk, pf, gates, emat, bmask, nt):
    nb, ltot, _ = qk.shape

    def chunk(s, d):
        return jnp.where(s == 0, nt, jnp.where(d == 0, s - 1, nt - s))

    const = lambda b, d, s: (0, 0)
    return pl.pallas_call(
        _scan_kernel, grid=(nb, 2, nt + 1),
        in_specs=[pl.BlockSpec((1, TT, 256), lambda b, d, s: (b, chunk(s, d), 0)),
                  pl.BlockSpec((1, TT, 256), lambda b, d, s: (b, chunk(s, d), 1)),
                  pl.BlockSpec((1, TT, 256), lambda b, d, s: (b, chunk(s, d), PF_VM)),
                  pl.BlockSpec((1, 1, TT, 256), lambda b, d, s: (b, d, chunk(s, d), 0)),
                  pl.BlockSpec((128, 256), const), pl.BlockSpec((256, 512), const)],
        out_specs=pl.BlockSpec((1, 1, TT, 256), lambda b, d, s: (b, d, chunk(s, d), 0)),
        out_shape=jax.ShapeDtypeStruct((nb, 2, ltot, 256), F32),
        scratch_shapes=[pltpu.VMEM((D_ML, 2 * D_ML), F32), pltpu.VMEM((1, 128), F32)],
        compiler_params=_params(("arbitrary", "arbitrary", "arbitrary")), name="mlstm_scan",
    )(qk, qk, pf, gates, emat, bmask)


def _hyfilt_kernel(tl, feat_ref, w1_ref, b1_ref, w2_ref, b2_ref, w3_ref, fq_ref, dec_ref, o_ref, ss_ref):
    i = pl.program_id(0)
    a = jnp.sin(fq_ref[0:1] * (_dot_f32(feat_ref[...], w1_ref[...]) + b1_ref[...]))
    a = jnp.sin(fq_ref[1:2] * (_dot_f32(a, w2_ref[...]) + b2_ref[...]))
    dec = dec_ref[...]
    f = _dot_f32(a, w3_ref[...]) * jnp.concatenate([dec, dec, dec, dec], axis=1)
    o_ref[...] = f
    row = lax.broadcasted_iota(jnp.int32, (tl, 1), 0) + i * tl
    col = lax.broadcasted_iota(jnp.int32, (1, 4 * D_HY), 1)
    drop = jnp.logical_and(row == 0, (col // D_HY) % 2 == 1)
    part = jnp.sum(jnp.where(drop, 0.0, f * f), axis=0, keepdims=True)

    @pl.when(i == 0)
    def _():
        ss_ref[...] = jnp.zeros_like(ss_ref)

    ss_ref[...] += part


def _hyena_filters(length, w1, b1, w2, b2, w3, freq):
    t = np.arange(length, dtype=np.float32)
    tn = (t / np.float32(length - 1)).astype(np.float32)
    bands = (HY_EMB - 1) // 2
    fr = np.linspace(1e-4, bands - 1, bands, dtype=np.float32)
    ang = (np.float32(2.0 * math.pi / length) * t[:, None] * fr[None, :]).astype(np.float32)
    feat = np.zeros((length, 128), np.float32)
    feat[:, 0] = tn
    feat[:, 1:1 + bands] = np.cos(ang)
    feat[:, 1 + bands:1 + 2 * bands] = -np.sin(ang)
    deltas = np.abs(np.linspace(math.log(HY_TARGET) / HY_SLOW, math.log(HY_TARGET) / HY_FAST, D_HY, dtype=np.float32))
    dec = np.exp(-tn[:, None] * deltas[None, :]).astype(np.float32)

    pad2 = lambda m, r, c: jnp.zeros((r, c), F32).at[:m.shape[0], :m.shape[1]].set(m.astype(F32))
    w1p = pad2(w1, 128, 128)
    w2p = pad2(w2, 128, 128)
    w3p = pad2(w3, 128, 4 * D_HY)
    b1p = pad2(b1[None], 1, 128)
    b2p = pad2(b2[None], 1, 128)
    fqp = pad2(freq, 2, 128)
    tl = min(length, 512)
    const = lambda i: (0, 0)
    return pl.pallas_call(
        functools.partial(_hyfilt_kernel, tl), grid=(length // tl,),
        in_specs=[pl.BlockSpec((tl, 128), lambda i: (i, 0)),
                  pl.BlockSpec((128, 128), const), pl.BlockSpec((1, 128), const),
                  pl.BlockSpec((128, 128), const), pl.BlockSpec((1, 128), const),
                  pl.BlockSpec((128, 4 * D_HY), const), pl.BlockSpec((2, 128), const),
                  pl.BlockSpec((tl, D_HY), lambda i: (i, 0))],
        out_specs=[pl.BlockSpec((tl, 4 * D_HY), lambda i: (i, 0)), pl.BlockSpec((1, 4 * D_HY), const)],
        out_shape=[jax.ShapeDtypeStruct((length, 4 * D_HY), F32), jax.ShapeDtypeStruct((1, 4 * D_HY), F32)],
        compiler_params=_params(("arbitrary",)), name="hyena_filter_mlp",
    )(feat, w1p, b1p, w2p, b2p, w3p, fqp, dec)


def _filter_scale(ssq):
    s = ssq.reshape(2, 2, D_HY)
    return lax.rsqrt(s[:, 0] + s[:, 1] + EPS)


@functools.lru_cache(maxsize=None)
def _fft_tables(length):
    n = 2 * length
    n1 = n // FFT_N2
    nh = n1 // 2
    k1 = np.arange(n1)[None, :, None].astype(np.float64)
    j = np.arange(nh)[None, None, :].astype(np.float64)
    n2 = np.arange(FFT_N2)[:, None, None].astype(np.float64)
    ang = -2.0 * np.pi * k1 * (FFT_N2 * j + n2) / n
    gf = np.concatenate([np.cos(ang), np.sin(ang)], axis=1)
    gi = np.concatenate([np.cos(ang).transpose(0, 2, 1), np.sin(ang).transpose(0, 2, 1)], axis=1)
    kk = np.arange(FFT_N2)[:, None] * np.arange(FFT_N2)[None, :]
    a2 = -2.0 * np.pi * kk / FFT_N2
    f2 = np.concatenate([np.cos(a2), np.sin(a2)], axis=0)
    cv = lambda m: np.asarray(m, np.float32).astype(BF16)
    return cv(gf), cv(gi), cv(f2), n1


def _stage1(n1, z_of, gf_ref, scr_ref, cols):
    nh = n1 // 2

    def body(n2, c):
        r = _dot(gf_ref[n2], z_of(n2))
        off = pl.multiple_of(n2 * n1, n1)
        if cols == 256:
            scr_ref[0, pl.ds(off, n1), :] = r[:n1, :128] - r[n1:, 128:]
            scr_ref[1, pl.ds(off, n1), :] = r[n1:, :128] + r[:n1, 128:]
        else:
            scr_ref[0, pl.ds(off, n1), :] = r[:n1]
            scr_ref[1, pl.ds(off, n1), :] = r[n1:]
        return c

    lax.fori_loop(0, FFT_N2, body, 0)
    del nh


def _stage2(n1, k1, f2_ref, scr_ref):
    ar = scr_ref[0, pl.ds(k1, FFT_N2, stride=n1), :]
    ai = scr_ref[1, pl.ds(k1, FFT_N2, stride=n1), :]
    r = _dot(f2_ref[...], jnp.concatenate([ar, ai], axis=1).astype(BF16))
    return r[:128, :128] - r[128:, 128:], r[128:, :128] + r[:128, 128:]


def _hyconv_kernel(n1, kg, z_ref, gf_ref, f2_ref, kf_ref, gi_ref, o_ref, scr_ref):
    s = pl.program_id(2)
    nh = n1 // 2

    @pl.when(s == 0)
    def _():
        def z_of(n2):
            zr = z_ref[0, 0, pl.ds(n2, nh, stride=FFT_N2), :]
            zi = z_ref[0, 1, pl.ds(n2, nh, stride=FFT_N2), :]
            return jnp.concatenate([zr, zi], axis=1).astype(BF16)
        _stage1(n1, z_of, gf_ref, scr_ref, 256)

    def k1_body(i, c):
        k1 = s * kg + i
        xr, xi = _stage2(n1, k1, f2_ref, scr_ref)
        off = pl.multiple_of(i * FFT_N2, FFT_N2)
        kr = kf_ref[0, pl.ds(off, FFT_N2), :]
        ki = kf_ref[1, pl.ds(off, FFT_N2), :]
        yc = jnp.concatenate([xr * kr - xi * ki, xr * ki + xi * kr], axis=1).astype(BF16)
        r = _dot(f2_ref[...], yc)
        scr_ref[0, pl.ds(k1, FFT_N2, stride=n1), :] = r[:128, :128] + r[128:, 128:]
        scr_ref[1, pl.ds(k1, FFT_N2, stride=n1), :] = r[:128, 128:] - r[128:, :128]
        return c

    lax.fori_loop(0, kg, k1_body, 0)

    @pl.when(s == pl.num_programs(2) - 1)
    def _():
        def body(n2, c):
            off = pl.multiple_of(n2 * n1, n1)
            bc = jnp.concatenate([scr_ref[0, pl.ds(off, n1), :], scr_ref[1, pl.ds(off, n1), :]], axis=1)
            r = _dot(gi_ref[n2], bc.astype(BF16))
            o_ref[0, 0, pl.ds(n2, nh, stride=FFT_N2), :] = r[:nh, :128] + r[nh:, 128:]
            o_ref[0, 1, pl.ds(n2, nh, stride=FFT_N2), :] = r[:nh, 128:] - r[nh:, :128]
            return c
        lax.fori_loop(0, FFT_N2, body, 0)


def _hyena_conv(z4, col0, kf, length):
    gf, gi, f2, n1 = _fft_tables(length)
    n = 2 * length
    kg = min(n1, 16)
    npair = z4.shape[0]
    big = pl.Buffered(1)
    return pl.pallas_call(
        functools.partial(_hyconv_kernel, n1, kg), grid=(npair, 2, n1 // kg),
        in_specs=[pl.BlockSpec((1, 2, length, 128), lambda p, c, s: (p, 0, 0, col0 + c), pipeline_mode=big),
                  pl.BlockSpec(gf.shape, lambda p, c, s: (0, 0, 0), pipeline_mode=big),
                  pl.BlockSpec(f2.shape, lambda p, c, s: (0, 0)),
                  pl.BlockSpec((2, kg * FFT_N2, 128), lambda p, c, s: (0, s, c)),
                  pl.BlockSpec(gi.shape, lambda p, c, s: (0, 0, 0), pipeline_mode=big)],
        out_specs=pl.BlockSpec((1, 2, length, 128), lambda p, c, s: (p, 0, 0, c), pipeline_mode=big),
        out_shape=jax.ShapeDtypeStruct((npair, 2, length, 256), F32),
        scratch_shapes=[pltpu.VMEM((2, n, 128), F32)],
        compiler_params=_params(("arbitrary", "arbitrary", "arbitrary")), name="hyena_fftconv",
    )(z4, gf, f2, kf, gi)


def _hyfft_kernel(n1, kg, z_ref, gf_ref, f2_ref, o_ref, scr_ref):
    s = pl.program_id(2)
    nh = n1 // 2

    @pl.when(s == 0)
    def _():
        _stage1(n1, lambda n2: z_ref[0, pl.ds(n2, nh, stride=FFT_N2), :].astype(BF16), gf_ref, scr_ref, 128)

    def k1_body(i, c):
        xr, xi = _stage2(n1, s * kg + i, f2_ref, scr_ref)
        off = pl.multiple_of(i * FFT_N2, FFT_N2)
        o_ref[0, 0, pl.ds(off, FFT_N2), :] = xr
        o_ref[0, 1, pl.ds(off, FFT_N2), :] = xi
        return c

    lax.fori_loop(0, kg, k1_body, 0)


def _filter_spectrum(filt4, length):
    gf, _, f2, n1 = _fft_tables(length)
    n = 2 * length
    kg = min(n1, 16)
    nf = filt4.shape[0]
    return pl.pallas_call(
        functools.partial(_hyfft_kernel, n1, kg), grid=(nf, 2, n1 // kg),
        in_specs=[pl.BlockSpec((1, length, 128), lambda f, c, s: (f, 0, c)),
                  pl.BlockSpec(gf.shape, lambda f, c, s: (0, 0, 0), pipeline_mode=pl.Buffered(1)),
                  pl.BlockSpec(f2.shape, lambda f, c, s: (0, 0))],
        out_specs=pl.BlockSpec((1, 2, kg * FFT_N2, 128), lambda f, c, s: (f, 0, s, c)),
        out_shape=jax.ShapeDtypeStruct((nf, 2, n, 256), F32),
        scratch_shapes=[pltpu.VMEM((2, n, 128), F32)],
        compiler_params=_params(("arbitrary", "arbitrary", "arbitrary")), name="hyena_filter_fft",
    )(filt4, gf, f2)


def _latent_filter_spectra(filt, scale, length):
    f4 = filt.reshape(length, 4, D_HY).transpose(1, 0, 2)
    lag0 = (jnp.arange(length) == 0)[None, :, None]
    is_bwd = (jnp.arange(4) % 2 == 1)[:, None, None]
    f4 = jnp.where(jnp.logical_and(lag0, is_bwd), 0.0, f4)
    sp = _filter_spectrum(f4, length).reshape(2, 2, 2, 2 * length, D_HY)
    sc = (scale / (2 * length))[:, None, None, :]
    re = (sp[:, 0, 0] + sp[:, 1, 0])[:, None]
    im = (sp[:, 0, 1] - sp[:, 1, 1])[:, None]
    return jnp.concatenate([re, im], axis=1) * sc


def _hy_mid_kernel(y_ref, hv_ref, b_ref, o_ref):
    v = hv_ref[0][:, 0:D_HY]
    x1 = hv_ref[0][:, D_HY:2 * D_HY]
    o_ref[0] = x1 * (y_ref[0] + v * b_ref[...])


def _hy_mid(y1, hv, bias0, nt):
    nb = y1.shape[0]
    return pl.pallas_call(
        _hy_mid_kernel, grid=(nb, nt),
        in_specs=[pl.BlockSpec((1, TT, D_HY), lambda b, t: (b, t, 0)),
                  pl.BlockSpec((1, TT, 512), lambda b, t: (b, t, 0)),
                  pl.BlockSpec((1, D_HY), lambda b, t: (0, 0))],
        out_specs=pl.BlockSpec((1, TT, D_HY), lambda b, t: (b, t, 0)),
        out_shape=jax.ShapeDtypeStruct(y1.shape, F32),
        compiler_params=_params(("arbitrary", "arbitrary")), name="hyena_mid",
    )(y1, hv, bias0.reshape(1, D_HY))


def _hy_end_kernel(y_ref, z_ref, x2_ref, zh_ref, b_ref, o_ref):
    z = z_ref[0]
    o_ref[0] = (x2_ref[0] * (y_ref[0] + z * b_ref[...]) * _silu(zh_ref[0])).astype(BF16)


def _hy_end(y2, z1, hv, pf, bias1, nt):
    nb = y2.shape[0]
    return pl.pallas_call(
        _hy_end_kernel, grid=(nb, nt),
        in_specs=[pl.BlockSpec((1, TT, D_HY), lambda b, t: (b, t, 0)),
                  pl.BlockSpec((1, TT, D_HY), lambda b, t: (b, t, 0)),
                  pl.BlockSpec((1, TT, D_HY), lambda b, t: (b, t, 2)),
                  pl.BlockSpec((1, TT, D_HY), lambda b, t: (b, t, PF_ZH)),
                  pl.BlockSpec((1, D_HY), lambda b, t: (0, 0))],
        out_specs=pl.BlockSpec((1, TT, D_HY), lambda b, t: (b, t, 0)),
        out_shape=jax.ShapeDtypeStruct(y2.shape, BF16),
        compiler_params=_params(("arbitrary", "arbitrary")), name="hyena_end",
    )(y2, z1, hv, pf, bias1.reshape(1, D_HY))


def _hy_ctx_kernel(lc, hv_ref, zh_ref, k2_ref, b_ref, fc_ref, ff_ref, ci_ref, o_ref):
    nf = 2 * lc
    hv = hv_ref[0]

    def conv(z, o):
        kf = _dot_exact_rhs(ff_ref[...], k2_ref[o])
        zs = _dot(fc_ref[...], z.astype(BF16))
        zr, zi = zs[:nf], zs[nf:]
        kr, ki = kf[:nf], kf[nf:]
        ys = jnp.concatenate([zr * kr - zi * ki, zr * ki + zi * kr], axis=0).astype(BF16)
        return _dot(ci_ref[...], ys) * (1.0 / nf)

    v = hv[:, 0:D_HY]
    x1 = hv[:, D_HY:2 * D_HY]
    x2 = hv[:, 2 * D_HY:3 * D_HY]
    z1 = x1 * (conv(v, 0) + v * b_ref[0:1])
    p = x2 * (conv(z1, 1) + z1 * b_ref[1:2])
    o_ref[0] = (p * _silu(zh_ref[0])).astype(BF16)


def _dot_exact_rhs(a_bf16, b):
    bh, bm, bl = _split3(b)
    return _dot(a_bf16, bh) + _dot(a_bf16, bm) + _dot(a_bf16, bl)


@functools.lru_cache(maxsize=None)
def _ctx_dft_tables(lc):
    nf = 2 * lc
    k = np.arange(nf)[:, None].astype(np.float64)
    ang = -2.0 * np.pi * k * np.arange(nf)[None, :] / nf
    ff = np.concatenate([np.cos(ang), np.sin(ang)], axis=0)
    fc = ff[:, :lc]
    a2 = 2.0 * np.pi * np.arange(lc)[:, None] * np.arange(nf)[None, :] / nf
    ci = np.concatenate([np.cos(a2), -np.sin(a2)], axis=1)
    cv = lambda m: np.asarray(m, np.float32).astype(BF16)
    return cv(fc), cv(ff), cv(ci)


def _hy_ctx(hv, pf, k2c, hy_bias, nt, lc):
    nb = hv.shape[0]
    fc, ff, ci = _ctx_dft_tables(lc)
    c2 = lambda b: (0, 0)
    return pl.pallas_call(
        functools.partial(_hy_ctx_kernel, lc), grid=(nb,),
        in_specs=[pl.BlockSpec((1, lc, 768), lambda b: (b, nt, 0)),
                  pl.BlockSpec((1, lc, D_HY), lambda b: (b, nt, PF_ZH)),
                  pl.BlockSpec(k2c.shape, lambda b: (0, 0, 0)),
                  pl.BlockSpec((2, D_HY), c2),
                  pl.BlockSpec(fc.shape, c2), pl.BlockSpec(ff.shape, c2), pl.BlockSpec(ci.shape, c2)],
        out_specs=pl.BlockSpec((1, lc, D_HY), lambda b: (b, 0, 0)),
        out_shape=jax.ShapeDtypeStruct((nb, lc, D_HY), BF16),
        compiler_params=_params(("arbitrary",)), name="hyena_ctx",
    )(hv, pf, k2c, hy_bias, fc, ff, ci)


def _out_kernel(na_ref, h0_ref, h1_ref, om_ref, zm_ref, hy_ref, w_ref, x_ref, mod_ref, g_ref, o_ref):
    ml = jax.nn.sigmoid(om_ref[0]) * (h0_ref[0, 0] + h1_ref[0, 0]) * _silu(zm_ref[0])
    y = (_dot(na_ref[0], w_ref[0:D_NA]) + _dot(ml.astype(BF16), w_ref[D_NA:D_NA + D_ML])
         + _dot(hy_ref[0], w_ref[D_NA + D_ML:D_MODEL]))
    ms = jnp.mean(y * y, axis=-1, keepdims=True)
    r = y * lax.rsqrt(ms + EPS) * g_ref[...]
    o_ref[0] = x_ref[0] + mod_ref[0][2:3] * r


def _out_projection(na_g, hml, pf, hy_g, w_out, xc, mod3, g_post, t0, ntiles, hy_t0, mod_row, out_rows, prev=None):
    nb = xc.shape[0]
    d = D_MODEL
    tok = lambda col: (lambda b, t: (b, t + t0, col))
    in_specs = [pl.BlockSpec((1, TT, 512), tok(0)),
                pl.BlockSpec((1, 1, TT, 256), lambda b, t: (b, 0, t + t0, 0)),
                pl.BlockSpec((1, 1, TT, 256), lambda b, t: (b, 1, t + t0, 0)),
                pl.BlockSpec((1, TT, 256), tok(PF_OM)), pl.BlockSpec((1, TT, 256), tok(PF_ZM)),
                pl.BlockSpec((1, TT, 256), lambda b, t: (b, t + hy_t0, 0)),
                pl.BlockSpec((d, d), lambda b, t: (0, 0), pipeline_mode=pl.Buffered(1)),
                pl.BlockSpec((1, TT, d), tok(0)),
                pl.BlockSpec((1, 3, d), lambda b, t: (mod_row(b), 0, 0)),
                pl.BlockSpec((1, d), lambda b, t: (0, 0))]
    args = [na_g, hml, hml, pf, pf, hy_g, w_out, xc, mod3, g_post.reshape(1, d)]
    aliases = {}
    if prev is not None:
        in_specs.append(pl.BlockSpec(memory_space=pl.ANY))
        args.append(prev)
        aliases = {len(args) - 1: 0}
        kern = lambda *r: _out_kernel(*r[:10], r[11])
    else:
        kern = _out_kernel
    return pl.pallas_call(
        kern, grid=(nb, ntiles), in_specs=in_specs,
        out_specs=pl.BlockSpec((1, TT, d), tok(0)),
        out_shape=jax.ShapeDtypeStruct((nb, out_rows, d), F32),
        input_output_aliases=aliases,
        compiler_params=_params(("arbitrary", "arbitrary")), name="out_proj_norm_residual",
    )(*args)


@functools.lru_cache(maxsize=None)
def _rope_tables(length, lc):
    n = HEAD_DIM // 4
    t = np.arange(length)
    inv = (ROPE_BASE ** (-np.arange(n, dtype=np.float32) / n)).astype(np.float32)
    pos = np.stack([t // GRID_W, t % GRID_W], axis=-1).astype(np.float32)
    ang = (pos[:, :, None] * inv).astype(np.float32)
    cos_h = np.concatenate([np.cos(ang[:, 0]), np.cos(ang[:, 0]), np.cos(ang[:, 1]), np.cos(ang[:, 1])], axis=-1)
    sin_h = np.concatenate([-np.sin(ang[:, 0]), np.sin(ang[:, 0]), -np.sin(ang[:, 1]), np.sin(ang[:, 1])], axis=-1)
    cos_t = np.concatenate([np.tile(cos_h, (1, 2 * H_ML)), np.ones((lc, 512), np.float32)], axis=0)
    sin_t = np.concatenate([np.tile(sin_h, (1, 2 * H_ML)), np.zeros((lc, 512), np.float32)], axis=0)
    kscale = np.concatenate([np.ones(256, np.float32), np.full(256, HEAD_DIM ** -0.5, np.float32)])
    return (cos_t * kscale).astype(np.float32), (sin_t * kscale).astype(np.float32)


@functools.lru_cache(maxsize=None)
def _scan_constants():
    tril = np.tril(np.ones((TT, TT), np.float32))
    emat = np.zeros((128, D_ML), np.float32)
    for h in range(H_ML):
        emat[h, h * HEAD_DIM:(h + 1) * HEAD_DIM] = 1.0
    rh = np.arange(D_ML)[:, None] // HEAD_DIM
    chd = (np.arange(2 * D_ML)[None, :] % D_ML) // HEAD_DIM
    bmask = (rh == chd).astype(np.float32)
    return tril.astype(BF16), np.ascontiguousarray(tril.T).astype(BF16), emat.astype(BF16), bmask


def _layer(xc, mod3, nb, length, lc, g_pre, g_post, w_in, b_if, conv_ml, conv_hy, rpb,
           hf_w1, hf_b1, hf_w2, hf_b2, hf_w3, hf_freq, hy_bias, w_out, update_ctx):
    nt = length // TT
    ltot = length + lc
    tril, triu, emat, bmask = _scan_constants()
    cos_t, sin_t = _rope_tables(length, lc)

    c = lambda a, b: w_in[:, a:b]
    wna = jnp.concatenate([c(0, 512) * (HEAD_DIM ** -0.5), c(512, 1536)], axis=1).astype(BF16)
    wpf = jnp.concatenate([c(1536, 2048), c(2048, 2560), c(2560, 2816), c(2816, 3072), c(3072, 3328),
                           c(4112, 4368), c(3328, 3344), jnp.zeros((D_MODEL, 240), F32), c(3344, 4112)],
                          axis=1).astype(BF16)
    ona, pf = _projection(xc, mod3, g_pre, wna, wpf, nt)

    na_g = _attention(ona, pf, _na_bias_table(rpb), nt, update_ctx)

    bif_row = jnp.zeros((1, 128), F32).at[0, :4 * H_ML].set(b_if.reshape(-1))
    qk, hv, gates = _pre(pf, conv_ml, conv_hy, bif_row, cos_t, sin_t, tril, triu, nt)
    hml = _mlstm_scan(qk, pf, gates, emat, bmask, nt)

    filt, ssq = _hyena_filters(length, hf_w1, hf_b1, hf_w2, hf_b2, hf_w3, hf_freq)
    kf = _latent_filter_spectra(filt, _filter_scale(ssq), length)
    hv4 = hv.reshape(nb // 2, 2, ltot, 768)
    y1 = _hyena_conv(hv4, 0, kf[0], length).reshape(nb, length, D_HY)
    z1 = _hy_mid(y1, hv, hy_bias[0], nt)
    y2 = _hyena_conv(z1.reshape(nb // 2, 2, length, D_HY), 0, kf[1], length).reshape(nb, length, D_HY)
    hy_g = _hy_end(y2, z1, hv, pf, hy_bias[1], nt)

    if not update_ctx:
        return _out_projection(na_g, hml, pf, hy_g, w_out, xc, mod3, g_post, 0, nt, 0, lambda b: b, length)

    xn = _out_projection(na_g, hml, pf, hy_g, w_out, xc, mod3, g_post, 0, nt, 0, lambda b: b, ltot)
    filt_c, ssq_c = _hyena_filters(lc, hf_w1, hf_b1, hf_w2, hf_b2, hf_w3, hf_freq)
    fc4 = (filt_c.reshape(lc, 2, 2, D_HY) * _filter_scale(ssq_c)[None, :, None, :])
    k2c = jnp.concatenate([fc4[:, :, 0], jnp.zeros((1, 2, D_HY), F32), jnp.flip(fc4[1:, :, 1], axis=0)], axis=0)
    hyc_g = _hy_ctx(hv, pf, k2c.transpose(1, 0, 2), hy_bias, nt, lc)
    return _out_projection(na_g, hml, pf, hyc_g, w_out, xc, mod3, g_post, nt, 1, 0, lambda b: nb, ltot, prev=xn)


def kernel(x, c, ctx, c_ctx, w_ada, b_ada, g_pre, g_post, w_in, b_if, conv_ml, conv_hy, rpb, hf_w1, hf_b1,
           hf_w2, hf_b2, hf_w3, hf_freq, hy_bias, w_out):
    nb, length, d = x.shape
    lc = ctx.shape[1]
    depth = w_in.shape[0]
    assert lc == TT and length % (2 * TT) == 0 and nb % 2 == 0 and d == D_MODEL
    xc = jnp.concatenate([x, ctx], axis=1)
    cc = jnp.zeros((16, d), F32).at[:nb].set(c).at[nb].set(c_ctx)
    for l in range(depth):
        mod3 = _modulation(cc, w_ada[l], b_ada[l]).reshape(16, 3, d)
        xc = _layer(xc, mod3, nb, length, lc, g_pre[l], g_post[l], w_in[l], b_if[l], conv_ml[l], conv_hy[l],
                    rpb[l], hf_w1[l], hf_b1[l], hf_w2[l], hf_b2[l], hf_w3[l], hf_freq[l], hy_bias[l],
                    w_out[l].astype(BF16), l < depth - 1)
    return xc
```

```python
import functools
import math

import numpy as np
import jax
import jax.numpy as jnp
from jax import lax
from jax.experimental import pallas as pl
from jax.experimental.pallas import tpu as pltpu

F32 = jnp.float32
BF16 = jnp.bfloat16

D_MODEL = 1024
GRID_W = 64
HEAD_DIM = 64
D_NA = 512
D_ML = 256
D_HY = 256
H_NA = 8
H_ML = 4
NA_ROWS = 8
NA_COLS = 16
HY_EMB = 33
HY_FFN = 64
HY_TARGET = 1e-2
HY_FAST = 0.3
HY_SLOW = 1.5
ROPE_BASE = 10000.0
EPS = 1e-6

TT = 256
BAND_ROWS = TT // GRID_W
FFT_N2 = 128
FFT_UNROLL = 8
NEG = -1e30
VMEM_LIMIT = 56 * 1024 * 1024

PF_ZA = 0
PF_QK = 1
PF_VM, PF_OM, PF_ZM, PF_ZH, PF_GM = 4, 5, 6, 7, 8
PF_HY = 3
PF_COLS = 3072


def _dot(a, b):
    return jnp.dot(a, b, preferred_element_type=F32)


def _dot_nt(a, b):
    return lax.dot_general(a, b, (((1,), (1,)), ((), ())), preferred_element_type=F32)


def _dot_tn(a, b):
    return lax.dot_general(a, b, (((0,), (0,)), ((), ())), preferred_element_type=F32)


def _split3(x):
    h = x.astype(BF16)
    r = x - h.astype(F32)
    m = r.astype(BF16)
    l = (r - m.astype(F32)).astype(BF16)
    return h, m, l


def _dot_f32(a, b):
    ah, am, al = _split3(a)
    bh, bm, bl = _split3(b)
    return (_dot(ah, bh) + (_dot(ah, bm) + _dot(am, bh))
            + (_dot(ah, bl) + _dot(al, bh) + _dot(am, bm)))


def _dot_exact_lhs(a_bf16, b):
    bh, bm, bl = _split3(b)
    return _dot(a_bf16, bh) + _dot(a_bf16, bm) + _dot(a_bf16, bl)


def _silu(x):
    return x * jax.nn.sigmoid(x)


def _params(sem, vmem=VMEM_LIMIT):
    return pltpu.CompilerParams(dimension_semantics=sem, vmem_limit_bytes=vmem)


def _mod_kernel(c_ref, w_ref, b_ref, o_ref):
    o_ref[...] = _dot_f32(_silu(c_ref[...]), w_ref[...]) + b_ref[...]


def _modulation(cc, w_ada, b_ada):
    n = w_ada.shape[1]
    bn = 512
    return pl.pallas_call(
        _mod_kernel, grid=(n // bn,),
        in_specs=[pl.BlockSpec(cc.shape, lambda j: (0, 0)),
                  pl.BlockSpec((w_ada.shape[0], bn), lambda j: (0, j)),
                  pl.BlockSpec((1, bn), lambda j: (0, j))],
        out_specs=pl.BlockSpec((cc.shape[0], bn), lambda j: (0, j)),
        out_shape=jax.ShapeDtypeStruct((cc.shape[0], n), F32),
        compiler_params=_params(("arbitrary",)), name="adaln_mod",
    )(cc, w_ada, b_ada.reshape(1, n))


def _proj_kernel(x_ref, mod_ref, g_ref, wna_ref, wpf_ref, ona_ref, opf_ref):
    x = x_ref[0]
    ms = jnp.mean(x * x, axis=-1, keepdims=True)
    y = x * lax.rsqrt(ms + EPS) * g_ref[...]
    mod = mod_ref[0]
    h = (y * (1.0 + mod[1:2]) + mod[0:1]).astype(BF16)
    ona_ref[0] = _dot(h, wna_ref[...]).astype(BF16)
    opf_ref[0] = _dot(h, wpf_ref[...])


def _projection(xc, mod3, g_pre, wna, wpf, nt):
    nb, ltot, d = xc.shape
    ctx_row = nb

    def mod_idx(b, t):
        return (jnp.where(t == nt, ctx_row, b), 0, 0)

    return pl.pallas_call(
        _proj_kernel, grid=(nb, nt + 1),
        in_specs=[pl.BlockSpec((1, TT, d), lambda b, t: (b, t, 0)),
                  pl.BlockSpec((1, 3, d), mod_idx),
                  pl.BlockSpec((1, d), lambda b, t: (0, 0)),
                  pl.BlockSpec(wna.shape, lambda b, t: (0, 0), pipeline_mode=pl.Buffered(1)),
                  pl.BlockSpec(wpf.shape, lambda b, t: (0, 0), pipeline_mode=pl.Buffered(1))],
        out_specs=[pl.BlockSpec((1, TT, wna.shape[1]), lambda b, t: (b, t, 0)),
                   pl.BlockSpec((1, TT, wpf.shape[1]), lambda b, t: (b, t, 0))],
        out_shape=[jax.ShapeDtypeStruct((nb, ltot, wna.shape[1]), BF16),
                   jax.ShapeDtypeStruct((nb, ltot, wpf.shape[1]), F32)],
        compiler_params=_params(("arbitrary", "arbitrary")), name="norm_in_proj",
    )(xc, mod3, g_pre.reshape(1, d), wna, wpf)


def _pre_kernel(nt, qk_ref, qkp_ref, qkn_ref, hy_ref, hyp_ref, hyn_ref, gm_ref, cml_ref, chy_ref,
                bif_ref, cos_ref, sin_ref, tril_ref, triu_ref, oqk_ref, ohv_ref, og_ref):
    t = pl.program_id(0)
    pm = jnp.where(jnp.logical_and(t != 0, t != nt), 1.0, 0.0)
    nm = jnp.where(jnp.logical_and(t != nt - 1, t != nt), 1.0, 0.0)
    row = lax.broadcasted_iota(jnp.int32, (TT, 1), 0)

    def conv3(u, up_row, un_row, w):
        prev = jnp.where(row == 0, up_row * pm, pltpu.roll(u, 1, 0))
        nxt = jnp.where(row == TT - 1, un_row * nm, pltpu.roll(u, TT - 1, 0))
        return prev * w[0:1] + u * w[1:2] + nxt * w[2:3]

    ohv_ref[0] = conv3(hy_ref[0], hyp_ref[0][7:8], hyn_ref[0][0:1], chy_ref[...])

    x = _silu(conv3(qk_ref[0], qkp_ref[0][7:8], qkn_ref[0][0:1], cml_ref[...]))
    lane = lax.broadcasted_iota(jnp.int32, (1, 2 * D_ML), 1)
    first = (lane % 32) < 16
    partner = jnp.where(first, pltpu.roll(x, 2 * D_ML - 16, 1), pltpu.roll(x, 16, 1))
    oqk_ref[0] = (x * cos_ref[...] + partner * sin_ref[...]).astype(BF16)

    g = gm_ref[0][:, 0:128] + bif_ref[...]
    gl = lax.broadcasted_iota(jnp.int32, (1, 128), 1)
    is_f = jnp.logical_and((gl % 8) >= 4, gl < 16)
    lf = jnp.where(is_f, jnp.minimum(g, 0.0) - jnp.log1p(jnp.exp(-jnp.abs(g))), 0.0)
    bcum = jnp.where(gl < 8, _dot_exact_lhs(tril_ref[...], lf), _dot_exact_lhs(triu_ref[...], lf))
    bi = pltpu.roll(bcum, 128 - H_ML, 1)
    a = g - bi
    keep = gl < H_ML
    og_ref[0, 0] = jnp.concatenate([jnp.where(keep, a, 0.0), jnp.where(keep, bi, 0.0)], axis=1)
    og_ref[0, 1] = jnp.concatenate([jnp.where(keep, pltpu.roll(a, 120, 1), 0.0),
                                    jnp.where(keep, pltpu.roll(bi, 120, 1), 0.0)], axis=1)


def _pre(pf, conv_ml, conv_hy, bif_row, cos_t, sin_t, tril, triu, nt):
    nb, ltot, _ = pf.shape
    last8 = ltot // 8 - 1
    r8 = TT // 8

    def cur(col):
        return lambda t, b: (b, t, col)

    def prv(col):
        return lambda t, b: (b, jnp.maximum(t * r8 - 1, 0), col)

    def nxt(col):
        return lambda t, b: (b, jnp.minimum((t + 1) * r8, last8), col)

    const = lambda t, b: (0, 0)
    return pl.pallas_call(
        functools.partial(_pre_kernel, nt), grid=(nt + 1, nb),
        in_specs=[pl.BlockSpec((1, TT, 512), cur(PF_QK)), pl.BlockSpec((1, 8, 512), prv(PF_QK)),
                  pl.BlockSpec((1, 8, 512), nxt(PF_QK)),
                  pl.BlockSpec((1, TT, 768), cur(PF_HY)), pl.BlockSpec((1, 8, 768), prv(PF_HY)),
                  pl.BlockSpec((1, 8, 768), nxt(PF_HY)),
                  pl.BlockSpec((1, TT, 256), cur(PF_GM)),
                  pl.BlockSpec((3, 512), const), pl.BlockSpec((3, 768), const), pl.BlockSpec((1, 128), const),
                  pl.BlockSpec((TT, 512), lambda t, b: (t, 0)), pl.BlockSpec((TT, 512), lambda t, b: (t, 0)),
                  pl.BlockSpec((TT, TT), const), pl.BlockSpec((TT, TT), const)],
        out_specs=[pl.BlockSpec((1, TT, 512), lambda t, b: (b, t, 0)),
                   pl.BlockSpec((1, TT, 768), lambda t, b: (b, t, 0)),
                   pl.BlockSpec((1, 2, TT, 256), lambda t, b: (b, 0, t, 0))],
        out_shape=[jax.ShapeDtypeStruct((nb, ltot, 512), BF16),
                   jax.ShapeDtypeStruct((nb, ltot, 768), F32),
                   jax.ShapeDtypeStruct((nb, 2, ltot, 256), F32)],
        compiler_params=_params(("arbitrary", "arbitrary")), name="conv_rope_gates",
    )(pf, pf, pf, pf, pf, pf, pf, conv_ml, conv_hy, bif_row, cos_t, sin_t, tril, triu)


def _na_kernel(q_ref, k0_ref, k1_ref, k2_ref, v0_ref, v1_ref, v2_ref, kc_ref, vc_ref, za_ref, tab_ref, o_ref):
    lane = lax.broadcasted_iota(jnp.int32, (1, 128), 1)
    for hp in range(H_NA // 2):
        sl = slice(hp * 128, (hp + 1) * 128)
        q2 = q_ref[0, :, sl]
        kl = jnp.concatenate([k0_ref[0, :, sl], k1_ref[0, :, sl], k2_ref[0, :, sl]], axis=0)
        vl = jnp.concatenate([v0_ref[0, :, sl], v1_ref[0, :, sl], v2_ref[0, :, sl]], axis=0)
        kc = kc_ref[0, :, sl]
        vc = vc_ref[0, :, sl]
        acc = None
        for e in range(2):
            hm = (lane >= HEAD_DIM) if e else (lane < HEAD_DIM)
            qm = jnp.where(hm, q2, jnp.zeros_like(q2))
            s_lat = _dot_nt(qm, kl) + tab_ref[0, 2 * hp + e]
            s_ctx = _dot_nt(qm, kc)
            m = jnp.maximum(jnp.max(s_lat, axis=-1, keepdims=True), jnp.max(s_ctx, axis=-1, keepdims=True))
            p_lat = jnp.exp(s_lat - m)
            p_ctx = jnp.exp(s_ctx - m)
            den = jnp.sum(p_lat, axis=-1, keepdims=True) + jnp.sum(p_ctx, axis=-1, keepdims=True)
            o = (_dot(p_lat.astype(BF16), vl) + _dot(p_ctx.astype(BF16), vc)) / den
            acc = o if e == 0 else jnp.where(hm, o, acc)
        o_ref[0, :, sl] = (acc * _silu(za_ref[0, :, sl])).astype(BF16)


def _attention(ona, pf, table, nt, with_ctx_queries):
    nb, ltot, _ = ona.shape
    nj = nt + 1 if with_ctx_queries else nt

    def kv(col, dj):
        return lambda j, b: (b, jnp.clip(j + dj, 0, nt - 1), col)

    def pat(j, b):
        p = jnp.where(j == 0, 0, jnp.where(j == nt - 1, 2, jnp.where(j == nt, 3, 1)))
        return (p, 0, 0, 0)

    return pl.pallas_call(
        _na_kernel, grid=(nj, nb),
        in_specs=[pl.BlockSpec((1, TT, 512), lambda j, b: (b, j, 0)),
                  pl.BlockSpec((1, TT, 512), kv(1, -1)), pl.BlockSpec((1, TT, 512), kv(1, 0)),
                  pl.BlockSpec((1, TT, 512), kv(1, 1)),
                  pl.BlockSpec((1, TT, 512), kv(2, -1)), pl.BlockSpec((1, TT, 512), kv(2, 0)),
                  pl.BlockSpec((1, TT, 512), kv(2, 1)),
                  pl.BlockSpec((1, TT, 512), lambda j, b: (b, nt, 1)),
                  pl.BlockSpec((1, TT, 512), lambda j, b: (b, nt, 2)),
                  pl.BlockSpec((1, TT, 512), lambda j, b: (b, j, PF_ZA)),
                  pl.BlockSpec((1, H_NA, TT, 3 * TT), pat)],
        out_specs=pl.BlockSpec((1, TT, 512), lambda j, b: (b, j, 0)),
        out_shape=jax.ShapeDtypeStruct((nb, ltot, 512), BF16),
        compiler_params=_params(("arbitrary", "arbitrary")), name="nbr_attention",
    )(ona, ona, ona, ona, ona, ona, ona, ona, ona, pf, table)


def _na_bias_table(rpb):
    a = np.arange(TT) // GRID_W
    qc = np.arange(TT) % GRID_W
    kidx = np.arange(3 * TT)
    kr = (kidx // TT - 1) * BAND_ROWS + (kidx % TT) // GRID_W
    kcol = kidx % GRID_W
    qstart = np.clip(qc - NA_COLS // 2, 0, GRID_W - NA_COLS)
    col_ok = (kcol[None, :] >= qstart[:, None]) & (kcol[None, :] < qstart[:, None] + NA_COLS)
    coff = np.clip(kcol[None, :] - qc[:, None], -(NA_COLS - 1), NA_COLS - 1) + NA_COLS - 1
    roff = np.clip(kr[None, :] - a[:, None] + NA_ROWS - 1, 0, 2 * NA_ROWS - 2)
    half = NA_ROWS // 2
    row_ok = [
        (kr[None, :] >= 0) & (kr[None, :] < NA_ROWS) & (a[:, None] >= 0),
        (kr[None, :] >= a[:, None] - half) & (kr[None, :] < a[:, None] + half),
        (kr[None, :] >= BAND_ROWS - NA_ROWS) & (kr[None, :] < BAND_ROWS) & (a[:, None] >= 0),
    ]
    nr, ncol = 2 * NA_ROWS - 1, 2 * NA_COLS - 1
    onehot = (coff[:GRID_W, :GRID_W].reshape(1, -1) == np.arange(ncol)[:, None]).astype(np.float32)
    cmat = jnp.dot(rpb.astype(F32).reshape(H_NA * nr, ncol), onehot,
                   precision=lax.Precision.HIGHEST).reshape(H_NA, nr, GRID_W, GRID_W)
    nkr = 3 * BAND_ROWS
    blocks = [jnp.concatenate([cmat[:, roff[ai * GRID_W, kj * GRID_W]] for kj in range(nkr)], axis=-1)
              for ai in range(BAND_ROWS)]
    bias = jnp.concatenate(blocks, axis=1)
    tabs = [jnp.where((r & col_ok)[None], bias, NEG) for r in row_ok]
    tabs.append(jnp.full_like(bias, NEG))
    return jnp.stack(tabs, axis=0)


def _scan_kernel(q_ref, k_ref, v_ref, g_ref, e_ref, bm_ref, o_ref, w_ref, m_ref):
    s = pl.program_id(2)
    sgn = 1 - 2 * pl.program_id(1)

    @pl.when(s == 0)
    def _():
        w_ref[...] = jnp.zeros_like(w_ref)
        m_ref[...] = jnp.zeros_like(m_ref)

    a = g_ref[0, 0, :, 0:128]
    bb = g_ref[0, 0, :, 128:256]
    m_prev = m_ref[...]
    mc = jnp.maximum(m_prev, jnp.max(a, axis=0, keepdims=True))
    small = jnp.concatenate([jnp.exp(a - mc), jnp.exp(-bb - mc),
                             jnp.broadcast_to(jnp.exp(m_prev - mc), (8, 128))], axis=0)
    sh = small.astype(BF16)
    sl = (small - sh.astype(F32)).astype(BF16)
    ex = _dot(sh, e_ref[...]) + _dot(sl, e_ref[...])
    w_e = ex[0:TT]
    eb_e = ex[TT:2 * TT]
    beta = ex[2 * TT:2 * TT + 1]
    beta2 = jnp.concatenate([beta, beta], axis=1)

    q = q_ref[0]
    kw = (k_ref[0].astype(F32) * w_e).astype(BF16)
    vb = v_ref[0].astype(BF16)
    vext = jnp.concatenate([vb, jnp.ones_like(vb)], axis=1)
    lane_h = lax.broadcasted_iota(jnp.int32, (1, D_ML), 1) // HEAD_DIM
    lane_h2 = jnp.concatenate([lane_h, lane_h], axis=1)
    ri = lax.broadcasted_iota(jnp.int32, (TT, TT), 0)
    ci = lax.broadcasted_iota(jnp.int32, (TT, TT), 1)
    tri = ((ci - ri) * sgn) <= 0

    intra = None
    for h in range(H_ML):
        qh = jnp.where(lane_h == h, q, jnp.zeros_like(q))
        sc = jnp.where(tri, _dot_nt(qh, kw), 0.0).astype(BF16)
        vh = jnp.where(lane_h2 == h, vext, jnp.zeros_like(vext))
        part = _dot(sc, vh)
        intra = part if intra is None else intra + part
    inter = _dot(q, w_ref[...].astype(BF16))
    tot = beta2 * inter + intra
    num = tot[:, 0:D_ML]
    den = tot[:, D_ML:2 * D_ML]
    o_ref[0, 0] = num / jnp.maximum(jnp.abs(den), eb_e)

    w_ref[...] = beta2 * w_ref[...] + bm_ref[...] * _dot_tn(kw, vext)
    m_ref[...] = jnp.min(bb, axis=0, keepdims=True) + mc


def _mlstm_scan(qk, pf, gates, emat, bmask, nt):
    nb, ltot, _ = qk.shape

    def chunk(s, d):
        return jnp.where(s == 0, nt, jnp.where(d == 0, s - 1, nt - s))

    const = lambda b, d, s: (0, 0)
    return pl.pallas_call(
        _scan_kernel, grid=(nb, 2, nt + 1),
        in_specs=[pl.BlockSpec((1, TT, 256), lambda b, d, s: (b, chunk(s, d), 0)),
                  pl.BlockSpec((1, TT, 256), lambda b, d, s: (b, chunk(s, d), 1)),
                  pl.BlockSpec((1, TT, 256), lambda b, d, s: (b, chunk(s, d), PF_VM)),
                  pl.BlockSpec((1, 1, TT, 256), lambda b, d, s: (b, d, chunk(s, d), 0)),
                  pl.BlockSpec((128, 256), const), pl.BlockSpec((256, 512), const)],
        out_specs=pl.BlockSpec((1, 1, TT, 256), lambda b, d, s: (b, d, chunk(s, d), 0)),
        out_shape=jax.ShapeDtypeStruct((nb, 2, ltot, 256), F32),
        scratch_shapes=[pltpu.VMEM((D_ML, 2 * D_ML), F32), pltpu.VMEM((1, 128), F32)],
        compiler_params=_params(("arbitrary", "arbitrary", "arbitrary")), name="mlstm_scan",
    )(qk, qk, pf, gates, emat, bmask)


def _hyfilt_kernel(tl, feat_ref, w1_ref, b1_ref, w2_ref, b2_ref, w3_ref, fq_ref, dec_ref, o_ref, ss_ref):
    i = pl.program_id(0)
    a = jnp.sin(fq_ref[0:1] * (_dot_f32(feat_ref[...], w1_ref[...]) + b1_ref[...]))
    a = jnp.sin(fq_ref[1:2] * (_dot_f32(a, w2_ref[...]) + b2_ref[...]))
    dec = dec_ref[...]
    f = _dot_f32(a, w3_ref[...]) * jnp.concatenate([dec, dec, dec, dec], axis=1)
    o_ref[...] = f
    row = lax.broadcasted_iota(jnp.int32, (tl, 1), 0) + i * tl
    col = lax.broadcasted_iota(jnp.int32, (1, 4 * D_HY), 1)
    drop = jnp.logical_and(row == 0, (col // D_HY) % 2 == 1)
    part = jnp.sum(jnp.where(drop, 0.0, f * f), axis=0, keepdims=True)

    @pl.when(i == 0)
    def _():
        ss_ref[...] = jnp.zeros_like(ss_ref)

    ss_ref[...] += part


def _hyena_filters(length, w1, b1, w2, b2, w3, freq):
    t = np.arange(length, dtype=np.float32)
    tn = (t / np.float32(length - 1)).astype(np.float32)
    bands = (HY_EMB - 1) // 2
    fr = np.linspace(1e-4, bands - 1, bands, dtype=np.float32)
    ang = (np.float32(2.0 * math.pi / length) * t[:, None] * fr[None, :]).astype(np.float32)
    feat = np.zeros((length, 128), np.float32)
    feat[:, 0] = tn
    feat[:, 1:1 + bands] = np.cos(ang)
    feat[:, 1 + bands:1 + 2 * bands] = -np.sin(ang)
    deltas = np.abs(np.linspace(math.log(HY_TARGET) / HY_SLOW, math.log(HY_TARGET) / HY_FAST, D_HY, dtype=np.float32))
    dec = np.exp(-tn[:, None] * deltas[None, :]).astype(np.float32)

    pad2 = lambda m, r, c: jnp.zeros((r, c), F32).at[:m.shape[0], :m.shape[1]].set(m.astype(F32))
    w1p = pad2(w1, 128, 128)
    w2p = pad2(w2, 128, 128)
    w3p = pad2(w3, 128, 4 * D_HY)
    b1p = pad2(b1[None], 1, 128)
    b2p = pad2(b2[None], 1, 128)
    fqp = pad2(freq, 2, 128)
    tl = min(length, 512)
    const = lambda i: (0, 0)
    return pl.pallas_call(
        functools.partial(_hyfilt_kernel, tl), grid=(length // tl,),
        in_specs=[pl.BlockSpec((tl, 128), lambda i: (i, 0)),
                  pl.BlockSpec((128, 128), const), pl.BlockSpec((1, 128), const),
                  pl.BlockSpec((128, 128), const), pl.BlockSpec((1, 128), const),
                  pl.BlockSpec((128, 4 * D_HY), const), pl.BlockSpec((2, 128), const),
                  pl.BlockSpec((tl, D_HY), lambda i: (i, 0))],
        out_specs=[pl.BlockSpec((tl, 4 * D_HY), lambda i: (i, 0)), pl.BlockSpec((1, 4 * D_HY), const)],
        out_shape=[jax.ShapeDtypeStruct((length, 4 * D_HY), F32), jax.ShapeDtypeStruct((1, 4 * D_HY), F32)],
        compiler_params=_params(("arbitrary",)), name="hyena_filter_mlp",
    )(feat, w1p, b1p, w2p, b2p, w3p, fqp, dec)


def _filter_scale(ssq):
    s = ssq.reshape(2, 2, D_HY)
    return lax.rsqrt(s[:, 0] + s[:, 1] + EPS)


@functools.lru_cache(maxsize=None)
def _fft_tables(length):
    n = 2 * length
    n1 = n // FFT_N2
    nh = n1 // 2
    k1 = np.arange(n1)[None, :, None].astype(np.float64)
    j = np.arange(nh)[None, None, :].astype(np.float64)
    n2 = np.arange(FFT_N2)[:, None, None].astype(np.float64)
    ang = -2.0 * np.pi * k1 * (FFT_N2 * j + n2) / n
    gf = np.concatenate([np.cos(ang), np.sin(ang)], axis=1)
    gi = np.concatenate([np.cos(ang).transpose(0, 2, 1), np.sin(ang).transpose(0, 2, 1)], axis=1)
    kk = np.arange(FFT_N2)[:, None] * np.arange(FFT_N2)[None, :]
    a2 = -2.0 * np.pi * kk / FFT_N2
    f2 = np.concatenate([np.cos(a2), np.sin(a2)], axis=0)
    cv = lambda m: np.asarray(m, np.float32).astype(BF16)
    return cv(gf), cv(gi), cv(f2), n1


def _stage1(n1, z_of, gf_ref, scr_ref, cols):
    nh = n1 // 2

    def body(n2, c):
        r = _dot(gf_ref[n2], z_of(n2))
        off = pl.multiple_of(n2 * n1, n1)
        if cols == 256:
            scr_ref[0, pl.ds(off, n1), :] = r[:n1, :128] - r[n1:, 128:]
            scr_ref[1, pl.ds(off, n1), :] = r[n1:, :128] + r[:n1, 128:]
        else:
            scr_ref[0, pl.ds(off, n1), :] = r[:n1]
            scr_ref[1, pl.ds(off, n1), :] = r[n1:]
        return c

    lax.fori_loop(0, FFT_N2, body, 0, unroll=FFT_UNROLL)
    del nh


def _stage2(n1, k1, f2_ref, scr_ref):
    ar = scr_ref[0, pl.ds(k1, FFT_N2, stride=n1), :]
    ai = scr_ref[1, pl.ds(k1, FFT_N2, stride=n1), :]
    r = _dot(f2_ref[...], jnp.concatenate([ar, ai], axis=1).astype(BF16))
    return r[:128, :128] - r[128:, 128:], r[128:, :128] + r[:128, 128:]


def _hyconv_kernel(n1, kg, z_ref, gf_ref, f2_ref, kf_ref, gi_ref, o_ref, scr_ref):
    s = pl.program_id(2)
    nh = n1 // 2

    @pl.when(s == 0)
    def _():
        def z_of(n2):
            zr = z_ref[0, 0, pl.ds(n2, nh, stride=FFT_N2), :]
            zi = z_ref[0, 1, pl.ds(n2, nh, stride=FFT_N2), :]
            return jnp.concatenate([zr, zi], axis=1).astype(BF16)
        _stage1(n1, z_of, gf_ref, scr_ref, 256)

    def k1_body(i, c):
        k1 = s * kg + i
        xr, xi = _stage2(n1, k1, f2_ref, scr_ref)
        off = pl.multiple_of(i * FFT_N2, FFT_N2)
        kr = kf_ref[0, pl.ds(off, FFT_N2), :]
        ki = kf_ref[1, pl.ds(off, FFT_N2), :]
        yc = jnp.concatenate([xr * kr - xi * ki, xr * ki + xi * kr], axis=1).astype(BF16)
        r = _dot(f2_ref[...], yc)
        scr_ref[0, pl.ds(k1, FFT_N2, stride=n1), :] = r[:128, :128] + r[128:, 128:]
        scr_ref[1, pl.ds(k1, FFT_N2, stride=n1), :] = r[:128, 128:] - r[128:, :128]
        return c

    lax.fori_loop(0, kg, k1_body, 0, unroll=FFT_UNROLL)

    @pl.when(s == pl.num_programs(2) - 1)
    def _():
        def body(n2, c):
            off = pl.multiple_of(n2 * n1, n1)
            bc = jnp.concatenate([scr_ref[0, pl.ds(off, n1), :], scr_ref[1, pl.ds(off, n1), :]], axis=1)
            r = _dot(gi_ref[n2], bc.astype(BF16))
            o_ref[0, 0, pl.ds(n2, nh, stride=FFT_N2), :] = r[:nh, :128] + r[nh:, 128:]
            o_ref[0, 1, pl.ds(n2, nh, stride=FFT_N2), :] = r[:nh, 128:] - r[nh:, :128]
            return c
        lax.fori_loop(0, FFT_N2, body, 0, unroll=FFT_UNROLL)


def _hyena_conv(z4, col0, kf, length):
    gf, gi, f2, n1 = _fft_tables(length)
    n = 2 * length
    kg = min(n1, 16)
    npair = z4.shape[0]
    big = pl.Buffered(1)
    return pl.pallas_call(
        functools.partial(_hyconv_kernel, n1, kg), grid=(npair, 2, n1 // kg),
        in_specs=[pl.BlockSpec((1, 2, length, 128), lambda p, c, s: (p, 0, 0, col0 + c), pipeline_mode=big),
                  pl.BlockSpec(gf.shape, lambda p, c, s: (0, 0, 0), pipeline_mode=big),
                  pl.BlockSpec(f2.shape, lambda p, c, s: (0, 0)),
                  pl.BlockSpec((2, kg * FFT_N2, 128), lambda p, c, s: (0, s, c)),
                  pl.BlockSpec(gi.shape, lambda p, c, s: (0, 0, 0), pipeline_mode=big)],
        out_specs=pl.BlockSpec((1, 2, length, 128), lambda p, c, s: (p, 0, 0, c), pipeline_mode=big),
        out_shape=jax.ShapeDtypeStruct((npair, 2, length, 256), F32),
        scratch_shapes=[pltpu.VMEM((2, n, 128), F32)],
        compiler_params=_params(("arbitrary", "arbitrary", "arbitrary")), name="hyena_fftconv",
    )(z4, gf, f2, kf, gi)


def _hyfft_kernel(n1, kg, z_ref, gf_ref, f2_ref, o_ref, scr_ref):
    s = pl.program_id(2)
    nh = n1 // 2

    @pl.when(s == 0)
    def _():
        _stage1(n1, lambda n2: z_ref[0, pl.ds(n2, nh, stride=FFT_N2), :].astype(BF16), gf_ref, scr_ref, 128)

    def k1_body(i, c):
        xr, xi = _stage2(n1, s * kg + i, f2_ref, scr_ref)
        off = pl.multiple_of(i * FFT_N2, FFT_N2)
        o_ref[0, 0, pl.ds(off, FFT_N2), :] = xr
        o_ref[0, 1, pl.ds(off, FFT_N2), :] = xi
        return c

    lax.fori_loop(0, kg, k1_body, 0, unroll=FFT_UNROLL)


def _filter_spectrum(filt4, length):
    gf, _, f2, n1 = _fft_tables(length)
    n = 2 * length
    kg = min(n1, 16)
    nf = filt4.shape[0]
    return pl.pallas_call(
        functools.partial(_hyfft_kernel, n1, kg), grid=(nf, 2, n1 // kg),
        in_specs=[pl.BlockSpec((1, length, 128), lambda f, c, s: (f, 0, c)),
                  pl.BlockSpec(gf.shape, lambda f, c, s: (0, 0, 0), pipeline_mode=pl.Buffered(1)),
                  pl.BlockSpec(f2.shape, lambda f, c, s: (0, 0))],
        out_specs=pl.BlockSpec((1, 2, kg * FFT_N2, 128), lambda f, c, s: (f, 0, s, c)),
        out_shape=jax.ShapeDtypeStruct((nf, 2, n, 256), F32),
        scratch_shapes=[pltpu.VMEM((2, n, 128), F32)],
        compiler_params=_params(("arbitrary", "arbitrary", "arbitrary")), name="hyena_filter_fft",
    )(filt4, gf, f2)


def _latent_filter_spectra(filt, scale, length):
    f4 = filt.reshape(length, 4, D_HY).transpose(1, 0, 2)
    lag0 = (jnp.arange(length) == 0)[None, :, None]
    is_bwd = (jnp.arange(4) % 2 == 1)[:, None, None]
    f4 = jnp.where(jnp.logical_and(lag0, is_bwd), 0.0, f4)
    sp = _filter_spectrum(f4, length).reshape(2, 2, 2, 2 * length, D_HY)
    sc = (scale / (2 * length))[:, None, None, :]
    re = (sp[:, 0, 0] + sp[:, 1, 0])[:, None]
    im = (sp[:, 0, 1] - sp[:, 1, 1])[:, None]
    return jnp.concatenate([re, im], axis=1) * sc


def _hy_mid_kernel(y_ref, hv_ref, b_ref, o_ref):
    v = hv_ref[0][:, 0:D_HY]
    x1 = hv_ref[0][:, D_HY:2 * D_HY]
    o_ref[0] = x1 * (y_ref[0] + v * b_ref[...])


def _hy_mid(y1, hv, bias0, nt):
    nb = y1.shape[0]
    return pl.pallas_call(
        _hy_mid_kernel, grid=(nb, nt),
        in_specs=[pl.BlockSpec((1, TT, D_HY), lambda b, t: (b, t, 0)),
                  pl.BlockSpec((1, TT, 512), lambda b, t: (b, t, 0)),
                  pl.BlockSpec((1, D_HY), lambda b, t: (0, 0))],
        out_specs=pl.BlockSpec((1, TT, D_HY), lambda b, t: (b, t, 0)),
        out_shape=jax.ShapeDtypeStruct(y1.shape, F32),
        compiler_params=_params(("arbitrary", "arbitrary")), name="hyena_mid",
    )(y1, hv, bias0.reshape(1, D_HY))


def _hy_end_kernel(y_ref, z_ref, x2_ref, zh_ref, b_ref, o_ref):
    z = z_ref[0]
    o_ref[0] = (x2_ref[0] * (y_ref[0] + z * b_ref[...]) * _silu(zh_ref[0])).astype(BF16)


def _hy_end(y2, z1, hv, pf, bias1, nt):
    nb = y2.shape[0]
    return pl.pallas_call(
        _hy_end_kernel, grid=(nb, nt),
        in_specs=[pl.BlockSpec((1, TT, D_HY), lambda b, t: (b, t, 0)),
                  pl.BlockSpec((1, TT, D_HY), lambda b, t: (b, t, 0)),
                  pl.BlockSpec((1, TT, D_HY), lambda b, t: (b, t, 2)),
                  pl.BlockSpec((1, TT, D_HY), lambda b, t: (b, t, PF_ZH)),
                  pl.BlockSpec((1, D_HY), lambda b, t: (0, 0))],
        out_specs=pl.BlockSpec((1, TT, D_HY), lambda b, t: (b, t, 0)),
        out_shape=jax.ShapeDtypeStruct(y2.shape, BF16),
        compiler_params=_params(("arbitrary", "arbitrary")), name="hyena_end",
    )(y2, z1, hv, pf, bias1.reshape(1, D_HY))


def _hy_ctx_kernel(lc, hv_ref, zh_ref, k2_ref, b_ref, fc_ref, ff_ref, ci_ref, o_ref):
    nf = 2 * lc
    hv = hv_ref[0]

    def conv(z, o):
        kf = _dot_exact_rhs(ff_ref[...], k2_ref[o])
        zs = _dot(fc_ref[...], z.astype(BF16))
        zr, zi = zs[:nf], zs[nf:]
        kr, ki = kf[:nf], kf[nf:]
        ys = jnp.concatenate([zr * kr - zi * ki, zr * ki + zi * kr], axis=0).astype(BF16)
        return _dot(ci_ref[...], ys) * (1.0 / nf)

    v = hv[:, 0:D_HY]
    x1 = hv[:, D_HY:2 * D_HY]
    x2 = hv[:, 2 * D_HY:3 * D_HY]
    z1 = x1 * (conv(v, 0) + v * b_ref[0:1])
    p = x2 * (conv(z1, 1) + z1 * b_ref[1:2])
    o_ref[0] = (p * _silu(zh_ref[0])).astype(BF16)


def _dot_exact_rhs(a_bf16, b):
    bh, bm, bl = _split3(b)
    return _dot(a_bf16, bh) + _dot(a_bf16, bm) + _dot(a_bf16, bl)


@functools.lru_cache(maxsize=None)
def _ctx_dft_tables(lc):
    nf = 2 * lc
    k = np.arange(nf)[:, None].astype(np.float64)
    ang = -2.0 * np.pi * k * np.arange(nf)[None, :] / nf
    ff = np.concatenate([np.cos(ang), np.sin(ang)], axis=0)
    fc = ff[:, :lc]
    a2 = 2.0 * np.pi * np.arange(lc)[:, None] * np.arange(nf)[None, :] / nf
    ci = np.concatenate([np.cos(a2), -np.sin(a2)], axis=1)
    cv = lambda m: np.asarray(m, np.float32).astype(BF16)
    return cv(fc), cv(ff), cv(ci)


def _hy_ctx(hv, pf, k2c, hy_bias, nt, lc):
    nb = hv.shape[0]
    fc, ff, ci = _ctx_dft_tables(lc)
    c2 = lambda b: (0, 0)
    return pl.pallas_call(
        functools.partial(_hy_ctx_kernel, lc), grid=(nb,),
        in_specs=[pl.BlockSpec((1, lc, 768), lambda b: (b, nt, 0)),
                  pl.BlockSpec((1, lc, D_HY), lambda b: (b, nt, PF_ZH)),
                  pl.BlockSpec(k2c.shape, lambda b: (0, 0, 0)),
                  pl.BlockSpec((2, D_HY), c2),
                  pl.BlockSpec(fc.shape, c2), pl.BlockSpec(ff.shape, c2), pl.BlockSpec(ci.shape, c2)],
        out_specs=pl.BlockSpec((1, lc, D_HY), lambda b: (b, 0, 0)),
        out_shape=jax.ShapeDtypeStruct((nb, lc, D_HY), BF16),
        compiler_params=_params(("arbitrary",)), name="hyena_ctx",
    )(hv, pf, k2c, hy_bias, fc, ff, ci)


def _out_kernel(na_ref, h0_ref, h1_ref, om_ref, zm_ref, hy_ref, w_ref, x_ref, mod_ref, g_ref, o_ref):
    ml = jax.nn.sigmoid(om_ref[0]) * (h0_ref[0, 0] + h1_ref[0, 0]) * _silu(zm_ref[0])
    y = (_dot(na_ref[0], w_ref[0:D_NA]) + _dot(ml.astype(BF16), w_ref[D_NA:D_NA + D_ML])
         + _dot(hy_ref[0], w_ref[D_NA + D_ML:D_MODEL]))
    ms = jnp.mean(y * y, axis=-1, keepdims=True)
    r = y * lax.rsqrt(ms + EPS) * g_ref[...]
    o_ref[0] = x_ref[0] + mod_ref[0][2:3] * r


def _out_projection(na_g, hml, pf, hy_g, w_out, xc, mod3, g_post, t0, ntiles, hy_t0, mod_row, out_rows, prev=None):
    nb = xc.shape[0]
    d = D_MODEL
    tok = lambda col: (lambda b, t: (b, t + t0, col))
    in_specs = [pl.BlockSpec((1, TT, 512), tok(0)),
                pl.BlockSpec((1, 1, TT, 256), lambda b, t: (b, 0, t + t0, 0)),
                pl.BlockSpec((1, 1, TT, 256), lambda b, t: (b, 1, t + t0, 0)),
                pl.BlockSpec((1, TT, 256), tok(PF_OM)), pl.BlockSpec((1, TT, 256), tok(PF_ZM)),
                pl.BlockSpec((1, TT, 256), lambda b, t: (b, t + hy_t0, 0)),
                pl.BlockSpec((d, d), lambda b, t: (0, 0), pipeline_mode=pl.Buffered(1)),
                pl.BlockSpec((1, TT, d), tok(0)),
                pl.BlockSpec((1, 3, d), lambda b, t: (mod_row(b), 0, 0)),
                pl.BlockSpec((1, d), lambda b, t: (0, 0))]
    args = [na_g, hml, hml, pf, pf, hy_g, w_out, xc, mod3, g_post.reshape(1, d)]
    aliases = {}
    if prev is not None:
        in_specs.append(pl.BlockSpec(memory_space=pl.ANY))
        args.append(prev)
        aliases = {len(args) - 1: 0}
        kern = lambda *r: _out_kernel(*r[:10], r[11])
    else:
        kern = _out_kernel
    return pl.pallas_call(
        kern, grid=(nb, ntiles), in_specs=in_specs,
        out_specs=pl.BlockSpec((1, TT, d), tok(0)),
        out_shape=jax.ShapeDtypeStruct((nb, out_rows, d), F32),
        input_output_aliases=aliases,
        compiler_params=_params(("arbitrary", "arbitrary")), name="out_proj_norm_residual",
    )(*args)


@functools.lru_cache(maxsize=None)
def _rope_tables(length, lc):
    n = HEAD_DIM // 4
    t = np.arange(length)
    inv = (ROPE_BASE ** (-np.arange(n, dtype=np.float32) / n)).astype(np.float32)
    pos = np.stack([t // GRID_W, t % GRID_W], axis=-1).astype(np.float32)
    ang = (pos[:, :, None] * inv).astype(np.float32)
    cos_h = np.concatenate([np.cos(ang[:, 0]), np.cos(ang[:, 0]), np.cos(ang[:, 1]), np.cos(ang[:, 1])], axis=-1)
    sin_h = np.concatenate([-np.sin(ang[:, 0]), np.sin(ang[:, 0]), -np.sin(ang[:, 1]), np.sin(ang[:, 1])], axis=-1)
    cos_t = np.concatenate([np.tile(cos_h, (1, 2 * H_ML)), np.ones((lc, 512), np.float32)], axis=0)
    sin_t = np.concatenate([np.tile(sin_h, (1, 2 * H_ML)), np.zeros((lc, 512), np.float32)], axis=0)
    kscale = np.concatenate([np.ones(256, np.float32), np.full(256, HEAD_DIM ** -0.5, np.float32)])
    return (cos_t * kscale).astype(np.float32), (sin_t * kscale).astype(np.float32)


@functools.lru_cache(maxsize=None)
def _scan_constants():
    tril = np.tril(np.ones((TT, TT), np.float32))
    emat = np.zeros((128, D_ML), np.float32)
    for h in range(H_ML):
        emat[h, h * HEAD_DIM:(h + 1) * HEAD_DIM] = 1.0
    rh = np.arange(D_ML)[:, None] // HEAD_DIM
    chd = (np.arange(2 * D_ML)[None, :] % D_ML) // HEAD_DIM
    bmask = (rh == chd).astype(np.float32)
    return tril.astype(BF16), np.ascontiguousarray(tril.T).astype(BF16), emat.astype(BF16), bmask


def _layer(xc, mod3, nb, length, lc, g_pre, g_post, w_in, b_if, conv_ml, conv_hy, rpb,
           hf_w1, hf_b1, hf_w2, hf_b2, hf_w3, hf_freq, hy_bias, w_out, update_ctx):
    nt = length // TT
    ltot = length + lc
    tril, triu, emat, bmask = _scan_constants()
    cos_t, sin_t = _rope_tables(length, lc)

    c = lambda a, b: w_in[:, a:b]
    wna = jnp.concatenate([c(0, 512) * (HEAD_DIM ** -0.5), c(512, 1536)], axis=1).astype(BF16)
    wpf = jnp.concatenate([c(1536, 2048), c(2048, 2560), c(2560, 2816), c(2816, 3072), c(3072, 3328),
                           c(4112, 4368), c(3328, 3344), jnp.zeros((D_MODEL, 240), F32), c(3344, 4112)],
                          axis=1).astype(BF16)
    ona, pf = _projection(xc, mod3, g_pre, wna, wpf, nt)

    na_g = _attention(ona, pf, _na_bias_table(rpb), nt, update_ctx)

    bif_row = jnp.zeros((1, 128), F32).at[0, :4 * H_ML].set(b_if.reshape(-1))
    qk, hv, gates = _pre(pf, conv_ml, conv_hy, bif_row, cos_t, sin_t, tril, triu, nt)
    hml = _mlstm_scan(qk, pf, gates, emat, bmask, nt)

    filt, ssq = _hyena_filters(length, hf_w1, hf_b1, hf_w2, hf_b2, hf_w3, hf_freq)
    kf = _latent_filter_spectra(filt, _filter_scale(ssq), length)
    hv4 = hv.reshape(nb // 2, 2, ltot, 768)
    y1 = _hyena_conv(hv4, 0, kf[0], length).reshape(nb, length, D_HY)
    z1 = _hy_mid(y1, hv, hy_bias[0], nt)
    y2 = _hyena_conv(z1.reshape(nb // 2, 2, length, D_HY), 0, kf[1], length).reshape(nb, length, D_HY)
    hy_g = _hy_end(y2, z1, hv, pf, hy_bias[1], nt)

    if not update_ctx:
        return _out_projection(na_g, hml, pf, hy_g, w_out, xc, mod3, g_post, 0, nt, 0, lambda b: b, length)

    xn = _out_projection(na_g, hml, pf, hy_g, w_out, xc, mod3, g_post, 0, nt, 0, lambda b: b, ltot)
    filt_c, ssq_c = _hyena_filters(lc, hf_w1, hf_b1, hf_w2, hf_b2, hf_w3, hf_freq)
    fc4 = (filt_c.reshape(lc, 2, 2, D_HY) * _filter_scale(ssq_c)[None, :, None, :])
    k2c = jnp.concatenate([fc4[:, :, 0], jnp.zeros((1, 2, D_HY), F32), jnp.flip(fc4[1:, :, 1], axis=0)], axis=0)
    hyc_g = _hy_ctx(hv, pf, k2c.transpose(1, 0, 2), hy_bias, nt, lc)
    return _out_projection(na_g, hml, pf, hyc_g, w_out, xc, mod3, g_post, nt, 1, 0, lambda b: nb, ltot, prev=xn)


def kernel(x, c, ctx, c_ctx, w_ada, b_ada, g_pre, g_post, w_in, b_if, conv_ml, conv_hy, rpb, hf_w1, hf_b1,
           hf_w2, hf_b2, hf_w3, hf_freq, hy_bias, w_out):
    nb, length, d = x.shape
    lc = ctx.shape[1]
    depth = w_in.shape[0]
    assert lc == TT and length % (2 * TT) == 0 and nb % 2 == 0 and d == D_MODEL
    xc = jnp.concatenate([x, ctx], axis=1)
    cc = jnp.zeros((16, d), F32).at[:nb].set(c).at[nb].set(c_ctx)
    for l in range(depth):
        mod3 = _modulation(cc, w_ada[l], b_ada[l]).reshape(16, 3, d)
        xc = _layer(xc, mod3, nb, length, lc, g_pre[l], g_post[l], w_in[l], b_if[l], conv_ml[l], conv_hy[l],
                    rpb[l], hf_w1[l], hf_b1[l], hf_w2[l], hf_b2[l], hf_w3[l], hf_freq[l], hy_bias[l],
                    w_out[l].astype(BF16), l < depth - 1)
    return xc
```

```python
import functools
import math

import numpy as np
import jax
import jax.numpy as jnp
from jax import lax
from jax.experimental import pallas as pl
from jax.experimental.pallas import tpu as pltpu

F32 = jnp.float32
BF16 = jnp.bfloat16

D_MODEL = 1024
GRID_W = 64
HEAD_DIM = 64
D_NA = 512
D_ML = 256
D_HY = 256
H_NA = 8
H_ML = 4
NA_ROWS = 8
NA_COLS = 16
HY_EMB = 33
HY_FFN = 64
HY_TARGET = 1e-2
HY_FAST = 0.3
HY_SLOW = 1.5
ROPE_BASE = 10000.0
EPS = 1e-6

TT = 256
BAND_ROWS = TT // GRID_W
PRE_ROWS = 64
FFT_N2 = 128
FFT_UNROLL = 8
NEG = -1e30
VMEM_LIMIT = 56 * 1024 * 1024

PF_ZA = 0
PF_QK = 1
PF_VM, PF_OM, PF_ZM, PF_ZH, PF_GM = 4, 5, 6, 7, 8
PF_HY = 3
PF_COLS = 3072


def _dot(a, b):
    return jnp.dot(a, b, preferred_element_type=F32)


def _dot_nt(a, b):
    return lax.dot_general(a, b, (((1,), (1,)), ((), ())), preferred_element_type=F32)


def _dot_tn(a, b):
    return lax.dot_general(a, b, (((0,), (0,)), ((), ())), preferred_element_type=F32)


def _split3(x):
    h = x.astype(BF16)
    r = x - h.astype(F32)
    m = r.astype(BF16)
    l = (r - m.astype(F32)).astype(BF16)
    return h, m, l


def _dot_f32(a, b):
    ah, am, al = _split3(a)
    bh, bm, bl = _split3(b)
    return (_dot(ah, bh) + (_dot(ah, bm) + _dot(am, bh))
            + (_dot(ah, bl) + _dot(al, bh) + _dot(am, bm)))


def _dot_exact_lhs(a_bf16, b):
    bh, bm, bl = _split3(b)
    return _dot(a_bf16, bh) + _dot(a_bf16, bm) + _dot(a_bf16, bl)


def _silu(x):
    return x * jax.nn.sigmoid(x)


def _params(sem, vmem=VMEM_LIMIT):
    return pltpu.CompilerParams(dimension_semantics=sem, vmem_limit_bytes=vmem)


def _mod_kernel(c_ref, w_ref, b_ref, o_ref):
    o_ref[...] = _dot_f32(_silu(c_ref[...]), w_ref[...]) + b_ref[...]


def _modulation(cc, w_ada, b_ada):
    n = w_ada.shape[1]
    bn = 512
    return pl.pallas_call(
        _mod_kernel, grid=(n // bn,),
        in_specs=[pl.BlockSpec(cc.shape, lambda j: (0, 0)),
                  pl.BlockSpec((w_ada.shape[0], bn), lambda j: (0, j)),
                  pl.BlockSpec((1, bn), lambda j: (0, j))],
        out_specs=pl.BlockSpec((cc.shape[0], bn), lambda j: (0, j)),
        out_shape=jax.ShapeDtypeStruct((cc.shape[0], n), F32),
        compiler_params=_params(("arbitrary",)), name="adaln_mod",
    )(cc, w_ada, b_ada.reshape(1, n))


def _proj_kernel(x_ref, mod_ref, g_ref, wna_ref, wpf_ref, ona_ref, opf_ref):
    x = x_ref[0]
    ms = jnp.mean(x * x, axis=-1, keepdims=True)
    y = x * lax.rsqrt(ms + EPS) * g_ref[...]
    mod = mod_ref[0]
    h = (y * (1.0 + mod[1:2]) + mod[0:1]).astype(BF16)
    ona_ref[0] = _dot(h, wna_ref[...]).astype(BF16)
    opf_ref[0] = _dot(h, wpf_ref[...])


def _projection(xc, mod3, g_pre, wna, wpf, nt):
    nb, ltot, d = xc.shape
    ctx_row = nb

    def mod_idx(b, t):
        return (jnp.where(t == nt, ctx_row, b), 0, 0)

    return pl.pallas_call(
        _proj_kernel, grid=(nb, nt + 1),
        in_specs=[pl.BlockSpec((1, TT, d), lambda b, t: (b, t, 0)),
                  pl.BlockSpec((1, 3, d), mod_idx),
                  pl.BlockSpec((1, d), lambda b, t: (0, 0)),
                  pl.BlockSpec(wna.shape, lambda b, t: (0, 0), pipeline_mode=pl.Buffered(1)),
                  pl.BlockSpec(wpf.shape, lambda b, t: (0, 0), pipeline_mode=pl.Buffered(1))],
        out_specs=[pl.BlockSpec((1, TT, wna.shape[1]), lambda b, t: (b, t, 0)),
                   pl.BlockSpec((1, TT, wpf.shape[1]), lambda b, t: (b, t, 0))],
        out_shape=[jax.ShapeDtypeStruct((nb, ltot, wna.shape[1]), BF16),
                   jax.ShapeDtypeStruct((nb, ltot, wpf.shape[1]), F32)],
        compiler_params=_params(("arbitrary", "arbitrary")), name="norm_in_proj",
    )(xc, mod3, g_pre.reshape(1, d), wna, wpf)


def _pre_kernel(nt, qk_ref, qkp_ref, qkn_ref, hy_ref, hyp_ref, hyn_ref, gm_ref, cml_ref, chy_ref,
                bif_ref, cos_ref, sin_ref, tril_ref, triu_ref, oqk_ref, ohv_ref, og_ref):
    t = pl.program_id(0)
    pm = jnp.where(jnp.logical_and(t != 0, t != nt), 1.0, 0.0)
    nm = jnp.where(jnp.logical_and(t != nt - 1, t != nt), 1.0, 0.0)
    row8 = lax.broadcasted_iota(jnp.int32, (8, 1), 0)
    nchunk = TT // PRE_ROWS

    def conv3(ref, p_ref, n_ref, w, c):
        r0 = c * PRE_ROWS
        u = ref[0, r0:r0 + PRE_ROWS]
        up = p_ref[0][7:8] * pm if c == 0 else ref[0, r0 - 1:r0]
        un = n_ref[0][0:1] * nm if c == nchunk - 1 else ref[0, r0 + PRE_ROWS:r0 + PRE_ROWS + 1]
        rp = pltpu.roll(u, 1, 0)
        rn = pltpu.roll(u, PRE_ROWS - 1, 0)
        prev = jnp.concatenate([jnp.where(row8 == 0, up, rp[0:8]), rp[8:]], axis=0)
        nxt = jnp.concatenate([rn[:PRE_ROWS - 8], jnp.where(row8 == 7, un, rn[PRE_ROWS - 8:])], axis=0)
        return prev * w[0:1] + u * w[1:2] + nxt * w[2:3]

    lane = lax.broadcasted_iota(jnp.int32, (1, 2 * D_ML), 1)
    first = (lane % 32) < 16
    for c in range(nchunk):
        rows = slice(c * PRE_ROWS, (c + 1) * PRE_ROWS)
        ohv_ref[0, rows] = conv3(hy_ref, hyp_ref, hyn_ref, chy_ref[...], c)
        x = _silu(conv3(qk_ref, qkp_ref, qkn_ref, cml_ref[...], c))
        partner = jnp.where(first, pltpu.roll(x, 2 * D_ML - 16, 1), pltpu.roll(x, 16, 1))
        oqk_ref[0, rows] = (x * cos_ref[rows] + partner * sin_ref[rows]).astype(BF16)

    g = gm_ref[0][:, 0:128] + bif_ref[...]
    gl = lax.broadcasted_iota(jnp.int32, (1, 128), 1)
    is_f = jnp.logical_and((gl % 8) >= 4, gl < 16)
    lf = jnp.where(is_f, jnp.minimum(g, 0.0) - jnp.log1p(jnp.exp(-jnp.abs(g))), 0.0)
    bcum = jnp.where(gl < 8, _dot_exact_lhs(tril_ref[...], lf), _dot_exact_lhs(triu_ref[...], lf))
    bi = pltpu.roll(bcum, 128 - H_ML, 1)
    a = g - bi
    keep = gl < H_ML
    og_ref[0, 0] = jnp.concatenate([jnp.where(keep, a, 0.0), jnp.where(keep, bi, 0.0)], axis=1)
    og_ref[0, 1] = jnp.concatenate([jnp.where(keep, pltpu.roll(a, 120, 1), 0.0),
                                    jnp.where(keep, pltpu.roll(bi, 120, 1), 0.0)], axis=1)


def _pre(pf, conv_ml, conv_hy, bif_row, cos_t, sin_t, tril, triu, nt):
    nb, ltot, _ = pf.shape
    last8 = ltot // 8 - 1
    r8 = TT // 8

    def cur(col):
        return lambda t, b: (b, t, col)

    def prv(col):
        return lambda t, b: (b, jnp.maximum(t * r8 - 1, 0), col)

    def nxt(col):
        return lambda t, b: (b, jnp.minimum((t + 1) * r8, last8), col)

    const = lambda t, b: (0, 0)
    return pl.pallas_call(
        functools.partial(_pre_kernel, nt), grid=(nt + 1, nb),
        in_specs=[pl.BlockSpec((1, TT, 512), cur(PF_QK)), pl.BlockSpec((1, 8, 512), prv(PF_QK)),
                  pl.BlockSpec((1, 8, 512), nxt(PF_QK)),
                  pl.BlockSpec((1, TT, 768), cur(PF_HY)), pl.BlockSpec((1, 8, 768), prv(PF_HY)),
                  pl.BlockSpec((1, 8, 768), nxt(PF_HY)),
                  pl.BlockSpec((1, TT, 256), cur(PF_GM)),
                  pl.BlockSpec((3, 512), const), pl.BlockSpec((3, 768), const), pl.BlockSpec((1, 128), const),
                  pl.BlockSpec((TT, 512), lambda t, b: (t, 0)), pl.BlockSpec((TT, 512), lambda t, b: (t, 0)),
                  pl.BlockSpec((TT, TT), const), pl.BlockSpec((TT, TT), const)],
        out_specs=[pl.BlockSpec((1, TT, 512), lambda t, b: (b, t, 0)),
                   pl.BlockSpec((1, TT, 768), lambda t, b: (b, t, 0)),
                   pl.BlockSpec((1, 2, TT, 256), lambda t, b: (b, 0, t, 0))],
        out_shape=[jax.ShapeDtypeStruct((nb, ltot, 512), BF16),
                   jax.ShapeDtypeStruct((nb, ltot, 768), F32),
                   jax.ShapeDtypeStruct((nb, 2, ltot, 256), F32)],
        compiler_params=_params(("arbitrary", "arbitrary")), name="conv_rope_gates",
    )(pf, pf, pf, pf, pf, pf, pf, conv_ml, conv_hy, bif_row, cos_t, sin_t, tril, triu)


def _na_kernel(q_ref, k0_ref, k1_ref, k2_ref, v0_ref, v1_ref, v2_ref, kc_ref, vc_ref, za_ref, tab_ref, o_ref):
    lane = lax.broadcasted_iota(jnp.int32, (1, 128), 1)
    for hp in range(H_NA // 2):
        sl = slice(hp * 128, (hp + 1) * 128)
        q2 = q_ref[0, :, sl]
        kl = jnp.concatenate([k0_ref[0, :, sl], k1_ref[0, :, sl], k2_ref[0, :, sl]], axis=0)
        vl = jnp.concatenate([v0_ref[0, :, sl], v1_ref[0, :, sl], v2_ref[0, :, sl]], axis=0)
        kc = kc_ref[0, :, sl]
        vc = vc_ref[0, :, sl]
        acc = None
        for e in range(2):
            hm = (lane >= HEAD_DIM) if e else (lane < HEAD_DIM)
            qm = jnp.where(hm, q2, jnp.zeros_like(q2))
            s_lat = _dot_nt(qm, kl) + tab_ref[0, 2 * hp + e]
            s_ctx = _dot_nt(qm, kc)
            m = jnp.maximum(jnp.max(s_lat, axis=-1, keepdims=True), jnp.max(s_ctx, axis=-1, keepdims=True))
            p_lat = jnp.exp(s_lat - m)
            p_ctx = jnp.exp(s_ctx - m)
            den = jnp.sum(p_lat, axis=-1, keepdims=True) + jnp.sum(p_ctx, axis=-1, keepdims=True)
            o = (_dot(p_lat.astype(BF16), vl) + _dot(p_ctx.astype(BF16), vc)) / den
            acc = o if e == 0 else jnp.where(hm, o, acc)
        o_ref[0, :, sl] = (acc * _silu(za_ref[0, :, sl])).astype(BF16)


def _attention(ona, pf, table, nt, with_ctx_queries):
    nb, ltot, _ = ona.shape
    nj = nt + 1 if with_ctx_queries else nt

    def kv(col, dj):
        return lambda j, b: (b, jnp.clip(j + dj, 0, nt - 1), col)

    def pat(j, b):
        p = jnp.where(j == 0, 0, jnp.where(j == nt - 1, 2, jnp.where(j == nt, 3, 1)))
        return (p, 0, 0, 0)

    return pl.pallas_call(
        _na_kernel, grid=(nj, nb),
        in_specs=[pl.BlockSpec((1, TT, 512), lambda j, b: (b, j, 0)),
                  pl.BlockSpec((1, TT, 512), kv(1, -1)), pl.BlockSpec((1, TT, 512), kv(1, 0)),
                  pl.BlockSpec((1, TT, 512), kv(1, 1)),
                  pl.BlockSpec((1, TT, 512), kv(2, -1)), pl.BlockSpec((1, TT, 512), kv(2, 0)),
                  pl.BlockSpec((1, TT, 512), kv(2, 1)),
                  pl.BlockSpec((1, TT, 512), lambda j, b: (b, nt, 1)),
                  pl.BlockSpec((1, TT, 512), lambda j, b: (b, nt, 2)),
                  pl.BlockSpec((1, TT, 512), lambda j, b: (b, j, PF_ZA)),
                  pl.BlockSpec((1, H_NA, TT, 3 * TT), pat)],
        out_specs=pl.BlockSpec((1, TT, 512), lambda j, b: (b, j, 0)),
        out_shape=jax.ShapeDtypeStruct((nb, ltot, 512), BF16),
        compiler_params=_params(("arbitrary", "arbitrary")), name="nbr_attention",
    )(ona, ona, ona, ona, ona, ona, ona, ona, ona, pf, table)


def _na_bias_table(rpb):
    a = np.arange(TT) // GRID_W
    qc = np.arange(TT) % GRID_W
    kidx = np.arange(3 * TT)
    kr = (kidx // TT - 1) * BAND_ROWS + (kidx % TT) // GRID_W
    kcol = kidx % GRID_W
    qstart = np.clip(qc - NA_COLS // 2, 0, GRID_W - NA_COLS)
    col_ok = (kcol[None, :] >= qstart[:, None]) & (kcol[None, :] < qstart[:, None] + NA_COLS)
    coff = np.clip(kcol[None, :] - qc[:, None], -(NA_COLS - 1), NA_COLS - 1) + NA_COLS - 1
    roff = np.clip(kr[None, :] - a[:, None] + NA_ROWS - 1, 0, 2 * NA_ROWS - 2)
    half = NA_ROWS // 2
    row_ok = [
        (kr[None, :] >= 0) & (kr[None, :] < NA_ROWS) & (a[:, None] >= 0),
        (kr[None, :] >= a[:, None] - half) & (kr[None, :] < a[:, None] + half),
        (kr[None, :] >= BAND_ROWS - NA_ROWS) & (kr[None, :] < BAND_ROWS) & (a[:, None] >= 0),
    ]
    nr, ncol = 2 * NA_ROWS - 1, 2 * NA_COLS - 1
    onehot = (coff[:GRID_W, :GRID_W].reshape(1, -1) == np.arange(ncol)[:, None]).astype(np.float32)
    cmat = jnp.dot(rpb.astype(F32).reshape(H_NA * nr, ncol), onehot,
                   precision=lax.Precision.HIGHEST).reshape(H_NA, nr, GRID_W, GRID_W)
    nkr = 3 * BAND_ROWS
    blocks = [jnp.concatenate([cmat[:, roff[ai * GRID_W, kj * GRID_W]] for kj in range(nkr)], axis=-1)
              for ai in range(BAND_ROWS)]
    bias = jnp.concatenate(blocks, axis=1)
    tabs = [jnp.where((r & col_ok)[None], bias, NEG) for r in row_ok]
    tabs.append(jnp.full_like(bias, NEG))
    return jnp.stack(tabs, axis=0)


def _head_lane_sums(x, lane_h):
    out = jnp.zeros_like(x)
    half = lax.broadcasted_iota(jnp.int32, (1, 128), 1) // HEAD_DIM
    for h in range(H_ML):
        t = x[:, (h // 2) * 128:(h // 2 + 1) * 128]
        sm = jnp.sum(jnp.where(half == h % 2, t, 0.0), axis=-1, keepdims=True)
        out = jnp.where(lane_h == h, sm, out)
    return out


def _scan_kernel(q_ref, k_ref, v_ref, g_ref, e_ref, bm_ref, o_ref, w_ref, n_ref, m_ref):
    s = pl.program_id(2)
    sgn = 1 - 2 * pl.program_id(1)

    @pl.when(s == 0)
    def _():
        w_ref[...] = jnp.zeros_like(w_ref)
        n_ref[...] = jnp.zeros_like(n_ref)
        m_ref[...] = jnp.zeros_like(m_ref)

    a = g_ref[0, 0, :, 0:128]
    bb = g_ref[0, 0, :, 128:256]
    m_prev = m_ref[...]
    mc = jnp.maximum(m_prev, jnp.max(a, axis=0, keepdims=True))
    small = jnp.concatenate([jnp.exp(a - mc), jnp.exp(-bb - mc),
                             jnp.broadcast_to(jnp.exp(m_prev - mc), (8, 128))], axis=0)
    sh = small.astype(BF16)
    sl = (small - sh.astype(F32)).astype(BF16)
    ex = _dot(sh, e_ref[...]) + _dot(sl, e_ref[...])
    w_e = ex[0:TT]
    eb_e = ex[TT:2 * TT]
    beta = ex[2 * TT:2 * TT + 1]

    q = q_ref[0]
    kwf = k_ref[0].astype(F32) * w_e
    kw = kwf.astype(BF16)
    vb = v_ref[0].astype(BF16)
    lane_h = lax.broadcasted_iota(jnp.int32, (1, D_ML), 1) // HEAD_DIM
    ri = lax.broadcasted_iota(jnp.int32, (TT, TT), 0)
    ci = lax.broadcasted_iota(jnp.int32, (TT, TT), 1)
    tri = ((ci - ri) * sgn) <= 0

    intra = None
    den_intra = jnp.zeros((TT, D_ML), F32)
    for h in range(H_ML):
        qh = jnp.where(lane_h == h, q, jnp.zeros_like(q))
        sc = jnp.where(tri, _dot_nt(qh, kw), 0.0)
        den_intra = jnp.where(lane_h == h, jnp.sum(sc, axis=-1, keepdims=True), den_intra)
        vh = jnp.where(lane_h == h, vb, jnp.zeros_like(vb))
        part = _dot(sc.astype(BF16), vh)
        intra = part if intra is None else intra + part
    num = beta * _dot(q, w_ref[...].astype(BF16)) + intra
    den = beta * _head_lane_sums(q.astype(F32) * n_ref[...], lane_h) + den_intra
    o_ref[0, 0] = num / jnp.maximum(jnp.abs(den), eb_e)

    w_ref[...] = beta * w_ref[...] + bm_ref[...] * _dot_tn(kw, vb)
    n_ref[...] = beta * n_ref[...] + jnp.sum(kwf, axis=0, keepdims=True)
    m_ref[...] = jnp.min(bb, axis=0, keepdims=True) + mc


def _mlstm_scan(qk, pf, gates, emat, bmask, nt):
    nb, ltot, _ = qk.shape

    def chunk(s, d):
        return jnp.where(s == 0, nt, jnp.where(d == 0, s - 1, nt - s))

    const = lambda b, d, s: (0, 0)
    return pl.pallas_call(
        _scan_kernel, grid=(nb, 2, nt + 1),
        in_specs=[pl.BlockSpec((1, TT, 256), lambda b, d, s: (b, chunk(s, d), 0)),
                  pl.BlockSpec((1, TT, 256), lambda b, d, s: (b, chunk(s, d), 1)),
                  pl.BlockSpec((1, TT, 256), lambda b, d, s: (b, chunk(s, d), PF_VM)),
                  pl.BlockSpec((1, 1, TT, 256), lambda b, d, s: (b, d, chunk(s, d), 0)),
                  pl.BlockSpec((128, 256), const), pl.BlockSpec((256, 256), const)],
        out_specs=pl.BlockSpec((1, 1, TT, 256), lambda b, d, s: (b, d, chunk(s, d), 0)),
        out_shape=jax.ShapeDtypeStruct((nb, 2, ltot, 256), F32),
        scratch_shapes=[pltpu.VMEM((D_ML, D_ML), F32), pltpu.VMEM((1, D_ML), F32), pltpu.VMEM((1, 128), F32)],
        compiler_params=_params(("arbitrary", "arbitrary", "arbitrary")), name="mlstm_scan",
    )(qk, qk, pf, gates, emat, bmask)


def _hyfilt_kernel(tl, feat_ref, w1_ref, b1_ref, w2_ref, b2_ref, w3_ref, fq_ref, dec_ref, o_ref, ss_ref):
    i = pl.program_id(0)
    a = jnp.sin(fq_ref[0:1] * (_dot_f32(feat_ref[...], w1_ref[...]) + b1_ref[...]))
    a = jnp.sin(fq_ref[1:2] * (_dot_f32(a, w2_ref[...]) + b2_ref[...]))
    dec = dec_ref[...]
    f = _dot_f32(a, w3_ref[...]) * jnp.concatenate([dec, dec, dec, dec], axis=1)
    o_ref[...] = f
    row = lax.broadcasted_iota(jnp.int32, (tl, 1), 0) + i * tl
    col = lax.broadcasted_iota(jnp.int32, (1, 4 * D_HY), 1)
    drop = jnp.logical_and(row == 0, (col // D_HY) % 2 == 1)
    part = jnp.sum(jnp.where(drop, 0.0, f * f), axis=0, keepdims=True)

    @pl.when(i == 0)
    def _():
        ss_ref[...] = jnp.zeros_like(ss_ref)

    ss_ref[...] += part


def _hyena_filters(length, w1, b1, w2, b2, w3, freq):
    t = np.arange(length, dtype=np.float32)
    tn = (t / np.float32(length - 1)).astype(np.float32)
    bands = (HY_EMB - 1) // 2
    fr = np.linspace(1e-4, bands - 1, bands, dtype=np.float32)
    ang = (np.float32(2.0 * math.pi / length) * t[:, None] * fr[None, :]).astype(np.float32)
    feat = np.zeros((length, 128), np.float32)
    feat[:, 0] = tn
    feat[:, 1:1 + bands] = np.cos(ang)
    feat[:, 1 + bands:1 + 2 * bands] = -np.sin(ang)
    deltas = np.abs(np.linspace(math.log(HY_TARGET) / HY_SLOW, math.log(HY_TARGET) / HY_FAST, D_HY, dtype=np.float32))
    dec = np.exp(-tn[:, None] * deltas[None, :]).astype(np.float32)

    pad2 = lambda m, r, c: jnp.zeros((r, c), F32).at[:m.shape[0], :m.shape[1]].set(m.astype(F32))
    w1p = pad2(w1, 128, 128)
    w2p = pad2(w2, 128, 128)
    w3p = pad2(w3, 128, 4 * D_HY)
    b1p = pad2(b1[None], 1, 128)
    b2p = pad2(b2[None], 1, 128)
    fqp = pad2(freq, 2, 128)
    tl = min(length, 512)
    const = lambda i: (0, 0)
    return pl.pallas_call(
        functools.partial(_hyfilt_kernel, tl), grid=(length // tl,),
        in_specs=[pl.BlockSpec((tl, 128), lambda i: (i, 0)),
                  pl.BlockSpec((128, 128), const), pl.BlockSpec((1, 128), const),
                  pl.BlockSpec((128, 128), const), pl.BlockSpec((1, 128), const),
                  pl.BlockSpec((128, 4 * D_HY), const), pl.BlockSpec((2, 128), const),
                  pl.BlockSpec((tl, D_HY), lambda i: (i, 0))],
        out_specs=[pl.BlockSpec((tl, 4 * D_HY), lambda i: (i, 0)), pl.BlockSpec((1, 4 * D_HY), const)],
        out_shape=[jax.ShapeDtypeStruct((length, 4 * D_HY), F32), jax.ShapeDtypeStruct((1, 4 * D_HY), F32)],
        compiler_params=_params(("arbitrary",)), name="hyena_filter_mlp",
    )(feat, w1p, b1p, w2p, b2p, w3p, fqp, dec)


def _filter_scale(ssq):
    s = ssq.reshape(2, 2, D_HY)
    return lax.rsqrt(s[:, 0] + s[:, 1] + EPS)


@functools.lru_cache(maxsize=None)
def _fft_tables(length):
    n = 2 * length
    n1 = n // FFT_N2
    nh = n1 // 2
    k1 = np.arange(n1)[None, :, None].astype(np.float64)
    j = np.arange(nh)[None, None, :].astype(np.float64)
    n2 = np.arange(FFT_N2)[:, None, None].astype(np.float64)
    ang = -2.0 * np.pi * k1 * (FFT_N2 * j + n2) / n
    gf = np.concatenate([np.cos(ang), np.sin(ang)], axis=1)
    gi = np.concatenate([np.cos(ang).transpose(0, 2, 1), np.sin(ang).transpose(0, 2, 1)], axis=1)
    kk = np.arange(FFT_N2)[:, None] * np.arange(FFT_N2)[None, :]
    a2 = -2.0 * np.pi * kk / FFT_N2
    f2 = np.concatenate([np.cos(a2), np.sin(a2)], axis=0)
    cv = lambda m: np.asarray(m, np.float32).astype(BF16)
    return cv(gf), cv(gi), cv(f2), n1


def _stage1(n1, z_of, gf_ref, scr_ref, cols):
    nh = n1 // 2

    def body(n2, c):
        r = _dot(gf_ref[n2], z_of(n2))
        off = pl.multiple_of(n2 * n1, n1)
        if cols == 256:
            scr_ref[0, pl.ds(off, n1), :] = r[:n1, :128] - r[n1:, 128:]
            scr_ref[1, pl.ds(off, n1), :] = r[n1:, :128] + r[:n1, 128:]
        else:
            scr_ref[0, pl.ds(off, n1), :] = r[:n1]
            scr_ref[1, pl.ds(off, n1), :] = r[n1:]
        return c

    lax.fori_loop(0, FFT_N2, body, 0, unroll=FFT_UNROLL)
    del nh


def _stage2(n1, k1, f2_ref, scr_ref):
    ar = scr_ref[0, pl.ds(k1, FFT_N2, stride=n1), :]
    ai = scr_ref[1, pl.ds(k1, FFT_N2, stride=n1), :]
    r = _dot(f2_ref[...], jnp.concatenate([ar, ai], axis=1).astype(BF16))
    return r[:128, :128] - r[128:, 128:], r[128:, :128] + r[:128, 128:]


def _hyconv_kernel(n1, kg, z_ref, gf_ref, f2_ref, kf_ref, gi_ref, o_ref, scr_ref):
    s = pl.program_id(2)
    nh = n1 // 2

    @pl.when(s == 0)
    def _():
        def z_of(n2):
            zr = z_ref[0, 0, pl.ds(n2, nh, stride=FFT_N2), :]
            zi = z_ref[0, 1, pl.ds(n2, nh, stride=FFT_N2), :]
            return jnp.concatenate([zr, zi], axis=1).astype(BF16)
        _stage1(n1, z_of, gf_ref, scr_ref, 256)

    def k1_body(i, c):
        k1 = s * kg + i
        xr, xi = _stage2(n1, k1, f2_ref, scr_ref)
        off = pl.multiple_of(i * FFT_N2, FFT_N2)
        kr = kf_ref[0, pl.ds(off, FFT_N2), :]
        ki = kf_ref[1, pl.ds(off, FFT_N2), :]
        yc = jnp.concatenate([xr * kr - xi * ki, xr * ki + xi * kr], axis=1).astype(BF16)
        r = _dot(f2_ref[...], yc)
        scr_ref[0, pl.ds(k1, FFT_N2, stride=n1), :] = r[:128, :128] + r[128:, 128:]
        scr_ref[1, pl.ds(k1, FFT_N2, stride=n1), :] = r[:128, 128:] - r[128:, :128]
        return c

    lax.fori_loop(0, kg, k1_body, 0, unroll=FFT_UNROLL)

    @pl.when(s == pl.num_programs(2) - 1)
    def _():
        def body(n2, c):
            off = pl.multiple_of(n2 * n1, n1)
            bc = jnp.concatenate([scr_ref[0, pl.ds(off, n1), :], scr_ref[1, pl.ds(off, n1), :]], axis=1)
            r = _dot(gi_ref[n2], bc.astype(BF16))
            o_ref[0, 0, pl.ds(n2, nh, stride=FFT_N2), :] = r[:nh, :128] + r[nh:, 128:]
            o_ref[0, 1, pl.ds(n2, nh, stride=FFT_N2), :] = r[:nh, 128:] - r[nh:, :128]
            return c
        lax.fori_loop(0, FFT_N2, body, 0, unroll=FFT_UNROLL)


def _hyena_conv(z4, col0, kf, length):
    gf, gi, f2, n1 = _fft_tables(length)
    n = 2 * length
    kg = min(n1, 16)
    npair = z4.shape[0]
    big = pl.Buffered(1)
    return pl.pallas_call(
        functools.partial(_hyconv_kernel, n1, kg), grid=(npair, 2, n1 // kg),
        in_specs=[pl.BlockSpec((1, 2, length, 128), lambda p, c, s: (p, 0, 0, col0 + c), pipeline_mode=big),
                  pl.BlockSpec(gf.shape, lambda p, c, s: (0, 0, 0), pipeline_mode=big),
                  pl.BlockSpec(f2.shape, lambda p, c, s: (0, 0)),
                  pl.BlockSpec((2, kg * FFT_N2, 128), lambda p, c, s: (0, s, c)),
                  pl.BlockSpec(gi.shape, lambda p, c, s: (0, 0, 0), pipeline_mode=big)],
        out_specs=pl.BlockSpec((1, 2, length, 128), lambda p, c, s: (p, 0, 0, c), pipeline_mode=big),
        out_shape=jax.ShapeDtypeStruct((npair, 2, length, 256), F32),
        scratch_shapes=[pltpu.VMEM((2, n, 128), F32)],
        compiler_params=_params(("arbitrary", "arbitrary", "arbitrary")), name="hyena_fftconv",
    )(z4, gf, f2, kf, gi)


def _hyfft_kernel(n1, kg, z_ref, gf_ref, f2_ref, o_ref, scr_ref):
    s = pl.program_id(2)

    @pl.when(s == 0)
    def _():
        _stage1(n1, lambda n2: z_ref[0, pl.ds(n2, n1, stride=FFT_N2), :].astype(BF16), gf_ref, scr_ref, 128)

    def k1_body(i, c):
        xr, xi = _stage2(n1, s * kg + i, f2_ref, scr_ref)
        off = pl.multiple_of(i * FFT_N2, FFT_N2)
        o_ref[0, 0, pl.ds(off, FFT_N2), :] = xr
        o_ref[0, 1, pl.ds(off, FFT_N2), :] = xi
        return c

    lax.fori_loop(0, kg, k1_body, 0, unroll=FFT_UNROLL)


@functools.lru_cache(maxsize=None)
def _fft_table_full(length):
    n = 2 * length
    n1 = n // FFT_N2
    k1 = np.arange(n1)[None, :, None].astype(np.float64)
    j = np.arange(n1)[None, None, :].astype(np.float64)
    n2 = np.arange(FFT_N2)[:, None, None].astype(np.float64)
    ang = -2.0 * np.pi * k1 * (FFT_N2 * j + n2) / n
    return np.concatenate([np.cos(ang), np.sin(ang)], axis=1).astype(np.float32).astype(BF16)


def _filter_spectrum(k2, length):
    _, _, f2, n1 = _fft_tables(length)
    gf = _fft_table_full(length)
    n = 2 * length
    kg = min(n1, 16)
    nf = k2.shape[0]
    big = pl.Buffered(1)
    return pl.pallas_call(
        functools.partial(_hyfft_kernel, n1, kg), grid=(nf, 2, n1 // kg),
        in_specs=[pl.BlockSpec((1, n, 128), lambda f, c, s: (f, 0, c), pipeline_mode=big),
                  pl.BlockSpec(gf.shape, lambda f, c, s: (0, 0, 0), pipeline_mode=big),
                  pl.BlockSpec(f2.shape, lambda f, c, s: (0, 0))],
        out_specs=pl.BlockSpec((1, 2, kg * FFT_N2, 128), lambda f, c, s: (f, 0, s, c)),
        out_shape=jax.ShapeDtypeStruct((nf, 2, n, 256), F32),
        scratch_shapes=[pltpu.VMEM((2, n, 128), F32)],
        compiler_params=_params(("arbitrary", "arbitrary", "arbitrary")), name="hyena_filter_fft",
    )(k2, gf, f2)


def _latent_filter_spectra(filt, scale, length):
    f4 = filt.reshape(length, 2, 2, D_HY) * (scale / (2 * length))[None, :, None, :]
    k2 = jnp.concatenate([f4[:, :, 0], jnp.zeros((1, 2, D_HY), F32), jnp.flip(f4[1:, :, 1], axis=0)], axis=0)
    return _filter_spectrum(k2.transpose(1, 0, 2), length)


def _hy_mid_kernel(y_ref, hv_ref, b_ref, o_ref):
    v = hv_ref[0][:, 0:D_HY]
    x1 = hv_ref[0][:, D_HY:2 * D_HY]
    o_ref[0] = x1 * (y_ref[0] + v * b_ref[...])


def _hy_mid(y1, hv, bias0):
    nb, length, _ = y1.shape
    te = 4 * TT
    return pl.pallas_call(
        _hy_mid_kernel, grid=(nb, length // te),
        in_specs=[pl.BlockSpec((1, te, D_HY), lambda b, t: (b, t, 0)),
                  pl.BlockSpec((1, te, 512), lambda b, t: (b, t, 0)),
                  pl.BlockSpec((1, D_HY), lambda b, t: (0, 0))],
        out_specs=pl.BlockSpec((1, te, D_HY), lambda b, t: (b, t, 0)),
        out_shape=jax.ShapeDtypeStruct(y1.shape, F32),
        compiler_params=_params(("arbitrary", "arbitrary")), name="hyena_mid",
    )(y1, hv, bias0.reshape(1, D_HY))


def _hy_ctx_kernel(lc, hv_ref, zh_ref, k2_ref, b_ref, fc_ref, ff_ref, ci_ref, o_ref):
    nf = 2 * lc
    hv = hv_ref[0]

    def conv(z, o):
        kf = _dot_exact_rhs(ff_ref[...], k2_ref[o])
        zs = _dot(fc_ref[...], z.astype(BF16))
        zr, zi = zs[:nf], zs[nf:]
        kr, ki = kf[:nf], kf[nf:]
        ys = jnp.concatenate([zr * kr - zi * ki, zr * ki + zi * kr], axis=0).astype(BF16)
        return _dot(ci_ref[...], ys) * (1.0 / nf)

    v = hv[:, 0:D_HY]
    x1 = hv[:, D_HY:2 * D_HY]
    x2 = hv[:, 2 * D_HY:3 * D_HY]
    z1 = x1 * (conv(v, 0) + v * b_ref[0:1])
    p = x2 * (conv(z1, 1) + z1 * b_ref[1:2])
    o_ref[0] = (p * _silu(zh_ref[0])).astype(BF16)


def _dot_exact_rhs(a_bf16, b):
    bh, bm, bl = _split3(b)
    return _dot(a_bf16, bh) + _dot(a_bf16, bm) + _dot(a_bf16, bl)


@functools.lru_cache(maxsize=None)
def _ctx_dft_tables(lc):
    nf = 2 * lc
    k = np.arange(nf)[:, None].astype(np.float64)
    ang = -2.0 * np.pi * k * np.arange(nf)[None, :] / nf
    ff = np.concatenate([np.cos(ang), np.sin(ang)], axis=0)
    fc = ff[:, :lc]
    a2 = 2.0 * np.pi * np.arange(lc)[:, None] * np.arange(nf)[None, :] / nf
    ci = np.concatenate([np.cos(a2), -np.sin(a2)], axis=1)
    cv = lambda m: np.asarray(m, np.float32).astype(BF16)
    return cv(fc), cv(ff), cv(ci)


def _hy_ctx(hv, pf, k2c, hy_bias, nt, lc):
    nb = hv.shape[0]
    fc, ff, ci = _ctx_dft_tables(lc)
    c2 = lambda b: (0, 0)
    return pl.pallas_call(
        functools.partial(_hy_ctx_kernel, lc), grid=(nb,),
        in_specs=[pl.BlockSpec((1, lc, 768), lambda b: (b, nt, 0)),
                  pl.BlockSpec((1, lc, D_HY), lambda b: (b, nt, PF_ZH)),
                  pl.BlockSpec(k2c.shape, lambda b: (0, 0, 0)),
                  pl.BlockSpec((2, D_HY), c2),
                  pl.BlockSpec(fc.shape, c2), pl.BlockSpec(ff.shape, c2), pl.BlockSpec(ci.shape, c2)],
        out_specs=pl.BlockSpec((1, lc, D_HY), lambda b: (b, 0, 0)),
        out_shape=jax.ShapeDtypeStruct((nb, lc, D_HY), BF16),
        compiler_params=_params(("arbitrary",)), name="hyena_ctx",
    )(hv, pf, k2c, hy_bias, fc, ff, ci)


def _out_kernel(n_hy, na_ref, h0_ref, h1_ref, om_ref, zm_ref, w_ref, x_ref, mod_ref, g_ref, *rest):
    hy_refs, o_ref = rest[:n_hy], rest[n_hy]
    if n_hy == 1:
        hy = hy_refs[0][0]
    else:
        y_ref, z_ref, x2_ref, zh_ref, b_ref = hy_refs
        z = z_ref[0]
        hy = (x2_ref[0] * (y_ref[0] + z * b_ref[...]) * _silu(zh_ref[0])).astype(BF16)
    ml = jax.nn.sigmoid(om_ref[0]) * (h0_ref[0, 0] + h1_ref[0, 0]) * _silu(zm_ref[0])
    y = (_dot(na_ref[0], w_ref[0:D_NA]) + _dot(ml.astype(BF16), w_ref[D_NA:D_NA + D_ML])
         + _dot(hy, w_ref[D_NA + D_ML:D_MODEL]))
    ms = jnp.mean(y * y, axis=-1, keepdims=True)
    r = y * lax.rsqrt(ms + EPS) * g_ref[...]
    o_ref[0] = x_ref[0] + mod_ref[0][2:3] * r


def _out_projection(na_g, hml, pf, hy_args, w_out, xc, mod3, g_post, to, t0, ntiles, mod_row, out_rows, prev=None):
    nb = xc.shape[0]
    d = D_MODEL
    tok = lambda col: (lambda b, t: (b, t + t0, col))
    own = lambda col: (lambda b, t: (b, t, col))
    in_specs = [pl.BlockSpec((1, to, 512), tok(0)),
                pl.BlockSpec((1, 1, to, 256), lambda b, t: (b, 0, t + t0, 0)),
                pl.BlockSpec((1, 1, to, 256), lambda b, t: (b, 1, t + t0, 0)),
                pl.BlockSpec((1, to, 256), tok(PF_OM)), pl.BlockSpec((1, to, 256), tok(PF_ZM)),
                pl.BlockSpec((d, d), lambda b, t: (0, 0), pipeline_mode=pl.Buffered(1)),
                pl.BlockSpec((1, to, d), tok(0)),
                pl.BlockSpec((1, 3, d), lambda b, t: (mod_row(b), 0, 0)),
                pl.BlockSpec((1, d), lambda b, t: (0, 0))]
    args = [na_g, hml, hml, pf, pf, w_out, xc, mod3, g_post.reshape(1, d)]
    if len(hy_args) == 1:
        in_specs.append(pl.BlockSpec((1, to, D_HY), own(0)))
        args.append(hy_args[0])
    else:
        y2, z1, hv, pf_, bias1 = hy_args
        in_specs += [pl.BlockSpec((1, to, D_HY), own(0)), pl.BlockSpec((1, to, D_HY), own(0)),
                     pl.BlockSpec((1, to, D_HY), tok(2)), pl.BlockSpec((1, to, D_HY), tok(PF_ZH)),
                     pl.BlockSpec((1, D_HY), lambda b, t: (0, 0))]
        args += [y2, z1, hv, pf_, bias1.reshape(1, D_HY)]
    n_hy = len(hy_args)
    aliases = {}
    if prev is not None:
        in_specs.append(pl.BlockSpec(memory_space=pl.ANY))
        args.append(prev)
        aliases = {len(args) - 1: 0}
        kern = lambda *r: _out_kernel(n_hy, *r[:9 + n_hy], r[10 + n_hy])
    else:
        kern = functools.partial(_out_kernel, n_hy)
    return pl.pallas_call(
        kern, grid=(nb, ntiles), in_specs=in_specs,
        out_specs=pl.BlockSpec((1, to, d), tok(0)),
        out_shape=jax.ShapeDtypeStruct((nb, out_rows, d), F32),
        input_output_aliases=aliases,
        compiler_params=_params(("arbitrary", "arbitrary")), name="out_proj_norm_residual",
    )(*args)


@functools.lru_cache(maxsize=None)
def _rope_tables(length, lc):
    n = HEAD_DIM // 4
    t = np.arange(length)
    inv = (ROPE_BASE ** (-np.arange(n, dtype=np.float32) / n)).astype(np.float32)
    pos = np.stack([t // GRID_W, t % GRID_W], axis=-1).astype(np.float32)
    ang = (pos[:, :, None] * inv).astype(np.float32)
    cos_h = np.concatenate([np.cos(ang[:, 0]), np.cos(ang[:, 0]), np.cos(ang[:, 1]), np.cos(ang[:, 1])], axis=-1)
    sin_h = np.concatenate([-np.sin(ang[:, 0]), np.sin(ang[:, 0]), -np.sin(ang[:, 1]), np.sin(ang[:, 1])], axis=-1)
    cos_t = np.concatenate([np.tile(cos_h, (1, 2 * H_ML)), np.ones((lc, 512), np.float32)], axis=0)
    sin_t = np.concatenate([np.tile(sin_h, (1, 2 * H_ML)), np.zeros((lc, 512), np.float32)], axis=0)
    kscale = np.concatenate([np.ones(256, np.float32), np.full(256, HEAD_DIM ** -0.5, np.float32)])
    return (cos_t * kscale).astype(np.float32), (sin_t * kscale).astype(np.float32)


@functools.lru_cache(maxsize=None)
def _scan_constants():
    tril = np.tril(np.ones((TT, TT), np.float32))
    emat = np.zeros((128, D_ML), np.float32)
    for h in range(H_ML):
        emat[h, h * HEAD_DIM:(h + 1) * HEAD_DIM] = 1.0
    rh = np.arange(D_ML)[:, None] // HEAD_DIM
    chd = np.arange(D_ML)[None, :] // HEAD_DIM
    bmask = (rh == chd).astype(np.float32)
    return tril.astype(BF16), np.ascontiguousarray(tril.T).astype(BF16), emat.astype(BF16), bmask


def _layer(xc, mod3, nb, length, lc, g_pre, g_post, w_in, b_if, conv_ml, conv_hy, rpb,
           hf_w1, hf_b1, hf_w2, hf_b2, hf_w3, hf_freq, hy_bias, w_out, update_ctx):
    nt = length // TT
    ltot = length + lc
    tril, triu, emat, bmask = _scan_constants()
    cos_t, sin_t = _rope_tables(length, lc)

    c = lambda a, b: w_in[:, a:b]
    wna = jnp.concatenate([c(0, 512) * (HEAD_DIM ** -0.5), c(512, 1536)], axis=1).astype(BF16)
    wpf = jnp.concatenate([c(1536, 2048), c(2048, 2560), c(2560, 2816), c(2816, 3072), c(3072, 3328),
                           c(4112, 4368), c(3328, 3344), jnp.zeros((D_MODEL, 240), F32), c(3344, 4112)],
                          axis=1).astype(BF16)
    ona, pf = _projection(xc, mod3, g_pre, wna, wpf, nt)

    na_g = _attention(ona, pf, _na_bias_table(rpb), nt, update_ctx)

    bif_row = jnp.zeros((1, 128), F32).at[0, :4 * H_ML].set(b_if.reshape(-1))
    qk, hv, gates = _pre(pf, conv_ml, conv_hy, bif_row, cos_t, sin_t, tril, triu, nt)
    hml = _mlstm_scan(qk, pf, gates, emat, bmask, nt)

    filt, ssq = _hyena_filters(length, hf_w1, hf_b1, hf_w2, hf_b2, hf_w3, hf_freq)
    kf = _latent_filter_spectra(filt, _filter_scale(ssq), length)
    hv4 = hv.reshape(nb // 2, 2, ltot, 768)
    y1 = _hyena_conv(hv4, 0, kf[0], length).reshape(nb, length, D_HY)
    z1 = _hy_mid(y1, hv, hy_bias[0])
    y2 = _hyena_conv(z1.reshape(nb // 2, 2, length, D_HY), 0, kf[1], length).reshape(nb, length, D_HY)
    hy_args = (y2, z1, hv, pf, hy_bias[1])
    to = 2 * TT
    if not update_ctx:
        return _out_projection(na_g, hml, pf, hy_args, w_out, xc, mod3, g_post, to, 0, length // to,
                               lambda b: b, length)

    xn = _out_projection(na_g, hml, pf, hy_args, w_out, xc, mod3, g_post, to, 0, length // to, lambda b: b, ltot)
    filt_c, ssq_c = _hyena_filters(lc, hf_w1, hf_b1, hf_w2, hf_b2, hf_w3, hf_freq)
    fc4 = (filt_c.reshape(lc, 2, 2, D_HY) * _filter_scale(ssq_c)[None, :, None, :])
    k2c = jnp.concatenate([fc4[:, :, 0], jnp.zeros((1, 2, D_HY), F32), jnp.flip(fc4[1:, :, 1], axis=0)], axis=0)
    hyc_g = _hy_ctx(hv, pf, k2c.transpose(1, 0, 2), hy_bias, nt, lc)
    return _out_projection(na_g, hml, pf, (hyc_g,), w_out, xc, mod3, g_post, TT, nt, 1, lambda b: nb, ltot, prev=xn)


def kernel(x, c, ctx, c_ctx, w_ada, b_ada, g_pre, g_post, w_in, b_if, conv_ml, conv_hy, rpb, hf_w1, hf_b1,
           hf_w2, hf_b2, hf_w3, hf_freq, hy_bias, w_out):
    nb, length, d = x.shape
    lc = ctx.shape[1]
    depth = w_in.shape[0]
    assert lc == TT and length % (4 * TT) == 0 and nb % 2 == 0 and d == D_MODEL
    xc = jnp.concatenate([x, ctx], axis=1)
    cc = jnp.zeros((16, d), F32).at[:nb].set(c).at[nb].set(c_ctx)
    for l in range(depth):
        mod3 = _modulation(cc, w_ada[l], b_ada[l]).reshape(16, 3, d)
        xc = _layer(xc, mod3, nb, length, lc, g_pre[l], g_post[l], w_in[l], b_if[l], conv_ml[l], conv_hy[l],
                    rpb[l], hf_w1[l], hf_b1[l], hf_w2[l], hf_b2[l], hf_w3[l], hf_freq[l], hy_bias[l],
                    w_out[l].astype(BF16), l < depth - 1)
    return xc
```

```python
import functools
import math

import numpy as np
import jax
import jax.numpy as jnp
from jax import lax
from jax.experimental import pallas as pl
from jax.experimental.pallas import tpu as pltpu

F32 = jnp.float32
BF16 = jnp.bfloat16

D_MODEL = 1024
GRID_W = 64
HEAD_DIM = 64
D_NA = 512
D_ML = 256
D_HY = 256
H_NA = 8
H_ML = 4
NA_ROWS = 8
NA_COLS = 16
HY_EMB = 33
HY_FFN = 64
HY_TARGET = 1e-2
HY_FAST = 0.3
HY_SLOW = 1.5
ROPE_BASE = 10000.0
EPS = 1e-6

TT = 256
BAND_ROWS = TT // GRID_W
SCAN_NB = 2
PRE_ROWS = 64
FFT_N2 = 128
FFT_UNROLL = 8
NEG = -1e30
VMEM_LIMIT = 56 * 1024 * 1024

PF_ZA = 0
PF_QK = 1
PF_VM, PF_OM, PF_ZM, PF_ZH, PF_GM = 4, 5, 6, 7, 8
PF_HY = 3
PF_COLS = 3072


def _dot(a, b):
    return jnp.dot(a, b, preferred_element_type=F32)


def _dot_nt(a, b):
    return lax.dot_general(a, b, (((1,), (1,)), ((), ())), preferred_element_type=F32)


def _dot_tn(a, b):
    return lax.dot_general(a, b, (((0,), (0,)), ((), ())), preferred_element_type=F32)


def _split3(x):
    h = x.astype(BF16)
    r = x - h.astype(F32)
    m = r.astype(BF16)
    l = (r - m.astype(F32)).astype(BF16)
    return h, m, l


def _dot_f32(a, b):
    ah, am, al = _split3(a)
    bh, bm, bl = _split3(b)
    return (_dot(ah, bh) + (_dot(ah, bm) + _dot(am, bh))
            + (_dot(ah, bl) + _dot(al, bh) + _dot(am, bm)))


def _dot_exact_lhs(a_bf16, b):
    bh, bm, bl = _split3(b)
    return _dot(a_bf16, bh) + _dot(a_bf16, bm) + _dot(a_bf16, bl)


def _silu(x):
    return x * jax.nn.sigmoid(x)


def _params(sem, vmem=VMEM_LIMIT):
    return pltpu.CompilerParams(dimension_semantics=sem, vmem_limit_bytes=vmem)


def _mod_kernel(c_ref, w_ref, b_ref, o_ref):
    o_ref[...] = _dot_f32(_silu(c_ref[...]), w_ref[...]) + b_ref[...]


def _modulation(cc, w_ada, b_ada):
    n = w_ada.shape[1]
    bn = 512
    return pl.pallas_call(
        _mod_kernel, grid=(n // bn,),
        in_specs=[pl.BlockSpec(cc.shape, lambda j: (0, 0)),
                  pl.BlockSpec((w_ada.shape[0], bn), lambda j: (0, j)),
                  pl.BlockSpec((1, bn), lambda j: (0, j))],
        out_specs=pl.BlockSpec((cc.shape[0], bn), lambda j: (0, j)),
        out_shape=jax.ShapeDtypeStruct((cc.shape[0], n), F32),
        compiler_params=_params(("arbitrary",)), name="adaln_mod",
    )(cc, w_ada, b_ada.reshape(1, n))


def _proj_kernel(nt, x_ref, c_ref, mod_ref, g_ref, wna_ref, wpf_ref, ona_ref, opf_ref):
    x = jnp.where(pl.program_id(1) == nt, c_ref[0], x_ref[0])
    ms = jnp.mean(x * x, axis=-1, keepdims=True)
    y = x * lax.rsqrt(ms + EPS) * g_ref[...]
    mod = mod_ref[0]
    h = (y * (1.0 + mod[1:2]) + mod[0:1]).astype(BF16)
    ona_ref[0] = _dot(h, wna_ref[...]).astype(BF16)
    opf_ref[0] = _dot(h, wpf_ref[...])


def _projection(x_lat, x_ctx, ctx_tile, mod3, g_pre, wna, wpf, nt):
    nb, _, d = x_lat.shape
    ltot = (nt + 1) * TT
    ctx_row = nb

    def mod_idx(b, t):
        return (jnp.where(t == nt, ctx_row, b), 0, 0)

    return pl.pallas_call(
        functools.partial(_proj_kernel, nt), grid=(nb, nt + 1),
        in_specs=[pl.BlockSpec((1, TT, d), lambda b, t: (b, jnp.minimum(t, nt - 1), 0)),
                  pl.BlockSpec((1, TT, d), lambda b, t: (b, ctx_tile, 0)),
                  pl.BlockSpec((1, 3, d), mod_idx),
                  pl.BlockSpec((1, d), lambda b, t: (0, 0)),
                  pl.BlockSpec(wna.shape, lambda b, t: (0, 0), pipeline_mode=pl.Buffered(1)),
                  pl.BlockSpec(wpf.shape, lambda b, t: (0, 0), pipeline_mode=pl.Buffered(1))],
        out_specs=[pl.BlockSpec((1, TT, wna.shape[1]), lambda b, t: (b, t, 0)),
                   pl.BlockSpec((1, TT, wpf.shape[1]), lambda b, t: (b, t, 0))],
        out_shape=[jax.ShapeDtypeStruct((nb, ltot, wna.shape[1]), BF16),
                   jax.ShapeDtypeStruct((nb, ltot, wpf.shape[1]), F32)],
        compiler_params=_params(("arbitrary", "arbitrary")), name="norm_in_proj",
    )(x_lat, x_ctx, mod3, g_pre.reshape(1, d), wna, wpf)


def _pre_kernel(nt, qk_ref, qkp_ref, qkn_ref, hy_ref, hyp_ref, hyn_ref, gm_ref, cml_ref, chy_ref,
                bif_ref, cos_ref, sin_ref, tril_ref, triu_ref, oqk_ref, ohv_ref, og_ref):
    t = pl.program_id(0)
    pm = jnp.where(jnp.logical_and(t != 0, t != nt), 1.0, 0.0)
    nm = jnp.where(jnp.logical_and(t != nt - 1, t != nt), 1.0, 0.0)
    row8 = lax.broadcasted_iota(jnp.int32, (8, 1), 0)
    nchunk = TT // PRE_ROWS

    def conv3(ref, p_ref, n_ref, w, c):
        r0 = c * PRE_ROWS
        u = ref[0, r0:r0 + PRE_ROWS]
        up = p_ref[0][7:8] * pm if c == 0 else ref[0, r0 - 1:r0]
        un = n_ref[0][0:1] * nm if c == nchunk - 1 else ref[0, r0 + PRE_ROWS:r0 + PRE_ROWS + 1]
        rp = pltpu.roll(u, 1, 0)
        rn = pltpu.roll(u, PRE_ROWS - 1, 0)
        prev = jnp.concatenate([jnp.where(row8 == 0, up, rp[0:8]), rp[8:]], axis=0)
        nxt = jnp.concatenate([rn[:PRE_ROWS - 8], jnp.where(row8 == 7, un, rn[PRE_ROWS - 8:])], axis=0)
        return prev * w[0:1] + u * w[1:2] + nxt * w[2:3]

    lane = lax.broadcasted_iota(jnp.int32, (1, 2 * D_ML), 1)
    first = (lane % 32) < 16
    for c in range(nchunk):
        rows = slice(c * PRE_ROWS, (c + 1) * PRE_ROWS)
        ohv_ref[0, rows] = conv3(hy_ref, hyp_ref, hyn_ref, chy_ref[...], c)
        x = _silu(conv3(qk_ref, qkp_ref, qkn_ref, cml_ref[...], c))
        partner = jnp.where(first, pltpu.roll(x, 2 * D_ML - 16, 1), pltpu.roll(x, 16, 1))
        oqk_ref[0, rows] = (x * cos_ref[rows] + partner * sin_ref[rows]).astype(BF16)

    g = gm_ref[0][:, 0:128] + bif_ref[...]
    gl = lax.broadcasted_iota(jnp.int32, (1, 128), 1)
    is_f = jnp.logical_and((gl % 8) >= 4, gl < 16)
    lf = jnp.where(is_f, jnp.minimum(g, 0.0) - jnp.log1p(jnp.exp(-jnp.abs(g))), 0.0)
    bcum = jnp.where(gl < 8, _dot_exact_lhs(tril_ref[...], lf), _dot_exact_lhs(triu_ref[...], lf))
    bi = pltpu.roll(bcum, 128 - H_ML, 1)
    a = g - bi
    keep = gl < H_ML
    og_ref[0, 0] = jnp.concatenate([jnp.where(keep, a, 0.0), jnp.where(keep, bi, 0.0)], axis=1)
    og_ref[0, 1] = jnp.concatenate([jnp.where(keep, pltpu.roll(a, 120, 1), 0.0),
                                    jnp.where(keep, pltpu.roll(bi, 120, 1), 0.0)], axis=1)


def _pre(pf, conv_ml, conv_hy, bif_row, cos_t, sin_t, tril, triu, nt):
    nb, ltot, _ = pf.shape
    last8 = ltot // 8 - 1
    r8 = TT // 8

    def cur(col):
        return lambda t, b: (b, t, col)

    def prv(col):
        return lambda t, b: (b, jnp.maximum(t * r8 - 1, 0), col)

    def nxt(col):
        return lambda t, b: (b, jnp.minimum((t + 1) * r8, last8), col)

    const = lambda t, b: (0, 0)
    return pl.pallas_call(
        functools.partial(_pre_kernel, nt), grid=(nt + 1, nb),
        in_specs=[pl.BlockSpec((1, TT, 512), cur(PF_QK)), pl.BlockSpec((1, 8, 512), prv(PF_QK)),
                  pl.BlockSpec((1, 8, 512), nxt(PF_QK)),
                  pl.BlockSpec((1, TT, 768), cur(PF_HY)), pl.BlockSpec((1, 8, 768), prv(PF_HY)),
                  pl.BlockSpec((1, 8, 768), nxt(PF_HY)),
                  pl.BlockSpec((1, TT, 256), cur(PF_GM)),
                  pl.BlockSpec((3, 512), const), pl.BlockSpec((3, 768), const), pl.BlockSpec((1, 128), const),
                  pl.BlockSpec((TT, 512), lambda t, b: (t, 0)), pl.BlockSpec((TT, 512), lambda t, b: (t, 0)),
                  pl.BlockSpec((TT, TT), const), pl.BlockSpec((TT, TT), const)],
        out_specs=[pl.BlockSpec((1, TT, 512), lambda t, b: (b, t, 0)),
                   pl.BlockSpec((1, TT, 768), lambda t, b: (b, t, 0)),
                   pl.BlockSpec((1, 2, TT, 256), lambda t, b: (b, 0, t, 0))],
        out_shape=[jax.ShapeDtypeStruct((nb, ltot, 512), BF16),
                   jax.ShapeDtypeStruct((nb, ltot, 768), F32),
                   jax.ShapeDtypeStruct((nb, 2, ltot, 256), F32)],
        compiler_params=_params(("arbitrary", "arbitrary")), name="conv_rope_gates",
    )(pf, pf, pf, pf, pf, pf, pf, conv_ml, conv_hy, bif_row, cos_t, sin_t, tril, triu)


def _na_kernel(q_ref, k0_ref, k1_ref, k2_ref, v0_ref, v1_ref, v2_ref, kc_ref, vc_ref, za_ref, tab_ref, o_ref):
    lane = lax.broadcasted_iota(jnp.int32, (1, 128), 1)
    k_refs = (k0_ref, k1_ref, k2_ref)
    v_refs = (v0_ref, v1_ref, v2_ref)

    def row_max(s):
        return jnp.max(jnp.maximum(s[:, :128], s[:, 128:]), axis=-1, keepdims=True)

    def pair(hp, carry):
        sl = pl.ds(pl.multiple_of(hp * 128, 128), 128)
        q2 = q_ref[0, :, sl]
        out = None
        for e in range(2):
            hm = (lane >= HEAD_DIM) if e else (lane < HEAD_DIM)
            qm = jnp.where(hm, q2, jnp.zeros_like(q2))

            def pv(p, v):
                return _dot(p.astype(BF16), jnp.where(hm, v, jnp.ones_like(v)))

            sc = [_dot_nt(qm, kc_ref[0, :, sl])]
            sc += [_dot_nt(qm, k_refs[blk][0, :, sl]) + tab_ref[0, 2 * hp + e, :, blk * TT:(blk + 1) * TT]
                   for blk in range(3)]
            m = row_max(jnp.maximum(jnp.maximum(sc[0], sc[1]), jnp.maximum(sc[2], sc[3])))
            acc = pv(jnp.exp(sc[0] - m), vc_ref[0, :, sl])
            for blk in range(3):
                acc = acc + pv(jnp.exp(sc[blk + 1] - m), v_refs[blk][0, :, sl])
            o = jnp.where(hm, acc / pltpu.roll(acc, HEAD_DIM, 1), 0.0)
            out = o if e == 0 else out + o
        o_ref[0, :, sl] = (out * _silu(za_ref[0, :, sl])).astype(BF16)
        return carry

    lax.fori_loop(0, H_NA // 2, pair, 0, unroll=True)


def _attention(ona, pf, table, nt, with_ctx_queries):
    nb, ltot, _ = ona.shape
    nj = nt + 1 if with_ctx_queries else nt

    def kv(col, dj):
        return lambda j, b: (b, jnp.clip(j + dj, 0, nt - 1), col)

    def pat(j, b):
        p = jnp.where(j == 0, 0, jnp.where(j == nt - 1, 2, jnp.where(j == nt, 3, 1)))
        return (p, 0, 0, 0)

    return pl.pallas_call(
        _na_kernel, grid=(nj, nb),
        in_specs=[pl.BlockSpec((1, TT, 512), lambda j, b: (b, j, 0)),
                  pl.BlockSpec((1, TT, 512), kv(1, -1)), pl.BlockSpec((1, TT, 512), kv(1, 0)),
                  pl.BlockSpec((1, TT, 512), kv(1, 1)),
                  pl.BlockSpec((1, TT, 512), kv(2, -1)), pl.BlockSpec((1, TT, 512), kv(2, 0)),
                  pl.BlockSpec((1, TT, 512), kv(2, 1)),
                  pl.BlockSpec((1, TT, 512), lambda j, b: (b, nt, 1)),
                  pl.BlockSpec((1, TT, 512), lambda j, b: (b, nt, 2)),
                  pl.BlockSpec((1, TT, 512), lambda j, b: (b, j, PF_ZA)),
                  pl.BlockSpec((1, H_NA, TT, 3 * TT), pat)],
        out_specs=pl.BlockSpec((1, TT, 512), lambda j, b: (b, j, 0)),
        out_shape=jax.ShapeDtypeStruct((nb, ltot, 512), BF16),
        compiler_params=_params(("arbitrary", "arbitrary")), name="nbr_attention",
    )(ona, ona, ona, ona, ona, ona, ona, ona, ona, pf, table)


def _na_bias_table(rpb):
    a = np.arange(TT) // GRID_W
    qc = np.arange(TT) % GRID_W
    kidx = np.arange(3 * TT)
    kr = (kidx // TT - 1) * BAND_ROWS + (kidx % TT) // GRID_W
    kcol = kidx % GRID_W
    qstart = np.clip(qc - NA_COLS // 2, 0, GRID_W - NA_COLS)
    col_ok = (kcol[None, :] >= qstart[:, None]) & (kcol[None, :] < qstart[:, None] + NA_COLS)
    coff = np.clip(kcol[None, :] - qc[:, None], -(NA_COLS - 1), NA_COLS - 1) + NA_COLS - 1
    roff = np.clip(kr[None, :] - a[:, None] + NA_ROWS - 1, 0, 2 * NA_ROWS - 2)
    half = NA_ROWS // 2
    row_ok = [
        (kr[None, :] >= 0) & (kr[None, :] < NA_ROWS) & (a[:, None] >= 0),
        (kr[None, :] >= a[:, None] - half) & (kr[None, :] < a[:, None] + half),
        (kr[None, :] >= BAND_ROWS - NA_ROWS) & (kr[None, :] < BAND_ROWS) & (a[:, None] >= 0),
    ]
    nr, ncol = 2 * NA_ROWS - 1, 2 * NA_COLS - 1
    onehot = (coff[:GRID_W, :GRID_W].reshape(1, -1) == np.arange(ncol)[:, None]).astype(np.float32)
    cmat = jnp.dot(rpb.astype(F32).reshape(H_NA * nr, ncol), onehot,
                   precision=lax.Precision.HIGHEST).reshape(H_NA, nr, GRID_W, GRID_W)
    nkr = 3 * BAND_ROWS
    blocks = [jnp.concatenate([cmat[:, roff[ai * GRID_W, kj * GRID_W]] for kj in range(nkr)], axis=-1)
              for ai in range(BAND_ROWS)]
    bias = jnp.concatenate(blocks, axis=1)
    tabs = [jnp.where((r & col_ok)[None], bias, NEG) for r in row_ok]
    tabs.append(jnp.full_like(bias, NEG))
    return jnp.stack(tabs, axis=0)


def _head_lane_sums(x, lane_h):
    out = jnp.zeros_like(x)
    half = lax.broadcasted_iota(jnp.int32, (1, 128), 1) // HEAD_DIM
    for h in range(H_ML):
        t = x[:, (h // 2) * 128:(h // 2 + 1) * 128]
        sm = jnp.sum(jnp.where(half == h % 2, t, 0.0), axis=-1, keepdims=True)
        out = jnp.where(lane_h == h, sm, out)
    return out


def _scan_kernel(q_ref, k_ref, v_ref, g_ref, e_ref, bm_ref, o_ref, w_ref, n_ref, m_ref):
    s = pl.program_id(2)
    sgn = 1 - 2 * pl.program_id(1)

    @pl.when(s == 0)
    def _():
        w_ref[...] = jnp.zeros_like(w_ref)
        n_ref[...] = jnp.zeros_like(n_ref)
        m_ref[...] = jnp.zeros_like(m_ref)

    lane_h = lax.broadcasted_iota(jnp.int32, (1, D_ML), 1) // HEAD_DIM
    ri = lax.broadcasted_iota(jnp.int32, (TT, TT), 0)
    ci = lax.broadcasted_iota(jnp.int32, (TT, TT), 1)
    tri = ((ci - ri) * sgn) <= 0

    for i in range(SCAN_NB):
        a = g_ref[i, 0, :, 0:128]
        bb = g_ref[i, 0, :, 128:256]
        m_prev = m_ref[i]
        mc = jnp.maximum(m_prev, jnp.max(a, axis=0, keepdims=True))
        small = jnp.concatenate([jnp.exp(a - mc), jnp.exp(-bb - mc),
                                 jnp.broadcast_to(jnp.exp(m_prev - mc), (8, 128))], axis=0)
        sh = small.astype(BF16)
        sl = (small - sh.astype(F32)).astype(BF16)
        ex = _dot(sh, e_ref[...]) + _dot(sl, e_ref[...])
        w_e = ex[0:TT]
        eb_e = ex[TT:2 * TT]
        beta = ex[2 * TT:2 * TT + 1]

        q = q_ref[i]
        kwf = k_ref[i].astype(F32) * w_e
        kw = kwf.astype(BF16)
        vb = v_ref[i].astype(BF16)
        intra = None
        den_intra = jnp.zeros((TT, D_ML), F32)
        for h in range(H_ML):
            qh = jnp.where(lane_h == h, q, jnp.zeros_like(q))
            sc = jnp.where(tri, _dot_nt(qh, kw), 0.0)
            den_intra = jnp.where(lane_h == h, jnp.sum(sc, axis=-1, keepdims=True), den_intra)
            vh = jnp.where(lane_h == h, vb, jnp.zeros_like(vb))
            part = _dot(sc.astype(BF16), vh)
            intra = part if intra is None else intra + part
        num = beta * _dot(q, w_ref[i].astype(BF16)) + intra
        den = beta * _head_lane_sums(q.astype(F32) * n_ref[i], lane_h) + den_intra
        o_ref[i, 0] = num / jnp.maximum(jnp.abs(den), eb_e)

        w_ref[i] = beta * w_ref[i] + bm_ref[...] * _dot_tn(kw, vb)
        n_ref[i] = beta * n_ref[i] + jnp.sum(kwf, axis=0, keepdims=True)
        m_ref[i] = jnp.min(bb, axis=0, keepdims=True) + mc


def _mlstm_scan(qk, pf, gates, emat, bmask, nt):
    nb, ltot, _ = qk.shape
    g = SCAN_NB

    def chunk(s, d):
        return jnp.where(s == 0, nt, jnp.where(d == 0, s - 1, nt - s))

    const = lambda b, d, s: (0, 0)
    return pl.pallas_call(
        _scan_kernel, grid=(nb // g, 2, nt + 1),
        in_specs=[pl.BlockSpec((g, TT, 256), lambda b, d, s: (b, chunk(s, d), 0)),
                  pl.BlockSpec((g, TT, 256), lambda b, d, s: (b, chunk(s, d), 1)),
                  pl.BlockSpec((g, TT, 256), lambda b, d, s: (b, chunk(s, d), PF_VM)),
                  pl.BlockSpec((g, 1, TT, 256), lambda b, d, s: (b, d, chunk(s, d), 0)),
                  pl.BlockSpec((128, 256), const), pl.BlockSpec((256, 256), const)],
        out_specs=pl.BlockSpec((g, 1, TT, 256), lambda b, d, s: (b, d, chunk(s, d), 0)),
        out_shape=jax.ShapeDtypeStruct((nb, 2, ltot, 256), F32),
        scratch_shapes=[pltpu.VMEM((g, D_ML, D_ML), F32), pltpu.VMEM((g, 1, D_ML), F32),
                        pltpu.VMEM((g, 1, 128), F32)],
        compiler_params=_params(("arbitrary", "arbitrary", "arbitrary")), name="mlstm_scan",
    )(qk, qk, pf, gates, emat, bmask)


def _hyfilt_kernel(tl, length, feat_ref, w1_ref, b1_ref, w2_ref, b2_ref, w3_ref, fq_ref, dec_ref, o_ref, ss_ref):
    i = pl.program_id(0)
    a = jnp.sin(fq_ref[0:1] * (_dot_f32(feat_ref[...], w1_ref[...]) + b1_ref[...]))
    a = jnp.sin(fq_ref[1:2] * (_dot_f32(a, w2_ref[...]) + b2_ref[...]))
    dec = dec_ref[...]
    f = _dot_f32(a, w3_ref[...]) * jnp.concatenate([dec, dec, dec, dec], axis=1)
    fwd = jnp.concatenate([f[:, 0:D_HY], f[:, 2 * D_HY:3 * D_HY]], axis=1)
    bwd = jnp.concatenate([f[:, D_HY:2 * D_HY], f[:, 3 * D_HY:4 * D_HY]], axis=1)
    row = lax.broadcasted_iota(jnp.int32, (tl, 1), 0) + i * tl
    k2 = jnp.where(row < length, fwd, jnp.where(row > length, bwd, 0.0))
    o_ref[...] = k2

    @pl.when(i == 0)
    def _():
        ss_ref[...] = jnp.zeros_like(ss_ref)

    ss_ref[...] += jnp.sum(k2 * k2, axis=0, keepdims=True)


def _hyena_filters(length, w1, b1, w2, b2, w3, freq):
    t = np.arange(length, dtype=np.float32)
    tn = (t / np.float32(length - 1)).astype(np.float32)
    bands = (HY_EMB - 1) // 2
    fr = np.linspace(1e-4, bands - 1, bands, dtype=np.float32)
    ang = (np.float32(2.0 * math.pi / length) * t[:, None] * fr[None, :]).astype(np.float32)
    feat = np.zeros((length, 128), np.float32)
    feat[:, 0] = tn
    feat[:, 1:1 + bands] = np.cos(ang)
    feat[:, 1 + bands:1 + 2 * bands] = -np.sin(ang)
    deltas = np.abs(np.linspace(math.log(HY_TARGET) / HY_SLOW, math.log(HY_TARGET) / HY_FAST, D_HY, dtype=np.float32))
    dec = np.exp(-tn[:, None] * deltas[None, :]).astype(np.float32)
    lag = np.concatenate([np.arange(length), [0], np.arange(length - 1, 0, -1)])
    feat, dec = feat[lag], dec[lag]

    pad2 = lambda m, r, c: jnp.zeros((r, c), F32).at[:m.shape[0], :m.shape[1]].set(m.astype(F32))
    w1p = pad2(w1, 128, 128)
    w2p = pad2(w2, 128, 128)
    w3p = pad2(w3, 128, 4 * D_HY)
    b1p = pad2(b1[None], 1, 128)
    b2p = pad2(b2[None], 1, 128)
    fqp = pad2(freq, 2, 128)
    tl = 512
    const = lambda i: (0, 0)
    return pl.pallas_call(
        functools.partial(_hyfilt_kernel, tl, length), grid=(2 * length // tl,),
        in_specs=[pl.BlockSpec((tl, 128), lambda i: (i, 0)),
                  pl.BlockSpec((128, 128), const), pl.BlockSpec((1, 128), const),
                  pl.BlockSpec((128, 128), const), pl.BlockSpec((1, 128), const),
                  pl.BlockSpec((128, 4 * D_HY), const), pl.BlockSpec((2, 128), const),
                  pl.BlockSpec((tl, D_HY), lambda i: (i, 0))],
        out_specs=[pl.BlockSpec((tl, 2 * D_HY), lambda i: (i, 0)), pl.BlockSpec((1, 2 * D_HY), const)],
        out_shape=[jax.ShapeDtypeStruct((2 * length, 2 * D_HY), F32), jax.ShapeDtypeStruct((1, 2 * D_HY), F32)],
        compiler_params=_params(("arbitrary",)), name="hyena_filter_mlp",
    )(feat, w1p, b1p, w2p, b2p, w3p, fqp, dec)


@functools.lru_cache(maxsize=None)
def _fft_tables(length):
    n = 2 * length
    n1 = n // FFT_N2
    nh = n1 // 2
    k1 = np.arange(n1)[None, :, None].astype(np.float64)
    j = np.arange(nh)[None, None, :].astype(np.float64)
    n2 = np.arange(FFT_N2)[:, None, None].astype(np.float64)
    ang = -2.0 * np.pi * k1 * (FFT_N2 * j + n2) / n
    gf = np.concatenate([np.cos(ang), np.sin(ang)], axis=1)
    gi = np.concatenate([np.cos(ang).transpose(0, 2, 1), np.sin(ang).transpose(0, 2, 1)], axis=1)
    kk = np.arange(FFT_N2)[:, None] * np.arange(FFT_N2)[None, :]
    a2 = -2.0 * np.pi * kk / FFT_N2
    f2 = np.concatenate([np.cos(a2), np.sin(a2)], axis=0)
    cv = lambda m: np.asarray(m, np.float32).astype(BF16)
    return cv(gf), cv(gi), cv(f2), n1


def _stage1(n1, z_of, gf_ref, scr_ref, cols):
    nh = n1 // 2

    def body(n2, c):
        r = _dot(gf_ref[n2], z_of(n2))
        off = pl.multiple_of(n2 * n1, n1)
        if cols == 256:
            scr_ref[0, pl.ds(off, n1), :] = r[:n1, :128] - r[n1:, 128:]
            scr_ref[1, pl.ds(off, n1), :] = r[n1:, :128] + r[:n1, 128:]
        else:
            scr_ref[0, pl.ds(off, n1), :] = r[:n1]
            scr_ref[1, pl.ds(off, n1), :] = r[n1:]
        return c

    lax.fori_loop(0, FFT_N2, body, 0, unroll=FFT_UNROLL)
    del nh


def _stage2(n1, k1, f2_ref, scr_ref):
    ar = scr_ref[0, pl.ds(k1, FFT_N2, stride=n1), :]
    ai = scr_ref[1, pl.ds(k1, FFT_N2, stride=n1), :]
    r = _dot(f2_ref[...], jnp.concatenate([ar, ai], axis=1).astype(BF16))
    return r[:128, :128] - r[128:, 128:], r[128:, :128] + r[:128, 128:]


def _hyconv_kernel(n1, kg, z_ref, gf_ref, f2_ref, kf_ref, gi_ref, o_ref, scr_ref):
    s = pl.program_id(2)
    nh = n1 // 2

    @pl.when(s == 0)
    def _():
        def z_of(n2):
            zr = z_ref[0, 0, pl.ds(n2, nh, stride=FFT_N2), :]
            zi = z_ref[0, 1, pl.ds(n2, nh, stride=FFT_N2), :]
            return jnp.concatenate([zr, zi], axis=1).astype(BF16)
        _stage1(n1, z_of, gf_ref, scr_ref, 256)

    def k1_body(i, c):
        k1 = s * kg + i
        xr, xi = _stage2(n1, k1, f2_ref, scr_ref)
        off = pl.multiple_of(i * FFT_N2, FFT_N2)
        kr = kf_ref[0, pl.ds(off, FFT_N2), :]
        ki = kf_ref[1, pl.ds(off, FFT_N2), :]
        yc = jnp.concatenate([xr * kr - xi * ki, xr * ki + xi * kr], axis=1).astype(BF16)
        r = _dot(f2_ref[...], yc)
        scr_ref[0, pl.ds(k1, FFT_N2, stride=n1), :] = r[:128, :128] + r[128:, 128:]
        scr_ref[1, pl.ds(k1, FFT_N2, stride=n1), :] = r[:128, 128:] - r[128:, :128]
        return c

    lax.fori_loop(0, kg, k1_body, 0, unroll=FFT_UNROLL)

    @pl.when(s == pl.num_programs(2) - 1)
    def _():
        def body(n2, c):
            off = pl.multiple_of(n2 * n1, n1)
            bc = jnp.concatenate([scr_ref[0, pl.ds(off, n1), :], scr_ref[1, pl.ds(off, n1), :]], axis=1)
            r = _dot(gi_ref[n2], bc.astype(BF16))
            o_ref[0, 0, pl.ds(n2, nh, stride=FFT_N2), :] = r[:nh, :128] + r[nh:, 128:]
            o_ref[0, 1, pl.ds(n2, nh, stride=FFT_N2), :] = r[:nh, 128:] - r[nh:, :128]
            return c
        lax.fori_loop(0, FFT_N2, body, 0, unroll=FFT_UNROLL)


def _hyena_conv(z4, col0, kf, length):
    gf, gi, f2, n1 = _fft_tables(length)
    n = 2 * length
    kg = min(n1, 16)
    npair = z4.shape[0]
    big = pl.Buffered(1)
    return pl.pallas_call(
        functools.partial(_hyconv_kernel, n1, kg), grid=(npair, 2, n1 // kg),
        in_specs=[pl.BlockSpec((1, 2, length, 128), lambda p, c, s: (p, 0, 0, col0 + c), pipeline_mode=big),
                  pl.BlockSpec(gf.shape, lambda p, c, s: (0, 0, 0), pipeline_mode=big),
                  pl.BlockSpec(f2.shape, lambda p, c, s: (0, 0)),
                  pl.BlockSpec((2, kg * FFT_N2, 128), lambda p, c, s: (0, s, c)),
                  pl.BlockSpec(gi.shape, lambda p, c, s: (0, 0, 0), pipeline_mode=big)],
        out_specs=pl.BlockSpec((1, 2, length, 128), lambda p, c, s: (p, 0, 0, c), pipeline_mode=big),
        out_shape=jax.ShapeDtypeStruct((npair, 2, length, 256), F32),
        scratch_shapes=[pltpu.VMEM((2, n, 128), F32)],
        compiler_params=_params(("arbitrary", "arbitrary", "arbitrary")), name="hyena_fftconv",
    )(z4, gf, f2, kf, gi)


def _hyfft_kernel(n1, kg, z_ref, gf_ref, f2_ref, o_ref, scr_ref):
    s = pl.program_id(2)

    @pl.when(s == 0)
    def _():
        _stage1(n1, lambda n2: z_ref[pl.ds(n2, n1, stride=FFT_N2), :].astype(BF16), gf_ref, scr_ref, 128)

    def k1_body(i, c):
        xr, xi = _stage2(n1, s * kg + i, f2_ref, scr_ref)
        off = pl.multiple_of(i * FFT_N2, FFT_N2)
        o_ref[0, 0, pl.ds(off, FFT_N2), :] = xr
        o_ref[0, 1, pl.ds(off, FFT_N2), :] = xi
        return c

    lax.fori_loop(0, kg, k1_body, 0, unroll=FFT_UNROLL)


@functools.lru_cache(maxsize=None)
def _fft_table_full(length):
    n = 2 * length
    n1 = n // FFT_N2
    k1 = np.arange(n1)[None, :, None].astype(np.float64)
    j = np.arange(n1)[None, None, :].astype(np.float64)
    n2 = np.arange(FFT_N2)[:, None, None].astype(np.float64)
    ang = -2.0 * np.pi * k1 * (FFT_N2 * j + n2) / n
    return np.concatenate([np.cos(ang), np.sin(ang)], axis=1).astype(np.float32).astype(BF16)


def _filter_spectrum(k2, length):
    _, _, f2, n1 = _fft_tables(length)
    gf = _fft_table_full(length)
    n = 2 * length
    kg = min(n1, 16)
    nf = k2.shape[1] // D_HY
    big = pl.Buffered(1)
    return pl.pallas_call(
        functools.partial(_hyfft_kernel, n1, kg), grid=(nf, 2, n1 // kg),
        in_specs=[pl.BlockSpec((n, 128), lambda f, c, s: (0, 2 * f + c), pipeline_mode=big),
                  pl.BlockSpec(gf.shape, lambda f, c, s: (0, 0, 0), pipeline_mode=big),
                  pl.BlockSpec(f2.shape, lambda f, c, s: (0, 0))],
        out_specs=pl.BlockSpec((1, 2, kg * FFT_N2, 128), lambda f, c, s: (f, 0, s, c)),
        out_shape=jax.ShapeDtypeStruct((nf, 2, n, 256), F32),
        scratch_shapes=[pltpu.VMEM((2, n, 128), F32)],
        compiler_params=_params(("arbitrary", "arbitrary", "arbitrary")), name="hyena_filter_fft",
    )(k2, gf, f2)


def _normalised(k2, ssq, extra_scale):
    return k2 * (lax.rsqrt(ssq + EPS) * extra_scale)


def _hy_mid_kernel(y_ref, hv_ref, b_ref, o_ref):
    v = hv_ref[0][:, 0:D_HY]
    x1 = hv_ref[0][:, D_HY:2 * D_HY]
    o_ref[0] = x1 * (y_ref[0] + v * b_ref[...])


def _hy_mid(y1, hv, bias0):
    nb, length, _ = y1.shape
    te = 4 * TT
    return pl.pallas_call(
        _hy_mid_kernel, grid=(nb, length // te),
        in_specs=[pl.BlockSpec((1, te, D_HY), lambda b, t: (b, t, 0)),
                  pl.BlockSpec((1, te, 512), lambda b, t: (b, t, 0)),
                  pl.BlockSpec((1, D_HY), lambda b, t: (0, 0))],
        out_specs=pl.BlockSpec((1, te, D_HY), lambda b, t: (b, t, 0)),
        out_shape=jax.ShapeDtypeStruct(y1.shape, F32),
        compiler_params=_params(("arbitrary", "arbitrary")), name="hyena_mid",
    )(y1, hv, bias0.reshape(1, D_HY))


def _hy_ctx_kernel(lc, hv_ref, zh_ref, k2_ref, b_ref, fc_ref, ff_ref, ci_ref, o_ref):
    nf = 2 * lc
    hv = hv_ref[0]

    def conv(z, o):
        kf = _dot_exact_rhs(ff_ref[...], k2_ref[:, o * D_HY:(o + 1) * D_HY])
        zs = _dot(fc_ref[...], z.astype(BF16))
        zr, zi = zs[:nf], zs[nf:]
        kr, ki = kf[:nf], kf[nf:]
        ys = jnp.concatenate([zr * kr - zi * ki, zr * ki + zi * kr], axis=0).astype(BF16)
        return _dot(ci_ref[...], ys) * (1.0 / nf)

    v = hv[:, 0:D_HY]
    x1 = hv[:, D_HY:2 * D_HY]
    x2 = hv[:, 2 * D_HY:3 * D_HY]
    z1 = x1 * (conv(v, 0) + v * b_ref[0:1])
    p = x2 * (conv(z1, 1) + z1 * b_ref[1:2])
    o_ref[0] = (p * _silu(zh_ref[0])).astype(BF16)


def _dot_exact_rhs(a_bf16, b):
    bh, bm, bl = _split3(b)
    return _dot(a_bf16, bh) + _dot(a_bf16, bm) + _dot(a_bf16, bl)


@functools.lru_cache(maxsize=None)
def _ctx_dft_tables(lc):
    nf = 2 * lc
    k = np.arange(nf)[:, None].astype(np.float64)
    ang = -2.0 * np.pi * k * np.arange(nf)[None, :] / nf
    ff = np.concatenate([np.cos(ang), np.sin(ang)], axis=0)
    fc = ff[:, :lc]
    a2 = 2.0 * np.pi * np.arange(lc)[:, None] * np.arange(nf)[None, :] / nf
    ci = np.concatenate([np.cos(a2), -np.sin(a2)], axis=1)
    cv = lambda m: np.asarray(m, np.float32).astype(BF16)
    return cv(fc), cv(ff), cv(ci)


def _hy_ctx(hv, pf, k2c, hy_bias, nt, lc):
    nb = hv.shape[0]
    fc, ff, ci = _ctx_dft_tables(lc)
    c2 = lambda b: (0, 0)
    return pl.pallas_call(
        functools.partial(_hy_ctx_kernel, lc), grid=(nb,),
        in_specs=[pl.BlockSpec((1, lc, 768), lambda b: (b, nt, 0)),
                  pl.BlockSpec((1, lc, D_HY), lambda b: (b, nt, PF_ZH)),
                  pl.BlockSpec(k2c.shape, c2),
                  pl.BlockSpec((2, D_HY), c2),
                  pl.BlockSpec(fc.shape, c2), pl.BlockSpec(ff.shape, c2), pl.BlockSpec(ci.shape, c2)],
        out_specs=pl.BlockSpec((1, lc, D_HY), lambda b: (b, 0, 0)),
        out_shape=jax.ShapeDtypeStruct((nb, lc, D_HY), BF16),
        compiler_params=_params(("arbitrary",)), name="hyena_ctx",
    )(hv, pf, k2c, hy_bias, fc, ff, ci)


def _out_kernel(n_hy, na_ref, h0_ref, h1_ref, om_ref, zm_ref, w_ref, x_ref, mod_ref, g_ref, *rest):
    hy_refs, o_ref = rest[:n_hy], rest[n_hy]
    if n_hy == 1:
        hy = hy_refs[0][0]
    else:
        y_ref, z_ref, x2_ref, zh_ref, b_ref = hy_refs
        z = z_ref[0]
        hy = (x2_ref[0] * (y_ref[0] + z * b_ref[...]) * _silu(zh_ref[0])).astype(BF16)
    ml = jax.nn.sigmoid(om_ref[0]) * (h0_ref[0, 0] + h1_ref[0, 0]) * _silu(zm_ref[0])
    y = (_dot(na_ref[0], w_ref[0:D_NA]) + _dot(ml.astype(BF16), w_ref[D_NA:D_NA + D_ML])
         + _dot(hy, w_ref[D_NA + D_ML:D_MODEL]))
    ms = jnp.mean(y * y, axis=-1, keepdims=True)
    r = y * lax.rsqrt(ms + EPS) * g_ref[...]
    o_ref[0] = x_ref[0] + mod_ref[0][2:3] * r


def _out_projection(na_g, hml, pf, hy_args, w_out, res, res_t0, mod3, g_post, to, t0, ntiles, mod_row, out_rows,
                    prev=None):
    nb = na_g.shape[0]
    d = D_MODEL
    tok = lambda col: (lambda b, t: (b, t + t0, col))
    own = lambda col: (lambda b, t: (b, t, col))
    in_specs = [pl.BlockSpec((1, to, 512), tok(0)),
                pl.BlockSpec((1, 1, to, 256), lambda b, t: (b, 0, t + t0, 0)),
                pl.BlockSpec((1, 1, to, 256), lambda b, t: (b, 1, t + t0, 0)),
                pl.BlockSpec((1, to, 256), tok(PF_OM)), pl.BlockSpec((1, to, 256), tok(PF_ZM)),
                pl.BlockSpec((d, d), lambda b, t: (0, 0), pipeline_mode=pl.Buffered(1)),
                pl.BlockSpec((1, to, d), lambda b, t: (b, t + res_t0, 0)),
                pl.BlockSpec((1, 3, d), lambda b, t: (mod_row(b), 0, 0)),
                pl.BlockSpec((1, d), lambda b, t: (0, 0))]
    args = [na_g, hml, hml, pf, pf, w_out, res, mod3, g_post.reshape(1, d)]
    if len(hy_args) == 1:
        in_specs.append(pl.BlockSpec((1, to, D_HY), own(0)))
        args.append(hy_args[0])
    else:
        y2, z1, hv, pf_, bias1 = hy_args
        in_specs += [pl.BlockSpec((1, to, D_HY), own(0)), pl.BlockSpec((1, to, D_HY), own(0)),
                     pl.BlockSpec((1, to, D_HY), tok(2)), pl.BlockSpec((1, to, D_HY), tok(PF_ZH)),
                     pl.BlockSpec((1, D_HY), lambda b, t: (0, 0))]
        args += [y2, z1, hv, pf_, bias1.reshape(1, D_HY)]
    n_hy = len(hy_args)
    aliases = {}
    if prev is not None:
        in_specs.append(pl.BlockSpec(memory_space=pl.ANY))
        args.append(prev)
        aliases = {len(args) - 1: 0}
        kern = lambda *r: _out_kernel(n_hy, *r[:9 + n_hy], r[10 + n_hy])
    else:
        kern = functools.partial(_out_kernel, n_hy)
    return pl.pallas_call(
        kern, grid=(nb, ntiles), in_specs=in_specs,
        out_specs=pl.BlockSpec((1, to, d), tok(0)),
        out_shape=jax.ShapeDtypeStruct((nb, out_rows, d), F32),
        input_output_aliases=aliases,
        compiler_params=_params(("arbitrary", "arbitrary")), name="out_proj_norm_residual",
    )(*args)


@functools.lru_cache(maxsize=None)
def _rope_tables(length, lc):
    n = HEAD_DIM // 4
    t = np.arange(length)
    inv = (ROPE_BASE ** (-np.arange(n, dtype=np.float32) / n)).astype(np.float32)
    pos = np.stack([t // GRID_W, t % GRID_W], axis=-1).astype(np.float32)
    ang = (pos[:, :, None] * inv).astype(np.float32)
    cos_h = np.concatenate([np.cos(ang[:, 0]), np.cos(ang[:, 0]), np.cos(ang[:, 1]), np.cos(ang[:, 1])], axis=-1)
    sin_h = np.concatenate([-np.sin(ang[:, 0]), np.sin(ang[:, 0]), -np.sin(ang[:, 1]), np.sin(ang[:, 1])], axis=-1)
    cos_t = np.concatenate([np.tile(cos_h, (1, 2 * H_ML)), np.ones((lc, 512), np.float32)], axis=0)
    sin_t = np.concatenate([np.tile(sin_h, (1, 2 * H_ML)), np.zeros((lc, 512), np.float32)], axis=0)
    kscale = np.concatenate([np.ones(256, np.float32), np.full(256, HEAD_DIM ** -0.5, np.float32)])
    return (cos_t * kscale).astype(np.float32), (sin_t * kscale).astype(np.float32)


@functools.lru_cache(maxsize=None)
def _scan_constants():
    tril = np.tril(np.ones((TT, TT), np.float32))
    emat = np.zeros((128, D_ML), np.float32)
    for h in range(H_ML):
        emat[h, h * HEAD_DIM:(h + 1) * HEAD_DIM] = 1.0
    rh = np.arange(D_ML)[:, None] // HEAD_DIM
    chd = np.arange(D_ML)[None, :] // HEAD_DIM
    bmask = (rh == chd).astype(np.float32)
    return tril.astype(BF16), np.ascontiguousarray(tril.T).astype(BF16), emat.astype(BF16), bmask


def _layer(x_lat, x_ctx, ctx_tile, mod3, nb, length, lc, g_pre, g_post, w_in, b_if, conv_ml, conv_hy, rpb,
           hf_w1, hf_b1, hf_w2, hf_b2, hf_w3, hf_freq, hy_bias, w_out, update_ctx):
    nt = length // TT
    ltot = length + lc
    tril, triu, emat, bmask = _scan_constants()
    cos_t, sin_t = _rope_tables(length, lc)

    c = lambda a, b: w_in[:, a:b]
    wna = jnp.concatenate([c(0, 512) * (HEAD_DIM ** -0.5), c(512, 1536)], axis=1).astype(BF16)
    wpf = jnp.concatenate([c(1536, 2048), c(2048, 2560), c(2560, 2816), c(2816, 3072), c(3072, 3328),
                           c(4112, 4368), c(3328, 3344), jnp.zeros((D_MODEL, 240), F32), c(3344, 4112)],
                          axis=1).astype(BF16)
    ona, pf = _projection(x_lat, x_ctx, ctx_tile, mod3, g_pre, wna, wpf, nt)

    na_g = _attention(ona, pf, _na_bias_table(rpb), nt, update_ctx)

    bif_row = jnp.zeros((1, 128), F32).at[0, :4 * H_ML].set(b_if.reshape(-1))
    qk, hv, gates = _pre(pf, conv_ml, conv_hy, bif_row, cos_t, sin_t, tril, triu, nt)
    hml = _mlstm_scan(qk, pf, gates, emat, bmask, nt)

    k2, ssq = _hyena_filters(length, hf_w1, hf_b1, hf_w2, hf_b2, hf_w3, hf_freq)
    kf = _filter_spectrum(_normalised(k2, ssq, 1.0 / (2 * length)), length)
    hv4 = hv.reshape(nb // 2, 2, ltot, 768)
    y1 = _hyena_conv(hv4, 0, kf[0], length).reshape(nb, length, D_HY)
    z1 = _hy_mid(y1, hv, hy_bias[0])
    y2 = _hyena_conv(z1.reshape(nb // 2, 2, length, D_HY), 0, kf[1], length).reshape(nb, length, D_HY)
    hy_args = (y2, z1, hv, pf, hy_bias[1])
    to = 2 * TT
    if not update_ctx:
        return _out_projection(na_g, hml, pf, hy_args, w_out, x_lat, 0, mod3, g_post, to, 0, length // to,
                               lambda b: b, length)

    xn = _out_projection(na_g, hml, pf, hy_args, w_out, x_lat, 0, mod3, g_post, to, 0, length // to,
                         lambda b: b, ltot)
    k2c, ssq_c = _hyena_filters(lc, hf_w1, hf_b1, hf_w2, hf_b2, hf_w3, hf_freq)
    hyc_g = _hy_ctx(hv, pf, _normalised(k2c, ssq_c, 1.0), hy_bias, nt, lc)
    return _out_projection(na_g, hml, pf, (hyc_g,), w_out, x_ctx, ctx_tile, mod3, g_post, TT, nt, 1,
                           lambda b: nb, ltot, prev=xn)


def kernel(x, c, ctx, c_ctx, w_ada, b_ada, g_pre, g_post, w_in, b_if, conv_ml, conv_hy, rpb, hf_w1, hf_b1,
           hf_w2, hf_b2, hf_w3, hf_freq, hy_bias, w_out):
    nb, length, d = x.shape
    lc = ctx.shape[1]
    depth = w_in.shape[0]
    assert lc == TT and length % (4 * TT) == 0 and nb % 2 == 0 and d == D_MODEL
    cc = jnp.zeros((16, d), F32).at[:nb].set(c).at[nb].set(c_ctx)
    x_lat, x_ctx, ctx_tile = x, ctx, 0
    for l in range(depth):
        mod3 = _modulation(cc, w_ada[l], b_ada[l]).reshape(16, 3, d)
        x_lat = _layer(x_lat, x_ctx, ctx_tile, mod3, nb, length, lc, g_pre[l], g_post[l], w_in[l], b_if[l],
                       conv_ml[l], conv_hy[l], rpb[l], hf_w1[l], hf_b1[l], hf_w2[l], hf_b2[l], hf_w3[l],
                       hf_freq[l], hy_bias[l], w_out[l].astype(BF16), l < depth - 1)
        x_ctx, ctx_tile = x_lat, length // TT
    return x_lat
```

```python
import functools
import math

import numpy as np
import jax
import jax.numpy as jnp
from jax import lax
from jax.experimental import pallas as pl
from jax.experimental.pallas import tpu as pltpu

F32 = jnp.float32
BF16 = jnp.bfloat16

D_MODEL = 1024
GRID_W = 64
HEAD_DIM = 64
D_NA = 512
D_ML = 256
D_HY = 256
H_NA = 8
H_ML = 4
NA_ROWS = 8
NA_COLS = 16
HY_EMB = 33
HY_FFN = 64
HY_TARGET = 1e-2
HY_FAST = 0.3
HY_SLOW = 1.5
ROPE_BASE = 10000.0
EPS = 1e-6

TT = 256
BAND_ROWS = TT // GRID_W
SCAN_NB = 2
PRE_ROWS = 64
FFT_N2 = 128
FFT_PAD = 8
FFT_UNROLL = 8
NEG = -1e30
VMEM_LIMIT = 56 * 1024 * 1024

PF_HY = 0
PF_VM, PF_OM, PF_ZM, PF_ZH = 3, 8, 9, 10
PF_ZA, PF_QK = 2, 3
PF_GM = 22


def _dot(a, b):
    return jnp.dot(a, b, preferred_element_type=F32)


def _dot_nt(a, b):
    return lax.dot_general(a, b, (((1,), (1,)), ((), ())), preferred_element_type=F32)


def _dot_tn(a, b):
    return lax.dot_general(a, b, (((0,), (0,)), ((), ())), preferred_element_type=F32)


def _split3(x):
    h = x.astype(BF16)
    r = x - h.astype(F32)
    m = r.astype(BF16)
    l = (r - m.astype(F32)).astype(BF16)
    return h, m, l


def _dot_f32(a, b):
    ah, am, al = _split3(a)
    bh, bm, bl = _split3(b)
    return (_dot(ah, bh) + (_dot(ah, bm) + _dot(am, bh))
            + (_dot(ah, bl) + _dot(al, bh) + _dot(am, bm)))


def _dot_exact_lhs(a_bf16, b):
    bh, bm, bl = _split3(b)
    return _dot(a_bf16, bh) + _dot(a_bf16, bm) + _dot(a_bf16, bl)


def _silu(x):
    return x * jax.nn.sigmoid(x)


def _params(sem, vmem=VMEM_LIMIT):
    return pltpu.CompilerParams(dimension_semantics=sem, vmem_limit_bytes=vmem)


def _mod_kernel(c_ref, w_ref, b_ref, o_ref):
    o_ref[...] = _dot_f32(_silu(c_ref[...]), w_ref[...]) + b_ref[...]


def _modulation(cc, w_ada, b_ada):
    n = w_ada.shape[1]
    bn = 512
    return pl.pallas_call(
        _mod_kernel, grid=(n // bn,),
        in_specs=[pl.BlockSpec(cc.shape, lambda j: (0, 0)),
                  pl.BlockSpec((w_ada.shape[0], bn), lambda j: (0, j)),
                  pl.BlockSpec((1, bn), lambda j: (0, j))],
        out_specs=pl.BlockSpec((cc.shape[0], bn), lambda j: (0, j)),
        out_shape=jax.ShapeDtypeStruct((cc.shape[0], n), F32),
        compiler_params=_params(("arbitrary",)), name="adaln_mod",
    )(cc, w_ada, b_ada.reshape(1, n))


def _proj_kernel(nt, x_ref, c_ref, mod_ref, g_ref, wna_ref, wpf_ref, ona_ref, opf_ref):
    x = jnp.where(pl.program_id(1) == nt, c_ref[0], x_ref[0])
    ms = jnp.mean(x * x, axis=-1, keepdims=True)
    y = x * lax.rsqrt(ms + EPS) * g_ref[...]
    mod = mod_ref[0]
    h = (y * (1.0 + mod[1:2]) + mod[0:1]).astype(BF16)
    ona_ref[0] = _dot(h, wna_ref[...]).astype(BF16)
    opf_ref[0] = _dot(h, wpf_ref[...])


def _projection(x_lat, x_ctx, ctx_tile, mod3, g_pre, wna, wpf, nt):
    nb, _, d = x_lat.shape
    ltot = (nt + 1) * TT
    ctx_row = nb

    def mod_idx(b, t):
        return (jnp.where(t == nt, ctx_row, b), 0, 0)

    return pl.pallas_call(
        functools.partial(_proj_kernel, nt), grid=(nb, nt + 1),
        in_specs=[pl.BlockSpec((1, TT, d), lambda b, t: (b, jnp.minimum(t, nt - 1), 0)),
                  pl.BlockSpec((1, TT, d), lambda b, t: (b, ctx_tile, 0)),
                  pl.BlockSpec((1, 3, d), mod_idx),
                  pl.BlockSpec((1, d), lambda b, t: (0, 0)),
                  pl.BlockSpec(wna.shape, lambda b, t: (0, 0), pipeline_mode=pl.Buffered(1)),
                  pl.BlockSpec(wpf.shape, lambda b, t: (0, 0), pipeline_mode=pl.Buffered(1))],
        out_specs=[pl.BlockSpec((1, TT, wna.shape[1]), lambda b, t: (b, t, 0)),
                   pl.BlockSpec((1, TT, wpf.shape[1]), lambda b, t: (b, t, 0))],
        out_shape=[jax.ShapeDtypeStruct((nb, ltot, wna.shape[1]), BF16),
                   jax.ShapeDtypeStruct((nb, ltot, wpf.shape[1]), F32)],
        compiler_params=_params(("arbitrary", "arbitrary")), name="norm_in_proj",
    )(x_lat, x_ctx, mod3, g_pre.reshape(1, d), wna, wpf)


def _pre_kernel(nt, qk_ref, qkp_ref, qkn_ref, hy_ref, hyp_ref, hyn_ref, gm_ref, cml_ref, chy_ref,
                bif_ref, cos_ref, sin_ref, tril_ref, triu_ref, oqk_ref, ohv_ref, og_ref):
    t = pl.program_id(0)
    pm = jnp.where(jnp.logical_and(t != 0, t != nt), 1.0, 0.0)
    nm = jnp.where(jnp.logical_and(t != nt - 1, t != nt), 1.0, 0.0)
    row8 = lax.broadcasted_iota(jnp.int32, (8, 1), 0)
    nchunk = TT // PRE_ROWS

    def conv3(ref, p_ref, n_ref, w, c):
        r0 = c * PRE_ROWS
        u = ref[0, r0:r0 + PRE_ROWS]
        up = p_ref[0][7:8] * pm if c == 0 else ref[0, r0 - 1:r0]
        un = n_ref[0][0:1] * nm if c == nchunk - 1 else ref[0, r0 + PRE_ROWS:r0 + PRE_ROWS + 1]
        rp = pltpu.roll(u, 1, 0)
        rn = pltpu.roll(u, PRE_ROWS - 1, 0)
        prev = jnp.concatenate([jnp.where(row8 == 0, up, rp[0:8]), rp[8:]], axis=0)
        nxt = jnp.concatenate([rn[:PRE_ROWS - 8], jnp.where(row8 == 7, un, rn[PRE_ROWS - 8:])], axis=0)
        return prev * w[0:1] + u * w[1:2] + nxt * w[2:3]

    lane = lax.broadcasted_iota(jnp.int32, (1, 2 * D_ML), 1)
    first = (lane % 32) < 16
    for c in range(nchunk):
        rows = slice(c * PRE_ROWS, (c + 1) * PRE_ROWS)
        ohv_ref[0, rows] = conv3(hy_ref, hyp_ref, hyn_ref, chy_ref[...], c)
        x = _silu(conv3(qk_ref, qkp_ref, qkn_ref, cml_ref[...], c))
        partner = jnp.where(first, pltpu.roll(x, 2 * D_ML - 16, 1), pltpu.roll(x, 16, 1))
        oqk_ref[0, rows] = (x * cos_ref[rows] + partner * sin_ref[rows]).astype(BF16)

    g = gm_ref[0] + bif_ref[...]
    gl = lax.broadcasted_iota(jnp.int32, (1, 128), 1)
    is_f = jnp.logical_and((gl % 8) >= 4, gl < 16)
    lf = jnp.where(is_f, jnp.minimum(g, 0.0) - jnp.log1p(jnp.exp(-jnp.abs(g))), 0.0)
    bcum = jnp.where(gl < 8, _dot_exact_lhs(tril_ref[...], lf), _dot_exact_lhs(triu_ref[...], lf))
    a = g - pltpu.roll(bcum, 128 - H_ML, 1)
    ab = jnp.where(is_f, bcum, a)
    keep = gl < 2 * H_ML
    og_ref[0, 0] = jnp.where(keep, ab, 0.0)
    og_ref[0, 1] = jnp.where(keep, pltpu.roll(ab, 120, 1), 0.0)


def _pre(pf, conv_ml, conv_hy, bif_row, cos_t, sin_t, tril, triu, nt):
    nb, ltot, _ = pf.shape
    last8 = ltot // 8 - 1
    r8 = TT // 8

    def cur(col):
        return lambda t, b: (b, t, col)

    def prv(col):
        return lambda t, b: (b, jnp.maximum(t * r8 - 1, 0), col)

    def nxt(col):
        return lambda t, b: (b, jnp.minimum((t + 1) * r8, last8), col)

    const = lambda t, b: (0, 0)
    return pl.pallas_call(
        functools.partial(_pre_kernel, nt), grid=(nt + 1, nb),
        in_specs=[pl.BlockSpec((1, TT, 512), cur(PF_QK)), pl.BlockSpec((1, 8, 512), prv(PF_QK)),
                  pl.BlockSpec((1, 8, 512), nxt(PF_QK)),
                  pl.BlockSpec((1, TT, 768), cur(PF_HY)), pl.BlockSpec((1, 8, 768), prv(PF_HY)),
                  pl.BlockSpec((1, 8, 768), nxt(PF_HY)),
                  pl.BlockSpec((1, TT, 128), cur(PF_GM)),
                  pl.BlockSpec((3, 512), const), pl.BlockSpec((3, 768), const), pl.BlockSpec((1, 128), const),
                  pl.BlockSpec((TT, 512), lambda t, b: (t, 0)), pl.BlockSpec((TT, 512), lambda t, b: (t, 0)),
                  pl.BlockSpec((TT, TT), const), pl.BlockSpec((TT, TT), const)],
        out_specs=[pl.BlockSpec((1, TT, 512), lambda t, b: (b, t, 0)),
                   pl.BlockSpec((1, TT, 768), lambda t, b: (b, t, 0)),
                   pl.BlockSpec((1, 2, TT, 128), lambda t, b: (b, 0, t, 0))],
        out_shape=[jax.ShapeDtypeStruct((nb, ltot, 512), BF16),
                   jax.ShapeDtypeStruct((nb, ltot, 768), F32),
                   jax.ShapeDtypeStruct((nb, 2, ltot, 128), F32)],
        compiler_params=_params(("arbitrary", "arbitrary")), name="conv_rope_gates",
    )(pf, pf, pf, pf, pf, pf, pf, conv_ml, conv_hy, bif_row, cos_t, sin_t, tril, triu)


def _na_kernel(q_ref, k0_ref, k1_ref, k2_ref, v0_ref, v1_ref, v2_ref, kc_ref, vc_ref, za_ref, tab_ref, o_ref):
    lane = lax.broadcasted_iota(jnp.int32, (1, 128), 1)
    k_refs = (k0_ref, k1_ref, k2_ref)
    v_refs = (v0_ref, v1_ref, v2_ref)

    def row_max(s):
        return jnp.max(jnp.maximum(s[:, :128], s[:, 128:]), axis=-1, keepdims=True)

    def pair(hp, carry):
        sl = pl.ds(pl.multiple_of(hp * 128, 128), 128)
        q2 = q_ref[0, :, sl]
        out = None
        for e in range(2):
            hm = (lane >= HEAD_DIM) if e else (lane < HEAD_DIM)
            qm = jnp.where(hm, q2, jnp.zeros_like(q2))

            sc = [_dot_nt(qm, kc_ref[0, :, sl])]
            sc += [_dot_nt(qm, k_refs[blk][0, :, sl]) + tab_ref[0, 2 * hp + e, :, blk * TT:(blk + 1) * TT]
                   for blk in range(3)]
            m = row_max(jnp.maximum(jnp.maximum(sc[0], sc[1]), jnp.maximum(sc[2], sc[3])))
            p = jnp.concatenate([jnp.exp(s - m).astype(BF16) for s in sc], axis=1)
            v = jnp.concatenate([vc_ref[0, :, sl]] + [r[0, :, sl] for r in v_refs], axis=0)
            acc = _dot(p, jnp.where(hm, v, jnp.ones_like(v)))
            o = jnp.where(hm, acc / pltpu.roll(acc, HEAD_DIM, 1), 0.0)
            out = o if e == 0 else out + o
        o_ref[0, :, sl] = (out * _silu(za_ref[0, :, sl])).astype(BF16)
        return carry

    lax.fori_loop(0, H_NA // 2, pair, 0, unroll=True)


def _attention(ona, pf, table, nt, with_ctx_queries):
    nb, ltot, _ = ona.shape
    nj = nt + 1 if with_ctx_queries else nt

    def kv(col, dj):
        return lambda j, b: (b, jnp.clip(j + dj, 0, nt - 1), col)

    def pat(j, b):
        p = jnp.where(j == 0, 0, jnp.where(j == nt - 1, 2, jnp.where(j == nt, 3, 1)))
        return (p, 0, 0, 0)

    return pl.pallas_call(
        _na_kernel, grid=(nj, nb),
        in_specs=[pl.BlockSpec((1, TT, 512), lambda j, b: (b, j, 0)),
                  pl.BlockSpec((1, TT, 512), kv(1, -1)), pl.BlockSpec((1, TT, 512), kv(1, 0)),
                  pl.BlockSpec((1, TT, 512), kv(1, 1)),
                  pl.BlockSpec((1, TT, 512), kv(2, -1)), pl.BlockSpec((1, TT, 512), kv(2, 0)),
                  pl.BlockSpec((1, TT, 512), kv(2, 1)),
                  pl.BlockSpec((1, TT, 512), lambda j, b: (b, nt, 1)),
                  pl.BlockSpec((1, TT, 512), lambda j, b: (b, nt, 2)),
                  pl.BlockSpec((1, TT, 512), lambda j, b: (b, j, PF_ZA)),
                  pl.BlockSpec((1, H_NA, TT, 3 * TT), pat)],
        out_specs=pl.BlockSpec((1, TT, 512), lambda j, b: (b, j, 0)),
        out_shape=jax.ShapeDtypeStruct((nb, ltot, 512), BF16),
        compiler_params=_params(("arbitrary", "arbitrary")), name="nbr_attention",
    )(ona, ona, ona, ona, ona, ona, ona, ona, ona, pf, table)


def _na_bias_table(rpb):
    a = np.arange(TT) // GRID_W
    qc = np.arange(TT) % GRID_W
    kidx = np.arange(3 * TT)
    kr = (kidx // TT - 1) * BAND_ROWS + (kidx % TT) // GRID_W
    kcol = kidx % GRID_W
    qstart = np.clip(qc - NA_COLS // 2, 0, GRID_W - NA_COLS)
    col_ok = (kcol[None, :] >= qstart[:, None]) & (kcol[None, :] < qstart[:, None] + NA_COLS)
    coff = np.clip(kcol[None, :] - qc[:, None], -(NA_COLS - 1), NA_COLS - 1) + NA_COLS - 1
    roff = np.clip(kr[None, :] - a[:, None] + NA_ROWS - 1, 0, 2 * NA_ROWS - 2)
    half = NA_ROWS // 2
    row_ok = [
        (kr[None, :] >= 0) & (kr[None, :] < NA_ROWS) & (a[:, None] >= 0),
        (kr[None, :] >= a[:, None] - half) & (kr[None, :] < a[:, None] + half),
        (kr[None, :] >= BAND_ROWS - NA_ROWS) & (kr[None, :] < BAND_ROWS) & (a[:, None] >= 0),
    ]
    nr, ncol = 2 * NA_ROWS - 1, 2 * NA_COLS - 1
    onehot = (coff[:GRID_W, :GRID_W].reshape(1, -1) == np.arange(ncol)[:, None]).astype(np.float32)
    cmat = jnp.dot(rpb.astype(F32).reshape(H_NA * nr, ncol), onehot,
                   precision=lax.Precision.HIGHEST).reshape(H_NA, nr, GRID_W, GRID_W)
    nkr = 3 * BAND_ROWS
    blocks = [jnp.concatenate([cmat[:, roff[ai * GRID_W, kj * GRID_W]] for kj in range(nkr)], axis=-1)
              for ai in range(BAND_ROWS)]
    bias = jnp.concatenate(blocks, axis=1)
    tabs = [jnp.where((r & col_ok)[None], bias, NEG) for r in row_ok]
    tabs.append(jnp.full_like(bias, NEG))
    return jnp.stack(tabs, axis=0)


def _head_lane_sums(x, lane_h):
    out = jnp.zeros_like(x)
    half = lax.broadcasted_iota(jnp.int32, (1, 128), 1) // HEAD_DIM
    for h in range(H_ML):
        t = x[:, (h // 2) * 128:(h // 2 + 1) * 128]
        sm = jnp.sum(jnp.where(half == h % 2, t, 0.0), axis=-1, keepdims=True)
        out = jnp.where(lane_h == h, sm, out)
    return out


def _scan_kernel(q_ref, k_ref, v_ref, g_ref, e_ref, bm_ref, o_ref, w_ref, n_ref, m_ref):
    s = pl.program_id(2)
    sgn = 1 - 2 * pl.program_id(1)

    @pl.when(s == 0)
    def _():
        w_ref[...] = jnp.zeros_like(w_ref)
        n_ref[...] = jnp.zeros_like(n_ref)
        m_ref[...] = jnp.zeros_like(m_ref)

    lane_h = lax.broadcasted_iota(jnp.int32, (1, D_ML), 1) // HEAD_DIM
    ri = lax.broadcasted_iota(jnp.int32, (TT, TT), 0)
    ci = lax.broadcasted_iota(jnp.int32, (TT, TT), 1)
    tri = ((ci - ri) * sgn) <= 0
    gate_lanes = lax.broadcasted_iota(jnp.int32, (1, 128), 1) < H_ML

    for i in range(SCAN_NB):
        gi = g_ref[i, 0]
        a = jnp.where(gate_lanes, gi, 0.0)
        bb = jnp.where(gate_lanes, pltpu.roll(gi, 128 - H_ML, 1), 0.0)
        m_prev = m_ref[i]
        mc = jnp.maximum(m_prev, jnp.max(a, axis=0, keepdims=True))
        small = jnp.concatenate([jnp.exp(a - mc), jnp.exp(-bb - mc),
                                 jnp.broadcast_to(jnp.exp(m_prev - mc), (8, 128))], axis=0)
        sh = small.astype(BF16)
        sl = (small - sh.astype(F32)).astype(BF16)
        ex = _dot(sh, e_ref[...]) + _dot(sl, e_ref[...])
        w_e = ex[0:TT]
        eb_e = ex[TT:2 * TT]
        beta = ex[2 * TT:2 * TT + 1]

        q = q_ref[i]
        kwf = k_ref[i].astype(F32) * w_e
        kw = kwf.astype(BF16)
        vb = v_ref[i].astype(BF16)
        intra = None
        den_intra = jnp.zeros((TT, D_ML), F32)
        for h in range(H_ML):
            qh = jnp.where(lane_h == h, q, jnp.zeros_like(q))
            sc = jnp.where(tri, _dot_nt(qh, kw), 0.0)
            den_intra = jnp.where(lane_h == h, jnp.sum(sc, axis=-1, keepdims=True), den_intra)
            vh = jnp.where(lane_h == h, vb, jnp.zeros_like(vb))
            part = _dot(sc.astype(BF16), vh)
            intra = part if intra is None else intra + part
        num = beta * _dot(q, w_ref[i].astype(BF16)) + intra
        den = beta * _head_lane_sums(q.astype(F32) * n_ref[i], lane_h) + den_intra
        o_ref[i, 0] = num / jnp.maximum(jnp.abs(den), eb_e)

        w_ref[i] = beta * w_ref[i] + bm_ref[...] * _dot_tn(kw, vb)
        n_ref[i] = beta * n_ref[i] + jnp.sum(kwf, axis=0, keepdims=True)
        m_ref[i] = jnp.min(bb, axis=0, keepdims=True) + mc


def _mlstm_scan(qk, pf, gates, emat, bmask, nt):
    nb, ltot, _ = qk.shape
    g = SCAN_NB

    def chunk(s, d):
        return jnp.where(s == 0, nt, jnp.where(d == 0, s - 1, nt - s))

    const = lambda b, d, s: (0, 0)
    return pl.pallas_call(
        _scan_kernel, grid=(nb // g, 2, nt + 1),
        in_specs=[pl.BlockSpec((g, TT, 256), lambda b, d, s: (b, chunk(s, d), 0)),
                  pl.BlockSpec((g, TT, 256), lambda b, d, s: (b, chunk(s, d), 1)),
                  pl.BlockSpec((g, TT, 256), lambda b, d, s: (b, chunk(s, d), PF_VM)),
                  pl.BlockSpec((g, 1, TT, 128), lambda b, d, s: (b, d, chunk(s, d), 0)),
                  pl.BlockSpec((128, 256), const), pl.BlockSpec((256, 256), const)],
        out_specs=pl.BlockSpec((g, 1, TT, 256), lambda b, d, s: (b, d, chunk(s, d), 0)),
        out_shape=jax.ShapeDtypeStruct((nb, 2, ltot, 256), F32),
        scratch_shapes=[pltpu.VMEM((g, D_ML, D_ML), F32), pltpu.VMEM((g, 1, D_ML), F32),
                        pltpu.VMEM((g, 1, 128), F32)],
        compiler_params=_params(("arbitrary", "arbitrary", "arbitrary")), name="mlstm_scan",
    )(qk, qk, pf, gates, emat, bmask)


def _hyfilt_kernel(tl, length, feat_ref, w1_ref, b1_ref, w2_ref, b2_ref, w3_ref, fq_ref, dec_ref, o_ref, ss_ref):
    i = pl.program_id(0)
    a = jnp.sin(fq_ref[0:1] * (_dot_f32(feat_ref[...], w1_ref[...]) + b1_ref[...]))
    a = jnp.sin(fq_ref[1:2] * (_dot_f32(a, w2_ref[...]) + b2_ref[...]))
    dec = dec_ref[...]
    f = _dot_f32(a, w3_ref[...]) * jnp.concatenate([dec, dec, dec, dec], axis=1)
    fwd = jnp.concatenate([f[:, 0:D_HY], f[:, 2 * D_HY:3 * D_HY]], axis=1)
    bwd = jnp.concatenate([f[:, D_HY:2 * D_HY], f[:, 3 * D_HY:4 * D_HY]], axis=1)
    row = lax.broadcasted_iota(jnp.int32, (tl, 1), 0) + i * tl
    k2 = jnp.where(row < length, fwd, jnp.where(row > length, bwd, 0.0))
    o_ref[...] = k2

    @pl.when(i == 0)
    def _():
        ss_ref[...] = jnp.zeros_like(ss_ref)

    ss_ref[...] += jnp.sum(k2 * k2, axis=0, keepdims=True)


def _hyena_filters(length, w1, b1, w2, b2, w3, freq):
    t = np.arange(length, dtype=np.float32)
    tn = (t / np.float32(length - 1)).astype(np.float32)
    bands = (HY_EMB - 1) // 2
    fr = np.linspace(1e-4, bands - 1, bands, dtype=np.float32)
    ang = (np.float32(2.0 * math.pi / length) * t[:, None] * fr[None, :]).astype(np.float32)
    feat = np.zeros((length, 128), np.float32)
    feat[:, 0] = tn
    feat[:, 1:1 + bands] = np.cos(ang)
    feat[:, 1 + bands:1 + 2 * bands] = -np.sin(ang)
    deltas = np.abs(np.linspace(math.log(HY_TARGET) / HY_SLOW, math.log(HY_TARGET) / HY_FAST, D_HY, dtype=np.float32))
    dec = np.exp(-tn[:, None] * deltas[None, :]).astype(np.float32)
    lag = np.concatenate([np.arange(length), [0], np.arange(length - 1, 0, -1)])
    feat, dec = feat[lag], dec[lag]

    pad2 = lambda m, r, c: jnp.zeros((r, c), F32).at[:m.shape[0], :m.shape[1]].set(m.astype(F32))
    w1p = pad2(w1, 128, 128)
    w2p = pad2(w2, 128, 128)
    w3p = pad2(w3, 128, 4 * D_HY)
    b1p = pad2(b1[None], 1, 128)
    b2p = pad2(b2[None], 1, 128)
    fqp = pad2(freq, 2, 128)
    tl = 512
    const = lambda i: (0, 0)
    return pl.pallas_call(
        functools.partial(_hyfilt_kernel, tl, length), grid=(2 * length // tl,),
        in_specs=[pl.BlockSpec((tl, 128), lambda i: (i, 0)),
                  pl.BlockSpec((128, 128), const), pl.BlockSpec((1, 128), const),
                  pl.BlockSpec((128, 128), const), pl.BlockSpec((1, 128), const),
                  pl.BlockSpec((128, 4 * D_HY), const), pl.BlockSpec((2, 128), const),
                  pl.BlockSpec((tl, D_HY), lambda i: (i, 0))],
        out_specs=[pl.BlockSpec((tl, 2 * D_HY), lambda i: (i, 0)), pl.BlockSpec((1, 2 * D_HY), const)],
        out_shape=[jax.ShapeDtypeStruct((2 * length, 2 * D_HY), F32), jax.ShapeDtypeStruct((1, 2 * D_HY), F32)],
        compiler_params=_params(("arbitrary",)), name="hyena_filter_mlp",
    )(feat, w1p, b1p, w2p, b2p, w3p, fqp, dec)


@functools.lru_cache(maxsize=None)
def _fft_tables(length):
    n = 2 * length
    n1 = n // FFT_N2
    nh = n1 // 2
    k1 = np.arange(n1)[None, :, None].astype(np.float64)
    j = np.arange(nh)[None, None, :].astype(np.float64)
    n2 = np.arange(FFT_N2)[:, None, None].astype(np.float64)
    ang = -2.0 * np.pi * k1 * (FFT_N2 * j + n2) / n
    gf = np.concatenate([np.cos(ang), np.sin(ang)], axis=1)
    gi = np.concatenate([np.cos(ang).transpose(0, 2, 1), np.sin(ang).transpose(0, 2, 1)], axis=1)
    kk = np.arange(FFT_N2)[:, None] * np.arange(FFT_N2)[None, :]
    a2 = -2.0 * np.pi * kk / FFT_N2
    f2 = np.concatenate([np.cos(a2), np.sin(a2)], axis=0)
    cv = lambda m: np.asarray(m, np.float32).astype(BF16)
    return cv(gf), cv(gi), cv(f2), n1


def _stage1(n1, z_of, gf_ref, scr_ref, cols):
    nh = n1 // 2

    def body(n2, c):
        r = _dot(gf_ref[n2], z_of(n2))
        off = pl.multiple_of(n2 * (n1 + FFT_PAD), 8)
        if cols == 256:
            scr_ref[0, pl.ds(off, n1), :] = r[:n1, :128] - r[n1:, 128:]
            scr_ref[1, pl.ds(off, n1), :] = r[n1:, :128] + r[:n1, 128:]
        else:
            scr_ref[0, pl.ds(off, n1), :] = r[:n1]
            scr_ref[1, pl.ds(off, n1), :] = r[n1:]
        return c

    lax.fori_loop(0, FFT_N2, body, 0, unroll=FFT_UNROLL)
    del nh


def _stage2(n1, k1, f2_ref, scr_ref):
    ar = scr_ref[0, pl.ds(k1, FFT_N2, stride=n1 + FFT_PAD), :]
    ai = scr_ref[1, pl.ds(k1, FFT_N2, stride=n1 + FFT_PAD), :]
    r = _dot(f2_ref[...], jnp.concatenate([ar, ai], axis=1).astype(BF16))
    return r[:128, :128] - r[128:, 128:], r[128:, :128] + r[:128, 128:]


def _hyconv_kernel(n1, kg, z_ref, gf_ref, f2_ref, kf_ref, gi_ref, o_ref, scr_ref):
    s = pl.program_id(2)
    nh = n1 // 2

    @pl.when(s == 0)
    def _():
        def z_of(n2):
            zr = z_ref[0, 0, pl.ds(n2, nh, stride=FFT_N2), :]
            zi = z_ref[0, 1, pl.ds(n2, nh, stride=FFT_N2), :]
            return jnp.concatenate([zr, zi], axis=1).astype(BF16)
        _stage1(n1, z_of, gf_ref, scr_ref, 256)

    def k1_body(i, c):
        k1 = s * kg + i
        xr, xi = _stage2(n1, k1, f2_ref, scr_ref)
        off = pl.multiple_of(i * FFT_N2, FFT_N2)
        kr = kf_ref[0, pl.ds(off, FFT_N2), :]
        ki = kf_ref[1, pl.ds(off, FFT_N2), :]
        yc = jnp.concatenate([xr * kr - xi * ki, xr * ki + xi * kr], axis=1).astype(BF16)
        r = _dot(f2_ref[...], yc)
        scr_ref[0, pl.ds(k1, FFT_N2, stride=n1 + FFT_PAD), :] = r[:128, :128] + r[128:, 128:]
        scr_ref[1, pl.ds(k1, FFT_N2, stride=n1 + FFT_PAD), :] = r[:128, 128:] - r[128:, :128]
        return c

    lax.fori_loop(0, kg, k1_body, 0, unroll=FFT_UNROLL)

    @pl.when(s == pl.num_programs(2) - 1)
    def _():
        def body(n2, c):
            off = pl.multiple_of(n2 * (n1 + FFT_PAD), 8)
            bc = jnp.concatenate([scr_ref[0, pl.ds(off, n1), :], scr_ref[1, pl.ds(off, n1), :]], axis=1)
            r = _dot(gi_ref[n2], bc.astype(BF16))
            o_ref[0, 0, pl.ds(n2, nh, stride=FFT_N2), :] = r[:nh, :128] + r[nh:, 128:]
            o_ref[0, 1, pl.ds(n2, nh, stride=FFT_N2), :] = r[:nh, 128:] - r[nh:, :128]
            return c
        lax.fori_loop(0, FFT_N2, body, 0, unroll=FFT_UNROLL)


def _hyena_conv(z4, col0, kf, length):
    gf, gi, f2, n1 = _fft_tables(length)
    n = 2 * length
    kg = min(n1, 16)
    npair = z4.shape[0]
    big = pl.Buffered(1)
    return pl.pallas_call(
        functools.partial(_hyconv_kernel, n1, kg), grid=(npair, 2, n1 // kg),
        in_specs=[pl.BlockSpec((1, 2, length, 128), lambda p, c, s: (p, 0, 0, col0 + c), pipeline_mode=big),
                  pl.BlockSpec(gf.shape, lambda p, c, s: (0, 0, 0), pipeline_mode=big),
                  pl.BlockSpec(f2.shape, lambda p, c, s: (0, 0)),
                  pl.BlockSpec((2, kg * FFT_N2, 128), lambda p, c, s: (0, s, c)),
                  pl.BlockSpec(gi.shape, lambda p, c, s: (0, 0, 0), pipeline_mode=big)],
        out_specs=pl.BlockSpec((1, 2, length, 128), lambda p, c, s: (p, 0, 0, c), pipeline_mode=big),
        out_shape=jax.ShapeDtypeStruct((npair, 2, length, 256), F32),
        scratch_shapes=[pltpu.VMEM((2, FFT_N2 * (n1 + FFT_PAD), 128), F32)],
        compiler_params=_params(("arbitrary", "arbitrary", "arbitrary")), name="hyena_fftconv",
    )(z4, gf, f2, kf, gi)


def _hyfft_kernel(n1, kg, z_ref, gf_ref, f2_ref, o_ref, scr_ref):
    s = pl.program_id(2)

    @pl.when(s == 0)
    def _():
        _stage1(n1, lambda n2: z_ref[pl.ds(n2, n1, stride=FFT_N2), :].astype(BF16), gf_ref, scr_ref, 128)

    def k1_body(i, c):
        xr, xi = _stage2(n1, s * kg + i, f2_ref, scr_ref)
        off = pl.multiple_of(i * FFT_N2, FFT_N2)
        o_ref[0, 0, pl.ds(off, FFT_N2), :] = xr
        o_ref[0, 1, pl.ds(off, FFT_N2), :] = xi
        return c

    lax.fori_loop(0, kg, k1_body, 0, unroll=FFT_UNROLL)


@functools.lru_cache(maxsize=None)
def _fft_table_full(length):
    n = 2 * length
    n1 = n // FFT_N2
    k1 = np.arange(n1)[None, :, None].astype(np.float64)
    j = np.arange(n1)[None, None, :].astype(np.float64)
    n2 = np.arange(FFT_N2)[:, None, None].astype(np.float64)
    ang = -2.0 * np.pi * k1 * (FFT_N2 * j + n2) / n
    return np.concatenate([np.cos(ang), np.sin(ang)], axis=1).astype(np.float32).astype(BF16)


def _filter_spectrum(k2, length):
    _, _, f2, n1 = _fft_tables(length)
    gf = _fft_table_full(length)
    n = 2 * length
    kg = min(n1, 16)
    nf = k2.shape[1] // D_HY
    big = pl.Buffered(1)
    return pl.pallas_call(
        functools.partial(_hyfft_kernel, n1, kg), grid=(nf, 2, n1 // kg),
        in_specs=[pl.BlockSpec((n, 128), lambda f, c, s: (0, 2 * f + c), pipeline_mode=big),
                  pl.BlockSpec(gf.shape, lambda f, c, s: (0, 0, 0), pipeline_mode=big),
                  pl.BlockSpec(f2.shape, lambda f, c, s: (0, 0))],
        out_specs=pl.BlockSpec((1, 2, kg * FFT_N2, 128), lambda f, c, s: (f, 0, s, c)),
        out_shape=jax.ShapeDtypeStruct((nf, 2, n, 256), F32),
        scratch_shapes=[pltpu.VMEM((2, FFT_N2 * (n1 + FFT_PAD), 128), F32)],
        compiler_params=_params(("arbitrary", "arbitrary", "arbitrary")), name="hyena_filter_fft",
    )(k2, gf, f2)


def _normalised(k2, ssq, extra_scale):
    return k2 * (lax.rsqrt(ssq + EPS) * extra_scale)


def _hy_mid_kernel(y_ref, hv_ref, b_ref, o_ref):
    v = hv_ref[0][:, 0:D_HY]
    x1 = hv_ref[0][:, D_HY:2 * D_HY]
    o_ref[0] = x1 * (y_ref[0] + v * b_ref[...])


def _hy_mid(y1, hv, bias0):
    nb, length, _ = y1.shape
    te = 4 * TT
    return pl.pallas_call(
        _hy_mid_kernel, grid=(nb, length // te),
        in_specs=[pl.BlockSpec((1, te, D_HY), lambda b, t: (b, t, 0)),
                  pl.BlockSpec((1, te, 512), lambda b, t: (b, t, 0)),
                  pl.BlockSpec((1, D_HY), lambda b, t: (0, 0))],
        out_specs=pl.BlockSpec((1, te, D_HY), lambda b, t: (b, t, 0)),
        out_shape=jax.ShapeDtypeStruct(y1.shape, F32),
        compiler_params=_params(("arbitrary", "arbitrary")), name="hyena_mid",
    )(y1, hv, bias0.reshape(1, D_HY))


def _hy_ctx_kernel(lc, hv_ref, zh_ref, k2_ref, b_ref, fc_ref, ff_ref, ci_ref, o_ref):
    nf = 2 * lc
    hv = hv_ref[0]

    def conv(z, o):
        kf = _dot_exact_rhs(ff_ref[...], k2_ref[:, o * D_HY:(o + 1) * D_HY])
        zs = _dot(fc_ref[...], z.astype(BF16))
        zr, zi = zs[:nf], zs[nf:]
        kr, ki = kf[:nf], kf[nf:]
        ys = jnp.concatenate([zr * kr - zi * ki, zr * ki + zi * kr], axis=0).astype(BF16)
        return _dot(ci_ref[...], ys) * (1.0 / nf)

    v = hv[:, 0:D_HY]
    x1 = hv[:, D_HY:2 * D_HY]
    x2 = hv[:, 2 * D_HY:3 * D_HY]
    z1 = x1 * (conv(v, 0) + v * b_ref[0:1])
    p = x2 * (conv(z1, 1) + z1 * b_ref[1:2])
    o_ref[0] = (p * _silu(zh_ref[0])).astype(BF16)


def _dot_exact_rhs(a_bf16, b):
    bh, bm, bl = _split3(b)
    return _dot(a_bf16, bh) + _dot(a_bf16, bm) + _dot(a_bf16, bl)


@functools.lru_cache(maxsize=None)
def _ctx_dft_tables(lc):
    nf = 2 * lc
    k = np.arange(nf)[:, None].astype(np.float64)
    ang = -2.0 * np.pi * k * np.arange(nf)[None, :] / nf
    ff = np.concatenate([np.cos(ang), np.sin(ang)], axis=0)
    fc = ff[:, :lc]
    a2 = 2.0 * np.pi * np.arange(lc)[:, None] * np.arange(nf)[None, :] / nf
    ci = np.concatenate([np.cos(a2), -np.sin(a2)], axis=1)
    cv = lambda m: np.asarray(m, np.float32).astype(BF16)
    return cv(fc), cv(ff), cv(ci)


def _hy_ctx(hv, pf, k2c, hy_bias, nt, lc):
    nb = hv.shape[0]
    fc, ff, ci = _ctx_dft_tables(lc)
    c2 = lambda b: (0, 0)
    return pl.pallas_call(
        functools.partial(_hy_ctx_kernel, lc), grid=(nb,),
        in_specs=[pl.BlockSpec((1, lc, 768), lambda b: (b, nt, 0)),
                  pl.BlockSpec((1, lc, D_HY), lambda b: (b, nt, PF_ZH)),
                  pl.BlockSpec(k2c.shape, c2),
                  pl.BlockSpec((2, D_HY), c2),
                  pl.BlockSpec(fc.shape, c2), pl.BlockSpec(ff.shape, c2), pl.BlockSpec(ci.shape, c2)],
        out_specs=pl.BlockSpec((1, lc, D_HY), lambda b: (b, 0, 0)),
        out_shape=jax.ShapeDtypeStruct((nb, lc, D_HY), BF16),
        compiler_params=_params(("arbitrary",)), name="hyena_ctx",
    )(hv, pf, k2c, hy_bias, fc, ff, ci)


def _out_kernel(n_hy, na_ref, h0_ref, h1_ref, om_ref, zm_ref, w_ref, x_ref, mod_ref, g_ref, *rest):
    hy_refs, o_ref = rest[:n_hy], rest[n_hy]
    if n_hy == 1:
        hy = hy_refs[0][0]
    else:
        y_ref, z_ref, x2_ref, zh_ref, b_ref = hy_refs
        z = z_ref[0]
        hy = (x2_ref[0] * (y_ref[0] + z * b_ref[...]) * _silu(zh_ref[0])).astype(BF16)
    ml = jax.nn.sigmoid(om_ref[0]) * (h0_ref[0, 0] + h1_ref[0, 0]) * _silu(zm_ref[0])
    y = (_dot(na_ref[0], w_ref[0:D_NA]) + _dot(ml.astype(BF16), w_ref[D_NA:D_NA + D_ML])
         + _dot(hy, w_ref[D_NA + D_ML:D_MODEL]))
    ms = jnp.mean(y * y, axis=-1, keepdims=True)
    r = y * lax.rsqrt(ms + EPS) * g_ref[...]
    o_ref[0] = x_ref[0] + mod_ref[0][2:3] * r


def _out_projection(na_g, hml, pf, hy_args, w_out, res, res_t0, mod3, g_post, to, t0, ntiles, mod_row, out_rows,
                    prev=None):
    nb = na_g.shape[0]
    d = D_MODEL
    tok = lambda col: (lambda b, t: (b, t + t0, col))
    own = lambda col: (lambda b, t: (b, t, col))
    in_specs = [pl.BlockSpec((1, to, 512), tok(0)),
                pl.BlockSpec((1, 1, to, 256), lambda b, t: (b, 0, t + t0, 0)),
                pl.BlockSpec((1, 1, to, 256), lambda b, t: (b, 1, t + t0, 0)),
                pl.BlockSpec((1, to, 256), tok(PF_OM)), pl.BlockSpec((1, to, 256), tok(PF_ZM)),
                pl.BlockSpec((d, d), lambda b, t: (0, 0), pipeline_mode=pl.Buffered(1)),
                pl.BlockSpec((1, to, d), lambda b, t: (b, t + res_t0, 0)),
                pl.BlockSpec((1, 3, d), lambda b, t: (mod_row(b), 0, 0)),
                pl.BlockSpec((1, d), lambda b, t: (0, 0))]
    args = [na_g, hml, hml, pf, pf, w_out, res, mod3, g_post.reshape(1, d)]
    if len(hy_args) == 1:
        in_specs.append(pl.BlockSpec((1, to, D_HY), own(0)))
        args.append(hy_args[0])
    else:
        y2, z1, hv, pf_, bias1 = hy_args
        in_specs += [pl.BlockSpec((1, to, D_HY), own(0)), pl.BlockSpec((1, to, D_HY), own(0)),
                     pl.BlockSpec((1, to, D_HY), tok(2)), pl.BlockSpec((1, to, D_HY), tok(PF_ZH)),
                     pl.BlockSpec((1, D_HY), lambda b, t: (0, 0))]
        args += [y2, z1, hv, pf_, bias1.reshape(1, D_HY)]
    n_hy = len(hy_args)
    aliases = {}
    if prev is not None:
        in_specs.append(pl.BlockSpec(memory_space=pl.ANY))
        args.append(prev)
        aliases = {len(args) - 1: 0}
        kern = lambda *r: _out_kernel(n_hy, *r[:9 + n_hy], r[10 + n_hy])
    else:
        kern = functools.partial(_out_kernel, n_hy)
    return pl.pallas_call(
        kern, grid=(nb, ntiles), in_specs=in_specs,
        out_specs=pl.BlockSpec((1, to, d), tok(0)),
        out_shape=jax.ShapeDtypeStruct((nb, out_rows, d), F32),
        input_output_aliases=aliases,
        compiler_params=_params(("arbitrary", "arbitrary")), name="out_proj_norm_residual",
    )(*args)


@functools.lru_cache(maxsize=None)
def _rope_tables(length, lc):
    n = HEAD_DIM // 4
    t = np.arange(length)
    inv = (ROPE_BASE ** (-np.arange(n, dtype=np.float32) / n)).astype(np.float32)
    pos = np.stack([t // GRID_W, t % GRID_W], axis=-1).astype(np.float32)
    ang = (pos[:, :, None] * inv).astype(np.float32)
    cos_h = np.concatenate([np.cos(ang[:, 0]), np.cos(ang[:, 0]), np.cos(ang[:, 1]), np.cos(ang[:, 1])], axis=-1)
    sin_h = np.concatenate([-np.sin(ang[:, 0]), np.sin(ang[:, 0]), -np.sin(ang[:, 1]), np.sin(ang[:, 1])], axis=-1)
    cos_t = np.concatenate([np.tile(cos_h, (1, 2 * H_ML)), np.ones((lc, 512), np.float32)], axis=0)
    sin_t = np.concatenate([np.tile(sin_h, (1, 2 * H_ML)), np.zeros((lc, 512), np.float32)], axis=0)
    kscale = np.concatenate([np.ones(256, np.float32), np.full(256, HEAD_DIM ** -0.5, np.float32)])
    return (cos_t * kscale).astype(np.float32), (sin_t * kscale).astype(np.float32)


@functools.lru_cache(maxsize=None)
def _scan_constants():
    tril = np.tril(np.ones((TT, TT), np.float32))
    emat = np.zeros((128, D_ML), np.float32)
    for h in range(H_ML):
        emat[h, h * HEAD_DIM:(h + 1) * HEAD_DIM] = 1.0
    rh = np.arange(D_ML)[:, None] // HEAD_DIM
    chd = np.arange(D_ML)[None, :] // HEAD_DIM
    bmask = (rh == chd).astype(np.float32)
    return tril.astype(BF16), np.ascontiguousarray(tril.T).astype(BF16), emat.astype(BF16), bmask


def _layer(x_lat, x_ctx, ctx_tile, mod3, nb, length, lc, g_pre, g_post, w_in, b_if, conv_ml, conv_hy, rpb,
           hf_w1, hf_b1, hf_w2, hf_b2, hf_w3, hf_freq, hy_bias, w_out, update_ctx):
    nt = length // TT
    ltot = length + lc
    tril, triu, emat, bmask = _scan_constants()
    cos_t, sin_t = _rope_tables(length, lc)

    c = lambda a, b: w_in[:, a:b]
    wna = jnp.concatenate([c(0, 512) * (HEAD_DIM ** -0.5), c(512, 1536)], axis=1).astype(BF16)
    wpf = jnp.concatenate([c(3344, 4112), c(2560, 2816), c(1536, 2048), c(2048, 2560), c(2816, 3072),
                           c(3072, 3328), c(4112, 4368), c(3328, 3344), jnp.zeros((D_MODEL, 112), F32)],
                          axis=1).astype(BF16)
    ona, pf = _projection(x_lat, x_ctx, ctx_tile, mod3, g_pre, wna, wpf, nt)

    na_g = _attention(ona, pf, _na_bias_table(rpb), nt, update_ctx)

    bif_row = jnp.zeros((1, 128), F32).at[0, :4 * H_ML].set(b_if.reshape(-1))
    qk, hv, gates = _pre(pf, conv_ml, conv_hy, bif_row, cos_t, sin_t, tril, triu, nt)
    hml = _mlstm_scan(qk, pf, gates, emat, bmask, nt)

    k2, ssq = _hyena_filters(length, hf_w1, hf_b1, hf_w2, hf_b2, hf_w3, hf_freq)
    kf = _filter_spectrum(_normalised(k2, ssq, 1.0 / (2 * length)), length)
    hv4 = hv.reshape(nb // 2, 2, ltot, 768)
    y1 = _hyena_conv(hv4, 0, kf[0], length).reshape(nb, length, D_HY)
    z1 = _hy_mid(y1, hv, hy_bias[0])
    y2 = _hyena_conv(z1.reshape(nb // 2, 2, length, D_HY), 0, kf[1], length).reshape(nb, length, D_HY)
    hy_args = (y2, z1, hv, pf, hy_bias[1])
    to = 2 * TT
    if not update_ctx:
        return _out_projection(na_g, hml, pf, hy_args, w_out, x_lat, 0, mod3, g_post, to, 0, length // to,
                               lambda b: b, length)

    xn = _out_projection(na_g, hml, pf, hy_args, w_out, x_lat, 0, mod3, g_post, to, 0, length // to,
                         lambda b: b, ltot)
    k2c, ssq_c = _hyena_filters(lc, hf_w1, hf_b1, hf_w2, hf_b2, hf_w3, hf_freq)
    hyc_g = _hy_ctx(hv, pf, _normalised(k2c, ssq_c, 1.0), hy_bias, nt, lc)
    return _out_projection(na_g, hml, pf, (hyc_g,), w_out, x_ctx, ctx_tile, mod3, g_post, TT, nt, 1,
                           lambda b: nb, ltot, prev=xn)


def kernel(x, c, ctx, c_ctx, w_ada, b_ada, g_pre, g_post, w_in, b_if, conv_ml, conv_hy, rpb, hf_w1, hf_b1,
           hf_w2, hf_b2, hf_w3, hf_freq, hy_bias, w_out):
    nb, length, d = x.shape
    lc = ctx.shape[1]
    depth = w_in.shape[0]
    assert lc == TT and length % (4 * TT) == 0 and nb % 2 == 0 and d == D_MODEL
    cc = jnp.zeros((16, d), F32).at[:nb].set(c).at[nb].set(c_ctx)
    x_lat, x_ctx, ctx_tile = x, ctx, 0
    for l in range(depth):
        mod3 = _modulation(cc, w_ada[l], b_ada[l]).reshape(16, 3, d)
        x_lat = _layer(x_lat, x_ctx, ctx_tile, mod3, nb, length, lc, g_pre[l], g_post[l], w_in[l], b_if[l],
                       conv_ml[l], conv_hy[l], rpb[l], hf_w1[l], hf_b1[l], hf_w2[l], hf_b2[l], hf_w3[l],
                       hf_freq[l], hy_bias[l], w_out[l].astype(BF16), l < depth - 1)
        x_ctx, ctx_tile = x_lat, length // TT
    return x_lat
```

```python
import functools
import math

import numpy as np
import jax
import jax.numpy as jnp
from jax import lax
from jax.experimental import pallas as pl
from jax.experimental.pallas import tpu as pltpu

F32 = jnp.float32
BF16 = jnp.bfloat16

D_MODEL = 1024
GRID_W = 64
HEAD_DIM = 64
D_NA = 512
D_ML = 256
D_HY = 256
H_NA = 8
H_ML = 4
NA_ROWS = 8
NA_COLS = 16
HY_EMB = 33
HY_FFN = 64
HY_TARGET = 1e-2
HY_FAST = 0.3
HY_SLOW = 1.5
ROPE_BASE = 10000.0
EPS = 1e-6

TT = 256
BAND_ROWS = TT // GRID_W
SCAN_NB = 2
PRE_ROWS = 64
FFT_N2 = 128
FFT_PAD = 8
FFT_UNROLL = 8
NEG = -1e30
VMEM_LIMIT = 56 * 1024 * 1024

PF_ZA = 0
PF_VM, PF_OM, PF_ZM, PF_ZH = 2, 3, 4, 5
LOC_QK, LOC_HY, LOC_GM, LOC_COLS = 0, 512, 1280, 1408
HALO = 8


def _dot(a, b):
    return jnp.dot(a, b, preferred_element_type=F32)


def _dot_nt(a, b):
    return lax.dot_general(a, b, (((1,), (1,)), ((), ())), preferred_element_type=F32)


def _dot_tn(a, b):
    return lax.dot_general(a, b, (((0,), (0,)), ((), ())), preferred_element_type=F32)


def _split3(x):
    h = x.astype(BF16)
    r = x - h.astype(F32)
    m = r.astype(BF16)
    l = (r - m.astype(F32)).astype(BF16)
    return h, m, l


def _dot_f32(a, b):
    ah, am, al = _split3(a)
    bh, bm, bl = _split3(b)
    return (_dot(ah, bh) + (_dot(ah, bm) + _dot(am, bh))
            + (_dot(ah, bl) + _dot(al, bh) + _dot(am, bm)))


def _dot_exact_lhs(a_bf16, b):
    bh, bm, bl = _split3(b)
    return _dot(a_bf16, bh) + _dot(a_bf16, bm) + _dot(a_bf16, bl)


def _silu(x):
    return x * jax.nn.sigmoid(x)


def _params(sem, vmem=VMEM_LIMIT):
    return pltpu.CompilerParams(dimension_semantics=sem, vmem_limit_bytes=vmem)


def _mod_kernel(c_ref, w_ref, b_ref, o_ref):
    o_ref[...] = _dot_f32(_silu(c_ref[...]), w_ref[...]) + b_ref[...]


def _modulation(cc, w_ada, b_ada):
    n = w_ada.shape[1]
    bn = 512
    return pl.pallas_call(
        _mod_kernel, grid=(n // bn,),
        in_specs=[pl.BlockSpec(cc.shape, lambda j: (0, 0)),
                  pl.BlockSpec((w_ada.shape[0], bn), lambda j: (0, j)),
                  pl.BlockSpec((1, bn), lambda j: (0, j))],
        out_specs=pl.BlockSpec((cc.shape[0], bn), lambda j: (0, j)),
        out_shape=jax.ShapeDtypeStruct((cc.shape[0], n), F32),
        compiler_params=_params(("arbitrary",)), name="adaln_mod",
    )(cc, w_ada, b_ada.reshape(1, n))


def _proj_kernel(nt, x_ref, c_ref, xp_ref, xn_ref, mod_ref, g_ref, wna_ref, wpr_ref, wlo_ref, cml_ref, chy_ref,
                 bif_ref, cos_ref, sin_ref, tril_ref, triu_ref, ona_ref, opr_ref, oqk_ref, ohv_ref, og_ref,
                 loc_ref):
    t = pl.program_id(0)
    x = jnp.where(t == nt, c_ref[0], x_ref[0])
    xs = jnp.concatenate([xp_ref[0], x, xn_ref[0]], axis=0)
    ms = jnp.mean(xs * xs, axis=-1, keepdims=True)
    y = xs * lax.rsqrt(ms + EPS) * g_ref[...]
    mod = mod_ref[0]
    h = (y * (1.0 + mod[1:2]) + mod[0:1]).astype(BF16)
    hm = h[HALO:HALO + TT]
    ona_ref[0] = _dot(hm, wna_ref[...]).astype(BF16)
    opr_ref[0] = _dot(hm, wpr_ref[...])
    loc_ref[...] = _dot(h, wlo_ref[...])

    pm = jnp.where(jnp.logical_and(t != 0, t != nt), 1.0, 0.0)
    nm = jnp.where(jnp.logical_and(t != nt - 1, t != nt), 1.0, 0.0)
    row8 = lax.broadcasted_iota(jnp.int32, (8, 1), 0)
    nchunk = TT // PRE_ROWS

    def conv3(cols, w, c):
        r0 = HALO + c * PRE_ROWS
        u = loc_ref[r0:r0 + PRE_ROWS, cols]
        up = loc_ref[r0 - 1:r0, cols]
        un = loc_ref[r0 + PRE_ROWS:r0 + PRE_ROWS + 1, cols]
        if c == 0:
            up = up * pm
        if c == nchunk - 1:
            un = un * nm
        rp = pltpu.roll(u, 1, 0)
        rn = pltpu.roll(u, PRE_ROWS - 1, 0)
        prev = jnp.concatenate([jnp.where(row8 == 0, up, rp[0:8]), rp[8:]], axis=0)
        nxt = jnp.concatenate([rn[:PRE_ROWS - 8], jnp.where(row8 == 7, un, rn[PRE_ROWS - 8:])], axis=0)
        return prev * w[0:1] + u * w[1:2] + nxt * w[2:3]

    lane = lax.broadcasted_iota(jnp.int32, (1, 2 * D_ML), 1)
    first = (lane % 32) < 16
    for c in range(nchunk):
        rows = slice(c * PRE_ROWS, (c + 1) * PRE_ROWS)
        ohv_ref[0, rows] = conv3(slice(LOC_HY, LOC_GM), chy_ref[...], c)
        x = _silu(conv3(slice(LOC_QK, LOC_HY), cml_ref[...], c))
        partner = jnp.where(first, pltpu.roll(x, 2 * D_ML - 16, 1), pltpu.roll(x, 16, 1))
        oqk_ref[0, rows] = (x * cos_ref[rows] + partner * sin_ref[rows]).astype(BF16)

    g = loc_ref[HALO:HALO + TT, LOC_GM:LOC_COLS] + bif_ref[...]
    gl = lax.broadcasted_iota(jnp.int32, (1, 128), 1)
    is_f = jnp.logical_and((gl % 8) >= 4, gl < 16)
    lf = jnp.where(is_f, jnp.minimum(g, 0.0) - jnp.log1p(jnp.exp(-jnp.abs(g))), 0.0)
    bcum = jnp.where(gl < 8, _dot_exact_lhs(tril_ref[...], lf), _dot_exact_lhs(triu_ref[...], lf))
    a = g - pltpu.roll(bcum, 128 - H_ML, 1)
    ab = jnp.where(is_f, bcum, a)
    keep = gl < 2 * H_ML
    og_ref[0, 0] = jnp.where(keep, ab, 0.0)
    og_ref[0, 1] = jnp.where(keep, pltpu.roll(ab, 120, 1), 0.0)


def _projection(x_lat, x_ctx, ctx_tile, mod3, g_pre, wna, wpr, wlo, conv_ml, conv_hy, bif_row, cos_t, sin_t,
                tril, triu, nt):
    nb, lat_rows, d = x_lat.shape
    ltot = (nt + 1) * TT
    ctx_row = nb
    last8 = lat_rows // HALO - 1
    r8 = TT // HALO

    def mod_idx(t, b):
        return (jnp.where(t == nt, ctx_row, b), 0, 0)

    const = lambda t, b: (0, 0)
    big = pl.Buffered(1)
    tok = lambda t, b: (b, t, 0)
    return pl.pallas_call(
        functools.partial(_proj_kernel, nt), grid=(nt + 1, nb),
        in_specs=[pl.BlockSpec((1, TT, d), lambda t, b: (b, jnp.minimum(t, nt - 1), 0)),
                  pl.BlockSpec((1, TT, d), lambda t, b: (b, ctx_tile, 0)),
                  pl.BlockSpec((1, HALO, d), lambda t, b: (b, jnp.clip(t * r8 - 1, 0, last8), 0)),
                  pl.BlockSpec((1, HALO, d), lambda t, b: (b, jnp.minimum((t + 1) * r8, last8), 0)),
                  pl.BlockSpec((1, 3, d), mod_idx),
                  pl.BlockSpec((1, d), const),
                  pl.BlockSpec(wna.shape, const, pipeline_mode=big),
                  pl.BlockSpec(wpr.shape, const, pipeline_mode=big),
                  pl.BlockSpec(wlo.shape, const, pipeline_mode=big),
                  pl.BlockSpec((3, 512), const), pl.BlockSpec((3, 768), const), pl.BlockSpec((1, 128), const),
                  pl.BlockSpec((TT, 512), lambda t, b: (t, 0)), pl.BlockSpec((TT, 512), lambda t, b: (t, 0)),
                  pl.BlockSpec((TT, TT), const), pl.BlockSpec((TT, TT), const)],
        out_specs=[pl.BlockSpec((1, TT, wna.shape[1]), tok), pl.BlockSpec((1, TT, wpr.shape[1]), tok),
                   pl.BlockSpec((1, TT, 512), tok), pl.BlockSpec((1, TT, 768), tok),
                   pl.BlockSpec((1, 2, TT, 128), lambda t, b: (b, 0, t, 0))],
        out_shape=[jax.ShapeDtypeStruct((nb, ltot, wna.shape[1]), BF16),
                   jax.ShapeDtypeStruct((nb, ltot, wpr.shape[1]), F32),
                   jax.ShapeDtypeStruct((nb, ltot, 512), BF16),
                   jax.ShapeDtypeStruct((nb, ltot, 768), F32),
                   jax.ShapeDtypeStruct((nb, 2, ltot, 128), F32)],
        scratch_shapes=[pltpu.VMEM((TT + 2 * HALO, LOC_COLS), F32)],
        compiler_params=_params(("arbitrary", "arbitrary")), name="norm_in_proj_conv",
    )(x_lat, x_ctx, x_lat, x_lat, mod3, g_pre.reshape(1, d), wna, wpr, wlo, conv_ml, conv_hy, bif_row,
      cos_t, sin_t, tril, triu)


def _na_kernel(q_ref, k0_ref, k1_ref, k2_ref, v0_ref, v1_ref, v2_ref, kc_ref, vc_ref, za_ref, tab_ref, o_ref):
    lane = lax.broadcasted_iota(jnp.int32, (1, 128), 1)
    k_refs = (k0_ref, k1_ref, k2_ref)
    v_refs = (v0_ref, v1_ref, v2_ref)

    def row_max(s):
        return jnp.max(jnp.maximum(s[:, :128], s[:, 128:]), axis=-1, keepdims=True)

    def pair(hp, carry):
        sl = pl.ds(pl.multiple_of(hp * 128, 128), 128)
        q2 = q_ref[0, :, sl]
        out = None
        for e in range(2):
            hm = (lane >= HEAD_DIM) if e else (lane < HEAD_DIM)
            qm = jnp.where(hm, q2, jnp.zeros_like(q2))

            sc = [_dot_nt(qm, kc_ref[0, :, sl])]
            sc += [_dot_nt(qm, k_refs[blk][0, :, sl]) + tab_ref[0, 2 * hp + e, :, blk * TT:(blk + 1) * TT]
                   for blk in range(3)]
            m = row_max(jnp.maximum(jnp.maximum(sc[0], sc[1]), jnp.maximum(sc[2], sc[3])))
            p = jnp.concatenate([jnp.exp(s - m).astype(BF16) for s in sc], axis=1)
            v = jnp.concatenate([vc_ref[0, :, sl]] + [r[0, :, sl] for r in v_refs], axis=0)
            acc = _dot(p, jnp.where(hm, v, jnp.ones_like(v)))
            o = jnp.where(hm, acc / pltpu.roll(acc, HEAD_DIM, 1), 0.0)
            out = o if e == 0 else out + o
        o_ref[0, :, sl] = (out * _silu(za_ref[0, :, sl])).astype(BF16)
        return carry

    lax.fori_loop(0, H_NA // 2, pair, 0, unroll=True)


def _attention(ona, pf, table, nt, with_ctx_queries):
    nb, ltot, _ = ona.shape
    nj = nt + 1 if with_ctx_queries else nt

    def kv(col, dj):
        return lambda j, b: (b, jnp.clip(j + dj, 0, nt - 1), col)

    def pat(j, b):
        p = jnp.where(j == 0, 0, jnp.where(j == nt - 1, 2, jnp.where(j == nt, 3, 1)))
        return (p, 0, 0, 0)

    return pl.pallas_call(
        _na_kernel, grid=(nj, nb),
        in_specs=[pl.BlockSpec((1, TT, 512), lambda j, b: (b, j, 0)),
                  pl.BlockSpec((1, TT, 512), kv(1, -1)), pl.BlockSpec((1, TT, 512), kv(1, 0)),
                  pl.BlockSpec((1, TT, 512), kv(1, 1)),
                  pl.BlockSpec((1, TT, 512), kv(2, -1)), pl.BlockSpec((1, TT, 512), kv(2, 0)),
                  pl.BlockSpec((1, TT, 512), kv(2, 1)),
                  pl.BlockSpec((1, TT, 512), lambda j, b: (b, nt, 1)),
                  pl.BlockSpec((1, TT, 512), lambda j, b: (b, nt, 2)),
                  pl.BlockSpec((1, TT, 512), lambda j, b: (b, j, PF_ZA)),
                  pl.BlockSpec((1, H_NA, TT, 3 * TT), pat)],
        out_specs=pl.BlockSpec((1, TT, 512), lambda j, b: (b, j, 0)),
        out_shape=jax.ShapeDtypeStruct((nb, ltot, 512), BF16),
        compiler_params=_params(("arbitrary", "arbitrary")), name="nbr_attention",
    )(ona, ona, ona, ona, ona, ona, ona, ona, ona, pf, table)


def _na_bias_table(rpb):
    a = np.arange(TT) // GRID_W
    qc = np.arange(TT) % GRID_W
    kidx = np.arange(3 * TT)
    kr = (kidx // TT - 1) * BAND_ROWS + (kidx % TT) // GRID_W
    kcol = kidx % GRID_W
    qstart = np.clip(qc - NA_COLS // 2, 0, GRID_W - NA_COLS)
    col_ok = (kcol[None, :] >= qstart[:, None]) & (kcol[None, :] < qstart[:, None] + NA_COLS)
    coff = np.clip(kcol[None, :] - qc[:, None], -(NA_COLS - 1), NA_COLS - 1) + NA_COLS - 1
    roff = np.clip(kr[None, :] - a[:, None] + NA_ROWS - 1, 0, 2 * NA_ROWS - 2)
    half = NA_ROWS // 2
    row_ok = [
        (kr[None, :] >= 0) & (kr[None, :] < NA_ROWS) & (a[:, None] >= 0),
        (kr[None, :] >= a[:, None] - half) & (kr[None, :] < a[:, None] + half),
        (kr[None, :] >= BAND_ROWS - NA_ROWS) & (kr[None, :] < BAND_ROWS) & (a[:, None] >= 0),
    ]
    nr, ncol = 2 * NA_ROWS - 1, 2 * NA_COLS - 1
    onehot = (coff[:GRID_W, :GRID_W].reshape(1, -1) == np.arange(ncol)[:, None]).astype(np.float32)
    cmat = jnp.dot(rpb.astype(F32).reshape(H_NA * nr, ncol), onehot,
                   precision=lax.Precision.HIGHEST).reshape(H_NA, nr, GRID_W, GRID_W)
    nkr = 3 * BAND_ROWS
    blocks = [jnp.concatenate([cmat[:, roff[ai * GRID_W, kj * GRID_W]] for kj in range(nkr)], axis=-1)
              for ai in range(BAND_ROWS)]
    bias = jnp.concatenate(blocks, axis=1)
    tabs = [jnp.where((r & col_ok)[None], bias, NEG) for r in row_ok]
    tabs.append(jnp.full_like(bias, NEG))
    return jnp.stack(tabs, axis=0)


def _head_lane_sums(x, lane_h):
    out = jnp.zeros_like(x)
    half = lax.broadcasted_iota(jnp.int32, (1, 128), 1) // HEAD_DIM
    for h in range(H_ML):
        t = x[:, (h // 2) * 128:(h // 2 + 1) * 128]
        sm = jnp.sum(jnp.where(half == h % 2, t, 0.0), axis=-1, keepdims=True)
        out = jnp.where(lane_h == h, sm, out)
    return out


def _scan_kernel(q_ref, k_ref, v_ref, g_ref, e_ref, bm_ref, o_ref, w_ref, n_ref, m_ref):
    s = pl.program_id(2)
    sgn = 1 - 2 * pl.program_id(1)

    @pl.when(s == 0)
    def _():
        w_ref[...] = jnp.zeros_like(w_ref)
        n_ref[...] = jnp.zeros_like(n_ref)
        m_ref[...] = jnp.zeros_like(m_ref)

    lane_h = lax.broadcasted_iota(jnp.int32, (1, D_ML), 1) // HEAD_DIM
    ri = lax.broadcasted_iota(jnp.int32, (TT, TT), 0)
    ci = lax.broadcasted_iota(jnp.int32, (TT, TT), 1)
    tri = ((ci - ri) * sgn) <= 0
    gate_lanes = lax.broadcasted_iota(jnp.int32, (1, 128), 1) < H_ML

    for i in range(SCAN_NB):
        gi = g_ref[i, 0]
        a = jnp.where(gate_lanes, gi, 0.0)
        bb = jnp.where(gate_lanes, pltpu.roll(gi, 128 - H_ML, 1), 0.0)
        m_prev = m_ref[i]
        mc = jnp.maximum(m_prev, jnp.max(a, axis=0, keepdims=True))
        small = jnp.concatenate([jnp.exp(a - mc), jnp.exp(-bb - mc),
                                 jnp.broadcast_to(jnp.exp(m_prev - mc), (8, 128))], axis=0)
        sh = small.astype(BF16)
        sl = (small - sh.astype(F32)).astype(BF16)
        ex = _dot(sh, e_ref[...]) + _dot(sl, e_ref[...])
        w_e = ex[0:TT]
        eb_e = ex[TT:2 * TT]
        beta = ex[2 * TT:2 * TT + 1]

        q = q_ref[i]
        kwf = k_ref[i].astype(F32) * w_e
        kw = kwf.astype(BF16)
        vb = v_ref[i].astype(BF16)
        intra = None
        den_intra = jnp.zeros((TT, D_ML), F32)
        for h in range(H_ML):
            qh = jnp.where(lane_h == h, q, jnp.zeros_like(q))
            sc = jnp.where(tri, _dot_nt(qh, kw), 0.0)
            den_intra = jnp.where(lane_h == h, jnp.sum(sc, axis=-1, keepdims=True), den_intra)
            vh = jnp.where(lane_h == h, vb, jnp.zeros_like(vb))
            part = _dot(sc.astype(BF16), vh)
            intra = part if intra is None else intra + part
        num = beta * _dot(q, w_ref[i].astype(BF16)) + intra
        den = beta * _head_lane_sums(q.astype(F32) * n_ref[i], lane_h) + den_intra
        o_ref[i, 0] = num / jnp.maximum(jnp.abs(den), eb_e)

        w_ref[i] = beta * w_ref[i] + bm_ref[...] * _dot_tn(kw, vb)
        n_ref[i] = beta * n_ref[i] + jnp.sum(kwf, axis=0, keepdims=True)
        m_ref[i] = jnp.min(bb, axis=0, keepdims=True) + mc


def _mlstm_scan(qk, pf, gates, emat, bmask, nt):
    nb, ltot, _ = qk.shape
    g = SCAN_NB

    def chunk(s, d):
        return jnp.where(s == 0, nt, jnp.where(d == 0, s - 1, nt - s))

    const = lambda b, d, s: (0, 0)
    return pl.pallas_call(
        _scan_kernel, grid=(nb // g, 2, nt + 1),
        in_specs=[pl.BlockSpec((g, TT, 256), lambda b, d, s: (b, chunk(s, d), 0)),
                  pl.BlockSpec((g, TT, 256), lambda b, d, s: (b, chunk(s, d), 1)),
                  pl.BlockSpec((g, TT, 256), lambda b, d, s: (b, chunk(s, d), PF_VM)),
                  pl.BlockSpec((g, 1, TT, 128), lambda b, d, s: (b, d, chunk(s, d), 0)),
                  pl.BlockSpec((128, 256), const), pl.BlockSpec((256, 256), const)],
        out_specs=pl.BlockSpec((g, 1, TT, 256), lambda b, d, s: (b, d, chunk(s, d), 0)),
        out_shape=jax.ShapeDtypeStruct((nb, 2, ltot, 256), F32),
        scratch_shapes=[pltpu.VMEM((g, D_ML, D_ML), F32), pltpu.VMEM((g, 1, D_ML), F32),
                        pltpu.VMEM((g, 1, 128), F32)],
        compiler_params=_params(("arbitrary", "arbitrary", "arbitrary")), name="mlstm_scan",
    )(qk, qk, pf, gates, emat, bmask)


def _hyfilt_kernel(tl, length, feat_ref, w1_ref, b1_ref, w2_ref, b2_ref, w3_ref, fq_ref, dec_ref, o_ref, ss_ref):
    i = pl.program_id(0)
    a = jnp.sin(fq_ref[0:1] * (_dot_f32(feat_ref[...], w1_ref[...]) + b1_ref[...]))
    a = jnp.sin(fq_ref[1:2] * (_dot_f32(a, w2_ref[...]) + b2_ref[...]))
    dec = dec_ref[...]
    f = _dot_f32(a, w3_ref[...]) * jnp.concatenate([dec, dec, dec, dec], axis=1)
    fwd = jnp.concatenate([f[:, 0:D_HY], f[:, 2 * D_HY:3 * D_HY]], axis=1)
    bwd = jnp.concatenate([f[:, D_HY:2 * D_HY], f[:, 3 * D_HY:4 * D_HY]], axis=1)
    row = lax.broadcasted_iota(jnp.int32, (tl, 1), 0) + i * tl
    k2 = jnp.where(row < length, fwd, jnp.where(row > length, bwd, 0.0))
    o_ref[...] = k2

    @pl.when(i == 0)
    def _():
        ss_ref[...] = jnp.zeros_like(ss_ref)

    ss_ref[...] += jnp.sum(k2 * k2, axis=0, keepdims=True)


def _hyena_filters(length, w1, b1, w2, b2, w3, freq):
    t = np.arange(length, dtype=np.float32)
    tn = (t / np.float32(length - 1)).astype(np.float32)
    bands = (HY_EMB - 1) // 2
    fr = np.linspace(1e-4, bands - 1, bands, dtype=np.float32)
    ang = (np.float32(2.0 * math.pi / length) * t[:, None] * fr[None, :]).astype(np.float32)
    feat = np.zeros((length, 128), np.float32)
    feat[:, 0] = tn
    feat[:, 1:1 + bands] = np.cos(ang)
    feat[:, 1 + bands:1 + 2 * bands] = -np.sin(ang)
    deltas = np.abs(np.linspace(math.log(HY_TARGET) / HY_SLOW, math.log(HY_TARGET) / HY_FAST, D_HY, dtype=np.float32))
    dec = np.exp(-tn[:, None] * deltas[None, :]).astype(np.float32)
    lag = np.concatenate([np.arange(length), [0], np.arange(length - 1, 0, -1)])
    feat, dec = feat[lag], dec[lag]

    pad2 = lambda m, r, c: jnp.zeros((r, c), F32).at[:m.shape[0], :m.shape[1]].set(m.astype(F32))
    w1p = pad2(w1, 128, 128)
    w2p = pad2(w2, 128, 128)
    w3p = pad2(w3, 128, 4 * D_HY)
    b1p = pad2(b1[None], 1, 128)
    b2p = pad2(b2[None], 1, 128)
    fqp = pad2(freq, 2, 128)
    tl = 512
    const = lambda i: (0, 0)
    return pl.pallas_call(
        functools.partial(_hyfilt_kernel, tl, length), grid=(2 * length // tl,),
        in_specs=[pl.BlockSpec((tl, 128), lambda i: (i, 0)),
                  pl.BlockSpec((128, 128), const), pl.BlockSpec((1, 128), const),
                  pl.BlockSpec((128, 128), const), pl.BlockSpec((1, 128), const),
                  pl.BlockSpec((128, 4 * D_HY), const), pl.BlockSpec((2, 128), const),
                  pl.BlockSpec((tl, D_HY), lambda i: (i, 0))],
        out_specs=[pl.BlockSpec((tl, 2 * D_HY), lambda i: (i, 0)), pl.BlockSpec((1, 2 * D_HY), const)],
        out_shape=[jax.ShapeDtypeStruct((2 * length, 2 * D_HY), F32), jax.ShapeDtypeStruct((1, 2 * D_HY), F32)],
        compiler_params=_params(("arbitrary",)), name="hyena_filter_mlp",
    )(feat, w1p, b1p, w2p, b2p, w3p, fqp, dec)


@functools.lru_cache(maxsize=None)
def _fft_tables(length):
    n = 2 * length
    n1 = n // FFT_N2
    nh = n1 // 2
    k1 = np.arange(n1)[None, :, None].astype(np.float64)
    j = np.arange(nh)[None, None, :].astype(np.float64)
    n2 = np.arange(FFT_N2)[:, None, None].astype(np.float64)
    ang = -2.0 * np.pi * k1 * (FFT_N2 * j + n2) / n
    gf = np.concatenate([np.cos(ang), np.sin(ang)], axis=1)
    gi = np.concatenate([np.cos(ang).transpose(0, 2, 1), np.sin(ang).transpose(0, 2, 1)], axis=1)
    kk = np.arange(FFT_N2)[:, None] * np.arange(FFT_N2)[None, :]
    a2 = -2.0 * np.pi * kk / FFT_N2
    f2 = np.concatenate([np.cos(a2), np.sin(a2)], axis=0)
    cv = lambda m: np.asarray(m, np.float32).astype(BF16)
    return cv(gf), cv(gi), cv(f2), n1


def _stage1(n1, z_of, gf_ref, scr_ref, cols):
    nh = n1 // 2

    def body(n2, c):
        r = _dot(gf_ref[n2], z_of(n2))
        off = pl.multiple_of(n2 * (n1 + FFT_PAD), 8)
        if cols == 256:
            scr_ref[0, pl.ds(off, n1), :] = r[:n1, :128] - r[n1:, 128:]
            scr_ref[1, pl.ds(off, n1), :] = r[n1:, :128] + r[:n1, 128:]
        else:
            scr_ref[0, pl.ds(off, n1), :] = r[:n1]
            scr_ref[1, pl.ds(off, n1), :] = r[n1:]
        return c

    lax.fori_loop(0, FFT_N2, body, 0, unroll=FFT_UNROLL)
    del nh


def _stage2(n1, k1, f2_ref, scr_ref):
    ar = scr_ref[0, pl.ds(k1, FFT_N2, stride=n1 + FFT_PAD), :]
    ai = scr_ref[1, pl.ds(k1, FFT_N2, stride=n1 + FFT_PAD), :]
    r = _dot(f2_ref[...], jnp.concatenate([ar, ai], axis=1).astype(BF16))
    return r[:128, :128] - r[128:, 128:], r[128:, :128] + r[:128, 128:]


def _hyconv_kernel(n1, kg, z_ref, gf_ref, f2_ref, kf_ref, gi_ref, o_ref, scr_ref):
    s = pl.program_id(2)
    nh = n1 // 2

    @pl.when(s == 0)
    def _():
        def z_of(n2):
            zr = z_ref[0, 0, pl.ds(n2, nh, stride=FFT_N2), :]
            zi = z_ref[0, 1, pl.ds(n2, nh, stride=FFT_N2), :]
            return jnp.concatenate([zr, zi], axis=1).astype(BF16)
        _stage1(n1, z_of, gf_ref, scr_ref, 256)

    def k1_body(i, c):
        k1 = s * kg + i
        xr, xi = _stage2(n1, k1, f2_ref, scr_ref)
        off = pl.multiple_of(i * FFT_N2, FFT_N2)
        kr = kf_ref[0, pl.ds(off, FFT_N2), :]
        ki = kf_ref[1, pl.ds(off, FFT_N2), :]
        yc = jnp.concatenate([xr * kr - xi * ki, xr * ki + xi * kr], axis=1).astype(BF16)
        r = _dot(f2_ref[...], yc)
        scr_ref[0, pl.ds(k1, FFT_N2, stride=n1 + FFT_PAD), :] = r[:128, :128] + r[128:, 128:]
        scr_ref[1, pl.ds(k1, FFT_N2, stride=n1 + FFT_PAD), :] = r[:128, 128:] - r[128:, :128]
        return c

    lax.fori_loop(0, kg, k1_body, 0, unroll=FFT_UNROLL)

    @pl.when(s == pl.num_programs(2) - 1)
    def _():
        def body(n2, c):
            off = pl.multiple_of(n2 * (n1 + FFT_PAD), 8)
            bc = jnp.concatenate([scr_ref[0, pl.ds(off, n1), :], scr_ref[1, pl.ds(off, n1), :]], axis=1)
            r = _dot(gi_ref[n2], bc.astype(BF16))
            o_ref[0, 0, pl.ds(n2, nh, stride=FFT_N2), :] = r[:nh, :128] + r[nh:, 128:]
            o_ref[0, 1, pl.ds(n2, nh, stride=FFT_N2), :] = r[:nh, 128:] - r[nh:, :128]
            return c
        lax.fori_loop(0, FFT_N2, body, 0, unroll=FFT_UNROLL)


def _hyena_conv(z4, col0, kf, length):
    gf, gi, f2, n1 = _fft_tables(length)
    n = 2 * length
    kg = min(n1, 16)
    npair = z4.shape[0]
    big = pl.Buffered(1)
    return pl.pallas_call(
        functools.partial(_hyconv_kernel, n1, kg), grid=(npair, 2, n1 // kg),
        in_specs=[pl.BlockSpec((1, 2, length, 128), lambda p, c, s: (p, 0, 0, col0 + c), pipeline_mode=big),
                  pl.BlockSpec(gf.shape, lambda p, c, s: (0, 0, 0), pipeline_mode=big),
                  pl.BlockSpec(f2.shape, lambda p, c, s: (0, 0)),
                  pl.BlockSpec((2, kg * FFT_N2, 128), lambda p, c, s: (0, s, c)),
                  pl.BlockSpec(gi.shape, lambda p, c, s: (0, 0, 0), pipeline_mode=big)],
        out_specs=pl.BlockSpec((1, 2, length, 128), lambda p, c, s: (p, 0, 0, c), pipeline_mode=big),
        out_shape=jax.ShapeDtypeStruct((npair, 2, length, 256), F32),
        scratch_shapes=[pltpu.VMEM((2, FFT_N2 * (n1 + FFT_PAD), 128), F32)],
        compiler_params=_params(("arbitrary", "arbitrary", "arbitrary")), name="hyena_fftconv",
    )(z4, gf, f2, kf, gi)


def _hyfft_kernel(n1, kg, z_ref, gf_ref, f2_ref, o_ref, scr_ref):
    s = pl.program_id(2)

    @pl.when(s == 0)
    def _():
        _stage1(n1, lambda n2: z_ref[pl.ds(n2, n1, stride=FFT_N2), :].astype(BF16), gf_ref, scr_ref, 128)

    def k1_body(i, c):
        xr, xi = _stage2(n1, s * kg + i, f2_ref, scr_ref)
        off = pl.multiple_of(i * FFT_N2, FFT_N2)
        o_ref[0, 0, pl.ds(off, FFT_N2), :] = xr
        o_ref[0, 1, pl.ds(off, FFT_N2), :] = xi
        return c

    lax.fori_loop(0, kg, k1_body, 0, unroll=FFT_UNROLL)


@functools.lru_cache(maxsize=None)
def _fft_table_full(length):
    n = 2 * length
    n1 = n // FFT_N2
    k1 = np.arange(n1)[None, :, None].astype(np.float64)
    j = np.arange(n1)[None, None, :].astype(np.float64)
    n2 = np.arange(FFT_N2)[:, None, None].astype(np.float64)
    ang = -2.0 * np.pi * k1 * (FFT_N2 * j + n2) / n
    return np.concatenate([np.cos(ang), np.sin(ang)], axis=1).astype(np.float32).astype(BF16)


def _filter_spectrum(k2, length):
    _, _, f2, n1 = _fft_tables(length)
    gf = _fft_table_full(length)
    n = 2 * length
    kg = min(n1, 16)
    nf = k2.shape[1] // D_HY
    big = pl.Buffered(1)
    return pl.pallas_call(
        functools.partial(_hyfft_kernel, n1, kg), grid=(nf, 2, n1 // kg),
        in_specs=[pl.BlockSpec((n, 128), lambda f, c, s: (0, 2 * f + c), pipeline_mode=big),
                  pl.BlockSpec(gf.shape, lambda f, c, s: (0, 0, 0), pipeline_mode=big),
                  pl.BlockSpec(f2.shape, lambda f, c, s: (0, 0))],
        out_specs=pl.BlockSpec((1, 2, kg * FFT_N2, 128), lambda f, c, s: (f, 0, s, c)),
        out_shape=jax.ShapeDtypeStruct((nf, 2, n, 256), F32),
        scratch_shapes=[pltpu.VMEM((2, FFT_N2 * (n1 + FFT_PAD), 128), F32)],
        compiler_params=_params(("arbitrary", "arbitrary", "arbitrary")), name="hyena_filter_fft",
    )(k2, gf, f2)


def _normalised(k2, ssq, extra_scale):
    return k2 * (lax.rsqrt(ssq + EPS) * extra_scale)


def _hy_mid_kernel(y_ref, hv_ref, b_ref, o_ref):
    v = hv_ref[0][:, 0:D_HY]
    x1 = hv_ref[0][:, D_HY:2 * D_HY]
    o_ref[0] = x1 * (y_ref[0] + v * b_ref[...])


def _hy_mid(y1, hv, bias0):
    nb, length, _ = y1.shape
    te = 4 * TT
    return pl.pallas_call(
        _hy_mid_kernel, grid=(nb, length // te),
        in_specs=[pl.BlockSpec((1, te, D_HY), lambda b, t: (b, t, 0)),
                  pl.BlockSpec((1, te, 512), lambda b, t: (b, t, 0)),
                  pl.BlockSpec((1, D_HY), lambda b, t: (0, 0))],
        out_specs=pl.BlockSpec((1, te, D_HY), lambda b, t: (b, t, 0)),
        out_shape=jax.ShapeDtypeStruct(y1.shape, F32),
        compiler_params=_params(("arbitrary", "arbitrary")), name="hyena_mid",
    )(y1, hv, bias0.reshape(1, D_HY))


def _hy_ctx_kernel(lc, hv_ref, zh_ref, k2_ref, b_ref, fc_ref, ff_ref, ci_ref, o_ref):
    nf = 2 * lc
    hv = hv_ref[0]

    def conv(z, o):
        kf = _dot_exact_rhs(ff_ref[...], k2_ref[:, o * D_HY:(o + 1) * D_HY])
        zs = _dot(fc_ref[...], z.astype(BF16))
        zr, zi = zs[:nf], zs[nf:]
        kr, ki = kf[:nf], kf[nf:]
        ys = jnp.concatenate([zr * kr - zi * ki, zr * ki + zi * kr], axis=0).astype(BF16)
        return _dot(ci_ref[...], ys) * (1.0 / nf)

    v = hv[:, 0:D_HY]
    x1 = hv[:, D_HY:2 * D_HY]
    x2 = hv[:, 2 * D_HY:3 * D_HY]
    z1 = x1 * (conv(v, 0) + v * b_ref[0:1])
    p = x2 * (conv(z1, 1) + z1 * b_ref[1:2])
    o_ref[0] = (p * _silu(zh_ref[0])).astype(BF16)


def _dot_exact_rhs(a_bf16, b):
    bh, bm, bl = _split3(b)
    return _dot(a_bf16, bh) + _dot(a_bf16, bm) + _dot(a_bf16, bl)


@functools.lru_cache(maxsize=None)
def _ctx_dft_tables(lc):
    nf = 2 * lc
    k = np.arange(nf)[:, None].astype(np.float64)
    ang = -2.0 * np.pi * k * np.arange(nf)[None, :] / nf
    ff = np.concatenate([np.cos(ang), np.sin(ang)], axis=0)
    fc = ff[:, :lc]
    a2 = 2.0 * np.pi * np.arange(lc)[:, None] * np.arange(nf)[None, :] / nf
    ci = np.concatenate([np.cos(a2), -np.sin(a2)], axis=1)
    cv = lambda m: np.asarray(m, np.float32).astype(BF16)
    return cv(fc), cv(ff), cv(ci)


def _hy_ctx(hv, pf, k2c, hy_bias, nt, lc):
    nb = hv.shape[0]
    fc, ff, ci = _ctx_dft_tables(lc)
    c2 = lambda b: (0, 0)
    return pl.pallas_call(
        functools.partial(_hy_ctx_kernel, lc), grid=(nb,),
        in_specs=[pl.BlockSpec((1, lc, 768), lambda b: (b, nt, 0)),
                  pl.BlockSpec((1, lc, D_HY), lambda b: (b, nt, PF_ZH)),
                  pl.BlockSpec(k2c.shape, c2),
                  pl.BlockSpec((2, D_HY), c2),
                  pl.BlockSpec(fc.shape, c2), pl.BlockSpec(ff.shape, c2), pl.BlockSpec(ci.shape, c2)],
        out_specs=pl.BlockSpec((1, lc, D_HY), lambda b: (b, 0, 0)),
        out_shape=jax.ShapeDtypeStruct((nb, lc, D_HY), BF16),
        compiler_params=_params(("arbitrary",)), name="hyena_ctx",
    )(hv, pf, k2c, hy_bias, fc, ff, ci)


def _out_kernel(n_hy, na_ref, h0_ref, h1_ref, om_ref, zm_ref, w_ref, x_ref, mod_ref, g_ref, *rest):
    hy_refs, o_ref = rest[:n_hy], rest[n_hy]
    if n_hy == 1:
        hy = hy_refs[0][0]
    else:
        y_ref, z_ref, x2_ref, zh_ref, b_ref = hy_refs
        z = z_ref[0]
        hy = (x2_ref[0] * (y_ref[0] + z * b_ref[...]) * _silu(zh_ref[0])).astype(BF16)
    ml = jax.nn.sigmoid(om_ref[0]) * (h0_ref[0, 0] + h1_ref[0, 0]) * _silu(zm_ref[0])
    y = (_dot(na_ref[0], w_ref[0:D_NA]) + _dot(ml.astype(BF16), w_ref[D_NA:D_NA + D_ML])
         + _dot(hy, w_ref[D_NA + D_ML:D_MODEL]))
    ms = jnp.mean(y * y, axis=-1, keepdims=True)
    r = y * lax.rsqrt(ms + EPS) * g_ref[...]
    o_ref[0] = x_ref[0] + mod_ref[0][2:3] * r


def _out_projection(na_g, hml, pf, hy_args, w_out, res, res_t0, mod3, g_post, to, t0, ntiles, mod_row, out_rows,
                    prev=None):
    nb = na_g.shape[0]
    d = D_MODEL
    tok = lambda col: (lambda b, t: (b, t + t0, col))
    own = lambda col: (lambda b, t: (b, t, col))
    in_specs = [pl.BlockSpec((1, to, 512), tok(0)),
                pl.BlockSpec((1, 1, to, 256), lambda b, t: (b, 0, t + t0, 0)),
                pl.BlockSpec((1, 1, to, 256), lambda b, t: (b, 1, t + t0, 0)),
                pl.BlockSpec((1, to, 256), tok(PF_OM)), pl.BlockSpec((1, to, 256), tok(PF_ZM)),
                pl.BlockSpec((d, d), lambda b, t: (0, 0), pipeline_mode=pl.Buffered(1)),
                pl.BlockSpec((1, to, d), lambda b, t: (b, t + res_t0, 0)),
                pl.BlockSpec((1, 3, d), lambda b, t: (mod_row(b), 0, 0)),
                pl.BlockSpec((1, d), lambda b, t: (0, 0))]
    args = [na_g, hml, hml, pf, pf, w_out, res, mod3, g_post.reshape(1, d)]
    if len(hy_args) == 1:
        in_specs.append(pl.BlockSpec((1, to, D_HY), own(0)))
        args.append(hy_args[0])
    else:
        y2, z1, hv, pf_, bias1 = hy_args
        in_specs += [pl.BlockSpec((1, to, D_HY), own(0)), pl.BlockSpec((1, to, D_HY), own(0)),
                     pl.BlockSpec((1, to, D_HY), tok(2)), pl.BlockSpec((1, to, D_HY), tok(PF_ZH)),
                     pl.BlockSpec((1, D_HY), lambda b, t: (0, 0))]
        args += [y2, z1, hv, pf_, bias1.reshape(1, D_HY)]
    n_hy = len(hy_args)
    aliases = {}
    if prev is not None:
        in_specs.append(pl.BlockSpec(memory_space=pl.ANY))
        args.append(prev)
        aliases = {len(args) - 1: 0}
        kern = lambda *r: _out_kernel(n_hy, *r[:9 + n_hy], r[10 + n_hy])
    else:
        kern = functools.partial(_out_kernel, n_hy)
    return pl.pallas_call(
        kern, grid=(nb, ntiles), in_specs=in_specs,
        out_specs=pl.BlockSpec((1, to, d), tok(0)),
        out_shape=jax.ShapeDtypeStruct((nb, out_rows, d), F32),
        input_output_aliases=aliases,
        compiler_params=_params(("arbitrary", "arbitrary")), name="out_proj_norm_residual",
    )(*args)


@functools.lru_cache(maxsize=None)
def _rope_tables(length, lc):
    n = HEAD_DIM // 4
    t = np.arange(length)
    inv = (ROPE_BASE ** (-np.arange(n, dtype=np.float32) / n)).astype(np.float32)
    pos = np.stack([t // GRID_W, t % GRID_W], axis=-1).astype(np.float32)
    ang = (pos[:, :, None] * inv).astype(np.float32)
    cos_h = np.concatenate([np.cos(ang[:, 0]), np.cos(ang[:, 0]), np.cos(ang[:, 1]), np.cos(ang[:, 1])], axis=-1)
    sin_h = np.concatenate([-np.sin(ang[:, 0]), np.sin(ang[:, 0]), -np.sin(ang[:, 1]), np.sin(ang[:, 1])], axis=-1)
    cos_t = np.concatenate([np.tile(cos_h, (1, 2 * H_ML)), np.ones((lc, 512), np.float32)], axis=0)
    sin_t = np.concatenate([np.tile(sin_h, (1, 2 * H_ML)), np.zeros((lc, 512), np.float32)], axis=0)
    kscale = np.concatenate([np.ones(256, np.float32), np.full(256, HEAD_DIM ** -0.5, np.float32)])
    return (cos_t * kscale).astype(np.float32), (sin_t * kscale).astype(np.float32)


@functools.lru_cache(maxsize=None)
def _scan_constants():
    tril = np.tril(np.ones((TT, TT), np.float32))
    emat = np.zeros((128, D_ML), np.float32)
    for h in range(H_ML):
        emat[h, h * HEAD_DIM:(h + 1) * HEAD_DIM] = 1.0
    rh = np.arange(D_ML)[:, None] // HEAD_DIM
    chd = np.arange(D_ML)[None, :] // HEAD_DIM
    bmask = (rh == chd).astype(np.float32)
    return tril.astype(BF16), np.ascontiguousarray(tril.T).astype(BF16), emat.astype(BF16), bmask


def _layer(x_lat, x_ctx, ctx_tile, mod3, nb, length, lc, g_pre, g_post, w_in, b_if, conv_ml, conv_hy, rpb,
           hf_w1, hf_b1, hf_w2, hf_b2, hf_w3, hf_freq, hy_bias, w_out, update_ctx):
    nt = length // TT
    ltot = length + lc
    tril, triu, emat, bmask = _scan_constants()
    cos_t, sin_t = _rope_tables(length, lc)

    c = lambda a, b: w_in[:, a:b]
    wna = jnp.concatenate([c(0, 512) * (HEAD_DIM ** -0.5), c(512, 1536)], axis=1).astype(BF16)
    wpr = jnp.concatenate([c(1536, 2048), c(2560, 2816), c(2816, 3072), c(3072, 3328), c(4112, 4368)],
                          axis=1).astype(BF16)
    wlo = jnp.concatenate([c(2048, 2560), c(3344, 4112), c(3328, 3344), jnp.zeros((D_MODEL, 112), F32)],
                          axis=1).astype(BF16)
    bif_row = jnp.zeros((1, 128), F32).at[0, :4 * H_ML].set(b_if.reshape(-1))
    ona, pf, qk, hv, gates = _projection(x_lat, x_ctx, ctx_tile, mod3, g_pre, wna, wpr, wlo, conv_ml, conv_hy,
                                         bif_row, cos_t, sin_t, tril, triu, nt)

    na_g = _attention(ona, pf, _na_bias_table(rpb), nt, update_ctx)

    hml = _mlstm_scan(qk, pf, gates, emat, bmask, nt)

    k2, ssq = _hyena_filters(length, hf_w1, hf_b1, hf_w2, hf_b2, hf_w3, hf_freq)
    kf = _filter_spectrum(_normalised(k2, ssq, 1.0 / (2 * length)), length)
    hv4 = hv.reshape(nb // 2, 2, ltot, 768)
    y1 = _hyena_conv(hv4, 0, kf[0], length).reshape(nb, length, D_HY)
    z1 = _hy_mid(y1, hv, hy_bias[0])
    y2 = _hyena_conv(z1.reshape(nb // 2, 2, length, D_HY), 0, kf[1], length).reshape(nb, length, D_HY)
    hy_args = (y2, z1, hv, pf, hy_bias[1])
    to = 2 * TT
    if not update_ctx:
        return _out_projection(na_g, hml, pf, hy_args, w_out, x_lat, 0, mod3, g_post, to, 0, length // to,
                               lambda b: b, length)

    xn = _out_projection(na_g, hml, pf, hy_args, w_out, x_lat, 0, mod3, g_post, to, 0, length // to,
                         lambda b: b, ltot)
    k2c, ssq_c = _hyena_filters(lc, hf_w1, hf_b1, hf_w2, hf_b2, hf_w3, hf_freq)
    hyc_g = _hy_ctx(hv, pf, _normalised(k2c, ssq_c, 1.0), hy_bias, nt, lc)
    return _out_projection(na_g, hml, pf, (hyc_g,), w_out, x_ctx, ctx_tile, mod3, g_post, TT, nt, 1,
                           lambda b: nb, ltot, prev=xn)


def kernel(x, c, ctx, c_ctx, w_ada, b_ada, g_pre, g_post, w_in, b_if, conv_ml, conv_hy, rpb, hf_w1, hf_b1,
           hf_w2, hf_b2, hf_w3, hf_freq, hy_bias, w_out):
    nb, length, d = x.shape
    lc = ctx.shape[1]
    depth = w_in.shape[0]
    assert lc == TT and length % (4 * TT) == 0 and nb % 2 == 0 and d == D_MODEL
    cc = jnp.zeros((16, d), F32).at[:nb].set(c).at[nb].set(c_ctx)
    x_lat, x_ctx, ctx_tile = x, ctx, 0
    for l in range(depth):
        mod3 = _modulation(cc, w_ada[l], b_ada[l]).reshape(16, 3, d)
        x_lat = _layer(x_lat, x_ctx, ctx_tile, mod3, nb, length, lc, g_pre[l], g_post[l], w_in[l], b_if[l],
                       conv_ml[l], conv_hy[l], rpb[l], hf_w1[l], hf_b1[l], hf_w2[l], hf_b2[l], hf_w3[l],
                       hf_freq[l], hy_bias[l], w_out[l].astype(BF16), l < depth - 1)
        x_ctx, ctx_tile = x_lat, length // TT
    return x_lat
```

```python
import functools
import math

import numpy as np
import jax
import jax.numpy as jnp
from jax import lax
from jax.experimental import pallas as pl
from jax.experimental.pallas import tpu as pltpu

F32 = jnp.float32
BF16 = jnp.bfloat16

D_MODEL = 1024
GRID_W = 64
HEAD_DIM = 64
D_NA = 512
D_ML = 256
D_HY = 256
H_NA = 8
H_ML = 4
NA_ROWS = 8
NA_COLS = 16
HY_EMB = 33
HY_FFN = 64
HY_TARGET = 1e-2
HY_FAST = 0.3
HY_SLOW = 1.5
ROPE_BASE = 10000.0
EPS = 1e-6

TT = 256
BAND_ROWS = TT // GRID_W
SCAN_NB = 4
PRE_ROWS = 64
FFT_N2 = 128
FFT_PAD = 8
FFT_UNROLL = 8
NEG = -1e30
VMEM_LIMIT = 56 * 1024 * 1024

PF_ZA = 0
PF_VM, PF_OM, PF_ZM, PF_ZH = 2, 3, 4, 5
LOC_QK, LOC_HY, LOC_GM, LOC_COLS = 0, 512, 1280, 1408
HALO = 8


def _dot(a, b):
    return jnp.dot(a, b, preferred_element_type=F32)


def _dot_nt(a, b):
    return lax.dot_general(a, b, (((1,), (1,)), ((), ())), preferred_element_type=F32)


def _dot_tn(a, b):
    return lax.dot_general(a, b, (((0,), (0,)), ((), ())), preferred_element_type=F32)


def _split3(x):
    h = x.astype(BF16)
    r = x - h.astype(F32)
    m = r.astype(BF16)
    l = (r - m.astype(F32)).astype(BF16)
    return h, m, l


def _dot_f32(a, b):
    ah, am, al = _split3(a)
    bh, bm, bl = _split3(b)
    return (_dot(ah, bh) + (_dot(ah, bm) + _dot(am, bh))
            + (_dot(ah, bl) + _dot(al, bh) + _dot(am, bm)))


def _dot_exact_lhs(a_bf16, b):
    bh, bm, bl = _split3(b)
    return _dot(a_bf16, bh) + _dot(a_bf16, bm) + _dot(a_bf16, bl)


def _silu(x):
    return x * jax.nn.sigmoid(x)


def _params(sem, vmem=VMEM_LIMIT):
    return pltpu.CompilerParams(dimension_semantics=sem, vmem_limit_bytes=vmem)


def _mod_kernel(c_ref, w_ref, b_ref, o_ref):
    o_ref[...] = _dot_f32(_silu(c_ref[...]), w_ref[...]) + b_ref[...]


def _modulation(cc, w_ada, b_ada):
    n = w_ada.shape[1]
    bn = 512
    return pl.pallas_call(
        _mod_kernel, grid=(n // bn,),
        in_specs=[pl.BlockSpec(cc.shape, lambda j: (0, 0)),
                  pl.BlockSpec((w_ada.shape[0], bn), lambda j: (0, j)),
                  pl.BlockSpec((1, bn), lambda j: (0, j))],
        out_specs=pl.BlockSpec((cc.shape[0], bn), lambda j: (0, j)),
        out_shape=jax.ShapeDtypeStruct((cc.shape[0], n), F32),
        compiler_params=_params(("arbitrary",)), name="adaln_mod",
    )(cc, w_ada, b_ada.reshape(1, n))


def _proj_kernel(nt, x_ref, c_ref, xp_ref, xn_ref, mod_ref, g_ref, wna_ref, wpr_ref, wlo_ref, cml_ref, chy_ref,
                 bif_ref, cos_ref, sin_ref, tril_ref, triu_ref, ona_ref, opr_ref, oqk_ref, ohv_ref, og_ref,
                 loc_ref):
    t = pl.program_id(0)
    x = jnp.where(t == nt, c_ref[0], x_ref[0])
    xs = jnp.concatenate([xp_ref[0], x, xn_ref[0]], axis=0)
    ms = jnp.mean(xs * xs, axis=-1, keepdims=True)
    y = xs * lax.rsqrt(ms + EPS) * g_ref[...]
    mod = mod_ref[0]
    h = (y * (1.0 + mod[1:2]) + mod[0:1]).astype(BF16)
    hm = h[HALO:HALO + TT]
    loc_ref[...] = _dot(h, wlo_ref[...])
    ona_ref[0] = _dot(hm, wna_ref[...]).astype(BF16)

    pm = jnp.where(jnp.logical_and(t != 0, t != nt), 1.0, 0.0)
    nm = jnp.where(jnp.logical_and(t != nt - 1, t != nt), 1.0, 0.0)
    row8 = lax.broadcasted_iota(jnp.int32, (8, 1), 0)
    nchunk = TT // PRE_ROWS

    def conv3(cols, w, c):
        r0 = HALO + c * PRE_ROWS
        u = loc_ref[r0:r0 + PRE_ROWS, cols]
        up = loc_ref[r0 - 1:r0, cols]
        un = loc_ref[r0 + PRE_ROWS:r0 + PRE_ROWS + 1, cols]
        if c == 0:
            up = up * pm
        if c == nchunk - 1:
            un = un * nm
        rp = pltpu.roll(u, 1, 0)
        rn = pltpu.roll(u, PRE_ROWS - 1, 0)
        prev = jnp.concatenate([jnp.where(row8 == 0, up, rp[0:8]), rp[8:]], axis=0)
        nxt = jnp.concatenate([rn[:PRE_ROWS - 8], jnp.where(row8 == 7, un, rn[PRE_ROWS - 8:])], axis=0)
        return prev * w[0:1] + u * w[1:2] + nxt * w[2:3]

    lane = lax.broadcasted_iota(jnp.int32, (1, 2 * D_ML), 1)
    first = (lane % 32) < 16
    for c in range(nchunk):
        rows = slice(c * PRE_ROWS, (c + 1) * PRE_ROWS)
        ohv_ref[0, rows] = conv3(slice(LOC_HY, LOC_GM), chy_ref[...], c)
        x = _silu(conv3(slice(LOC_QK, LOC_HY), cml_ref[...], c))
        partner = jnp.where(first, pltpu.roll(x, 2 * D_ML - 16, 1), pltpu.roll(x, 16, 1))
        oqk_ref[0, rows] = (x * cos_ref[rows] + partner * sin_ref[rows]).astype(BF16)

    g = loc_ref[HALO:HALO + TT, LOC_GM:LOC_COLS] + bif_ref[...]
    gl = lax.broadcasted_iota(jnp.int32, (1, 128), 1)
    is_f = jnp.logical_and((gl % 8) >= 4, gl < 16)
    lf = jnp.where(is_f, jnp.minimum(g, 0.0) - jnp.log1p(jnp.exp(-jnp.abs(g))), 0.0)
    bcum = jnp.where(gl < 8, _dot_exact_lhs(tril_ref[...], lf), _dot_exact_lhs(triu_ref[...], lf))
    a = g - pltpu.roll(bcum, 128 - H_ML, 1)
    ab = jnp.where(is_f, bcum, a)
    keep = gl < 2 * H_ML
    og_ref[0, 0] = jnp.where(keep, ab, 0.0)
    og_ref[0, 1] = jnp.where(keep, pltpu.roll(ab, 120, 1), 0.0)
    opr_ref[0] = _dot(hm, wpr_ref[...])


def _projection(x_lat, x_ctx, ctx_tile, mod3, g_pre, wna, wpr, wlo, conv_ml, conv_hy, bif_row, cos_t, sin_t,
                tril, triu, nt):
    nb, lat_rows, d = x_lat.shape
    ltot = (nt + 1) * TT
    ctx_row = nb
    last8 = lat_rows // HALO - 1
    r8 = TT // HALO

    def mod_idx(t, b):
        return (jnp.where(t == nt, ctx_row, b), 0, 0)

    const = lambda t, b: (0, 0)
    big = pl.Buffered(1)
    tok = lambda t, b: (b, t, 0)
    return pl.pallas_call(
        functools.partial(_proj_kernel, nt), grid=(nt + 1, nb),
        in_specs=[pl.BlockSpec((1, TT, d), lambda t, b: (b, jnp.minimum(t, nt - 1), 0)),
                  pl.BlockSpec((1, TT, d), lambda t, b: (b, ctx_tile, 0)),
                  pl.BlockSpec((1, HALO, d), lambda t, b: (b, jnp.clip(t * r8 - 1, 0, last8), 0)),
                  pl.BlockSpec((1, HALO, d), lambda t, b: (b, jnp.minimum((t + 1) * r8, last8), 0)),
                  pl.BlockSpec((1, 3, d), mod_idx),
                  pl.BlockSpec((1, d), const),
                  pl.BlockSpec(wna.shape, const, pipeline_mode=big),
                  pl.BlockSpec(wpr.shape, const, pipeline_mode=big),
                  pl.BlockSpec(wlo.shape, const, pipeline_mode=big),
                  pl.BlockSpec((3, 512), const), pl.BlockSpec((3, 768), const), pl.BlockSpec((1, 128), const),
                  pl.BlockSpec((TT, 512), lambda t, b: (t, 0)), pl.BlockSpec((TT, 512), lambda t, b: (t, 0)),
                  pl.BlockSpec((TT, TT), const), pl.BlockSpec((TT, TT), const)],
        out_specs=[pl.BlockSpec((1, TT, wna.shape[1]), tok), pl.BlockSpec((1, TT, wpr.shape[1]), tok),
                   pl.BlockSpec((1, TT, 512), tok), pl.BlockSpec((1, TT, 768), tok),
                   pl.BlockSpec((1, 2, TT, 128), lambda t, b: (b, 0, t, 0))],
        out_shape=[jax.ShapeDtypeStruct((nb, ltot, wna.shape[1]), BF16),
                   jax.ShapeDtypeStruct((nb, ltot, wpr.shape[1]), F32),
                   jax.ShapeDtypeStruct((nb, ltot, 512), BF16),
                   jax.ShapeDtypeStruct((nb, ltot, 768), F32),
                   jax.ShapeDtypeStruct((nb, 2, ltot, 128), F32)],
        scratch_shapes=[pltpu.VMEM((TT + 2 * HALO, LOC_COLS), F32)],
        compiler_params=_params(("arbitrary", "arbitrary")), name="norm_in_proj_conv",
    )(x_lat, x_ctx, x_lat, x_lat, mod3, g_pre.reshape(1, d), wna, wpr, wlo, conv_ml, conv_hy, bif_row,
      cos_t, sin_t, tril, triu)


def _na_kernel(q_ref, k0_ref, k1_ref, k2_ref, v0_ref, v1_ref, v2_ref, kc_ref, vc_ref, za_ref, tab_ref, o_ref):
    lane = lax.broadcasted_iota(jnp.int32, (1, 128), 1)
    k_refs = (k0_ref, k1_ref, k2_ref)
    v_refs = (v0_ref, v1_ref, v2_ref)

    def row_max(s):
        return jnp.max(jnp.maximum(s[:, :128], s[:, 128:]), axis=-1, keepdims=True)

    def head_mask(h):
        return (lane >= HEAD_DIM) if h % 2 else (lane < HEAD_DIM)

    def scores(h):
        sl = slice((h // 2) * 128, (h // 2 + 1) * 128)
        q2 = q_ref[0, :, sl]
        qm = jnp.where(head_mask(h), q2, jnp.zeros_like(q2))
        sc = [_dot_nt(qm, kc_ref[0, :, sl])]
        sc += [_dot_nt(qm, k_refs[blk][0, :, sl]) + tab_ref[0, h, :, blk * TT:(blk + 1) * TT] for blk in range(3)]
        return sc

    def softmax(sc):
        m = row_max(jnp.maximum(jnp.maximum(sc[0], sc[1]), jnp.maximum(sc[2], sc[3])))
        return jnp.concatenate([jnp.exp(s - m).astype(BF16) for s in sc], axis=1)

    def values(h, p):
        sl = slice((h // 2) * 128, (h // 2 + 1) * 128)
        hm = head_mask(h)
        v = jnp.concatenate([vc_ref[0, :, sl]] + [r[0, :, sl] for r in v_refs], axis=0)
        acc = _dot(p, jnp.where(hm, v, jnp.ones_like(v)))
        return jnp.where(hm, acc / pltpu.roll(acc, HEAD_DIM, 1), 0.0)

    outs = {}
    sc = scores(0)
    for h in range(H_NA):
        p = softmax(sc)
        if h + 1 < H_NA:
            sc = scores(h + 1)
        outs[h] = values(h, p)
        if h % 2 == 1:
            sl = slice((h // 2) * 128, (h // 2 + 1) * 128)
            o_ref[0, :, sl] = ((outs[h - 1] + outs[h]) * _silu(za_ref[0, :, sl])).astype(BF16)


def _attention(ona, pf, table, nt, with_ctx_queries):
    nb, ltot, _ = ona.shape
    nj = nt + 1 if with_ctx_queries else nt

    def kv(col, dj):
        return lambda j, b: (b, jnp.clip(j + dj, 0, nt - 1), col)

    def pat(j, b):
        p = jnp.where(j == 0, 0, jnp.where(j == nt - 1, 2, jnp.where(j == nt, 3, 1)))
        return (p, 0, 0, 0)

    return pl.pallas_call(
        _na_kernel, grid=(nj, nb),
        in_specs=[pl.BlockSpec((1, TT, 512), lambda j, b: (b, j, 0)),
                  pl.BlockSpec((1, TT, 512), kv(1, -1)), pl.BlockSpec((1, TT, 512), kv(1, 0)),
                  pl.BlockSpec((1, TT, 512), kv(1, 1)),
                  pl.BlockSpec((1, TT, 512), kv(2, -1)), pl.BlockSpec((1, TT, 512), kv(2, 0)),
                  pl.BlockSpec((1, TT, 512), kv(2, 1)),
                  pl.BlockSpec((1, TT, 512), lambda j, b: (b, nt, 1)),
                  pl.BlockSpec((1, TT, 512), lambda j, b: (b, nt, 2)),
                  pl.BlockSpec((1, TT, 512), lambda j, b: (b, j, PF_ZA)),
                  pl.BlockSpec((1, H_NA, TT, 3 * TT), pat)],
        out_specs=pl.BlockSpec((1, TT, 512), lambda j, b: (b, j, 0)),
        out_shape=jax.ShapeDtypeStruct((nb, ltot, 512), BF16),
        compiler_params=_params(("arbitrary", "arbitrary")), name="nbr_attention",
    )(ona, ona, ona, ona, ona, ona, ona, ona, ona, pf, table)


def _na_bias_table(rpb):
    a = np.arange(TT) // GRID_W
    qc = np.arange(TT) % GRID_W
    kidx = np.arange(3 * TT)
    kr = (kidx // TT - 1) * BAND_ROWS + (kidx % TT) // GRID_W
    kcol = kidx % GRID_W
    qstart = np.clip(qc - NA_COLS // 2, 0, GRID_W - NA_COLS)
    col_ok = (kcol[None, :] >= qstart[:, None]) & (kcol[None, :] < qstart[:, None] + NA_COLS)
    coff = np.clip(kcol[None, :] - qc[:, None], -(NA_COLS - 1), NA_COLS - 1) + NA_COLS - 1
    roff = np.clip(kr[None, :] - a[:, None] + NA_ROWS - 1, 0, 2 * NA_ROWS - 2)
    half = NA_ROWS // 2
    row_ok = [
        (kr[None, :] >= 0) & (kr[None, :] < NA_ROWS) & (a[:, None] >= 0),
        (kr[None, :] >= a[:, None] - half) & (kr[None, :] < a[:, None] + half),
        (kr[None, :] >= BAND_ROWS - NA_ROWS) & (kr[None, :] < BAND_ROWS) & (a[:, None] >= 0),
    ]
    nr, ncol = 2 * NA_ROWS - 1, 2 * NA_COLS - 1
    onehot = (coff[:GRID_W, :GRID_W].reshape(1, -1) == np.arange(ncol)[:, None]).astype(np.float32)
    cmat = jnp.dot(rpb.astype(F32).reshape(H_NA * nr, ncol), onehot,
                   precision=lax.Precision.HIGHEST).reshape(H_NA, nr, GRID_W, GRID_W)
    nkr = 3 * BAND_ROWS
    blocks = [jnp.concatenate([cmat[:, roff[ai * GRID_W, kj * GRID_W]] for kj in range(nkr)], axis=-1)
              for ai in range(BAND_ROWS)]
    bias = jnp.concatenate(blocks, axis=1)
    tabs = [jnp.where((r & col_ok)[None], bias, NEG) for r in row_ok]
    tabs.append(jnp.full_like(bias, NEG))
    return jnp.stack(tabs, axis=0)


def _head_lane_sums(x, lane_h):
    out = jnp.zeros_like(x)
    half = lax.broadcasted_iota(jnp.int32, (1, 128), 1) // HEAD_DIM
    for h in range(H_ML):
        t = x[:, (h // 2) * 128:(h // 2 + 1) * 128]
        sm = jnp.sum(jnp.where(half == h % 2, t, 0.0), axis=-1, keepdims=True)
        out = jnp.where(lane_h == h, sm, out)
    return out


def _scan_kernel(q_ref, k_ref, v_ref, g_ref, e_ref, bm_ref, o_ref, w_ref, n_ref, m_ref):
    s = pl.program_id(2)
    sgn = 1 - 2 * pl.program_id(1)

    @pl.when(s == 0)
    def _():
        w_ref[...] = jnp.zeros_like(w_ref)
        n_ref[...] = jnp.zeros_like(n_ref)
        m_ref[...] = jnp.zeros_like(m_ref)

    lane_h = lax.broadcasted_iota(jnp.int32, (1, D_ML), 1) // HEAD_DIM
    ri = lax.broadcasted_iota(jnp.int32, (TT, TT), 0)
    ci = lax.broadcasted_iota(jnp.int32, (TT, TT), 1)
    tri = ((ci - ri) * sgn) <= 0
    gate_lanes = lax.broadcasted_iota(jnp.int32, (1, 128), 1) < H_ML

    seqs = range(SCAN_NB)
    gate, kws, kwfs, vbs = [], [], [], []
    for i in seqs:
        gi = g_ref[i, 0]
        a = jnp.where(gate_lanes, gi, 0.0)
        bb = jnp.where(gate_lanes, pltpu.roll(gi, 128 - H_ML, 1), 0.0)
        m_prev = m_ref[i]
        mc = jnp.maximum(m_prev, jnp.max(a, axis=0, keepdims=True))
        small = jnp.concatenate([jnp.exp(a - mc), jnp.exp(-bb - mc),
                                 jnp.broadcast_to(jnp.exp(m_prev - mc), (8, 128))], axis=0)
        sh = small.astype(BF16)
        sl = (small - sh.astype(F32)).astype(BF16)
        ex = _dot(sh, e_ref[...]) + _dot(sl, e_ref[...])
        gate.append((ex[TT:2 * TT], ex[2 * TT:2 * TT + 1], jnp.min(bb, axis=0, keepdims=True) + mc))
        kwf = k_ref[i].astype(F32) * ex[0:TT]
        kwfs.append(kwf)
        kws.append(kwf.astype(BF16))
        vbs.append(v_ref[i].astype(BF16))

    scores = [[_dot_nt(jnp.where(lane_h == h, q_ref[i], jnp.zeros_like(q_ref[i])), kws[i]) for h in range(H_ML)]
              for i in seqs]
    inters = [_dot(q_ref[i], w_ref[i].astype(BF16)) for i in seqs]
    updates = [_dot_tn(kws[i], vbs[i]) for i in seqs]

    for i in seqs:
        eb_e, beta, m_new = gate[i]
        intra = None
        den_intra = jnp.zeros((TT, D_ML), F32)
        for h in range(H_ML):
            sc = jnp.where(tri, scores[i][h], 0.0)
            den_intra = jnp.where(lane_h == h, jnp.sum(sc, axis=-1, keepdims=True), den_intra)
            vh = jnp.where(lane_h == h, vbs[i], jnp.zeros_like(vbs[i]))
            part = _dot(sc.astype(BF16), vh)
            intra = part if intra is None else intra + part
        num = beta * inters[i] + intra
        den = beta * _head_lane_sums(q_ref[i].astype(F32) * n_ref[i], lane_h) + den_intra
        o_ref[i, 0] = num / jnp.maximum(jnp.abs(den), eb_e)

        w_ref[i] = beta * w_ref[i] + bm_ref[...] * updates[i]
        n_ref[i] = beta * n_ref[i] + jnp.sum(kwfs[i], axis=0, keepdims=True)
        m_ref[i] = m_new


def _mlstm_scan(qk, pf, gates, emat, bmask, nt):
    nb, ltot, _ = qk.shape
    g = SCAN_NB

    def chunk(s, d):
        return jnp.where(s == 0, nt, jnp.where(d == 0, s - 1, nt - s))

    const = lambda b, d, s: (0, 0)
    return pl.pallas_call(
        _scan_kernel, grid=(nb // g, 2, nt + 1),
        in_specs=[pl.BlockSpec((g, TT, 256), lambda b, d, s: (b, chunk(s, d), 0)),
                  pl.BlockSpec((g, TT, 256), lambda b, d, s: (b, chunk(s, d), 1)),
                  pl.BlockSpec((g, TT, 256), lambda b, d, s: (b, chunk(s, d), PF_VM)),
                  pl.BlockSpec((g, 1, TT, 128), lambda b, d, s: (b, d, chunk(s, d), 0)),
                  pl.BlockSpec((128, 256), const), pl.BlockSpec((256, 256), const)],
        out_specs=pl.BlockSpec((g, 1, TT, 256), lambda b, d, s: (b, d, chunk(s, d), 0)),
        out_shape=jax.ShapeDtypeStruct((nb, 2, ltot, 256), F32),
        scratch_shapes=[pltpu.VMEM((g, D_ML, D_ML), F32), pltpu.VMEM((g, 1, D_ML), F32),
                        pltpu.VMEM((g, 1, 128), F32)],
        compiler_params=_params(("arbitrary", "arbitrary", "arbitrary")), name="mlstm_scan",
    )(qk, qk, pf, gates, emat, bmask)


def _hyfilt_kernel(tl, length, feat_ref, w1_ref, b1_ref, w2_ref, b2_ref, w3_ref, fq_ref, dec_ref, o_ref, ss_ref):
    i = pl.program_id(0)
    a = jnp.sin(fq_ref[0:1] * (_dot_f32(feat_ref[...], w1_ref[...]) + b1_ref[...]))
    a = jnp.sin(fq_ref[1:2] * (_dot_f32(a, w2_ref[...]) + b2_ref[...]))
    dec = dec_ref[...]
    f = _dot_f32(a, w3_ref[...]) * jnp.concatenate([dec, dec, dec, dec], axis=1)
    fwd = jnp.concatenate([f[:, 0:D_HY], f[:, 2 * D_HY:3 * D_HY]], axis=1)
    bwd = jnp.concatenate([f[:, D_HY:2 * D_HY], f[:, 3 * D_HY:4 * D_HY]], axis=1)
    row = lax.broadcasted_iota(jnp.int32, (tl, 1), 0) + i * tl
    k2 = jnp.where(row < length, fwd, jnp.where(row > length, bwd, 0.0))
    o_ref[...] = k2

    @pl.when(i == 0)
    def _():
        ss_ref[...] = jnp.zeros_like(ss_ref)

    ss_ref[...] += jnp.sum(k2 * k2, axis=0, keepdims=True)


def _hyena_filters(length, w1, b1, w2, b2, w3, freq):
    t = np.arange(length, dtype=np.float32)
    tn = (t / np.float32(length - 1)).astype(np.float32)
    bands = (HY_EMB - 1) // 2
    fr = np.linspace(1e-4, bands - 1, bands, dtype=np.float32)
    ang = (np.float32(2.0 * math.pi / length) * t[:, None] * fr[None, :]).astype(np.float32)
    feat = np.zeros((length, 128), np.float32)
    feat[:, 0] = tn
    feat[:, 1:1 + bands] = np.cos(ang)
    feat[:, 1 + bands:1 + 2 * bands] = -np.sin(ang)
    deltas = np.abs(np.linspace(math.log(HY_TARGET) / HY_SLOW, math.log(HY_TARGET) / HY_FAST, D_HY, dtype=np.float32))
    dec = np.exp(-tn[:, None] * deltas[None, :]).astype(np.float32)
    lag = np.concatenate([np.arange(length), [0], np.arange(length - 1, 0, -1)])
    feat, dec = feat[lag], dec[lag]

    pad2 = lambda m, r, c: jnp.zeros((r, c), F32).at[:m.shape[0], :m.shape[1]].set(m.astype(F32))
    w1p = pad2(w1, 128, 128)
    w2p = pad2(w2, 128, 128)
    w3p = pad2(w3, 128, 4 * D_HY)
    b1p = pad2(b1[None], 1, 128)
    b2p = pad2(b2[None], 1, 128)
    fqp = pad2(freq, 2, 128)
    tl = 512
    const = lambda i: (0, 0)
    return pl.pallas_call(
        functools.partial(_hyfilt_kernel, tl, length), grid=(2 * length // tl,),
        in_specs=[pl.BlockSpec((tl, 128), lambda i: (i, 0)),
                  pl.BlockSpec((128, 128), const), pl.BlockSpec((1, 128), const),
                  pl.BlockSpec((128, 128), const), pl.BlockSpec((1, 128), const),
                  pl.BlockSpec((128, 4 * D_HY), const), pl.BlockSpec((2, 128), const),
                  pl.BlockSpec((tl, D_HY), lambda i: (i, 0))],
        out_specs=[pl.BlockSpec((tl, 2 * D_HY), lambda i: (i, 0)), pl.BlockSpec((1, 2 * D_HY), const)],
        out_shape=[jax.ShapeDtypeStruct((2 * length, 2 * D_HY), F32), jax.ShapeDtypeStruct((1, 2 * D_HY), F32)],
        compiler_params=_params(("arbitrary",)), name="hyena_filter_mlp",
    )(feat, w1p, b1p, w2p, b2p, w3p, fqp, dec)


@functools.lru_cache(maxsize=None)
def _fft_tables(length):
    n = 2 * length
    n1 = n // FFT_N2
    nh = n1 // 2
    k1 = np.arange(n1)[None, :, None].astype(np.float64)
    j = np.arange(nh)[None, None, :].astype(np.float64)
    n2 = np.arange(FFT_N2)[:, None, None].astype(np.float64)
    ang = -2.0 * np.pi * k1 * (FFT_N2 * j + n2) / n
    gf = np.concatenate([np.cos(ang), np.sin(ang)], axis=1)
    gi = np.concatenate([np.cos(ang).transpose(0, 2, 1), np.sin(ang).transpose(0, 2, 1)], axis=1)
    kk = np.arange(FFT_N2)[:, None] * np.arange(FFT_N2)[None, :]
    a2 = -2.0 * np.pi * kk / FFT_N2
    f2 = np.concatenate([np.cos(a2), np.sin(a2)], axis=0)
    cv = lambda m: np.asarray(m, np.float32).astype(BF16)
    return cv(gf), cv(gi), cv(f2), n1


def _stage1(n1, z_of, gf_ref, scr_ref, cols):
    def group(gidx, c):
        n2s = [gidx * FFT_UNROLL + u for u in range(FFT_UNROLL)]
        zs = [z_of(n2) for n2 in n2s]
        rs = [_dot(gf_ref[n2], z) for n2, z in zip(n2s, zs)]
        for n2, r in zip(n2s, rs):
            off = pl.multiple_of(n2 * (n1 + FFT_PAD), 8)
            if cols == 256:
                scr_ref[0, pl.ds(off, n1), :] = r[:n1, :128] - r[n1:, 128:]
                scr_ref[1, pl.ds(off, n1), :] = r[n1:, :128] + r[:n1, 128:]
            else:
                scr_ref[0, pl.ds(off, n1), :] = r[:n1]
                scr_ref[1, pl.ds(off, n1), :] = r[n1:]
        return c

    lax.fori_loop(0, FFT_N2 // FFT_UNROLL, group, 0)


def _stage2(n1, k1, f2_ref, scr_ref):
    ar = scr_ref[0, pl.ds(k1, FFT_N2, stride=n1 + FFT_PAD), :]
    ai = scr_ref[1, pl.ds(k1, FFT_N2, stride=n1 + FFT_PAD), :]
    r = _dot(f2_ref[...], jnp.concatenate([ar, ai], axis=1).astype(BF16))
    return r[:128, :128] - r[128:, 128:], r[128:, :128] + r[:128, 128:]


def _hyconv_kernel(n1, kg, z_ref, gf_ref, f2_ref, kf_ref, gi_ref, o_ref, scr_ref):
    s = pl.program_id(2)
    nh = n1 // 2

    @pl.when(s == 0)
    def _():
        def z_of(n2):
            zr = z_ref[0, 0, pl.ds(n2, nh, stride=FFT_N2), :]
            zi = z_ref[0, 1, pl.ds(n2, nh, stride=FFT_N2), :]
            return jnp.concatenate([zr, zi], axis=1).astype(BF16)
        _stage1(n1, z_of, gf_ref, scr_ref, 256)

    def k1_group(gidx, c):
        idx = [gidx * FFT_UNROLL + u for u in range(FFT_UNROLL)]
        xs = [_stage2(n1, s * kg + i, f2_ref, scr_ref) for i in idx]
        ys = []
        for i, (xr, xi) in zip(idx, xs):
            off = pl.multiple_of(i * FFT_N2, FFT_N2)
            kr = kf_ref[0, pl.ds(off, FFT_N2), :]
            ki = kf_ref[1, pl.ds(off, FFT_N2), :]
            ys.append(jnp.concatenate([xr * kr - xi * ki, xr * ki + xi * kr], axis=1).astype(BF16))
        rs = [_dot(f2_ref[...], yc) for yc in ys]
        for i, r in zip(idx, rs):
            k1 = s * kg + i
            scr_ref[0, pl.ds(k1, FFT_N2, stride=n1 + FFT_PAD), :] = r[:128, :128] + r[128:, 128:]
            scr_ref[1, pl.ds(k1, FFT_N2, stride=n1 + FFT_PAD), :] = r[:128, 128:] - r[128:, :128]
        return c

    lax.fori_loop(0, kg // FFT_UNROLL, k1_group, 0)

    @pl.when(s == pl.num_programs(2) - 1)
    def _():
        def group(gidx, c):
            n2s = [gidx * FFT_UNROLL + u for u in range(FFT_UNROLL)]
            bcs = []
            for n2 in n2s:
                off = pl.multiple_of(n2 * (n1 + FFT_PAD), 8)
                bcs.append(jnp.concatenate([scr_ref[0, pl.ds(off, n1), :], scr_ref[1, pl.ds(off, n1), :]],
                                           axis=1).astype(BF16))
            rs = [_dot(gi_ref[n2], bc) for n2, bc in zip(n2s, bcs)]
            for n2, r in zip(n2s, rs):
                o_ref[0, 0, pl.ds(n2, nh, stride=FFT_N2), :] = r[:nh, :128] + r[nh:, 128:]
                o_ref[0, 1, pl.ds(n2, nh, stride=FFT_N2), :] = r[:nh, 128:] - r[nh:, :128]
            return c
        lax.fori_loop(0, FFT_N2 // FFT_UNROLL, group, 0)


def _hyena_conv(z4, col0, kf, length):
    gf, gi, f2, n1 = _fft_tables(length)
    n = 2 * length
    kg = min(n1, 16)
    npair = z4.shape[0]
    big = pl.Buffered(1)
    return pl.pallas_call(
        functools.partial(_hyconv_kernel, n1, kg), grid=(npair, 2, n1 // kg),
        in_specs=[pl.BlockSpec((1, 2, length, 128), lambda p, c, s: (p, 0, 0, col0 + c), pipeline_mode=big),
                  pl.BlockSpec(gf.shape, lambda p, c, s: (0, 0, 0), pipeline_mode=big),
                  pl.BlockSpec(f2.shape, lambda p, c, s: (0, 0)),
                  pl.BlockSpec((2, kg * FFT_N2, 128), lambda p, c, s: (0, s, c)),
                  pl.BlockSpec(gi.shape, lambda p, c, s: (0, 0, 0), pipeline_mode=big)],
        out_specs=pl.BlockSpec((1, 2, length, 128), lambda p, c, s: (p, 0, 0, c), pipeline_mode=big),
        out_shape=jax.ShapeDtypeStruct((npair, 2, length, 256), F32),
        scratch_shapes=[pltpu.VMEM((2, FFT_N2 * (n1 + FFT_PAD), 128), F32)],
        compiler_params=_params(("arbitrary", "arbitrary", "arbitrary")), name="hyena_fftconv",
    )(z4, gf, f2, kf, gi)


def _hyfft_kernel(n1, kg, z_ref, gf_ref, f2_ref, o_ref, scr_ref):
    s = pl.program_id(2)

    @pl.when(s == 0)
    def _():
        _stage1(n1, lambda n2: z_ref[pl.ds(n2, n1, stride=FFT_N2), :].astype(BF16), gf_ref, scr_ref, 128)

    def k1_group(gidx, c):
        idx = [gidx * FFT_UNROLL + u for u in range(FFT_UNROLL)]
        xs = [_stage2(n1, s * kg + i, f2_ref, scr_ref) for i in idx]
        for i, (xr, xi) in zip(idx, xs):
            off = pl.multiple_of(i * FFT_N2, FFT_N2)
            o_ref[0, 0, pl.ds(off, FFT_N2), :] = xr
            o_ref[0, 1, pl.ds(off, FFT_N2), :] = xi
        return c

    lax.fori_loop(0, kg // FFT_UNROLL, k1_group, 0)


@functools.lru_cache(maxsize=None)
def _fft_table_full(length):
    n = 2 * length
    n1 = n // FFT_N2
    k1 = np.arange(n1)[None, :, None].astype(np.float64)
    j = np.arange(n1)[None, None, :].astype(np.float64)
    n2 = np.arange(FFT_N2)[:, None, None].astype(np.float64)
    ang = -2.0 * np.pi * k1 * (FFT_N2 * j + n2) / n
    return np.concatenate([np.cos(ang), np.sin(ang)], axis=1).astype(np.float32).astype(BF16)


def _filter_spectrum(k2, length):
    _, _, f2, n1 = _fft_tables(length)
    gf = _fft_table_full(length)
    n = 2 * length
    kg = min(n1, 16)
    nf = k2.shape[1] // D_HY
    big = pl.Buffered(1)
    return pl.pallas_call(
        functools.partial(_hyfft_kernel, n1, kg), grid=(nf, 2, n1 // kg),
        in_specs=[pl.BlockSpec((n, 128), lambda f, c, s: (0, 2 * f + c), pipeline_mode=big),
                  pl.BlockSpec(gf.shape, lambda f, c, s: (0, 0, 0), pipeline_mode=big),
                  pl.BlockSpec(f2.shape, lambda f, c, s: (0, 0))],
        out_specs=pl.BlockSpec((1, 2, kg * FFT_N2, 128), lambda f, c, s: (f, 0, s, c)),
        out_shape=jax.ShapeDtypeStruct((nf, 2, n, 256), F32),
        scratch_shapes=[pltpu.VMEM((2, FFT_N2 * (n1 + FFT_PAD), 128), F32)],
        compiler_params=_params(("arbitrary", "arbitrary", "arbitrary")), name="hyena_filter_fft",
    )(k2, gf, f2)


def _normalised(k2, ssq, extra_scale):
    return k2 * (lax.rsqrt(ssq + EPS) * extra_scale)


def _hy_mid_kernel(y_ref, hv_ref, b_ref, o_ref):
    v = hv_ref[0][:, 0:D_HY]
    x1 = hv_ref[0][:, D_HY:2 * D_HY]
    o_ref[0] = x1 * (y_ref[0] + v * b_ref[...])


def _hy_mid(y1, hv, bias0):
    nb, length, _ = y1.shape
    te = 4 * TT
    return pl.pallas_call(
        _hy_mid_kernel, grid=(nb, length // te),
        in_specs=[pl.BlockSpec((1, te, D_HY), lambda b, t: (b, t, 0)),
                  pl.BlockSpec((1, te, 512), lambda b, t: (b, t, 0)),
                  pl.BlockSpec((1, D_HY), lambda b, t: (0, 0))],
        out_specs=pl.BlockSpec((1, te, D_HY), lambda b, t: (b, t, 0)),
        out_shape=jax.ShapeDtypeStruct(y1.shape, F32),
        compiler_params=_params(("arbitrary", "arbitrary")), name="hyena_mid",
    )(y1, hv, bias0.reshape(1, D_HY))


def _hy_ctx_kernel(lc, hv_ref, zh_ref, k2_ref, b_ref, fc_ref, ff_ref, ci_ref, o_ref):
    nf = 2 * lc
    hv = hv_ref[0]

    def conv(z, o):
        kf = _dot_exact_rhs(ff_ref[...], k2_ref[:, o * D_HY:(o + 1) * D_HY])
        zs = _dot(fc_ref[...], z.astype(BF16))
        zr, zi = zs[:nf], zs[nf:]
        kr, ki = kf[:nf], kf[nf:]
        ys = jnp.concatenate([zr * kr - zi * ki, zr * ki + zi * kr], axis=0).astype(BF16)
        return _dot(ci_ref[...], ys) * (1.0 / nf)

    v = hv[:, 0:D_HY]
    x1 = hv[:, D_HY:2 * D_HY]
    x2 = hv[:, 2 * D_HY:3 * D_HY]
    z1 = x1 * (conv(v, 0) + v * b_ref[0:1])
    p = x2 * (conv(z1, 1) + z1 * b_ref[1:2])
    o_ref[0] = (p * _silu(zh_ref[0])).astype(BF16)


def _dot_exact_rhs(a_bf16, b):
    bh, bm, bl = _split3(b)
    return _dot(a_bf16, bh) + _dot(a_bf16, bm) + _dot(a_bf16, bl)


@functools.lru_cache(maxsize=None)
def _ctx_dft_tables(lc):
    nf = 2 * lc
    k = np.arange(nf)[:, None].astype(np.float64)
    ang = -2.0 * np.pi * k * np.arange(nf)[None, :] / nf
    ff = np.concatenate([np.cos(ang), np.sin(ang)], axis=0)
    fc = ff[:, :lc]
    a2 = 2.0 * np.pi * np.arange(lc)[:, None] * np.arange(nf)[None, :] / nf
    ci = np.concatenate([np.cos(a2), -np.sin(a2)], axis=1)
    cv = lambda m: np.asarray(m, np.float32).astype(BF16)
    return cv(fc), cv(ff), cv(ci)


def _hy_ctx(hv, pf, k2c, hy_bias, nt, lc):
    nb = hv.shape[0]
    fc, ff, ci = _ctx_dft_tables(lc)
    c2 = lambda b: (0, 0)
    return pl.pallas_call(
        functools.partial(_hy_ctx_kernel, lc), grid=(nb,),
        in_specs=[pl.BlockSpec((1, lc, 768), lambda b: (b, nt, 0)),
                  pl.BlockSpec((1, lc, D_HY), lambda b: (b, nt, PF_ZH)),
                  pl.BlockSpec(k2c.shape, c2),
                  pl.BlockSpec((2, D_HY), c2),
                  pl.BlockSpec(fc.shape, c2), pl.BlockSpec(ff.shape, c2), pl.BlockSpec(ci.shape, c2)],
        out_specs=pl.BlockSpec((1, lc, D_HY), lambda b: (b, 0, 0)),
        out_shape=jax.ShapeDtypeStruct((nb, lc, D_HY), BF16),
        compiler_params=_params(("arbitrary",)), name="hyena_ctx",
    )(hv, pf, k2c, hy_bias, fc, ff, ci)


def _out_kernel(n_hy, na_ref, h0_ref, h1_ref, om_ref, zm_ref, w_ref, x_ref, mod_ref, g_ref, *rest):
    hy_refs, o_ref = rest[:n_hy], rest[n_hy]
    chunks = [slice(r, r + TT) for r in range(0, o_ref.shape[1], TT)]
    acts = []
    for rows in chunks:
        if n_hy == 1:
            hy = hy_refs[0][0, rows]
        else:
            y_ref, z_ref, x2_ref, zh_ref, b_ref = hy_refs
            z = z_ref[0, rows]
            hy = (x2_ref[0, rows] * (y_ref[0, rows] + z * b_ref[...]) * _silu(zh_ref[0, rows])).astype(BF16)
        ml = (jax.nn.sigmoid(om_ref[0, rows]) * (h0_ref[0, 0, rows] + h1_ref[0, 0, rows])
              * _silu(zm_ref[0, rows])).astype(BF16)
        acts.append((ml, hy))
    ys = [_dot(na_ref[0, rows], w_ref[0:D_NA]) + _dot(ml, w_ref[D_NA:D_NA + D_ML])
          + _dot(hy, w_ref[D_NA + D_ML:D_MODEL]) for rows, (ml, hy) in zip(chunks, acts)]
    for rows, y in zip(chunks, ys):
        ms = jnp.mean(y * y, axis=-1, keepdims=True)
        r = y * lax.rsqrt(ms + EPS) * g_ref[...]
        o_ref[0, rows] = x_ref[0, rows] + mod_ref[0][2:3] * r


def _out_projection(na_g, hml, pf, hy_args, w_out, res, res_t0, mod3, g_post, to, t0, ntiles, mod_row, out_rows,
                    prev=None):
    nb = na_g.shape[0]
    d = D_MODEL
    tok = lambda col: (lambda b, t: (b, t + t0, col))
    own = lambda col: (lambda b, t: (b, t, col))
    in_specs = [pl.BlockSpec((1, to, 512), tok(0)),
                pl.BlockSpec((1, 1, to, 256), lambda b, t: (b, 0, t + t0, 0)),
                pl.BlockSpec((1, 1, to, 256), lambda b, t: (b, 1, t + t0, 0)),
                pl.BlockSpec((1, to, 256), tok(PF_OM)), pl.BlockSpec((1, to, 256), tok(PF_ZM)),
                pl.BlockSpec((d, d), lambda b, t: (0, 0), pipeline_mode=pl.Buffered(1)),
                pl.BlockSpec((1, to, d), lambda b, t: (b, t + res_t0, 0)),
                pl.BlockSpec((1, 3, d), lambda b, t: (mod_row(b), 0, 0)),
                pl.BlockSpec((1, d), lambda b, t: (0, 0))]
    args = [na_g, hml, hml, pf, pf, w_out, res, mod3, g_post.reshape(1, d)]
    if len(hy_args) == 1:
        in_specs.append(pl.BlockSpec((1, to, D_HY), own(0)))
        args.append(hy_args[0])
    else:
        y2, z1, hv, pf_, bias1 = hy_args
        in_specs += [pl.BlockSpec((1, to, D_HY), own(0)), pl.BlockSpec((1, to, D_HY), own(0)),
                     pl.BlockSpec((1, to, D_HY), tok(2)), pl.BlockSpec((1, to, D_HY), tok(PF_ZH)),
                     pl.BlockSpec((1, D_HY), lambda b, t: (0, 0))]
        args += [y2, z1, hv, pf_, bias1.reshape(1, D_HY)]
    n_hy = len(hy_args)
    aliases = {}
    if prev is not None:
        in_specs.append(pl.BlockSpec(memory_space=pl.ANY))
        args.append(prev)
        aliases = {len(args) - 1: 0}
        kern = lambda *r: _out_kernel(n_hy, *r[:9 + n_hy], r[10 + n_hy])
    else:
        kern = functools.partial(_out_kernel, n_hy)
    return pl.pallas_call(
        kern, grid=(nb, ntiles), in_specs=in_specs,
        out_specs=pl.BlockSpec((1, to, d), tok(0)),
        out_shape=jax.ShapeDtypeStruct((nb, out_rows, d), F32),
        input_output_aliases=aliases,
        compiler_params=_params(("arbitrary", "arbitrary")), name="out_proj_norm_residual",
    )(*args)


@functools.lru_cache(maxsize=None)
def _rope_tables(length, lc):
    n = HEAD_DIM // 4
    t = np.arange(length)
    inv = (ROPE_BASE ** (-np.arange(n, dtype=np.float32) / n)).astype(np.float32)
    pos = np.stack([t // GRID_W, t % GRID_W], axis=-1).astype(np.float32)
    ang = (pos[:, :, None] * inv).astype(np.float32)
    cos_h = np.concatenate([np.cos(ang[:, 0]), np.cos(ang[:, 0]), np.cos(ang[:, 1]), np.cos(ang[:, 1])], axis=-1)
    sin_h = np.concatenate([-np.sin(ang[:, 0]), np.sin(ang[:, 0]), -np.sin(ang[:, 1]), np.sin(ang[:, 1])], axis=-1)
    cos_t = np.concatenate([np.tile(cos_h, (1, 2 * H_ML)), np.ones((lc, 512), np.float32)], axis=0)
    sin_t = np.concatenate([np.tile(sin_h, (1, 2 * H_ML)), np.zeros((lc, 512), np.float32)], axis=0)
    kscale = np.concatenate([np.ones(256, np.float32), np.full(256, HEAD_DIM ** -0.5, np.float32)])
    return (cos_t * kscale).astype(np.float32), (sin_t * kscale).astype(np.float32)


@functools.lru_cache(maxsize=None)
def _scan_constants():
    tril = np.tril(np.ones((TT, TT), np.float32))
    emat = np.zeros((128, D_ML), np.float32)
    for h in range(H_ML):
        emat[h, h * HEAD_DIM:(h + 1) * HEAD_DIM] = 1.0
    rh = np.arange(D_ML)[:, None] // HEAD_DIM
    chd = np.arange(D_ML)[None, :] // HEAD_DIM
    bmask = (rh == chd).astype(np.float32)
    return tril.astype(BF16), np.ascontiguousarray(tril.T).astype(BF16), emat.astype(BF16), bmask


def _layer(x_lat, x_ctx, ctx_tile, mod3, nb, length, lc, g_pre, g_post, w_in, b_if, conv_ml, conv_hy, rpb,
           hf_w1, hf_b1, hf_w2, hf_b2, hf_w3, hf_freq, hy_bias, w_out, update_ctx):
    nt = length // TT
    ltot = length + lc
    tril, triu, emat, bmask = _scan_constants()
    cos_t, sin_t = _rope_tables(length, lc)

    c = lambda a, b: w_in[:, a:b]
    wna = jnp.concatenate([c(0, 512) * (HEAD_DIM ** -0.5), c(512, 1536)], axis=1).astype(BF16)
    wpr = jnp.concatenate([c(1536, 2048), c(2560, 2816), c(2816, 3072), c(3072, 3328), c(4112, 4368)],
                          axis=1).astype(BF16)
    wlo = jnp.concatenate([c(2048, 2560), c(3344, 4112), c(3328, 3344), jnp.zeros((D_MODEL, 112), F32)],
                          axis=1).astype(BF16)
    bif_row = jnp.zeros((1, 128), F32).at[0, :4 * H_ML].set(b_if.reshape(-1))
    ona, pf, qk, hv, gates = _projection(x_lat, x_ctx, ctx_tile, mod3, g_pre, wna, wpr, wlo, conv_ml, conv_hy,
                                         bif_row, cos_t, sin_t, tril, triu, nt)

    na_g = _attention(ona, pf, _na_bias_table(rpb), nt, update_ctx)

    hml = _mlstm_scan(qk, pf, gates, emat, bmask, nt)

    k2, ssq = _hyena_filters(length, hf_w1, hf_b1, hf_w2, hf_b2, hf_w3, hf_freq)
    kf = _filter_spectrum(_normalised(k2, ssq, 1.0 / (2 * length)), length)
    hv4 = hv.reshape(nb // 2, 2, ltot, 768)
    y1 = _hyena_conv(hv4, 0, kf[0], length).reshape(nb, length, D_HY)
    z1 = _hy_mid(y1, hv, hy_bias[0])
    y2 = _hyena_conv(z1.reshape(nb // 2, 2, length, D_HY), 0, kf[1], length).reshape(nb, length, D_HY)
    hy_args = (y2, z1, hv, pf, hy_bias[1])
    to = 2 * TT
    if not update_ctx:
        return _out_projection(na_g, hml, pf, hy_args, w_out, x_lat, 0, mod3, g_post, to, 0, length // to,
                               lambda b: b, length)

    xn = _out_projection(na_g, hml, pf, hy_args, w_out, x_lat, 0, mod3, g_post, to, 0, length // to,
                         lambda b: b, ltot)
    k2c, ssq_c = _hyena_filters(lc, hf_w1, hf_b1, hf_w2, hf_b2, hf_w3, hf_freq)
    hyc_g = _hy_ctx(hv, pf, _normalised(k2c, ssq_c, 1.0), hy_bias, nt, lc)
    return _out_projection(na_g, hml, pf, (hyc_g,), w_out, x_ctx, ctx_tile, mod3, g_post, TT, nt, 1,
                           lambda b: nb, ltot, prev=xn)


def kernel(x, c, ctx, c_ctx, w_ada, b_ada, g_pre, g_post, w_in, b_if, conv_ml, conv_hy, rpb, hf_w1, hf_b1,
           hf_w2, hf_b2, hf_w3, hf_freq, hy_bias, w_out):
    nb, length, d = x.shape
    lc = ctx.shape[1]
    depth = w_in.shape[0]
    assert lc == TT and length % (4 * TT) == 0 and nb % 2 == 0 and nb % SCAN_NB == 0 and d == D_MODEL
    cc = jnp.zeros((16, d), F32).at[:nb].set(c).at[nb].set(c_ctx)
    x_lat, x_ctx, ctx_tile = x, ctx, 0
    for l in range(depth):
        mod3 = _modulation(cc, w_ada[l], b_ada[l]).reshape(16, 3, d)
        x_lat = _layer(x_lat, x_ctx, ctx_tile, mod3, nb, length, lc, g_pre[l], g_post[l], w_in[l], b_if[l],
                       conv_ml[l], conv_hy[l], rpb[l], hf_w1[l], hf_b1[l], hf_w2[l], hf_b2[l], hf_w3[l],
                       hf_freq[l], hy_bias[l], w_out[l].astype(BF16), l < depth - 1)
        x_ctx, ctx_tile = x_lat, length // TT
    return x_lat
```

```python
import functools
import math

import numpy as np
import jax
import jax.numpy as jnp
from jax import lax
from jax.experimental import pallas as pl
from jax.experimental.pallas import tpu as pltpu

F32 = jnp.float32
BF16 = jnp.bfloat16

D_MODEL = 1024
GRID_W = 64
HEAD_DIM = 64
D_NA = 512
D_ML = 256
D_HY = 256
H_NA = 8
H_ML = 4
NA_ROWS = 8
NA_COLS = 16
HY_EMB = 33
HY_FFN = 64
HY_TARGET = 1e-2
HY_FAST = 0.3
HY_SLOW = 1.5
ROPE_BASE = 10000.0
EPS = 1e-6

TT = 256
BAND_ROWS = TT // GRID_W
SCAN_NB = 4
PRE_ROWS = 64
FFT_N2 = 128
FFT_PAD = 8
FFT_UNROLL = 8
NEG = -1e30
VMEM_LIMIT = 56 * 1024 * 1024

PF_ZA = 0
PF_VM, PF_OM, PF_ZM, PF_ZH = 2, 3, 4, 5
LOC_QK, LOC_HY, LOC_GM, LOC_COLS = 0, 512, 1280, 1408
HALO = 8


def _dot(a, b):
    return jnp.dot(a, b, preferred_element_type=F32)


def _dot_nt(a, b):
    return lax.dot_general(a, b, (((1,), (1,)), ((), ())), preferred_element_type=F32)


def _dot_tn(a, b):
    return lax.dot_general(a, b, (((0,), (0,)), ((), ())), preferred_element_type=F32)


def _split3(x):
    h = x.astype(BF16)
    r = x - h.astype(F32)
    m = r.astype(BF16)
    l = (r - m.astype(F32)).astype(BF16)
    return h, m, l


def _dot_f32(a, b):
    ah, am, al = _split3(a)
    bh, bm, bl = _split3(b)
    return (_dot(ah, bh) + (_dot(ah, bm) + _dot(am, bh))
            + (_dot(ah, bl) + _dot(al, bh) + _dot(am, bm)))


def _dot_2x(a, b):
    ah, am, _ = _split3(a)
    bh, bm, _ = _split3(b)
    return _dot(ah, bh) + (_dot(ah, bm) + _dot(am, bh))


def _silu(x):
    return x * jax.nn.sigmoid(x)


def _params(sem, vmem=VMEM_LIMIT):
    return pltpu.CompilerParams(dimension_semantics=sem, vmem_limit_bytes=vmem)


def _mod_kernel(c_ref, w_ref, b_ref, o_ref):
    o_ref[...] = _dot_f32(_silu(c_ref[...]), w_ref[...]) + b_ref[...]


def _modulation(cc, w_ada, b_ada):
    n = w_ada.shape[1]
    bn = 512
    return pl.pallas_call(
        _mod_kernel, grid=(n // bn,),
        in_specs=[pl.BlockSpec(cc.shape, lambda j: (0, 0)),
                  pl.BlockSpec((w_ada.shape[0], bn), lambda j: (0, j)),
                  pl.BlockSpec((1, bn), lambda j: (0, j))],
        out_specs=pl.BlockSpec((cc.shape[0], bn), lambda j: (0, j)),
        out_shape=jax.ShapeDtypeStruct((cc.shape[0], n), F32),
        compiler_params=_params(("arbitrary",)), name="adaln_mod",
    )(cc, w_ada, b_ada.reshape(1, n))


def _proj_kernel(nt, x_ref, c_ref, xp_ref, xn_ref, mod_ref, g_ref, wna_ref, wpr_ref, wlo_ref, cml_ref, chy_ref,
                 bif_ref, cos_ref, sin_ref, tri_ref, ona_ref, opr_ref, oqk_ref, ohv_ref, og_ref, loc_ref):
    t = pl.program_id(0)
    x = jnp.where(t == nt, c_ref[0], x_ref[0])
    xs = jnp.concatenate([xp_ref[0], x, xn_ref[0]], axis=0)
    ms = jnp.mean(xs * xs, axis=-1, keepdims=True)
    y = xs * lax.rsqrt(ms + EPS) * g_ref[...]
    mod = mod_ref[0]
    h = (y * (1.0 + mod[1:2]) + mod[0:1]).astype(BF16)
    hm = h[HALO:HALO + TT]
    loc_ref[...] = _dot(h, wlo_ref[...])
    ona_ref[0] = _dot(hm, wna_ref[...]).astype(BF16)

    pm = jnp.where(jnp.logical_and(t != 0, t != nt), 1.0, 0.0)
    nm = jnp.where(jnp.logical_and(t != nt - 1, t != nt), 1.0, 0.0)
    row8 = lax.broadcasted_iota(jnp.int32, (8, 1), 0)
    nchunk = TT // PRE_ROWS

    def conv3(cols, w, c):
        r0 = HALO + c * PRE_ROWS
        u = loc_ref[r0:r0 + PRE_ROWS, cols]
        up = loc_ref[r0 - 1:r0, cols]
        un = loc_ref[r0 + PRE_ROWS:r0 + PRE_ROWS + 1, cols]
        if c == 0:
            up = up * pm
        if c == nchunk - 1:
            un = un * nm
        rp = pltpu.roll(u, 1, 0)
        rn = pltpu.roll(u, PRE_ROWS - 1, 0)
        prev = jnp.concatenate([jnp.where(row8 == 0, up, rp[0:8]), rp[8:]], axis=0)
        nxt = jnp.concatenate([rn[:PRE_ROWS - 8], jnp.where(row8 == 7, un, rn[PRE_ROWS - 8:])], axis=0)
        return prev * w[0:1] + u * w[1:2] + nxt * w[2:3]

    lane = lax.broadcasted_iota(jnp.int32, (1, 2 * D_ML), 1)
    first = (lane % 32) < 16
    for c in range(nchunk):
        rows = slice(c * PRE_ROWS, (c + 1) * PRE_ROWS)
        ohv_ref[0, rows] = conv3(slice(LOC_HY, LOC_GM), chy_ref[...], c)
        x = _silu(conv3(slice(LOC_QK, LOC_HY), cml_ref[...], c))
        partner = jnp.where(first, pltpu.roll(x, 2 * D_ML - 16, 1), pltpu.roll(x, 16, 1))
        oqk_ref[0, rows] = (x * cos_ref[rows] + partner * sin_ref[rows]).astype(BF16)

    g = loc_ref[HALO:HALO + TT, LOC_GM:LOC_COLS] + bif_ref[...]
    gl = lax.broadcasted_iota(jnp.int32, (1, 128), 1)
    is_f = jnp.logical_and((gl % 8) >= 4, gl < 16)
    lf = jnp.where(is_f, jnp.minimum(g, 0.0) - jnp.log1p(jnp.exp(-jnp.abs(g))), 0.0)
    hi, mid, lo = _split3(lf)
    packed = (hi.astype(F32) + pltpu.roll(mid.astype(F32), 16, 1) + pltpu.roll(lo.astype(F32), 32, 1)).astype(BF16)
    cs = _dot(tri_ref[...], packed)
    cs = cs + pltpu.roll(cs, 112, 1) + pltpu.roll(cs, 96, 1)
    bcum = jnp.where(gl < 8, cs[0:TT], cs[TT:2 * TT])
    a = g - pltpu.roll(bcum, 128 - H_ML, 1)
    ab = jnp.where(is_f, bcum, a)
    keep = gl < 2 * H_ML
    og_ref[0, 0] = jnp.where(keep, ab, 0.0)
    og_ref[0, 1] = jnp.where(keep, pltpu.roll(ab, 120, 1), 0.0)
    opr_ref[0] = _dot(hm, wpr_ref[...]).astype(BF16)


def _projection(x_lat, x_ctx, ctx_tile, mod3, g_pre, wna, wpr, wlo, conv_ml, conv_hy, bif_row, cos_t, sin_t,
                tri, nt):
    nb, lat_rows, d = x_lat.shape
    ltot = (nt + 1) * TT
    ctx_row = nb
    last8 = lat_rows // HALO - 1
    r8 = TT // HALO

    def mod_idx(t, b):
        return (jnp.where(t == nt, ctx_row, b), 0, 0)

    const = lambda t, b: (0, 0)
    big = pl.Buffered(1)
    tok = lambda t, b: (b, t, 0)
    return pl.pallas_call(
        functools.partial(_proj_kernel, nt), grid=(nt + 1, nb),
        in_specs=[pl.BlockSpec((1, TT, d), lambda t, b: (b, jnp.minimum(t, nt - 1), 0)),
                  pl.BlockSpec((1, TT, d), lambda t, b: (b, ctx_tile, 0)),
                  pl.BlockSpec((1, HALO, d), lambda t, b: (b, jnp.clip(t * r8 - 1, 0, last8), 0)),
                  pl.BlockSpec((1, HALO, d), lambda t, b: (b, jnp.minimum((t + 1) * r8, last8), 0)),
                  pl.BlockSpec((1, 3, d), mod_idx),
                  pl.BlockSpec((1, d), const),
                  pl.BlockSpec(wna.shape, const, pipeline_mode=big),
                  pl.BlockSpec(wpr.shape, const, pipeline_mode=big),
                  pl.BlockSpec(wlo.shape, const, pipeline_mode=big),
                  pl.BlockSpec((3, 512), const), pl.BlockSpec((3, 768), const), pl.BlockSpec((1, 128), const),
                  pl.BlockSpec((TT, 512), lambda t, b: (t, 0)), pl.BlockSpec((TT, 512), lambda t, b: (t, 0)),
                  pl.BlockSpec((2 * TT, TT), const)],
        out_specs=[pl.BlockSpec((1, TT, wna.shape[1]), tok), pl.BlockSpec((1, TT, wpr.shape[1]), tok),
                   pl.BlockSpec((1, TT, 512), tok), pl.BlockSpec((1, TT, 768), tok),
                   pl.BlockSpec((1, 2, TT, 128), lambda t, b: (b, 0, t, 0))],
        out_shape=[jax.ShapeDtypeStruct((nb, ltot, wna.shape[1]), BF16),
                   jax.ShapeDtypeStruct((nb, ltot, wpr.shape[1]), BF16),
                   jax.ShapeDtypeStruct((nb, ltot, 512), BF16),
                   jax.ShapeDtypeStruct((nb, ltot, 768), F32),
                   jax.ShapeDtypeStruct((nb, 2, ltot, 128), F32)],
        scratch_shapes=[pltpu.VMEM((TT + 2 * HALO, LOC_COLS), F32)],
        compiler_params=_params(("arbitrary", "arbitrary")), name="norm_in_proj_conv",
    )(x_lat, x_ctx, x_lat, x_lat, mod3, g_pre.reshape(1, d), wna, wpr, wlo, conv_ml, conv_hy, bif_row,
      cos_t, sin_t, tri)


def _na_kernel(q_ref, k0_ref, k1_ref, k2_ref, v0_ref, v1_ref, v2_ref, kc_ref, vc_ref, za_ref, tab_ref, o_ref):
    lane = lax.broadcasted_iota(jnp.int32, (1, 128), 1)
    k_refs = (k0_ref, k1_ref, k2_ref)
    v_refs = (v0_ref, v1_ref, v2_ref)

    def row_max(s):
        return jnp.max(jnp.maximum(s[:, :128], s[:, 128:]), axis=-1, keepdims=True)

    def head_mask(h):
        return (lane >= HEAD_DIM) if h % 2 else (lane < HEAD_DIM)

    def masked_q(h):
        sl = slice((h // 2) * 128, (h // 2 + 1) * 128)
        q2 = q_ref[0, :, sl]
        return jnp.where(head_mask(h), q2, jnp.zeros_like(q2))

    def score_block(h, qm, blk):
        sl = slice((h // 2) * 128, (h // 2 + 1) * 128)
        if blk == 0:
            return _dot_nt(qm, kc_ref[0, :, sl])
        return _dot_nt(qm, k_refs[blk - 1][0, :, sl]) + tab_ref[0, h, :, (blk - 1) * TT:blk * TT]

    def values(h, p):
        sl = slice((h // 2) * 128, (h // 2 + 1) * 128)
        hm = head_mask(h)
        v = jnp.concatenate([vc_ref[0, :, sl]] + [r[0, :, sl] for r in v_refs], axis=0)
        acc = _dot(p, jnp.where(hm, v, jnp.ones_like(v)))
        return jnp.where(hm, acc / pltpu.roll(acc, HEAD_DIM, 1), 0.0)

    outs = {}
    qm = masked_q(0)
    sc = [score_block(0, qm, blk) for blk in range(4)]
    for h in range(H_NA):
        m = row_max(jnp.maximum(jnp.maximum(sc[0], sc[1]), jnp.maximum(sc[2], sc[3])))
        qm = masked_q(h + 1) if h + 1 < H_NA else None
        nxt, ps = [], []
        for blk in range(4):
            if qm is not None:
                nxt.append(score_block(h + 1, qm, blk))
            ps.append(jnp.exp(sc[blk] - m).astype(BF16))
        sc = nxt
        outs[h] = values(h, jnp.concatenate(ps, axis=1))
        if h % 2 == 1:
            sl = slice((h // 2) * 128, (h // 2 + 1) * 128)
            o_ref[0, :, sl] = ((outs[h - 1] + outs[h]) * _silu(za_ref[0, :, sl].astype(F32))).astype(BF16)


def _attention(ona, pf, table, nt, with_ctx_queries):
    nb, ltot, _ = ona.shape
    nj = nt + 1 if with_ctx_queries else nt

    def kv(col, dj):
        return lambda j, b: (b, jnp.clip(j + dj, 0, nt - 1), col)

    def pat(j, b):
        p = jnp.where(j == 0, 0, jnp.where(j == nt - 1, 2, jnp.where(j == nt, 3, 1)))
        return (p, 0, 0, 0)

    return pl.pallas_call(
        _na_kernel, grid=(nj, nb),
        in_specs=[pl.BlockSpec((1, TT, 512), lambda j, b: (b, j, 0)),
                  pl.BlockSpec((1, TT, 512), kv(1, -1)), pl.BlockSpec((1, TT, 512), kv(1, 0)),
                  pl.BlockSpec((1, TT, 512), kv(1, 1)),
                  pl.BlockSpec((1, TT, 512), kv(2, -1)), pl.BlockSpec((1, TT, 512), kv(2, 0)),
                  pl.BlockSpec((1, TT, 512), kv(2, 1)),
                  pl.BlockSpec((1, TT, 512), lambda j, b: (b, nt, 1)),
                  pl.BlockSpec((1, TT, 512), lambda j, b: (b, nt, 2)),
                  pl.BlockSpec((1, TT, 512), lambda j, b: (b, j, PF_ZA)),
                  pl.BlockSpec((1, H_NA, TT, 3 * TT), pat)],
        out_specs=pl.BlockSpec((1, TT, 512), lambda j, b: (b, j, 0)),
        out_shape=jax.ShapeDtypeStruct((nb, ltot, 512), BF16),
        compiler_params=_params(("arbitrary", "arbitrary")), name="nbr_attention",
    )(ona, ona, ona, ona, ona, ona, ona, ona, ona, pf, table)


def _na_bias_table(rpb):
    a = np.arange(TT) // GRID_W
    qc = np.arange(TT) % GRID_W
    kidx = np.arange(3 * TT)
    kr = (kidx // TT - 1) * BAND_ROWS + (kidx % TT) // GRID_W
    kcol = kidx % GRID_W
    qstart = np.clip(qc - NA_COLS // 2, 0, GRID_W - NA_COLS)
    col_ok = (kcol[None, :] >= qstart[:, None]) & (kcol[None, :] < qstart[:, None] + NA_COLS)
    coff = np.clip(kcol[None, :] - qc[:, None], -(NA_COLS - 1), NA_COLS - 1) + NA_COLS - 1
    roff = np.clip(kr[None, :] - a[:, None] + NA_ROWS - 1, 0, 2 * NA_ROWS - 2)
    half = NA_ROWS // 2
    row_ok = [
        (kr[None, :] >= 0) & (kr[None, :] < NA_ROWS) & (a[:, None] >= 0),
        (kr[None, :] >= a[:, None] - half) & (kr[None, :] < a[:, None] + half),
        (kr[None, :] >= BAND_ROWS - NA_ROWS) & (kr[None, :] < BAND_ROWS) & (a[:, None] >= 0),
    ]
    nr, ncol = 2 * NA_ROWS - 1, 2 * NA_COLS - 1
    onehot = (coff[:GRID_W, :GRID_W].reshape(1, -1) == np.arange(ncol)[:, None]).astype(np.float32)
    cmat = jnp.dot(rpb.astype(F32).reshape(H_NA * nr, ncol), onehot,
                   precision=lax.Precision.HIGHEST).reshape(H_NA, nr, GRID_W, GRID_W)
    nkr = 3 * BAND_ROWS
    blocks = [jnp.concatenate([cmat[:, roff[ai * GRID_W, kj * GRID_W]] for kj in range(nkr)], axis=-1)
              for ai in range(BAND_ROWS)]
    bias = jnp.concatenate(blocks, axis=1)
    tabs = [jnp.where((r & col_ok)[None], bias, NEG) for r in row_ok]
    tabs.append(jnp.full_like(bias, NEG))
    return jnp.stack(tabs, axis=0)


def _head_lane_sums(x, lane_h):
    out = jnp.zeros_like(x)
    half = lax.broadcasted_iota(jnp.int32, (1, 128), 1) // HEAD_DIM
    for h in range(H_ML):
        t = x[:, (h // 2) * 128:(h // 2 + 1) * 128]
        sm = jnp.sum(jnp.where(half == h % 2, t, 0.0), axis=-1, keepdims=True)
        out = jnp.where(lane_h == h, sm, out)
    return out


def _scan_kernel(q_ref, k_ref, v_ref, g_ref, e_ref, bm_ref, o_ref, w_ref, n_ref, m_ref):
    s = pl.program_id(2)
    sgn = 1 - 2 * pl.program_id(1)

    @pl.when(s == 0)
    def _():
        w_ref[...] = jnp.zeros_like(w_ref)
        n_ref[...] = jnp.zeros_like(n_ref)
        m_ref[...] = jnp.zeros_like(m_ref)

    lane_h = lax.broadcasted_iota(jnp.int32, (1, D_ML), 1) // HEAD_DIM
    ri = lax.broadcasted_iota(jnp.int32, (TT, TT), 0)
    ci = lax.broadcasted_iota(jnp.int32, (TT, TT), 1)
    tri = ((ci - ri) * sgn) <= 0
    gate_lanes = lax.broadcasted_iota(jnp.int32, (1, 128), 1) < H_ML

    seqs = range(SCAN_NB)
    gate, kws, kwfs, vbs = [], [], [], []
    for i in seqs:
        gi = g_ref[i, 0]
        a = jnp.where(gate_lanes, gi, 0.0)
        bb = jnp.where(gate_lanes, pltpu.roll(gi, 128 - H_ML, 1), 0.0)
        m_prev = m_ref[i]
        mc = jnp.maximum(m_prev, jnp.max(a, axis=0, keepdims=True))
        small = jnp.concatenate([jnp.exp(a - mc), jnp.exp(-bb - mc),
                                 jnp.broadcast_to(jnp.exp(m_prev - mc), (8, 128))], axis=0)
        sh = small.astype(BF16)
        sl = (small - sh.astype(F32)).astype(BF16)
        ex = _dot(sh, e_ref[...]) + _dot(sl, e_ref[...])
        gate.append((ex[TT:2 * TT], ex[2 * TT:2 * TT + 1], jnp.min(bb, axis=0, keepdims=True) + mc))
        kwf = k_ref[i].astype(F32) * ex[0:TT]
        kwfs.append(kwf)
        kws.append(kwf.astype(BF16))
        vbs.append(v_ref[i].astype(BF16))

    scores = [[_dot_nt(jnp.where(lane_h == h, q_ref[i], jnp.zeros_like(q_ref[i])), kws[i]) for h in range(H_ML)]
              for i in seqs]
    inters = [_dot(q_ref[i], w_ref[i].astype(BF16)) for i in seqs]
    updates = [_dot_tn(kws[i], vbs[i]) for i in seqs]

    for i in seqs:
        eb_e, beta, m_new = gate[i]
        intra = None
        den_intra = jnp.zeros((TT, D_ML), F32)
        for h in range(H_ML):
            sc = jnp.where(tri, scores[i][h], 0.0)
            den_intra = jnp.where(lane_h == h, jnp.sum(sc, axis=-1, keepdims=True), den_intra)
            vh = jnp.where(lane_h == h, vbs[i], jnp.zeros_like(vbs[i]))
            part = _dot(sc.astype(BF16), vh)
            intra = part if intra is None else intra + part
        num = beta * inters[i] + intra
        den = beta * _head_lane_sums(q_ref[i].astype(F32) * n_ref[i], lane_h) + den_intra
        o_ref[i, 0] = (num / jnp.maximum(jnp.abs(den), eb_e)).astype(BF16)

        w_ref[i] = beta * w_ref[i] + bm_ref[...] * updates[i]
        n_ref[i] = beta * n_ref[i] + jnp.sum(kwfs[i], axis=0, keepdims=True)
        m_ref[i] = m_new


def _mlstm_scan(qk, pf, gates, emat, bmask, nt):
    nb, ltot, _ = qk.shape
    g = SCAN_NB

    def chunk(s, d):
        return jnp.where(s == 0, nt, jnp.where(d == 0, s - 1, nt - s))

    const = lambda b, d, s: (0, 0)
    return pl.pallas_call(
        _scan_kernel, grid=(nb // g, 2, nt + 1),
        in_specs=[pl.BlockSpec((g, TT, 256), lambda b, d, s: (b, chunk(s, d), 0)),
                  pl.BlockSpec((g, TT, 256), lambda b, d, s: (b, chunk(s, d), 1)),
                  pl.BlockSpec((g, TT, 256), lambda b, d, s: (b, chunk(s, d), PF_VM)),
                  pl.BlockSpec((g, 1, TT, 128), lambda b, d, s: (b, d, chunk(s, d), 0)),
                  pl.BlockSpec((128, 256), const), pl.BlockSpec((256, 256), const)],
        out_specs=pl.BlockSpec((g, 1, TT, 256), lambda b, d, s: (b, d, chunk(s, d), 0)),
        out_shape=jax.ShapeDtypeStruct((nb, 2, ltot, 256), BF16),
        scratch_shapes=[pltpu.VMEM((g, D_ML, D_ML), F32), pltpu.VMEM((g, 1, D_ML), F32),
                        pltpu.VMEM((g, 1, 128), F32)],
        compiler_params=_params(("arbitrary", "arbitrary", "arbitrary")), name="mlstm_scan",
    )(qk, qk, pf, gates, emat, bmask)


def _hyfilt_kernel(tl, length, feat_ref, w1_ref, b1_ref, w2_ref, b2_ref, w3_ref, fq_ref, dec_ref, o_ref, ss_ref):
    i = pl.program_id(0)
    a = jnp.sin(fq_ref[0:1] * (_dot_2x(feat_ref[...], w1_ref[...]) + b1_ref[...]))
    a = jnp.sin(fq_ref[1:2] * (_dot_2x(a, w2_ref[...]) + b2_ref[...]))
    dec = dec_ref[...]
    f = _dot_2x(a, w3_ref[...]) * jnp.concatenate([dec, dec, dec, dec], axis=1)
    fwd = jnp.concatenate([f[:, 0:D_HY], f[:, 2 * D_HY:3 * D_HY]], axis=1)
    bwd = jnp.concatenate([f[:, D_HY:2 * D_HY], f[:, 3 * D_HY:4 * D_HY]], axis=1)
    row = lax.broadcasted_iota(jnp.int32, (tl, 1), 0) + i * tl
    k2 = jnp.where(row < length, fwd, jnp.where(row > length, bwd, 0.0))
    o_ref[...] = k2

    @pl.when(i == 0)
    def _():
        ss_ref[...] = jnp.zeros_like(ss_ref)

    ss_ref[...] += jnp.sum(k2 * k2, axis=0, keepdims=True)


def _hyena_filters(length, w1, b1, w2, b2, w3, freq):
    t = np.arange(length, dtype=np.float32)
    tn = (t / np.float32(length - 1)).astype(np.float32)
    bands = (HY_EMB - 1) // 2
    fr = np.linspace(1e-4, bands - 1, bands, dtype=np.float32)
    ang = (np.float32(2.0 * math.pi / length) * t[:, None] * fr[None, :]).astype(np.float32)
    feat = np.zeros((length, 128), np.float32)
    feat[:, 0] = tn
    feat[:, 1:1 + bands] = np.cos(ang)
    feat[:, 1 + bands:1 + 2 * bands] = -np.sin(ang)
    deltas = np.abs(np.linspace(math.log(HY_TARGET) / HY_SLOW, math.log(HY_TARGET) / HY_FAST, D_HY, dtype=np.float32))
    dec = np.exp(-tn[:, None] * deltas[None, :]).astype(np.float32)
    lag = np.concatenate([np.arange(length), [0], np.arange(length - 1, 0, -1)])
    feat, dec = feat[lag], dec[lag]

    pad2 = lambda m, r, c: jnp.zeros((r, c), F32).at[:m.shape[0], :m.shape[1]].set(m.astype(F32))
    w1p = pad2(w1, 128, 128)
    w2p = pad2(w2, 128, 128)
    w3p = pad2(w3, 128, 4 * D_HY)
    b1p = pad2(b1[None], 1, 128)
    b2p = pad2(b2[None], 1, 128)
    fqp = pad2(freq, 2, 128)
    tl = 512
    const = lambda i: (0, 0)
    return pl.pallas_call(
        functools.partial(_hyfilt_kernel, tl, length), grid=(2 * length // tl,),
        in_specs=[pl.BlockSpec((tl, 128), lambda i: (i, 0)),
                  pl.BlockSpec((128, 128), const), pl.BlockSpec((1, 128), const),
                  pl.BlockSpec((128, 128), const), pl.BlockSpec((1, 128), const),
                  pl.BlockSpec((128, 4 * D_HY), const), pl.BlockSpec((2, 128), const),
                  pl.BlockSpec((tl, D_HY), lambda i: (i, 0))],
        out_specs=[pl.BlockSpec((tl, 2 * D_HY), lambda i: (i, 0)), pl.BlockSpec((1, 2 * D_HY), const)],
        out_shape=[jax.ShapeDtypeStruct((2 * length, 2 * D_HY), F32), jax.ShapeDtypeStruct((1, 2 * D_HY), F32)],
        compiler_params=_params(("arbitrary",)), name="hyena_filter_mlp",
    )(feat, w1p, b1p, w2p, b2p, w3p, fqp, dec)


@functools.lru_cache(maxsize=None)
def _fft_tables(length):
    n = 2 * length
    n1 = n // FFT_N2
    nh = n1 // 2
    k1 = np.arange(n1)[None, :, None].astype(np.float64)
    j = np.arange(nh)[None, None, :].astype(np.float64)
    n2 = np.arange(FFT_N2)[:, None, None].astype(np.float64)
    ang = -2.0 * np.pi * k1 * (FFT_N2 * j + n2) / n
    gf = np.concatenate([np.cos(ang), np.sin(ang)], axis=1)
    gi = np.concatenate([np.cos(ang).transpose(0, 2, 1), np.sin(ang).transpose(0, 2, 1)], axis=1)
    kk = np.arange(FFT_N2)[:, None] * np.arange(FFT_N2)[None, :]
    a2 = -2.0 * np.pi * kk / FFT_N2
    f2 = np.concatenate([np.cos(a2), np.sin(a2)], axis=0)
    cv = lambda m: np.asarray(m, np.float32).astype(BF16)
    return cv(gf), cv(gi), cv(f2), n1


def _stage1(n1, z_of, gf_ref, scr_ref, cols):
    def group(gidx, c):
        n2s = [gidx * FFT_UNROLL + u for u in range(FFT_UNROLL)]
        zs = [z_of(n2) for n2 in n2s]
        rs = [_dot(gf_ref[n2], z) for n2, z in zip(n2s, zs)]
        for n2, r in zip(n2s, rs):
            off = pl.multiple_of(n2 * (n1 + FFT_PAD), 8)
            if cols == 256:
                scr_ref[0, pl.ds(off, n1), :] = r[:n1, :128] - r[n1:, 128:]
                scr_ref[1, pl.ds(off, n1), :] = r[n1:, :128] + r[:n1, 128:]
            else:
                scr_ref[0, pl.ds(off, n1), :] = r[:n1]
                scr_ref[1, pl.ds(off, n1), :] = r[n1:]
        return c

    lax.fori_loop(0, FFT_N2 // FFT_UNROLL, group, 0)


def _stage2(n1, k1, f2_ref, scr_ref):
    ar = scr_ref[0, pl.ds(k1, FFT_N2, stride=n1 + FFT_PAD), :]
    ai = scr_ref[1, pl.ds(k1, FFT_N2, stride=n1 + FFT_PAD), :]
    r = _dot(f2_ref[...], jnp.concatenate([ar, ai], axis=1).astype(BF16))
    return r[:128, :128] - r[128:, 128:], r[128:, :128] + r[:128, 128:]


def _hyconv_kernel(n1, kg, z_ref, gf_ref, f2_ref, kf_ref, gi_ref, o_ref, scr_ref):
    s = pl.program_id(2)
    nh = n1 // 2

    @pl.when(s == 0)
    def _():
        def z_of(n2):
            zr = z_ref[0, 0, pl.ds(n2, nh, stride=FFT_N2), :]
            zi = z_ref[0, 1, pl.ds(n2, nh, stride=FFT_N2), :]
            return jnp.concatenate([zr, zi], axis=1).astype(BF16)
        _stage1(n1, z_of, gf_ref, scr_ref, 256)

    def k1_group(gidx, c):
        idx = [gidx * FFT_UNROLL + u for u in range(FFT_UNROLL)]
        xs = [_stage2(n1, s * kg + i, f2_ref, scr_ref) for i in idx]
        ys = []
        for i, (xr, xi) in zip(idx, xs):
            off = pl.multiple_of(i * FFT_N2, FFT_N2)
            kr = kf_ref[0, pl.ds(off, FFT_N2), :]
            ki = kf_ref[1, pl.ds(off, FFT_N2), :]
            ys.append(jnp.concatenate([xr * kr - xi * ki, xr * ki + xi * kr], axis=1).astype(BF16))
        rs = [_dot(f2_ref[...], yc) for yc in ys]
        for i, r in zip(idx, rs):
            k1 = s * kg + i
            scr_ref[0, pl.ds(k1, FFT_N2, stride=n1 + FFT_PAD), :] = r[:128, :128] + r[128:, 128:]
            scr_ref[1, pl.ds(k1, FFT_N2, stride=n1 + FFT_PAD), :] = r[:128, 128:] - r[128:, :128]
        return c

    lax.fori_loop(0, kg // FFT_UNROLL, k1_group, 0)

    @pl.when(s == pl.num_programs(2) - 1)
    def _():
        def group(gidx, c):
            n2s = [gidx * FFT_UNROLL + u for u in range(FFT_UNROLL)]
            bcs = []
            for n2 in n2s:
                off = pl.multiple_of(n2 * (n1 + FFT_PAD), 8)
                bcs.append(jnp.concatenate([scr_ref[0, pl.ds(off, n1), :], scr_ref[1, pl.ds(off, n1), :]],
                                           axis=1).astype(BF16))
            rs = [_dot(gi_ref[n2], bc) for n2, bc in zip(n2s, bcs)]
            for n2, r in zip(n2s, rs):
                o_ref[0, 0, pl.ds(n2, nh, stride=FFT_N2), :] = r[:nh, :128] + r[nh:, 128:]
                o_ref[0, 1, pl.ds(n2, nh, stride=FFT_N2), :] = r[:nh, 128:] - r[nh:, :128]
            return c
        lax.fori_loop(0, FFT_N2 // FFT_UNROLL, group, 0)


def _hyena_conv(z4, col0, kf, length):
    gf, gi, f2, n1 = _fft_tables(length)
    n = 2 * length
    kg = FFT_UNROLL
    npair = z4.shape[0]
    big = pl.Buffered(1)
    return pl.pallas_call(
        functools.partial(_hyconv_kernel, n1, kg), grid=(npair, 2, n1 // kg),
        in_specs=[pl.BlockSpec((1, 2, length, 128), lambda p, c, s: (p, 0, 0, col0 + c), pipeline_mode=big),
                  pl.BlockSpec(gf.shape, lambda p, c, s: (0, 0, 0), pipeline_mode=big),
                  pl.BlockSpec(f2.shape, lambda p, c, s: (0, 0)),
                  pl.BlockSpec((2, kg * FFT_N2, 128), lambda p, c, s: (0, s, c)),
                  pl.BlockSpec(gi.shape, lambda p, c, s: (0, 0, 0), pipeline_mode=big)],
        out_specs=pl.BlockSpec((1, 2, length, 128), lambda p, c, s: (p, 0, 0, c)),
        out_shape=jax.ShapeDtypeStruct((npair, 2, length, 256), F32),
        scratch_shapes=[pltpu.VMEM((2, FFT_N2 * (n1 + FFT_PAD), 128), F32)],
        compiler_params=_params(("arbitrary", "arbitrary", "arbitrary")), name="hyena_fftconv",
    )(z4, gf, f2, kf, gi)


def _hyfft_kernel(n1, kg, z_ref, gf_ref, f2_ref, o_ref, scr_ref):
    s = pl.program_id(2)

    @pl.when(s == 0)
    def _():
        _stage1(n1, lambda n2: z_ref[pl.ds(n2, n1, stride=FFT_N2), :].astype(BF16), gf_ref, scr_ref, 128)

    def k1_group(gidx, c):
        idx = [gidx * FFT_UNROLL + u for u in range(FFT_UNROLL)]
        xs = [_stage2(n1, s * kg + i, f2_ref, scr_ref) for i in idx]
        for i, (xr, xi) in zip(idx, xs):
            off = pl.multiple_of(i * FFT_N2, FFT_N2)
            o_ref[0, 0, pl.ds(off, FFT_N2), :] = xr
            o_ref[0, 1, pl.ds(off, FFT_N2), :] = xi
        return c

    lax.fori_loop(0, kg // FFT_UNROLL, k1_group, 0)


@functools.lru_cache(maxsize=None)
def _fft_table_full(length):
    n = 2 * length
    n1 = n // FFT_N2
    k1 = np.arange(n1)[None, :, None].astype(np.float64)
    j = np.arange(n1)[None, None, :].astype(np.float64)
    n2 = np.arange(FFT_N2)[:, None, None].astype(np.float64)
    ang = -2.0 * np.pi * k1 * (FFT_N2 * j + n2) / n
    return np.concatenate([np.cos(ang), np.sin(ang)], axis=1).astype(np.float32).astype(BF16)


def _filter_spectrum(k2, length):
    _, _, f2, n1 = _fft_tables(length)
    gf = _fft_table_full(length)
    n = 2 * length
    kg = min(n1, 16)
    nf = k2.shape[1] // D_HY
    big = pl.Buffered(1)
    return pl.pallas_call(
        functools.partial(_hyfft_kernel, n1, kg), grid=(nf, 2, n1 // kg),
        in_specs=[pl.BlockSpec((n, 128), lambda f, c, s: (0, 2 * f + c), pipeline_mode=big),
                  pl.BlockSpec(gf.shape, lambda f, c, s: (0, 0, 0), pipeline_mode=big),
                  pl.BlockSpec(f2.shape, lambda f, c, s: (0, 0))],
        out_specs=pl.BlockSpec((1, 2, kg * FFT_N2, 128), lambda f, c, s: (f, 0, s, c)),
        out_shape=jax.ShapeDtypeStruct((nf, 2, n, 256), F32),
        scratch_shapes=[pltpu.VMEM((2, FFT_N2 * (n1 + FFT_PAD), 128), F32)],
        compiler_params=_params(("arbitrary", "arbitrary", "arbitrary")), name="hyena_filter_fft",
    )(k2, gf, f2)


def _normalised(k2, ssq, extra_scale):
    return k2 * (lax.rsqrt(ssq + EPS) * extra_scale)


def _hy_mid_kernel(y_ref, hv_ref, b_ref, o_ref):
    v = hv_ref[0][:, 0:D_HY]
    x1 = hv_ref[0][:, D_HY:2 * D_HY]
    o_ref[0] = x1 * (y_ref[0] + v * b_ref[...])


def _hy_mid(y1, hv, bias0):
    nb, length, _ = y1.shape
    te = 4 * TT
    return pl.pallas_call(
        _hy_mid_kernel, grid=(nb, length // te),
        in_specs=[pl.BlockSpec((1, te, D_HY), lambda b, t: (b, t, 0)),
                  pl.BlockSpec((1, te, 512), lambda b, t: (b, t, 0)),
                  pl.BlockSpec((1, D_HY), lambda b, t: (0, 0))],
        out_specs=pl.BlockSpec((1, te, D_HY), lambda b, t: (b, t, 0)),
        out_shape=jax.ShapeDtypeStruct(y1.shape, F32),
        compiler_params=_params(("arbitrary", "arbitrary")), name="hyena_mid",
    )(y1, hv, bias0.reshape(1, D_HY))


def _hy_ctx_kernel(lc, hv_ref, zh_ref, k2_ref, b_ref, fc_ref, ff_ref, ci_ref, o_ref):
    nf = 2 * lc
    hv = hv_ref[0]

    def conv(z, o):
        kf = _dot_exact_rhs(ff_ref[...], k2_ref[:, o * D_HY:(o + 1) * D_HY])
        zs = _dot(fc_ref[...], z.astype(BF16))
        zr, zi = zs[:nf], zs[nf:]
        kr, ki = kf[:nf], kf[nf:]
        ys = jnp.concatenate([zr * kr - zi * ki, zr * ki + zi * kr], axis=0).astype(BF16)
        return _dot(ci_ref[...], ys) * (1.0 / nf)

    v = hv[:, 0:D_HY]
    x1 = hv[:, D_HY:2 * D_HY]
    x2 = hv[:, 2 * D_HY:3 * D_HY]
    z1 = x1 * (conv(v, 0) + v * b_ref[0:1])
    p = x2 * (conv(z1, 1) + z1 * b_ref[1:2])
    o_ref[0] = (p * _silu(zh_ref[0].astype(F32))).astype(BF16)


def _dot_exact_rhs(a_bf16, b):
    bh, bm, bl = _split3(b)
    return _dot(a_bf16, bh) + _dot(a_bf16, bm) + _dot(a_bf16, bl)


@functools.lru_cache(maxsize=None)
def _ctx_dft_tables(lc):
    nf = 2 * lc
    k = np.arange(nf)[:, None].astype(np.float64)
    ang = -2.0 * np.pi * k * np.arange(nf)[None, :] / nf
    ff = np.concatenate([np.cos(ang), np.sin(ang)], axis=0)
    fc = ff[:, :lc]
    a2 = 2.0 * np.pi * np.arange(lc)[:, None] * np.arange(nf)[None, :] / nf
    ci = np.concatenate([np.cos(a2), -np.sin(a2)], axis=1)
    cv = lambda m: np.asarray(m, np.float32).astype(BF16)
    return cv(fc), cv(ff), cv(ci)


def _hy_ctx(hv, pf, k2c, hy_bias, nt, lc):
    nb = hv.shape[0]
    fc, ff, ci = _ctx_dft_tables(lc)
    c2 = lambda b: (0, 0)
    return pl.pallas_call(
        functools.partial(_hy_ctx_kernel, lc), grid=(nb,),
        in_specs=[pl.BlockSpec((1, lc, 768), lambda b: (b, nt, 0)),
                  pl.BlockSpec((1, lc, D_HY), lambda b: (b, nt, PF_ZH)),
                  pl.BlockSpec(k2c.shape, c2),
                  pl.BlockSpec((2, D_HY), c2),
                  pl.BlockSpec(fc.shape, c2), pl.BlockSpec(ff.shape, c2), pl.BlockSpec(ci.shape, c2)],
        out_specs=pl.BlockSpec((1, lc, D_HY), lambda b: (b, 0, 0)),
        out_shape=jax.ShapeDtypeStruct((nb, lc, D_HY), BF16),
        compiler_params=_params(("arbitrary",)), name="hyena_ctx",
    )(hv, pf, k2c, hy_bias, fc, ff, ci)


def _out_kernel(n_hy, na_ref, h0_ref, h1_ref, om_ref, zm_ref, w_ref, x_ref, mod_ref, g_ref, *rest):
    hy_refs, o_ref = rest[:n_hy], rest[n_hy]
    chunks = [slice(r, r + TT) for r in range(0, o_ref.shape[1], TT)]
    acts = []
    for rows in chunks:
        if n_hy == 1:
            hy = hy_refs[0][0, rows]
        else:
            y_ref, z_ref, x2_ref, zh_ref, b_ref = hy_refs
            z = z_ref[0, rows]
            hy = (x2_ref[0, rows] * (y_ref[0, rows] + z * b_ref[...])
                  * _silu(zh_ref[0, rows].astype(F32))).astype(BF16)
        hsum = h0_ref[0, 0, rows].astype(F32) + h1_ref[0, 0, rows].astype(F32)
        ml = (jax.nn.sigmoid(om_ref[0, rows].astype(F32)) * hsum * _silu(zm_ref[0, rows].astype(F32))).astype(BF16)
        acts.append((ml, hy))
    ys = [_dot(na_ref[0, rows], w_ref[0:D_NA]) + _dot(ml, w_ref[D_NA:D_NA + D_ML])
          + _dot(hy, w_ref[D_NA + D_ML:D_MODEL]) for rows, (ml, hy) in zip(chunks, acts)]
    for rows, y in zip(chunks, ys):
        ms = jnp.mean(y * y, axis=-1, keepdims=True)
        r = y * lax.rsqrt(ms + EPS) * g_ref[...]
        o_ref[0, rows] = x_ref[0, rows] + mod_ref[0][2:3] * r


def _out_projection(na_g, hml, pf, hy_args, w_out, res, res_t0, mod3, g_post, to, t0, ntiles, mod_row, out_rows,
                    prev=None):
    nb = na_g.shape[0]
    d = D_MODEL
    tok = lambda col: (lambda b, t: (b, t + t0, col))
    own = lambda col: (lambda b, t: (b, t, col))
    in_specs = [pl.BlockSpec((1, to, 512), tok(0)),
                pl.BlockSpec((1, 1, to, 256), lambda b, t: (b, 0, t + t0, 0)),
                pl.BlockSpec((1, 1, to, 256), lambda b, t: (b, 1, t + t0, 0)),
                pl.BlockSpec((1, to, 256), tok(PF_OM)), pl.BlockSpec((1, to, 256), tok(PF_ZM)),
                pl.BlockSpec((d, d), lambda b, t: (0, 0), pipeline_mode=pl.Buffered(1)),
                pl.BlockSpec((1, to, d), lambda b, t: (b, t + res_t0, 0)),
                pl.BlockSpec((1, 3, d), lambda b, t: (mod_row(b), 0, 0)),
                pl.BlockSpec((1, d), lambda b, t: (0, 0))]
    args = [na_g, hml, hml, pf, pf, w_out, res, mod3, g_post.reshape(1, d)]
    if len(hy_args) == 1:
        in_specs.append(pl.BlockSpec((1, to, D_HY), own(0)))
        args.append(hy_args[0])
    else:
        y2, z1, hv, pf_, bias1 = hy_args
        in_specs += [pl.BlockSpec((1, to, D_HY), own(0)), pl.BlockSpec((1, to, D_HY), own(0)),
                     pl.BlockSpec((1, to, D_HY), tok(2)), pl.BlockSpec((1, to, D_HY), tok(PF_ZH)),
                     pl.BlockSpec((1, D_HY), lambda b, t: (0, 0))]
        args += [y2, z1, hv, pf_, bias1.reshape(1, D_HY)]
    n_hy = len(hy_args)
    aliases = {}
    if prev is not None:
        in_specs.append(pl.BlockSpec(memory_space=pl.ANY))
        args.append(prev)
        aliases = {len(args) - 1: 0}
        kern = lambda *r: _out_kernel(n_hy, *r[:9 + n_hy], r[10 + n_hy])
    else:
        kern = functools.partial(_out_kernel, n_hy)
    return pl.pallas_call(
        kern, grid=(nb, ntiles), in_specs=in_specs,
        out_specs=pl.BlockSpec((1, to, d), tok(0)),
        out_shape=jax.ShapeDtypeStruct((nb, out_rows, d), F32),
        input_output_aliases=aliases,
        compiler_params=_params(("arbitrary", "arbitrary")), name="out_proj_norm_residual",
    )(*args)


@functools.lru_cache(maxsize=None)
def _rope_tables(length, lc):
    n = HEAD_DIM // 4
    t = np.arange(length)
    inv = (ROPE_BASE ** (-np.arange(n, dtype=np.float32) / n)).astype(np.float32)
    pos = np.stack([t // GRID_W, t % GRID_W], axis=-1).astype(np.float32)
    ang = (pos[:, :, None] * inv).astype(np.float32)
    cos_h = np.concatenate([np.cos(ang[:, 0]), np.cos(ang[:, 0]), np.cos(ang[:, 1]), np.cos(ang[:, 1])], axis=-1)
    sin_h = np.concatenate([-np.sin(ang[:, 0]), np.sin(ang[:, 0]), -np.sin(ang[:, 1]), np.sin(ang[:, 1])], axis=-1)
    cos_t = np.concatenate([np.tile(cos_h, (1, 2 * H_ML)), np.ones((lc, 512), np.float32)], axis=0)
    sin_t = np.concatenate([np.tile(sin_h, (1, 2 * H_ML)), np.zeros((lc, 512), np.float32)], axis=0)
    kscale = np.concatenate([np.ones(256, np.float32), np.full(256, HEAD_DIM ** -0.5, np.float32)])
    return (cos_t * kscale).astype(np.float32), (sin_t * kscale).astype(np.float32)


@functools.lru_cache(maxsize=None)
def _scan_constants():
    tril = np.tril(np.ones((TT, TT), np.float32))
    emat = np.zeros((128, D_ML), np.float32)
    for h in range(H_ML):
        emat[h, h * HEAD_DIM:(h + 1) * HEAD_DIM] = 1.0
    rh = np.arange(D_ML)[:, None] // HEAD_DIM
    chd = np.arange(D_ML)[None, :] // HEAD_DIM
    bmask = (rh == chd).astype(np.float32)
    tri = np.concatenate([tril, tril.T], axis=0)
    return tri.astype(BF16), emat.astype(BF16), bmask


def _layer(x_lat, x_ctx, ctx_tile, mod3, nb, length, lc, g_pre, g_post, w_in, b_if, conv_ml, conv_hy, rpb,
           hf_w1, hf_b1, hf_w2, hf_b2, hf_w3, hf_freq, hy_bias, w_out, update_ctx):
    nt = length // TT
    ltot = length + lc
    tri, emat, bmask = _scan_constants()
    cos_t, sin_t = _rope_tables(length, lc)

    c = lambda a, b: w_in[:, a:b]
    wna = jnp.concatenate([c(0, 512) * (HEAD_DIM ** -0.5), c(512, 1536)], axis=1).astype(BF16)
    wpr = jnp.concatenate([c(1536, 2048), c(2560, 2816), c(2816, 3072), c(3072, 3328), c(4112, 4368)],
                          axis=1).astype(BF16)
    wlo = jnp.concatenate([c(2048, 2560), c(3344, 4112), c(3328, 3344), jnp.zeros((D_MODEL, 112), F32)],
                          axis=1).astype(BF16)
    bif_row = jnp.zeros((1, 128), F32).at[0, :4 * H_ML].set(b_if.reshape(-1))
    ona, pf, qk, hv, gates = _projection(x_lat, x_ctx, ctx_tile, mod3, g_pre, wna, wpr, wlo, conv_ml, conv_hy,
                                         bif_row, cos_t, sin_t, tri, nt)

    na_g = _attention(ona, pf, _na_bias_table(rpb), nt, update_ctx)

    hml = _mlstm_scan(qk, pf, gates, emat, bmask, nt)

    k2, ssq = _hyena_filters(length, hf_w1, hf_b1, hf_w2, hf_b2, hf_w3, hf_freq)
    kf = _filter_spectrum(_normalised(k2, ssq, 1.0 / (2 * length)), length)
    hv4 = hv.reshape(nb // 2, 2, ltot, 768)
    y1 = _hyena_conv(hv4, 0, kf[0], length).reshape(nb, length, D_HY)
    z1 = _hy_mid(y1, hv, hy_bias[0])
    y2 = _hyena_conv(z1.reshape(nb // 2, 2, length, D_HY), 0, kf[1], length).reshape(nb, length, D_HY)
    hy_args = (y2, z1, hv, pf, hy_bias[1])
    to = 2 * TT
    if not update_ctx:
        return _out_projection(na_g, hml, pf, hy_args, w_out, x_lat, 0, mod3, g_post, to, 0, length // to,
                               lambda b: b, length)

    xn = _out_projection(na_g, hml, pf, hy_args, w_out, x_lat, 0, mod3, g_post, to, 0, length // to,
                         lambda b: b, ltot)
    k2c, ssq_c = _hyena_filters(lc, hf_w1, hf_b1, hf_w2, hf_b2, hf_w3, hf_freq)
    hyc_g = _hy_ctx(hv, pf, _normalised(k2c, ssq_c, 1.0), hy_bias, nt, lc)
    return _out_projection(na_g, hml, pf, (hyc_g,), w_out, x_ctx, ctx_tile, mod3, g_post, TT, nt, 1,
                           lambda b: nb, ltot, prev=xn)


def kernel(x, c, ctx, c_ctx, w_ada, b_ada, g_pre, g_post, w_in, b_if, conv_ml, conv_hy, rpb, hf_w1, hf_b1,
           hf_w2, hf_b2, hf_w3, hf_freq, hy_bias, w_out):
    nb, length, d = x.shape
    lc = ctx.shape[1]
    depth = w_in.shape[0]
    assert lc == TT and length % (4 * TT) == 0 and nb % 2 == 0 and nb % SCAN_NB == 0 and d == D_MODEL
    cc = jnp.zeros((16, d), F32).at[:nb].set(c).at[nb].set(c_ctx)
    x_lat, x_ctx, ctx_tile = x, ctx, 0
    for l in range(depth):
        mod3 = _modulation(cc, w_ada[l], b_ada[l]).reshape(16, 3, d)
        x_lat = _layer(x_lat, x_ctx, ctx_tile, mod3, nb, length, lc, g_pre[l], g_post[l], w_in[l], b_if[l],
                       conv_ml[l], conv_hy[l], rpb[l], hf_w1[l], hf_b1[l], hf_w2[l], hf_b2[l], hf_w3[l],
                       hf_freq[l], hy_bias[l], w_out[l].astype(BF16), l < depth - 1)
        x_ctx, ctx_tile = x_lat, length // TT
    return x_lat
```

```python
import functools
import math

import numpy as np
import jax
import jax.numpy as jnp
from jax import lax
from jax.experimental import pallas as pl
from jax.experimental.pallas import tpu as pltpu

F32 = jnp.float32
BF16 = jnp.bfloat16

D_MODEL = 1024
GRID_W = 64
HEAD_DIM = 64
D_NA = 512
D_ML = 256
D_HY = 256
H_NA = 8
H_ML = 4
NA_ROWS = 8
NA_COLS = 16
HY_EMB = 33
HY_FFN = 64
HY_TARGET = 1e-2
HY_FAST = 0.3
HY_SLOW = 1.5
ROPE_BASE = 10000.0
EPS = 1e-6

TT = 256
BAND_ROWS = TT // GRID_W
SCAN_NB = 4
PRE_ROWS = 64
FFT_N2 = 128
FFT_PAD = 8
FFT_UNROLL = 8
NEG = -1e30
VMEM_LIMIT = 56 * 1024 * 1024
FFTCONV_VMEM_LIMIT = 59 * 1024 * 1024

PF_ZA = 0
PF_VM, PF_OM, PF_ZM, PF_ZH = 2, 3, 4, 5
LOC_QK, LOC_HY, LOC_GM, LOC_COLS = 0, 512, 1280, 1408
HALO = 8


def _dot(a, b):
    return jnp.dot(a, b, preferred_element_type=F32)


def _dot_nt(a, b):
    return lax.dot_general(a, b, (((1,), (1,)), ((), ())), preferred_element_type=F32)


def _dot_tn(a, b):
    return lax.dot_general(a, b, (((0,), (0,)), ((), ())), preferred_element_type=F32)


def _split3(x):
    h = x.astype(BF16)
    r = x - h.astype(F32)
    m = r.astype(BF16)
    l = (r - m.astype(F32)).astype(BF16)
    return h, m, l


def _dot_f32(a, b):
    ah, am, al = _split3(a)
    bh, bm, bl = _split3(b)
    return (_dot(ah, bh) + (_dot(ah, bm) + _dot(am, bh))
            + (_dot(ah, bl) + _dot(al, bh) + _dot(am, bm)))


def _dot_2x(a, b):
    ah, am, _ = _split3(a)
    bh, bm, _ = _split3(b)
    return _dot(ah, bh) + (_dot(ah, bm) + _dot(am, bh))


def _silu(x):
    return x * jax.nn.sigmoid(x)


def _params(sem, vmem=VMEM_LIMIT):
    return pltpu.CompilerParams(dimension_semantics=sem, vmem_limit_bytes=vmem)


def _mod_kernel(c_ref, w_ref, b_ref, o_ref):
    o_ref[...] = _dot_f32(_silu(c_ref[...]), w_ref[...]) + b_ref[...]


def _modulation(cc, w_ada, b_ada):
    n = w_ada.shape[1]
    bn = 512
    return pl.pallas_call(
        _mod_kernel, grid=(n // bn,),
        in_specs=[pl.BlockSpec(cc.shape, lambda j: (0, 0)),
                  pl.BlockSpec((w_ada.shape[0], bn), lambda j: (0, j)),
                  pl.BlockSpec((1, bn), lambda j: (0, j))],
        out_specs=pl.BlockSpec((cc.shape[0], bn), lambda j: (0, j)),
        out_shape=jax.ShapeDtypeStruct((cc.shape[0], n), F32),
        compiler_params=_params(("arbitrary",)), name="adaln_mod",
    )(cc, w_ada, b_ada.reshape(1, n))


def _proj_kernel(nt, x_ref, c_ref, xp_ref, xn_ref, mod_ref, g_ref, wna_ref, wpr_ref, wlo_ref, cml_ref, chy_ref,
                 bif_ref, cos_ref, sin_ref, tri_ref, ona_ref, opr_ref, oqk_ref, ohv_ref, og_ref, loc_ref):
    t = pl.program_id(0)
    x = jnp.where(t == nt, c_ref[0], x_ref[0])
    xs = jnp.concatenate([xp_ref[0], x, xn_ref[0]], axis=0)
    ms = jnp.mean(xs * xs, axis=-1, keepdims=True)
    y = xs * lax.rsqrt(ms + EPS) * g_ref[...]
    mod = mod_ref[0]
    h = (y * (1.0 + mod[1:2]) + mod[0:1]).astype(BF16)
    hm = h[HALO:HALO + TT]
    loc_ref[...] = _dot(h, wlo_ref[...])
    ona_ref[0] = _dot(hm, wna_ref[...]).astype(BF16)

    pm = jnp.where(jnp.logical_and(t != 0, t != nt), 1.0, 0.0)
    nm = jnp.where(jnp.logical_and(t != nt - 1, t != nt), 1.0, 0.0)
    row8 = lax.broadcasted_iota(jnp.int32, (8, 1), 0)
    nchunk = TT // PRE_ROWS

    def conv3(cols, w, c):
        r0 = HALO + c * PRE_ROWS
        u = loc_ref[r0:r0 + PRE_ROWS, cols]
        up = loc_ref[r0 - 1:r0, cols]
        un = loc_ref[r0 + PRE_ROWS:r0 + PRE_ROWS + 1, cols]
        if c == 0:
            up = up * pm
        if c == nchunk - 1:
            un = un * nm
        rp = pltpu.roll(u, 1, 0)
        rn = pltpu.roll(u, PRE_ROWS - 1, 0)
        prev = jnp.concatenate([jnp.where(row8 == 0, up, rp[0:8]), rp[8:]], axis=0)
        nxt = jnp.concatenate([rn[:PRE_ROWS - 8], jnp.where(row8 == 7, un, rn[PRE_ROWS - 8:])], axis=0)
        return prev * w[0:1] + u * w[1:2] + nxt * w[2:3]

    lane = lax.broadcasted_iota(jnp.int32, (1, 2 * D_ML), 1)
    first = (lane % 32) < 16
    for c in range(nchunk):
        rows = slice(c * PRE_ROWS, (c + 1) * PRE_ROWS)
        ohv_ref[0, rows] = conv3(slice(LOC_HY, LOC_GM), chy_ref[...], c)
        x = _silu(conv3(slice(LOC_QK, LOC_HY), cml_ref[...], c))
        partner = jnp.where(first, pltpu.roll(x, 2 * D_ML - 16, 1), pltpu.roll(x, 16, 1))
        oqk_ref[0, rows] = (x * cos_ref[rows] + partner * sin_ref[rows]).astype(BF16)

    g = loc_ref[HALO:HALO + TT, LOC_GM:LOC_COLS] + bif_ref[...]
    gl = lax.broadcasted_iota(jnp.int32, (1, 128), 1)
    is_f = jnp.logical_and((gl % 8) >= 4, gl < 16)
    lf = jnp.where(is_f, jnp.minimum(g, 0.0) - jnp.log1p(jnp.exp(-jnp.abs(g))), 0.0)
    hi, mid, lo = _split3(lf)
    packed = (hi.astype(F32) + pltpu.roll(mid.astype(F32), 16, 1) + pltpu.roll(lo.astype(F32), 32, 1)).astype(BF16)
    cs = _dot(tri_ref[...], packed)
    cs = cs + pltpu.roll(cs, 112, 1) + pltpu.roll(cs, 96, 1)
    bcum = jnp.where(gl < 8, cs[0:TT], cs[TT:2 * TT])
    a = g - pltpu.roll(bcum, 128 - H_ML, 1)
    ab = jnp.where(is_f, bcum, a)
    keep = gl < 2 * H_ML
    og_ref[0, 0] = jnp.where(keep, ab, 0.0)
    og_ref[0, 1] = jnp.where(keep, pltpu.roll(ab, 120, 1), 0.0)
    opr_ref[0] = _dot(hm, wpr_ref[...]).astype(BF16)


def _projection(x_lat, x_ctx, ctx_tile, mod3, g_pre, wna, wpr, wlo, conv_ml, conv_hy, bif_row, cos_t, sin_t,
                tri, nt):
    nb, lat_rows, d = x_lat.shape
    ltot = (nt + 1) * TT
    ctx_row = nb
    last8 = lat_rows // HALO - 1
    r8 = TT // HALO

    def mod_idx(t, b):
        return (jnp.where(t == nt, ctx_row, b), 0, 0)

    const = lambda t, b: (0, 0)
    big = pl.Buffered(1)
    tok = lambda t, b: (b, t, 0)
    return pl.pallas_call(
        functools.partial(_proj_kernel, nt), grid=(nt + 1, nb),
        in_specs=[pl.BlockSpec((1, TT, d), lambda t, b: (b, jnp.minimum(t, nt - 1), 0)),
                  pl.BlockSpec((1, TT, d), lambda t, b: (b, ctx_tile, 0)),
                  pl.BlockSpec((1, HALO, d), lambda t, b: (b, jnp.clip(t * r8 - 1, 0, last8), 0)),
                  pl.BlockSpec((1, HALO, d), lambda t, b: (b, jnp.minimum((t + 1) * r8, last8), 0)),
                  pl.BlockSpec((1, 3, d), mod_idx),
                  pl.BlockSpec((1, d), const),
                  pl.BlockSpec(wna.shape, const, pipeline_mode=big),
                  pl.BlockSpec(wpr.shape, const, pipeline_mode=big),
                  pl.BlockSpec(wlo.shape, const, pipeline_mode=big),
                  pl.BlockSpec((3, 512), const), pl.BlockSpec((3, 768), const), pl.BlockSpec((1, 128), const),
                  pl.BlockSpec((TT, 512), lambda t, b: (t, 0)), pl.BlockSpec((TT, 512), lambda t, b: (t, 0)),
                  pl.BlockSpec((2 * TT, TT), const)],
        out_specs=[pl.BlockSpec((1, TT, wna.shape[1]), tok), pl.BlockSpec((1, TT, wpr.shape[1]), tok),
                   pl.BlockSpec((1, TT, 512), tok), pl.BlockSpec((1, TT, 768), tok),
                   pl.BlockSpec((1, 2, TT, 128), lambda t, b: (b, 0, t, 0))],
        out_shape=[jax.ShapeDtypeStruct((nb, ltot, wna.shape[1]), BF16),
                   jax.ShapeDtypeStruct((nb, ltot, wpr.shape[1]), BF16),
                   jax.ShapeDtypeStruct((nb, ltot, 512), BF16),
                   jax.ShapeDtypeStruct((nb, ltot, 768), F32),
                   jax.ShapeDtypeStruct((nb, 2, ltot, 128), F32)],
        scratch_shapes=[pltpu.VMEM((TT + 2 * HALO, LOC_COLS), F32)],
        compiler_params=_params(("arbitrary", "arbitrary")), name="norm_in_proj_conv",
    )(x_lat, x_ctx, x_lat, x_lat, mod3, g_pre.reshape(1, d), wna, wpr, wlo, conv_ml, conv_hy, bif_row,
      cos_t, sin_t, tri)


def _na_kernel(q_ref, k0_ref, k1_ref, k2_ref, v0_ref, v1_ref, v2_ref, kc_ref, vc_ref, za_ref, tab_ref, o_ref):
    lane = lax.broadcasted_iota(jnp.int32, (1, 128), 1)
    k_refs = (k0_ref, k1_ref, k2_ref)
    v_refs = (v0_ref, v1_ref, v2_ref)

    def row_max(s):
        return jnp.max(jnp.maximum(s[:, :128], s[:, 128:]), axis=-1, keepdims=True)

    def head_mask(h):
        return (lane >= HEAD_DIM) if h % 2 else (lane < HEAD_DIM)

    def masked_q(h):
        sl = slice((h // 2) * 128, (h // 2 + 1) * 128)
        q2 = q_ref[0, :, sl]
        return jnp.where(head_mask(h), q2, jnp.zeros_like(q2))

    def score_block(h, qm, blk):
        sl = slice((h // 2) * 128, (h // 2 + 1) * 128)
        if blk == 0:
            return _dot_nt(qm, kc_ref[0, :, sl])
        return _dot_nt(qm, k_refs[blk - 1][0, :, sl]) + tab_ref[0, h, :, (blk - 1) * TT:blk * TT]

    def values(h, p):
        sl = slice((h // 2) * 128, (h // 2 + 1) * 128)
        hm = head_mask(h)
        v = jnp.concatenate([vc_ref[0, :, sl]] + [r[0, :, sl] for r in v_refs], axis=0)
        acc = _dot(p, jnp.where(hm, v, jnp.ones_like(v)))
        return jnp.where(hm, acc / pltpu.roll(acc, HEAD_DIM, 1), 0.0)

    outs = {}
    qm = masked_q(0)
    sc = [score_block(0, qm, blk) for blk in range(4)]
    for h in range(H_NA):
        m = row_max(jnp.maximum(jnp.maximum(sc[0], sc[1]), jnp.maximum(sc[2], sc[3])))
        qm = masked_q(h + 1) if h + 1 < H_NA else None
        nxt, ps = [], []
        for blk in range(4):
            if qm is not None:
                nxt.append(score_block(h + 1, qm, blk))
            ps.append(jnp.exp(sc[blk] - m).astype(BF16))
        sc = nxt
        outs[h] = values(h, jnp.concatenate(ps, axis=1))
        if h % 2 == 1:
            sl = slice((h // 2) * 128, (h // 2 + 1) * 128)
            o_ref[0, :, sl] = ((outs[h - 1] + outs[h]) * _silu(za_ref[0, :, sl].astype(F32))).astype(BF16)


def _attention(ona, pf, table, nt, with_ctx_queries):
    nb, ltot, _ = ona.shape
    nj = nt + 1 if with_ctx_queries else nt

    def kv(col, dj):
        return lambda j, b: (b, jnp.clip(j + dj, 0, nt - 1), col)

    def pat(j, b):
        p = jnp.where(j == 0, 0, jnp.where(j == nt - 1, 2, jnp.where(j == nt, 3, 1)))
        return (p, 0, 0, 0)

    return pl.pallas_call(
        _na_kernel, grid=(nj, nb),
        in_specs=[pl.BlockSpec((1, TT, 512), lambda j, b: (b, j, 0)),
                  pl.BlockSpec((1, TT, 512), kv(1, -1)), pl.BlockSpec((1, TT, 512), kv(1, 0)),
                  pl.BlockSpec((1, TT, 512), kv(1, 1)),
                  pl.BlockSpec((1, TT, 512), kv(2, -1)), pl.BlockSpec((1, TT, 512), kv(2, 0)),
                  pl.BlockSpec((1, TT, 512), kv(2, 1)),
                  pl.BlockSpec((1, TT, 512), lambda j, b: (b, nt, 1)),
                  pl.BlockSpec((1, TT, 512), lambda j, b: (b, nt, 2)),
                  pl.BlockSpec((1, TT, 512), lambda j, b: (b, j, PF_ZA)),
                  pl.BlockSpec((1, H_NA, TT, 3 * TT), pat)],
        out_specs=pl.BlockSpec((1, TT, 512), lambda j, b: (b, j, 0)),
        out_shape=jax.ShapeDtypeStruct((nb, ltot, 512), BF16),
        compiler_params=_params(("arbitrary", "arbitrary")), name="nbr_attention",
    )(ona, ona, ona, ona, ona, ona, ona, ona, ona, pf, table)


def _na_bias_table(rpb):
    a = np.arange(TT) // GRID_W
    qc = np.arange(TT) % GRID_W
    kidx = np.arange(3 * TT)
    kr = (kidx // TT - 1) * BAND_ROWS + (kidx % TT) // GRID_W
    kcol = kidx % GRID_W
    qstart = np.clip(qc - NA_COLS // 2, 0, GRID_W - NA_COLS)
    col_ok = (kcol[None, :] >= qstart[:, None]) & (kcol[None, :] < qstart[:, None] + NA_COLS)
    coff = np.clip(kcol[None, :] - qc[:, None], -(NA_COLS - 1), NA_COLS - 1) + NA_COLS - 1
    roff = np.clip(kr[None, :] - a[:, None] + NA_ROWS - 1, 0, 2 * NA_ROWS - 2)
    half = NA_ROWS // 2
    row_ok = [
        (kr[None, :] >= 0) & (kr[None, :] < NA_ROWS) & (a[:, None] >= 0),
        (kr[None, :] >= a[:, None] - half) & (kr[None, :] < a[:, None] + half),
        (kr[None, :] >= BAND_ROWS - NA_ROWS) & (kr[None, :] < BAND_ROWS) & (a[:, None] >= 0),
    ]
    nr, ncol = 2 * NA_ROWS - 1, 2 * NA_COLS - 1
    onehot = (coff[:GRID_W, :GRID_W].reshape(1, -1) == np.arange(ncol)[:, None]).astype(np.float32)
    cmat = jnp.dot(rpb.astype(F32).reshape(H_NA * nr, ncol), onehot,
                   precision=lax.Precision.HIGHEST).reshape(H_NA, nr, GRID_W, GRID_W)
    nkr = 3 * BAND_ROWS
    blocks = [jnp.concatenate([cmat[:, roff[ai * GRID_W, kj * GRID_W]] for kj in range(nkr)], axis=-1)
              for ai in range(BAND_ROWS)]
    bias = jnp.concatenate(blocks, axis=1)
    tabs = [jnp.where((r & col_ok)[None], bias, NEG) for r in row_ok]
    tabs.append(jnp.full_like(bias, NEG))
    return jnp.stack(tabs, axis=0)


def _head_lane_sums(x, lane_h):
    out = jnp.zeros_like(x)
    half = lax.broadcasted_iota(jnp.int32, (1, 128), 1) // HEAD_DIM
    for h in range(H_ML):
        t = x[:, (h // 2) * 128:(h // 2 + 1) * 128]
        sm = jnp.sum(jnp.where(half == h % 2, t, 0.0), axis=-1, keepdims=True)
        out = jnp.where(lane_h == h, sm, out)
    return out


def _scan_kernel(q_ref, k_ref, v_ref, g_ref, e_ref, bm_ref, o_ref, w_ref, n_ref, m_ref):
    s = pl.program_id(2)
    sgn = 1 - 2 * pl.program_id(1)

    @pl.when(s == 0)
    def _():
        w_ref[...] = jnp.zeros_like(w_ref)
        n_ref[...] = jnp.zeros_like(n_ref)
        m_ref[...] = jnp.zeros_like(m_ref)

    lane_h = lax.broadcasted_iota(jnp.int32, (1, D_ML), 1) // HEAD_DIM
    ri = lax.broadcasted_iota(jnp.int32, (TT, TT), 0)
    ci = lax.broadcasted_iota(jnp.int32, (TT, TT), 1)
    tri = ((ci - ri) * sgn) <= 0
    gate_lanes = lax.broadcasted_iota(jnp.int32, (1, 128), 1) < H_ML

    seqs = range(SCAN_NB)
    gate, kws, kwfs, vbs = [], [], [], []
    for i in seqs:
        gi = g_ref[i, 0]
        a = jnp.where(gate_lanes, gi, 0.0)
        bb = jnp.where(gate_lanes, pltpu.roll(gi, 128 - H_ML, 1), 0.0)
        m_prev = m_ref[i]
        mc = jnp.maximum(m_prev, jnp.max(a, axis=0, keepdims=True))
        small = jnp.concatenate([jnp.exp(a - mc), jnp.exp(-bb - mc),
                                 jnp.broadcast_to(jnp.exp(m_prev - mc), (8, 128))], axis=0)
        sh = small.astype(BF16)
        sl = (small - sh.astype(F32)).astype(BF16)
        ex = _dot(sh, e_ref[...]) + _dot(sl, e_ref[...])
        gate.append((ex[TT:2 * TT], ex[2 * TT:2 * TT + 1], jnp.min(bb, axis=0, keepdims=True) + mc))
        kwf = k_ref[i].astype(F32) * ex[0:TT]
        kwfs.append(kwf)
        kws.append(kwf.astype(BF16))
        vbs.append(v_ref[i].astype(BF16))

    scores = [[_dot_nt(jnp.where(lane_h == h, q_ref[i], jnp.zeros_like(q_ref[i])), kws[i]) for h in range(H_ML)]
              for i in seqs]
    inters = [_dot(q_ref[i], w_ref[i].astype(BF16)) for i in seqs]
    updates = [_dot_tn(kws[i], vbs[i]) for i in seqs]

    for i in seqs:
        eb_e, beta, m_new = gate[i]
        intra = None
        den_intra = jnp.zeros((TT, D_ML), F32)
        for h in range(H_ML):
            sc = jnp.where(tri, scores[i][h], 0.0)
            den_intra = jnp.where(lane_h == h, jnp.sum(sc, axis=-1, keepdims=True), den_intra)
            vh = jnp.where(lane_h == h, vbs[i], jnp.zeros_like(vbs[i]))
            part = _dot(sc.astype(BF16), vh)
            intra = part if intra is None else intra + part
        num = beta * inters[i] + intra
        den = beta * _head_lane_sums(q_ref[i].astype(F32) * n_ref[i], lane_h) + den_intra
        o_ref[i, 0] = (num / jnp.maximum(jnp.abs(den), eb_e)).astype(BF16)

        w_ref[i] = beta * w_ref[i] + bm_ref[...] * updates[i]
        n_ref[i] = beta * n_ref[i] + jnp.sum(kwfs[i], axis=0, keepdims=True)
        m_ref[i] = m_new


def _mlstm_scan(qk, pf, gates, emat, bmask, nt):
    nb, ltot, _ = qk.shape
    g = SCAN_NB

    def chunk(s, d):
        return jnp.where(s == 0, nt, jnp.where(d == 0, s - 1, nt - s))

    const = lambda b, d, s: (0, 0)
    return pl.pallas_call(
        _scan_kernel, grid=(nb // g, 2, nt + 1),
        in_specs=[pl.BlockSpec((g, TT, 256), lambda b, d, s: (b, chunk(s, d), 0)),
                  pl.BlockSpec((g, TT, 256), lambda b, d, s: (b, chunk(s, d), 1)),
                  pl.BlockSpec((g, TT, 256), lambda b, d, s: (b, chunk(s, d), PF_VM)),
                  pl.BlockSpec((g, 1, TT, 128), lambda b, d, s: (b, d, chunk(s, d), 0)),
                  pl.BlockSpec((128, 256), const), pl.BlockSpec((256, 256), const)],
        out_specs=pl.BlockSpec((g, 1, TT, 256), lambda b, d, s: (b, d, chunk(s, d), 0)),
        out_shape=jax.ShapeDtypeStruct((nb, 2, ltot, 256), BF16),
        scratch_shapes=[pltpu.VMEM((g, D_ML, D_ML), F32), pltpu.VMEM((g, 1, D_ML), F32),
                        pltpu.VMEM((g, 1, 128), F32)],
        compiler_params=_params(("arbitrary", "arbitrary", "arbitrary")), name="mlstm_scan",
    )(qk, qk, pf, gates, emat, bmask)


def _hyfilt_kernel(tl, length, feat_ref, w1_ref, b1_ref, w2_ref, b2_ref, w3_ref, fq_ref, dec_ref, o_ref, ss_ref):
    i = pl.program_id(0)
    a = jnp.sin(fq_ref[0:1] * (_dot_2x(feat_ref[...], w1_ref[...]) + b1_ref[...]))
    a = jnp.sin(fq_ref[1:2] * (_dot_2x(a, w2_ref[...]) + b2_ref[...]))
    dec = dec_ref[...]
    f = _dot_2x(a, w3_ref[...]) * jnp.concatenate([dec, dec, dec, dec], axis=1)
    fwd = jnp.concatenate([f[:, 0:D_HY], f[:, 2 * D_HY:3 * D_HY]], axis=1)
    bwd = jnp.concatenate([f[:, D_HY:2 * D_HY], f[:, 3 * D_HY:4 * D_HY]], axis=1)
    row = lax.broadcasted_iota(jnp.int32, (tl, 1), 0) + i * tl
    k2 = jnp.where(row < length, fwd, jnp.where(row > length, bwd, 0.0))
    o_ref[...] = k2

    @pl.when(i == 0)
    def _():
        ss_ref[...] = jnp.zeros_like(ss_ref)

    ss_ref[...] += jnp.sum(k2 * k2, axis=0, keepdims=True)


def _hyena_filters(length, w1, b1, w2, b2, w3, freq):
    t = np.arange(length, dtype=np.float32)
    tn = (t / np.float32(length - 1)).astype(np.float32)
    bands = (HY_EMB - 1) // 2
    fr = np.linspace(1e-4, bands - 1, bands, dtype=np.float32)
    ang = (np.float32(2.0 * math.pi / length) * t[:, None] * fr[None, :]).astype(np.float32)
    feat = np.zeros((length, 128), np.float32)
    feat[:, 0] = tn
    feat[:, 1:1 + bands] = np.cos(ang)
    feat[:, 1 + bands:1 + 2 * bands] = -np.sin(ang)
    deltas = np.abs(np.linspace(math.log(HY_TARGET) / HY_SLOW, math.log(HY_TARGET) / HY_FAST, D_HY, dtype=np.float32))
    dec = np.exp(-tn[:, None] * deltas[None, :]).astype(np.float32)
    lag = np.concatenate([np.arange(length), [0], np.arange(length - 1, 0, -1)])
    feat, dec = feat[lag], dec[lag]

    pad2 = lambda m, r, c: jnp.zeros((r, c), F32).at[:m.shape[0], :m.shape[1]].set(m.astype(F32))
    w1p = pad2(w1, 128, 128)
    w2p = pad2(w2, 128, 128)
    w3p = pad2(w3, 128, 4 * D_HY)
    b1p = pad2(b1[None], 1, 128)
    b2p = pad2(b2[None], 1, 128)
    fqp = pad2(freq, 2, 128)
    tl = 512
    const = lambda i: (0, 0)
    return pl.pallas_call(
        functools.partial(_hyfilt_kernel, tl, length), grid=(2 * length // tl,),
        in_specs=[pl.BlockSpec((tl, 128), lambda i: (i, 0)),
                  pl.BlockSpec((128, 128), const), pl.BlockSpec((1, 128), const),
                  pl.BlockSpec((128, 128), const), pl.BlockSpec((1, 128), const),
                  pl.BlockSpec((128, 4 * D_HY), const), pl.BlockSpec((2, 128), const),
                  pl.BlockSpec((tl, D_HY), lambda i: (i, 0))],
        out_specs=[pl.BlockSpec((tl, 2 * D_HY), lambda i: (i, 0)), pl.BlockSpec((1, 2 * D_HY), const)],
        out_shape=[jax.ShapeDtypeStruct((2 * length, 2 * D_HY), F32), jax.ShapeDtypeStruct((1, 2 * D_HY), F32)],
        compiler_params=_params(("arbitrary",)), name="hyena_filter_mlp",
    )(feat, w1p, b1p, w2p, b2p, w3p, fqp, dec)


@functools.lru_cache(maxsize=None)
def _fft_tables(length):
    n = 2 * length
    n1 = n // FFT_N2
    nh = n1 // 2
    k1 = np.arange(n1)[None, :, None].astype(np.float64)
    j = np.arange(nh)[None, None, :].astype(np.float64)
    n2 = np.arange(FFT_N2)[:, None, None].astype(np.float64)
    ang = -2.0 * np.pi * k1 * (FFT_N2 * j + n2) / n
    gf = np.concatenate([np.cos(ang), np.sin(ang)], axis=1)
    gi = np.concatenate([np.cos(ang).transpose(0, 2, 1), np.sin(ang).transpose(0, 2, 1)], axis=1)
    kk = np.arange(FFT_N2)[:, None] * np.arange(FFT_N2)[None, :]
    a2 = -2.0 * np.pi * kk / FFT_N2
    f2 = np.concatenate([np.cos(a2), np.sin(a2)], axis=0)
    cv = lambda m: np.asarray(m, np.float32).astype(BF16)
    return cv(gf), cv(gi), cv(f2), n1


def _stage1(n1, z_of, gf_ref, scr_ref, cols):
    def group(gidx, c):
        n2s = [gidx * FFT_UNROLL + u for u in range(FFT_UNROLL)]
        zs = [z_of(n2) for n2 in n2s]
        rs = [_dot(gf_ref[n2], z) for n2, z in zip(n2s, zs)]
        for n2, r in zip(n2s, rs):
            off = pl.multiple_of(n2 * (n1 + FFT_PAD), 8)
            if cols == 256:
                scr_ref[0, pl.ds(off, n1), :] = r[:n1, :128] - r[n1:, 128:]
                scr_ref[1, pl.ds(off, n1), :] = r[n1:, :128] + r[:n1, 128:]
            else:
                scr_ref[0, pl.ds(off, n1), :] = r[:n1]
                scr_ref[1, pl.ds(off, n1), :] = r[n1:]
        return c

    lax.fori_loop(0, FFT_N2 // FFT_UNROLL, group, 0)


def _stage2(n1, k1, f2_ref, scr_ref):
    ar = scr_ref[0, pl.ds(k1, FFT_N2, stride=n1 + FFT_PAD), :]
    ai = scr_ref[1, pl.ds(k1, FFT_N2, stride=n1 + FFT_PAD), :]
    r = _dot(f2_ref[...], jnp.concatenate([ar, ai], axis=1).astype(BF16))
    return r[:128, :128] - r[128:, 128:], r[128:, :128] + r[:128, 128:]


def _hyconv_kernel(n1, kg, z_ref, gf_ref, f2_ref, kf_ref, gi_ref, o_ref, scr_ref, tp_ref):
    s = pl.program_id(2)
    nh = n1 // 2
    pitch = FFT_N2 + FFT_PAD

    @pl.when(s == 0)
    def _():
        for e in range(2):
            for j in range(nh):
                tp_ref[e, j * pitch:j * pitch + FFT_N2, :] = z_ref[0, e, j * FFT_N2:(j + 1) * FFT_N2, :]

        def z_of(n2):
            zr = tp_ref[0, pl.ds(n2, nh, stride=pitch), :]
            zi = tp_ref[1, pl.ds(n2, nh, stride=pitch), :]
            return jnp.concatenate([zr, zi], axis=1).astype(BF16)
        _stage1(n1, z_of, gf_ref, scr_ref, 256)

    def k1_group(gidx, c):
        idx = [gidx * FFT_UNROLL + u for u in range(FFT_UNROLL)]
        xs = [_stage2(n1, s * kg + i, f2_ref, scr_ref) for i in idx]
        ys = []
        for i, (xr, xi) in zip(idx, xs):
            off = pl.multiple_of(i * FFT_N2, FFT_N2)
            kr = kf_ref[0, pl.ds(off, FFT_N2), :]
            ki = kf_ref[1, pl.ds(off, FFT_N2), :]
            ys.append(jnp.concatenate([xr * kr - xi * ki, xr * ki + xi * kr], axis=1).astype(BF16))
        rs = [_dot(f2_ref[...], yc) for yc in ys]
        for i, r in zip(idx, rs):
            k1 = s * kg + i
            scr_ref[0, pl.ds(k1, FFT_N2, stride=n1 + FFT_PAD), :] = r[:128, :128] + r[128:, 128:]
            scr_ref[1, pl.ds(k1, FFT_N2, stride=n1 + FFT_PAD), :] = r[:128, 128:] - r[128:, :128]
        return c

    lax.fori_loop(0, kg // FFT_UNROLL, k1_group, 0)

    @pl.when(s == pl.num_programs(2) - 1)
    def _():
        def group(gidx, c):
            n2s = [gidx * FFT_UNROLL + u for u in range(FFT_UNROLL)]
            bcs = []
            for n2 in n2s:
                off = pl.multiple_of(n2 * (n1 + FFT_PAD), 8)
                bcs.append(jnp.concatenate([scr_ref[0, pl.ds(off, n1), :], scr_ref[1, pl.ds(off, n1), :]],
                                           axis=1).astype(BF16))
            rs = [_dot(gi_ref[n2], bc) for n2, bc in zip(n2s, bcs)]
            for n2, r in zip(n2s, rs):
                tp_ref[0, pl.ds(n2, nh, stride=pitch), :] = r[:nh, :128] + r[nh:, 128:]
                tp_ref[1, pl.ds(n2, nh, stride=pitch), :] = r[:nh, 128:] - r[nh:, :128]
            return c
        lax.fori_loop(0, FFT_N2 // FFT_UNROLL, group, 0)
        for e in range(2):
            for j in range(nh):
                o_ref[0, e, j * FFT_N2:(j + 1) * FFT_N2, :] = tp_ref[e, j * pitch:j * pitch + FFT_N2, :]


def _hyena_conv(z4, col0, kf, length):
    gf, gi, f2, n1 = _fft_tables(length)
    n = 2 * length
    kg = min(n1, 2 * FFT_UNROLL)
    npair = z4.shape[0]
    big = pl.Buffered(1)
    return pl.pallas_call(
        functools.partial(_hyconv_kernel, n1, kg), grid=(npair, 2, n1 // kg),
        in_specs=[pl.BlockSpec((1, 2, length, 128), lambda p, c, s: (p, 0, 0, col0 + c), pipeline_mode=big),
                  pl.BlockSpec(gf.shape, lambda p, c, s: (0, 0, 0), pipeline_mode=big),
                  pl.BlockSpec(f2.shape, lambda p, c, s: (0, 0)),
                  pl.BlockSpec((2, kg * FFT_N2, 128), lambda p, c, s: (0, s, c)),
                  pl.BlockSpec(gi.shape, lambda p, c, s: (0, 0, 0), pipeline_mode=big)],
        out_specs=pl.BlockSpec((1, 2, length, 128), lambda p, c, s: (p, 0, 0, c), pipeline_mode=big),
        out_shape=jax.ShapeDtypeStruct((npair, 2, length, 256), F32),
        scratch_shapes=[pltpu.VMEM((2, FFT_N2 * (n1 + FFT_PAD), 128), F32),
                        pltpu.VMEM((2, (n1 // 2) * (FFT_N2 + FFT_PAD), 128), F32)],
        compiler_params=_params(("arbitrary", "arbitrary", "arbitrary"), FFTCONV_VMEM_LIMIT), name="hyena_fftconv",
    )(z4, gf, f2, kf, gi)


def _hyfft_kernel(n1, kg, z_ref, gf_ref, f2_ref, o_ref, scr_ref):
    s = pl.program_id(2)

    @pl.when(s == 0)
    def _():
        _stage1(n1, lambda n2: z_ref[pl.ds(n2, n1, stride=FFT_N2), :].astype(BF16), gf_ref, scr_ref, 128)

    def k1_group(gidx, c):
        idx = [gidx * FFT_UNROLL + u for u in range(FFT_UNROLL)]
        xs = [_stage2(n1, s * kg + i, f2_ref, scr_ref) for i in idx]
        for i, (xr, xi) in zip(idx, xs):
            off = pl.multiple_of(i * FFT_N2, FFT_N2)
            o_ref[0, 0, pl.ds(off, FFT_N2), :] = xr
            o_ref[0, 1, pl.ds(off, FFT_N2), :] = xi
        return c

    lax.fori_loop(0, kg // FFT_UNROLL, k1_group, 0)


@functools.lru_cache(maxsize=None)
def _fft_table_full(length):
    n = 2 * length
    n1 = n // FFT_N2
    k1 = np.arange(n1)[None, :, None].astype(np.float64)
    j = np.arange(n1)[None, None, :].astype(np.float64)
    n2 = np.arange(FFT_N2)[:, None, None].astype(np.float64)
    ang = -2.0 * np.pi * k1 * (FFT_N2 * j + n2) / n
    return np.concatenate([np.cos(ang), np.sin(ang)], axis=1).astype(np.float32).astype(BF16)


def _filter_spectrum(k2, length):
    _, _, f2, n1 = _fft_tables(length)
    gf = _fft_table_full(length)
    n = 2 * length
    kg = min(n1, 16)
    nf = k2.shape[1] // D_HY
    big = pl.Buffered(1)
    return pl.pallas_call(
        functools.partial(_hyfft_kernel, n1, kg), grid=(nf, 2, n1 // kg),
        in_specs=[pl.BlockSpec((n, 128), lambda f, c, s: (0, 2 * f + c), pipeline_mode=big),
                  pl.BlockSpec(gf.shape, lambda f, c, s: (0, 0, 0), pipeline_mode=big),
                  pl.BlockSpec(f2.shape, lambda f, c, s: (0, 0))],
        out_specs=pl.BlockSpec((1, 2, kg * FFT_N2, 128), lambda f, c, s: (f, 0, s, c)),
        out_shape=jax.ShapeDtypeStruct((nf, 2, n, 256), F32),
        scratch_shapes=[pltpu.VMEM((2, FFT_N2 * (n1 + FFT_PAD), 128), F32)],
        compiler_params=_params(("arbitrary", "arbitrary", "arbitrary")), name="hyena_filter_fft",
    )(k2, gf, f2)


def _normalised(k2, ssq, extra_scale):
    return k2 * (lax.rsqrt(ssq + EPS) * extra_scale)


def _hy_mid_kernel(y_ref, hv_ref, b_ref, o_ref):
    v = hv_ref[0][:, 0:D_HY]
    x1 = hv_ref[0][:, D_HY:2 * D_HY]
    o_ref[0] = x1 * (y_ref[0] + v * b_ref[...])


def _hy_mid(y1, hv, bias0):
    nb, length, _ = y1.shape
    te = 4 * TT
    return pl.pallas_call(
        _hy_mid_kernel, grid=(nb, length // te),
        in_specs=[pl.BlockSpec((1, te, D_HY), lambda b, t: (b, t, 0)),
                  pl.BlockSpec((1, te, 512), lambda b, t: (b, t, 0)),
                  pl.BlockSpec((1, D_HY), lambda b, t: (0, 0))],
        out_specs=pl.BlockSpec((1, te, D_HY), lambda b, t: (b, t, 0)),
        out_shape=jax.ShapeDtypeStruct(y1.shape, F32),
        compiler_params=_params(("arbitrary", "arbitrary")), name="hyena_mid",
    )(y1, hv, bias0.reshape(1, D_HY))


def _hy_ctx_kernel(lc, hv_ref, zh_ref, k2_ref, b_ref, fc_ref, ff_ref, ci_ref, o_ref):
    nf = 2 * lc
    hv = hv_ref[0]

    def conv(z, o):
        kf = _dot_exact_rhs(ff_ref[...], k2_ref[:, o * D_HY:(o + 1) * D_HY])
        zs = _dot(fc_ref[...], z.astype(BF16))
        zr, zi = zs[:nf], zs[nf:]
        kr, ki = kf[:nf], kf[nf:]
        ys = jnp.concatenate([zr * kr - zi * ki, zr * ki + zi * kr], axis=0).astype(BF16)
        return _dot(ci_ref[...], ys) * (1.0 / nf)

    v = hv[:, 0:D_HY]
    x1 = hv[:, D_HY:2 * D_HY]
    x2 = hv[:, 2 * D_HY:3 * D_HY]
    z1 = x1 * (conv(v, 0) + v * b_ref[0:1])
    p = x2 * (conv(z1, 1) + z1 * b_ref[1:2])
    o_ref[0] = (p * _silu(zh_ref[0].astype(F32))).astype(BF16)


def _dot_exact_rhs(a_bf16, b):
    bh, bm, bl = _split3(b)
    return _dot(a_bf16, bh) + _dot(a_bf16, bm) + _dot(a_bf16, bl)


@functools.lru_cache(maxsize=None)
def _ctx_dft_tables(lc):
    nf = 2 * lc
    k = np.arange(nf)[:, None].astype(np.float64)
    ang = -2.0 * np.pi * k * np.arange(nf)[None, :] / nf
    ff = np.concatenate([np.cos(ang), np.sin(ang)], axis=0)
    fc = ff[:, :lc]
    a2 = 2.0 * np.pi * np.arange(lc)[:, None] * np.arange(nf)[None, :] / nf
    ci = np.concatenate([np.cos(a2), -np.sin(a2)], axis=1)
    cv = lambda m: np.asarray(m, np.float32).astype(BF16)
    return cv(fc), cv(ff), cv(ci)


def _hy_ctx(hv, pf, k2c, hy_bias, nt, lc):
    nb = hv.shape[0]
    fc, ff, ci = _ctx_dft_tables(lc)
    c2 = lambda b: (0, 0)
    return pl.pallas_call(
        functools.partial(_hy_ctx_kernel, lc), grid=(nb,),
        in_specs=[pl.BlockSpec((1, lc, 768), lambda b: (b, nt, 0)),
                  pl.BlockSpec((1, lc, D_HY), lambda b: (b, nt, PF_ZH)),
                  pl.BlockSpec(k2c.shape, c2),
                  pl.BlockSpec((2, D_HY), c2),
                  pl.BlockSpec(fc.shape, c2), pl.BlockSpec(ff.shape, c2), pl.BlockSpec(ci.shape, c2)],
        out_specs=pl.BlockSpec((1, lc, D_HY), lambda b: (b, 0, 0)),
        out_shape=jax.ShapeDtypeStruct((nb, lc, D_HY), BF16),
        compiler_params=_params(("arbitrary",)), name="hyena_ctx",
    )(hv, pf, k2c, hy_bias, fc, ff, ci)


def _out_kernel(n_hy, na_ref, h0_ref, h1_ref, om_ref, zm_ref, w_ref, x_ref, mod_ref, g_ref, *rest):
    hy_refs, o_ref = rest[:n_hy], rest[n_hy]
    chunks = [slice(r, r + TT) for r in range(0, o_ref.shape[1], TT)]
    acts = []
    for rows in chunks:
        if n_hy == 1:
            hy = hy_refs[0][0, rows]
        else:
            y_ref, z_ref, x2_ref, zh_ref, b_ref = hy_refs
            z = z_ref[0, rows]
            hy = (x2_ref[0, rows] * (y_ref[0, rows] + z * b_ref[...])
                  * _silu(zh_ref[0, rows].astype(F32))).astype(BF16)
        hsum = h0_ref[0, 0, rows].astype(F32) + h1_ref[0, 0, rows].astype(F32)
        ml = (jax.nn.sigmoid(om_ref[0, rows].astype(F32)) * hsum * _silu(zm_ref[0, rows].astype(F32))).astype(BF16)
        acts.append((ml, hy))
    ys = [_dot(na_ref[0, rows], w_ref[0:D_NA]) + _dot(ml, w_ref[D_NA:D_NA + D_ML])
          + _dot(hy, w_ref[D_NA + D_ML:D_MODEL]) for rows, (ml, hy) in zip(chunks, acts)]
    for rows, y in zip(chunks, ys):
        ms = jnp.mean(y * y, axis=-1, keepdims=True)
        r = y * lax.rsqrt(ms + EPS) * g_ref[...]
        o_ref[0, rows] = x_ref[0, rows] + mod_ref[0][2:3] * r


def _out_projection(na_g, hml, pf, hy_args, w_out, res, res_t0, mod3, g_post, to, t0, ntiles, mod_row, out_rows,
                    prev=None):
    nb = na_g.shape[0]
    d = D_MODEL
    tok = lambda col: (lambda b, t: (b, t + t0, col))
    own = lambda col: (lambda b, t: (b, t, col))
    in_specs = [pl.BlockSpec((1, to, 512), tok(0)),
                pl.BlockSpec((1, 1, to, 256), lambda b, t: (b, 0, t + t0, 0)),
                pl.BlockSpec((1, 1, to, 256), lambda b, t: (b, 1, t + t0, 0)),
                pl.BlockSpec((1, to, 256), tok(PF_OM)), pl.BlockSpec((1, to, 256), tok(PF_ZM)),
                pl.BlockSpec((d, d), lambda b, t: (0, 0), pipeline_mode=pl.Buffered(1)),
                pl.BlockSpec((1, to, d), lambda b, t: (b, t + res_t0, 0)),
                pl.BlockSpec((1, 3, d), lambda b, t: (mod_row(b), 0, 0)),
                pl.BlockSpec((1, d), lambda b, t: (0, 0))]
    args = [na_g, hml, hml, pf, pf, w_out, res, mod3, g_post.reshape(1, d)]
    if len(hy_args) == 1:
        in_specs.append(pl.BlockSpec((1, to, D_HY), own(0)))
        args.append(hy_args[0])
    else:
        y2, z1, hv, pf_, bias1 = hy_args
        in_specs += [pl.BlockSpec((1, to, D_HY), own(0)), pl.BlockSpec((1, to, D_HY), own(0)),
                     pl.BlockSpec((1, to, D_HY), tok(2)), pl.BlockSpec((1, to, D_HY), tok(PF_ZH)),
                     pl.BlockSpec((1, D_HY), lambda b, t: (0, 0))]
        args += [y2, z1, hv, pf_, bias1.reshape(1, D_HY)]
    n_hy = len(hy_args)
    aliases = {}
    if prev is not None:
        in_specs.append(pl.BlockSpec(memory_space=pl.ANY))
        args.append(prev)
        aliases = {len(args) - 1: 0}
        kern = lambda *r: _out_kernel(n_hy, *r[:9 + n_hy], r[10 + n_hy])
    else:
        kern = functools.partial(_out_kernel, n_hy)
    return pl.pallas_call(
        kern, grid=(nb, ntiles), in_specs=in_specs,
        out_specs=pl.BlockSpec((1, to, d), tok(0)),
        out_shape=jax.ShapeDtypeStruct((nb, out_rows, d), F32),
        input_output_aliases=aliases,
        compiler_params=_params(("arbitrary", "arbitrary")), name="out_proj_norm_residual",
    )(*args)


@functools.lru_cache(maxsize=None)
def _rope_tables(length, lc):
    n = HEAD_DIM // 4
    t = np.arange(length)
    inv = (ROPE_BASE ** (-np.arange(n, dtype=np.float32) / n)).astype(np.float32)
    pos = np.stack([t // GRID_W, t % GRID_W], axis=-1).astype(np.float32)
    ang = (pos[:, :, None] * inv).astype(np.float32)
    cos_h = np.concatenate([np.cos(ang[:, 0]), np.cos(ang[:, 0]), np.cos(ang[:, 1]), np.cos(ang[:, 1])], axis=-1)
    sin_h = np.concatenate([-np.sin(ang[:, 0]), np.sin(ang[:, 0]), -np.sin(ang[:, 1]), np.sin(ang[:, 1])], axis=-1)
    cos_t = np.concatenate([np.tile(cos_h, (1, 2 * H_ML)), np.ones((lc, 512), np.float32)], axis=0)
    sin_t = np.concatenate([np.tile(sin_h, (1, 2 * H_ML)), np.zeros((lc, 512), np.float32)], axis=0)
    kscale = np.concatenate([np.ones(256, np.float32), np.full(256, HEAD_DIM ** -0.5, np.float32)])
    return (cos_t * kscale).astype(np.float32), (sin_t * kscale).astype(np.float32)


@functools.lru_cache(maxsize=None)
def _scan_constants():
    tril = np.tril(np.ones((TT, TT), np.float32))
    emat = np.zeros((128, D_ML), np.float32)
    for h in range(H_ML):
        emat[h, h * HEAD_DIM:(h + 1) * HEAD_DIM] = 1.0
    rh = np.arange(D_ML)[:, None] // HEAD_DIM
    chd = np.arange(D_ML)[None, :] // HEAD_DIM
    bmask = (rh == chd).astype(np.float32)
    tri = np.concatenate([tril, tril.T], axis=0)
    return tri.astype(BF16), emat.astype(BF16), bmask


def _layer(x_lat, x_ctx, ctx_tile, mod3, nb, length, lc, g_pre, g_post, w_in, b_if, conv_ml, conv_hy, rpb,
           hf_w1, hf_b1, hf_w2, hf_b2, hf_w3, hf_freq, hy_bias, w_out, update_ctx):
    nt = length // TT
    ltot = length + lc
    tri, emat, bmask = _scan_constants()
    cos_t, sin_t = _rope_tables(length, lc)

    c = lambda a, b: w_in[:, a:b]
    wna = jnp.concatenate([c(0, 512) * (HEAD_DIM ** -0.5), c(512, 1536)], axis=1).astype(BF16)
    wpr = jnp.concatenate([c(1536, 2048), c(2560, 2816), c(2816, 3072), c(3072, 3328), c(4112, 4368)],
                          axis=1).astype(BF16)
    wlo = jnp.concatenate([c(2048, 2560), c(3344, 4112), c(3328, 3344), jnp.zeros((D_MODEL, 112), F32)],
                          axis=1).astype(BF16)
    bif_row = jnp.zeros((1, 128), F32).at[0, :4 * H_ML].set(b_if.reshape(-1))
    ona, pf, qk, hv, gates = _projection(x_lat, x_ctx, ctx_tile, mod3, g_pre, wna, wpr, wlo, conv_ml, conv_hy,
                                         bif_row, cos_t, sin_t, tri, nt)

    na_g = _attention(ona, pf, _na_bias_table(rpb), nt, update_ctx)

    hml = _mlstm_scan(qk, pf, gates, emat, bmask, nt)

    k2, ssq = _hyena_filters(length, hf_w1, hf_b1, hf_w2, hf_b2, hf_w3, hf_freq)
    kf = _filter_spectrum(_normalised(k2, ssq, 1.0 / (2 * length)), length)
    hv4 = hv.reshape(nb // 2, 2, ltot, 768)
    y1 = _hyena_conv(hv4, 0, kf[0], length).reshape(nb, length, D_HY)
    z1 = _hy_mid(y1, hv, hy_bias[0])
    y2 = _hyena_conv(z1.reshape(nb // 2, 2, length, D_HY), 0, kf[1], length).reshape(nb, length, D_HY)
    hy_args = (y2, z1, hv, pf, hy_bias[1])
    to = 2 * TT
    if not update_ctx:
        return _out_projection(na_g, hml, pf, hy_args, w_out, x_lat, 0, mod3, g_post, to, 0, length // to,
                               lambda b: b, length)

    xn = _out_projection(na_g, hml, pf, hy_args, w_out, x_lat, 0, mod3, g_post, to, 0, length // to,
                         lambda b: b, ltot)
    k2c, ssq_c = _hyena_filters(lc, hf_w1, hf_b1, hf_w2, hf_b2, hf_w3, hf_freq)
    hyc_g = _hy_ctx(hv, pf, _normalised(k2c, ssq_c, 1.0), hy_bias, nt, lc)
    return _out_projection(na_g, hml, pf, (hyc_g,), w_out, x_ctx, ctx_tile, mod3, g_post, TT, nt, 1,
                           lambda b: nb, ltot, prev=xn)


def kernel(x, c, ctx, c_ctx, w_ada, b_ada, g_pre, g_post, w_in, b_if, conv_ml, conv_hy, rpb, hf_w1, hf_b1,
           hf_w2, hf_b2, hf_w3, hf_freq, hy_bias, w_out):
    nb, length, d = x.shape
    lc = ctx.shape[1]
    depth = w_in.shape[0]
    assert lc == TT and length % (4 * TT) == 0 and nb % 2 == 0 and nb % SCAN_NB == 0 and d == D_MODEL
    cc = jnp.zeros((16, d), F32).at[:nb].set(c).at[nb].set(c_ctx)
    x_lat, x_ctx, ctx_tile = x, ctx, 0
    for l in range(depth):
        mod3 = _modulation(cc, w_ada[l], b_ada[l]).reshape(16, 3, d)
        x_lat = _layer(x_lat, x_ctx, ctx_tile, mod3, nb, length, lc, g_pre[l], g_post[l], w_in[l], b_if[l],
                       conv_ml[l], conv_hy[l], rpb[l], hf_w1[l], hf_b1[l], hf_w2[l], hf_b2[l], hf_w3[l],
                       hf_freq[l], hy_bias[l], w_out[l].astype(BF16), l < depth - 1)
        x_ctx, ctx_tile = x_lat, length // TT
    return x_lat
```

```python
import functools
import math

import numpy as np
import jax
import jax.numpy as jnp
from jax import lax
from jax.experimental import pallas as pl
from jax.experimental.pallas import tpu as pltpu

F32 = jnp.float32
BF16 = jnp.bfloat16

D_MODEL = 1024
GRID_W = 64
HEAD_DIM = 64
D_NA = 512
D_ML = 256
D_HY = 256
H_NA = 8
H_ML = 4
NA_ROWS = 8
NA_COLS = 16
HY_EMB = 33
HY_FFN = 64
HY_TARGET = 1e-2
HY_FAST = 0.3
HY_SLOW = 1.5
ROPE_BASE = 10000.0
EPS = 1e-6

TT = 256
BAND_ROWS = TT // GRID_W
SCAN_NB = 4
PRE_ROWS = 64
FFT_N2 = 128
FFT_PAD = 8
FFT_UNROLL = 8
NEG = -1e30
LOG2E = math.log2(math.e)
VMEM_LIMIT = 56 * 1024 * 1024
FFTCONV_VMEM_LIMIT = 59 * 1024 * 1024

PF_ZA = 0
PF_VM, PF_OM, PF_ZM, PF_ZH = 2, 3, 4, 5
LOC_QK, LOC_HY, LOC_GM, LOC_COLS = 0, 512, 1280, 1408
HALO = 8


def _dot(a, b):
    return jnp.dot(a, b, preferred_element_type=F32)


def _dot_nt(a, b):
    return lax.dot_general(a, b, (((1,), (1,)), ((), ())), preferred_element_type=F32)


def _dot_tn(a, b):
    return lax.dot_general(a, b, (((0,), (0,)), ((), ())), preferred_element_type=F32)


def _split3(x):
    h = x.astype(BF16)
    r = x - h.astype(F32)
    m = r.astype(BF16)
    l = (r - m.astype(F32)).astype(BF16)
    return h, m, l


def _dot_f32(a, b):
    ah, am, al = _split3(a)
    bh, bm, bl = _split3(b)
    return (_dot(ah, bh) + (_dot(ah, bm) + _dot(am, bh))
            + (_dot(ah, bl) + _dot(al, bh) + _dot(am, bm)))


def _dot_2x(a, b):
    ah, am, _ = _split3(a)
    bh, bm, _ = _split3(b)
    return _dot(ah, bh) + (_dot(ah, bm) + _dot(am, bh))


def _silu(x):
    return x * jax.nn.sigmoid(x)


def _params(sem, vmem=VMEM_LIMIT):
    return pltpu.CompilerParams(dimension_semantics=sem, vmem_limit_bytes=vmem)


def _mod_kernel(c_ref, w_ref, b_ref, o_ref):
    o_ref[...] = _dot_f32(_silu(c_ref[...]), w_ref[...]) + b_ref[...]


def _modulation(cc, w_ada, b_ada):
    n = w_ada.shape[1]
    bn = 512
    return pl.pallas_call(
        _mod_kernel, grid=(n // bn,),
        in_specs=[pl.BlockSpec(cc.shape, lambda j: (0, 0)),
                  pl.BlockSpec((w_ada.shape[0], bn), lambda j: (0, j)),
                  pl.BlockSpec((1, bn), lambda j: (0, j))],
        out_specs=pl.BlockSpec((cc.shape[0], bn), lambda j: (0, j)),
        out_shape=jax.ShapeDtypeStruct((cc.shape[0], n), F32),
        compiler_params=_params(("arbitrary",)), name="adaln_mod",
    )(cc, w_ada, b_ada.reshape(1, n))


def _proj_kernel(nt, x_ref, c_ref, xp_ref, xn_ref, mod_ref, g_ref, wna_ref, wpr_ref, wlo_ref, cml_ref, chy_ref,
                 bif_ref, cos_ref, sin_ref, tri_ref, ona_ref, opr_ref, oqk_ref, ohv_ref, og_ref, loc_ref):
    t = pl.program_id(0)
    x = jnp.where(t == nt, c_ref[0], x_ref[0])
    xs = jnp.concatenate([xp_ref[0], x, xn_ref[0]], axis=0)
    ms = jnp.mean(xs * xs, axis=-1, keepdims=True)
    y = xs * lax.rsqrt(ms + EPS) * g_ref[...]
    mod = mod_ref[0]
    h = (y * (1.0 + mod[1:2]) + mod[0:1]).astype(BF16)
    hm = h[HALO:HALO + TT]
    loc_ref[...] = _dot(h, wlo_ref[...])
    ona_ref[0] = _dot(hm, wna_ref[...]).astype(BF16)

    pm = jnp.where(jnp.logical_and(t != 0, t != nt), 1.0, 0.0)
    nm = jnp.where(jnp.logical_and(t != nt - 1, t != nt), 1.0, 0.0)
    row8 = lax.broadcasted_iota(jnp.int32, (8, 1), 0)
    nchunk = TT // PRE_ROWS

    def conv3(cols, w, c):
        r0 = HALO + c * PRE_ROWS
        u = loc_ref[r0:r0 + PRE_ROWS, cols]
        up = loc_ref[r0 - 1:r0, cols]
        un = loc_ref[r0 + PRE_ROWS:r0 + PRE_ROWS + 1, cols]
        if c == 0:
            up = up * pm
        if c == nchunk - 1:
            un = un * nm
        rp = pltpu.roll(u, 1, 0)
        rn = pltpu.roll(u, PRE_ROWS - 1, 0)
        prev = jnp.concatenate([jnp.where(row8 == 0, up, rp[0:8]), rp[8:]], axis=0)
        nxt = jnp.concatenate([rn[:PRE_ROWS - 8], jnp.where(row8 == 7, un, rn[PRE_ROWS - 8:])], axis=0)
        return prev * w[0:1] + u * w[1:2] + nxt * w[2:3]

    lane = lax.broadcasted_iota(jnp.int32, (1, 2 * D_ML), 1)
    first = (lane % 32) < 16
    for c in range(nchunk):
        rows = slice(c * PRE_ROWS, (c + 1) * PRE_ROWS)
        ohv_ref[0, rows] = conv3(slice(LOC_HY, LOC_GM), chy_ref[...], c).astype(BF16)
        x = _silu(conv3(slice(LOC_QK, LOC_HY), cml_ref[...], c))
        partner = jnp.where(first, pltpu.roll(x, 2 * D_ML - 16, 1), pltpu.roll(x, 16, 1))
        oqk_ref[0, rows] = (x * cos_ref[rows] + partner * sin_ref[rows]).astype(BF16)

    g = loc_ref[HALO:HALO + TT, LOC_GM:LOC_COLS] + bif_ref[...]
    gl = lax.broadcasted_iota(jnp.int32, (1, 128), 1)
    is_f = jnp.logical_and((gl % 8) >= 4, gl < 16)
    lf = jnp.where(is_f, jnp.minimum(g, 0.0) - jnp.log1p(jnp.exp(-jnp.abs(g))), 0.0)
    hi, mid, lo = _split3(lf)
    packed = (hi.astype(F32) + pltpu.roll(mid.astype(F32), 16, 1) + pltpu.roll(lo.astype(F32), 32, 1)).astype(BF16)
    cs = _dot(tri_ref[...], packed)
    cs = cs + pltpu.roll(cs, 112, 1) + pltpu.roll(cs, 96, 1)
    bcum = jnp.where(gl < 8, cs[0:TT], cs[TT:2 * TT])
    a = g - pltpu.roll(bcum, 128 - H_ML, 1)
    ab = jnp.where(is_f, bcum, a)
    keep = gl < 2 * H_ML
    og_ref[0, 0] = jnp.where(keep, ab, 0.0)
    og_ref[0, 1] = jnp.where(keep, pltpu.roll(ab, 120, 1), 0.0)
    opr_ref[0] = _dot(hm, wpr_ref[...]).astype(BF16)


def _projection(x_lat, x_ctx, ctx_tile, mod3, g_pre, wna, wpr, wlo, conv_ml, conv_hy, bif_row, cos_t, sin_t,
                tri, nt):
    nb, lat_rows, d = x_lat.shape
    ltot = (nt + 1) * TT
    ctx_row = nb
    last8 = lat_rows // HALO - 1
    r8 = TT // HALO

    def mod_idx(t, b):
        return (jnp.where(t == nt, ctx_row, b), 0, 0)

    const = lambda t, b: (0, 0)
    big = pl.Buffered(1)
    tok = lambda t, b: (b, t, 0)
    return pl.pallas_call(
        functools.partial(_proj_kernel, nt), grid=(nt + 1, nb),
        in_specs=[pl.BlockSpec((1, TT, d), lambda t, b: (b, jnp.minimum(t, nt - 1), 0)),
                  pl.BlockSpec((1, TT, d), lambda t, b: (b, ctx_tile, 0)),
                  pl.BlockSpec((1, HALO, d), lambda t, b: (b, jnp.clip(t * r8 - 1, 0, last8), 0)),
                  pl.BlockSpec((1, HALO, d), lambda t, b: (b, jnp.minimum((t + 1) * r8, last8), 0)),
                  pl.BlockSpec((1, 3, d), mod_idx),
                  pl.BlockSpec((1, d), const),
                  pl.BlockSpec(wna.shape, const, pipeline_mode=big),
                  pl.BlockSpec(wpr.shape, const, pipeline_mode=big),
                  pl.BlockSpec(wlo.shape, const, pipeline_mode=big),
                  pl.BlockSpec((3, 512), const), pl.BlockSpec((3, 768), const), pl.BlockSpec((1, 128), const),
                  pl.BlockSpec((TT, 512), lambda t, b: (t, 0)), pl.BlockSpec((TT, 512), lambda t, b: (t, 0)),
                  pl.BlockSpec((2 * TT, TT), const)],
        out_specs=[pl.BlockSpec((1, TT, wna.shape[1]), tok), pl.BlockSpec((1, TT, wpr.shape[1]), tok),
                   pl.BlockSpec((1, TT, 512), tok), pl.BlockSpec((1, TT, 768), tok),
                   pl.BlockSpec((1, 2, TT, 128), lambda t, b: (b, 0, t, 0))],
        out_shape=[jax.ShapeDtypeStruct((nb, ltot, wna.shape[1]), BF16),
                   jax.ShapeDtypeStruct((nb, ltot, wpr.shape[1]), BF16),
                   jax.ShapeDtypeStruct((nb, ltot, 512), BF16),
                   jax.ShapeDtypeStruct((nb, ltot, 768), BF16),
                   jax.ShapeDtypeStruct((nb, 2, ltot, 128), F32)],
        scratch_shapes=[pltpu.VMEM((TT + 2 * HALO, LOC_COLS), F32)],
        compiler_params=_params(("arbitrary", "arbitrary")), name="norm_in_proj_conv",
    )(x_lat, x_ctx, x_lat, x_lat, mod3, g_pre.reshape(1, d), wna, wpr, wlo, conv_ml, conv_hy, bif_row,
      cos_t, sin_t, tri)


def _na_kernel(q_ref, k0_ref, k1_ref, k2_ref, v0_ref, v1_ref, v2_ref, kc_ref, vc_ref, za_ref, tab_ref, o_ref):
    lane = lax.broadcasted_iota(jnp.int32, (1, 128), 1)
    k_refs = (k0_ref, k1_ref, k2_ref)
    v_refs = (v0_ref, v1_ref, v2_ref)

    def row_max(s):
        return jnp.max(jnp.maximum(s[:, :128], s[:, 128:]), axis=-1, keepdims=True)

    def head_mask(h):
        return (lane >= HEAD_DIM) if h % 2 else (lane < HEAD_DIM)

    def masked_q(h):
        sl = slice((h // 2) * 128, (h // 2 + 1) * 128)
        q2 = q_ref[0, :, sl]
        return jnp.where(head_mask(h), q2, jnp.zeros_like(q2))

    def score_block(h, qm, blk):
        sl = slice((h // 2) * 128, (h // 2 + 1) * 128)
        if blk == 0:
            return _dot_nt(qm, kc_ref[0, :, sl])
        return _dot_nt(qm, k_refs[blk - 1][0, :, sl]) + tab_ref[0, h, :, (blk - 1) * TT:blk * TT]

    def values(h, p):
        sl = slice((h // 2) * 128, (h // 2 + 1) * 128)
        hm = head_mask(h)
        v = jnp.concatenate([vc_ref[0, :, sl]] + [r[0, :, sl] for r in v_refs], axis=0)
        acc = _dot(p, jnp.where(hm, v, jnp.ones_like(v)))
        return jnp.where(hm, acc / pltpu.roll(acc, HEAD_DIM, 1), 0.0)

    outs = {}
    qm = masked_q(0)
    sc = [score_block(0, qm, blk) for blk in range(4)]
    for h in range(H_NA):
        m = row_max(jnp.maximum(jnp.maximum(sc[0], sc[1]), jnp.maximum(sc[2], sc[3])))
        qm = masked_q(h + 1) if h + 1 < H_NA else None
        nxt, ps = [], []
        for blk in range(4):
            if qm is not None:
                nxt.append(score_block(h + 1, qm, blk))
            ps.append(jnp.exp2(sc[blk] - m).astype(BF16))
        sc = nxt
        outs[h] = values(h, jnp.concatenate(ps, axis=1))
        if h % 2 == 1:
            sl = slice((h // 2) * 128, (h // 2 + 1) * 128)
            o_ref[0, :, sl] = ((outs[h - 1] + outs[h]) * _silu(za_ref[0, :, sl].astype(F32))).astype(BF16)


def _attention(ona, pf, table, nt, with_ctx_queries):
    nb, ltot, _ = ona.shape
    nj = nt + 1 if with_ctx_queries else nt

    def kv(col, dj):
        return lambda j, b: (b, jnp.clip(j + dj, 0, nt - 1), col)

    def pat(j, b):
        p = jnp.where(j == 0, 0, jnp.where(j == nt - 1, 2, jnp.where(j == nt, 3, 1)))
        return (p, 0, 0, 0)

    return pl.pallas_call(
        _na_kernel, grid=(nj, nb),
        in_specs=[pl.BlockSpec((1, TT, 512), lambda j, b: (b, j, 0)),
                  pl.BlockSpec((1, TT, 512), kv(1, -1)), pl.BlockSpec((1, TT, 512), kv(1, 0)),
                  pl.BlockSpec((1, TT, 512), kv(1, 1)),
                  pl.BlockSpec((1, TT, 512), kv(2, -1)), pl.BlockSpec((1, TT, 512), kv(2, 0)),
                  pl.BlockSpec((1, TT, 512), kv(2, 1)),
                  pl.BlockSpec((1, TT, 512), lambda j, b: (b, nt, 1)),
                  pl.BlockSpec((1, TT, 512), lambda j, b: (b, nt, 2)),
                  pl.BlockSpec((1, TT, 512), lambda j, b: (b, j, PF_ZA)),
                  pl.BlockSpec((1, H_NA, TT, 3 * TT), pat)],
        out_specs=pl.BlockSpec((1, TT, 512), lambda j, b: (b, j, 0)),
        out_shape=jax.ShapeDtypeStruct((nb, ltot, 512), BF16),
        compiler_params=_params(("arbitrary", "arbitrary")), name="nbr_attention",
    )(ona, ona, ona, ona, ona, ona, ona, ona, ona, pf, table)


def _na_bias_table(rpb):
    a = np.arange(TT) // GRID_W
    qc = np.arange(TT) % GRID_W
    kidx = np.arange(3 * TT)
    kr = (kidx // TT - 1) * BAND_ROWS + (kidx % TT) // GRID_W
    kcol = kidx % GRID_W
    qstart = np.clip(qc - NA_COLS // 2, 0, GRID_W - NA_COLS)
    col_ok = (kcol[None, :] >= qstart[:, None]) & (kcol[None, :] < qstart[:, None] + NA_COLS)
    coff = np.clip(kcol[None, :] - qc[:, None], -(NA_COLS - 1), NA_COLS - 1) + NA_COLS - 1
    roff = np.clip(kr[None, :] - a[:, None] + NA_ROWS - 1, 0, 2 * NA_ROWS - 2)
    half = NA_ROWS // 2
    row_ok = [
        (kr[None, :] >= 0) & (kr[None, :] < NA_ROWS) & (a[:, None] >= 0),
        (kr[None, :] >= a[:, None] - half) & (kr[None, :] < a[:, None] + half),
        (kr[None, :] >= BAND_ROWS - NA_ROWS) & (kr[None, :] < BAND_ROWS) & (a[:, None] >= 0),
    ]
    nr, ncol = 2 * NA_ROWS - 1, 2 * NA_COLS - 1
    onehot = (coff[:GRID_W, :GRID_W].reshape(1, -1) == np.arange(ncol)[:, None]).astype(np.float32)
    cmat = jnp.dot(rpb.astype(F32).reshape(H_NA * nr, ncol), onehot,
                   precision=lax.Precision.HIGHEST).reshape(H_NA, nr, GRID_W, GRID_W)
    nkr = 3 * BAND_ROWS
    blocks = [jnp.concatenate([cmat[:, roff[ai * GRID_W, kj * GRID_W]] for kj in range(nkr)], axis=-1)
              for ai in range(BAND_ROWS)]
    bias = jnp.concatenate(blocks, axis=1)
    bias = bias * LOG2E
    tabs = [jnp.where((r & col_ok)[None], bias, NEG) for r in row_ok]
    tabs.append(jnp.full_like(bias, NEG))
    return jnp.stack(tabs, axis=0)


def _head_lane_sums(x, lane_h):
    out = jnp.zeros_like(x)
    half = lax.broadcasted_iota(jnp.int32, (1, 128), 1) // HEAD_DIM
    for h in range(H_ML):
        t = x[:, (h // 2) * 128:(h // 2 + 1) * 128]
        sm = jnp.sum(jnp.where(half == h % 2, t, 0.0), axis=-1, keepdims=True)
        out = jnp.where(lane_h == h, sm, out)
    return out


def _scan_kernel(q_ref, k_ref, v_ref, g_ref, e_ref, bm_ref, o_ref, w_ref, n_ref, m_ref):
    s = pl.program_id(2)
    sgn = 1 - 2 * pl.program_id(1)

    @pl.when(s == 0)
    def _():
        w_ref[...] = jnp.zeros_like(w_ref)
        n_ref[...] = jnp.zeros_like(n_ref)
        m_ref[...] = jnp.zeros_like(m_ref)

    lane_h = lax.broadcasted_iota(jnp.int32, (1, D_ML), 1) // HEAD_DIM
    ri = lax.broadcasted_iota(jnp.int32, (TT, TT), 0)
    ci = lax.broadcasted_iota(jnp.int32, (TT, TT), 1)
    tri = ((ci - ri) * sgn) <= 0
    gate_lanes = lax.broadcasted_iota(jnp.int32, (1, 128), 1) < H_ML

    seqs = range(SCAN_NB)
    gate, kws, kwfs, vbs = [], [], [], []
    for i in seqs:
        gi = g_ref[i, 0]
        a = jnp.where(gate_lanes, gi, 0.0)
        bb = jnp.where(gate_lanes, pltpu.roll(gi, 128 - H_ML, 1), 0.0)
        m_prev = m_ref[i]
        mc = jnp.maximum(m_prev, jnp.max(a, axis=0, keepdims=True))
        small = jnp.concatenate([jnp.exp(a - mc), jnp.exp(-bb - mc),
                                 jnp.broadcast_to(jnp.exp(m_prev - mc), (8, 128))], axis=0)
        sh = small.astype(BF16)
        sl = (small - sh.astype(F32)).astype(BF16)
        ex = _dot(sh, e_ref[...]) + _dot(sl, e_ref[...])
        gate.append((ex[TT:2 * TT], ex[2 * TT:2 * TT + 1], jnp.min(bb, axis=0, keepdims=True) + mc))
        kwf = k_ref[i].astype(F32) * ex[0:TT]
        kwfs.append(kwf)
        kws.append(kwf.astype(BF16))
        vbs.append(v_ref[i].astype(BF16))

    scores = [[_dot_nt(jnp.where(lane_h == h, q_ref[i], jnp.zeros_like(q_ref[i])), kws[i]) for h in range(H_ML)]
              for i in seqs]
    inters = [_dot(q_ref[i], w_ref[i].astype(BF16)) for i in seqs]
    updates = [_dot_tn(kws[i], vbs[i]) for i in seqs]

    for i in seqs:
        eb_e, beta, m_new = gate[i]
        intra = None
        den_intra = jnp.zeros((TT, D_ML), F32)
        for h in range(H_ML):
            sc = jnp.where(tri, scores[i][h], 0.0)
            den_intra = jnp.where(lane_h == h, jnp.sum(sc, axis=-1, keepdims=True), den_intra)
            vh = jnp.where(lane_h == h, vbs[i], jnp.zeros_like(vbs[i]))
            part = _dot(sc.astype(BF16), vh)
            intra = part if intra is None else intra + part
        num = beta * inters[i] + intra
        den = beta * _head_lane_sums(q_ref[i].astype(F32) * n_ref[i], lane_h) + den_intra
        o_ref[i, 0] = (num / jnp.maximum(jnp.abs(den), eb_e)).astype(BF16)

        w_ref[i] = beta * w_ref[i] + bm_ref[...] * updates[i]
        n_ref[i] = beta * n_ref[i] + jnp.sum(kwfs[i], axis=0, keepdims=True)
        m_ref[i] = m_new


def _mlstm_scan(qk, pf, gates, emat, bmask, nt):
    nb, ltot, _ = qk.shape
    g = SCAN_NB

    def chunk(s, d):
        return jnp.where(s == 0, nt, jnp.where(d == 0, s - 1, nt - s))

    const = lambda b, d, s: (0, 0)
    return pl.pallas_call(
        _scan_kernel, grid=(nb // g, 2, nt + 1),
        in_specs=[pl.BlockSpec((g, TT, 256), lambda b, d, s: (b, chunk(s, d), 0)),
                  pl.BlockSpec((g, TT, 256), lambda b, d, s: (b, chunk(s, d), 1)),
                  pl.BlockSpec((g, TT, 256), lambda b, d, s: (b, chunk(s, d), PF_VM)),
                  pl.BlockSpec((g, 1, TT, 128), lambda b, d, s: (b, d, chunk(s, d), 0)),
                  pl.BlockSpec((128, 256), const), pl.BlockSpec((256, 256), const)],
        out_specs=pl.BlockSpec((g, 1, TT, 256), lambda b, d, s: (b, d, chunk(s, d), 0)),
        out_shape=jax.ShapeDtypeStruct((nb, 2, ltot, 256), BF16),
        scratch_shapes=[pltpu.VMEM((g, D_ML, D_ML), F32), pltpu.VMEM((g, 1, D_ML), F32),
                        pltpu.VMEM((g, 1, 128), F32)],
        compiler_params=_params(("arbitrary", "arbitrary", "arbitrary")), name="mlstm_scan",
    )(qk, qk, pf, gates, emat, bmask)


def _hyfilt_kernel(tl, length, feat_ref, w1_ref, b1_ref, w2_ref, b2_ref, w3_ref, fq_ref, dec_ref, o_ref, ss_ref):
    i = pl.program_id(0)
    a = jnp.sin(fq_ref[0:1] * (_dot_2x(feat_ref[...], w1_ref[...]) + b1_ref[...]))
    a = jnp.sin(fq_ref[1:2] * (_dot_2x(a, w2_ref[...]) + b2_ref[...]))
    dec = dec_ref[...]
    f = _dot_2x(a, w3_ref[...]) * jnp.concatenate([dec, dec, dec, dec], axis=1)
    fwd = jnp.concatenate([f[:, 0:D_HY], f[:, 2 * D_HY:3 * D_HY]], axis=1)
    bwd = jnp.concatenate([f[:, D_HY:2 * D_HY], f[:, 3 * D_HY:4 * D_HY]], axis=1)
    row = lax.broadcasted_iota(jnp.int32, (tl, 1), 0) + i * tl
    k2 = jnp.where(row < length, fwd, jnp.where(row > length, bwd, 0.0))
    o_ref[...] = k2

    @pl.when(i == 0)
    def _():
        ss_ref[...] = jnp.zeros_like(ss_ref)

    ss_ref[...] += jnp.sum(k2 * k2, axis=0, keepdims=True)


def _hyena_filters(length, w1, b1, w2, b2, w3, freq):
    t = np.arange(length, dtype=np.float32)
    tn = (t / np.float32(length - 1)).astype(np.float32)
    bands = (HY_EMB - 1) // 2
    fr = np.linspace(1e-4, bands - 1, bands, dtype=np.float32)
    ang = (np.float32(2.0 * math.pi / length) * t[:, None] * fr[None, :]).astype(np.float32)
    feat = np.zeros((length, 128), np.float32)
    feat[:, 0] = tn
    feat[:, 1:1 + bands] = np.cos(ang)
    feat[:, 1 + bands:1 + 2 * bands] = -np.sin(ang)
    deltas = np.abs(np.linspace(math.log(HY_TARGET) / HY_SLOW, math.log(HY_TARGET) / HY_FAST, D_HY, dtype=np.float32))
    dec = np.exp(-tn[:, None] * deltas[None, :]).astype(np.float32)
    lag = np.concatenate([np.arange(length), [0], np.arange(length - 1, 0, -1)])
    feat, dec = feat[lag], dec[lag]

    pad2 = lambda m, r, c: jnp.zeros((r, c), F32).at[:m.shape[0], :m.shape[1]].set(m.astype(F32))
    w1p = pad2(w1, 128, 128)
    w2p = pad2(w2, 128, 128)
    w3p = pad2(w3, 128, 4 * D_HY)
    b1p = pad2(b1[None], 1, 128)
    b2p = pad2(b2[None], 1, 128)
    fqp = pad2(freq, 2, 128)
    tl = 512
    const = lambda i: (0, 0)
    return pl.pallas_call(
        functools.partial(_hyfilt_kernel, tl, length), grid=(2 * length // tl,),
        in_specs=[pl.BlockSpec((tl, 128), lambda i: (i, 0)),
                  pl.BlockSpec((128, 128), const), pl.BlockSpec((1, 128), const),
                  pl.BlockSpec((128, 128), const), pl.BlockSpec((1, 128), const),
                  pl.BlockSpec((128, 4 * D_HY), const), pl.BlockSpec((2, 128), const),
                  pl.BlockSpec((tl, D_HY), lambda i: (i, 0))],
        out_specs=[pl.BlockSpec((tl, 2 * D_HY), lambda i: (i, 0)), pl.BlockSpec((1, 2 * D_HY), const)],
        out_shape=[jax.ShapeDtypeStruct((2 * length, 2 * D_HY), F32), jax.ShapeDtypeStruct((1, 2 * D_HY), F32)],
        compiler_params=_params(("arbitrary",)), name="hyena_filter_mlp",
    )(feat, w1p, b1p, w2p, b2p, w3p, fqp, dec)


@functools.lru_cache(maxsize=None)
def _fft_tables(length):
    n = 2 * length
    n1 = n // FFT_N2
    nh = n1 // 2
    k1 = np.arange(n1)[None, :, None].astype(np.float64)
    j = np.arange(nh)[None, None, :].astype(np.float64)
    n2 = np.arange(FFT_N2)[:, None, None].astype(np.float64)
    ang = -2.0 * np.pi * k1 * (FFT_N2 * j + n2) / n
    gf = np.concatenate([np.cos(ang), np.sin(ang)], axis=1)
    gi = np.concatenate([np.cos(ang).transpose(0, 2, 1), np.sin(ang).transpose(0, 2, 1)], axis=1)
    kk = np.arange(FFT_N2)[:, None] * np.arange(FFT_N2)[None, :]
    a2 = -2.0 * np.pi * kk / FFT_N2
    f2 = np.concatenate([np.cos(a2), np.sin(a2)], axis=0)
    cv = lambda m: np.asarray(m, np.float32).astype(BF16)
    return cv(gf), cv(gi), cv(f2), n1


def _stage1(n1, z_of, gf_ref, scr_ref, cols):
    def group(gidx, c):
        n2s = [gidx * FFT_UNROLL + u for u in range(FFT_UNROLL)]
        zs = [z_of(n2) for n2 in n2s]
        rs = [_dot(gf_ref[n2], z) for n2, z in zip(n2s, zs)]
        for n2, r in zip(n2s, rs):
            off = pl.multiple_of(n2 * (n1 + FFT_PAD), 8)
            if cols == 256:
                scr_ref[0, pl.ds(off, n1), :] = r[:n1, :128] - r[n1:, 128:]
                scr_ref[1, pl.ds(off, n1), :] = r[n1:, :128] + r[:n1, 128:]
            else:
                scr_ref[0, pl.ds(off, n1), :] = r[:n1]
                scr_ref[1, pl.ds(off, n1), :] = r[n1:]
        return c

    lax.fori_loop(0, FFT_N2 // FFT_UNROLL, group, 0)


def _stage2(n1, k1, f2_ref, scr_ref):
    ar = scr_ref[0, pl.ds(k1, FFT_N2, stride=n1 + FFT_PAD), :]
    ai = scr_ref[1, pl.ds(k1, FFT_N2, stride=n1 + FFT_PAD), :]
    r = _dot(f2_ref[...], jnp.concatenate([ar, ai], axis=1).astype(BF16))
    return r[:128, :128] - r[128:, 128:], r[128:, :128] + r[:128, 128:]


def _hyconv_kernel(n1, kg, z_ref, gf_ref, f2_ref, kf_ref, gi_ref, o_ref, scr_ref, tp_ref):
    s = pl.program_id(2)
    nh = n1 // 2
    pitch = FFT_N2 + FFT_PAD

    @pl.when(s == 0)
    def _():
        for e in range(2):
            for j in range(nh):
                tp_ref[e, j * pitch:j * pitch + FFT_N2, :] = z_ref[0, e, j * FFT_N2:(j + 1) * FFT_N2, :].astype(F32)

        def z_of(n2):
            zr = tp_ref[0, pl.ds(n2, nh, stride=pitch), :]
            zi = tp_ref[1, pl.ds(n2, nh, stride=pitch), :]
            return jnp.concatenate([zr, zi], axis=1).astype(BF16)
        _stage1(n1, z_of, gf_ref, scr_ref, 256)

    def k1_group(gidx, c):
        idx = [gidx * FFT_UNROLL + u for u in range(FFT_UNROLL)]
        xs = [_stage2(n1, s * kg + i, f2_ref, scr_ref) for i in idx]
        ys = []
        for i, (xr, xi) in zip(idx, xs):
            off = pl.multiple_of(i * FFT_N2, FFT_N2)
            kr = kf_ref[0, 0, pl.ds(off, FFT_N2), :]
            ki = kf_ref[0, 1, pl.ds(off, FFT_N2), :]
            ys.append(jnp.concatenate([xr * kr - xi * ki, xr * ki + xi * kr], axis=1).astype(BF16))
        rs = [_dot(f2_ref[...], yc) for yc in ys]
        for i, r in zip(idx, rs):
            k1 = s * kg + i
            scr_ref[0, pl.ds(k1, FFT_N2, stride=n1 + FFT_PAD), :] = r[:128, :128] + r[128:, 128:]
            scr_ref[1, pl.ds(k1, FFT_N2, stride=n1 + FFT_PAD), :] = r[:128, 128:] - r[128:, :128]
        return c

    lax.fori_loop(0, kg // FFT_UNROLL, k1_group, 0)

    @pl.when(s == pl.num_programs(2) - 1)
    def _():
        def group(gidx, c):
            n2s = [gidx * FFT_UNROLL + u for u in range(FFT_UNROLL)]
            bcs = []
            for n2 in n2s:
                off = pl.multiple_of(n2 * (n1 + FFT_PAD), 8)
                bcs.append(jnp.concatenate([scr_ref[0, pl.ds(off, n1), :], scr_ref[1, pl.ds(off, n1), :]],
                                           axis=1).astype(BF16))
            rs = [_dot(gi_ref[n2], bc) for n2, bc in zip(n2s, bcs)]
            for n2, r in zip(n2s, rs):
                tp_ref[0, pl.ds(n2, nh, stride=pitch), :] = r[:nh, :128] + r[nh:, 128:]
                tp_ref[1, pl.ds(n2, nh, stride=pitch), :] = r[:nh, 128:] - r[nh:, :128]
            return c
        lax.fori_loop(0, FFT_N2 // FFT_UNROLL, group, 0)
        for e in range(2):
            for j in range(nh):
                o_ref[0, e, j * FFT_N2:(j + 1) * FFT_N2, :] = tp_ref[e, j * pitch:j * pitch + FFT_N2, :].astype(BF16)


def _hyena_conv(z4, col0, kf, order, length):
    gf, gi, f2, n1 = _fft_tables(length)
    n = 2 * length
    kg = min(n1, 2 * FFT_UNROLL)
    npair = z4.shape[0]
    big = pl.Buffered(1)
    return pl.pallas_call(
        functools.partial(_hyconv_kernel, n1, kg), grid=(npair, 2, n1 // kg),
        in_specs=[pl.BlockSpec((1, 2, length, 128), lambda p, c, s: (p, 0, 0, col0 + c)),
                  pl.BlockSpec(gf.shape, lambda p, c, s: (0, 0, 0), pipeline_mode=big),
                  pl.BlockSpec(f2.shape, lambda p, c, s: (0, 0)),
                  pl.BlockSpec((1, 2, kg * FFT_N2, 128), lambda p, c, s: (order, 0, s, c)),
                  pl.BlockSpec(gi.shape, lambda p, c, s: (0, 0, 0), pipeline_mode=big)],
        out_specs=pl.BlockSpec((1, 2, length, 128), lambda p, c, s: (p, 0, 0, c)),
        out_shape=jax.ShapeDtypeStruct((npair, 2, length, 256), BF16),
        scratch_shapes=[pltpu.VMEM((2, FFT_N2 * (n1 + FFT_PAD), 128), F32),
                        pltpu.VMEM((2, (n1 // 2) * (FFT_N2 + FFT_PAD), 128), F32)],
        compiler_params=_params(("arbitrary", "arbitrary", "arbitrary"), FFTCONV_VMEM_LIMIT), name="hyena_fftconv",
    )(z4, gf, f2, kf, gi)


def _hyfft_kernel(n1, kg, z_ref, gf_ref, f2_ref, o_ref, scr_ref):
    s = pl.program_id(2)

    @pl.when(s == 0)
    def _():
        _stage1(n1, lambda n2: z_ref[pl.ds(n2, n1, stride=FFT_N2), :].astype(BF16), gf_ref, scr_ref, 128)

    def k1_group(gidx, c):
        idx = [gidx * FFT_UNROLL + u for u in range(FFT_UNROLL)]
        xs = [_stage2(n1, s * kg + i, f2_ref, scr_ref) for i in idx]
        for i, (xr, xi) in zip(idx, xs):
            off = pl.multiple_of(i * FFT_N2, FFT_N2)
            o_ref[0, 0, pl.ds(off, FFT_N2), :] = xr
            o_ref[0, 1, pl.ds(off, FFT_N2), :] = xi
        return c

    lax.fori_loop(0, kg // FFT_UNROLL, k1_group, 0)


@functools.lru_cache(maxsize=None)
def _fft_table_full(length):
    n = 2 * length
    n1 = n // FFT_N2
    k1 = np.arange(n1)[None, :, None].astype(np.float64)
    j = np.arange(n1)[None, None, :].astype(np.float64)
    n2 = np.arange(FFT_N2)[:, None, None].astype(np.float64)
    ang = -2.0 * np.pi * k1 * (FFT_N2 * j + n2) / n
    return np.concatenate([np.cos(ang), np.sin(ang)], axis=1).astype(np.float32).astype(BF16)


def _filter_spectrum(k2, length):
    _, _, f2, n1 = _fft_tables(length)
    gf = _fft_table_full(length)
    n = 2 * length
    kg = min(n1, 16)
    nf = k2.shape[1] // D_HY
    big = pl.Buffered(1)
    return pl.pallas_call(
        functools.partial(_hyfft_kernel, n1, kg), grid=(nf, 2, n1 // kg),
        in_specs=[pl.BlockSpec((n, 128), lambda f, c, s: (0, 2 * f + c), pipeline_mode=big),
                  pl.BlockSpec(gf.shape, lambda f, c, s: (0, 0, 0), pipeline_mode=big),
                  pl.BlockSpec(f2.shape, lambda f, c, s: (0, 0))],
        out_specs=pl.BlockSpec((1, 2, kg * FFT_N2, 128), lambda f, c, s: (f, 0, s, c)),
        out_shape=jax.ShapeDtypeStruct((nf, 2, n, 256), F32),
        scratch_shapes=[pltpu.VMEM((2, FFT_N2 * (n1 + FFT_PAD), 128), F32)],
        compiler_params=_params(("arbitrary", "arbitrary", "arbitrary")), name="hyena_filter_fft",
    )(k2, gf, f2)


def _normalised(k2, ssq, extra_scale):
    return k2 * (lax.rsqrt(ssq + EPS) * extra_scale)


def _hy_mid_kernel(y_ref, hv_ref, b_ref, o_ref):
    v = hv_ref[0][:, 0:D_HY].astype(F32)
    x1 = hv_ref[0][:, D_HY:2 * D_HY].astype(F32)
    o_ref[0] = (x1 * (y_ref[0].astype(F32) + v * b_ref[...])).astype(BF16)


def _hy_mid(y1, hv, bias0):
    nb, length, _ = y1.shape
    te = 4 * TT
    return pl.pallas_call(
        _hy_mid_kernel, grid=(nb, length // te),
        in_specs=[pl.BlockSpec((1, te, D_HY), lambda b, t: (b, t, 0)),
                  pl.BlockSpec((1, te, 512), lambda b, t: (b, t, 0)),
                  pl.BlockSpec((1, D_HY), lambda b, t: (0, 0))],
        out_specs=pl.BlockSpec((1, te, D_HY), lambda b, t: (b, t, 0)),
        out_shape=jax.ShapeDtypeStruct(y1.shape, BF16),
        compiler_params=_params(("arbitrary", "arbitrary")), name="hyena_mid",
    )(y1, hv, bias0.reshape(1, D_HY))


def _hy_ctx_kernel(lc, hv_ref, zh_ref, k2_ref, b_ref, fc_ref, ff_ref, ci_ref, o_ref):
    nf = 2 * lc
    hv = hv_ref[0].astype(F32)

    def conv(z, o):
        kf = _dot_exact_rhs(ff_ref[...], k2_ref[:, o * D_HY:(o + 1) * D_HY])
        zs = _dot(fc_ref[...], z.astype(BF16))
        zr, zi = zs[:nf], zs[nf:]
        kr, ki = kf[:nf], kf[nf:]
        ys = jnp.concatenate([zr * kr - zi * ki, zr * ki + zi * kr], axis=0).astype(BF16)
        return _dot(ci_ref[...], ys) * (1.0 / nf)

    v = hv[:, 0:D_HY]
    x1 = hv[:, D_HY:2 * D_HY]
    x2 = hv[:, 2 * D_HY:3 * D_HY]
    z1 = x1 * (conv(v, 0) + v * b_ref[0:1])
    p = x2 * (conv(z1, 1) + z1 * b_ref[1:2])
    o_ref[0] = (p * _silu(zh_ref[0].astype(F32))).astype(BF16)


def _dot_exact_rhs(a_bf16, b):
    bh, bm, bl = _split3(b)
    return _dot(a_bf16, bh) + _dot(a_bf16, bm) + _dot(a_bf16, bl)


@functools.lru_cache(maxsize=None)
def _ctx_dft_tables(lc):
    nf = 2 * lc
    k = np.arange(nf)[:, None].astype(np.float64)
    ang = -2.0 * np.pi * k * np.arange(nf)[None, :] / nf
    ff = np.concatenate([np.cos(ang), np.sin(ang)], axis=0)
    fc = ff[:, :lc]
    a2 = 2.0 * np.pi * np.arange(lc)[:, None] * np.arange(nf)[None, :] / nf
    ci = np.concatenate([np.cos(a2), -np.sin(a2)], axis=1)
    cv = lambda m: np.asarray(m, np.float32).astype(BF16)
    return cv(fc), cv(ff), cv(ci)


def _hy_ctx(hv, pf, k2c, hy_bias, nt, lc):
    nb = hv.shape[0]
    fc, ff, ci = _ctx_dft_tables(lc)
    c2 = lambda b: (0, 0)
    return pl.pallas_call(
        functools.partial(_hy_ctx_kernel, lc), grid=(nb,),
        in_specs=[pl.BlockSpec((1, lc, 768), lambda b: (b, nt, 0)),
                  pl.BlockSpec((1, lc, D_HY), lambda b: (b, nt, PF_ZH)),
                  pl.BlockSpec(k2c.shape, c2),
                  pl.BlockSpec((2, D_HY), c2),
                  pl.BlockSpec(fc.shape, c2), pl.BlockSpec(ff.shape, c2), pl.BlockSpec(ci.shape, c2)],
        out_specs=pl.BlockSpec((1, lc, D_HY), lambda b: (b, 0, 0)),
        out_shape=jax.ShapeDtypeStruct((nb, lc, D_HY), BF16),
        compiler_params=_params(("arbitrary",)), name="hyena_ctx",
    )(hv, pf, k2c, hy_bias, fc, ff, ci)


def _out_kernel(n_hy, na_ref, h0_ref, h1_ref, om_ref, zm_ref, w_ref, x_ref, mod_ref, g_ref, *rest):
    hy_refs, o_ref = rest[:n_hy], rest[n_hy]
    chunks = [slice(r, r + TT) for r in range(0, o_ref.shape[1], TT)]
    acts = []
    for rows in chunks:
        if n_hy == 1:
            hy = hy_refs[0][0, rows]
        else:
            y_ref, z_ref, x2_ref, zh_ref, b_ref = hy_refs
            z = z_ref[0, rows].astype(F32)
            hy = (x2_ref[0, rows].astype(F32) * (y_ref[0, rows].astype(F32) + z * b_ref[...])
                  * _silu(zh_ref[0, rows].astype(F32))).astype(BF16)
        hsum = h0_ref[0, 0, rows].astype(F32) + h1_ref[0, 0, rows].astype(F32)
        ml = (jax.nn.sigmoid(om_ref[0, rows].astype(F32)) * hsum * _silu(zm_ref[0, rows].astype(F32))).astype(BF16)
        acts.append((ml, hy))
    ys = [_dot(na_ref[0, rows], w_ref[0:D_NA]) + _dot(ml, w_ref[D_NA:D_NA + D_ML])
          + _dot(hy, w_ref[D_NA + D_ML:D_MODEL]) for rows, (ml, hy) in zip(chunks, acts)]
    for rows, y in zip(chunks, ys):
        ms = jnp.mean(y * y, axis=-1, keepdims=True)
        r = y * lax.rsqrt(ms + EPS) * g_ref[...]
        o_ref[0, rows] = x_ref[0, rows] + mod_ref[0][2:3] * r


def _out_projection(na_g, hml, pf, hy_args, w_out, res, res_t0, mod3, g_post, to, t0, ntiles, mod_row, out_rows,
                    prev=None):
    nb = na_g.shape[0]
    d = D_MODEL
    tok = lambda col: (lambda b, t: (b, t + t0, col))
    own = lambda col: (lambda b, t: (b, t, col))
    in_specs = [pl.BlockSpec((1, to, 512), tok(0)),
                pl.BlockSpec((1, 1, to, 256), lambda b, t: (b, 0, t + t0, 0)),
                pl.BlockSpec((1, 1, to, 256), lambda b, t: (b, 1, t + t0, 0)),
                pl.BlockSpec((1, to, 256), tok(PF_OM)), pl.BlockSpec((1, to, 256), tok(PF_ZM)),
                pl.BlockSpec((d, d), lambda b, t: (0, 0), pipeline_mode=pl.Buffered(1)),
                pl.BlockSpec((1, to, d), lambda b, t: (b, t + res_t0, 0)),
                pl.BlockSpec((1, 3, d), lambda b, t: (mod_row(b), 0, 0)),
                pl.BlockSpec((1, d), lambda b, t: (0, 0))]
    args = [na_g, hml, hml, pf, pf, w_out, res, mod3, g_post.reshape(1, d)]
    if len(hy_args) == 1:
        in_specs.append(pl.BlockSpec((1, to, D_HY), own(0)))
        args.append(hy_args[0])
    else:
        y2, z1, hv, pf_, bias1 = hy_args
        in_specs += [pl.BlockSpec((1, to, D_HY), own(0)), pl.BlockSpec((1, to, D_HY), own(0)),
                     pl.BlockSpec((1, to, D_HY), tok(2)), pl.BlockSpec((1, to, D_HY), tok(PF_ZH)),
                     pl.BlockSpec((1, D_HY), lambda b, t: (0, 0))]
        args += [y2, z1, hv, pf_, bias1.reshape(1, D_HY)]
    n_hy = len(hy_args)
    aliases = {}
    if prev is not None:
        in_specs.append(pl.BlockSpec(memory_space=pl.ANY))
        args.append(prev)
        aliases = {len(args) - 1: 0}
        kern = lambda *r: _out_kernel(n_hy, *r[:9 + n_hy], r[10 + n_hy])
    else:
        kern = functools.partial(_out_kernel, n_hy)
    return pl.pallas_call(
        kern, grid=(nb, ntiles), in_specs=in_specs,
        out_specs=pl.BlockSpec((1, to, d), tok(0)),
        out_shape=jax.ShapeDtypeStruct((nb, out_rows, d), F32),
        input_output_aliases=aliases,
        compiler_params=_params(("arbitrary", "arbitrary")), name="out_proj_norm_residual",
    )(*args)


@functools.lru_cache(maxsize=None)
def _rope_tables(length, lc):
    n = HEAD_DIM // 4
    t = np.arange(length)
    inv = (ROPE_BASE ** (-np.arange(n, dtype=np.float32) / n)).astype(np.float32)
    pos = np.stack([t // GRID_W, t % GRID_W], axis=-1).astype(np.float32)
    ang = (pos[:, :, None] * inv).astype(np.float32)
    cos_h = np.concatenate([np.cos(ang[:, 0]), np.cos(ang[:, 0]), np.cos(ang[:, 1]), np.cos(ang[:, 1])], axis=-1)
    sin_h = np.concatenate([-np.sin(ang[:, 0]), np.sin(ang[:, 0]), -np.sin(ang[:, 1]), np.sin(ang[:, 1])], axis=-1)
    cos_t = np.concatenate([np.tile(cos_h, (1, 2 * H_ML)), np.ones((lc, 512), np.float32)], axis=0)
    sin_t = np.concatenate([np.tile(sin_h, (1, 2 * H_ML)), np.zeros((lc, 512), np.float32)], axis=0)
    kscale = np.concatenate([np.ones(256, np.float32), np.full(256, HEAD_DIM ** -0.5, np.float32)])
    return (cos_t * kscale).astype(np.float32), (sin_t * kscale).astype(np.float32)


@functools.lru_cache(maxsize=None)
def _scan_constants():
    tril = np.tril(np.ones((TT, TT), np.float32))
    emat = np.zeros((128, D_ML), np.float32)
    for h in range(H_ML):
        emat[h, h * HEAD_DIM:(h + 1) * HEAD_DIM] = 1.0
    rh = np.arange(D_ML)[:, None] // HEAD_DIM
    chd = np.arange(D_ML)[None, :] // HEAD_DIM
    bmask = (rh == chd).astype(np.float32)
    tri = np.concatenate([tril, tril.T], axis=0)
    return tri.astype(BF16), emat.astype(BF16), bmask


def _layer(x_lat, x_ctx, ctx_tile, mod3, nb, length, lc, g_pre, g_post, w_in, b_if, conv_ml, conv_hy, rpb,
           hf_w1, hf_b1, hf_w2, hf_b2, hf_w3, hf_freq, hy_bias, w_out, update_ctx):
    nt = length // TT
    ltot = length + lc
    tri, emat, bmask = _scan_constants()
    cos_t, sin_t = _rope_tables(length, lc)

    c = lambda a, b: w_in[:, a:b]
    wna = jnp.concatenate([c(0, 512) * (HEAD_DIM ** -0.5 * LOG2E), c(512, 1536)], axis=1).astype(BF16)
    wpr = jnp.concatenate([c(1536, 2048), c(2560, 2816), c(2816, 3072), c(3072, 3328), c(4112, 4368)],
                          axis=1).astype(BF16)
    wlo = jnp.concatenate([c(2048, 2560), c(3344, 4112), c(3328, 3344), jnp.zeros((D_MODEL, 112), F32)],
                          axis=1).astype(BF16)
    bif_row = jnp.zeros((1, 128), F32).at[0, :4 * H_ML].set(b_if.reshape(-1))
    ona, pf, qk, hv, gates = _projection(x_lat, x_ctx, ctx_tile, mod3, g_pre, wna, wpr, wlo, conv_ml, conv_hy,
                                         bif_row, cos_t, sin_t, tri, nt)

    na_g = _attention(ona, pf, _na_bias_table(rpb), nt, update_ctx)

    hml = _mlstm_scan(qk, pf, gates, emat, bmask, nt)

    k2, ssq = _hyena_filters(length, hf_w1, hf_b1, hf_w2, hf_b2, hf_w3, hf_freq)
    kf = _filter_spectrum(_normalised(k2, ssq, 1.0 / (2 * length)), length)
    hv4 = hv.reshape(nb // 2, 2, ltot, 768)
    y1 = _hyena_conv(hv4, 0, kf, 0, length).reshape(nb, length, D_HY)
    z1 = _hy_mid(y1, hv, hy_bias[0])
    y2 = _hyena_conv(z1.reshape(nb // 2, 2, length, D_HY), 0, kf, 1, length).reshape(nb, length, D_HY)
    hy_args = (y2, z1, hv, pf, hy_bias[1])
    to = 2 * TT
    if not update_ctx:
        return _out_projection(na_g, hml, pf, hy_args, w_out, x_lat, 0, mod3, g_post, to, 0, length // to,
                               lambda b: b, length)

    xn = _out_projection(na_g, hml, pf, hy_args, w_out, x_lat, 0, mod3, g_post, to, 0, length // to,
                         lambda b: b, ltot)
    k2c, ssq_c = _hyena_filters(lc, hf_w1, hf_b1, hf_w2, hf_b2, hf_w3, hf_freq)
    hyc_g = _hy_ctx(hv, pf, _normalised(k2c, ssq_c, 1.0), hy_bias, nt, lc)
    return _out_projection(na_g, hml, pf, (hyc_g,), w_out, x_ctx, ctx_tile, mod3, g_post, TT, nt, 1,
                           lambda b: nb, ltot, prev=xn)


def kernel(x, c, ctx, c_ctx, w_ada, b_ada, g_pre, g_post, w_in, b_if, conv_ml, conv_hy, rpb, hf_w1, hf_b1,
           hf_w2, hf_b2, hf_w3, hf_freq, hy_bias, w_out):
    nb, length, d = x.shape
    lc = ctx.shape[1]
    depth = w_in.shape[0]
    assert lc == TT and length % (4 * TT) == 0 and nb % 2 == 0 and nb % SCAN_NB == 0 and d == D_MODEL
    cc = jnp.zeros((16, d), F32).at[:nb].set(c).at[nb].set(c_ctx)
    x_lat, x_ctx, ctx_tile = x, ctx, 0
    for l in range(depth):
        mod3 = _modulation(cc, w_ada[l], b_ada[l]).reshape(16, 3, d)
        x_lat = _layer(x_lat, x_ctx, ctx_tile, mod3, nb, length, lc, g_pre[l], g_post[l], w_in[l], b_if[l],
                       conv_ml[l], conv_hy[l], rpb[l], hf_w1[l], hf_b1[l], hf_w2[l], hf_b2[l], hf_w3[l],
                       hf_freq[l], hy_bias[l], w_out[l].astype(BF16), l < depth - 1)
        x_ctx, ctx_tile = x_lat, length // TT
    return x_lat
```

```python
import functools
import math

import numpy as np
import jax
import jax.numpy as jnp
from jax import lax
from jax.experimental import pallas as pl
from jax.experimental.pallas import tpu as pltpu

F32 = jnp.float32
BF16 = jnp.bfloat16

D_MODEL = 1024
GRID_W = 64
HEAD_DIM = 64
D_NA = 512
D_ML = 256
D_HY = 256
H_NA = 8
H_ML = 4
NA_ROWS = 8
NA_COLS = 16
HY_EMB = 33
HY_FFN = 64
HY_TARGET = 1e-2
HY_FAST = 0.3
HY_SLOW = 1.5
ROPE_BASE = 10000.0
EPS = 1e-6

TT = 256
BAND_ROWS = TT // GRID_W
SCAN_NB = 4
PRE_ROWS = 64
FFT_N2 = 128
FFT_PAD = 8
FFT_UNROLL = 8
NEG = -1e30
LOG2E = math.log2(math.e)
VMEM_LIMIT = 56 * 1024 * 1024
FFTCONV_VMEM_LIMIT = 59 * 1024 * 1024

PF_ZA = 0
PF_VM, PF_OM, PF_ZM, PF_ZH = 2, 3, 4, 5
LOC_QK, LOC_HY, LOC_GM, LOC_COLS = 0, 512, 1280, 1408
HALO = 8


def _dot(a, b):
    return jnp.dot(a, b, preferred_element_type=F32)


def _dot_nt(a, b):
    return lax.dot_general(a, b, (((1,), (1,)), ((), ())), preferred_element_type=F32)


def _dot_tn(a, b):
    return lax.dot_general(a, b, (((0,), (0,)), ((), ())), preferred_element_type=F32)


def _split3(x):
    h = x.astype(BF16)
    r = x - h.astype(F32)
    m = r.astype(BF16)
    l = (r - m.astype(F32)).astype(BF16)
    return h, m, l


def _dot_f32(a, b):
    ah, am, al = _split3(a)
    bh, bm, bl = _split3(b)
    return (_dot(ah, bh) + (_dot(ah, bm) + _dot(am, bh))
            + (_dot(ah, bl) + _dot(al, bh) + _dot(am, bm)))


def _dot_2x(a, b):
    ah, am, _ = _split3(a)
    bh, bm, _ = _split3(b)
    return _dot(ah, bh) + (_dot(ah, bm) + _dot(am, bh))


def _silu(x):
    return x * jax.nn.sigmoid(x)


def _params(sem, vmem=VMEM_LIMIT):
    return pltpu.CompilerParams(dimension_semantics=sem, vmem_limit_bytes=vmem)


def _mod_kernel(c_ref, w_ref, b_ref, o_ref):
    o_ref[...] = _dot_f32(_silu(c_ref[...]), w_ref[...]) + b_ref[...]


def _modulation(cc, w_ada, b_ada):
    n = w_ada.shape[1]
    bn = 512
    return pl.pallas_call(
        _mod_kernel, grid=(n // bn,),
        in_specs=[pl.BlockSpec(cc.shape, lambda j: (0, 0)),
                  pl.BlockSpec((w_ada.shape[0], bn), lambda j: (0, j)),
                  pl.BlockSpec((1, bn), lambda j: (0, j))],
        out_specs=pl.BlockSpec((cc.shape[0], bn), lambda j: (0, j)),
        out_shape=jax.ShapeDtypeStruct((cc.shape[0], n), F32),
        compiler_params=_params(("arbitrary",)), name="adaln_mod",
    )(cc, w_ada, b_ada.reshape(1, n))


def _proj_kernel(nt, x_ref, c_ref, xp_ref, xn_ref, mod_ref, g_ref, wna_ref, wpr_ref, wlo_ref, cml_ref, chy_ref,
                 bif_ref, cos_ref, sin_ref, tri_ref, ona_ref, opr_ref, oqk_ref, ohv_ref, og_ref, loc_ref):
    t = pl.program_id(0)
    x = jnp.where(t == nt, c_ref[0], x_ref[0])
    xs = jnp.concatenate([xp_ref[0], x, xn_ref[0]], axis=0)
    ms = jnp.mean(xs * xs, axis=-1, keepdims=True)
    y = xs * lax.rsqrt(ms + EPS) * g_ref[...]
    mod = mod_ref[0]
    h = (y * (1.0 + mod[1:2]) + mod[0:1]).astype(BF16)
    hm = h[HALO:HALO + TT]
    loc_ref[...] = _dot(h, wlo_ref[...])
    ona_ref[0] = _dot(hm, wna_ref[...]).astype(BF16)

    pm = jnp.where(jnp.logical_and(t != 0, t != nt), 1.0, 0.0)
    nm = jnp.where(jnp.logical_and(t != nt - 1, t != nt), 1.0, 0.0)
    row8 = lax.broadcasted_iota(jnp.int32, (8, 1), 0)
    nchunk = TT // PRE_ROWS

    def conv3(cols, w, c):
        r0 = HALO + c * PRE_ROWS
        u = loc_ref[r0:r0 + PRE_ROWS, cols]
        up = loc_ref[r0 - 1:r0, cols]
        un = loc_ref[r0 + PRE_ROWS:r0 + PRE_ROWS + 1, cols]
        if c == 0:
            up = up * pm
        if c == nchunk - 1:
            un = un * nm
        rp = pltpu.roll(u, 1, 0)
        rn = pltpu.roll(u, PRE_ROWS - 1, 0)
        prev = jnp.concatenate([jnp.where(row8 == 0, up, rp[0:8]), rp[8:]], axis=0)
        nxt = jnp.concatenate([rn[:PRE_ROWS - 8], jnp.where(row8 == 7, un, rn[PRE_ROWS - 8:])], axis=0)
        return prev * w[0:1] + u * w[1:2] + nxt * w[2:3]

    lane = lax.broadcasted_iota(jnp.int32, (1, 2 * D_ML), 1)
    first = (lane % 32) < 16
    for c in range(nchunk):
        rows = slice(c * PRE_ROWS, (c + 1) * PRE_ROWS)
        ohv_ref[0, rows] = conv3(slice(LOC_HY, LOC_GM), chy_ref[...], c).astype(BF16)
        x = _silu(conv3(slice(LOC_QK, LOC_HY), cml_ref[...], c))
        partner = jnp.where(first, pltpu.roll(x, 2 * D_ML - 16, 1), pltpu.roll(x, 16, 1))
        oqk_ref[0, rows] = (x * cos_ref[rows] + partner * sin_ref[rows]).astype(BF16)

    g = loc_ref[HALO:HALO + TT, LOC_GM:LOC_COLS] + bif_ref[...]
    gl = lax.broadcasted_iota(jnp.int32, (1, 128), 1)
    is_f = jnp.logical_and((gl % 8) >= 4, gl < 16)
    lf = jnp.where(is_f, jnp.minimum(g, 0.0) - jnp.log1p(jnp.exp(-jnp.abs(g))), 0.0)
    hi, mid, lo = _split3(lf)
    packed = (hi.astype(F32) + pltpu.roll(mid.astype(F32), 16, 1) + pltpu.roll(lo.astype(F32), 32, 1)).astype(BF16)
    cs = _dot(tri_ref[...], packed)
    cs = cs + pltpu.roll(cs, 112, 1) + pltpu.roll(cs, 96, 1)
    bcum = jnp.where(gl < 8, cs[0:TT], cs[TT:2 * TT])
    a = g - pltpu.roll(bcum, 128 - H_ML, 1)
    ab = jnp.where(is_f, bcum, a)
    keep = gl < 2 * H_ML
    og_ref[0, 0] = jnp.where(keep, ab, 0.0)
    og_ref[0, 1] = jnp.where(keep, pltpu.roll(ab, 120, 1), 0.0)
    opr_ref[0] = _dot(hm, wpr_ref[...]).astype(BF16)


def _projection(x_lat, x_ctx, ctx_tile, mod3, g_pre, wna, wpr, wlo, conv_ml, conv_hy, bif_row, cos_t, sin_t,
                tri, nt):
    nb, lat_rows, d = x_lat.shape
    ltot = (nt + 1) * TT
    ctx_row = nb
    last8 = lat_rows // HALO - 1
    r8 = TT // HALO

    def mod_idx(t, b):
        return (jnp.where(t == nt, ctx_row, b), 0, 0)

    const = lambda t, b: (0, 0)
    big = pl.Buffered(1)
    tok = lambda t, b: (b, t, 0)
    return pl.pallas_call(
        functools.partial(_proj_kernel, nt), grid=(nt + 1, nb),
        in_specs=[pl.BlockSpec((1, TT, d), lambda t, b: (b, jnp.minimum(t, nt - 1), 0)),
                  pl.BlockSpec((1, TT, d), lambda t, b: (b, ctx_tile, 0)),
                  pl.BlockSpec((1, HALO, d), lambda t, b: (b, jnp.clip(t * r8 - 1, 0, last8), 0)),
                  pl.BlockSpec((1, HALO, d), lambda t, b: (b, jnp.minimum((t + 1) * r8, last8), 0)),
                  pl.BlockSpec((1, 3, d), mod_idx),
                  pl.BlockSpec((1, d), const),
                  pl.BlockSpec(wna.shape, const, pipeline_mode=big),
                  pl.BlockSpec(wpr.shape, const, pipeline_mode=big),
                  pl.BlockSpec(wlo.shape, const, pipeline_mode=big),
                  pl.BlockSpec((3, 512), const), pl.BlockSpec((3, 768), const), pl.BlockSpec((1, 128), const),
                  pl.BlockSpec((TT, 512), lambda t, b: (t, 0)), pl.BlockSpec((TT, 512), lambda t, b: (t, 0)),
                  pl.BlockSpec((2 * TT, TT), const)],
        out_specs=[pl.BlockSpec((1, TT, wna.shape[1]), tok), pl.BlockSpec((1, TT, wpr.shape[1]), tok),
                   pl.BlockSpec((1, TT, 512), tok), pl.BlockSpec((1, TT, 768), tok),
                   pl.BlockSpec((1, 2, TT, 128), lambda t, b: (b, 0, t, 0))],
        out_shape=[jax.ShapeDtypeStruct((nb, ltot, wna.shape[1]), BF16),
                   jax.ShapeDtypeStruct((nb, ltot, wpr.shape[1]), BF16),
                   jax.ShapeDtypeStruct((nb, ltot, 512), BF16),
                   jax.ShapeDtypeStruct((nb, ltot, 768), BF16),
                   jax.ShapeDtypeStruct((nb, 2, ltot, 128), F32)],
        scratch_shapes=[pltpu.VMEM((TT + 2 * HALO, LOC_COLS), F32)],
        compiler_params=_params(("arbitrary", "arbitrary")), name="norm_in_proj_conv",
    )(x_lat, x_ctx, x_lat, x_lat, mod3, g_pre.reshape(1, d), wna, wpr, wlo, conv_ml, conv_hy, bif_row,
      cos_t, sin_t, tri)


def _na_kernel(q_ref, k0_ref, k1_ref, k2_ref, v0_ref, v1_ref, v2_ref, kc_ref, vc_ref, za_ref, tab_ref, o_ref):
    lane = lax.broadcasted_iota(jnp.int32, (1, 128), 1)
    k_refs = (k0_ref, k1_ref, k2_ref)
    v_refs = (v0_ref, v1_ref, v2_ref)

    def row_max(s):
        return jnp.max(jnp.maximum(s[:, :128], s[:, 128:]), axis=-1, keepdims=True)

    def head_mask(h):
        return (lane >= HEAD_DIM) if h % 2 else (lane < HEAD_DIM)

    def masked_q(h):
        sl = slice((h // 2) * 128, (h // 2 + 1) * 128)
        q2 = q_ref[0, :, sl]
        return jnp.where(head_mask(h), q2, jnp.zeros_like(q2))

    def score_block(h, qm, blk):
        sl = slice((h // 2) * 128, (h // 2 + 1) * 128)
        if blk == 0:
            return _dot_nt(qm, kc_ref[0, :, sl])
        return _dot_nt(qm, k_refs[blk - 1][0, :, sl]) + tab_ref[0, h, :, (blk - 1) * TT:blk * TT]

    def values(h, p):
        sl = slice((h // 2) * 128, (h // 2 + 1) * 128)
        hm = head_mask(h)
        v = jnp.concatenate([vc_ref[0, :, sl]] + [r[0, :, sl] for r in v_refs], axis=0)
        acc = _dot(p, jnp.where(hm, v, jnp.ones_like(v)))
        return jnp.where(hm, acc / pltpu.roll(acc, HEAD_DIM, 1), 0.0)

    outs = {}
    qm = masked_q(0)
    sc = [score_block(0, qm, blk) for blk in range(4)]
    for h in range(H_NA):
        m = row_max(jnp.maximum(jnp.maximum(sc[0], sc[1]), jnp.maximum(sc[2], sc[3])))
        qm = masked_q(h + 1) if h + 1 < H_NA else None
        nxt, ps = [], []
        for blk in range(4):
            if qm is not None:
                nxt.append(score_block(h + 1, qm, blk))
            ps.append(jnp.exp2(sc[blk] - m).astype(BF16))
        sc = nxt
        outs[h] = values(h, jnp.concatenate(ps, axis=1))
        if h % 2 == 1:
            sl = slice((h // 2) * 128, (h // 2 + 1) * 128)
            o_ref[0, :, sl] = ((outs[h - 1] + outs[h]) * _silu(za_ref[0, :, sl].astype(F32))).astype(BF16)


def _attention(ona, pf, table, nt, with_ctx_queries):
    nb, ltot, _ = ona.shape
    nj = nt + 1 if with_ctx_queries else nt

    def kv(col, dj):
        return lambda j, b: (b, jnp.clip(j + dj, 0, nt - 1), col)

    def pat(j, b):
        p = jnp.where(j == 0, 0, jnp.where(j == nt - 1, 2, jnp.where(j == nt, 3, 1)))
        return (p, 0, 0, 0)

    return pl.pallas_call(
        _na_kernel, grid=(nj, nb),
        in_specs=[pl.BlockSpec((1, TT, 512), lambda j, b: (b, j, 0)),
                  pl.BlockSpec((1, TT, 512), kv(1, -1)), pl.BlockSpec((1, TT, 512), kv(1, 0)),
                  pl.BlockSpec((1, TT, 512), kv(1, 1)),
                  pl.BlockSpec((1, TT, 512), kv(2, -1)), pl.BlockSpec((1, TT, 512), kv(2, 0)),
                  pl.BlockSpec((1, TT, 512), kv(2, 1)),
                  pl.BlockSpec((1, TT, 512), lambda j, b: (b, nt, 1)),
                  pl.BlockSpec((1, TT, 512), lambda j, b: (b, nt, 2)),
                  pl.BlockSpec((1, TT, 512), lambda j, b: (b, j, PF_ZA)),
                  pl.BlockSpec((1, H_NA, TT, 3 * TT), pat)],
        out_specs=pl.BlockSpec((1, TT, 512), lambda j, b: (b, j, 0)),
        out_shape=jax.ShapeDtypeStruct((nb, ltot, 512), BF16),
        compiler_params=_params(("arbitrary", "arbitrary")), name="nbr_attention",
    )(ona, ona, ona, ona, ona, ona, ona, ona, ona, pf, table)


def _na_bias_table(rpb):
    a = np.arange(TT) // GRID_W
    qc = np.arange(TT) % GRID_W
    kidx = np.arange(3 * TT)
    kr = (kidx // TT - 1) * BAND_ROWS + (kidx % TT) // GRID_W
    kcol = kidx % GRID_W
    qstart = np.clip(qc - NA_COLS // 2, 0, GRID_W - NA_COLS)
    col_ok = (kcol[None, :] >= qstart[:, None]) & (kcol[None, :] < qstart[:, None] + NA_COLS)
    coff = np.clip(kcol[None, :] - qc[:, None], -(NA_COLS - 1), NA_COLS - 1) + NA_COLS - 1
    roff = np.clip(kr[None, :] - a[:, None] + NA_ROWS - 1, 0, 2 * NA_ROWS - 2)
    half = NA_ROWS // 2
    row_ok = [
        (kr[None, :] >= 0) & (kr[None, :] < NA_ROWS) & (a[:, None] >= 0),
        (kr[None, :] >= a[:, None] - half) & (kr[None, :] < a[:, None] + half),
        (kr[None, :] >= BAND_ROWS - NA_ROWS) & (kr[None, :] < BAND_ROWS) & (a[:, None] >= 0),
    ]
    nr, ncol = 2 * NA_ROWS - 1, 2 * NA_COLS - 1
    onehot = (coff[:GRID_W, :GRID_W].reshape(1, -1) == np.arange(ncol)[:, None]).astype(np.float32)
    cmat = jnp.dot(rpb.astype(F32).reshape(H_NA * nr, ncol), onehot,
                   precision=lax.Precision.HIGHEST).reshape(H_NA, nr, GRID_W, GRID_W)
    nkr = 3 * BAND_ROWS
    blocks = [jnp.concatenate([cmat[:, roff[ai * GRID_W, kj * GRID_W]] for kj in range(nkr)], axis=-1)
              for ai in range(BAND_ROWS)]
    bias = jnp.concatenate(blocks, axis=1)
    bias = bias * LOG2E
    tabs = [jnp.where((r & col_ok)[None], bias, NEG) for r in row_ok]
    tabs.append(jnp.full_like(bias, NEG))
    return jnp.stack(tabs, axis=0)


def _head_lane_sums(x, lane_h):
    out = jnp.zeros_like(x)
    half = lax.broadcasted_iota(jnp.int32, (1, 128), 1) // HEAD_DIM
    for h in range(H_ML):
        t = x[:, (h // 2) * 128:(h // 2 + 1) * 128]
        sm = jnp.sum(jnp.where(half == h % 2, t, 0.0), axis=-1, keepdims=True)
        out = jnp.where(lane_h == h, sm, out)
    return out


def _scan_kernel(q_ref, k_ref, v_ref, g_ref, e_ref, bm_ref, o_ref, w_ref, n_ref, m_ref):
    s = pl.program_id(2)
    sgn = 1 - 2 * pl.program_id(1)

    @pl.when(s == 0)
    def _():
        w_ref[...] = jnp.zeros_like(w_ref)
        n_ref[...] = jnp.zeros_like(n_ref)
        m_ref[...] = jnp.zeros_like(m_ref)

    lane_h = lax.broadcasted_iota(jnp.int32, (1, D_ML), 1) // HEAD_DIM
    ri = lax.broadcasted_iota(jnp.int32, (TT, TT), 0)
    ci = lax.broadcasted_iota(jnp.int32, (TT, TT), 1)
    tri = ((ci - ri) * sgn) <= 0
    gate_lanes = lax.broadcasted_iota(jnp.int32, (1, 128), 1) < H_ML

    seqs = range(SCAN_NB)
    gate, kws, kwfs, vbs = [], [], [], []
    for i in seqs:
        gi = g_ref[i, 0]
        a = jnp.where(gate_lanes, gi, 0.0)
        bb = jnp.where(gate_lanes, pltpu.roll(gi, 128 - H_ML, 1), 0.0)
        m_prev = m_ref[i]
        mc = jnp.maximum(m_prev, jnp.max(a, axis=0, keepdims=True))
        small = jnp.concatenate([jnp.exp(a - mc), jnp.exp(-bb - mc),
                                 jnp.broadcast_to(jnp.exp(m_prev - mc), (8, 128))], axis=0)
        sh = small.astype(BF16)
        sl = (small - sh.astype(F32)).astype(BF16)
        ex = _dot(sh, e_ref[...]) + _dot(sl, e_ref[...])
        gate.append((ex[TT:2 * TT], ex[2 * TT:2 * TT + 1], jnp.min(bb, axis=0, keepdims=True) + mc))
        kwf = k_ref[i].astype(F32) * ex[0:TT]
        kwfs.append(kwf)
        kws.append(kwf.astype(BF16))
        vbs.append(v_ref[i].astype(BF16))

    scores = [[_dot_nt(jnp.where(lane_h == h, q_ref[i], jnp.zeros_like(q_ref[i])), kws[i]) for h in range(H_ML)]
              for i in seqs]
    inters = [_dot(q_ref[i], w_ref[i].astype(BF16)) for i in seqs]
    updates = [_dot_tn(kws[i], vbs[i]) for i in seqs]

    def masked(i, h):
        sc = jnp.where(tri, scores[i][h], 0.0)
        return sc.astype(BF16), jnp.sum(sc, axis=-1, keepdims=True)

    order = [(i, h) for i in seqs for h in range(H_ML)]
    cur = masked(*order[0])
    intras, dens = {}, {}
    for n, (i, h) in enumerate(order):
        nxt = masked(*order[n + 1]) if n + 1 < len(order) else None
        scb, rs = cur
        part = _dot(scb, jnp.where(lane_h == h, vbs[i], jnp.zeros_like(vbs[i])))
        intras[i] = part if h == 0 else intras[i] + part
        dens[i] = jnp.where(lane_h == h, rs, dens[i] if h else jnp.zeros((TT, D_ML), F32))
        cur = nxt

    for i in seqs:
        eb_e, beta, m_new = gate[i]
        intra, den_intra = intras[i], dens[i]
        num = beta * inters[i] + intra
        den = beta * _head_lane_sums(q_ref[i].astype(F32) * n_ref[i], lane_h) + den_intra
        o_ref[i, 0] = (num / jnp.maximum(jnp.abs(den), eb_e)).astype(BF16)

        w_ref[i] = beta * w_ref[i] + bm_ref[...] * updates[i]
        n_ref[i] = beta * n_ref[i] + jnp.sum(kwfs[i], axis=0, keepdims=True)
        m_ref[i] = m_new


def _mlstm_scan(qk, pf, gates, emat, bmask, nt):
    nb, ltot, _ = qk.shape
    g = SCAN_NB

    def chunk(s, d):
        return jnp.where(s == 0, nt, jnp.where(d == 0, s - 1, nt - s))

    const = lambda b, d, s: (0, 0)
    return pl.pallas_call(
        _scan_kernel, grid=(nb // g, 2, nt + 1),
        in_specs=[pl.BlockSpec((g, TT, 256), lambda b, d, s: (b, chunk(s, d), 0)),
                  pl.BlockSpec((g, TT, 256), lambda b, d, s: (b, chunk(s, d), 1)),
                  pl.BlockSpec((g, TT, 256), lambda b, d, s: (b, chunk(s, d), PF_VM)),
                  pl.BlockSpec((g, 1, TT, 128), lambda b, d, s: (b, d, chunk(s, d), 0)),
                  pl.BlockSpec((128, 256), const), pl.BlockSpec((256, 256), const)],
        out_specs=pl.BlockSpec((g, 1, TT, 256), lambda b, d, s: (b, d, chunk(s, d), 0)),
        out_shape=jax.ShapeDtypeStruct((nb, 2, ltot, 256), BF16),
        scratch_shapes=[pltpu.VMEM((g, D_ML, D_ML), F32), pltpu.VMEM((g, 1, D_ML), F32),
                        pltpu.VMEM((g, 1, 128), F32)],
        compiler_params=_params(("arbitrary", "arbitrary", "arbitrary")), name="mlstm_scan",
    )(qk, qk, pf, gates, emat, bmask)


def _hyfilt_kernel(tl, length, feat_ref, w1_ref, b1_ref, w2_ref, b2_ref, w3_ref, fq_ref, dec_ref, o_ref, ss_ref):
    i = pl.program_id(0)
    th = tl // 2
    a = jnp.sin(fq_ref[0:1] * (_dot_2x(feat_ref[...], w1_ref[...]) + b1_ref[...]))
    a = jnp.sin(fq_ref[1:2] * (_dot_2x(a, w2_ref[...]) + b2_ref[...]))
    part = None
    for half in range(2):
        dec = dec_ref[half * th:(half + 1) * th]
        f = _dot_2x(a, w3_ref[half]) * jnp.concatenate([dec, dec, dec, dec], axis=1)
        fwd = jnp.concatenate([f[:, 0:D_HY], f[:, 2 * D_HY:3 * D_HY]], axis=1)
        bwd = jnp.concatenate([f[:, D_HY:2 * D_HY], f[:, 3 * D_HY:4 * D_HY]], axis=1)
        row = lax.broadcasted_iota(jnp.int32, (th, 1), 0) + (i * tl + half * th)
        k2 = jnp.where(row < length, fwd, jnp.where(row > length, bwd, 0.0))
        o_ref[half * th:(half + 1) * th] = k2
        sq = jnp.sum(k2 * k2, axis=0, keepdims=True)
        part = sq if part is None else part + sq

    @pl.when(i == 0)
    def _():
        ss_ref[...] = jnp.zeros_like(ss_ref)

    ss_ref[...] += part


def _hyena_filters(length, w1, b1, w2, b2, w3, freq):
    t = np.arange(length, dtype=np.float32)
    tn = (t / np.float32(length - 1)).astype(np.float32)
    bands = (HY_EMB - 1) // 2
    fr = np.linspace(1e-4, bands - 1, bands, dtype=np.float32)
    ang = (np.float32(2.0 * math.pi / length) * t[:, None] * fr[None, :]).astype(np.float32)
    feat = np.zeros((length, 128), np.float32)
    feat[:, 0] = tn
    feat[:, 1:1 + bands] = np.cos(ang)
    feat[:, 1 + bands:1 + 2 * bands] = -np.sin(ang)
    deltas = np.abs(np.linspace(math.log(HY_TARGET) / HY_SLOW, math.log(HY_TARGET) / HY_FAST, D_HY, dtype=np.float32))
    dec = np.exp(-tn[:, None] * deltas[None, :]).astype(np.float32)
    lag = np.concatenate([np.arange(length), [0], np.arange(length - 1, 0, -1)])
    feat, dec = feat[lag], dec[lag]

    tl = 512
    hw = 64
    fp = feat.reshape(2 * length // tl, 2, tl // 2, 128)[..., :hw]
    feat = np.concatenate([fp[:, 0], fp[:, 1]], axis=-1).reshape(length, 128)
    pad2 = lambda m, r, c: jnp.zeros((r, c), F32).at[:m.shape[0], :m.shape[1]].set(m.astype(F32))
    bdiag = lambda m: jnp.zeros((128, 128), F32).at[:hw, :hw].set(pad2(m, hw, hw)).at[hw:, hw:].set(pad2(m, hw, hw))
    twice = lambda v: jnp.concatenate([pad2(v, v.shape[0], hw)] * 2, axis=1)
    w1p = bdiag(w1)
    w2p = bdiag(w2)
    w3h = pad2(w3, hw, 4 * D_HY)
    w3p = jnp.stack([jnp.concatenate([w3h, jnp.zeros_like(w3h)], axis=0),
                     jnp.concatenate([jnp.zeros_like(w3h), w3h], axis=0)])
    b1p = twice(b1[None])
    b2p = twice(b2[None])
    fqp = twice(freq)
    const = lambda i: (0, 0)
    return pl.pallas_call(
        functools.partial(_hyfilt_kernel, tl, length), grid=(2 * length // tl,),
        in_specs=[pl.BlockSpec((tl // 2, 128), lambda i: (i, 0)),
                  pl.BlockSpec((128, 128), const), pl.BlockSpec((1, 128), const),
                  pl.BlockSpec((128, 128), const), pl.BlockSpec((1, 128), const),
                  pl.BlockSpec((2, 128, 4 * D_HY), lambda i: (0, 0, 0)), pl.BlockSpec((2, 128), const),
                  pl.BlockSpec((tl, D_HY), lambda i: (i, 0))],
        out_specs=[pl.BlockSpec((tl, 2 * D_HY), lambda i: (i, 0)), pl.BlockSpec((1, 2 * D_HY), const)],
        out_shape=[jax.ShapeDtypeStruct((2 * length, 2 * D_HY), F32), jax.ShapeDtypeStruct((1, 2 * D_HY), F32)],
        compiler_params=_params(("arbitrary",)), name="hyena_filter_mlp",
    )(feat, w1p, b1p, w2p, b2p, w3p, fqp, dec)


@functools.lru_cache(maxsize=None)
def _fft_tables(length):
    n = 2 * length
    n1 = n // FFT_N2
    nh = n1 // 2
    k1 = np.arange(n1)[None, :, None].astype(np.float64)
    j = np.arange(nh)[None, None, :].astype(np.float64)
    n2 = np.arange(FFT_N2)[:, None, None].astype(np.float64)
    ang = -2.0 * np.pi * k1 * (FFT_N2 * j + n2) / n
    gf = np.concatenate([np.cos(ang), np.sin(ang)], axis=1)
    gi = np.concatenate([np.cos(ang).transpose(0, 2, 1), np.sin(ang).transpose(0, 2, 1)], axis=1)
    kk = np.arange(FFT_N2)[:, None] * np.arange(FFT_N2)[None, :]
    a2 = -2.0 * np.pi * kk / FFT_N2
    f2 = np.concatenate([np.cos(a2), np.sin(a2)], axis=0)
    cv = lambda m: np.asarray(m, np.float32).astype(BF16)
    return cv(gf), cv(gi), cv(f2), n1


def _stage1(n1, z_of, gf_ref, scr_ref, cols):
    def group(gidx, c):
        n2s = [gidx * FFT_UNROLL + u for u in range(FFT_UNROLL)]
        zs = [z_of(n2) for n2 in n2s]
        rs = [_dot(gf_ref[n2], z) for n2, z in zip(n2s, zs)]
        for n2, r in zip(n2s, rs):
            off = pl.multiple_of(n2 * (n1 + FFT_PAD), 8)
            if cols == 256:
                scr_ref[0, pl.ds(off, n1), :] = r[:n1, :128] - r[n1:, 128:]
                scr_ref[1, pl.ds(off, n1), :] = r[n1:, :128] + r[:n1, 128:]
            else:
                scr_ref[0, pl.ds(off, n1), :] = r[:n1]
                scr_ref[1, pl.ds(off, n1), :] = r[n1:]
        return c

    lax.fori_loop(0, FFT_N2 // FFT_UNROLL, group, 0)


def _stage2(n1, k1, f2_ref, scr_ref):
    ar = scr_ref[0, pl.ds(k1, FFT_N2, stride=n1 + FFT_PAD), :]
    ai = scr_ref[1, pl.ds(k1, FFT_N2, stride=n1 + FFT_PAD), :]
    r = _dot(f2_ref[...], jnp.concatenate([ar, ai], axis=1).astype(BF16))
    return r[:128, :128] - r[128:, 128:], r[128:, :128] + r[:128, 128:]


def _hyconv_kernel(n1, kg, z_ref, gf_ref, f2_ref, kf_ref, gi_ref, o_ref, scr_ref, tp_ref):
    s = pl.program_id(2)
    nh = n1 // 2
    pitch = FFT_N2 + FFT_PAD

    @pl.when(s == 0)
    def _():
        for e in range(2):
            for j in range(nh):
                tp_ref[e, j * pitch:j * pitch + FFT_N2, :] = z_ref[0, e, j * FFT_N2:(j + 1) * FFT_N2, :].astype(F32)

        def z_of(n2):
            zr = tp_ref[0, pl.ds(n2, nh, stride=pitch), :]
            zi = tp_ref[1, pl.ds(n2, nh, stride=pitch), :]
            return jnp.concatenate([zr, zi], axis=1).astype(BF16)
        _stage1(n1, z_of, gf_ref, scr_ref, 256)

    def k1_group(gidx, c):
        idx = [gidx * FFT_UNROLL + u for u in range(FFT_UNROLL)]
        xs = [_stage2(n1, s * kg + i, f2_ref, scr_ref) for i in idx]
        ys = []
        for i, (xr, xi) in zip(idx, xs):
            off = pl.multiple_of(i * FFT_N2, FFT_N2)
            kr = kf_ref[0, 0, pl.ds(off, FFT_N2), :]
            ki = kf_ref[0, 1, pl.ds(off, FFT_N2), :]
            ys.append(jnp.concatenate([xr * kr - xi * ki, xr * ki + xi * kr], axis=1).astype(BF16))
        rs = [_dot(f2_ref[...], yc) for yc in ys]
        for i, r in zip(idx, rs):
            k1 = s * kg + i
            scr_ref[0, pl.ds(k1, FFT_N2, stride=n1 + FFT_PAD), :] = r[:128, :128] + r[128:, 128:]
            scr_ref[1, pl.ds(k1, FFT_N2, stride=n1 + FFT_PAD), :] = r[:128, 128:] - r[128:, :128]
        return c

    lax.fori_loop(0, kg // FFT_UNROLL, k1_group, 0)

    @pl.when(s == pl.num_programs(2) - 1)
    def _():
        def group(gidx, c):
            n2s = [gidx * FFT_UNROLL + u for u in range(FFT_UNROLL)]
            bcs = []
            for n2 in n2s:
                off = pl.multiple_of(n2 * (n1 + FFT_PAD), 8)
                bcs.append(jnp.concatenate([scr_ref[0, pl.ds(off, n1), :], scr_ref[1, pl.ds(off, n1), :]],
                                           axis=1).astype(BF16))
            rs = [_dot(gi_ref[n2], bc) for n2, bc in zip(n2s, bcs)]
            for n2, r in zip(n2s, rs):
                tp_ref[0, pl.ds(n2, nh, stride=pitch), :] = r[:nh, :128] + r[nh:, 128:]
                tp_ref[1, pl.ds(n2, nh, stride=pitch), :] = r[:nh, 128:] - r[nh:, :128]
            return c
        lax.fori_loop(0, FFT_N2 // FFT_UNROLL, group, 0)
        for e in range(2):
            for j in range(nh):
                o_ref[0, e, j * FFT_N2:(j + 1) * FFT_N2, :] = tp_ref[e, j * pitch:j * pitch + FFT_N2, :].astype(BF16)


def _hyena_conv(z4, col0, kf, order, length):
    gf, gi, f2, n1 = _fft_tables(length)
    n = 2 * length
    kg = min(n1, 2 * FFT_UNROLL)
    npair = z4.shape[0]
    big = pl.Buffered(1)
    return pl.pallas_call(
        functools.partial(_hyconv_kernel, n1, kg), grid=(npair, 2, n1 // kg),
        in_specs=[pl.BlockSpec((1, 2, length, 128), lambda p, c, s: (p, 0, 0, col0 + c)),
                  pl.BlockSpec(gf.shape, lambda p, c, s: (0, 0, 0), pipeline_mode=big),
                  pl.BlockSpec(f2.shape, lambda p, c, s: (0, 0)),
                  pl.BlockSpec((1, 2, kg * FFT_N2, 128), lambda p, c, s: (order, 0, s, c)),
                  pl.BlockSpec(gi.shape, lambda p, c, s: (0, 0, 0), pipeline_mode=big)],
        out_specs=pl.BlockSpec((1, 2, length, 128), lambda p, c, s: (p, 0, 0, c)),
        out_shape=jax.ShapeDtypeStruct((npair, 2, length, 256), BF16),
        scratch_shapes=[pltpu.VMEM((2, FFT_N2 * (n1 + FFT_PAD), 128), F32),
                        pltpu.VMEM((2, (n1 // 2) * (FFT_N2 + FFT_PAD), 128), F32)],
        compiler_params=_params(("arbitrary", "arbitrary", "arbitrary"), FFTCONV_VMEM_LIMIT), name="hyena_fftconv",
    )(z4, gf, f2, kf, gi)


def _hyfft_kernel(n1, kg, z_ref, gf_ref, f2_ref, o_ref, scr_ref, tp_ref):
    s = pl.program_id(2)
    pitch = FFT_N2 + FFT_PAD

    @pl.when(s == 0)
    def _():
        for j in range(n1):
            tp_ref[j * pitch:j * pitch + FFT_N2, :] = z_ref[j * FFT_N2:(j + 1) * FFT_N2, :]
        _stage1(n1, lambda n2: tp_ref[pl.ds(n2, n1, stride=pitch), :].astype(BF16), gf_ref, scr_ref, 128)

    def k1_group(gidx, c):
        idx = [gidx * FFT_UNROLL + u for u in range(FFT_UNROLL)]
        xs = [_stage2(n1, s * kg + i, f2_ref, scr_ref) for i in idx]
        for i, (xr, xi) in zip(idx, xs):
            off = pl.multiple_of(i * FFT_N2, FFT_N2)
            o_ref[0, 0, pl.ds(off, FFT_N2), :] = xr
            o_ref[0, 1, pl.ds(off, FFT_N2), :] = xi
        return c

    lax.fori_loop(0, kg // FFT_UNROLL, k1_group, 0)


@functools.lru_cache(maxsize=None)
def _fft_table_full(length):
    n = 2 * length
    n1 = n // FFT_N2
    k1 = np.arange(n1)[None, :, None].astype(np.float64)
    j = np.arange(n1)[None, None, :].astype(np.float64)
    n2 = np.arange(FFT_N2)[:, None, None].astype(np.float64)
    ang = -2.0 * np.pi * k1 * (FFT_N2 * j + n2) / n
    return np.concatenate([np.cos(ang), np.sin(ang)], axis=1).astype(np.float32).astype(BF16)


def _filter_spectrum(k2, length):
    _, _, f2, n1 = _fft_tables(length)
    gf = _fft_table_full(length)
    n = 2 * length
    kg = min(n1, 16)
    nf = k2.shape[1] // D_HY
    big = pl.Buffered(1)
    return pl.pallas_call(
        functools.partial(_hyfft_kernel, n1, kg), grid=(nf, 2, n1 // kg),
        in_specs=[pl.BlockSpec((n, 128), lambda f, c, s: (0, 2 * f + c), pipeline_mode=big),
                  pl.BlockSpec(gf.shape, lambda f, c, s: (0, 0, 0), pipeline_mode=big),
                  pl.BlockSpec(f2.shape, lambda f, c, s: (0, 0))],
        out_specs=pl.BlockSpec((1, 2, kg * FFT_N2, 128), lambda f, c, s: (f, 0, s, c)),
        out_shape=jax.ShapeDtypeStruct((nf, 2, n, 256), F32),
        scratch_shapes=[pltpu.VMEM((2, FFT_N2 * (n1 + FFT_PAD), 128), F32),
                        pltpu.VMEM((n1 * (FFT_N2 + FFT_PAD), 128), F32)],
        compiler_params=_params(("arbitrary", "arbitrary", "arbitrary")), name="hyena_filter_fft",
    )(k2, gf, f2)


def _normalised(k2, ssq, extra_scale):
    return k2 * (lax.rsqrt(ssq + EPS) * extra_scale)


def _hy_mid_kernel(y_ref, hv_ref, b_ref, o_ref):
    v = hv_ref[0][:, 0:D_HY].astype(F32)
    x1 = hv_ref[0][:, D_HY:2 * D_HY].astype(F32)
    o_ref[0] = (x1 * (y_ref[0].astype(F32) + v * b_ref[...])).astype(BF16)


def _hy_mid(y1, hv, bias0):
    nb, length, _ = y1.shape
    te = 4 * TT
    return pl.pallas_call(
        _hy_mid_kernel, grid=(nb, length // te),
        in_specs=[pl.BlockSpec((1, te, D_HY), lambda b, t: (b, t, 0)),
                  pl.BlockSpec((1, te, 512), lambda b, t: (b, t, 0)),
                  pl.BlockSpec((1, D_HY), lambda b, t: (0, 0))],
        out_specs=pl.BlockSpec((1, te, D_HY), lambda b, t: (b, t, 0)),
        out_shape=jax.ShapeDtypeStruct(y1.shape, BF16),
        compiler_params=_params(("arbitrary", "arbitrary")), name="hyena_mid",
    )(y1, hv, bias0.reshape(1, D_HY))


def _hy_ctx_kernel(lc, hv_ref, zh_ref, k2_ref, b_ref, fc_ref, ff_ref, ci_ref, o_ref):
    nf = 2 * lc
    hv = hv_ref[0].astype(F32)

    def conv(z, o):
        kf = _dot_exact_rhs(ff_ref[...], k2_ref[:, o * D_HY:(o + 1) * D_HY])
        zs = _dot(fc_ref[...], z.astype(BF16))
        zr, zi = zs[:nf], zs[nf:]
        kr, ki = kf[:nf], kf[nf:]
        ys = jnp.concatenate([zr * kr - zi * ki, zr * ki + zi * kr], axis=0).astype(BF16)
        return _dot(ci_ref[...], ys) * (1.0 / nf)

    v = hv[:, 0:D_HY]
    x1 = hv[:, D_HY:2 * D_HY]
    x2 = hv[:, 2 * D_HY:3 * D_HY]
    z1 = x1 * (conv(v, 0) + v * b_ref[0:1])
    p = x2 * (conv(z1, 1) + z1 * b_ref[1:2])
    o_ref[0] = (p * _silu(zh_ref[0].astype(F32))).astype(BF16)


def _dot_exact_rhs(a_bf16, b):
    bh, bm, bl = _split3(b)
    return _dot(a_bf16, bh) + _dot(a_bf16, bm) + _dot(a_bf16, bl)


@functools.lru_cache(maxsize=None)
def _ctx_dft_tables(lc):
    nf = 2 * lc
    k = np.arange(nf)[:, None].astype(np.float64)
    ang = -2.0 * np.pi * k * np.arange(nf)[None, :] / nf
    ff = np.concatenate([np.cos(ang), np.sin(ang)], axis=0)
    fc = ff[:, :lc]
    a2 = 2.0 * np.pi * np.arange(lc)[:, None] * np.arange(nf)[None, :] / nf
    ci = np.concatenate([np.cos(a2), -np.sin(a2)], axis=1)
    cv = lambda m: np.asarray(m, np.float32).astype(BF16)
    return cv(fc), cv(ff), cv(ci)


def _hy_ctx(hv, pf, k2c, hy_bias, nt, lc):
    nb = hv.shape[0]
    fc, ff, ci = _ctx_dft_tables(lc)
    c2 = lambda b: (0, 0)
    return pl.pallas_call(
        functools.partial(_hy_ctx_kernel, lc), grid=(nb,),
        in_specs=[pl.BlockSpec((1, lc, 768), lambda b: (b, nt, 0)),
                  pl.BlockSpec((1, lc, D_HY), lambda b: (b, nt, PF_ZH)),
                  pl.BlockSpec(k2c.shape, c2),
                  pl.BlockSpec((2, D_HY), c2),
                  pl.BlockSpec(fc.shape, c2), pl.BlockSpec(ff.shape, c2), pl.BlockSpec(ci.shape, c2)],
        out_specs=pl.BlockSpec((1, lc, D_HY), lambda b: (b, 0, 0)),
        out_shape=jax.ShapeDtypeStruct((nb, lc, D_HY), BF16),
        compiler_params=_params(("arbitrary",)), name="hyena_ctx",
    )(hv, pf, k2c, hy_bias, fc, ff, ci)


def _out_kernel(n_hy, na_ref, h0_ref, h1_ref, om_ref, zm_ref, w_ref, x_ref, mod_ref, g_ref, *rest):
    hy_refs, o_ref = rest[:n_hy], rest[n_hy]
    chunks = [slice(r, r + TT) for r in range(0, o_ref.shape[1], TT)]
    acts = []
    for rows in chunks:
        if n_hy == 1:
            hy = hy_refs[0][0, rows]
        else:
            y_ref, z_ref, x2_ref, zh_ref, b_ref = hy_refs
            z = z_ref[0, rows].astype(F32)
            hy = (x2_ref[0, rows].astype(F32) * (y_ref[0, rows].astype(F32) + z * b_ref[...])
                  * _silu(zh_ref[0, rows].astype(F32))).astype(BF16)
        hsum = h0_ref[0, 0, rows].astype(F32) + h1_ref[0, 0, rows].astype(F32)
        ml = (jax.nn.sigmoid(om_ref[0, rows].astype(F32)) * hsum * _silu(zm_ref[0, rows].astype(F32))).astype(BF16)
        acts.append((ml, hy))
    ys = [_dot(na_ref[0, rows], w_ref[0:D_NA]) + _dot(ml, w_ref[D_NA:D_NA + D_ML])
          + _dot(hy, w_ref[D_NA + D_ML:D_MODEL]) for rows, (ml, hy) in zip(chunks, acts)]
    for rows, y in zip(chunks, ys):
        ms = jnp.mean(y * y, axis=-1, keepdims=True)
        r = y * lax.rsqrt(ms + EPS) * g_ref[...]
        o_ref[0, rows] = x_ref[0, rows] + mod_ref[0][2:3] * r


def _out_projection(na_g, hml, pf, hy_args, w_out, res, res_t0, mod3, g_post, to, t0, ntiles, mod_row, out_rows,
                    prev=None):
    nb = na_g.shape[0]
    d = D_MODEL
    tok = lambda col: (lambda b, t: (b, t + t0, col))
    own = lambda col: (lambda b, t: (b, t, col))
    in_specs = [pl.BlockSpec((1, to, 512), tok(0)),
                pl.BlockSpec((1, 1, to, 256), lambda b, t: (b, 0, t + t0, 0)),
                pl.BlockSpec((1, 1, to, 256), lambda b, t: (b, 1, t + t0, 0)),
                pl.BlockSpec((1, to, 256), tok(PF_OM)), pl.BlockSpec((1, to, 256), tok(PF_ZM)),
                pl.BlockSpec((d, d), lambda b, t: (0, 0), pipeline_mode=pl.Buffered(1)),
                pl.BlockSpec((1, to, d), lambda b, t: (b, t + res_t0, 0)),
                pl.BlockSpec((1, 3, d), lambda b, t: (mod_row(b), 0, 0)),
                pl.BlockSpec((1, d), lambda b, t: (0, 0))]
    args = [na_g, hml, hml, pf, pf, w_out, res, mod3, g_post.reshape(1, d)]
    if len(hy_args) == 1:
        in_specs.append(pl.BlockSpec((1, to, D_HY), own(0)))
        args.append(hy_args[0])
    else:
        y2, z1, hv, pf_, bias1 = hy_args
        in_specs += [pl.BlockSpec((1, to, D_HY), own(0)), pl.BlockSpec((1, to, D_HY), own(0)),
                     pl.BlockSpec((1, to, D_HY), tok(2)), pl.BlockSpec((1, to, D_HY), tok(PF_ZH)),
                     pl.BlockSpec((1, D_HY), lambda b, t: (0, 0))]
        args += [y2, z1, hv, pf_, bias1.reshape(1, D_HY)]
    n_hy = len(hy_args)
    aliases = {}
    if prev is not None:
        in_specs.append(pl.BlockSpec(memory_space=pl.ANY))
        args.append(prev)
        aliases = {len(args) - 1: 0}
        kern = lambda *r: _out_kernel(n_hy, *r[:9 + n_hy], r[10 + n_hy])
    else:
        kern = functools.partial(_out_kernel, n_hy)
    return pl.pallas_call(
        kern, grid=(nb, ntiles), in_specs=in_specs,
        out_specs=pl.BlockSpec((1, to, d), tok(0)),
        out_shape=jax.ShapeDtypeStruct((nb, out_rows, d), F32),
        input_output_aliases=aliases,
        compiler_params=_params(("arbitrary", "arbitrary")), name="out_proj_norm_residual",
    )(*args)


@functools.lru_cache(maxsize=None)
def _rope_tables(length, lc):
    n = HEAD_DIM // 4
    t = np.arange(length)
    inv = (ROPE_BASE ** (-np.arange(n, dtype=np.float32) / n)).astype(np.float32)
    pos = np.stack([t // GRID_W, t % GRID_W], axis=-1).astype(np.float32)
    ang = (pos[:, :, None] * inv).astype(np.float32)
    cos_h = np.concatenate([np.cos(ang[:, 0]), np.cos(ang[:, 0]), np.cos(ang[:, 1]), np.cos(ang[:, 1])], axis=-1)
    sin_h = np.concatenate([-np.sin(ang[:, 0]), np.sin(ang[:, 0]), -np.sin(ang[:, 1]), np.sin(ang[:, 1])], axis=-1)
    cos_t = np.concatenate([np.tile(cos_h, (1, 2 * H_ML)), np.ones((lc, 512), np.float32)], axis=0)
    sin_t = np.concatenate([np.tile(sin_h, (1, 2 * H_ML)), np.zeros((lc, 512), np.float32)], axis=0)
    kscale = np.concatenate([np.ones(256, np.float32), np.full(256, HEAD_DIM ** -0.5, np.float32)])
    return (cos_t * kscale).astype(np.float32), (sin_t * kscale).astype(np.float32)


@functools.lru_cache(maxsize=None)
def _scan_constants():
    tril = np.tril(np.ones((TT, TT), np.float32))
    emat = np.zeros((128, D_ML), np.float32)
    for h in range(H_ML):
        emat[h, h * HEAD_DIM:(h + 1) * HEAD_DIM] = 1.0
    rh = np.arange(D_ML)[:, None] // HEAD_DIM
    chd = np.arange(D_ML)[None, :] // HEAD_DIM
    bmask = (rh == chd).astype(np.float32)
    tri = np.concatenate([tril, tril.T], axis=0)
    return tri.astype(BF16), emat.astype(BF16), bmask


def _layer(x_lat, x_ctx, ctx_tile, mod3, nb, length, lc, g_pre, g_post, w_in, b_if, conv_ml, conv_hy, rpb,
           hf_w1, hf_b1, hf_w2, hf_b2, hf_w3, hf_freq, hy_bias, w_out, update_ctx):
    nt = length // TT
    ltot = length + lc
    tri, emat, bmask = _scan_constants()
    cos_t, sin_t = _rope_tables(length, lc)

    c = lambda a, b: w_in[:, a:b]
    wna = jnp.concatenate([c(0, 512) * (HEAD_DIM ** -0.5 * LOG2E), c(512, 1536)], axis=1).astype(BF16)
    wpr = jnp.concatenate([c(1536, 2048), c(2560, 2816), c(2816, 3072), c(3072, 3328), c(4112, 4368)],
                          axis=1).astype(BF16)
    wlo = jnp.concatenate([c(2048, 2560), c(3344, 4112), c(3328, 3344), jnp.zeros((D_MODEL, 112), F32)],
                          axis=1).astype(BF16)
    bif_row = jnp.zeros((1, 128), F32).at[0, :4 * H_ML].set(b_if.reshape(-1))
    ona, pf, qk, hv, gates = _projection(x_lat, x_ctx, ctx_tile, mod3, g_pre, wna, wpr, wlo, conv_ml, conv_hy,
                                         bif_row, cos_t, sin_t, tri, nt)

    na_g = _attention(ona, pf, _na_bias_table(rpb), nt, update_ctx)

    hml = _mlstm_scan(qk, pf, gates, emat, bmask, nt)

    k2, ssq = _hyena_filters(length, hf_w1, hf_b1, hf_w2, hf_b2, hf_w3, hf_freq)
    kf = _filter_spectrum(_normalised(k2, ssq, 1.0 / (2 * length)), length)
    hv4 = hv.reshape(nb // 2, 2, ltot, 768)
    y1 = _hyena_conv(hv4, 0, kf, 0, length).reshape(nb, length, D_HY)
    z1 = _hy_mid(y1, hv, hy_bias[0])
    y2 = _hyena_conv(z1.reshape(nb // 2, 2, length, D_HY), 0, kf, 1, length).reshape(nb, length, D_HY)
    hy_args = (y2, z1, hv, pf, hy_bias[1])
    to = 2 * TT
    if not update_ctx:
        return _out_projection(na_g, hml, pf, hy_args, w_out, x_lat, 0, mod3, g_post, to, 0, length // to,
                               lambda b: b, length)

    xn = _out_projection(na_g, hml, pf, hy_args, w_out, x_lat, 0, mod3, g_post, to, 0, length // to,
                         lambda b: b, ltot)
    k2c, ssq_c = _hyena_filters(lc, hf_w1, hf_b1, hf_w2, hf_b2, hf_w3, hf_freq)
    hyc_g = _hy_ctx(hv, pf, _normalised(k2c, ssq_c, 1.0), hy_bias, nt, lc)
    return _out_projection(na_g, hml, pf, (hyc_g,), w_out, x_ctx, ctx_tile, mod3, g_post, TT, nt, 1,
                           lambda b: nb, ltot, prev=xn)


def kernel(x, c, ctx, c_ctx, w_ada, b_ada, g_pre, g_post, w_in, b_if, conv_ml, conv_hy, rpb, hf_w1, hf_b1,
           hf_w2, hf_b2, hf_w3, hf_freq, hy_bias, w_out):
    nb, length, d = x.shape
    lc = ctx.shape[1]
    depth = w_in.shape[0]
    assert lc == TT and length % (4 * TT) == 0 and nb % 2 == 0 and nb % SCAN_NB == 0 and d == D_MODEL
    cc = jnp.zeros((16, d), F32).at[:nb].set(c).at[nb].set(c_ctx)
    x_lat, x_ctx, ctx_tile = x, ctx, 0
    for l in range(depth):
        mod3 = _modulation(cc, w_ada[l], b_ada[l]).reshape(16, 3, d)
        x_lat = _layer(x_lat, x_ctx, ctx_tile, mod3, nb, length, lc, g_pre[l], g_post[l], w_in[l], b_if[l],
                       conv_ml[l], conv_hy[l], rpb[l], hf_w1[l], hf_b1[l], hf_w2[l], hf_b2[l], hf_w3[l],
                       hf_freq[l], hy_bias[l], w_out[l].astype(BF16), l < depth - 1)
        x_ctx, ctx_tile = x_lat, length // TT
    return x_lat
```

```python
import functools
import math

import numpy as np
import jax
import jax.numpy as jnp
from jax import lax
from jax.experimental import pallas as pl
from jax.experimental.pallas import tpu as pltpu

F32 = jnp.float32
BF16 = jnp.bfloat16

D_MODEL = 1024
GRID_W = 64
HEAD_DIM = 64
D_NA = 512
D_ML = 256
D_HY = 256
H_NA = 8
H_ML = 4
NA_ROWS = 8
NA_COLS = 16
HY_EMB = 33
HY_FFN = 64
HY_TARGET = 1e-2
HY_FAST = 0.3
HY_SLOW = 1.5
ROPE_BASE = 10000.0
EPS = 1e-6

TT = 256
BAND_ROWS = TT // GRID_W
SCAN_NB = 4
PRE_ROWS = 64
FFT_N2 = 128
FFT_PAD = 8
FFT_UNROLL = 8
NEG = -1e30
BIG = 3e38
LOG2E = math.log2(math.e)
VMEM_LIMIT = 56 * 1024 * 1024
FFTCONV_VMEM_LIMIT = 59 * 1024 * 1024

PF_ZA = 0
PF_VM, PF_OM, PF_ZM, PF_ZH = 2, 3, 4, 5
LOC_QK, LOC_HY, LOC_GM, LOC_COLS = 0, 512, 1280, 1408
HALO = 8


def _dot(a, b):
    return jnp.dot(a, b, preferred_element_type=F32)


def _dot_nt(a, b):
    return lax.dot_general(a, b, (((1,), (1,)), ((), ())), preferred_element_type=F32)


def _dot_tn(a, b):
    return lax.dot_general(a, b, (((0,), (0,)), ((), ())), preferred_element_type=F32)


def _split3(x):
    h = x.astype(BF16)
    r = x - h.astype(F32)
    m = r.astype(BF16)
    l = (r - m.astype(F32)).astype(BF16)
    return h, m, l


def _dot_f32(a, b):
    ah, am, al = _split3(a)
    bh, bm, bl = _split3(b)
    return (_dot(ah, bh) + (_dot(ah, bm) + _dot(am, bh))
            + (_dot(ah, bl) + _dot(al, bh) + _dot(am, bm)))


def _dot_2x(a, b):
    ah, am, _ = _split3(a)
    bh, bm, _ = _split3(b)
    return _dot(ah, bh) + (_dot(ah, bm) + _dot(am, bh))


def _silu(x):
    return x * jax.nn.sigmoid(x)


def _params(sem, vmem=VMEM_LIMIT):
    return pltpu.CompilerParams(dimension_semantics=sem, vmem_limit_bytes=vmem)


def _mod_kernel(c_ref, w_ref, b_ref, o_ref):
    o_ref[...] = _dot_f32(_silu(c_ref[...]), w_ref[...]) + b_ref[...]


def _modulation(cc, w_ada, b_ada):
    n = w_ada.shape[1]
    bn = 512
    return pl.pallas_call(
        _mod_kernel, grid=(n // bn,),
        in_specs=[pl.BlockSpec(cc.shape, lambda j: (0, 0)),
                  pl.BlockSpec((w_ada.shape[0], bn), lambda j: (0, j)),
                  pl.BlockSpec((1, bn), lambda j: (0, j))],
        out_specs=pl.BlockSpec((cc.shape[0], bn), lambda j: (0, j)),
        out_shape=jax.ShapeDtypeStruct((cc.shape[0], n), F32),
        compiler_params=_params(("arbitrary",)), name="adaln_mod",
    )(cc, w_ada, b_ada.reshape(1, n))


def _proj_kernel(nt, x_ref, c_ref, xp_ref, xn_ref, mod_ref, g_ref, wna_ref, wpr_ref, wlo_ref, cml_ref, chy_ref,
                 bif_ref, cos_ref, sin_ref, tri_ref, ona_ref, opr_ref, oqk_ref, ohv_ref, og_ref, ogt_ref, loc_ref):
    t = pl.program_id(0)
    x = jnp.where(t == nt, c_ref[0], x_ref[0])
    xs = jnp.concatenate([xp_ref[0], x, xn_ref[0]], axis=0)
    ms = jnp.mean(xs * xs, axis=-1, keepdims=True)
    y = xs * lax.rsqrt(ms + EPS) * g_ref[...]
    mod = mod_ref[0]
    h = (y * (1.0 + mod[1:2]) + mod[0:1]).astype(BF16)
    hm = h[HALO:HALO + TT]
    loc_ref[...] = _dot(h, wlo_ref[...])

    g = loc_ref[HALO:HALO + TT, LOC_GM:LOC_COLS] + bif_ref[...]
    gl = lax.broadcasted_iota(jnp.int32, (1, 128), 1)
    is_f = jnp.logical_and((gl % 8) >= 4, gl < 16)
    lf = jnp.where(is_f, jnp.minimum(g, 0.0) - jnp.log1p(jnp.exp(-jnp.abs(g))), 0.0)
    hi, mid, lo = _split3(lf)
    packed = (hi.astype(F32) + pltpu.roll(mid.astype(F32), 16, 1) + pltpu.roll(lo.astype(F32), 32, 1)).astype(BF16)
    cs = _dot(tri_ref[...], packed)
    cs = cs + pltpu.roll(cs, 112, 1) + pltpu.roll(cs, 96, 1)
    bcum = jnp.where(gl < 8, cs[0:TT], cs[TT:2 * TT])
    a = g - pltpu.roll(bcum, 128 - H_ML, 1)
    at = jnp.where(is_f, 0.0, a).T[0:16]
    tlane = lax.broadcasted_iota(jnp.int32, (1, TT), 1)
    pre = suf = at
    shift = 1
    while shift < TT:
        pre = jnp.maximum(pre, jnp.where(tlane >= shift, pltpu.roll(pre, shift, 1), -BIG))
        suf = jnp.maximum(suf, jnp.where(tlane < TT - shift, pltpu.roll(suf, TT - shift, 1), -BIG))
        shift *= 2
    cm_t = jnp.where(lax.broadcasted_iota(jnp.int32, (16, 1), 0) < 8, pre, suf)
    cmax = jnp.concatenate([cm_t, jnp.zeros((128 - 16, TT), F32)], axis=0).T
    ab = jnp.where(is_f, bcum, a)
    og_ref[0, 0] = LOG2E * (jnp.where(gl < 8, ab, 0.0) + pltpu.roll(jnp.where(gl < H_ML, cmax, 0.0), 8, 1))
    og_ref[0, 1] = LOG2E * (pltpu.roll(jnp.where(jnp.logical_and(gl >= 8, gl < 16), ab, 0.0), 120, 1)
                            + jnp.where(jnp.logical_and(gl >= 8, gl < 8 + H_ML), cmax, 0.0))
    ogt_ref[0] = LOG2E * at

    ona_ref[0] = _dot(hm, wna_ref[...]).astype(BF16)

    pm = jnp.where(jnp.logical_and(t != 0, t != nt), 1.0, 0.0)
    nm = jnp.where(jnp.logical_and(t != nt - 1, t != nt), 1.0, 0.0)
    row8 = lax.broadcasted_iota(jnp.int32, (8, 1), 0)
    nchunk = TT // PRE_ROWS

    def conv3(cols, w, c):
        r0 = HALO + c * PRE_ROWS
        u = loc_ref[r0:r0 + PRE_ROWS, cols]
        up = loc_ref[r0 - 1:r0, cols]
        un = loc_ref[r0 + PRE_ROWS:r0 + PRE_ROWS + 1, cols]
        if c == 0:
            up = up * pm
        if c == nchunk - 1:
            un = un * nm
        rp = pltpu.roll(u, 1, 0)
        rn = pltpu.roll(u, PRE_ROWS - 1, 0)
        prev = jnp.concatenate([jnp.where(row8 == 0, up, rp[0:8]), rp[8:]], axis=0)
        nxt = jnp.concatenate([rn[:PRE_ROWS - 8], jnp.where(row8 == 7, un, rn[PRE_ROWS - 8:])], axis=0)
        return prev * w[0:1] + u * w[1:2] + nxt * w[2:3]

    lane = lax.broadcasted_iota(jnp.int32, (1, 2 * D_ML), 1)
    first = (lane % 32) < 16
    for c in range(nchunk):
        rows = slice(c * PRE_ROWS, (c + 1) * PRE_ROWS)
        ohv_ref[0, rows] = conv3(slice(LOC_HY, LOC_GM), chy_ref[...], c).astype(BF16)
        x = _silu(conv3(slice(LOC_QK, LOC_HY), cml_ref[...], c))
        partner = jnp.where(first, pltpu.roll(x, 2 * D_ML - 16, 1), pltpu.roll(x, 16, 1))
        oqk_ref[0, rows] = (x * cos_ref[rows] + partner * sin_ref[rows]).astype(BF16)

    opr_ref[0] = _dot(hm, wpr_ref[...]).astype(BF16)


def _projection(x_lat, x_ctx, ctx_tile, mod3, g_pre, wna, wpr, wlo, conv_ml, conv_hy, bif_row, cos_t, sin_t,
                tri, nt):
    nb, lat_rows, d = x_lat.shape
    ltot = (nt + 1) * TT
    ctx_row = nb
    last8 = lat_rows // HALO - 1
    r8 = TT // HALO

    def mod_idx(t, b):
        return (jnp.where(t == nt, ctx_row, b), 0, 0)

    const = lambda t, b: (0, 0)
    big = pl.Buffered(1)
    tok = lambda t, b: (b, t, 0)
    return pl.pallas_call(
        functools.partial(_proj_kernel, nt), grid=(nt + 1, nb),
        in_specs=[pl.BlockSpec((1, TT, d), lambda t, b: (b, jnp.minimum(t, nt - 1), 0)),
                  pl.BlockSpec((1, TT, d), lambda t, b: (b, ctx_tile, 0)),
                  pl.BlockSpec((1, HALO, d), lambda t, b: (b, jnp.clip(t * r8 - 1, 0, last8), 0)),
                  pl.BlockSpec((1, HALO, d), lambda t, b: (b, jnp.minimum((t + 1) * r8, last8), 0)),
                  pl.BlockSpec((1, 3, d), mod_idx),
                  pl.BlockSpec((1, d), const),
                  pl.BlockSpec(wna.shape, const, pipeline_mode=big),
                  pl.BlockSpec(wpr.shape, const, pipeline_mode=big),
                  pl.BlockSpec(wlo.shape, const, pipeline_mode=big),
                  pl.BlockSpec((3, 512), const), pl.BlockSpec((3, 768), const), pl.BlockSpec((1, 128), const),
                  pl.BlockSpec((TT, 512), lambda t, b: (t, 0)), pl.BlockSpec((TT, 512), lambda t, b: (t, 0)),
                  pl.BlockSpec((2 * TT, TT), const)],
        out_specs=[pl.BlockSpec((1, TT, wna.shape[1]), tok), pl.BlockSpec((1, TT, wpr.shape[1]), tok),
                   pl.BlockSpec((1, TT, 512), tok), pl.BlockSpec((1, TT, 768), tok),
                   pl.BlockSpec((1, 2, TT, 128), lambda t, b: (b, 0, t, 0)),
                   pl.BlockSpec((1, 16, TT), lambda t, b: (b, 0, t))],
        out_shape=[jax.ShapeDtypeStruct((nb, ltot, wna.shape[1]), BF16),
                   jax.ShapeDtypeStruct((nb, ltot, wpr.shape[1]), BF16),
                   jax.ShapeDtypeStruct((nb, ltot, 512), BF16),
                   jax.ShapeDtypeStruct((nb, ltot, 768), BF16),
                   jax.ShapeDtypeStruct((nb, 2, ltot, 128), F32),
                   jax.ShapeDtypeStruct((nb, 16, ltot), F32)],
        scratch_shapes=[pltpu.VMEM((TT + 2 * HALO, LOC_COLS), F32)],
        compiler_params=_params(("arbitrary", "arbitrary")), name="norm_in_proj_conv",
    )(x_lat, x_ctx, x_lat, x_lat, mod3, g_pre.reshape(1, d), wna, wpr, wlo, conv_ml, conv_hy, bif_row,
      cos_t, sin_t, tri)


def _na_kernel(q_ref, k0_ref, k1_ref, k2_ref, v0_ref, v1_ref, v2_ref, kc_ref, vc_ref, za_ref, tab_ref, o_ref):
    lane = lax.broadcasted_iota(jnp.int32, (1, 128), 1)
    k_refs = (k0_ref, k1_ref, k2_ref)
    v_refs = (v0_ref, v1_ref, v2_ref)

    def row_max(s):
        return jnp.max(jnp.maximum(s[:, :128], s[:, 128:]), axis=-1, keepdims=True)

    def head_mask(h):
        return (lane >= HEAD_DIM) if h % 2 else (lane < HEAD_DIM)

    def masked_q(h):
        sl = slice((h // 2) * 128, (h // 2 + 1) * 128)
        q2 = q_ref[0, :, sl]
        return jnp.where(head_mask(h), q2, jnp.zeros_like(q2))

    def score_block(h, qm, blk):
        sl = slice((h // 2) * 128, (h // 2 + 1) * 128)
        if blk == 0:
            return _dot_nt(qm, kc_ref[0, :, sl])
        return _dot_nt(qm, k_refs[blk - 1][0, :, sl]) + tab_ref[0, h, :, (blk - 1) * TT:blk * TT]

    def values(h, p):
        sl = slice((h // 2) * 128, (h // 2 + 1) * 128)
        hm = head_mask(h)
        v = jnp.concatenate([vc_ref[0, :, sl]] + [r[0, :, sl] for r in v_refs], axis=0)
        acc = _dot(p, jnp.where(hm, v, jnp.ones_like(v)))
        return jnp.where(hm, acc / pltpu.roll(acc, HEAD_DIM, 1), 0.0)

    outs = {}
    qm = masked_q(0)
    sc = [score_block(0, qm, blk) for blk in range(4)]
    for h in range(H_NA):
        m = row_max(jnp.maximum(jnp.maximum(sc[0], sc[1]), jnp.maximum(sc[2], sc[3])))
        qm = masked_q(h + 1) if h + 1 < H_NA else None
        nxt, ps = [], []
        for blk in range(4):
            if qm is not None:
                nxt.append(score_block(h + 1, qm, blk))
            ps.append(jnp.exp2(sc[blk] - m).astype(BF16))
        sc = nxt
        outs[h] = values(h, jnp.concatenate(ps, axis=1))
        if h % 2 == 1:
            sl = slice((h // 2) * 128, (h // 2 + 1) * 128)
            o_ref[0, :, sl] = ((outs[h - 1] + outs[h]) * _silu(za_ref[0, :, sl].astype(F32))).astype(BF16)


def _attention(ona, pf, table, nt, with_ctx_queries):
    nb, ltot, _ = ona.shape
    nj = nt + 1 if with_ctx_queries else nt

    def kv(col, dj):
        return lambda j, b: (b, jnp.clip(j + dj, 0, nt - 1), col)

    def pat(j, b):
        p = jnp.where(j == 0, 0, jnp.where(j == nt - 1, 2, jnp.where(j == nt, 3, 1)))
        return (p, 0, 0, 0)

    return pl.pallas_call(
        _na_kernel, grid=(nj, nb),
        in_specs=[pl.BlockSpec((1, TT, 512), lambda j, b: (b, j, 0)),
                  pl.BlockSpec((1, TT, 512), kv(1, -1)), pl.BlockSpec((1, TT, 512), kv(1, 0)),
                  pl.BlockSpec((1, TT, 512), kv(1, 1)),
                  pl.BlockSpec((1, TT, 512), kv(2, -1)), pl.BlockSpec((1, TT, 512), kv(2, 0)),
                  pl.BlockSpec((1, TT, 512), kv(2, 1)),
                  pl.BlockSpec((1, TT, 512), lambda j, b: (b, nt, 1)),
                  pl.BlockSpec((1, TT, 512), lambda j, b: (b, nt, 2)),
                  pl.BlockSpec((1, TT, 512), lambda j, b: (b, j, PF_ZA)),
                  pl.BlockSpec((1, H_NA, TT, 3 * TT), pat)],
        out_specs=pl.BlockSpec((1, TT, 512), lambda j, b: (b, j, 0)),
        out_shape=jax.ShapeDtypeStruct((nb, ltot, 512), BF16),
        compiler_params=_params(("arbitrary", "arbitrary")), name="nbr_attention",
    )(ona, ona, ona, ona, ona, ona, ona, ona, ona, pf, table)


def _na_bias_table(rpb):
    a = np.arange(TT) // GRID_W
    qc = np.arange(TT) % GRID_W
    kidx = np.arange(3 * TT)
    kr = (kidx // TT - 1) * BAND_ROWS + (kidx % TT) // GRID_W
    kcol = kidx % GRID_W
    qstart = np.clip(qc - NA_COLS // 2, 0, GRID_W - NA_COLS)
    col_ok = (kcol[None, :] >= qstart[:, None]) & (kcol[None, :] < qstart[:, None] + NA_COLS)
    coff = np.clip(kcol[None, :] - qc[:, None], -(NA_COLS - 1), NA_COLS - 1) + NA_COLS - 1
    roff = np.clip(kr[None, :] - a[:, None] + NA_ROWS - 1, 0, 2 * NA_ROWS - 2)
    half = NA_ROWS // 2
    row_ok = [
        (kr[None, :] >= 0) & (kr[None, :] < NA_ROWS) & (a[:, None] >= 0),
        (kr[None, :] >= a[:, None] - half) & (kr[None, :] < a[:, None] + half),
        (kr[None, :] >= BAND_ROWS - NA_ROWS) & (kr[None, :] < BAND_ROWS) & (a[:, None] >= 0),
    ]
    nr, ncol = 2 * NA_ROWS - 1, 2 * NA_COLS - 1
    onehot = (coff[:GRID_W, :GRID_W].reshape(1, -1) == np.arange(ncol)[:, None]).astype(np.float32)
    cmat = jnp.dot(rpb.astype(F32).reshape(H_NA * nr, ncol), onehot,
                   precision=lax.Precision.HIGHEST).reshape(H_NA, nr, GRID_W, GRID_W)
    nkr = 3 * BAND_ROWS
    blocks = [jnp.concatenate([cmat[:, roff[ai * GRID_W, kj * GRID_W]] for kj in range(nkr)], axis=-1)
              for ai in range(BAND_ROWS)]
    bias = jnp.concatenate(blocks, axis=1)
    bias = bias * LOG2E
    tabs = [jnp.where((r & col_ok)[None], bias, NEG) for r in row_ok]
    tabs.append(jnp.full_like(bias, NEG))
    return jnp.stack(tabs, axis=0)


def _head_lane_sums(x, lane_h):
    out = jnp.zeros_like(x)
    half = lax.broadcasted_iota(jnp.int32, (1, 128), 1) // HEAD_DIM
    for h in range(H_ML):
        t = x[:, (h // 2) * 128:(h // 2 + 1) * 128]
        sm = jnp.sum(jnp.where(half == h % 2, t, 0.0), axis=-1, keepdims=True)
        out = jnp.where(lane_h == h, sm, out)
    return out


def _scan_kernel(q_ref, k_ref, v_ref, g_ref, gt_ref, e_ref, bm_ref, o_ref, w_ref, n_ref, m_ref):
    s = pl.program_id(2)
    sgn = 1 - 2 * pl.program_id(1)

    @pl.when(s == 0)
    def _():
        w_ref[...] = jnp.zeros_like(w_ref)
        n_ref[...] = jnp.zeros_like(n_ref)
        m_ref[...] = jnp.zeros_like(m_ref)

    lane_h = lax.broadcasted_iota(jnp.int32, (1, D_ML), 1) // HEAD_DIM
    ri = lax.broadcasted_iota(jnp.int32, (TT, TT), 0)
    ci = lax.broadcasted_iota(jnp.int32, (TT, TT), 1)
    tri = ((ci - ri) * sgn) <= 0
    gate_lanes = lax.broadcasted_iota(jnp.int32, (1, 128), 1) < H_ML

    seqs = range(SCAN_NB)
    gate, kws, kwfs, vbs, mrows = [], [], [], [], []
    for i in seqs:
        gi = g_ref[i, 0]
        a = jnp.where(gate_lanes, gi, 0.0)
        bb = jnp.where(gate_lanes, pltpu.roll(gi, 128 - H_ML, 1), 0.0)
        cmax = jnp.where(gate_lanes, pltpu.roll(gi, 128 - 2 * H_ML, 1), 0.0)
        m_prev = m_ref[i]
        mt = jnp.maximum(m_prev, cmax)
        mc = jnp.maximum(m_prev, jnp.max(a, axis=0, keepdims=True))
        small = jnp.concatenate([jnp.exp2(a - mc), jnp.exp2(-bb - mt), jnp.exp2(m_prev - mt),
                                 jnp.broadcast_to(jnp.exp2(m_prev - mc), (8, 128))], axis=0)
        sh = small.astype(BF16)
        sl = (small - sh.astype(F32)).astype(BF16)
        ex = _dot(sh, e_ref[...]) + _dot(sl, e_ref[...])
        gate.append((ex[TT:2 * TT], ex[2 * TT:3 * TT], ex[3 * TT:3 * TT + 1],
                     jnp.min(bb, axis=0, keepdims=True) + mc))
        mrows.append(mt)
        kwf = k_ref[i].astype(F32) * ex[0:TT]
        kwfs.append(kwf)
        kws.append(kwf.astype(BF16))
        vbs.append(v_ref[i].astype(BF16))

    scores = [[_dot_nt(jnp.where(lane_h == h, q_ref[i], jnp.zeros_like(q_ref[i])), k_ref[i])
               for h in range(H_ML)] for i in seqs]
    inters = [_dot(q_ref[i], w_ref[i].astype(BF16)) for i in seqs]
    updates = [_dot_tn(kws[i], vbs[i]) for i in seqs]

    def masked(i, h):
        arg = jnp.where(tri, gt_ref[i, h:h + 1, :] - mrows[i][:, h:h + 1], -BIG)
        sc = scores[i][h] * jnp.exp2(arg)
        return sc.astype(BF16), jnp.sum(sc, axis=-1, keepdims=True)

    order = [(i, h) for i in seqs for h in range(H_ML)]
    cur = masked(*order[0])
    intras, dens = {}, {}
    for n, (i, h) in enumerate(order):
        nxt = masked(*order[n + 1]) if n + 1 < len(order) else None
        scb, rs = cur
        part = _dot(scb, jnp.where(lane_h == h, vbs[i], jnp.zeros_like(vbs[i])))
        intras[i] = part if h == 0 else intras[i] + part
        dens[i] = jnp.where(lane_h == h, rs, dens[i] if h else jnp.zeros((TT, D_ML), F32))
        cur = nxt

    for i in seqs:
        eb_e, beta_t, decay, m_new = gate[i]
        intra, den_intra = intras[i], dens[i]
        num = beta_t * inters[i] + intra
        den = beta_t * _head_lane_sums(q_ref[i].astype(F32) * n_ref[i], lane_h) + den_intra
        o_ref[i, 0] = (num / jnp.maximum(jnp.abs(den), eb_e)).astype(BF16)

        w_ref[i] = decay * w_ref[i] + bm_ref[...] * updates[i]
        n_ref[i] = decay * n_ref[i] + jnp.sum(kwfs[i], axis=0, keepdims=True)
        m_ref[i] = m_new


def _mlstm_scan(qk, pf, gates, gates_t, emat, bmask, nt):
    nb, ltot, _ = qk.shape
    g = SCAN_NB

    def chunk(s, d):
        return jnp.where(s == 0, nt, jnp.where(d == 0, s - 1, nt - s))

    const = lambda b, d, s: (0, 0)
    return pl.pallas_call(
        _scan_kernel, grid=(nb // g, 2, nt + 1),
        in_specs=[pl.BlockSpec((g, TT, 256), lambda b, d, s: (b, chunk(s, d), 0)),
                  pl.BlockSpec((g, TT, 256), lambda b, d, s: (b, chunk(s, d), 1)),
                  pl.BlockSpec((g, TT, 256), lambda b, d, s: (b, chunk(s, d), PF_VM)),
                  pl.BlockSpec((g, 1, TT, 128), lambda b, d, s: (b, d, chunk(s, d), 0)),
                  pl.BlockSpec((g, 8, TT), lambda b, d, s: (b, d, chunk(s, d))),
                  pl.BlockSpec((128, 256), const), pl.BlockSpec((256, 256), const)],
        out_specs=pl.BlockSpec((g, 1, TT, 256), lambda b, d, s: (b, d, chunk(s, d), 0)),
        out_shape=jax.ShapeDtypeStruct((nb, 2, ltot, 256), BF16),
        scratch_shapes=[pltpu.VMEM((g, D_ML, D_ML), F32), pltpu.VMEM((g, 1, D_ML), F32),
                        pltpu.VMEM((g, 1, 128), F32)],
        compiler_params=_params(("arbitrary", "arbitrary", "arbitrary")), name="mlstm_scan",
    )(qk, qk, pf, gates, gates_t, emat, bmask)


def _hyfilt_kernel(tl, length, feat_ref, w1_ref, b1_ref, w2_ref, b2_ref, w3_ref, fq_ref, dec_ref, o_ref, ss_ref):
    i = pl.program_id(0)
    th = tl // 2
    a = jnp.sin(fq_ref[0:1] * (_dot_2x(feat_ref[...], w1_ref[...]) + b1_ref[...]))
    a = jnp.sin(fq_ref[1:2] * (_dot_2x(a, w2_ref[...]) + b2_ref[...]))
    part = None
    for half in range(2):
        dec = dec_ref[half * th:(half + 1) * th]
        f = _dot_2x(a, w3_ref[half]) * jnp.concatenate([dec, dec, dec, dec], axis=1)
        fwd = jnp.concatenate([f[:, 0:D_HY], f[:, 2 * D_HY:3 * D_HY]], axis=1)
        bwd = jnp.concatenate([f[:, D_HY:2 * D_HY], f[:, 3 * D_HY:4 * D_HY]], axis=1)
        row = lax.broadcasted_iota(jnp.int32, (th, 1), 0) + (i * tl + half * th)
        k2 = jnp.where(row < length, fwd, jnp.where(row > length, bwd, 0.0))
        o_ref[half * th:(half + 1) * th] = k2
        sq = jnp.sum(k2 * k2, axis=0, keepdims=True)
        part = sq if part is None else part + sq

    @pl.when(i == 0)
    def _():
        ss_ref[...] = jnp.zeros_like(ss_ref)

    ss_ref[...] += part


def _hyena_filters(length, w1, b1, w2, b2, w3, freq):
    t = np.arange(length, dtype=np.float32)
    tn = (t / np.float32(length - 1)).astype(np.float32)
    bands = (HY_EMB - 1) // 2
    fr = np.linspace(1e-4, bands - 1, bands, dtype=np.float32)
    ang = (np.float32(2.0 * math.pi / length) * t[:, None] * fr[None, :]).astype(np.float32)
    feat = np.zeros((length, 128), np.float32)
    feat[:, 0] = tn
    feat[:, 1:1 + bands] = np.cos(ang)
    feat[:, 1 + bands:1 + 2 * bands] = -np.sin(ang)
    deltas = np.abs(np.linspace(math.log(HY_TARGET) / HY_SLOW, math.log(HY_TARGET) / HY_FAST, D_HY, dtype=np.float32))
    dec = np.exp(-tn[:, None] * deltas[None, :]).astype(np.float32)
    lag = np.concatenate([np.arange(length), [0], np.arange(length - 1, 0, -1)])
    feat, dec = feat[lag], dec[lag]

    tl = 512
    hw = 64
    fp = feat.reshape(2 * length // tl, 2, tl // 2, 128)[..., :hw]
    feat = np.concatenate([fp[:, 0], fp[:, 1]], axis=-1).reshape(length, 128)
    pad2 = lambda m, r, c: jnp.zeros((r, c), F32).at[:m.shape[0], :m.shape[1]].set(m.astype(F32))
    bdiag = lambda m: jnp.zeros((128, 128), F32).at[:hw, :hw].set(pad2(m, hw, hw)).at[hw:, hw:].set(pad2(m, hw, hw))
    twice = lambda v: jnp.concatenate([pad2(v, v.shape[0], hw)] * 2, axis=1)
    w1p = bdiag(w1)
    w2p = bdiag(w2)
    w3h = pad2(w3, hw, 4 * D_HY)
    w3p = jnp.stack([jnp.concatenate([w3h, jnp.zeros_like(w3h)], axis=0),
                     jnp.concatenate([jnp.zeros_like(w3h), w3h], axis=0)])
    b1p = twice(b1[None])
    b2p = twice(b2[None])
    fqp = twice(freq)
    const = lambda i: (0, 0)
    return pl.pallas_call(
        functools.partial(_hyfilt_kernel, tl, length), grid=(2 * length // tl,),
        in_specs=[pl.BlockSpec((tl // 2, 128), lambda i: (i, 0)),
                  pl.BlockSpec((128, 128), const), pl.BlockSpec((1, 128), const),
                  pl.BlockSpec((128, 128), const), pl.BlockSpec((1, 128), const),
                  pl.BlockSpec((2, 128, 4 * D_HY), lambda i: (0, 0, 0)), pl.BlockSpec((2, 128), const),
                  pl.BlockSpec((tl, D_HY), lambda i: (i, 0))],
        out_specs=[pl.BlockSpec((tl, 2 * D_HY), lambda i: (i, 0)), pl.BlockSpec((1, 2 * D_HY), const)],
        out_shape=[jax.ShapeDtypeStruct((2 * length, 2 * D_HY), F32), jax.ShapeDtypeStruct((1, 2 * D_HY), F32)],
        compiler_params=_params(("arbitrary",)), name="hyena_filter_mlp",
    )(feat, w1p, b1p, w2p, b2p, w3p, fqp, dec)


@functools.lru_cache(maxsize=None)
def _fft_tables(length):
    n = 2 * length
    n1 = n // FFT_N2
    nh = n1 // 2
    k1 = np.arange(n1)[None, :, None].astype(np.float64)
    j = np.arange(nh)[None, None, :].astype(np.float64)
    n2 = np.arange(FFT_N2)[:, None, None].astype(np.float64)
    ang = -2.0 * np.pi * k1 * (FFT_N2 * j + n2) / n
    gf = np.concatenate([np.cos(ang), np.sin(ang)], axis=1)
    gi = np.concatenate([np.cos(ang).transpose(0, 2, 1), np.sin(ang).transpose(0, 2, 1)], axis=1)
    kk = np.arange(FFT_N2)[:, None] * np.arange(FFT_N2)[None, :]
    a2 = -2.0 * np.pi * kk / FFT_N2
    f2 = np.concatenate([np.cos(a2), np.sin(a2)], axis=0)
    cv = lambda m: np.asarray(m, np.float32).astype(BF16)
    return cv(gf), cv(gi), cv(f2), n1


def _stage1(n1, z_of, gf_ref, scr_ref, cols):
    def group(gidx, c):
        n2s = [gidx * FFT_UNROLL + u for u in range(FFT_UNROLL)]
        zs = [z_of(n2) for n2 in n2s]
        rs = [_dot(gf_ref[n2], z) for n2, z in zip(n2s, zs)]
        for n2, r in zip(n2s, rs):
            off = pl.multiple_of(n2 * (n1 + FFT_PAD), 8)
            if cols == 256:
                scr_ref[0, pl.ds(off, n1), :] = r[:n1, :128] - r[n1:, 128:]
                scr_ref[1, pl.ds(off, n1), :] = r[n1:, :128] + r[:n1, 128:]
            else:
                scr_ref[0, pl.ds(off, n1), :] = r[:n1]
                scr_ref[1, pl.ds(off, n1), :] = r[n1:]
        return c

    lax.fori_loop(0, FFT_N2 // FFT_UNROLL, group, 0)


def _stage2(n1, k1, f2_ref, scr_ref):
    ar = scr_ref[0, pl.ds(k1, FFT_N2, stride=n1 + FFT_PAD), :]
    ai = scr_ref[1, pl.ds(k1, FFT_N2, stride=n1 + FFT_PAD), :]
    r = _dot(f2_ref[...], jnp.concatenate([ar, ai], axis=1).astype(BF16))
    return r[:128, :128] - r[128:, 128:], r[128:, :128] + r[:128, 128:]


def _hyconv_kernel(n1, kg, z_ref, gf_ref, f2_ref, kf_ref, gi_ref, o_ref, scr_ref, tp_ref):
    s = pl.program_id(2)
    nh = n1 // 2
    pitch = FFT_N2 + FFT_PAD

    @pl.when(s == 0)
    def _():
        for e in range(2):
            for j in range(nh):
                tp_ref[e, j * pitch:j * pitch + FFT_N2, :] = z_ref[0, e, j * FFT_N2:(j + 1) * FFT_N2, :].astype(F32)

        def z_of(n2):
            zr = tp_ref[0, pl.ds(n2, nh, stride=pitch), :]
            zi = tp_ref[1, pl.ds(n2, nh, stride=pitch), :]
            return jnp.concatenate([zr, zi], axis=1).astype(BF16)
        _stage1(n1, z_of, gf_ref, scr_ref, 256)

    def k1_group(gidx, c):
        idx = [gidx * FFT_UNROLL + u for u in range(FFT_UNROLL)]
        xs = [_stage2(n1, s * kg + i, f2_ref, scr_ref) for i in idx]
        ys = []
        for i, (xr, xi) in zip(idx, xs):
            off = pl.multiple_of(i * FFT_N2, FFT_N2)
            kr = kf_ref[0, 0, pl.ds(off, FFT_N2), :]
            ki = kf_ref[0, 1, pl.ds(off, FFT_N2), :]
            ys.append(jnp.concatenate([xr * kr - xi * ki, xr * ki + xi * kr], axis=1).astype(BF16))
        rs = [_dot(f2_ref[...], yc) for yc in ys]
        for i, r in zip(idx, rs):
            k1 = s * kg + i
            scr_ref[0, pl.ds(k1, FFT_N2, stride=n1 + FFT_PAD), :] = r[:128, :128] + r[128:, 128:]
            scr_ref[1, pl.ds(k1, FFT_N2, stride=n1 + FFT_PAD), :] = r[:128, 128:] - r[128:, :128]
        return c

    lax.fori_loop(0, kg // FFT_UNROLL, k1_group, 0)

    @pl.when(s == pl.num_programs(2) - 1)
    def _():
        def group(gidx, c):
            n2s = [gidx * FFT_UNROLL + u for u in range(FFT_UNROLL)]
            bcs = []
            for n2 in n2s:
                off = pl.multiple_of(n2 * (n1 + FFT_PAD), 8)
                bcs.append(jnp.concatenate([scr_ref[0, pl.ds(off, n1), :], scr_ref[1, pl.ds(off, n1), :]],
                                           axis=1).astype(BF16))
            rs = [_dot(gi_ref[n2], bc) for n2, bc in zip(n2s, bcs)]
            for n2, r in zip(n2s, rs):
                tp_ref[0, pl.ds(n2, nh, stride=pitch), :] = r[:nh, :128] + r[nh:, 128:]
                tp_ref[1, pl.ds(n2, nh, stride=pitch), :] = r[:nh, 128:] - r[nh:, :128]
            return c
        lax.fori_loop(0, FFT_N2 // FFT_UNROLL, group, 0)
        for e in range(2):
            for j in range(nh):
                o_ref[0, e, j * FFT_N2:(j + 1) * FFT_N2, :] = tp_ref[e, j * pitch:j * pitch + FFT_N2, :].astype(BF16)


def _hyena_conv(z4, col0, kf, order, length):
    gf, gi, f2, n1 = _fft_tables(length)
    n = 2 * length
    kg = min(n1, 2 * FFT_UNROLL)
    npair = z4.shape[0]
    big = pl.Buffered(1)
    return pl.pallas_call(
        functools.partial(_hyconv_kernel, n1, kg), grid=(npair, 2, n1 // kg),
        in_specs=[pl.BlockSpec((1, 2, length, 128), lambda p, c, s: (p, 0, 0, col0 + c)),
                  pl.BlockSpec(gf.shape, lambda p, c, s: (0, 0, 0), pipeline_mode=big),
                  pl.BlockSpec(f2.shape, lambda p, c, s: (0, 0)),
                  pl.BlockSpec((1, 2, kg * FFT_N2, 128), lambda p, c, s: (order, 0, s, c)),
                  pl.BlockSpec(gi.shape, lambda p, c, s: (0, 0, 0), pipeline_mode=big)],
        out_specs=pl.BlockSpec((1, 2, length, 128), lambda p, c, s: (p, 0, 0, c)),
        out_shape=jax.ShapeDtypeStruct((npair, 2, length, 256), BF16),
        scratch_shapes=[pltpu.VMEM((2, FFT_N2 * (n1 + FFT_PAD), 128), F32),
                        pltpu.VMEM((2, (n1 // 2) * (FFT_N2 + FFT_PAD), 128), F32)],
        compiler_params=_params(("arbitrary", "arbitrary", "arbitrary"), FFTCONV_VMEM_LIMIT), name="hyena_fftconv",
    )(z4, gf, f2, kf, gi)


def _hyfft_kernel(n1, kg, z_ref, gf_ref, f2_ref, o_ref, scr_ref, tp_ref):
    s = pl.program_id(2)
    pitch = FFT_N2 + FFT_PAD

    @pl.when(s == 0)
    def _():
        for j in range(n1):
            tp_ref[j * pitch:j * pitch + FFT_N2, :] = z_ref[j * FFT_N2:(j + 1) * FFT_N2, :]
        _stage1(n1, lambda n2: tp_ref[pl.ds(n2, n1, stride=pitch), :].astype(BF16), gf_ref, scr_ref, 128)

    def k1_group(gidx, c):
        idx = [gidx * FFT_UNROLL + u for u in range(FFT_UNROLL)]
        xs = [_stage2(n1, s * kg + i, f2_ref, scr_ref) for i in idx]
        for i, (xr, xi) in zip(idx, xs):
            off = pl.multiple_of(i * FFT_N2, FFT_N2)
            o_ref[0, 0, pl.ds(off, FFT_N2), :] = xr
            o_ref[0, 1, pl.ds(off, FFT_N2), :] = xi
        return c

    lax.fori_loop(0, kg // FFT_UNROLL, k1_group, 0)


@functools.lru_cache(maxsize=None)
def _fft_table_full(length):
    n = 2 * length
    n1 = n // FFT_N2
    k1 = np.arange(n1)[None, :, None].astype(np.float64)
    j = np.arange(n1)[None, None, :].astype(np.float64)
    n2 = np.arange(FFT_N2)[:, None, None].astype(np.float64)
    ang = -2.0 * np.pi * k1 * (FFT_N2 * j + n2) / n
    return np.concatenate([np.cos(ang), np.sin(ang)], axis=1).astype(np.float32).astype(BF16)


def _filter_spectrum(k2, length):
    _, _, f2, n1 = _fft_tables(length)
    gf = _fft_table_full(length)
    n = 2 * length
    kg = min(n1, 16)
    nf = k2.shape[1] // D_HY
    big = pl.Buffered(1)
    return pl.pallas_call(
        functools.partial(_hyfft_kernel, n1, kg), grid=(nf, 2, n1 // kg),
        in_specs=[pl.BlockSpec((n, 128), lambda f, c, s: (0, 2 * f + c), pipeline_mode=big),
                  pl.BlockSpec(gf.shape, lambda f, c, s: (0, 0, 0), pipeline_mode=big),
                  pl.BlockSpec(f2.shape, lambda f, c, s: (0, 0))],
        out_specs=pl.BlockSpec((1, 2, kg * FFT_N2, 128), lambda f, c, s: (f, 0, s, c)),
        out_shape=jax.ShapeDtypeStruct((nf, 2, n, 256), F32),
        scratch_shapes=[pltpu.VMEM((2, FFT_N2 * (n1 + FFT_PAD), 128), F32),
                        pltpu.VMEM((n1 * (FFT_N2 + FFT_PAD), 128), F32)],
        compiler_params=_params(("arbitrary", "arbitrary", "arbitrary")), name="hyena_filter_fft",
    )(k2, gf, f2)


def _normalised(k2, ssq, extra_scale):
    return k2 * (lax.rsqrt(ssq + EPS) * extra_scale)


def _hy_mid_kernel(y_ref, hv_ref, b_ref, o_ref):
    v = hv_ref[0][:, 0:D_HY].astype(F32)
    x1 = hv_ref[0][:, D_HY:2 * D_HY].astype(F32)
    o_ref[0] = (x1 * (y_ref[0].astype(F32) + v * b_ref[...])).astype(BF16)


def _hy_mid(y1, hv, bias0):
    nb, length, _ = y1.shape
    te = 4 * TT
    return pl.pallas_call(
        _hy_mid_kernel, grid=(nb, length // te),
        in_specs=[pl.BlockSpec((1, te, D_HY), lambda b, t: (b, t, 0)),
                  pl.BlockSpec((1, te, 512), lambda b, t: (b, t, 0)),
                  pl.BlockSpec((1, D_HY), lambda b, t: (0, 0))],
        out_specs=pl.BlockSpec((1, te, D_HY), lambda b, t: (b, t, 0)),
        out_shape=jax.ShapeDtypeStruct(y1.shape, BF16),
        compiler_params=_params(("arbitrary", "arbitrary")), name="hyena_mid",
    )(y1, hv, bias0.reshape(1, D_HY))


def _hy_ctx_kernel(lc, hv_ref, zh_ref, k2_ref, b_ref, fc_ref, ff_ref, ci_ref, o_ref):
    nf = 2 * lc
    hv = hv_ref[0].astype(F32)

    def conv(z, o):
        kf = _dot_exact_rhs(ff_ref[...], k2_ref[:, o * D_HY:(o + 1) * D_HY])
        zs = _dot(fc_ref[...], z.astype(BF16))
        zr, zi = zs[:nf], zs[nf:]
        kr, ki = kf[:nf], kf[nf:]
        ys = jnp.concatenate([zr * kr - zi * ki, zr * ki + zi * kr], axis=0).astype(BF16)
        return _dot(ci_ref[...], ys) * (1.0 / nf)

    v = hv[:, 0:D_HY]
    x1 = hv[:, D_HY:2 * D_HY]
    x2 = hv[:, 2 * D_HY:3 * D_HY]
    z1 = x1 * (conv(v, 0) + v * b_ref[0:1])
    p = x2 * (conv(z1, 1) + z1 * b_ref[1:2])
    o_ref[0] = (p * _silu(zh_ref[0].astype(F32))).astype(BF16)


def _dot_exact_rhs(a_bf16, b):
    bh, bm, bl = _split3(b)
    return _dot(a_bf16, bh) + _dot(a_bf16, bm) + _dot(a_bf16, bl)


@functools.lru_cache(maxsize=None)
def _ctx_dft_tables(lc):
    nf = 2 * lc
    k = np.arange(nf)[:, None].astype(np.float64)
    ang = -2.0 * np.pi * k * np.arange(nf)[None, :] / nf
    ff = np.concatenate([np.cos(ang), np.sin(ang)], axis=0)
    fc = ff[:, :lc]
    a2 = 2.0 * np.pi * np.arange(lc)[:, None] * np.arange(nf)[None, :] / nf
    ci = np.concatenate([np.cos(a2), -np.sin(a2)], axis=1)
    cv = lambda m: np.asarray(m, np.float32).astype(BF16)
    return cv(fc), cv(ff), cv(ci)


def _hy_ctx(hv, pf, k2c, hy_bias, nt, lc):
    nb = hv.shape[0]
    fc, ff, ci = _ctx_dft_tables(lc)
    c2 = lambda b: (0, 0)
    return pl.pallas_call(
        functools.partial(_hy_ctx_kernel, lc), grid=(nb,),
        in_specs=[pl.BlockSpec((1, lc, 768), lambda b: (b, nt, 0)),
                  pl.BlockSpec((1, lc, D_HY), lambda b: (b, nt, PF_ZH)),
                  pl.BlockSpec(k2c.shape, c2),
                  pl.BlockSpec((2, D_HY), c2),
                  pl.BlockSpec(fc.shape, c2), pl.BlockSpec(ff.shape, c2), pl.BlockSpec(ci.shape, c2)],
        out_specs=pl.BlockSpec((1, lc, D_HY), lambda b: (b, 0, 0)),
        out_shape=jax.ShapeDtypeStruct((nb, lc, D_HY), BF16),
        compiler_params=_params(("arbitrary",)), name="hyena_ctx",
    )(hv, pf, k2c, hy_bias, fc, ff, ci)


def _out_kernel(n_hy, na_ref, h0_ref, h1_ref, om_ref, zm_ref, w_ref, x_ref, mod_ref, g_ref, *rest):
    hy_refs, o_ref = rest[:n_hy], rest[n_hy]
    chunks = [slice(r, r + TT) for r in range(0, o_ref.shape[1], TT)]
    acts = []
    for rows in chunks:
        if n_hy == 1:
            hy = hy_refs[0][0, rows]
        else:
            y_ref, z_ref, x2_ref, zh_ref, b_ref = hy_refs
            z = z_ref[0, rows].astype(F32)
            hy = (x2_ref[0, rows].astype(F32) * (y_ref[0, rows].astype(F32) + z * b_ref[...])
                  * _silu(zh_ref[0, rows].astype(F32))).astype(BF16)
        hsum = h0_ref[0, 0, rows].astype(F32) + h1_ref[0, 0, rows].astype(F32)
        ml = (jax.nn.sigmoid(om_ref[0, rows].astype(F32)) * hsum * _silu(zm_ref[0, rows].astype(F32))).astype(BF16)
        acts.append((ml, hy))
    ys = [_dot(na_ref[0, rows], w_ref[0:D_NA]) + _dot(ml, w_ref[D_NA:D_NA + D_ML])
          + _dot(hy, w_ref[D_NA + D_ML:D_MODEL]) for rows, (ml, hy) in zip(chunks, acts)]
    for rows, y in zip(chunks, ys):
        ms = jnp.mean(y * y, axis=-1, keepdims=True)
        r = y * lax.rsqrt(ms + EPS) * g_ref[...]
        o_ref[0, rows] = x_ref[0, rows] + mod_ref[0][2:3] * r


def _out_projection(na_g, hml, pf, hy_args, w_out, res, res_t0, mod3, g_post, to, t0, ntiles, mod_row, out_rows,
                    prev=None):
    nb = na_g.shape[0]
    d = D_MODEL
    tok = lambda col: (lambda b, t: (b, t + t0, col))
    own = lambda col: (lambda b, t: (b, t, col))
    in_specs = [pl.BlockSpec((1, to, 512), tok(0)),
                pl.BlockSpec((1, 1, to, 256), lambda b, t: (b, 0, t + t0, 0)),
                pl.BlockSpec((1, 1, to, 256), lambda b, t: (b, 1, t + t0, 0)),
                pl.BlockSpec((1, to, 256), tok(PF_OM)), pl.BlockSpec((1, to, 256), tok(PF_ZM)),
                pl.BlockSpec((d, d), lambda b, t: (0, 0), pipeline_mode=pl.Buffered(1)),
                pl.BlockSpec((1, to, d), lambda b, t: (b, t + res_t0, 0)),
                pl.BlockSpec((1, 3, d), lambda b, t: (mod_row(b), 0, 0)),
                pl.BlockSpec((1, d), lambda b, t: (0, 0))]
    args = [na_g, hml, hml, pf, pf, w_out, res, mod3, g_post.reshape(1, d)]
    if len(hy_args) == 1:
        in_specs.append(pl.BlockSpec((1, to, D_HY), own(0)))
        args.append(hy_args[0])
    else:
        y2, z1, hv, pf_, bias1 = hy_args
        in_specs += [pl.BlockSpec((1, to, D_HY), own(0)), pl.BlockSpec((1, to, D_HY), own(0)),
                     pl.BlockSpec((1, to, D_HY), tok(2)), pl.BlockSpec((1, to, D_HY), tok(PF_ZH)),
                     pl.BlockSpec((1, D_HY), lambda b, t: (0, 0))]
        args += [y2, z1, hv, pf_, bias1.reshape(1, D_HY)]
    n_hy = len(hy_args)
    aliases = {}
    if prev is not None:
        in_specs.append(pl.BlockSpec(memory_space=pl.ANY))
        args.append(prev)
        aliases = {len(args) - 1: 0}
        kern = lambda *r: _out_kernel(n_hy, *r[:9 + n_hy], r[10 + n_hy])
    else:
        kern = functools.partial(_out_kernel, n_hy)
    return pl.pallas_call(
        kern, grid=(nb, ntiles), in_specs=in_specs,
        out_specs=pl.BlockSpec((1, to, d), tok(0)),
        out_shape=jax.ShapeDtypeStruct((nb, out_rows, d), F32),
        input_output_aliases=aliases,
        compiler_params=_params(("arbitrary", "arbitrary")), name="out_proj_norm_residual",
    )(*args)


@functools.lru_cache(maxsize=None)
def _rope_tables(length, lc):
    n = HEAD_DIM // 4
    t = np.arange(length)
    inv = (ROPE_BASE ** (-np.arange(n, dtype=np.float32) / n)).astype(np.float32)
    pos = np.stack([t // GRID_W, t % GRID_W], axis=-1).astype(np.float32)
    ang = (pos[:, :, None] * inv).astype(np.float32)
    cos_h = np.concatenate([np.cos(ang[:, 0]), np.cos(ang[:, 0]), np.cos(ang[:, 1]), np.cos(ang[:, 1])], axis=-1)
    sin_h = np.concatenate([-np.sin(ang[:, 0]), np.sin(ang[:, 0]), -np.sin(ang[:, 1]), np.sin(ang[:, 1])], axis=-1)
    cos_t = np.concatenate([np.tile(cos_h, (1, 2 * H_ML)), np.ones((lc, 512), np.float32)], axis=0)
    sin_t = np.concatenate([np.tile(sin_h, (1, 2 * H_ML)), np.zeros((lc, 512), np.float32)], axis=0)
    kscale = np.concatenate([np.ones(256, np.float32), np.full(256, HEAD_DIM ** -0.5, np.float32)])
    return (cos_t * kscale).astype(np.float32), (sin_t * kscale).astype(np.float32)


@functools.lru_cache(maxsize=None)
def _scan_constants():
    tril = np.tril(np.ones((TT, TT), np.float32))
    emat = np.zeros((128, D_ML), np.float32)
    for h in range(H_ML):
        emat[h, h * HEAD_DIM:(h + 1) * HEAD_DIM] = 1.0
    rh = np.arange(D_ML)[:, None] // HEAD_DIM
    chd = np.arange(D_ML)[None, :] // HEAD_DIM
    bmask = (rh == chd).astype(np.float32)
    tri = np.concatenate([tril, tril.T], axis=0)
    return tri.astype(BF16), emat.astype(BF16), bmask


def _layer(x_lat, x_ctx, ctx_tile, mod3, nb, length, lc, g_pre, g_post, w_in, b_if, conv_ml, conv_hy, rpb,
           hf_w1, hf_b1, hf_w2, hf_b2, hf_w3, hf_freq, hy_bias, w_out, update_ctx):
    nt = length // TT
    ltot = length + lc
    tri, emat, bmask = _scan_constants()
    cos_t, sin_t = _rope_tables(length, lc)

    c = lambda a, b: w_in[:, a:b]
    wna = jnp.concatenate([c(0, 512) * (HEAD_DIM ** -0.5 * LOG2E), c(512, 1536)], axis=1).astype(BF16)
    wpr = jnp.concatenate([c(1536, 2048), c(2560, 2816), c(2816, 3072), c(3072, 3328), c(4112, 4368)],
                          axis=1).astype(BF16)
    wlo = jnp.concatenate([c(2048, 2560), c(3344, 4112), c(3328, 3344), jnp.zeros((D_MODEL, 112), F32)],
                          axis=1).astype(BF16)
    bif_row = jnp.zeros((1, 128), F32).at[0, :4 * H_ML].set(b_if.reshape(-1))
    ona, pf, qk, hv, gates, gates_t = _projection(x_lat, x_ctx, ctx_tile, mod3, g_pre, wna, wpr, wlo, conv_ml,
                                                  conv_hy, bif_row, cos_t, sin_t, tri, nt)

    na_g = _attention(ona, pf, _na_bias_table(rpb), nt, update_ctx)

    hml = _mlstm_scan(qk, pf, gates, gates_t, emat, bmask, nt)

    k2, ssq = _hyena_filters(length, hf_w1, hf_b1, hf_w2, hf_b2, hf_w3, hf_freq)
    kf = _filter_spectrum(_normalised(k2, ssq, 1.0 / (2 * length)), length)
    hv4 = hv.reshape(nb // 2, 2, ltot, 768)
    y1 = _hyena_conv(hv4, 0, kf, 0, length).reshape(nb, length, D_HY)
    z1 = _hy_mid(y1, hv, hy_bias[0])
    y2 = _hyena_conv(z1.reshape(nb // 2, 2, length, D_HY), 0, kf, 1, length).reshape(nb, length, D_HY)
    hy_args = (y2, z1, hv, pf, hy_bias[1])
    to = 2 * TT
    if not update_ctx:
        return _out_projection(na_g, hml, pf, hy_args, w_out, x_lat, 0, mod3, g_post, to, 0, length // to,
                               lambda b: b, length)

    xn = _out_projection(na_g, hml, pf, hy_args, w_out, x_lat, 0, mod3, g_post, to, 0, length // to,
                         lambda b: b, ltot)
    k2c, ssq_c = _hyena_filters(lc, hf_w1, hf_b1, hf_w2, hf_b2, hf_w3, hf_freq)
    hyc_g = _hy_ctx(hv, pf, _normalised(k2c, ssq_c, 1.0), hy_bias, nt, lc)
    return _out_projection(na_g, hml, pf, (hyc_g,), w_out, x_ctx, ctx_tile, mod3, g_post, TT, nt, 1,
                           lambda b: nb, ltot, prev=xn)


def kernel(x, c, ctx, c_ctx, w_ada, b_ada, g_pre, g_post, w_in, b_if, conv_ml, conv_hy, rpb, hf_w1, hf_b1,
           hf_w2, hf_b2, hf_w3, hf_freq, hy_bias, w_out):
    nb, length, d = x.shape
    lc = ctx.shape[1]
    depth = w_in.shape[0]
    assert lc == TT and length % (4 * TT) == 0 and nb % 2 == 0 and nb % SCAN_NB == 0 and d == D_MODEL
    cc = jnp.zeros((16, d), F32).at[:nb].set(c).at[nb].set(c_ctx)
    x_lat, x_ctx, ctx_tile = x, ctx, 0
    for l in range(depth):
        mod3 = _modulation(cc, w_ada[l], b_ada[l]).reshape(16, 3, d)
        x_lat = _layer(x_lat, x_ctx, ctx_tile, mod3, nb, length, lc, g_pre[l], g_post[l], w_in[l], b_if[l],
                       conv_ml[l], conv_hy[l], rpb[l], hf_w1[l], hf_b1[l], hf_w2[l], hf_b2[l], hf_w3[l],
                       hf_freq[l], hy_bias[l], w_out[l].astype(BF16), l < depth - 1)
        x_ctx, ctx_tile = x_lat, length // TT
    return x_lat
```

```python
import functools
import math

import numpy as np
import jax
import jax.numpy as jnp
from jax import lax
from jax.experimental import pallas as pl
from jax.experimental.pallas import tpu as pltpu

F32 = jnp.float32
BF16 = jnp.bfloat16

D_MODEL = 1024
GRID_W = 64
HEAD_DIM = 64
D_NA = 512
D_ML = 256
D_HY = 256
H_NA = 8
H_ML = 4
NA_ROWS = 8
NA_COLS = 16
HY_EMB = 33
HY_FFN = 64
HY_TARGET = 1e-2
HY_FAST = 0.3
HY_SLOW = 1.5
ROPE_BASE = 10000.0
EPS = 1e-6

TT = 256
BAND_ROWS = TT // GRID_W
SCAN_NB = 4
PRE_ROWS = 64
FFT_N2 = 128
FFT_PAD = 8
FFT_UNROLL = 8
NEG = -1e30
BIG = 3e38
LOG2E = math.log2(math.e)
VMEM_LIMIT = 56 * 1024 * 1024
FFTCONV_VMEM_LIMIT = 59 * 1024 * 1024

PF_ZA = 0
PF_VM, PF_OM, PF_ZM, PF_ZH = 2, 3, 4, 5
LOC_QK, LOC_HY, LOC_GM, LOC_COLS = 0, 512, 1280, 1408
HALO = 8


def _dot(a, b):
    return jnp.dot(a, b, preferred_element_type=F32)


def _dot_nt(a, b):
    return lax.dot_general(a, b, (((1,), (1,)), ((), ())), preferred_element_type=F32)


def _dot_tn(a, b):
    return lax.dot_general(a, b, (((0,), (0,)), ((), ())), preferred_element_type=F32)


def _split3(x):
    h = x.astype(BF16)
    r = x - h.astype(F32)
    m = r.astype(BF16)
    l = (r - m.astype(F32)).astype(BF16)
    return h, m, l


def _dot_f32(a, b):
    ah, am, al = _split3(a)
    bh, bm, bl = _split3(b)
    return (_dot(ah, bh) + (_dot(ah, bm) + _dot(am, bh))
            + (_dot(ah, bl) + _dot(al, bh) + _dot(am, bm)))


def _dot_2x(a, b):
    ah, am, _ = _split3(a)
    bh, bm, _ = _split3(b)
    return _dot(ah, bh) + (_dot(ah, bm) + _dot(am, bh))


def _silu(x):
    return x * jax.nn.sigmoid(x)


def _params(sem, vmem=VMEM_LIMIT):
    return pltpu.CompilerParams(dimension_semantics=sem, vmem_limit_bytes=vmem)


def _mod_kernel(c_ref, w_ref, b_ref, o_ref):
    o_ref[...] = _dot_f32(_silu(c_ref[...]), w_ref[...]) + b_ref[...]


def _modulation(cc, w_ada, b_ada):
    n = w_ada.shape[1]
    bn = 512
    return pl.pallas_call(
        _mod_kernel, grid=(n // bn,),
        in_specs=[pl.BlockSpec(cc.shape, lambda j: (0, 0)),
                  pl.BlockSpec((w_ada.shape[0], bn), lambda j: (0, j)),
                  pl.BlockSpec((1, bn), lambda j: (0, j))],
        out_specs=pl.BlockSpec((cc.shape[0], bn), lambda j: (0, j)),
        out_shape=jax.ShapeDtypeStruct((cc.shape[0], n), F32),
        compiler_params=_params(("arbitrary",)), name="adaln_mod",
    )(cc, w_ada, b_ada.reshape(1, n))


def _proj_kernel(nt, x_ref, c_ref, xp_ref, xn_ref, mod_ref, g_ref, wna_ref, wpr_ref, wlo_ref, cml_ref, chy_ref,
                 bif_ref, cos_ref, sin_ref, tri_ref, ona_ref, opr_ref, oqk_ref, ohv_ref, og_ref, ogt_ref, loc_ref):
    t = pl.program_id(0)
    x = jnp.where(t == nt, c_ref[0], x_ref[0])
    xs = jnp.concatenate([xp_ref[0], x, xn_ref[0]], axis=0)
    ms = jnp.mean(xs * xs, axis=-1, keepdims=True)
    y = xs * lax.rsqrt(ms + EPS) * g_ref[...]
    mod = mod_ref[0]
    h = (y * (1.0 + mod[1:2]) + mod[0:1]).astype(BF16)
    hm = h[HALO:HALO + TT]
    loc_ref[...] = _dot(h, wlo_ref[...])

    g = loc_ref[HALO:HALO + TT, LOC_GM:LOC_COLS] + bif_ref[...]
    gl = lax.broadcasted_iota(jnp.int32, (1, 128), 1)
    is_f = jnp.logical_and((gl % 8) >= 4, gl < 16)
    lf = jnp.where(is_f, jnp.minimum(g, 0.0) - jnp.log1p(jnp.exp(-jnp.abs(g))), 0.0)
    hi, mid, lo = _split3(lf)
    packed = (hi.astype(F32) + pltpu.roll(mid.astype(F32), 16, 1) + pltpu.roll(lo.astype(F32), 32, 1)).astype(BF16)
    cs = _dot(tri_ref[...], packed)
    cs = cs + pltpu.roll(cs, 112, 1) + pltpu.roll(cs, 96, 1)
    bcum = jnp.where(gl < 8, cs[0:TT], cs[TT:2 * TT])
    a = g - pltpu.roll(bcum, 128 - H_ML, 1)
    at = jnp.where(is_f, 0.0, a).T[0:16]
    tlane = lax.broadcasted_iota(jnp.int32, (1, TT), 1)
    pre = suf = at
    shift = 1
    while shift < TT:
        pre = jnp.maximum(pre, jnp.where(tlane >= shift, pltpu.roll(pre, shift, 1), -BIG))
        suf = jnp.maximum(suf, jnp.where(tlane < TT - shift, pltpu.roll(suf, TT - shift, 1), -BIG))
        shift *= 2
    cm_t = jnp.where(lax.broadcasted_iota(jnp.int32, (16, 1), 0) < 8, pre, suf)
    cmax = jnp.concatenate([cm_t, jnp.zeros((128 - 16, TT), F32)], axis=0).T
    ab = jnp.where(is_f, bcum, a)
    og_ref[0, 0] = LOG2E * (jnp.where(gl < 8, ab, 0.0) + pltpu.roll(jnp.where(gl < H_ML, cmax, 0.0), 8, 1))
    og_ref[0, 1] = LOG2E * (pltpu.roll(jnp.where(jnp.logical_and(gl >= 8, gl < 16), ab, 0.0), 120, 1)
                            + jnp.where(jnp.logical_and(gl >= 8, gl < 8 + H_ML), cmax, 0.0))
    ogt_ref[0] = LOG2E * at

    ona_ref[0] = _dot(hm, wna_ref[...]).astype(BF16)

    pm = jnp.where(jnp.logical_and(t != 0, t != nt), 1.0, 0.0)
    nm = jnp.where(jnp.logical_and(t != nt - 1, t != nt), 1.0, 0.0)
    row8 = lax.broadcasted_iota(jnp.int32, (8, 1), 0)
    nchunk = TT // PRE_ROWS

    def conv3(cols, w, c):
        r0 = HALO + c * PRE_ROWS
        u = loc_ref[r0:r0 + PRE_ROWS, cols]
        up = loc_ref[r0 - 1:r0, cols]
        un = loc_ref[r0 + PRE_ROWS:r0 + PRE_ROWS + 1, cols]
        if c == 0:
            up = up * pm
        if c == nchunk - 1:
            un = un * nm
        rp = pltpu.roll(u, 1, 0)
        rn = pltpu.roll(u, PRE_ROWS - 1, 0)
        prev = jnp.concatenate([jnp.where(row8 == 0, up, rp[0:8]), rp[8:]], axis=0)
        nxt = jnp.concatenate([rn[:PRE_ROWS - 8], jnp.where(row8 == 7, un, rn[PRE_ROWS - 8:])], axis=0)
        return prev * w[0:1] + u * w[1:2] + nxt * w[2:3]

    lane = lax.broadcasted_iota(jnp.int32, (1, 2 * D_ML), 1)
    first = (lane % 32) < 16
    for c in range(nchunk):
        rows = slice(c * PRE_ROWS, (c + 1) * PRE_ROWS)
        ohv_ref[0, rows] = conv3(slice(LOC_HY, LOC_GM), chy_ref[...], c).astype(BF16)
        x = _silu(conv3(slice(LOC_QK, LOC_HY), cml_ref[...], c))
        partner = jnp.where(first, pltpu.roll(x, 2 * D_ML - 16, 1), pltpu.roll(x, 16, 1))
        oqk_ref[0, rows] = (x * cos_ref[rows] + partner * sin_ref[rows]).astype(BF16)

    opr_ref[0] = _dot(hm, wpr_ref[...]).astype(BF16)


def _projection(x_lat, x_ctx, ctx_tile, mod3, g_pre, wna, wpr, wlo, conv_ml, conv_hy, bif_row, cos_t, sin_t,
                tri, nt):
    nb, lat_rows, d = x_lat.shape
    ltot = (nt + 1) * TT
    ctx_row = nb
    last8 = lat_rows // HALO - 1
    r8 = TT // HALO

    def mod_idx(t, b):
        return (jnp.where(t == nt, ctx_row, b), 0, 0)

    const = lambda t, b: (0, 0)
    big = pl.Buffered(1)
    tok = lambda t, b: (b, t, 0)
    return pl.pallas_call(
        functools.partial(_proj_kernel, nt), grid=(nt + 1, nb),
        in_specs=[pl.BlockSpec((1, TT, d), lambda t, b: (b, jnp.minimum(t, nt - 1), 0)),
                  pl.BlockSpec((1, TT, d), lambda t, b: (b, ctx_tile, 0)),
                  pl.BlockSpec((1, HALO, d), lambda t, b: (b, jnp.clip(t * r8 - 1, 0, last8), 0)),
                  pl.BlockSpec((1, HALO, d), lambda t, b: (b, jnp.minimum((t + 1) * r8, last8), 0)),
                  pl.BlockSpec((1, 3, d), mod_idx),
                  pl.BlockSpec((1, d), const),
                  pl.BlockSpec(wna.shape, const, pipeline_mode=big),
                  pl.BlockSpec(wpr.shape, const, pipeline_mode=big),
                  pl.BlockSpec(wlo.shape, const, pipeline_mode=big),
                  pl.BlockSpec((3, 512), const), pl.BlockSpec((3, 768), const), pl.BlockSpec((1, 128), const),
                  pl.BlockSpec((TT, 512), lambda t, b: (t, 0)), pl.BlockSpec((TT, 512), lambda t, b: (t, 0)),
                  pl.BlockSpec((2 * TT, TT), const)],
        out_specs=[pl.BlockSpec((1, TT, wna.shape[1]), tok), pl.BlockSpec((1, TT, wpr.shape[1]), tok),
                   pl.BlockSpec((1, TT, 512), tok), pl.BlockSpec((1, TT, 768), tok),
                   pl.BlockSpec((1, 2, TT, 128), lambda t, b: (b, 0, t, 0)),
                   pl.BlockSpec((1, 16, TT), lambda t, b: (b, 0, t))],
        out_shape=[jax.ShapeDtypeStruct((nb, ltot, wna.shape[1]), BF16),
                   jax.ShapeDtypeStruct((nb, ltot, wpr.shape[1]), BF16),
                   jax.ShapeDtypeStruct((nb, ltot, 512), BF16),
                   jax.ShapeDtypeStruct((nb, ltot, 768), BF16),
                   jax.ShapeDtypeStruct((nb, 2, ltot, 128), F32),
                   jax.ShapeDtypeStruct((nb, 16, ltot), F32)],
        scratch_shapes=[pltpu.VMEM((TT + 2 * HALO, LOC_COLS), F32)],
        compiler_params=_params(("arbitrary", "arbitrary")), name="norm_in_proj_conv",
    )(x_lat, x_ctx, x_lat, x_lat, mod3, g_pre.reshape(1, d), wna, wpr, wlo, conv_ml, conv_hy, bif_row,
      cos_t, sin_t, tri)


def _na_kernel(q_ref, k0_ref, k1_ref, k2_ref, v0_ref, v1_ref, v2_ref, kc_ref, vc_ref, za_ref, tab_ref, o_ref):
    lane = lax.broadcasted_iota(jnp.int32, (1, 128), 1)
    k_refs = (k0_ref, k1_ref, k2_ref)
    v_refs = (v0_ref, v1_ref, v2_ref)

    def row_max(s):
        return jnp.max(jnp.maximum(s[:, :128], s[:, 128:]), axis=-1, keepdims=True)

    def head_mask(h):
        return (lane >= HEAD_DIM) if h % 2 else (lane < HEAD_DIM)

    def masked_q(h):
        sl = slice((h // 2) * 128, (h // 2 + 1) * 128)
        q2 = q_ref[0, :, sl]
        return jnp.where(head_mask(h), q2, jnp.zeros_like(q2))

    def score_block(h, qm, blk):
        sl = slice((h // 2) * 128, (h // 2 + 1) * 128)
        if blk == 0:
            return _dot_nt(qm, kc_ref[0, :, sl])
        return _dot_nt(qm, k_refs[blk - 1][0, :, sl]) + tab_ref[0, h, :, (blk - 1) * TT:blk * TT]

    def values(h, p):
        sl = slice((h // 2) * 128, (h // 2 + 1) * 128)
        hm = head_mask(h)
        v = jnp.concatenate([vc_ref[0, :, sl]] + [r[0, :, sl] for r in v_refs], axis=0)
        acc = _dot(p, jnp.where(hm, v, jnp.ones_like(v)))
        return jnp.where(hm, acc / pltpu.roll(acc, HEAD_DIM, 1), 0.0)

    outs = {}
    qm = masked_q(0)
    sc = [score_block(0, qm, blk) for blk in range(4)]
    for h in range(H_NA):
        m = row_max(jnp.maximum(jnp.maximum(sc[0], sc[1]), jnp.maximum(sc[2], sc[3])))
        qm = masked_q(h + 1) if h + 1 < H_NA else None
        nxt, ps = [], []
        for blk in range(4):
            if qm is not None:
                nxt.append(score_block(h + 1, qm, blk))
            ps.append(jnp.exp2(sc[blk] - m).astype(BF16))
        sc = nxt
        outs[h] = values(h, jnp.concatenate(ps, axis=1))
        if h % 2 == 1:
            sl = slice((h // 2) * 128, (h // 2 + 1) * 128)
            o_ref[0, :, sl] = ((outs[h - 1] + outs[h]) * _silu(za_ref[0, :, sl].astype(F32))).astype(BF16)


def _attention(ona, pf, table, nt, with_ctx_queries):
    nb, ltot, _ = ona.shape
    nj = nt + 1 if with_ctx_queries else nt

    def kv(col, dj):
        return lambda j, b: (b, jnp.clip(j + dj, 0, nt - 1), col)

    def pat(j, b):
        p = jnp.where(j == 0, 0, jnp.where(j == nt - 1, 2, jnp.where(j == nt, 3, 1)))
        return (p, 0, 0, 0)

    return pl.pallas_call(
        _na_kernel, grid=(nj, nb),
        in_specs=[pl.BlockSpec((1, TT, 512), lambda j, b: (b, j, 0)),
                  pl.BlockSpec((1, TT, 512), kv(1, -1)), pl.BlockSpec((1, TT, 512), kv(1, 0)),
                  pl.BlockSpec((1, TT, 512), kv(1, 1)),
                  pl.BlockSpec((1, TT, 512), kv(2, -1)), pl.BlockSpec((1, TT, 512), kv(2, 0)),
                  pl.BlockSpec((1, TT, 512), kv(2, 1)),
                  pl.BlockSpec((1, TT, 512), lambda j, b: (b, nt, 1)),
                  pl.BlockSpec((1, TT, 512), lambda j, b: (b, nt, 2)),
                  pl.BlockSpec((1, TT, 512), lambda j, b: (b, j, PF_ZA)),
                  pl.BlockSpec((1, H_NA, TT, 3 * TT), pat)],
        out_specs=pl.BlockSpec((1, TT, 512), lambda j, b: (b, j, 0)),
        out_shape=jax.ShapeDtypeStruct((nb, ltot, 512), BF16),
        compiler_params=_params(("arbitrary", "arbitrary")), name="nbr_attention",
    )(ona, ona, ona, ona, ona, ona, ona, ona, ona, pf, table)


def _na_bias_table(rpb):
    a = np.arange(TT) // GRID_W
    qc = np.arange(TT) % GRID_W
    kidx = np.arange(3 * TT)
    kr = (kidx // TT - 1) * BAND_ROWS + (kidx % TT) // GRID_W
    kcol = kidx % GRID_W
    qstart = np.clip(qc - NA_COLS // 2, 0, GRID_W - NA_COLS)
    col_ok = (kcol[None, :] >= qstart[:, None]) & (kcol[None, :] < qstart[:, None] + NA_COLS)
    coff = np.clip(kcol[None, :] - qc[:, None], -(NA_COLS - 1), NA_COLS - 1) + NA_COLS - 1
    roff = np.clip(kr[None, :] - a[:, None] + NA_ROWS - 1, 0, 2 * NA_ROWS - 2)
    half = NA_ROWS // 2
    row_ok = [
        (kr[None, :] >= 0) & (kr[None, :] < NA_ROWS) & (a[:, None] >= 0),
        (kr[None, :] >= a[:, None] - half) & (kr[None, :] < a[:, None] + half),
        (kr[None, :] >= BAND_ROWS - NA_ROWS) & (kr[None, :] < BAND_ROWS) & (a[:, None] >= 0),
    ]
    nr, ncol = 2 * NA_ROWS - 1, 2 * NA_COLS - 1
    onehot = (coff[:GRID_W, :GRID_W].reshape(1, -1) == np.arange(ncol)[:, None]).astype(np.float32)
    cmat = jnp.dot(rpb.astype(F32).reshape(H_NA * nr, ncol), onehot,
                   precision=lax.Precision.HIGHEST).reshape(H_NA, nr, GRID_W, GRID_W)
    nkr = 3 * BAND_ROWS
    blocks = [jnp.concatenate([cmat[:, roff[ai * GRID_W, kj * GRID_W]] for kj in range(nkr)], axis=-1)
              for ai in range(BAND_ROWS)]
    bias = jnp.concatenate(blocks, axis=1)
    bias = bias * LOG2E
    tabs = [jnp.where((r & col_ok)[None], bias, NEG) for r in row_ok]
    tabs.append(jnp.full_like(bias, NEG))
    return jnp.stack(tabs, axis=0)


def _head_lane_sums(x, lane_h):
    out = jnp.zeros_like(x)
    half = lax.broadcasted_iota(jnp.int32, (1, 128), 1) // HEAD_DIM
    for h in range(H_ML):
        t = x[:, (h // 2) * 128:(h // 2 + 1) * 128]
        sm = jnp.sum(jnp.where(half == h % 2, t, 0.0), axis=-1, keepdims=True)
        out = jnp.where(lane_h == h, sm, out)
    return out


def _scan_kernel(q_ref, k_ref, v_ref, g_ref, gt_ref, e_ref, bm_ref, o_ref, w_ref, n_ref, m_ref):
    s = pl.program_id(2)
    sgn = 1 - 2 * pl.program_id(1)

    @pl.when(s == 0)
    def _():
        w_ref[...] = jnp.zeros_like(w_ref)
        n_ref[...] = jnp.zeros_like(n_ref)
        m_ref[...] = jnp.zeros_like(m_ref)

    lane_h = lax.broadcasted_iota(jnp.int32, (1, D_ML), 1) // HEAD_DIM
    ri = lax.broadcasted_iota(jnp.int32, (TT, TT), 0)
    ci = lax.broadcasted_iota(jnp.int32, (TT, TT), 1)
    tri = ((ci - ri) * sgn) <= 0
    gate_lanes = lax.broadcasted_iota(jnp.int32, (1, 128), 1) < H_ML

    seqs = range(SCAN_NB)
    gate, kws, kwfs, vbs, mrows = [], [], [], [], []
    for i in seqs:
        gi = g_ref[i, 0]
        a = jnp.where(gate_lanes, gi, 0.0)
        bb = jnp.where(gate_lanes, pltpu.roll(gi, 128 - H_ML, 1), 0.0)
        cmax = jnp.where(gate_lanes, pltpu.roll(gi, 128 - 2 * H_ML, 1), 0.0)
        m_prev = m_ref[i]
        mt = jnp.maximum(m_prev, cmax)
        mc = jnp.maximum(m_prev, jnp.max(a, axis=0, keepdims=True))
        small = jnp.concatenate([jnp.exp2(a - mc), jnp.exp2(-bb - mt), jnp.exp2(m_prev - mt),
                                 jnp.broadcast_to(jnp.exp2(m_prev - mc), (8, 128))], axis=0)
        sh = small.astype(BF16)
        sl = (small - sh.astype(F32)).astype(BF16)
        ex = _dot(sh, e_ref[...]) + _dot(sl, e_ref[...])
        gate.append((ex[TT:2 * TT], ex[2 * TT:3 * TT], ex[3 * TT:3 * TT + 1],
                     jnp.min(bb, axis=0, keepdims=True) + mc))
        mrows.append(mt)
        kwf = k_ref[i].astype(F32) * ex[0:TT]
        kwfs.append(kwf)
        kws.append(kwf.astype(BF16))
        vbs.append(v_ref[i].astype(BF16))

    scores = [[_dot_nt(jnp.where(lane_h == h, q_ref[i], jnp.zeros_like(q_ref[i])), k_ref[i])
               for h in range(H_ML)] for i in seqs]
    inters = [_dot(q_ref[i], w_ref[i].astype(BF16)) for i in seqs]
    updates = [_dot_tn(kws[i], vbs[i]) for i in seqs]

    def masked(i, h):
        arg = jnp.where(tri, gt_ref[i, h:h + 1, :] - mrows[i][:, h:h + 1], -BIG)
        sc = scores[i][h] * jnp.exp2(arg)
        return sc.astype(BF16), jnp.sum(sc, axis=-1, keepdims=True)

    order = [(i, h) for i in seqs for h in range(H_ML)]
    cur = masked(*order[0])
    intras, dens = {}, {}
    for n, (i, h) in enumerate(order):
        nxt = masked(*order[n + 1]) if n + 1 < len(order) else None
        scb, rs = cur
        part = _dot(scb, jnp.where(lane_h == h, vbs[i], jnp.zeros_like(vbs[i])))
        intras[i] = part if h == 0 else intras[i] + part
        dens[i] = jnp.where(lane_h == h, rs, dens[i] if h else jnp.zeros((TT, D_ML), F32))
        cur = nxt

    for i in seqs:
        eb_e, beta_t, decay, m_new = gate[i]
        intra, den_intra = intras[i], dens[i]
        num = beta_t * inters[i] + intra
        den = beta_t * _head_lane_sums(q_ref[i].astype(F32) * n_ref[i], lane_h) + den_intra
        o_ref[i, 0] = (num / jnp.maximum(jnp.abs(den), eb_e)).astype(BF16)

        w_ref[i] = decay * w_ref[i] + bm_ref[...] * updates[i]
        n_ref[i] = decay * n_ref[i] + jnp.sum(kwfs[i], axis=0, keepdims=True)
        m_ref[i] = m_new


def _mlstm_scan(qk, pf, gates, gates_t, emat, bmask, nt):
    nb, ltot, _ = qk.shape
    g = SCAN_NB

    def chunk(s, d):
        return jnp.where(s == 0, nt, jnp.where(d == 0, s - 1, nt - s))

    const = lambda b, d, s: (0, 0)
    return pl.pallas_call(
        _scan_kernel, grid=(nb // g, 2, nt + 1),
        in_specs=[pl.BlockSpec((g, TT, 256), lambda b, d, s: (b, chunk(s, d), 0)),
                  pl.BlockSpec((g, TT, 256), lambda b, d, s: (b, chunk(s, d), 1)),
                  pl.BlockSpec((g, TT, 256), lambda b, d, s: (b, chunk(s, d), PF_VM)),
                  pl.BlockSpec((g, 1, TT, 128), lambda b, d, s: (b, d, chunk(s, d), 0)),
                  pl.BlockSpec((g, 8, TT), lambda b, d, s: (b, d, chunk(s, d))),
                  pl.BlockSpec((128, 256), const), pl.BlockSpec((256, 256), const)],
        out_specs=pl.BlockSpec((g, 1, TT, 256), lambda b, d, s: (b, d, chunk(s, d), 0)),
        out_shape=jax.ShapeDtypeStruct((nb, 2, ltot, 256), BF16),
        scratch_shapes=[pltpu.VMEM((g, D_ML, D_ML), F32), pltpu.VMEM((g, 1, D_ML), F32),
                        pltpu.VMEM((g, 1, 128), F32)],
        compiler_params=_params(("arbitrary", "arbitrary", "arbitrary")), name="mlstm_scan",
    )(qk, qk, pf, gates, gates_t, emat, bmask)


def _hyfilt_kernel(tl, length, feat_ref, w1_ref, b1_ref, w2_ref, b2_ref, w3_ref, fq_ref, dec_ref, o_ref, ss_ref):
    i = pl.program_id(0)
    th = tl // 2
    a = jnp.sin(fq_ref[0:1] * (_dot_2x(feat_ref[...], w1_ref[...]) + b1_ref[...]))
    a = jnp.sin(fq_ref[1:2] * (_dot_2x(a, w2_ref[...]) + b2_ref[...]))
    part = None
    for half in range(2):
        dec = dec_ref[half * th:(half + 1) * th]
        f = _dot_2x(a, w3_ref[half]) * jnp.concatenate([dec, dec, dec, dec], axis=1)
        fwd = jnp.concatenate([f[:, 0:D_HY], f[:, 2 * D_HY:3 * D_HY]], axis=1)
        bwd = jnp.concatenate([f[:, D_HY:2 * D_HY], f[:, 3 * D_HY:4 * D_HY]], axis=1)
        row = lax.broadcasted_iota(jnp.int32, (th, 1), 0) + (i * tl + half * th)
        k2 = jnp.where(row < length, fwd, jnp.where(row > length, bwd, 0.0))
        o_ref[half * th:(half + 1) * th] = k2
        sq = jnp.sum(k2 * k2, axis=0, keepdims=True)
        part = sq if part is None else part + sq

    @pl.when(i == 0)
    def _():
        ss_ref[...] = jnp.zeros_like(ss_ref)

    ss_ref[...] += part


def _hyena_filters(length, w1, b1, w2, b2, w3, freq):
    t = np.arange(length, dtype=np.float32)
    tn = (t / np.float32(length - 1)).astype(np.float32)
    bands = (HY_EMB - 1) // 2
    fr = np.linspace(1e-4, bands - 1, bands, dtype=np.float32)
    ang = (np.float32(2.0 * math.pi / length) * t[:, None] * fr[None, :]).astype(np.float32)
    feat = np.zeros((length, 128), np.float32)
    feat[:, 0] = tn
    feat[:, 1:1 + bands] = np.cos(ang)
    feat[:, 1 + bands:1 + 2 * bands] = -np.sin(ang)
    deltas = np.abs(np.linspace(math.log(HY_TARGET) / HY_SLOW, math.log(HY_TARGET) / HY_FAST, D_HY, dtype=np.float32))
    dec = np.exp(-tn[:, None] * deltas[None, :]).astype(np.float32)
    lag = np.concatenate([np.arange(length), [0], np.arange(length - 1, 0, -1)])
    feat, dec = feat[lag], dec[lag]

    tl = 512
    hw = 64
    fp = feat.reshape(2 * length // tl, 2, tl // 2, 128)[..., :hw]
    feat = np.concatenate([fp[:, 0], fp[:, 1]], axis=-1).reshape(length, 128)
    pad2 = lambda m, r, c: jnp.zeros((r, c), F32).at[:m.shape[0], :m.shape[1]].set(m.astype(F32))
    bdiag = lambda m: jnp.zeros((128, 128), F32).at[:hw, :hw].set(pad2(m, hw, hw)).at[hw:, hw:].set(pad2(m, hw, hw))
    twice = lambda v: jnp.concatenate([pad2(v, v.shape[0], hw)] * 2, axis=1)
    w1p = bdiag(w1)
    w2p = bdiag(w2)
    w3h = pad2(w3, hw, 4 * D_HY)
    w3p = jnp.stack([jnp.concatenate([w3h, jnp.zeros_like(w3h)], axis=0),
                     jnp.concatenate([jnp.zeros_like(w3h), w3h], axis=0)])
    b1p = twice(b1[None])
    b2p = twice(b2[None])
    fqp = twice(freq)
    const = lambda i: (0, 0)
    return pl.pallas_call(
        functools.partial(_hyfilt_kernel, tl, length), grid=(2 * length // tl,),
        in_specs=[pl.BlockSpec((tl // 2, 128), lambda i: (i, 0)),
                  pl.BlockSpec((128, 128), const), pl.BlockSpec((1, 128), const),
                  pl.BlockSpec((128, 128), const), pl.BlockSpec((1, 128), const),
                  pl.BlockSpec((2, 128, 4 * D_HY), lambda i: (0, 0, 0)), pl.BlockSpec((2, 128), const),
                  pl.BlockSpec((tl, D_HY), lambda i: (i, 0))],
        out_specs=[pl.BlockSpec((tl, 2 * D_HY), lambda i: (i, 0)), pl.BlockSpec((1, 2 * D_HY), const)],
        out_shape=[jax.ShapeDtypeStruct((2 * length, 2 * D_HY), F32), jax.ShapeDtypeStruct((1, 2 * D_HY), F32)],
        compiler_params=_params(("arbitrary",)), name="hyena_filter_mlp",
    )(feat, w1p, b1p, w2p, b2p, w3p, fqp, dec)


@functools.lru_cache(maxsize=None)
def _fft_tables(length):
    n = 2 * length
    n1 = n // FFT_N2
    nh = n1 // 2
    k1 = np.arange(n1)[None, :, None].astype(np.float64)
    j = np.arange(nh)[None, None, :].astype(np.float64)
    n2 = np.arange(FFT_N2)[:, None, None].astype(np.float64)
    ang = -2.0 * np.pi * k1 * (FFT_N2 * j + n2) / n
    gf = np.concatenate([np.cos(ang), np.sin(ang)], axis=1)
    gi = np.concatenate([np.cos(ang).transpose(0, 2, 1), np.sin(ang).transpose(0, 2, 1)], axis=1)
    kk = np.arange(FFT_N2)[:, None] * np.arange(FFT_N2)[None, :]
    a2 = -2.0 * np.pi * kk / FFT_N2
    f2 = np.concatenate([np.cos(a2), np.sin(a2)], axis=0)
    cv = lambda m: np.asarray(m, np.float32).astype(BF16)
    return cv(gf), cv(gi), cv(f2), n1


def _stage1(n1, z_of, gf_ref, scr_ref, cols):
    def group(gidx, c):
        n2s = [gidx * FFT_UNROLL + u for u in range(FFT_UNROLL)]
        zs = [z_of(n2) for n2 in n2s]
        rs = [_dot(gf_ref[n2], z) for n2, z in zip(n2s, zs)]
        for n2, r in zip(n2s, rs):
            off = pl.multiple_of(n2 * (n1 + FFT_PAD), 8)
            if cols == 256:
                scr_ref[0, pl.ds(off, n1), :] = r[:n1, :128] - r[n1:, 128:]
                scr_ref[1, pl.ds(off, n1), :] = r[n1:, :128] + r[:n1, 128:]
            else:
                scr_ref[0, pl.ds(off, n1), :] = r[:n1]
                scr_ref[1, pl.ds(off, n1), :] = r[n1:]
        return c

    lax.fori_loop(0, FFT_N2 // FFT_UNROLL, group, 0)


def _stage2(n1, k1, f2_ref, scr_ref):
    ar = scr_ref[0, pl.ds(k1, FFT_N2, stride=n1 + FFT_PAD), :]
    ai = scr_ref[1, pl.ds(k1, FFT_N2, stride=n1 + FFT_PAD), :]
    r = _dot(f2_ref[...], jnp.concatenate([ar, ai], axis=1).astype(BF16))
    return r[:128, :128] - r[128:, 128:], r[128:, :128] + r[:128, 128:]


def _hyconv_kernel(n1, kg, gated, z_ref, gf_ref, f2_ref, kf_ref, gi_ref, *rest):
    if gated:
        x1_ref, b_ref, o_ref, scr_ref, tp_ref = rest
    else:
        o_ref, scr_ref, tp_ref = rest
    s = pl.program_id(2)
    nh = n1 // 2
    pitch = FFT_N2 + FFT_PAD

    @pl.when(s == 0)
    def _():
        for e in range(2):
            for j in range(nh):
                tp_ref[e, j * pitch:j * pitch + FFT_N2, :] = z_ref[0, e, j * FFT_N2:(j + 1) * FFT_N2, :].astype(F32)

        def z_of(n2):
            zr = tp_ref[0, pl.ds(n2, nh, stride=pitch), :]
            zi = tp_ref[1, pl.ds(n2, nh, stride=pitch), :]
            return jnp.concatenate([zr, zi], axis=1).astype(BF16)
        _stage1(n1, z_of, gf_ref, scr_ref, 256)

    def k1_group(gidx, c):
        idx = [gidx * FFT_UNROLL + u for u in range(FFT_UNROLL)]
        xs = [_stage2(n1, s * kg + i, f2_ref, scr_ref) for i in idx]
        ys = []
        for i, (xr, xi) in zip(idx, xs):
            off = pl.multiple_of(i * FFT_N2, FFT_N2)
            kr = kf_ref[0, 0, pl.ds(off, FFT_N2), :]
            ki = kf_ref[0, 1, pl.ds(off, FFT_N2), :]
            ys.append(jnp.concatenate([xr * kr - xi * ki, xr * ki + xi * kr], axis=1).astype(BF16))
        rs = [_dot(f2_ref[...], yc) for yc in ys]
        for i, r in zip(idx, rs):
            k1 = s * kg + i
            scr_ref[0, pl.ds(k1, FFT_N2, stride=n1 + FFT_PAD), :] = r[:128, :128] + r[128:, 128:]
            scr_ref[1, pl.ds(k1, FFT_N2, stride=n1 + FFT_PAD), :] = r[:128, 128:] - r[128:, :128]
        return c

    lax.fori_loop(0, kg // FFT_UNROLL, k1_group, 0)

    @pl.when(s == pl.num_programs(2) - 1)
    def _():
        def group(gidx, c):
            n2s = [gidx * FFT_UNROLL + u for u in range(FFT_UNROLL)]
            bcs = []
            for n2 in n2s:
                off = pl.multiple_of(n2 * (n1 + FFT_PAD), 8)
                bcs.append(jnp.concatenate([scr_ref[0, pl.ds(off, n1), :], scr_ref[1, pl.ds(off, n1), :]],
                                           axis=1).astype(BF16))
            rs = [_dot(gi_ref[n2], bc) for n2, bc in zip(n2s, bcs)]
            for n2, r in zip(n2s, rs):
                tp_ref[0, pl.ds(n2, nh, stride=pitch), :] = r[:nh, :128] + r[nh:, 128:]
                tp_ref[1, pl.ds(n2, nh, stride=pitch), :] = r[:nh, 128:] - r[nh:, :128]
            return c
        lax.fori_loop(0, FFT_N2 // FFT_UNROLL, group, 0)
        for e in range(2):
            for j in range(nh):
                rows = slice(j * FFT_N2, (j + 1) * FFT_N2)
                y = tp_ref[e, j * pitch:j * pitch + FFT_N2, :]
                if gated:
                    y = x1_ref[0, e, rows, :].astype(F32) * (y + z_ref[0, e, rows, :].astype(F32) * b_ref[...])
                o_ref[0, e, rows, :] = y.astype(BF16)


def _hyena_conv(z4, kf, order, length, x1_col0=None, bias=None):
    gf, gi, f2, n1 = _fft_tables(length)
    n = 2 * length
    kg = min(n1, 2 * FFT_UNROLL)
    npair = z4.shape[0]
    big = pl.Buffered(1)
    gated = bias is not None
    in_specs = [pl.BlockSpec((1, 2, length, 128), lambda p, c, s: (p, 0, 0, c)),
                pl.BlockSpec(gf.shape, lambda p, c, s: (0, 0, 0), pipeline_mode=big),
                pl.BlockSpec(f2.shape, lambda p, c, s: (0, 0)),
                pl.BlockSpec((1, 2, kg * FFT_N2, 128), lambda p, c, s: (order, 0, s, c)),
                pl.BlockSpec(gi.shape, lambda p, c, s: (0, 0, 0), pipeline_mode=big)]
    args = [z4, gf, f2, kf, gi]
    if gated:
        in_specs += [pl.BlockSpec((1, 2, length, 128), lambda p, c, s: (p, 0, 0, x1_col0 + c), pipeline_mode=big),
                     pl.BlockSpec((1, 128), lambda p, c, s: (0, c))]
        args += [z4, bias.reshape(1, D_HY)]
    return pl.pallas_call(
        functools.partial(_hyconv_kernel, n1, kg, gated), grid=(npair, 2, n1 // kg),
        in_specs=in_specs,
        out_specs=pl.BlockSpec((1, 2, length, 128), lambda p, c, s: (p, 0, 0, c),
                               pipeline_mode=pl.Buffered(1 if gated else 2)),
        out_shape=jax.ShapeDtypeStruct((npair, 2, length, 256), BF16),
        scratch_shapes=[pltpu.VMEM((2, FFT_N2 * (n1 + FFT_PAD), 128), F32),
                        pltpu.VMEM((2, (n1 // 2) * (FFT_N2 + FFT_PAD), 128), F32)],
        compiler_params=_params(("arbitrary", "arbitrary", "arbitrary"), FFTCONV_VMEM_LIMIT), name="hyena_fftconv",
    )(*args)


def _hyfft_kernel(n1, kg, z_ref, gf_ref, f2_ref, o_ref, scr_ref, tp_ref):
    s = pl.program_id(2)
    pitch = FFT_N2 + FFT_PAD

    @pl.when(s == 0)
    def _():
        for j in range(n1):
            tp_ref[j * pitch:j * pitch + FFT_N2, :] = z_ref[j * FFT_N2:(j + 1) * FFT_N2, :]
        _stage1(n1, lambda n2: tp_ref[pl.ds(n2, n1, stride=pitch), :].astype(BF16), gf_ref, scr_ref, 128)

    def k1_group(gidx, c):
        idx = [gidx * FFT_UNROLL + u for u in range(FFT_UNROLL)]
        xs = [_stage2(n1, s * kg + i, f2_ref, scr_ref) for i in idx]
        for i, (xr, xi) in zip(idx, xs):
            off = pl.multiple_of(i * FFT_N2, FFT_N2)
            o_ref[0, 0, pl.ds(off, FFT_N2), :] = xr
            o_ref[0, 1, pl.ds(off, FFT_N2), :] = xi
        return c

    lax.fori_loop(0, kg // FFT_UNROLL, k1_group, 0)


@functools.lru_cache(maxsize=None)
def _fft_table_full(length):
    n = 2 * length
    n1 = n // FFT_N2
    k1 = np.arange(n1)[None, :, None].astype(np.float64)
    j = np.arange(n1)[None, None, :].astype(np.float64)
    n2 = np.arange(FFT_N2)[:, None, None].astype(np.float64)
    ang = -2.0 * np.pi * k1 * (FFT_N2 * j + n2) / n
    return np.concatenate([np.cos(ang), np.sin(ang)], axis=1).astype(np.float32).astype(BF16)


def _filter_spectrum(k2, length):
    _, _, f2, n1 = _fft_tables(length)
    gf = _fft_table_full(length)
    n = 2 * length
    kg = min(n1, 16)
    nf = k2.shape[1] // D_HY
    big = pl.Buffered(1)
    return pl.pallas_call(
        functools.partial(_hyfft_kernel, n1, kg), grid=(nf, 2, n1 // kg),
        in_specs=[pl.BlockSpec((n, 128), lambda f, c, s: (0, 2 * f + c), pipeline_mode=big),
                  pl.BlockSpec(gf.shape, lambda f, c, s: (0, 0, 0), pipeline_mode=big),
                  pl.BlockSpec(f2.shape, lambda f, c, s: (0, 0))],
        out_specs=pl.BlockSpec((1, 2, kg * FFT_N2, 128), lambda f, c, s: (f, 0, s, c)),
        out_shape=jax.ShapeDtypeStruct((nf, 2, n, 256), F32),
        scratch_shapes=[pltpu.VMEM((2, FFT_N2 * (n1 + FFT_PAD), 128), F32),
                        pltpu.VMEM((n1 * (FFT_N2 + FFT_PAD), 128), F32)],
        compiler_params=_params(("arbitrary", "arbitrary", "arbitrary")), name="hyena_filter_fft",
    )(k2, gf, f2)


def _normalised(k2, ssq, extra_scale):
    return k2 * (lax.rsqrt(ssq + EPS) * extra_scale)


def _hy_ctx_kernel(lc, hv_ref, zh_ref, k2_ref, b_ref, fc_ref, ff_ref, ci_ref, o_ref):
    nf = 2 * lc
    hv = hv_ref[0].astype(F32)

    def conv(z, o):
        kf = _dot_exact_rhs(ff_ref[...], k2_ref[:, o * D_HY:(o + 1) * D_HY])
        zs = _dot(fc_ref[...], z.astype(BF16))
        zr, zi = zs[:nf], zs[nf:]
        kr, ki = kf[:nf], kf[nf:]
        ys = jnp.concatenate([zr * kr - zi * ki, zr * ki + zi * kr], axis=0).astype(BF16)
        return _dot(ci_ref[...], ys) * (1.0 / nf)

    v = hv[:, 0:D_HY]
    x1 = hv[:, D_HY:2 * D_HY]
    x2 = hv[:, 2 * D_HY:3 * D_HY]
    z1 = x1 * (conv(v, 0) + v * b_ref[0:1])
    p = x2 * (conv(z1, 1) + z1 * b_ref[1:2])
    o_ref[0] = (p * _silu(zh_ref[0].astype(F32))).astype(BF16)


def _dot_exact_rhs(a_bf16, b):
    bh, bm, bl = _split3(b)
    return _dot(a_bf16, bh) + _dot(a_bf16, bm) + _dot(a_bf16, bl)


@functools.lru_cache(maxsize=None)
def _ctx_dft_tables(lc):
    nf = 2 * lc
    k = np.arange(nf)[:, None].astype(np.float64)
    ang = -2.0 * np.pi * k * np.arange(nf)[None, :] / nf
    ff = np.concatenate([np.cos(ang), np.sin(ang)], axis=0)
    fc = ff[:, :lc]
    a2 = 2.0 * np.pi * np.arange(lc)[:, None] * np.arange(nf)[None, :] / nf
    ci = np.concatenate([np.cos(a2), -np.sin(a2)], axis=1)
    cv = lambda m: np.asarray(m, np.float32).astype(BF16)
    return cv(fc), cv(ff), cv(ci)


def _hy_ctx(hv, pf, k2c, hy_bias, nt, lc):
    nb = hv.shape[0]
    fc, ff, ci = _ctx_dft_tables(lc)
    c2 = lambda b: (0, 0)
    return pl.pallas_call(
        functools.partial(_hy_ctx_kernel, lc), grid=(nb,),
        in_specs=[pl.BlockSpec((1, lc, 768), lambda b: (b, nt, 0)),
                  pl.BlockSpec((1, lc, D_HY), lambda b: (b, nt, PF_ZH)),
                  pl.BlockSpec(k2c.shape, c2),
                  pl.BlockSpec((2, D_HY), c2),
                  pl.BlockSpec(fc.shape, c2), pl.BlockSpec(ff.shape, c2), pl.BlockSpec(ci.shape, c2)],
        out_specs=pl.BlockSpec((1, lc, D_HY), lambda b: (b, 0, 0)),
        out_shape=jax.ShapeDtypeStruct((nb, lc, D_HY), BF16),
        compiler_params=_params(("arbitrary",)), name="hyena_ctx",
    )(hv, pf, k2c, hy_bias, fc, ff, ci)


def _out_kernel(n_hy, na_ref, h0_ref, h1_ref, om_ref, zm_ref, w_ref, x_ref, mod_ref, g_ref, *rest):
    hy_refs, o_ref = rest[:n_hy], rest[n_hy]
    chunks = [slice(r, r + TT) for r in range(0, o_ref.shape[1], TT)]
    acts = []
    for rows in chunks:
        if n_hy == 1:
            hy = hy_refs[0][0, rows]
        else:
            y_ref, z_ref, x2_ref, zh_ref, b_ref = hy_refs
            z = z_ref[0, rows].astype(F32)
            hy = (x2_ref[0, rows].astype(F32) * (y_ref[0, rows].astype(F32) + z * b_ref[...])
                  * _silu(zh_ref[0, rows].astype(F32))).astype(BF16)
        hsum = h0_ref[0, 0, rows].astype(F32) + h1_ref[0, 0, rows].astype(F32)
        ml = (jax.nn.sigmoid(om_ref[0, rows].astype(F32)) * hsum * _silu(zm_ref[0, rows].astype(F32))).astype(BF16)
        acts.append((ml, hy))
    ys = [_dot(na_ref[0, rows], w_ref[0:D_NA]) + _dot(ml, w_ref[D_NA:D_NA + D_ML])
          + _dot(hy, w_ref[D_NA + D_ML:D_MODEL]) for rows, (ml, hy) in zip(chunks, acts)]
    for rows, y in zip(chunks, ys):
        ms = jnp.mean(y * y, axis=-1, keepdims=True)
        r = y * lax.rsqrt(ms + EPS) * g_ref[...]
        o_ref[0, rows] = x_ref[0, rows] + mod_ref[0][2:3] * r


def _out_projection(na_g, hml, pf, hy_args, w_out, res, res_t0, mod3, g_post, to, t0, ntiles, mod_row, out_rows,
                    prev=None):
    nb = na_g.shape[0]
    d = D_MODEL
    tok = lambda col: (lambda b, t: (b, t + t0, col))
    own = lambda col: (lambda b, t: (b, t, col))
    in_specs = [pl.BlockSpec((1, to, 512), tok(0)),
                pl.BlockSpec((1, 1, to, 256), lambda b, t: (b, 0, t + t0, 0)),
                pl.BlockSpec((1, 1, to, 256), lambda b, t: (b, 1, t + t0, 0)),
                pl.BlockSpec((1, to, 256), tok(PF_OM)), pl.BlockSpec((1, to, 256), tok(PF_ZM)),
                pl.BlockSpec((d, d), lambda b, t: (0, 0), pipeline_mode=pl.Buffered(1)),
                pl.BlockSpec((1, to, d), lambda b, t: (b, t + res_t0, 0)),
                pl.BlockSpec((1, 3, d), lambda b, t: (mod_row(b), 0, 0)),
                pl.BlockSpec((1, d), lambda b, t: (0, 0))]
    args = [na_g, hml, hml, pf, pf, w_out, res, mod3, g_post.reshape(1, d)]
    if len(hy_args) == 1:
        in_specs.append(pl.BlockSpec((1, to, D_HY), own(0)))
        args.append(hy_args[0])
    else:
        y2, z1, hv, pf_, bias1 = hy_args
        in_specs += [pl.BlockSpec((1, to, D_HY), own(0)), pl.BlockSpec((1, to, D_HY), own(0)),
                     pl.BlockSpec((1, to, D_HY), tok(2)), pl.BlockSpec((1, to, D_HY), tok(PF_ZH)),
                     pl.BlockSpec((1, D_HY), lambda b, t: (0, 0))]
        args += [y2, z1, hv, pf_, bias1.reshape(1, D_HY)]
    n_hy = len(hy_args)
    aliases = {}
    if prev is not None:
        in_specs.append(pl.BlockSpec(memory_space=pl.ANY))
        args.append(prev)
        aliases = {len(args) - 1: 0}
        kern = lambda *r: _out_kernel(n_hy, *r[:9 + n_hy], r[10 + n_hy])
    else:
        kern = functools.partial(_out_kernel, n_hy)
    return pl.pallas_call(
        kern, grid=(nb, ntiles), in_specs=in_specs,
        out_specs=pl.BlockSpec((1, to, d), tok(0)),
        out_shape=jax.ShapeDtypeStruct((nb, out_rows, d), F32),
        input_output_aliases=aliases,
        compiler_params=_params(("arbitrary", "arbitrary")), name="out_proj_norm_residual",
    )(*args)


@functools.lru_cache(maxsize=None)
def _rope_tables(length, lc):
    n = HEAD_DIM // 4
    t = np.arange(length)
    inv = (ROPE_BASE ** (-np.arange(n, dtype=np.float32) / n)).astype(np.float32)
    pos = np.stack([t // GRID_W, t % GRID_W], axis=-1).astype(np.float32)
    ang = (pos[:, :, None] * inv).astype(np.float32)
    cos_h = np.concatenate([np.cos(ang[:, 0]), np.cos(ang[:, 0]), np.cos(ang[:, 1]), np.cos(ang[:, 1])], axis=-1)
    sin_h = np.concatenate([-np.sin(ang[:, 0]), np.sin(ang[:, 0]), -np.sin(ang[:, 1]), np.sin(ang[:, 1])], axis=-1)
    cos_t = np.concatenate([np.tile(cos_h, (1, 2 * H_ML)), np.ones((lc, 512), np.float32)], axis=0)
    sin_t = np.concatenate([np.tile(sin_h, (1, 2 * H_ML)), np.zeros((lc, 512), np.float32)], axis=0)
    kscale = np.concatenate([np.ones(256, np.float32), np.full(256, HEAD_DIM ** -0.5, np.float32)])
    return (cos_t * kscale).astype(np.float32), (sin_t * kscale).astype(np.float32)


@functools.lru_cache(maxsize=None)
def _scan_constants():
    tril = np.tril(np.ones((TT, TT), np.float32))
    emat = np.zeros((128, D_ML), np.float32)
    for h in range(H_ML):
        emat[h, h * HEAD_DIM:(h + 1) * HEAD_DIM] = 1.0
    rh = np.arange(D_ML)[:, None] // HEAD_DIM
    chd = np.arange(D_ML)[None, :] // HEAD_DIM
    bmask = (rh == chd).astype(np.float32)
    tri = np.concatenate([tril, tril.T], axis=0)
    return tri.astype(BF16), emat.astype(BF16), bmask


def _layer(x_lat, x_ctx, ctx_tile, mod3, nb, length, lc, g_pre, g_post, w_in, b_if, conv_ml, conv_hy, rpb,
           hf_w1, hf_b1, hf_w2, hf_b2, hf_w3, hf_freq, hy_bias, w_out, update_ctx):
    nt = length // TT
    ltot = length + lc
    tri, emat, bmask = _scan_constants()
    cos_t, sin_t = _rope_tables(length, lc)

    c = lambda a, b: w_in[:, a:b]
    wna = jnp.concatenate([c(0, 512) * (HEAD_DIM ** -0.5 * LOG2E), c(512, 1536)], axis=1).astype(BF16)
    wpr = jnp.concatenate([c(1536, 2048), c(2560, 2816), c(2816, 3072), c(3072, 3328), c(4112, 4368)],
                          axis=1).astype(BF16)
    wlo = jnp.concatenate([c(2048, 2560), c(3344, 4112), c(3328, 3344), jnp.zeros((D_MODEL, 112), F32)],
                          axis=1).astype(BF16)
    bif_row = jnp.zeros((1, 128), F32).at[0, :4 * H_ML].set(b_if.reshape(-1))
    ona, pf, qk, hv, gates, gates_t = _projection(x_lat, x_ctx, ctx_tile, mod3, g_pre, wna, wpr, wlo, conv_ml,
                                                  conv_hy, bif_row, cos_t, sin_t, tri, nt)

    na_g = _attention(ona, pf, _na_bias_table(rpb), nt, update_ctx)

    hml = _mlstm_scan(qk, pf, gates, gates_t, emat, bmask, nt)

    k2, ssq = _hyena_filters(length, hf_w1, hf_b1, hf_w2, hf_b2, hf_w3, hf_freq)
    kf = _filter_spectrum(_normalised(k2, ssq, 1.0 / (2 * length)), length)
    hv4 = hv.reshape(nb // 2, 2, ltot, 768)
    z1 = _hyena_conv(hv4, kf, 0, length, x1_col0=2, bias=hy_bias[0])
    y2 = _hyena_conv(z1, kf, 1, length).reshape(nb, length, D_HY)
    z1 = z1.reshape(nb, length, D_HY)
    hy_args = (y2, z1, hv, pf, hy_bias[1])
    to = 2 * TT
    if not update_ctx:
        return _out_projection(na_g, hml, pf, hy_args, w_out, x_lat, 0, mod3, g_post, to, 0, length // to,
                               lambda b: b, length)

    xn = _out_projection(na_g, hml, pf, hy_args, w_out, x_lat, 0, mod3, g_post, to, 0, length // to,
                         lambda b: b, ltot)
    k2c, ssq_c = _hyena_filters(lc, hf_w1, hf_b1, hf_w2, hf_b2, hf_w3, hf_freq)
    hyc_g = _hy_ctx(hv, pf, _normalised(k2c, ssq_c, 1.0), hy_bias, nt, lc)
    return _out_projection(na_g, hml, pf, (hyc_g,), w_out, x_ctx, ctx_tile, mod3, g_post, TT, nt, 1,
                           lambda b: nb, ltot, prev=xn)


def kernel(x, c, ctx, c_ctx, w_ada, b_ada, g_pre, g_post, w_in, b_if, conv_ml, conv_hy, rpb, hf_w1, hf_b1,
           hf_w2, hf_b2, hf_w3, hf_freq, hy_bias, w_out):
    nb, length, d = x.shape
    lc = ctx.shape[1]
    depth = w_in.shape[0]
    assert lc == TT and length % (4 * TT) == 0 and nb % 2 == 0 and nb % SCAN_NB == 0 and d == D_MODEL
    cc = jnp.zeros((16, d), F32).at[:nb].set(c).at[nb].set(c_ctx)
    x_lat, x_ctx, ctx_tile = x, ctx, 0
    for l in range(depth):
        mod3 = _modulation(cc, w_ada[l], b_ada[l]).reshape(16, 3, d)
        x_lat = _layer(x_lat, x_ctx, ctx_tile, mod3, nb, length, lc, g_pre[l], g_post[l], w_in[l], b_if[l],
                       conv_ml[l], conv_hy[l], rpb[l], hf_w1[l], hf_b1[l], hf_w2[l], hf_b2[l], hf_w3[l],
                       hf_freq[l], hy_bias[l], w_out[l].astype(BF16), l < depth - 1)
        x_ctx, ctx_tile = x_lat, length // TT
    return x_lat
```

```python
import functools
import math

import numpy as np
import jax
import jax.numpy as jnp
from jax import lax
from jax.experimental import pallas as pl
from jax.experimental.pallas import tpu as pltpu

F32 = jnp.float32
BF16 = jnp.bfloat16

D_MODEL = 1024
GRID_W = 64
HEAD_DIM = 64
D_NA = 512
D_ML = 256
D_HY = 256
H_NA = 8
H_ML = 4
NA_ROWS = 8
NA_COLS = 16
HY_EMB = 33
HY_FFN = 64
HY_TARGET = 1e-2
HY_FAST = 0.3
HY_SLOW = 1.5
ROPE_BASE = 10000.0
EPS = 1e-6

TT = 256
BAND_ROWS = TT // GRID_W
SCAN_NB = 4
PRE_ROWS = 64
FFT_N2 = 128
FFT_PAD = 8
FFT_UNROLL = 8
NEG = -1e30
BIG = 3e38
LOG2E = math.log2(math.e)
VMEM_LIMIT = 56 * 1024 * 1024
FFTCONV_VMEM_LIMIT = 59 * 1024 * 1024

PF_ZA = 0
PF_VM, PF_OM, PF_ZM, PF_ZH = 2, 3, 4, 5
LOC_QK, LOC_HY, LOC_GM, LOC_COLS = 0, 512, 1280, 1408
HALO = 8


def _dot(a, b):
    return jnp.dot(a, b, preferred_element_type=F32)


def _dot_nt(a, b):
    return lax.dot_general(a, b, (((1,), (1,)), ((), ())), preferred_element_type=F32)


def _dot_tn(a, b):
    return lax.dot_general(a, b, (((0,), (0,)), ((), ())), preferred_element_type=F32)


def _split3(x):
    h = x.astype(BF16)
    r = x - h.astype(F32)
    m = r.astype(BF16)
    l = (r - m.astype(F32)).astype(BF16)
    return h, m, l


def _dot_f32(a, b):
    ah, am, al = _split3(a)
    bh, bm, bl = _split3(b)
    return (_dot(ah, bh) + (_dot(ah, bm) + _dot(am, bh))
            + (_dot(ah, bl) + _dot(al, bh) + _dot(am, bm)))


def _dot_2x(a, b):
    ah, am, _ = _split3(a)
    bh, bm, _ = _split3(b)
    return _dot(ah, bh) + (_dot(ah, bm) + _dot(am, bh))


def _silu(x):
    return x * jax.nn.sigmoid(x)


def _params(sem, vmem=VMEM_LIMIT):
    return pltpu.CompilerParams(dimension_semantics=sem, vmem_limit_bytes=vmem)


def _mod_kernel(c_ref, w_ref, b_ref, o_ref):
    o_ref[...] = _dot_f32(_silu(c_ref[...]), w_ref[...]) + b_ref[...]


def _modulation(cc, w_ada, b_ada):
    n = w_ada.shape[1]
    bn = 512
    return pl.pallas_call(
        _mod_kernel, grid=(n // bn,),
        in_specs=[pl.BlockSpec(cc.shape, lambda j: (0, 0)),
                  pl.BlockSpec((w_ada.shape[0], bn), lambda j: (0, j)),
                  pl.BlockSpec((1, bn), lambda j: (0, j))],
        out_specs=pl.BlockSpec((cc.shape[0], bn), lambda j: (0, j)),
        out_shape=jax.ShapeDtypeStruct((cc.shape[0], n), F32),
        compiler_params=_params(("arbitrary",)), name="adaln_mod",
    )(cc, w_ada, b_ada.reshape(1, n))


def _proj_kernel(nt, x_ref, c_ref, xp_ref, xn_ref, mod_ref, g_ref, wna_ref, wpr_ref, wlo_ref, cml_ref, chy_ref,
                 bif_ref, cos_ref, sin_ref, tri_ref, ona_ref, opr_ref, oqk_ref, ohv_ref, og_ref, ogt_ref, loc_ref):
    t = pl.program_id(0)
    x = jnp.where(t == nt, c_ref[0], x_ref[0])
    xs = jnp.concatenate([xp_ref[0], x, xn_ref[0]], axis=0)
    ms = jnp.mean(xs * xs, axis=-1, keepdims=True)
    y = xs * lax.rsqrt(ms + EPS) * g_ref[...]
    mod = mod_ref[0]
    h = (y * (1.0 + mod[1:2]) + mod[0:1]).astype(BF16)
    hm = h[HALO:HALO + TT]
    loc_ref[...] = _dot(h, wlo_ref[...])
    ona_ref[0] = _dot(hm, wna_ref[...]).astype(BF16)

    g = loc_ref[HALO:HALO + TT, LOC_GM:LOC_COLS] + bif_ref[...]
    gl = lax.broadcasted_iota(jnp.int32, (1, 128), 1)
    is_f = jnp.logical_and((gl % 8) >= 4, gl < 16)
    lf = jnp.where(is_f, jnp.minimum(g, 0.0) - jnp.log1p(jnp.exp(-jnp.abs(g))), 0.0)
    hi, mid, lo = _split3(lf)
    packed = (hi.astype(F32) + pltpu.roll(mid.astype(F32), 16, 1) + pltpu.roll(lo.astype(F32), 32, 1)).astype(BF16)
    cs = _dot(tri_ref[...], packed)
    cs = cs + pltpu.roll(cs, 112, 1) + pltpu.roll(cs, 96, 1)
    bcum = jnp.where(gl < 8, cs[0:TT], cs[TT:2 * TT])
    a = g - pltpu.roll(bcum, 128 - H_ML, 1)
    at = jnp.where(is_f, 0.0, a).T[0:16]
    tlane = lax.broadcasted_iota(jnp.int32, (1, TT), 1)
    pre = suf = at
    shift = 1
    while shift < TT:
        pre = jnp.maximum(pre, jnp.where(tlane >= shift, pltpu.roll(pre, shift, 1), -BIG))
        suf = jnp.maximum(suf, jnp.where(tlane < TT - shift, pltpu.roll(suf, TT - shift, 1), -BIG))
        shift *= 2
    cm_t = jnp.where(lax.broadcasted_iota(jnp.int32, (16, 1), 0) < 8, pre, suf)
    cmax = jnp.concatenate([cm_t, jnp.zeros((128 - 16, TT), F32)], axis=0).T
    ab = jnp.where(is_f, bcum, a)
    og_ref[0, 0] = LOG2E * (jnp.where(gl < 8, ab, 0.0) + pltpu.roll(jnp.where(gl < H_ML, cmax, 0.0), 8, 1))
    og_ref[0, 1] = LOG2E * (pltpu.roll(jnp.where(jnp.logical_and(gl >= 8, gl < 16), ab, 0.0), 120, 1)
                            + jnp.where(jnp.logical_and(gl >= 8, gl < 8 + H_ML), cmax, 0.0))
    ogt_ref[0] = LOG2E * at
    opr_ref[0] = _dot(hm, wpr_ref[...]).astype(BF16)

    pm = jnp.where(jnp.logical_and(t != 0, t != nt), 1.0, 0.0)
    nm = jnp.where(jnp.logical_and(t != nt - 1, t != nt), 1.0, 0.0)
    row8 = lax.broadcasted_iota(jnp.int32, (8, 1), 0)
    nchunk = TT // PRE_ROWS

    def conv3(cols, w, c):
        r0 = HALO + c * PRE_ROWS
        u = loc_ref[r0:r0 + PRE_ROWS, cols]
        up = loc_ref[r0 - 1:r0, cols]
        un = loc_ref[r0 + PRE_ROWS:r0 + PRE_ROWS + 1, cols]
        if c == 0:
            up = up * pm
        if c == nchunk - 1:
            un = un * nm
        rp = pltpu.roll(u, 1, 0)
        rn = pltpu.roll(u, PRE_ROWS - 1, 0)
        prev = jnp.concatenate([jnp.where(row8 == 0, up, rp[0:8]), rp[8:]], axis=0)
        nxt = jnp.concatenate([rn[:PRE_ROWS - 8], jnp.where(row8 == 7, un, rn[PRE_ROWS - 8:])], axis=0)
        return prev * w[0:1] + u * w[1:2] + nxt * w[2:3]

    lane = lax.broadcasted_iota(jnp.int32, (1, 2 * D_ML), 1)
    first = (lane % 32) < 16
    for c in range(nchunk):
        rows = slice(c * PRE_ROWS, (c + 1) * PRE_ROWS)
        ohv_ref[0, rows] = conv3(slice(LOC_HY, LOC_GM), chy_ref[...], c).astype(BF16)
        x = _silu(conv3(slice(LOC_QK, LOC_HY), cml_ref[...], c))
        partner = jnp.where(first, pltpu.roll(x, 2 * D_ML - 16, 1), pltpu.roll(x, 16, 1))
        oqk_ref[0, rows] = (x * cos_ref[rows] + partner * sin_ref[rows]).astype(BF16)


def _projection(x_lat, x_ctx, ctx_tile, mod3, g_pre, wna, wpr, wlo, conv_ml, conv_hy, bif_row, cos_t, sin_t,
                tri, nt):
    nb, lat_rows, d = x_lat.shape
    ltot = (nt + 1) * TT
    ctx_row = nb
    last8 = lat_rows // HALO - 1
    r8 = TT // HALO

    def mod_idx(t, b):
        return (jnp.where(t == nt, ctx_row, b), 0, 0)

    const = lambda t, b: (0, 0)
    big = pl.Buffered(1)
    tok = lambda t, b: (b, t, 0)
    return pl.pallas_call(
        functools.partial(_proj_kernel, nt), grid=(nt + 1, nb),
        in_specs=[pl.BlockSpec((1, TT, d), lambda t, b: (b, jnp.minimum(t, nt - 1), 0)),
                  pl.BlockSpec((1, TT, d), lambda t, b: (b, ctx_tile, 0)),
                  pl.BlockSpec((1, HALO, d), lambda t, b: (b, jnp.clip(t * r8 - 1, 0, last8), 0)),
                  pl.BlockSpec((1, HALO, d), lambda t, b: (b, jnp.minimum((t + 1) * r8, last8), 0)),
                  pl.BlockSpec((1, 3, d), mod_idx),
                  pl.BlockSpec((1, d), const),
                  pl.BlockSpec(wna.shape, const, pipeline_mode=big),
                  pl.BlockSpec(wpr.shape, const, pipeline_mode=big),
                  pl.BlockSpec(wlo.shape, const, pipeline_mode=big),
                  pl.BlockSpec((3, 512), const), pl.BlockSpec((3, 768), const), pl.BlockSpec((1, 128), const),
                  pl.BlockSpec((TT, 512), lambda t, b: (t, 0)), pl.BlockSpec((TT, 512), lambda t, b: (t, 0)),
                  pl.BlockSpec((2 * TT, TT), const)],
        out_specs=[pl.BlockSpec((1, TT, wna.shape[1]), tok), pl.BlockSpec((1, TT, wpr.shape[1]), tok),
                   pl.BlockSpec((1, TT, 512), tok), pl.BlockSpec((1, TT, 768), tok),
                   pl.BlockSpec((1, 2, TT, 128), lambda t, b: (b, 0, t, 0)),
                   pl.BlockSpec((1, 16, TT), lambda t, b: (b, 0, t))],
        out_shape=[jax.ShapeDtypeStruct((nb, ltot, wna.shape[1]), BF16),
                   jax.ShapeDtypeStruct((nb, ltot, wpr.shape[1]), BF16),
                   jax.ShapeDtypeStruct((nb, ltot, 512), BF16),
                   jax.ShapeDtypeStruct((nb, ltot, 768), BF16),
                   jax.ShapeDtypeStruct((nb, 2, ltot, 128), F32),
                   jax.ShapeDtypeStruct((nb, 16, ltot), F32)],
        scratch_shapes=[pltpu.VMEM((TT + 2 * HALO, LOC_COLS), F32)],
        compiler_params=_params(("arbitrary", "arbitrary")), name="norm_in_proj_conv",
    )(x_lat, x_ctx, x_lat, x_lat, mod3, g_pre.reshape(1, d), wna, wpr, wlo, conv_ml, conv_hy, bif_row,
      cos_t, sin_t, tri)


def _na_kernel(q_ref, k0_ref, k1_ref, k2_ref, v0_ref, v1_ref, v2_ref, kc_ref, vc_ref, za_ref, tab_ref, o_ref):
    lane = lax.broadcasted_iota(jnp.int32, (1, 128), 1)
    k_refs = (k0_ref, k1_ref, k2_ref)
    v_refs = (v0_ref, v1_ref, v2_ref)

    def row_max(s):
        return jnp.max(jnp.maximum(s[:, :128], s[:, 128:]), axis=-1, keepdims=True)

    def head_mask(h):
        return (lane >= HEAD_DIM) if h % 2 else (lane < HEAD_DIM)

    def masked_q(h):
        sl = slice((h // 2) * 128, (h // 2 + 1) * 128)
        q2 = q_ref[0, :, sl]
        return jnp.where(head_mask(h), q2, jnp.zeros_like(q2))

    def score_block(h, qm, blk):
        sl = slice((h // 2) * 128, (h // 2 + 1) * 128)
        if blk == 0:
            return _dot_nt(qm, kc_ref[0, :, sl])
        return _dot_nt(qm, k_refs[blk - 1][0, :, sl]) + tab_ref[0, h, :, (blk - 1) * TT:blk * TT]

    def values(h, p):
        sl = slice((h // 2) * 128, (h // 2 + 1) * 128)
        hm = head_mask(h)
        v = jnp.concatenate([vc_ref[0, :, sl]] + [r[0, :, sl] for r in v_refs], axis=0)
        acc = _dot(p, jnp.where(hm, v, jnp.ones_like(v)))
        return jnp.where(hm, acc / pltpu.roll(acc, HEAD_DIM, 1), 0.0)

    outs = {}
    qm = masked_q(0)
    sc = [score_block(0, qm, blk) for blk in range(4)]
    for h in range(H_NA):
        m = row_max(jnp.maximum(jnp.maximum(sc[0], sc[1]), jnp.maximum(sc[2], sc[3])))
        qm = masked_q(h + 1) if h + 1 < H_NA else None
        nxt, ps = [], []
        for blk in range(4):
            if qm is not None:
                nxt.append(score_block(h + 1, qm, blk))
            ps.append(jnp.exp2(sc[blk] - m).astype(BF16))
        sc = nxt
        outs[h] = values(h, jnp.concatenate(ps, axis=1))
        if h % 2 == 1:
            sl = slice((h // 2) * 128, (h // 2 + 1) * 128)
            o_ref[0, :, sl] = ((outs[h - 1] + outs[h]) * _silu(za_ref[0, :, sl].astype(F32))).astype(BF16)


def _attention(ona, pf, table, nt, with_ctx_queries):
    nb, ltot, _ = ona.shape
    nj = nt + 1 if with_ctx_queries else nt

    def kv(col, dj):
        return lambda j, b: (b, jnp.clip(j + dj, 0, nt - 1), col)

    def pat(j, b):
        p = jnp.where(j == 0, 0, jnp.where(j == nt - 1, 2, jnp.where(j == nt, 3, 1)))
        return (p, 0, 0, 0)

    return pl.pallas_call(
        _na_kernel, grid=(nj, nb),
        in_specs=[pl.BlockSpec((1, TT, 512), lambda j, b: (b, j, 0)),
                  pl.BlockSpec((1, TT, 512), kv(1, -1)), pl.BlockSpec((1, TT, 512), kv(1, 0)),
                  pl.BlockSpec((1, TT, 512), kv(1, 1)),
                  pl.BlockSpec((1, TT, 512), kv(2, -1)), pl.BlockSpec((1, TT, 512), kv(2, 0)),
                  pl.BlockSpec((1, TT, 512), kv(2, 1)),
                  pl.BlockSpec((1, TT, 512), lambda j, b: (b, nt, 1)),
                  pl.BlockSpec((1, TT, 512), lambda j, b: (b, nt, 2)),
                  pl.BlockSpec((1, TT, 512), lambda j, b: (b, j, PF_ZA)),
                  pl.BlockSpec((1, H_NA, TT, 3 * TT), pat)],
        out_specs=pl.BlockSpec((1, TT, 512), lambda j, b: (b, j, 0)),
        out_shape=jax.ShapeDtypeStruct((nb, ltot, 512), BF16),
        compiler_params=_params(("arbitrary", "arbitrary")), name="nbr_attention",
    )(ona, ona, ona, ona, ona, ona, ona, ona, ona, pf, table)


def _na_bias_table(rpb):
    a = np.arange(TT) // GRID_W
    qc = np.arange(TT) % GRID_W
    kidx = np.arange(3 * TT)
    kr = (kidx // TT - 1) * BAND_ROWS + (kidx % TT) // GRID_W
    kcol = kidx % GRID_W
    qstart = np.clip(qc - NA_COLS // 2, 0, GRID_W - NA_COLS)
    col_ok = (kcol[None, :] >= qstart[:, None]) & (kcol[None, :] < qstart[:, None] + NA_COLS)
    coff = np.clip(kcol[None, :] - qc[:, None], -(NA_COLS - 1), NA_COLS - 1) + NA_COLS - 1
    roff = np.clip(kr[None, :] - a[:, None] + NA_ROWS - 1, 0, 2 * NA_ROWS - 2)
    half = NA_ROWS // 2
    row_ok = [
        (kr[None, :] >= 0) & (kr[None, :] < NA_ROWS) & (a[:, None] >= 0),
        (kr[None, :] >= a[:, None] - half) & (kr[None, :] < a[:, None] + half),
        (kr[None, :] >= BAND_ROWS - NA_ROWS) & (kr[None, :] < BAND_ROWS) & (a[:, None] >= 0),
    ]
    nr, ncol = 2 * NA_ROWS - 1, 2 * NA_COLS - 1
    onehot = (coff[:GRID_W, :GRID_W].reshape(1, -1) == np.arange(ncol)[:, None]).astype(np.float32)
    cmat = jnp.dot(rpb.astype(F32).reshape(H_NA * nr, ncol), onehot,
                   precision=lax.Precision.HIGHEST).reshape(H_NA, nr, GRID_W, GRID_W)
    nkr = 3 * BAND_ROWS
    blocks = [jnp.concatenate([cmat[:, roff[ai * GRID_W, kj * GRID_W]] for kj in range(nkr)], axis=-1)
              for ai in range(BAND_ROWS)]
    bias = jnp.concatenate(blocks, axis=1)
    bias = bias * LOG2E
    tabs = [jnp.where((r & col_ok)[None], bias, NEG) for r in row_ok]
    tabs.append(jnp.full_like(bias, NEG))
    return jnp.stack(tabs, axis=0)


def _head_lane_sums(x, lane_h):
    out = jnp.zeros_like(x)
    half = lax.broadcasted_iota(jnp.int32, (1, 128), 1) // HEAD_DIM
    for h in range(H_ML):
        t = x[:, (h // 2) * 128:(h // 2 + 1) * 128]
        sm = jnp.sum(jnp.where(half == h % 2, t, 0.0), axis=-1, keepdims=True)
        out = jnp.where(lane_h == h, sm, out)
    return out


def _scan_kernel(q_ref, k_ref, v_ref, g_ref, gt_ref, e_ref, bm_ref, o_ref, w_ref, n_ref, m_ref):
    s = pl.program_id(2)
    sgn = 1 - 2 * pl.program_id(1)

    @pl.when(s == 0)
    def _():
        w_ref[...] = jnp.zeros_like(w_ref)
        n_ref[...] = jnp.zeros_like(n_ref)
        m_ref[...] = jnp.zeros_like(m_ref)

    lane_h = lax.broadcasted_iota(jnp.int32, (1, D_ML), 1) // HEAD_DIM
    ri = lax.broadcasted_iota(jnp.int32, (TT, TT), 0)
    ci = lax.broadcasted_iota(jnp.int32, (TT, TT), 1)
    tri = ((ci - ri) * sgn) <= 0
    gate_lanes = lax.broadcasted_iota(jnp.int32, (1, 128), 1) < H_ML

    seqs = range(SCAN_NB)
    gate, kws, kwfs, vbs, mrows = [], [], [], [], []
    for i in seqs:
        gi = g_ref[i, 0]
        a = jnp.where(gate_lanes, gi, 0.0)
        bb = jnp.where(gate_lanes, pltpu.roll(gi, 128 - H_ML, 1), 0.0)
        cmax = jnp.where(gate_lanes, pltpu.roll(gi, 128 - 2 * H_ML, 1), 0.0)
        m_prev = m_ref[i]
        mt = jnp.maximum(m_prev, cmax)
        mc = jnp.maximum(m_prev, jnp.max(a, axis=0, keepdims=True))
        small = jnp.concatenate([jnp.exp2(a - mc), jnp.exp2(-bb - mt), jnp.exp2(m_prev - mt),
                                 jnp.broadcast_to(jnp.exp2(m_prev - mc), (8, 128))], axis=0)
        sh = small.astype(BF16)
        sl = (small - sh.astype(F32)).astype(BF16)
        ex = _dot(sh, e_ref[...]) + _dot(sl, e_ref[...])
        gate.append((ex[TT:2 * TT], ex[2 * TT:3 * TT], ex[3 * TT:3 * TT + 1],
                     jnp.min(bb, axis=0, keepdims=True) + mc))
        mrows.append(mt)
        kwf = k_ref[i].astype(F32) * ex[0:TT]
        kwfs.append(kwf)
        kws.append(kwf.astype(BF16))
        vbs.append(v_ref[i].astype(BF16))

    scores = [[_dot_nt(jnp.where(lane_h == h, q_ref[i], jnp.zeros_like(q_ref[i])), k_ref[i])
               for h in range(H_ML)] for i in seqs]
    inters = [_dot(q_ref[i], w_ref[i].astype(BF16)) for i in seqs]
    updates = [_dot_tn(kws[i], vbs[i]) for i in seqs]

    def masked(i, h):
        arg = jnp.where(tri, gt_ref[i, h:h + 1, :] - mrows[i][:, h:h + 1], -BIG)
        sc = scores[i][h] * jnp.exp2(arg)
        return sc.astype(BF16), jnp.sum(sc, axis=-1, keepdims=True)

    order = [(i, h) for i in seqs for h in range(H_ML)]
    cur = masked(*order[0])
    intras, dens = {}, {}
    for n, (i, h) in enumerate(order):
        nxt = masked(*order[n + 1]) if n + 1 < len(order) else None
        scb, rs = cur
        part = _dot(scb, jnp.where(lane_h == h, vbs[i], jnp.zeros_like(vbs[i])))
        intras[i] = part if h == 0 else intras[i] + part
        dens[i] = jnp.where(lane_h == h, rs, dens[i] if h else jnp.zeros((TT, D_ML), F32))
        cur = nxt

    for i in seqs:
        eb_e, beta_t, decay, m_new = gate[i]
        intra, den_intra = intras[i], dens[i]
        num = beta_t * inters[i] + intra
        den = beta_t * _head_lane_sums(q_ref[i].astype(F32) * n_ref[i], lane_h) + den_intra
        o_ref[i, 0] = (num / jnp.maximum(jnp.abs(den), eb_e)).astype(BF16)

        w_ref[i] = decay * w_ref[i] + bm_ref[...] * updates[i]
        n_ref[i] = decay * n_ref[i] + jnp.sum(kwfs[i], axis=0, keepdims=True)
        m_ref[i] = m_new


def _mlstm_scan(qk, pf, gates, gates_t, emat, bmask, nt):
    nb, ltot, _ = qk.shape
    g = SCAN_NB

    def chunk(s, d):
        return jnp.where(s == 0, nt, jnp.where(d == 0, s - 1, nt - s))

    const = lambda b, d, s: (0, 0)
    return pl.pallas_call(
        _scan_kernel, grid=(nb // g, 2, nt + 1),
        in_specs=[pl.BlockSpec((g, TT, 256), lambda b, d, s: (b, chunk(s, d), 0)),
                  pl.BlockSpec((g, TT, 256), lambda b, d, s: (b, chunk(s, d), 1)),
                  pl.BlockSpec((g, TT, 256), lambda b, d, s: (b, chunk(s, d), PF_VM)),
                  pl.BlockSpec((g, 1, TT, 128), lambda b, d, s: (b, d, chunk(s, d), 0)),
                  pl.BlockSpec((g, 8, TT), lambda b, d, s: (b, d, chunk(s, d))),
                  pl.BlockSpec((128, 256), const), pl.BlockSpec((256, 256), const)],
        out_specs=pl.BlockSpec((g, 1, TT, 256), lambda b, d, s: (b, d, chunk(s, d), 0)),
        out_shape=jax.ShapeDtypeStruct((nb, 2, ltot, 256), BF16),
        scratch_shapes=[pltpu.VMEM((g, D_ML, D_ML), F32), pltpu.VMEM((g, 1, D_ML), F32),
                        pltpu.VMEM((g, 1, 128), F32)],
        compiler_params=_params(("arbitrary", "arbitrary", "arbitrary")), name="mlstm_scan",
    )(qk, qk, pf, gates, gates_t, emat, bmask)


def _hyfilt_kernel(tl, length, feat_ref, w1_ref, b1_ref, w2_ref, b2_ref, w3_ref, fq_ref, dec_ref, o_ref, ss_ref):
    i = pl.program_id(0)
    th = tl // 2
    a = jnp.sin(fq_ref[0:1] * (_dot_2x(feat_ref[...], w1_ref[...]) + b1_ref[...]))
    a = jnp.sin(fq_ref[1:2] * (_dot_2x(a, w2_ref[...]) + b2_ref[...]))
    part = None
    for half in range(2):
        dec = dec_ref[half * th:(half + 1) * th]
        f = _dot_2x(a, w3_ref[half]) * jnp.concatenate([dec, dec, dec, dec], axis=1)
        fwd = jnp.concatenate([f[:, 0:D_HY], f[:, 2 * D_HY:3 * D_HY]], axis=1)
        bwd = jnp.concatenate([f[:, D_HY:2 * D_HY], f[:, 3 * D_HY:4 * D_HY]], axis=1)
        row = lax.broadcasted_iota(jnp.int32, (th, 1), 0) + (i * tl + half * th)
        k2 = jnp.where(row < length, fwd, jnp.where(row > length, bwd, 0.0))
        o_ref[half * th:(half + 1) * th] = k2
        sq = jnp.sum(k2 * k2, axis=0, keepdims=True)
        part = sq if part is None else part + sq

    @pl.when(i == 0)
    def _():
        ss_ref[...] = jnp.zeros_like(ss_ref)

    ss_ref[...] += part


def _hyena_filters(length, w1, b1, w2, b2, w3, freq):
    t = np.arange(length, dtype=np.float32)
    tn = (t / np.float32(length - 1)).astype(np.float32)
    bands = (HY_EMB - 1) // 2
    fr = np.linspace(1e-4, bands - 1, bands, dtype=np.float32)
    ang = (np.float32(2.0 * math.pi / length) * t[:, None] * fr[None, :]).astype(np.float32)
    feat = np.zeros((length, 128), np.float32)
    feat[:, 0] = tn
    feat[:, 1:1 + bands] = np.cos(ang)
    feat[:, 1 + bands:1 + 2 * bands] = -np.sin(ang)
    deltas = np.abs(np.linspace(math.log(HY_TARGET) / HY_SLOW, math.log(HY_TARGET) / HY_FAST, D_HY, dtype=np.float32))
    dec = np.exp(-tn[:, None] * deltas[None, :]).astype(np.float32)
    lag = np.concatenate([np.arange(length), [0], np.arange(length - 1, 0, -1)])
    feat, dec = feat[lag], dec[lag]

    tl = 512
    hw = 64
    fp = feat.reshape(2 * length // tl, 2, tl // 2, 128)[..., :hw]
    feat = np.concatenate([fp[:, 0], fp[:, 1]], axis=-1).reshape(length, 128)
    pad2 = lambda m, r, c: jnp.zeros((r, c), F32).at[:m.shape[0], :m.shape[1]].set(m.astype(F32))
    bdiag = lambda m: jnp.zeros((128, 128), F32).at[:hw, :hw].set(pad2(m, hw, hw)).at[hw:, hw:].set(pad2(m, hw, hw))
    twice = lambda v: jnp.concatenate([pad2(v, v.shape[0], hw)] * 2, axis=1)
    w1p = bdiag(w1)
    w2p = bdiag(w2)
    w3h = pad2(w3, hw, 4 * D_HY)
    w3p = jnp.stack([jnp.concatenate([w3h, jnp.zeros_like(w3h)], axis=0),
                     jnp.concatenate([jnp.zeros_like(w3h), w3h], axis=0)])
    b1p = twice(b1[None])
    b2p = twice(b2[None])
    fqp = twice(freq)
    const = lambda i: (0, 0)
    return pl.pallas_call(
        functools.partial(_hyfilt_kernel, tl, length), grid=(2 * length // tl,),
        in_specs=[pl.BlockSpec((tl // 2, 128), lambda i: (i, 0)),
                  pl.BlockSpec((128, 128), const), pl.BlockSpec((1, 128), const),
                  pl.BlockSpec((128, 128), const), pl.BlockSpec((1, 128), const),
                  pl.BlockSpec((2, 128, 4 * D_HY), lambda i: (0, 0, 0)), pl.BlockSpec((2, 128), const),
                  pl.BlockSpec((tl, D_HY), lambda i: (i, 0))],
        out_specs=[pl.BlockSpec((tl, 2 * D_HY), lambda i: (i, 0)), pl.BlockSpec((1, 2 * D_HY), const)],
        out_shape=[jax.ShapeDtypeStruct((2 * length, 2 * D_HY), F32), jax.ShapeDtypeStruct((1, 2 * D_HY), F32)],
        compiler_params=_params(("arbitrary",)), name="hyena_filter_mlp",
    )(feat, w1p, b1p, w2p, b2p, w3p, fqp, dec)


@functools.lru_cache(maxsize=None)
def _fft_tables(length):
    n = 2 * length
    n1 = n // FFT_N2
    nh = n1 // 2
    k1 = np.arange(n1)[None, :, None].astype(np.float64)
    j = np.arange(nh)[None, None, :].astype(np.float64)
    n2 = np.arange(FFT_N2)[:, None, None].astype(np.float64)
    ang = -2.0 * np.pi * k1 * (FFT_N2 * j + n2) / n
    gf = np.concatenate([np.cos(ang), np.sin(ang)], axis=1)
    gi = np.concatenate([np.cos(ang).transpose(0, 2, 1), np.sin(ang).transpose(0, 2, 1)], axis=1)
    kk = np.arange(FFT_N2)[:, None] * np.arange(FFT_N2)[None, :]
    a2 = -2.0 * np.pi * kk / FFT_N2
    f2 = np.concatenate([np.cos(a2), np.sin(a2)], axis=0)
    cv = lambda m: np.asarray(m, np.float32).astype(BF16)
    return cv(gf), cv(gi), cv(f2), n1


def _stage1(n1, z_of, gf_ref, scr_ref, cols):
    def group(gidx, c):
        n2s = [gidx * FFT_UNROLL + u for u in range(FFT_UNROLL)]
        zs = [z_of(n2) for n2 in n2s]
        rs = [_dot(gf_ref[n2], z) for n2, z in zip(n2s, zs)]
        for n2, r in zip(n2s, rs):
            off = pl.multiple_of(n2 * (n1 + FFT_PAD), 8)
            if cols == 256:
                scr_ref[0, pl.ds(off, n1), :] = r[:n1, :128] - r[n1:, 128:]
                scr_ref[1, pl.ds(off, n1), :] = r[n1:, :128] + r[:n1, 128:]
            else:
                scr_ref[0, pl.ds(off, n1), :] = r[:n1]
                scr_ref[1, pl.ds(off, n1), :] = r[n1:]
        return c

    lax.fori_loop(0, FFT_N2 // FFT_UNROLL, group, 0)


def _stage2(n1, k1, f2_ref, scr_ref):
    ar = scr_ref[0, pl.ds(k1, FFT_N2, stride=n1 + FFT_PAD), :]
    ai = scr_ref[1, pl.ds(k1, FFT_N2, stride=n1 + FFT_PAD), :]
    r = _dot(f2_ref[...], jnp.concatenate([ar, ai], axis=1).astype(BF16))
    return r[:128, :128] - r[128:, 128:], r[128:, :128] + r[:128, 128:]


def _hyconv_kernel(n1, kg, gated, z_ref, gf_ref, f2_ref, kf_ref, gi_ref, *rest):
    if gated:
        x1_ref, b_ref, o_ref, scr_ref, tp_ref = rest
    else:
        o_ref, scr_ref, tp_ref = rest
    s = pl.program_id(2)
    nh = n1 // 2
    pitch = FFT_N2 + FFT_PAD

    @pl.when(s == 0)
    def _():
        for e in range(2):
            for j in range(nh):
                tp_ref[e, j * pitch:j * pitch + FFT_N2, :] = z_ref[0, e, j * FFT_N2:(j + 1) * FFT_N2, :].astype(F32)

        def z_of(n2):
            zr = tp_ref[0, pl.ds(n2, nh, stride=pitch), :]
            zi = tp_ref[1, pl.ds(n2, nh, stride=pitch), :]
            return jnp.concatenate([zr, zi], axis=1).astype(BF16)
        _stage1(n1, z_of, gf_ref, scr_ref, 256)

    def k1_group(gidx, c):
        idx = [gidx * FFT_UNROLL + u for u in range(FFT_UNROLL)]
        xs = [_stage2(n1, s * kg + i, f2_ref, scr_ref) for i in idx]
        ys = []
        for i, (xr, xi) in zip(idx, xs):
            off = pl.multiple_of(i * FFT_N2, FFT_N2)
            kr = kf_ref[0, 0, pl.ds(off, FFT_N2), :]
            ki = kf_ref[0, 1, pl.ds(off, FFT_N2), :]
            ys.append(jnp.concatenate([xr * kr - xi * ki, xr * ki + xi * kr], axis=1).astype(BF16))
        rs = [_dot(f2_ref[...], yc) for yc in ys]
        for i, r in zip(idx, rs):
            k1 = s * kg + i
            scr_ref[0, pl.ds(k1, FFT_N2, stride=n1 + FFT_PAD), :] = r[:128, :128] + r[128:, 128:]
            scr_ref[1, pl.ds(k1, FFT_N2, stride=n1 + FFT_PAD), :] = r[:128, 128:] - r[128:, :128]
        return c

    lax.fori_loop(0, kg // FFT_UNROLL, k1_group, 0)

    @pl.when(s == pl.num_programs(2) - 1)
    def _():
        def group(gidx, c):
            n2s = [gidx * FFT_UNROLL + u for u in range(FFT_UNROLL)]
            bcs = []
            for n2 in n2s:
                off = pl.multiple_of(n2 * (n1 + FFT_PAD), 8)
                bcs.append(jnp.concatenate([scr_ref[0, pl.ds(off, n1), :], scr_ref[1, pl.ds(off, n1), :]],
                                           axis=1).astype(BF16))
            rs = [_dot(gi_ref[n2], bc) for n2, bc in zip(n2s, bcs)]
            for n2, r in zip(n2s, rs):
                tp_ref[0, pl.ds(n2, nh, stride=pitch), :] = r[:nh, :128] + r[nh:, 128:]
                tp_ref[1, pl.ds(n2, nh, stride=pitch), :] = r[:nh, 128:] - r[nh:, :128]
            return c
        lax.fori_loop(0, FFT_N2 // FFT_UNROLL, group, 0)
        for e in range(2):
            for j in range(nh):
                rows = slice(j * FFT_N2, (j + 1) * FFT_N2)
                y = tp_ref[e, j * pitch:j * pitch + FFT_N2, :]
                if gated:
                    y = x1_ref[0, e, rows, :].astype(F32) * (y + z_ref[0, e, rows, :].astype(F32) * b_ref[...])
                o_ref[0, e, rows, :] = y.astype(BF16)


def _hyena_conv(z4, kf, order, length, x1_col0=None, bias=None):
    gf, gi, f2, n1 = _fft_tables(length)
    n = 2 * length
    kg = min(n1, 2 * FFT_UNROLL)
    npair = z4.shape[0]
    big = pl.Buffered(1)
    gated = bias is not None
    in_specs = [pl.BlockSpec((1, 2, length, 128), lambda p, c, s: (p, 0, 0, c)),
                pl.BlockSpec(gf.shape, lambda p, c, s: (0, 0, 0), pipeline_mode=big),
                pl.BlockSpec(f2.shape, lambda p, c, s: (0, 0)),
                pl.BlockSpec((1, 2, kg * FFT_N2, 128), lambda p, c, s: (order, 0, s, c)),
                pl.BlockSpec(gi.shape, lambda p, c, s: (0, 0, 0), pipeline_mode=big)]
    args = [z4, gf, f2, kf, gi]
    if gated:
        in_specs += [pl.BlockSpec((1, 2, length, 128), lambda p, c, s: (p, 0, 0, x1_col0 + c), pipeline_mode=big),
                     pl.BlockSpec((1, 128), lambda p, c, s: (0, c))]
        args += [z4, bias.reshape(1, D_HY)]
    return pl.pallas_call(
        functools.partial(_hyconv_kernel, n1, kg, gated), grid=(npair, 2, n1 // kg),
        in_specs=in_specs,
        out_specs=pl.BlockSpec((1, 2, length, 128), lambda p, c, s: (p, 0, 0, c),
                               pipeline_mode=pl.Buffered(1 if gated else 2)),
        out_shape=jax.ShapeDtypeStruct((npair, 2, length, 256), BF16),
        scratch_shapes=[pltpu.VMEM((2, FFT_N2 * (n1 + FFT_PAD), 128), F32),
                        pltpu.VMEM((2, (n1 // 2) * (FFT_N2 + FFT_PAD), 128), F32)],
        compiler_params=_params(("arbitrary", "arbitrary", "arbitrary"), FFTCONV_VMEM_LIMIT), name="hyena_fftconv",
    )(*args)


def _hyfft_kernel(n1, kg, z_ref, gf_ref, f2_ref, o_ref, scr_ref, tp_ref):
    s = pl.program_id(2)
    pitch = FFT_N2 + FFT_PAD

    @pl.when(s == 0)
    def _():
        for j in range(n1):
            tp_ref[j * pitch:j * pitch + FFT_N2, :] = z_ref[j * FFT_N2:(j + 1) * FFT_N2, :]
        _stage1(n1, lambda n2: tp_ref[pl.ds(n2, n1, stride=pitch), :].astype(BF16), gf_ref, scr_ref, 128)

    def k1_group(gidx, c):
        idx = [gidx * FFT_UNROLL + u for u in range(FFT_UNROLL)]
        xs = [_stage2(n1, s * kg + i, f2_ref, scr_ref) for i in idx]
        for i, (xr, xi) in zip(idx, xs):
            off = pl.multiple_of(i * FFT_N2, FFT_N2)
            o_ref[0, 0, pl.ds(off, FFT_N2), :] = xr
            o_ref[0, 1, pl.ds(off, FFT_N2), :] = xi
        return c

    lax.fori_loop(0, kg // FFT_UNROLL, k1_group, 0)


@functools.lru_cache(maxsize=None)
def _fft_table_full(length):
    n = 2 * length
    n1 = n // FFT_N2
    k1 = np.arange(n1)[None, :, None].astype(np.float64)
    j = np.arange(n1)[None, None, :].astype(np.float64)
    n2 = np.arange(FFT_N2)[:, None, None].astype(np.float64)
    ang = -2.0 * np.pi * k1 * (FFT_N2 * j + n2) / n
    return np.concatenate([np.cos(ang), np.sin(ang)], axis=1).astype(np.float32).astype(BF16)


def _filter_spectrum(k2, length):
    _, _, f2, n1 = _fft_tables(length)
    gf = _fft_table_full(length)
    n = 2 * length
    kg = min(n1, 16)
    nf = k2.shape[1] // D_HY
    big = pl.Buffered(1)
    return pl.pallas_call(
        functools.partial(_hyfft_kernel, n1, kg), grid=(nf, 2, n1 // kg),
        in_specs=[pl.BlockSpec((n, 128), lambda f, c, s: (0, 2 * f + c), pipeline_mode=big),
                  pl.BlockSpec(gf.shape, lambda f, c, s: (0, 0, 0), pipeline_mode=big),
                  pl.BlockSpec(f2.shape, lambda f, c, s: (0, 0))],
        out_specs=pl.BlockSpec((1, 2, kg * FFT_N2, 128), lambda f, c, s: (f, 0, s, c)),
        out_shape=jax.ShapeDtypeStruct((nf, 2, n, 256), F32),
        scratch_shapes=[pltpu.VMEM((2, FFT_N2 * (n1 + FFT_PAD), 128), F32),
                        pltpu.VMEM((n1 * (FFT_N2 + FFT_PAD), 128), F32)],
        compiler_params=_params(("arbitrary", "arbitrary", "arbitrary")), name="hyena_filter_fft",
    )(k2, gf, f2)


def _normalised(k2, ssq, extra_scale):
    return k2 * (lax.rsqrt(ssq + EPS) * extra_scale)


def _hy_ctx_kernel(lc, hv_ref, zh_ref, k2_ref, b_ref, fc_ref, ff_ref, ci_ref, o_ref):
    nf = 2 * lc
    hv = hv_ref[0].astype(F32)

    def conv(z, o):
        kf = _dot_exact_rhs(ff_ref[...], k2_ref[:, o * D_HY:(o + 1) * D_HY])
        zs = _dot(fc_ref[...], z.astype(BF16))
        zr, zi = zs[:nf], zs[nf:]
        kr, ki = kf[:nf], kf[nf:]
        ys = jnp.concatenate([zr * kr - zi * ki, zr * ki + zi * kr], axis=0).astype(BF16)
        return _dot(ci_ref[...], ys) * (1.0 / nf)

    v = hv[:, 0:D_HY]
    x1 = hv[:, D_HY:2 * D_HY]
    x2 = hv[:, 2 * D_HY:3 * D_HY]
    z1 = x1 * (conv(v, 0) + v * b_ref[0:1])
    p = x2 * (conv(z1, 1) + z1 * b_ref[1:2])
    o_ref[0] = (p * _silu(zh_ref[0].astype(F32))).astype(BF16)


def _dot_exact_rhs(a_bf16, b):
    bh, bm, bl = _split3(b)
    return _dot(a_bf16, bh) + _dot(a_bf16, bm) + _dot(a_bf16, bl)


@functools.lru_cache(maxsize=None)
def _ctx_dft_tables(lc):
    nf = 2 * lc
    k = np.arange(nf)[:, None].astype(np.float64)
    ang = -2.0 * np.pi * k * np.arange(nf)[None, :] / nf
    ff = np.concatenate([np.cos(ang), np.sin(ang)], axis=0)
    fc = ff[:, :lc]
    a2 = 2.0 * np.pi * np.arange(lc)[:, None] * np.arange(nf)[None, :] / nf
    ci = np.concatenate([np.cos(a2), -np.sin(a2)], axis=1)
    cv = lambda m: np.asarray(m, np.float32).astype(BF16)
    return cv(fc), cv(ff), cv(ci)


def _hy_ctx(hv, pf, k2c, hy_bias, nt, lc):
    nb = hv.shape[0]
    fc, ff, ci = _ctx_dft_tables(lc)
    c2 = lambda b: (0, 0)
    return pl.pallas_call(
        functools.partial(_hy_ctx_kernel, lc), grid=(nb,),
        in_specs=[pl.BlockSpec((1, lc, 768), lambda b: (b, nt, 0)),
                  pl.BlockSpec((1, lc, D_HY), lambda b: (b, nt, PF_ZH)),
                  pl.BlockSpec(k2c.shape, c2),
                  pl.BlockSpec((2, D_HY), c2),
                  pl.BlockSpec(fc.shape, c2), pl.BlockSpec(ff.shape, c2), pl.BlockSpec(ci.shape, c2)],
        out_specs=pl.BlockSpec((1, lc, D_HY), lambda b: (b, 0, 0)),
        out_shape=jax.ShapeDtypeStruct((nb, lc, D_HY), BF16),
        compiler_params=_params(("arbitrary",)), name="hyena_ctx",
    )(hv, pf, k2c, hy_bias, fc, ff, ci)


def _out_kernel(n_hy, na_ref, h0_ref, h1_ref, om_ref, zm_ref, w_ref, x_ref, mod_ref, g_ref, *rest):
    hy_refs, o_ref = rest[:n_hy], rest[n_hy]
    chunks = [slice(r, r + TT) for r in range(0, o_ref.shape[1], TT)]
    acts = []
    for rows in chunks:
        if n_hy == 1:
            hy = hy_refs[0][0, rows]
        else:
            y_ref, z_ref, x2_ref, zh_ref, b_ref = hy_refs
            z = z_ref[0, rows].astype(F32)
            hy = (x2_ref[0, rows].astype(F32) * (y_ref[0, rows].astype(F32) + z * b_ref[...])
                  * _silu(zh_ref[0, rows].astype(F32))).astype(BF16)
        hsum = h0_ref[0, 0, rows].astype(F32) + h1_ref[0, 0, rows].astype(F32)
        ml = (jax.nn.sigmoid(om_ref[0, rows].astype(F32)) * hsum * _silu(zm_ref[0, rows].astype(F32))).astype(BF16)
        acts.append((ml, hy))
    ys = [_dot(na_ref[0, rows], w_ref[0:D_NA]) + _dot(ml, w_ref[D_NA:D_NA + D_ML])
          + _dot(hy, w_ref[D_NA + D_ML:D_MODEL]) for rows, (ml, hy) in zip(chunks, acts)]
    for rows, y in zip(chunks, ys):
        ms = jnp.mean(y * y, axis=-1, keepdims=True)
        r = y * lax.rsqrt(ms + EPS) * g_ref[...]
        o_ref[0, rows] = x_ref[0, rows] + mod_ref[0][2:3] * r


def _out_projection(na_g, hml, pf, hy_args, w_out, res, res_t0, mod3, g_post, to, t0, ntiles, mod_row, out_rows,
                    prev=None):
    nb = na_g.shape[0]
    d = D_MODEL
    tok = lambda col: (lambda b, t: (b, t + t0, col))
    own = lambda col: (lambda b, t: (b, t, col))
    in_specs = [pl.BlockSpec((1, to, 512), tok(0)),
                pl.BlockSpec((1, 1, to, 256), lambda b, t: (b, 0, t + t0, 0)),
                pl.BlockSpec((1, 1, to, 256), lambda b, t: (b, 1, t + t0, 0)),
                pl.BlockSpec((1, to, 256), tok(PF_OM)), pl.BlockSpec((1, to, 256), tok(PF_ZM)),
                pl.BlockSpec((d, d), lambda b, t: (0, 0), pipeline_mode=pl.Buffered(1)),
                pl.BlockSpec((1, to, d), lambda b, t: (b, t + res_t0, 0)),
                pl.BlockSpec((1, 3, d), lambda b, t: (mod_row(b), 0, 0)),
                pl.BlockSpec((1, d), lambda b, t: (0, 0))]
    args = [na_g, hml, hml, pf, pf, w_out, res, mod3, g_post.reshape(1, d)]
    if len(hy_args) == 1:
        in_specs.append(pl.BlockSpec((1, to, D_HY), own(0)))
        args.append(hy_args[0])
    else:
        y2, z1, hv, pf_, bias1 = hy_args
        in_specs += [pl.BlockSpec((1, to, D_HY), own(0)), pl.BlockSpec((1, to, D_HY), own(0)),
                     pl.BlockSpec((1, to, D_HY), tok(2)), pl.BlockSpec((1, to, D_HY), tok(PF_ZH)),
                     pl.BlockSpec((1, D_HY), lambda b, t: (0, 0))]
        args += [y2, z1, hv, pf_, bias1.reshape(1, D_HY)]
    n_hy = len(hy_args)
    aliases = {}
    if prev is not None:
        in_specs.append(pl.BlockSpec(memory_space=pl.ANY))
        args.append(prev)
        aliases = {len(args) - 1: 0}
        kern = lambda *r: _out_kernel(n_hy, *r[:9 + n_hy], r[10 + n_hy])
    else:
        kern = functools.partial(_out_kernel, n_hy)
    return pl.pallas_call(
        kern, grid=(nb, ntiles), in_specs=in_specs,
        out_specs=pl.BlockSpec((1, to, d), tok(0)),
        out_shape=jax.ShapeDtypeStruct((nb, out_rows, d), F32),
        input_output_aliases=aliases,
        compiler_params=_params(("arbitrary", "arbitrary")), name="out_proj_norm_residual",
    )(*args)


@functools.lru_cache(maxsize=None)
def _rope_tables(length, lc):
    n = HEAD_DIM // 4
    t = np.arange(length)
    inv = (ROPE_BASE ** (-np.arange(n, dtype=np.float32) / n)).astype(np.float32)
    pos = np.stack([t // GRID_W, t % GRID_W], axis=-1).astype(np.float32)
    ang = (pos[:, :, None] * inv).astype(np.float32)
    cos_h = np.concatenate([np.cos(ang[:, 0]), np.cos(ang[:, 0]), np.cos(ang[:, 1]), np.cos(ang[:, 1])], axis=-1)
    sin_h = np.concatenate([-np.sin(ang[:, 0]), np.sin(ang[:, 0]), -np.sin(ang[:, 1]), np.sin(ang[:, 1])], axis=-1)
    cos_t = np.concatenate([np.tile(cos_h, (1, 2 * H_ML)), np.ones((lc, 512), np.float32)], axis=0)
    sin_t = np.concatenate([np.tile(sin_h, (1, 2 * H_ML)), np.zeros((lc, 512), np.float32)], axis=0)
    kscale = np.concatenate([np.ones(256, np.float32), np.full(256, HEAD_DIM ** -0.5, np.float32)])
    return (cos_t * kscale).astype(np.float32), (sin_t * kscale).astype(np.float32)


@functools.lru_cache(maxsize=None)
def _scan_constants():
    tril = np.tril(np.ones((TT, TT), np.float32))
    emat = np.zeros((128, D_ML), np.float32)
    for h in range(H_ML):
        emat[h, h * HEAD_DIM:(h + 1) * HEAD_DIM] = 1.0
    rh = np.arange(D_ML)[:, None] // HEAD_DIM
    chd = np.arange(D_ML)[None, :] // HEAD_DIM
    bmask = (rh == chd).astype(np.float32)
    tri = np.concatenate([tril, tril.T], axis=0)
    return tri.astype(BF16), emat.astype(BF16), bmask


def _layer(x_lat, x_ctx, ctx_tile, mod3, nb, length, lc, g_pre, g_post, w_in, b_if, conv_ml, conv_hy, rpb,
           hf_w1, hf_b1, hf_w2, hf_b2, hf_w3, hf_freq, hy_bias, w_out, update_ctx):
    nt = length // TT
    ltot = length + lc
    tri, emat, bmask = _scan_constants()
    cos_t, sin_t = _rope_tables(length, lc)

    c = lambda a, b: w_in[:, a:b]
    wna = jnp.concatenate([c(0, 512) * (HEAD_DIM ** -0.5 * LOG2E), c(512, 1536)], axis=1).astype(BF16)
    wpr = jnp.concatenate([c(1536, 2048), c(2560, 2816), c(2816, 3072), c(3072, 3328), c(4112, 4368)],
                          axis=1).astype(BF16)
    wlo = jnp.concatenate([c(2048, 2560), c(3344, 4112), c(3328, 3344), jnp.zeros((D_MODEL, 112), F32)],
                          axis=1).astype(BF16)
    bif_row = jnp.zeros((1, 128), F32).at[0, :4 * H_ML].set(b_if.reshape(-1))
    ona, pf, qk, hv, gates, gates_t = _projection(x_lat, x_ctx, ctx_tile, mod3, g_pre, wna, wpr, wlo, conv_ml,
                                                  conv_hy, bif_row, cos_t, sin_t, tri, nt)

    na_g = _attention(ona, pf, _na_bias_table(rpb), nt, update_ctx)

    hml = _mlstm_scan(qk, pf, gates, gates_t, emat, bmask, nt)

    k2, ssq = _hyena_filters(length, hf_w1, hf_b1, hf_w2, hf_b2, hf_w3, hf_freq)
    kf = _filter_spectrum(_normalised(k2, ssq, 1.0 / (2 * length)), length)
    hv4 = hv.reshape(nb // 2, 2, ltot, 768)
    z1 = _hyena_conv(hv4, kf, 0, length, x1_col0=2, bias=hy_bias[0])
    y2 = _hyena_conv(z1, kf, 1, length).reshape(nb, length, D_HY)
    z1 = z1.reshape(nb, length, D_HY)
    hy_args = (y2, z1, hv, pf, hy_bias[1])
    to = 2 * TT
    if not update_ctx:
        return _out_projection(na_g, hml, pf, hy_args, w_out, x_lat, 0, mod3, g_post, to, 0, length // to,
                               lambda b: b, length)

    xn = _out_projection(na_g, hml, pf, hy_args, w_out, x_lat, 0, mod3, g_post, to, 0, length // to,
                         lambda b: b, ltot)
    k2c, ssq_c = _hyena_filters(lc, hf_w1, hf_b1, hf_w2, hf_b2, hf_w3, hf_freq)
    hyc_g = _hy_ctx(hv, pf, _normalised(k2c, ssq_c, 1.0), hy_bias, nt, lc)
    return _out_projection(na_g, hml, pf, (hyc_g,), w_out, x_ctx, ctx_tile, mod3, g_post, TT, nt, 1,
                           lambda b: nb, ltot, prev=xn)


def kernel(x, c, ctx, c_ctx, w_ada, b_ada, g_pre, g_post, w_in, b_if, conv_ml, conv_hy, rpb, hf_w1, hf_b1,
           hf_w2, hf_b2, hf_w3, hf_freq, hy_bias, w_out):
    nb, length, d = x.shape
    lc = ctx.shape[1]
    depth = w_in.shape[0]
    assert lc == TT and length % (4 * TT) == 0 and nb % 2 == 0 and nb % SCAN_NB == 0 and d == D_MODEL
    cc = jnp.zeros((16, d), F32).at[:nb].set(c).at[nb].set(c_ctx)
    x_lat, x_ctx, ctx_tile = x, ctx, 0
    for l in range(depth):
        mod3 = _modulation(cc, w_ada[l], b_ada[l]).reshape(16, 3, d)
        x_lat = _layer(x_lat, x_ctx, ctx_tile, mod3, nb, length, lc, g_pre[l], g_post[l], w_in[l], b_if[l],
                       conv_ml[l], conv_hy[l], rpb[l], hf_w1[l], hf_b1[l], hf_w2[l], hf_b2[l], hf_w3[l],
                       hf_freq[l], hy_bias[l], w_out[l].astype(BF16), l < depth - 1)
        x_ctx, ctx_tile = x_lat, length // TT
    return x_lat
```

```python
import functools
import math

import numpy as np
import jax
import jax.numpy as jnp
from jax import lax
from jax.experimental import pallas as pl
from jax.experimental.pallas import tpu as pltpu

F32 = jnp.float32
BF16 = jnp.bfloat16

D_MODEL = 1024
GRID_W = 64
HEAD_DIM = 64
D_NA = 512
D_ML = 256
D_HY = 256
H_NA = 8
H_ML = 4
NA_ROWS = 8
NA_COLS = 16
HY_EMB = 33
HY_FFN = 64
HY_TARGET = 1e-2
HY_FAST = 0.3
HY_SLOW = 1.5
ROPE_BASE = 10000.0
EPS = 1e-6

TT = 256
BAND_ROWS = TT // GRID_W
SCAN_NB = 8
PRE_ROWS = 64
FFT_N2 = 128
FFT_PAD = 8
FFT_UNROLL = 8
NEG = -1e30
BIG = 3e38
LOG2E = math.log2(math.e)
VMEM_LIMIT = 56 * 1024 * 1024
FFTCONV_VMEM_LIMIT = 59 * 1024 * 1024

PF_ZA = 0
PF_VM, PF_OM, PF_ZM, PF_ZH = 2, 3, 4, 5
LOC_QK, LOC_HY, LOC_GM, LOC_COLS = 0, 512, 1280, 1408
HALO = 8


def _dot(a, b):
    return jnp.dot(a, b, preferred_element_type=F32)


def _dot_nt(a, b):
    return lax.dot_general(a, b, (((1,), (1,)), ((), ())), preferred_element_type=F32)


def _dot_tn(a, b):
    return lax.dot_general(a, b, (((0,), (0,)), ((), ())), preferred_element_type=F32)


def _split3(x):
    h = x.astype(BF16)
    r = x - h.astype(F32)
    m = r.astype(BF16)
    l = (r - m.astype(F32)).astype(BF16)
    return h, m, l


def _dot_f32(a, b):
    ah, am, al = _split3(a)
    bh, bm, bl = _split3(b)
    return (_dot(ah, bh) + (_dot(ah, bm) + _dot(am, bh))
            + (_dot(ah, bl) + _dot(al, bh) + _dot(am, bm)))


def _dot_2x(a, b):
    ah, am, _ = _split3(a)
    bh, bm, _ = _split3(b)
    return _dot(ah, bh) + (_dot(ah, bm) + _dot(am, bh))


def _silu(x):
    return x * jax.nn.sigmoid(x)


def _params(sem, vmem=VMEM_LIMIT):
    return pltpu.CompilerParams(dimension_semantics=sem, vmem_limit_bytes=vmem)


def _mod_kernel(c_ref, w_ref, b_ref, o_ref):
    o_ref[...] = _dot_f32(_silu(c_ref[...]), w_ref[...]) + b_ref[...]


def _modulation(cc, w_ada, b_ada):
    n = w_ada.shape[1]
    bn = 512
    return pl.pallas_call(
        _mod_kernel, grid=(n // bn,),
        in_specs=[pl.BlockSpec(cc.shape, lambda j: (0, 0)),
                  pl.BlockSpec((w_ada.shape[0], bn), lambda j: (0, j)),
                  pl.BlockSpec((1, bn), lambda j: (0, j))],
        out_specs=pl.BlockSpec((cc.shape[0], bn), lambda j: (0, j)),
        out_shape=jax.ShapeDtypeStruct((cc.shape[0], n), F32),
        compiler_params=_params(("arbitrary",)), name="adaln_mod",
    )(cc, w_ada, b_ada.reshape(1, n))


def _proj_kernel(nt, x_ref, c_ref, xp_ref, xn_ref, mod_ref, g_ref, wna_ref, wpr_ref, wlo_ref, cml_ref, chy_ref,
                 bif_ref, cos_ref, sin_ref, tri_ref, ona_ref, opr_ref, oqk_ref, ohv_ref, og_ref, ogt_ref, loc_ref):
    t = pl.program_id(0)
    x = jnp.where(t == nt, c_ref[0], x_ref[0])
    xs = jnp.concatenate([xp_ref[0], x, xn_ref[0]], axis=0)
    ms = jnp.mean(xs * xs, axis=-1, keepdims=True)
    y = xs * lax.rsqrt(ms + EPS) * g_ref[...]
    mod = mod_ref[0]
    h = (y * (1.0 + mod[1:2]) + mod[0:1]).astype(BF16)
    hm = h[HALO:HALO + TT]
    loc_ref[...] = _dot(h, wlo_ref[...])
    ona_ref[0] = _dot(hm, wna_ref[...]).astype(BF16)

    g = loc_ref[HALO:HALO + TT, LOC_GM:LOC_COLS] + bif_ref[...]
    gl = lax.broadcasted_iota(jnp.int32, (1, 128), 1)
    is_f = jnp.logical_and((gl % 8) >= 4, gl < 16)
    lf = jnp.where(is_f, jnp.minimum(g, 0.0) - jnp.log1p(jnp.exp(-jnp.abs(g))), 0.0)
    hi, mid, lo = _split3(lf)
    packed = (hi.astype(F32) + pltpu.roll(mid.astype(F32), 16, 1) + pltpu.roll(lo.astype(F32), 32, 1)).astype(BF16)
    cs = _dot(tri_ref[...], packed)
    cs = cs + pltpu.roll(cs, 112, 1) + pltpu.roll(cs, 96, 1)
    bcum = jnp.where(gl < 8, cs[0:TT], cs[TT:2 * TT])
    a = g - pltpu.roll(bcum, 128 - H_ML, 1)
    at = jnp.where(is_f, 0.0, a).T[0:16]
    tlane = lax.broadcasted_iota(jnp.int32, (1, TT), 1)
    pre = suf = at
    shift = 1
    while shift < TT:
        pre = jnp.maximum(pre, jnp.where(tlane >= shift, pltpu.roll(pre, shift, 1), -BIG))
        suf = jnp.maximum(suf, jnp.where(tlane < TT - shift, pltpu.roll(suf, TT - shift, 1), -BIG))
        shift *= 2
    cm_t = jnp.where(lax.broadcasted_iota(jnp.int32, (16, 1), 0) < 8, pre, suf)
    cmax = jnp.concatenate([cm_t, jnp.zeros((128 - 16, TT), F32)], axis=0).T
    ab = jnp.where(is_f, bcum, a)
    og_ref[0, 0] = LOG2E * (jnp.where(gl < 8, ab, 0.0) + pltpu.roll(jnp.where(gl < H_ML, cmax, 0.0), 8, 1))
    og_ref[0, 1] = LOG2E * (pltpu.roll(jnp.where(jnp.logical_and(gl >= 8, gl < 16), ab, 0.0), 120, 1)
                            + jnp.where(jnp.logical_and(gl >= 8, gl < 8 + H_ML), cmax, 0.0))
    ogt_ref[0] = LOG2E * at
    opr_ref[0] = _dot(hm, wpr_ref[...]).astype(BF16)

    pm = jnp.where(jnp.logical_and(t != 0, t != nt), 1.0, 0.0)
    nm = jnp.where(jnp.logical_and(t != nt - 1, t != nt), 1.0, 0.0)
    row8 = lax.broadcasted_iota(jnp.int32, (8, 1), 0)
    nchunk = TT // PRE_ROWS

    def conv3(cols, w, c):
        r0 = HALO + c * PRE_ROWS
        u = loc_ref[r0:r0 + PRE_ROWS, cols]
        up = loc_ref[r0 - 1:r0, cols]
        un = loc_ref[r0 + PRE_ROWS:r0 + PRE_ROWS + 1, cols]
        if c == 0:
            up = up * pm
        if c == nchunk - 1:
            un = un * nm
        rp = pltpu.roll(u, 1, 0)
        rn = pltpu.roll(u, PRE_ROWS - 1, 0)
        prev = jnp.concatenate([jnp.where(row8 == 0, up, rp[0:8]), rp[8:]], axis=0)
        nxt = jnp.concatenate([rn[:PRE_ROWS - 8], jnp.where(row8 == 7, un, rn[PRE_ROWS - 8:])], axis=0)
        return prev * w[0:1] + u * w[1:2] + nxt * w[2:3]

    lane = lax.broadcasted_iota(jnp.int32, (1, 2 * D_ML), 1)
    first = (lane % 32) < 16
    for c in range(nchunk):
        rows = slice(c * PRE_ROWS, (c + 1) * PRE_ROWS)
        ohv_ref[0, rows] = conv3(slice(LOC_HY, LOC_GM), chy_ref[...], c).astype(BF16)
        x = _silu(conv3(slice(LOC_QK, LOC_HY), cml_ref[...], c))
        partner = jnp.where(first, pltpu.roll(x, 2 * D_ML - 16, 1), pltpu.roll(x, 16, 1))
        oqk_ref[0, rows] = (x * cos_ref[rows] + partner * sin_ref[rows]).astype(BF16)


def _projection(x_lat, x_ctx, ctx_tile, mod3, g_pre, wna, wpr, wlo, conv_ml, conv_hy, bif_row, cos_t, sin_t,
                tri, nt):
    nb, lat_rows, d = x_lat.shape
    ltot = (nt + 1) * TT
    ctx_row = nb
    last8 = lat_rows // HALO - 1
    r8 = TT // HALO

    def mod_idx(t, b):
        return (jnp.where(t == nt, ctx_row, b), 0, 0)

    const = lambda t, b: (0, 0)
    big = pl.Buffered(1)
    tok = lambda t, b: (b, t, 0)
    return pl.pallas_call(
        functools.partial(_proj_kernel, nt), grid=(nt + 1, nb),
        in_specs=[pl.BlockSpec((1, TT, d), lambda t, b: (b, jnp.minimum(t, nt - 1), 0)),
                  pl.BlockSpec((1, TT, d), lambda t, b: (b, ctx_tile, 0)),
                  pl.BlockSpec((1, HALO, d), lambda t, b: (b, jnp.clip(t * r8 - 1, 0, last8), 0)),
                  pl.BlockSpec((1, HALO, d), lambda t, b: (b, jnp.minimum((t + 1) * r8, last8), 0)),
                  pl.BlockSpec((1, 3, d), mod_idx),
                  pl.BlockSpec((1, d), const),
                  pl.BlockSpec(wna.shape, const, pipeline_mode=big),
                  pl.BlockSpec(wpr.shape, const, pipeline_mode=big),
                  pl.BlockSpec(wlo.shape, const, pipeline_mode=big),
                  pl.BlockSpec((3, 512), const), pl.BlockSpec((3, 768), const), pl.BlockSpec((1, 128), const),
                  pl.BlockSpec((TT, 512), lambda t, b: (t, 0)), pl.BlockSpec((TT, 512), lambda t, b: (t, 0)),
                  pl.BlockSpec((2 * TT, TT), const)],
        out_specs=[pl.BlockSpec((1, TT, wna.shape[1]), tok), pl.BlockSpec((1, TT, wpr.shape[1]), tok),
                   pl.BlockSpec((1, TT, 512), tok), pl.BlockSpec((1, TT, 768), tok),
                   pl.BlockSpec((1, 2, TT, 128), lambda t, b: (b, 0, t, 0)),
                   pl.BlockSpec((1, 16, TT), lambda t, b: (b, 0, t))],
        out_shape=[jax.ShapeDtypeStruct((nb, ltot, wna.shape[1]), BF16),
                   jax.ShapeDtypeStruct((nb, ltot, wpr.shape[1]), BF16),
                   jax.ShapeDtypeStruct((nb, ltot, 512), BF16),
                   jax.ShapeDtypeStruct((nb, ltot, 768), BF16),
                   jax.ShapeDtypeStruct((nb, 2, ltot, 128), F32),
                   jax.ShapeDtypeStruct((nb, 16, ltot), F32)],
        scratch_shapes=[pltpu.VMEM((TT + 2 * HALO, LOC_COLS), F32)],
        compiler_params=_params(("arbitrary", "arbitrary")), name="norm_in_proj_conv",
    )(x_lat, x_ctx, x_lat, x_lat, mod3, g_pre.reshape(1, d), wna, wpr, wlo, conv_ml, conv_hy, bif_row,
      cos_t, sin_t, tri)


def _na_kernel(q_ref, k0_ref, k1_ref, k2_ref, v0_ref, v1_ref, v2_ref, kc_ref, vc_ref, za_ref, tab_ref, o_ref):
    lane = lax.broadcasted_iota(jnp.int32, (1, 128), 1)
    k_refs = (k0_ref, k1_ref, k2_ref)
    v_refs = (v0_ref, v1_ref, v2_ref)

    def row_max(s):
        return jnp.max(jnp.maximum(s[:, :128], s[:, 128:]), axis=-1, keepdims=True)

    def head_mask(h):
        return (lane >= HEAD_DIM) if h % 2 else (lane < HEAD_DIM)

    def masked_q(h):
        sl = slice((h // 2) * 128, (h // 2 + 1) * 128)
        q2 = q_ref[0, :, sl]
        return jnp.where(head_mask(h), q2, jnp.zeros_like(q2))

    def score_block(h, qm, blk):
        sl = slice((h // 2) * 128, (h // 2 + 1) * 128)
        if blk == 0:
            return _dot_nt(qm, kc_ref[0, :, sl])
        return _dot_nt(qm, k_refs[blk - 1][0, :, sl]) + tab_ref[0, h, :, (blk - 1) * TT:blk * TT]

    def values(h, p):
        sl = slice((h // 2) * 128, (h // 2 + 1) * 128)
        hm = head_mask(h)
        v = jnp.concatenate([vc_ref[0, :, sl]] + [r[0, :, sl] for r in v_refs], axis=0)
        acc = _dot(p, jnp.where(hm, v, jnp.ones_like(v)))
        return jnp.where(hm, acc / pltpu.roll(acc, HEAD_DIM, 1), 0.0)

    outs = {}
    qm = masked_q(0)
    sc = [score_block(0, qm, blk) for blk in range(4)]
    for h in range(H_NA):
        m = row_max(jnp.maximum(jnp.maximum(sc[0], sc[1]), jnp.maximum(sc[2], sc[3])))
        qm = masked_q(h + 1) if h + 1 < H_NA else None
        nxt, ps = [], []
        for blk in range(4):
            if qm is not None:
                nxt.append(score_block(h + 1, qm, blk))
            ps.append(jnp.exp2(sc[blk] - m).astype(BF16))
        sc = nxt
        outs[h] = values(h, jnp.concatenate(ps, axis=1))
        if h % 2 == 1:
            sl = slice((h // 2) * 128, (h // 2 + 1) * 128)
            o_ref[0, :, sl] = ((outs[h - 1] + outs[h]) * _silu(za_ref[0, :, sl].astype(F32))).astype(BF16)


def _attention(ona, pf, table, nt, with_ctx_queries):
    nb, ltot, _ = ona.shape
    nj = nt + 1 if with_ctx_queries else nt

    def kv(col, dj):
        return lambda j, b: (b, jnp.clip(j + dj, 0, nt - 1), col)

    def pat(j, b):
        p = jnp.where(j == 0, 0, jnp.where(j == nt - 1, 2, jnp.where(j == nt, 3, 1)))
        return (p, 0, 0, 0)

    return pl.pallas_call(
        _na_kernel, grid=(nj, nb),
        in_specs=[pl.BlockSpec((1, TT, 512), lambda j, b: (b, j, 0)),
                  pl.BlockSpec((1, TT, 512), kv(1, -1)), pl.BlockSpec((1, TT, 512), kv(1, 0)),
                  pl.BlockSpec((1, TT, 512), kv(1, 1)),
                  pl.BlockSpec((1, TT, 512), kv(2, -1)), pl.BlockSpec((1, TT, 512), kv(2, 0)),
                  pl.BlockSpec((1, TT, 512), kv(2, 1)),
                  pl.BlockSpec((1, TT, 512), lambda j, b: (b, nt, 1)),
                  pl.BlockSpec((1, TT, 512), lambda j, b: (b, nt, 2)),
                  pl.BlockSpec((1, TT, 512), lambda j, b: (b, j, PF_ZA)),
                  pl.BlockSpec((1, H_NA, TT, 3 * TT), pat)],
        out_specs=pl.BlockSpec((1, TT, 512), lambda j, b: (b, j, 0)),
        out_shape=jax.ShapeDtypeStruct((nb, ltot, 512), BF16),
        compiler_params=_params(("arbitrary", "arbitrary")), name="nbr_attention",
    )(ona, ona, ona, ona, ona, ona, ona, ona, ona, pf, table)


def _na_bias_table(rpb):
    a = np.arange(TT) // GRID_W
    qc = np.arange(TT) % GRID_W
    kidx = np.arange(3 * TT)
    kr = (kidx // TT - 1) * BAND_ROWS + (kidx % TT) // GRID_W
    kcol = kidx % GRID_W
    qstart = np.clip(qc - NA_COLS // 2, 0, GRID_W - NA_COLS)
    col_ok = (kcol[None, :] >= qstart[:, None]) & (kcol[None, :] < qstart[:, None] + NA_COLS)
    coff = np.clip(kcol[None, :] - qc[:, None], -(NA_COLS - 1), NA_COLS - 1) + NA_COLS - 1
    roff = np.clip(kr[None, :] - a[:, None] + NA_ROWS - 1, 0, 2 * NA_ROWS - 2)
    half = NA_ROWS // 2
    row_ok = [
        (kr[None, :] >= 0) & (kr[None, :] < NA_ROWS) & (a[:, None] >= 0),
        (kr[None, :] >= a[:, None] - half) & (kr[None, :] < a[:, None] + half),
        (kr[None, :] >= BAND_ROWS - NA_ROWS) & (kr[None, :] < BAND_ROWS) & (a[:, None] >= 0),
    ]
    nr, ncol = 2 * NA_ROWS - 1, 2 * NA_COLS - 1
    onehot = (coff[:GRID_W, :GRID_W].reshape(1, -1) == np.arange(ncol)[:, None]).astype(np.float32)
    cmat = jnp.dot(rpb.astype(F32).reshape(H_NA * nr, ncol), onehot,
                   precision=lax.Precision.HIGHEST).reshape(H_NA, nr, GRID_W, GRID_W)
    nkr = 3 * BAND_ROWS
    blocks = [jnp.concatenate([cmat[:, roff[ai * GRID_W, kj * GRID_W]] for kj in range(nkr)], axis=-1)
              for ai in range(BAND_ROWS)]
    bias = jnp.concatenate(blocks, axis=1)
    bias = bias * LOG2E
    tabs = [jnp.where((r & col_ok)[None], bias, NEG) for r in row_ok]
    tabs.append(jnp.full_like(bias, NEG))
    return jnp.stack(tabs, axis=0)


def _expand(small, e_ref):
    sh = small.astype(BF16)
    sl = (small - sh.astype(F32)).astype(BF16)
    return _dot(sh, e_ref[...]) + _dot(sl, e_ref[...])


def _head_sums(x):
    lane = lax.broadcasted_iota(jnp.int32, (1, 128), 1)
    out = jnp.zeros((x.shape[0], 128), F32)
    for h in range(H_ML):
        t = x[:, (h // 2) * 128:(h // 2 + 1) * 128]
        sm = jnp.sum(jnp.where(lane // HEAD_DIM == h % 2, t, 0.0), axis=-1, keepdims=True)
        out = jnp.where(lane == h, sm, out)
    return out


def _scan_kernel(q_ref, k_ref, v_ref, g_ref, gt_ref, e_ref, bm_ref, cm_ref, o_ref, w_ref, n_ref, m_ref):
    s = pl.program_id(2)

    @pl.when(s == 0)
    def _():
        w_ref[...] = jnp.zeros_like(w_ref)
        n_ref[...] = jnp.zeros_like(n_ref)
        m_ref[...] = jnp.zeros_like(m_ref)

    lane_h = lax.broadcasted_iota(jnp.int32, (1, D_ML), 1) // HEAD_DIM
    gate_iota = lax.broadcasted_iota(jnp.int32, (1, 128), 1)
    gate_lanes = gate_iota < H_ML

    seqs = range(SCAN_NB)
    gate, kws, kwfs, vbs, mrows = [], [], [], [], []
    for i in seqs:
        gi = g_ref[i, 0]
        a = jnp.where(gate_lanes, gi, 0.0)
        bb = jnp.where(gate_lanes, pltpu.roll(gi, 128 - H_ML, 1), 0.0)
        cmax = jnp.where(gate_lanes, pltpu.roll(gi, 128 - 2 * H_ML, 1), 0.0)
        m_prev = m_ref[i]
        mt = jnp.maximum(m_prev, cmax)
        mc = jnp.maximum(m_prev, jnp.max(a, axis=0, keepdims=True))
        beta_s = jnp.exp2(m_prev - mt)
        small = jnp.concatenate([jnp.exp2(a - mc), beta_s,
                                 jnp.broadcast_to(jnp.exp2(m_prev - mc), (8, 128))], axis=0)
        ex = _expand(small, e_ref)
        gate.append((jnp.exp2(-bb - mt), beta_s, ex[TT:2 * TT], ex[2 * TT:2 * TT + 1],
                     jnp.min(bb, axis=0, keepdims=True) + mc))
        mrows.append(mt)
        kwf = k_ref[i].astype(F32) * ex[0:TT]
        kwfs.append(kwf)
        kws.append(kwf.astype(BF16))
        vbs.append(v_ref[i].astype(BF16))

    scores = [[_dot_nt(jnp.where(lane_h == h, q_ref[i], jnp.zeros_like(q_ref[i])), k_ref[i])
               for h in range(H_ML)] for i in seqs]
    inters = [_dot(q_ref[i], w_ref[i].astype(BF16)) for i in seqs]
    updates = [_dot_tn(kws[i], vbs[i]) for i in seqs]

    def masked(i, h):
        arg = (gt_ref[i, h:h + 1, :] - mrows[i][:, h:h + 1]) + cm_ref[0]
        sc = scores[i][h] * jnp.exp2(arg)
        return sc.astype(BF16), jnp.sum(sc, axis=-1, keepdims=True)

    order = [(i, h) for i in seqs for h in range(H_ML)]
    cur = masked(*order[0])
    intras, dens = {}, {}
    for n, (i, h) in enumerate(order):
        nxt = masked(*order[n + 1]) if n + 1 < len(order) else None
        scb, rs = cur
        part = _dot(scb, jnp.where(lane_h == h, vbs[i], jnp.zeros_like(vbs[i])))
        intras[i] = part if h == 0 else intras[i] + part
        dens[i] = jnp.where(gate_iota == h, rs, dens[i] if h else jnp.zeros((TT, 128), F32))
        cur = nxt

    for i in seqs:
        eb_s, beta_s, beta_t, decay, m_new = gate[i]
        num = beta_t * inters[i] + intras[i]
        den_s = beta_s * _head_sums(q_ref[i].astype(F32) * n_ref[i]) + dens[i]
        den = _expand(jnp.maximum(jnp.abs(den_s), eb_s), e_ref)
        o_ref[i, 0] = (num / den).astype(BF16)

        w_ref[i] = decay * w_ref[i] + bm_ref[...] * updates[i]
        n_ref[i] = decay * n_ref[i] + jnp.sum(kwfs[i], axis=0, keepdims=True)
        m_ref[i] = m_new


def _mlstm_scan(qk, pf, gates, gates_t, emat, bmask, causal, nt):
    nb, ltot, _ = qk.shape
    g = SCAN_NB

    def chunk(s, d):
        return jnp.where(s == 0, nt, jnp.where(d == 0, s - 1, nt - s))

    const = lambda b, d, s: (0, 0)
    return pl.pallas_call(
        _scan_kernel, grid=(nb // g, 2, nt + 1),
        in_specs=[pl.BlockSpec((g, TT, 256), lambda b, d, s: (b, chunk(s, d), 0)),
                  pl.BlockSpec((g, TT, 256), lambda b, d, s: (b, chunk(s, d), 1)),
                  pl.BlockSpec((g, TT, 256), lambda b, d, s: (b, chunk(s, d), PF_VM)),
                  pl.BlockSpec((g, 1, TT, 128), lambda b, d, s: (b, d, chunk(s, d), 0)),
                  pl.BlockSpec((g, 8, TT), lambda b, d, s: (b, d, chunk(s, d))),
                  pl.BlockSpec((128, 256), const), pl.BlockSpec((256, 256), const),
                  pl.BlockSpec((1, TT, TT), lambda b, d, s: (d, 0, 0))],
        out_specs=pl.BlockSpec((g, 1, TT, 256), lambda b, d, s: (b, d, chunk(s, d), 0)),
        out_shape=jax.ShapeDtypeStruct((nb, 2, ltot, 256), BF16),
        scratch_shapes=[pltpu.VMEM((g, D_ML, D_ML), F32), pltpu.VMEM((g, 1, D_ML), F32),
                        pltpu.VMEM((g, 1, 128), F32)],
        compiler_params=_params(("arbitrary", "arbitrary", "arbitrary")), name="mlstm_scan",
    )(qk, qk, pf, gates, gates_t, emat, bmask, causal)


def _hyfilt_kernel(tl, length, feat_ref, w1_ref, b1_ref, w2_ref, b2_ref, w3_ref, fq_ref, dec_ref, o_ref, ss_ref):
    i = pl.program_id(0)
    th = tl // 2
    a = jnp.sin(fq_ref[0:1] * (_dot_2x(feat_ref[...], w1_ref[...]) + b1_ref[...]))
    a = jnp.sin(fq_ref[1:2] * (_dot_2x(a, w2_ref[...]) + b2_ref[...]))
    part = None
    for half in range(2):
        dec = dec_ref[half * th:(half + 1) * th]
        f = _dot_2x(a, w3_ref[half]) * jnp.concatenate([dec, dec, dec, dec], axis=1)
        fwd = jnp.concatenate([f[:, 0:D_HY], f[:, 2 * D_HY:3 * D_HY]], axis=1)
        bwd = jnp.concatenate([f[:, D_HY:2 * D_HY], f[:, 3 * D_HY:4 * D_HY]], axis=1)
        row = lax.broadcasted_iota(jnp.int32, (th, 1), 0) + (i * tl + half * th)
        k2 = jnp.where(row < length, fwd, jnp.where(row > length, bwd, 0.0))
        o_ref[half * th:(half + 1) * th] = k2
        sq = jnp.sum(k2 * k2, axis=0, keepdims=True)
        part = sq if part is None else part + sq

    @pl.when(i == 0)
    def _():
        ss_ref[...] = jnp.zeros_like(ss_ref)

    ss_ref[...] += part


def _hyena_filters(length, w1, b1, w2, b2, w3, freq):
    t = np.arange(length, dtype=np.float32)
    tn = (t / np.float32(length - 1)).astype(np.float32)
    bands = (HY_EMB - 1) // 2
    fr = np.linspace(1e-4, bands - 1, bands, dtype=np.float32)
    ang = (np.float32(2.0 * math.pi / length) * t[:, None] * fr[None, :]).astype(np.float32)
    feat = np.zeros((length, 128), np.float32)
    feat[:, 0] = tn
    feat[:, 1:1 + bands] = np.cos(ang)
    feat[:, 1 + bands:1 + 2 * bands] = -np.sin(ang)
    deltas = np.abs(np.linspace(math.log(HY_TARGET) / HY_SLOW, math.log(HY_TARGET) / HY_FAST, D_HY, dtype=np.float32))
    dec = np.exp(-tn[:, None] * deltas[None, :]).astype(np.float32)
    lag = np.concatenate([np.arange(length), [0], np.arange(length - 1, 0, -1)])
    feat, dec = feat[lag], dec[lag]

    tl = 512
    hw = 64
    fp = feat.reshape(2 * length // tl, 2, tl // 2, 128)[..., :hw]
    feat = np.concatenate([fp[:, 0], fp[:, 1]], axis=-1).reshape(length, 128)
    pad2 = lambda m, r, c: jnp.zeros((r, c), F32).at[:m.shape[0], :m.shape[1]].set(m.astype(F32))
    bdiag = lambda m: jnp.zeros((128, 128), F32).at[:hw, :hw].set(pad2(m, hw, hw)).at[hw:, hw:].set(pad2(m, hw, hw))
    twice = lambda v: jnp.concatenate([pad2(v, v.shape[0], hw)] * 2, axis=1)
    w1p = bdiag(w1)
    w2p = bdiag(w2)
    w3h = pad2(w3, hw, 4 * D_HY)
    w3p = jnp.stack([jnp.concatenate([w3h, jnp.zeros_like(w3h)], axis=0),
                     jnp.concatenate([jnp.zeros_like(w3h), w3h], axis=0)])
    b1p = twice(b1[None])
    b2p = twice(b2[None])
    fqp = twice(freq)
    const = lambda i: (0, 0)
    return pl.pallas_call(
        functools.partial(_hyfilt_kernel, tl, length), grid=(2 * length // tl,),
        in_specs=[pl.BlockSpec((tl // 2, 128), lambda i: (i, 0)),
                  pl.BlockSpec((128, 128), const), pl.BlockSpec((1, 128), const),
                  pl.BlockSpec((128, 128), const), pl.BlockSpec((1, 128), const),
                  pl.BlockSpec((2, 128, 4 * D_HY), lambda i: (0, 0, 0)), pl.BlockSpec((2, 128), const),
                  pl.BlockSpec((tl, D_HY), lambda i: (i, 0))],
        out_specs=[pl.BlockSpec((tl, 2 * D_HY), lambda i: (i, 0)), pl.BlockSpec((1, 2 * D_HY), const)],
        out_shape=[jax.ShapeDtypeStruct((2 * length, 2 * D_HY), F32), jax.ShapeDtypeStruct((1, 2 * D_HY), F32)],
        compiler_params=_params(("arbitrary",)), name="hyena_filter_mlp",
    )(feat, w1p, b1p, w2p, b2p, w3p, fqp, dec)


@functools.lru_cache(maxsize=None)
def _fft_tables(length):
    n = 2 * length
    n1 = n // FFT_N2
    nh = n1 // 2
    k1 = np.arange(n1)[None, :, None].astype(np.float64)
    j = np.arange(nh)[None, None, :].astype(np.float64)
    n2 = np.arange(FFT_N2)[:, None, None].astype(np.float64)
    ang = -2.0 * np.pi * k1 * (FFT_N2 * j + n2) / n
    gf = np.concatenate([np.cos(ang), np.sin(ang)], axis=1)
    gi = np.concatenate([np.cos(ang).transpose(0, 2, 1), np.sin(ang).transpose(0, 2, 1)], axis=1)
    kk = np.arange(FFT_N2)[:, None] * np.arange(FFT_N2)[None, :]
    a2 = -2.0 * np.pi * kk / FFT_N2
    f2 = np.concatenate([np.cos(a2), np.sin(a2)], axis=0)
    cv = lambda m: np.asarray(m, np.float32).astype(BF16)
    return cv(gf), cv(gi), cv(f2), n1


def _stage1(n1, z_of, gf_ref, scr_ref, cols):
    def group(gidx, c):
        n2s = [gidx * FFT_UNROLL + u for u in range(FFT_UNROLL)]
        zs = [z_of(n2) for n2 in n2s]
        rs = [_dot(gf_ref[n2], z) for n2, z in zip(n2s, zs)]
        for n2, r in zip(n2s, rs):
            off = pl.multiple_of(n2 * (n1 + FFT_PAD), 8)
            if cols == 256:
                scr_ref[0, pl.ds(off, n1), :] = r[:n1, :128] - r[n1:, 128:]
                scr_ref[1, pl.ds(off, n1), :] = r[n1:, :128] + r[:n1, 128:]
            else:
                scr_ref[0, pl.ds(off, n1), :] = r[:n1]
                scr_ref[1, pl.ds(off, n1), :] = r[n1:]
        return c

    lax.fori_loop(0, FFT_N2 // FFT_UNROLL, group, 0)


def _stage2(n1, k1, f2_ref, scr_ref):
    ar = scr_ref[0, pl.ds(k1, FFT_N2, stride=n1 + FFT_PAD), :]
    ai = scr_ref[1, pl.ds(k1, FFT_N2, stride=n1 + FFT_PAD), :]
    r = _dot(f2_ref[...], jnp.concatenate([ar, ai], axis=1).astype(BF16))
    return r[:128, :128] - r[128:, 128:], r[128:, :128] + r[:128, 128:]


def _hyconv_kernel(n1, kg, gated, z_ref, gf_ref, f2_ref, kf_ref, gi_ref, *rest):
    if gated:
        x1_ref, b_ref, o_ref, scr_ref, tp_ref = rest
    else:
        o_ref, scr_ref, tp_ref = rest
    s = pl.program_id(2)
    nh = n1 // 2
    pitch = FFT_N2 + FFT_PAD

    @pl.when(s == 0)
    def _():
        for e in range(2):
            for j in range(nh):
                tp_ref[e, j * pitch:j * pitch + FFT_N2, :] = z_ref[0, e, j * FFT_N2:(j + 1) * FFT_N2, :].astype(F32)

        def z_of(n2):
            zr = tp_ref[0, pl.ds(n2, nh, stride=pitch), :]
            zi = tp_ref[1, pl.ds(n2, nh, stride=pitch), :]
            return jnp.concatenate([zr, zi], axis=1).astype(BF16)
        _stage1(n1, z_of, gf_ref, scr_ref, 256)

    def k1_group(gidx, c):
        idx = [gidx * FFT_UNROLL + u for u in range(FFT_UNROLL)]
        xs = [_stage2(n1, s * kg + i, f2_ref, scr_ref) for i in idx]
        ys = []
        for i, (xr, xi) in zip(idx, xs):
            off = pl.multiple_of(i * FFT_N2, FFT_N2)
            kr = kf_ref[0, 0, pl.ds(off, FFT_N2), :]
            ki = kf_ref[0, 1, pl.ds(off, FFT_N2), :]
            ys.append(jnp.concatenate([xr * kr - xi * ki, xr * ki + xi * kr], axis=1).astype(BF16))
        rs = [_dot(f2_ref[...], yc) for yc in ys]
        for i, r in zip(idx, rs):
            k1 = s * kg + i
            scr_ref[0, pl.ds(k1, FFT_N2, stride=n1 + FFT_PAD), :] = r[:128, :128] + r[128:, 128:]
            scr_ref[1, pl.ds(k1, FFT_N2, stride=n1 + FFT_PAD), :] = r[:128, 128:] - r[128:, :128]
        return c

    lax.fori_loop(0, kg // FFT_UNROLL, k1_group, 0)

    @pl.when(s == pl.num_programs(2) - 1)
    def _():
        def group(gidx, c):
            n2s = [gidx * FFT_UNROLL + u for u in range(FFT_UNROLL)]
            bcs = []
            for n2 in n2s:
                off = pl.multiple_of(n2 * (n1 + FFT_PAD), 8)
                bcs.append(jnp.concatenate([scr_ref[0, pl.ds(off, n1), :], scr_ref[1, pl.ds(off, n1), :]],
                                           axis=1).astype(BF16))
            rs = [_dot(gi_ref[n2], bc) for n2, bc in zip(n2s, bcs)]
            for n2, r in zip(n2s, rs):
                tp_ref[0, pl.ds(n2, nh, stride=pitch), :] = r[:nh, :128] + r[nh:, 128:]
                tp_ref[1, pl.ds(n2, nh, stride=pitch), :] = r[:nh, 128:] - r[nh:, :128]
            return c
        lax.fori_loop(0, FFT_N2 // FFT_UNROLL, group, 0)
        for e in range(2):
            for j in range(nh):
                rows = slice(j * FFT_N2, (j + 1) * FFT_N2)
                y = tp_ref[e, j * pitch:j * pitch + FFT_N2, :]
                if gated:
                    y = x1_ref[0, e, rows, :].astype(F32) * (y + z_ref[0, e, rows, :].astype(F32) * b_ref[...])
                o_ref[0, e, rows, :] = y.astype(BF16)


def _hyena_conv(z4, kf, order, length, x1_col0=None, bias=None):
    gf, gi, f2, n1 = _fft_tables(length)
    n = 2 * length
    kg = min(n1, 2 * FFT_UNROLL)
    npair = z4.shape[0]
    big = pl.Buffered(1)
    gated = bias is not None
    in_specs = [pl.BlockSpec((1, 2, length, 128), lambda p, c, s: (p, 0, 0, c)),
                pl.BlockSpec(gf.shape, lambda p, c, s: (0, 0, 0), pipeline_mode=big),
                pl.BlockSpec(f2.shape, lambda p, c, s: (0, 0)),
                pl.BlockSpec((1, 2, kg * FFT_N2, 128), lambda p, c, s: (order, 0, s, c)),
                pl.BlockSpec(gi.shape, lambda p, c, s: (0, 0, 0), pipeline_mode=big)]
    args = [z4, gf, f2, kf, gi]
    if gated:
        in_specs += [pl.BlockSpec((1, 2, length, 128), lambda p, c, s: (p, 0, 0, x1_col0 + c), pipeline_mode=big),
                     pl.BlockSpec((1, 128), lambda p, c, s: (0, c))]
        args += [z4, bias.reshape(1, D_HY)]
    return pl.pallas_call(
        functools.partial(_hyconv_kernel, n1, kg, gated), grid=(npair, 2, n1 // kg),
        in_specs=in_specs,
        out_specs=pl.BlockSpec((1, 2, length, 128), lambda p, c, s: (p, 0, 0, c),
                               pipeline_mode=pl.Buffered(1 if gated else 2)),
        out_shape=jax.ShapeDtypeStruct((npair, 2, length, 256), BF16),
        scratch_shapes=[pltpu.VMEM((2, FFT_N2 * (n1 + FFT_PAD), 128), F32),
                        pltpu.VMEM((2, (n1 // 2) * (FFT_N2 + FFT_PAD), 128), F32)],
        compiler_params=_params(("arbitrary", "arbitrary", "arbitrary"), FFTCONV_VMEM_LIMIT), name="hyena_fftconv",
    )(*args)


def _hyfft_kernel(n1, kg, z_ref, gf_ref, f2_ref, o_ref, scr_ref, tp_ref):
    s = pl.program_id(2)
    pitch = FFT_N2 + FFT_PAD

    @pl.when(s == 0)
    def _():
        for j in range(n1):
            tp_ref[j * pitch:j * pitch + FFT_N2, :] = z_ref[j * FFT_N2:(j + 1) * FFT_N2, :]
        _stage1(n1, lambda n2: tp_ref[pl.ds(n2, n1, stride=pitch), :].astype(BF16), gf_ref, scr_ref, 128)

    def k1_group(gidx, c):
        idx = [gidx * FFT_UNROLL + u for u in range(FFT_UNROLL)]
        xs = [_stage2(n1, s * kg + i, f2_ref, scr_ref) for i in idx]
        for i, (xr, xi) in zip(idx, xs):
            off = pl.multiple_of(i * FFT_N2, FFT_N2)
            o_ref[0, 0, pl.ds(off, FFT_N2), :] = xr
            o_ref[0, 1, pl.ds(off, FFT_N2), :] = xi
        return c

    lax.fori_loop(0, kg // FFT_UNROLL, k1_group, 0)


@functools.lru_cache(maxsize=None)
def _fft_table_full(length):
    n = 2 * length
    n1 = n // FFT_N2
    k1 = np.arange(n1)[None, :, None].astype(np.float64)
    j = np.arange(n1)[None, None, :].astype(np.float64)
    n2 = np.arange(FFT_N2)[:, None, None].astype(np.float64)
    ang = -2.0 * np.pi * k1 * (FFT_N2 * j + n2) / n
    return np.concatenate([np.cos(ang), np.sin(ang)], axis=1).astype(np.float32).astype(BF16)


def _filter_spectrum(k2, length):
    _, _, f2, n1 = _fft_tables(length)
    gf = _fft_table_full(length)
    n = 2 * length
    kg = min(n1, 16)
    nf = k2.shape[1] // D_HY
    big = pl.Buffered(1)
    return pl.pallas_call(
        functools.partial(_hyfft_kernel, n1, kg), grid=(nf, 2, n1 // kg),
        in_specs=[pl.BlockSpec((n, 128), lambda f, c, s: (0, 2 * f + c), pipeline_mode=big),
                  pl.BlockSpec(gf.shape, lambda f, c, s: (0, 0, 0), pipeline_mode=big),
                  pl.BlockSpec(f2.shape, lambda f, c, s: (0, 0))],
        out_specs=pl.BlockSpec((1, 2, kg * FFT_N2, 128), lambda f, c, s: (f, 0, s, c)),
        out_shape=jax.ShapeDtypeStruct((nf, 2, n, 256), F32),
        scratch_shapes=[pltpu.VMEM((2, FFT_N2 * (n1 + FFT_PAD), 128), F32),
                        pltpu.VMEM((n1 * (FFT_N2 + FFT_PAD), 128), F32)],
        compiler_params=_params(("arbitrary", "arbitrary", "arbitrary")), name="hyena_filter_fft",
    )(k2, gf, f2)


def _normalised(k2, ssq, extra_scale):
    return k2 * (lax.rsqrt(ssq + EPS) * extra_scale)


def _hy_ctx_kernel(lc, hv_ref, zh_ref, k2_ref, b_ref, fc_ref, ff_ref, ci_ref, o_ref):
    nf = 2 * lc
    hv = hv_ref[0].astype(F32)

    def conv(z, o):
        kf = _dot_exact_rhs(ff_ref[...], k2_ref[:, o * D_HY:(o + 1) * D_HY])
        zs = _dot(fc_ref[...], z.astype(BF16))
        zr, zi = zs[:nf], zs[nf:]
        kr, ki = kf[:nf], kf[nf:]
        ys = jnp.concatenate([zr * kr - zi * ki, zr * ki + zi * kr], axis=0).astype(BF16)
        return _dot(ci_ref[...], ys) * (1.0 / nf)

    v = hv[:, 0:D_HY]
    x1 = hv[:, D_HY:2 * D_HY]
    x2 = hv[:, 2 * D_HY:3 * D_HY]
    z1 = x1 * (conv(v, 0) + v * b_ref[0:1])
    p = x2 * (conv(z1, 1) + z1 * b_ref[1:2])
    o_ref[0] = (p * _silu(zh_ref[0].astype(F32))).astype(BF16)


def _dot_exact_rhs(a_bf16, b):
    bh, bm, bl = _split3(b)
    return _dot(a_bf16, bh) + _dot(a_bf16, bm) + _dot(a_bf16, bl)


@functools.lru_cache(maxsize=None)
def _ctx_dft_tables(lc):
    nf = 2 * lc
    k = np.arange(nf)[:, None].astype(np.float64)
    ang = -2.0 * np.pi * k * np.arange(nf)[None, :] / nf
    ff = np.concatenate([np.cos(ang), np.sin(ang)], axis=0)
    fc = ff[:, :lc]
    a2 = 2.0 * np.pi * np.arange(lc)[:, None] * np.arange(nf)[None, :] / nf
    ci = np.concatenate([np.cos(a2), -np.sin(a2)], axis=1)
    cv = lambda m: np.asarray(m, np.float32).astype(BF16)
    return cv(fc), cv(ff), cv(ci)


def _hy_ctx(hv, pf, k2c, hy_bias, nt, lc):
    nb = hv.shape[0]
    fc, ff, ci = _ctx_dft_tables(lc)
    c2 = lambda b: (0, 0)
    return pl.pallas_call(
        functools.partial(_hy_ctx_kernel, lc), grid=(nb,),
        in_specs=[pl.BlockSpec((1, lc, 768), lambda b: (b, nt, 0)),
                  pl.BlockSpec((1, lc, D_HY), lambda b: (b, nt, PF_ZH)),
                  pl.BlockSpec(k2c.shape, c2),
                  pl.BlockSpec((2, D_HY), c2),
                  pl.BlockSpec(fc.shape, c2), pl.BlockSpec(ff.shape, c2), pl.BlockSpec(ci.shape, c2)],
        out_specs=pl.BlockSpec((1, lc, D_HY), lambda b: (b, 0, 0)),
        out_shape=jax.ShapeDtypeStruct((nb, lc, D_HY), BF16),
        compiler_params=_params(("arbitrary",)), name="hyena_ctx",
    )(hv, pf, k2c, hy_bias, fc, ff, ci)


def _out_kernel(n_hy, na_ref, h0_ref, h1_ref, om_ref, zm_ref, w_ref, x_ref, mod_ref, g_ref, *rest):
    hy_refs, o_ref = rest[:n_hy], rest[n_hy]
    chunks = [slice(r, r + TT) for r in range(0, o_ref.shape[1], TT)]
    acts = []
    for rows in chunks:
        if n_hy == 1:
            hy = hy_refs[0][0, rows]
        else:
            y_ref, z_ref, x2_ref, zh_ref, b_ref = hy_refs
            z = z_ref[0, rows].astype(F32)
            hy = (x2_ref[0, rows].astype(F32) * (y_ref[0, rows].astype(F32) + z * b_ref[...])
                  * _silu(zh_ref[0, rows].astype(F32))).astype(BF16)
        hsum = h0_ref[0, 0, rows].astype(F32) + h1_ref[0, 0, rows].astype(F32)
        ml = (jax.nn.sigmoid(om_ref[0, rows].astype(F32)) * hsum * _silu(zm_ref[0, rows].astype(F32))).astype(BF16)
        acts.append((ml, hy))
    ys = [_dot(na_ref[0, rows], w_ref[0:D_NA]) + _dot(ml, w_ref[D_NA:D_NA + D_ML])
          + _dot(hy, w_ref[D_NA + D_ML:D_MODEL]) for rows, (ml, hy) in zip(chunks, acts)]
    for rows, y in zip(chunks, ys):
        ms = jnp.mean(y * y, axis=-1, keepdims=True)
        r = y * lax.rsqrt(ms + EPS) * g_ref[...]
        o_ref[0, rows] = x_ref[0, rows] + mod_ref[0][2:3] * r


def _out_projection(na_g, hml, pf, hy_args, w_out, res, res_t0, mod3, g_post, to, t0, ntiles, mod_row, out_rows,
                    prev=None):
    nb = na_g.shape[0]
    d = D_MODEL
    tok = lambda col: (lambda b, t: (b, t + t0, col))
    own = lambda col: (lambda b, t: (b, t, col))
    in_specs = [pl.BlockSpec((1, to, 512), tok(0)),
                pl.BlockSpec((1, 1, to, 256), lambda b, t: (b, 0, t + t0, 0)),
                pl.BlockSpec((1, 1, to, 256), lambda b, t: (b, 1, t + t0, 0)),
                pl.BlockSpec((1, to, 256), tok(PF_OM)), pl.BlockSpec((1, to, 256), tok(PF_ZM)),
                pl.BlockSpec((d, d), lambda b, t: (0, 0), pipeline_mode=pl.Buffered(1)),
                pl.BlockSpec((1, to, d), lambda b, t: (b, t + res_t0, 0)),
                pl.BlockSpec((1, 3, d), lambda b, t: (mod_row(b), 0, 0)),
                pl.BlockSpec((1, d), lambda b, t: (0, 0))]
    args = [na_g, hml, hml, pf, pf, w_out, res, mod3, g_post.reshape(1, d)]
    if len(hy_args) == 1:
        in_specs.append(pl.BlockSpec((1, to, D_HY), own(0)))
        args.append(hy_args[0])
    else:
        y2, z1, hv, pf_, bias1 = hy_args
        in_specs += [pl.BlockSpec((1, to, D_HY), own(0)), pl.BlockSpec((1, to, D_HY), own(0)),
                     pl.BlockSpec((1, to, D_HY), tok(2)), pl.BlockSpec((1, to, D_HY), tok(PF_ZH)),
                     pl.BlockSpec((1, D_HY), lambda b, t: (0, 0))]
        args += [y2, z1, hv, pf_, bias1.reshape(1, D_HY)]
    n_hy = len(hy_args)
    aliases = {}
    if prev is not None:
        in_specs.append(pl.BlockSpec(memory_space=pl.ANY))
        args.append(prev)
        aliases = {len(args) - 1: 0}
        kern = lambda *r: _out_kernel(n_hy, *r[:9 + n_hy], r[10 + n_hy])
    else:
        kern = functools.partial(_out_kernel, n_hy)
    return pl.pallas_call(
        kern, grid=(nb, ntiles), in_specs=in_specs,
        out_specs=pl.BlockSpec((1, to, d), tok(0)),
        out_shape=jax.ShapeDtypeStruct((nb, out_rows, d), F32),
        input_output_aliases=aliases,
        compiler_params=_params(("arbitrary", "arbitrary")), name="out_proj_norm_residual",
    )(*args)


@functools.lru_cache(maxsize=None)
def _rope_tables(length, lc):
    n = HEAD_DIM // 4
    t = np.arange(length)
    inv = (ROPE_BASE ** (-np.arange(n, dtype=np.float32) / n)).astype(np.float32)
    pos = np.stack([t // GRID_W, t % GRID_W], axis=-1).astype(np.float32)
    ang = (pos[:, :, None] * inv).astype(np.float32)
    cos_h = np.concatenate([np.cos(ang[:, 0]), np.cos(ang[:, 0]), np.cos(ang[:, 1]), np.cos(ang[:, 1])], axis=-1)
    sin_h = np.concatenate([-np.sin(ang[:, 0]), np.sin(ang[:, 0]), -np.sin(ang[:, 1]), np.sin(ang[:, 1])], axis=-1)
    cos_t = np.concatenate([np.tile(cos_h, (1, 2 * H_ML)), np.ones((lc, 512), np.float32)], axis=0)
    sin_t = np.concatenate([np.tile(sin_h, (1, 2 * H_ML)), np.zeros((lc, 512), np.float32)], axis=0)
    kscale = np.concatenate([np.ones(256, np.float32), np.full(256, HEAD_DIM ** -0.5, np.float32)])
    return (cos_t * kscale).astype(np.float32), (sin_t * kscale).astype(np.float32)


@functools.lru_cache(maxsize=None)
def _scan_constants():
    tril = np.tril(np.ones((TT, TT), np.float32))
    emat = np.zeros((128, D_ML), np.float32)
    for h in range(H_ML):
        emat[h, h * HEAD_DIM:(h + 1) * HEAD_DIM] = 1.0
    rh = np.arange(D_ML)[:, None] // HEAD_DIM
    chd = np.arange(D_ML)[None, :] // HEAD_DIM
    bmask = (rh == chd).astype(np.float32)
    tri = np.concatenate([tril, tril.T], axis=0)
    causal = np.stack([np.where(tril > 0, 0.0, -BIG), np.where(tril.T > 0, 0.0, -BIG)]).astype(np.float32)
    return tri.astype(BF16), emat.astype(BF16), bmask, causal


def _layer(x_lat, x_ctx, ctx_tile, mod3, nb, length, lc, g_pre, g_post, w_in, b_if, conv_ml, conv_hy, rpb,
           hf_w1, hf_b1, hf_w2, hf_b2, hf_w3, hf_freq, hy_bias, w_out, update_ctx):
    nt = length // TT
    ltot = length + lc
    tri, emat, bmask, causal = _scan_constants()
    cos_t, sin_t = _rope_tables(length, lc)

    c = lambda a, b: w_in[:, a:b]
    wna = jnp.concatenate([c(0, 512) * (HEAD_DIM ** -0.5 * LOG2E), c(512, 1536)], axis=1).astype(BF16)
    wpr = jnp.concatenate([c(1536, 2048), c(2560, 2816), c(2816, 3072), c(3072, 3328), c(4112, 4368)],
                          axis=1).astype(BF16)
    wlo = jnp.concatenate([c(2048, 2560), c(3344, 4112), c(3328, 3344), jnp.zeros((D_MODEL, 112), F32)],
                          axis=1).astype(BF16)
    bif_row = jnp.zeros((1, 128), F32).at[0, :4 * H_ML].set(b_if.reshape(-1))
    ona, pf, qk, hv, gates, gates_t = _projection(x_lat, x_ctx, ctx_tile, mod3, g_pre, wna, wpr, wlo, conv_ml,
                                                  conv_hy, bif_row, cos_t, sin_t, tri, nt)

    na_g = _attention(ona, pf, _na_bias_table(rpb), nt, update_ctx)

    hml = _mlstm_scan(qk, pf, gates, gates_t, emat, bmask, causal, nt)

    k2, ssq = _hyena_filters(length, hf_w1, hf_b1, hf_w2, hf_b2, hf_w3, hf_freq)
    kf = _filter_spectrum(_normalised(k2, ssq, 1.0 / (2 * length)), length)
    hv4 = hv.reshape(nb // 2, 2, ltot, 768)
    z1 = _hyena_conv(hv4, kf, 0, length, x1_col0=2, bias=hy_bias[0])
    y2 = _hyena_conv(z1, kf, 1, length).reshape(nb, length, D_HY)
    z1 = z1.reshape(nb, length, D_HY)
    hy_args = (y2, z1, hv, pf, hy_bias[1])
    to = 2 * TT
    if not update_ctx:
        return _out_projection(na_g, hml, pf, hy_args, w_out, x_lat, 0, mod3, g_post, to, 0, length // to,
                               lambda b: b, length)

    xn = _out_projection(na_g, hml, pf, hy_args, w_out, x_lat, 0, mod3, g_post, to, 0, length // to,
                         lambda b: b, ltot)
    k2c, ssq_c = _hyena_filters(lc, hf_w1, hf_b1, hf_w2, hf_b2, hf_w3, hf_freq)
    hyc_g = _hy_ctx(hv, pf, _normalised(k2c, ssq_c, 1.0), hy_bias, nt, lc)
    return _out_projection(na_g, hml, pf, (hyc_g,), w_out, x_ctx, ctx_tile, mod3, g_post, TT, nt, 1,
                           lambda b: nb, ltot, prev=xn)


def kernel(x, c, ctx, c_ctx, w_ada, b_ada, g_pre, g_post, w_in, b_if, conv_ml, conv_hy, rpb, hf_w1, hf_b1,
           hf_w2, hf_b2, hf_w3, hf_freq, hy_bias, w_out):
    nb, length, d = x.shape
    lc = ctx.shape[1]
    depth = w_in.shape[0]
    assert lc == TT and length % (4 * TT) == 0 and nb % 2 == 0 and nb % SCAN_NB == 0 and d == D_MODEL
    cc = jnp.zeros((16, d), F32).at[:nb].set(c).at[nb].set(c_ctx)
    x_lat, x_ctx, ctx_tile = x, ctx, 0
    for l in range(depth):
        mod3 = _modulation(cc, w_ada[l], b_ada[l]).reshape(16, 3, d)
        x_lat = _layer(x_lat, x_ctx, ctx_tile, mod3, nb, length, lc, g_pre[l], g_post[l], w_in[l], b_if[l],
                       conv_ml[l], conv_hy[l], rpb[l], hf_w1[l], hf_b1[l], hf_w2[l], hf_b2[l], hf_w3[l],
                       hf_freq[l], hy_bias[l], w_out[l].astype(BF16), l < depth - 1)
        x_ctx, ctx_tile = x_lat, length // TT
    return x_lat
```

```python
import functools
import math

import numpy as np
import jax
import jax.numpy as jnp
from jax import lax
from jax.experimental import pallas as pl
from jax.experimental.pallas import tpu as pltpu

F32 = jnp.float32
BF16 = jnp.bfloat16

D_MODEL = 1024
GRID_W = 64
HEAD_DIM = 64
D_NA = 512
D_ML = 256
D_HY = 256
H_NA = 8
H_ML = 4
NA_ROWS = 8
NA_COLS = 16
HY_EMB = 33
HY_FFN = 64
HY_TARGET = 1e-2
HY_FAST = 0.3
HY_SLOW = 1.5
ROPE_BASE = 10000.0
EPS = 1e-6

TT = 256
BAND_ROWS = TT // GRID_W
SCAN_NB = 8
PRE_ROWS = 64
FFT_N2 = 128
FFT_PAD = 8
FFT_UNROLL = 8
NEG = -1e30
BIG = 3e38
LOG2E = math.log2(math.e)
VMEM_LIMIT = 56 * 1024 * 1024
FFTCONV_VMEM_LIMIT = 59 * 1024 * 1024

PF_ZA = 0
PF_VM, PF_OM, PF_ZM, PF_ZH = 2, 3, 4, 5
LOC_QK, LOC_HY, LOC_GM, LOC_COLS = 0, 512, 1280, 1408
HALO = 8


def _dot(a, b):
    return jnp.dot(a, b, preferred_element_type=F32)


def _dot_nt(a, b):
    return lax.dot_general(a, b, (((1,), (1,)), ((), ())), preferred_element_type=F32)


def _dot_tn(a, b):
    return lax.dot_general(a, b, (((0,), (0,)), ((), ())), preferred_element_type=F32)


def _split3(x):
    h = x.astype(BF16)
    r = x - h.astype(F32)
    m = r.astype(BF16)
    l = (r - m.astype(F32)).astype(BF16)
    return h, m, l


def _dot_f32(a, b):
    ah, am, al = _split3(a)
    bh, bm, bl = _split3(b)
    return (_dot(ah, bh) + (_dot(ah, bm) + _dot(am, bh))
            + (_dot(ah, bl) + _dot(al, bh) + _dot(am, bm)))


def _dot_2x(a, b):
    ah, am, _ = _split3(a)
    bh, bm, _ = _split3(b)
    return _dot(ah, bh) + (_dot(ah, bm) + _dot(am, bh))


def _silu(x):
    return x * jax.nn.sigmoid(x)


def _params(sem, vmem=VMEM_LIMIT):
    return pltpu.CompilerParams(dimension_semantics=sem, vmem_limit_bytes=vmem)


def _mod_kernel(c_ref, w_ref, b_ref, o_ref):
    o_ref[...] = _dot_f32(_silu(c_ref[...]), w_ref[...]) + b_ref[...]


def _modulation(cc, w_ada, b_ada):
    n = w_ada.shape[1]
    bn = 512
    return pl.pallas_call(
        _mod_kernel, grid=(n // bn,),
        in_specs=[pl.BlockSpec(cc.shape, lambda j: (0, 0)),
                  pl.BlockSpec((w_ada.shape[0], bn), lambda j: (0, j)),
                  pl.BlockSpec((1, bn), lambda j: (0, j))],
        out_specs=pl.BlockSpec((cc.shape[0], bn), lambda j: (0, j)),
        out_shape=jax.ShapeDtypeStruct((cc.shape[0], n), F32),
        compiler_params=_params(("arbitrary",)), name="adaln_mod",
    )(cc, w_ada, b_ada.reshape(1, n))


def _proj_kernel(nt, x_ref, c_ref, xp_ref, xn_ref, mod_ref, g_ref, wna_ref, wpr_ref, wlo_ref, cml_ref, chy_ref,
                 bif_ref, cos_ref, sin_ref, tri_ref, ona_ref, opr_ref, oqk_ref, ohv_ref, og_ref, ogt_ref, loc_ref):
    t = pl.program_id(0)
    x = jnp.where(t == nt, c_ref[0], x_ref[0])
    xs = jnp.concatenate([xp_ref[0], x, xn_ref[0]], axis=0)
    ms = jnp.mean(xs * xs, axis=-1, keepdims=True)
    y = xs * lax.rsqrt(ms + EPS) * g_ref[...]
    mod = mod_ref[0]
    h = (y * (1.0 + mod[1:2]) + mod[0:1]).astype(BF16)
    hm = h[HALO:HALO + TT]
    loc_ref[...] = _dot(h, wlo_ref[...])
    ona_ref[0] = _dot(hm, wna_ref[...]).astype(BF16)

    g = loc_ref[HALO:HALO + TT, LOC_GM:LOC_COLS] + bif_ref[...]
    gl = lax.broadcasted_iota(jnp.int32, (1, 128), 1)
    is_f = jnp.logical_and((gl % 8) >= 4, gl < 16)
    lf = jnp.where(is_f, jnp.minimum(g, 0.0) - jnp.log1p(jnp.exp(-jnp.abs(g))), 0.0)
    hi, mid, lo = _split3(lf)
    packed = (hi.astype(F32) + pltpu.roll(mid.astype(F32), 16, 1) + pltpu.roll(lo.astype(F32), 32, 1)).astype(BF16)
    cs = _dot(tri_ref[...], packed)
    cs = cs + pltpu.roll(cs, 112, 1) + pltpu.roll(cs, 96, 1)
    bcum = jnp.where(gl < 8, cs[0:TT], cs[TT:2 * TT])
    a = g - pltpu.roll(bcum, 128 - H_ML, 1)
    at = jnp.where(is_f, 0.0, a).T[0:16]
    tlane = lax.broadcasted_iota(jnp.int32, (1, TT), 1)
    pre = suf = at
    shift = 1
    while shift < TT:
        pre = jnp.maximum(pre, jnp.where(tlane >= shift, pltpu.roll(pre, shift, 1), -BIG))
        suf = jnp.maximum(suf, jnp.where(tlane < TT - shift, pltpu.roll(suf, TT - shift, 1), -BIG))
        shift *= 2
    cm_t = jnp.where(lax.broadcasted_iota(jnp.int32, (16, 1), 0) < 8, pre, suf)
    cmax = jnp.concatenate([cm_t, jnp.zeros((128 - 16, TT), F32)], axis=0).T
    ab = jnp.where(is_f, bcum, a)
    og_ref[0, 0] = LOG2E * (jnp.where(gl < 8, ab, 0.0) + pltpu.roll(jnp.where(gl < H_ML, cmax, 0.0), 8, 1))
    og_ref[0, 1] = LOG2E * (pltpu.roll(jnp.where(jnp.logical_and(gl >= 8, gl < 16), ab, 0.0), 120, 1)
                            + jnp.where(jnp.logical_and(gl >= 8, gl < 8 + H_ML), cmax, 0.0))
    ogt_ref[0] = LOG2E * at
    opr_ref[0] = _dot(hm, wpr_ref[...]).astype(BF16)

    pm = jnp.where(jnp.logical_and(t != 0, t != nt), 1.0, 0.0)
    nm = jnp.where(jnp.logical_and(t != nt - 1, t != nt), 1.0, 0.0)
    row8 = lax.broadcasted_iota(jnp.int32, (8, 1), 0)
    nchunk = TT // PRE_ROWS

    def conv3(cols, w, c):
        r0 = HALO + c * PRE_ROWS
        u = loc_ref[r0:r0 + PRE_ROWS, cols]
        up = loc_ref[r0 - 1:r0, cols]
        un = loc_ref[r0 + PRE_ROWS:r0 + PRE_ROWS + 1, cols]
        if c == 0:
            up = up * pm
        if c == nchunk - 1:
            un = un * nm
        rp = pltpu.roll(u, 1, 0)
        rn = pltpu.roll(u, PRE_ROWS - 1, 0)
        prev = jnp.concatenate([jnp.where(row8 == 0, up, rp[0:8]), rp[8:]], axis=0)
        nxt = jnp.concatenate([rn[:PRE_ROWS - 8], jnp.where(row8 == 7, un, rn[PRE_ROWS - 8:])], axis=0)
        return prev * w[0:1] + u * w[1:2] + nxt * w[2:3]

    lane = lax.broadcasted_iota(jnp.int32, (1, 2 * D_ML), 1)
    first = (lane % 32) < 16
    for c in range(nchunk):
        rows = slice(c * PRE_ROWS, (c + 1) * PRE_ROWS)
        ohv_ref[0, rows] = conv3(slice(LOC_HY, LOC_GM), chy_ref[...], c).astype(BF16)
        x = _silu(conv3(slice(LOC_QK, LOC_HY), cml_ref[...], c))
        partner = jnp.where(first, pltpu.roll(x, 2 * D_ML - 16, 1), pltpu.roll(x, 16, 1))
        oqk_ref[0, rows] = (x * cos_ref[rows] + partner * sin_ref[rows]).astype(BF16)


def _projection(x_lat, x_ctx, ctx_tile, mod3, g_pre, wna, wpr, wlo, conv_ml, conv_hy, bif_row, cos_t, sin_t,
                tri, nt):
    nb, lat_rows, d = x_lat.shape
    ltot = (nt + 1) * TT
    ctx_row = nb
    last8 = lat_rows // HALO - 1
    r8 = TT // HALO

    def mod_idx(t, b):
        return (jnp.where(t == nt, ctx_row, b), 0, 0)

    const = lambda t, b: (0, 0)
    big = pl.Buffered(1)
    tok = lambda t, b: (b, t, 0)
    return pl.pallas_call(
        functools.partial(_proj_kernel, nt), grid=(nt + 1, nb),
        in_specs=[pl.BlockSpec((1, TT, d), lambda t, b: (b, jnp.minimum(t, nt - 1), 0)),
                  pl.BlockSpec((1, TT, d), lambda t, b: (b, ctx_tile, 0)),
                  pl.BlockSpec((1, HALO, d), lambda t, b: (b, jnp.clip(t * r8 - 1, 0, last8), 0)),
                  pl.BlockSpec((1, HALO, d), lambda t, b: (b, jnp.minimum((t + 1) * r8, last8), 0)),
                  pl.BlockSpec((1, 3, d), mod_idx),
                  pl.BlockSpec((1, d), const),
                  pl.BlockSpec(wna.shape, const, pipeline_mode=big),
                  pl.BlockSpec(wpr.shape, const, pipeline_mode=big),
                  pl.BlockSpec(wlo.shape, const, pipeline_mode=big),
                  pl.BlockSpec((3, 512), const), pl.BlockSpec((3, 768), const), pl.BlockSpec((1, 128), const),
                  pl.BlockSpec((TT, 512), lambda t, b: (t, 0)), pl.BlockSpec((TT, 512), lambda t, b: (t, 0)),
                  pl.BlockSpec((2 * TT, TT), const)],
        out_specs=[pl.BlockSpec((1, TT, wna.shape[1]), tok), pl.BlockSpec((1, TT, wpr.shape[1]), tok),
                   pl.BlockSpec((1, TT, 512), tok), pl.BlockSpec((1, TT, 768), tok),
                   pl.BlockSpec((1, 2, TT, 128), lambda t, b: (b, 0, t, 0)),
                   pl.BlockSpec((1, 16, TT), lambda t, b: (b, 0, t))],
        out_shape=[jax.ShapeDtypeStruct((nb, ltot, wna.shape[1]), BF16),
                   jax.ShapeDtypeStruct((nb, ltot, wpr.shape[1]), BF16),
                   jax.ShapeDtypeStruct((nb, ltot, 512), BF16),
                   jax.ShapeDtypeStruct((nb, ltot, 768), BF16),
                   jax.ShapeDtypeStruct((nb, 2, ltot, 128), F32),
                   jax.ShapeDtypeStruct((nb, 16, ltot), F32)],
        scratch_shapes=[pltpu.VMEM((TT + 2 * HALO, LOC_COLS), F32)],
        compiler_params=_params(("arbitrary", "arbitrary")), name="norm_in_proj_conv",
    )(x_lat, x_ctx, x_lat, x_lat, mod3, g_pre.reshape(1, d), wna, wpr, wlo, conv_ml, conv_hy, bif_row,
      cos_t, sin_t, tri)


def _na_kernel(q_ref, k0_ref, k1_ref, k2_ref, v0_ref, v1_ref, v2_ref, kc_ref, vc_ref, za_ref, tab_ref, o_ref):
    lane = lax.broadcasted_iota(jnp.int32, (1, 128), 1)
    k_refs = (k0_ref, k1_ref, k2_ref)
    v_refs = (v0_ref, v1_ref, v2_ref)

    def row_max(s):
        return jnp.max(jnp.maximum(s[:, :128], s[:, 128:]), axis=-1, keepdims=True)

    def head_mask(h):
        return (lane >= HEAD_DIM) if h % 2 else (lane < HEAD_DIM)

    def masked_q(h):
        sl = slice((h // 2) * 128, (h // 2 + 1) * 128)
        q2 = q_ref[0, :, sl]
        return jnp.where(head_mask(h), q2, jnp.zeros_like(q2))

    def score_block(h, qm, blk):
        sl = slice((h // 2) * 128, (h // 2 + 1) * 128)
        if blk == 0:
            return _dot_nt(qm, kc_ref[0, :, sl])
        return _dot_nt(qm, k_refs[blk - 1][0, :, sl]) + tab_ref[0, h, :, (blk - 1) * TT:blk * TT]

    def values(h, p):
        sl = slice((h // 2) * 128, (h // 2 + 1) * 128)
        hm = head_mask(h)
        v = jnp.concatenate([vc_ref[0, :, sl]] + [r[0, :, sl] for r in v_refs], axis=0)
        acc = _dot(p, jnp.where(hm, v, jnp.ones_like(v)))
        return jnp.where(hm, acc / pltpu.roll(acc, HEAD_DIM, 1), 0.0)

    outs = {}
    qm = masked_q(0)
    sc = [score_block(0, qm, blk) for blk in range(4)]
    for h in range(H_NA):
        m = row_max(jnp.maximum(jnp.maximum(sc[0], sc[1]), jnp.maximum(sc[2], sc[3])))
        qm = masked_q(h + 1) if h + 1 < H_NA else None
        nxt, ps = [], []
        for blk in range(4):
            if qm is not None:
                nxt.append(score_block(h + 1, qm, blk))
            ps.append(jnp.exp2(sc[blk] - m).astype(BF16))
        sc = nxt
        outs[h] = values(h, jnp.concatenate(ps, axis=1))
        if h % 2 == 1:
            sl = slice((h // 2) * 128, (h // 2 + 1) * 128)
            o_ref[0, :, sl] = ((outs[h - 1] + outs[h]) * _silu(za_ref[0, :, sl].astype(F32))).astype(BF16)


def _attention(ona, pf, table, nt, with_ctx_queries):
    nb, ltot, _ = ona.shape
    nj = nt + 1 if with_ctx_queries else nt

    def kv(col, dj):
        return lambda j, b: (b, jnp.clip(j + dj, 0, nt - 1), col)

    def pat(j, b):
        p = jnp.where(j == 0, 0, jnp.where(j == nt - 1, 2, jnp.where(j == nt, 3, 1)))
        return (p, 0, 0, 0)

    return pl.pallas_call(
        _na_kernel, grid=(nj, nb),
        in_specs=[pl.BlockSpec((1, TT, 512), lambda j, b: (b, j, 0)),
                  pl.BlockSpec((1, TT, 512), kv(1, -1)), pl.BlockSpec((1, TT, 512), kv(1, 0)),
                  pl.BlockSpec((1, TT, 512), kv(1, 1)),
                  pl.BlockSpec((1, TT, 512), kv(2, -1)), pl.BlockSpec((1, TT, 512), kv(2, 0)),
                  pl.BlockSpec((1, TT, 512), kv(2, 1)),
                  pl.BlockSpec((1, TT, 512), lambda j, b: (b, nt, 1)),
                  pl.BlockSpec((1, TT, 512), lambda j, b: (b, nt, 2)),
                  pl.BlockSpec((1, TT, 512), lambda j, b: (b, j, PF_ZA)),
                  pl.BlockSpec((1, H_NA, TT, 3 * TT), pat)],
        out_specs=pl.BlockSpec((1, TT, 512), lambda j, b: (b, j, 0)),
        out_shape=jax.ShapeDtypeStruct((nb, ltot, 512), BF16),
        compiler_params=_params(("arbitrary", "arbitrary")), name="nbr_attention",
    )(ona, ona, ona, ona, ona, ona, ona, ona, ona, pf, table)


def _na_bias_table(rpb):
    a = np.arange(TT) // GRID_W
    qc = np.arange(TT) % GRID_W
    kidx = np.arange(3 * TT)
    kr = (kidx // TT - 1) * BAND_ROWS + (kidx % TT) // GRID_W
    kcol = kidx % GRID_W
    qstart = np.clip(qc - NA_COLS // 2, 0, GRID_W - NA_COLS)
    col_ok = (kcol[None, :] >= qstart[:, None]) & (kcol[None, :] < qstart[:, None] + NA_COLS)
    coff = np.clip(kcol[None, :] - qc[:, None], -(NA_COLS - 1), NA_COLS - 1) + NA_COLS - 1
    roff = np.clip(kr[None, :] - a[:, None] + NA_ROWS - 1, 0, 2 * NA_ROWS - 2)
    half = NA_ROWS // 2
    row_ok = [
        (kr[None, :] >= 0) & (kr[None, :] < NA_ROWS) & (a[:, None] >= 0),
        (kr[None, :] >= a[:, None] - half) & (kr[None, :] < a[:, None] + half),
        (kr[None, :] >= BAND_ROWS - NA_ROWS) & (kr[None, :] < BAND_ROWS) & (a[:, None] >= 0),
    ]
    nr, ncol = 2 * NA_ROWS - 1, 2 * NA_COLS - 1
    onehot = (coff[:GRID_W, :GRID_W].reshape(1, -1) == np.arange(ncol)[:, None]).astype(np.float32)
    cmat = jnp.dot(rpb.astype(F32).reshape(H_NA * nr, ncol), onehot,
                   precision=lax.Precision.HIGHEST).reshape(H_NA, nr, GRID_W, GRID_W)
    nkr = 3 * BAND_ROWS
    blocks = [jnp.concatenate([cmat[:, roff[ai * GRID_W, kj * GRID_W]] for kj in range(nkr)], axis=-1)
              for ai in range(BAND_ROWS)]
    bias = jnp.concatenate(blocks, axis=1)
    bias = bias * LOG2E
    tabs = [jnp.where((r & col_ok)[None], bias, NEG) for r in row_ok]
    tabs.append(jnp.full_like(bias, NEG))
    return jnp.stack(tabs, axis=0)


def _expand(small, e_ref):
    sh = small.astype(BF16)
    sl = (small - sh.astype(F32)).astype(BF16)
    return _dot(sh, e_ref[...]) + _dot(sl, e_ref[...])


def _head_sums(x):
    lane = lax.broadcasted_iota(jnp.int32, (1, 128), 1)
    out = jnp.zeros((x.shape[0], 128), F32)
    for h in range(H_ML):
        t = x[:, (h // 2) * 128:(h // 2 + 1) * 128]
        sm = jnp.sum(jnp.where(lane // HEAD_DIM == h % 2, t, 0.0), axis=-1, keepdims=True)
        out = jnp.where(lane == h, sm, out)
    return out


def _scan_kernel(q_ref, k_ref, v_ref, g_ref, gt_ref, e_ref, bm_ref, cm_ref, o_ref, w_ref, n_ref, m_ref):
    s = pl.program_id(2)

    @pl.when(s == 0)
    def _():
        w_ref[...] = jnp.zeros_like(w_ref)
        n_ref[...] = jnp.zeros_like(n_ref)
        m_ref[...] = jnp.zeros_like(m_ref)

    lane_h = lax.broadcasted_iota(jnp.int32, (1, D_ML), 1) // HEAD_DIM
    gate_iota = lax.broadcasted_iota(jnp.int32, (1, 128), 1)
    gate_lanes = gate_iota < H_ML

    seqs = range(SCAN_NB)
    gate, kws, kwfs, vbs, mrows = [], [], [], [], []
    for i in seqs:
        gi = g_ref[i, 0]
        a = jnp.where(gate_lanes, gi, 0.0)
        bb = jnp.where(gate_lanes, pltpu.roll(gi, 128 - H_ML, 1), 0.0)
        cmax = jnp.where(gate_lanes, pltpu.roll(gi, 128 - 2 * H_ML, 1), 0.0)
        m_prev = m_ref[i]
        mt = jnp.maximum(m_prev, cmax)
        mc = jnp.maximum(m_prev, jnp.max(a, axis=0, keepdims=True))
        beta_s = jnp.exp2(m_prev - mt)
        small = jnp.concatenate([jnp.exp2(a - mc), beta_s,
                                 jnp.broadcast_to(jnp.exp2(m_prev - mc), (8, 128))], axis=0)
        ex = _expand(small, e_ref)
        gate.append((jnp.exp2(-bb - mt), beta_s, ex[TT:2 * TT], ex[2 * TT:2 * TT + 1],
                     jnp.min(bb, axis=0, keepdims=True) + mc))
        mrows.append(mt)
        kwf = k_ref[i].astype(F32) * ex[0:TT]
        kwfs.append(kwf)
        kws.append(kwf.astype(BF16))
        vbs.append(v_ref[i].astype(BF16))

    scores = [[_dot_nt(jnp.where(lane_h == h, q_ref[i], jnp.zeros_like(q_ref[i])), k_ref[i])
               for h in range(H_ML)] for i in seqs]
    inters = [_dot(q_ref[i], w_ref[i].astype(BF16)) for i in seqs]
    updates = [_dot_tn(kws[i], vbs[i]) for i in seqs]

    def masked(i, h):
        arg = (gt_ref[i, h:h + 1, :] - mrows[i][:, h:h + 1]) + cm_ref[0]
        sc = scores[i][h] * jnp.exp2(arg)
        return sc.astype(BF16), jnp.sum(sc, axis=-1, keepdims=True)

    order = [(i, h) for i in seqs for h in range(H_ML)]
    cur = masked(*order[0])
    intras, dens = {}, {}
    for n, (i, h) in enumerate(order):
        nxt = masked(*order[n + 1]) if n + 1 < len(order) else None
        scb, rs = cur
        part = _dot(scb, jnp.where(lane_h == h, vbs[i], jnp.zeros_like(vbs[i])))
        intras[i] = part if h == 0 else intras[i] + part
        dens[i] = jnp.where(gate_iota == h, rs, dens[i] if h else jnp.zeros((TT, 128), F32))
        cur = nxt

    for i in seqs:
        eb_s, beta_s, beta_t, decay, m_new = gate[i]
        num = beta_t * inters[i] + intras[i]
        den_s = beta_s * _head_sums(q_ref[i].astype(F32) * n_ref[i]) + dens[i]
        den = _expand(jnp.maximum(jnp.abs(den_s), eb_s), e_ref)
        o_ref[i, 0] = (num / den).astype(BF16)

        w_ref[i] = decay * w_ref[i] + bm_ref[...] * updates[i]
        n_ref[i] = decay * n_ref[i] + jnp.sum(kwfs[i], axis=0, keepdims=True)
        m_ref[i] = m_new


def _mlstm_scan(qk, pf, gates, gates_t, emat, bmask, causal, nt):
    nb, ltot, _ = qk.shape
    g = SCAN_NB

    def chunk(s, d):
        return jnp.where(s == 0, nt, jnp.where(d == 0, s - 1, nt - s))

    const = lambda b, d, s: (0, 0)
    return pl.pallas_call(
        _scan_kernel, grid=(nb // g, 2, nt + 1),
        in_specs=[pl.BlockSpec((g, TT, 256), lambda b, d, s: (b, chunk(s, d), 0)),
                  pl.BlockSpec((g, TT, 256), lambda b, d, s: (b, chunk(s, d), 1)),
                  pl.BlockSpec((g, TT, 256), lambda b, d, s: (b, chunk(s, d), PF_VM)),
                  pl.BlockSpec((g, 1, TT, 128), lambda b, d, s: (b, d, chunk(s, d), 0)),
                  pl.BlockSpec((g, 8, TT), lambda b, d, s: (b, d, chunk(s, d))),
                  pl.BlockSpec((128, 256), const), pl.BlockSpec((256, 256), const),
                  pl.BlockSpec((1, TT, TT), lambda b, d, s: (d, 0, 0))],
        out_specs=pl.BlockSpec((g, 1, TT, 256), lambda b, d, s: (b, d, chunk(s, d), 0)),
        out_shape=jax.ShapeDtypeStruct((nb, 2, ltot, 256), BF16),
        scratch_shapes=[pltpu.VMEM((g, D_ML, D_ML), F32), pltpu.VMEM((g, 1, D_ML), F32),
                        pltpu.VMEM((g, 1, 128), F32)],
        compiler_params=_params(("arbitrary", "arbitrary", "arbitrary")), name="mlstm_scan",
    )(qk, qk, pf, gates, gates_t, emat, bmask, causal)


def _hyfilt_kernel(tl, length, feat_ref, w1_ref, b1_ref, w2_ref, b2_ref, w3_ref, fq_ref, dec_ref, o_ref, ss_ref):
    i = pl.program_id(0)
    th = tl // 2
    a = jnp.sin(fq_ref[0:1] * (_dot_2x(feat_ref[...], w1_ref[...]) + b1_ref[...]))
    a = jnp.sin(fq_ref[1:2] * (_dot_2x(a, w2_ref[...]) + b2_ref[...]))
    part = None
    for half in range(2):
        dec = dec_ref[half * th:(half + 1) * th]
        f = _dot_2x(a, w3_ref[half]) * jnp.concatenate([dec, dec, dec, dec], axis=1)
        fwd = jnp.concatenate([f[:, 0:D_HY], f[:, 2 * D_HY:3 * D_HY]], axis=1)
        bwd = jnp.concatenate([f[:, D_HY:2 * D_HY], f[:, 3 * D_HY:4 * D_HY]], axis=1)
        row = lax.broadcasted_iota(jnp.int32, (th, 1), 0) + (i * tl + half * th)
        k2 = jnp.where(row < length, fwd, jnp.where(row > length, bwd, 0.0))
        o_ref[half * th:(half + 1) * th] = k2
        sq = jnp.sum(k2 * k2, axis=0, keepdims=True)
        part = sq if part is None else part + sq

    @pl.when(i == 0)
    def _():
        ss_ref[...] = jnp.zeros_like(ss_ref)

    ss_ref[...] += part


def _hyena_filters(length, w1, b1, w2, b2, w3, freq):
    t = np.arange(length, dtype=np.float32)
    tn = (t / np.float32(length - 1)).astype(np.float32)
    bands = (HY_EMB - 1) // 2
    fr = np.linspace(1e-4, bands - 1, bands, dtype=np.float32)
    ang = (np.float32(2.0 * math.pi / length) * t[:, None] * fr[None, :]).astype(np.float32)
    feat = np.zeros((length, 128), np.float32)
    feat[:, 0] = tn
    feat[:, 1:1 + bands] = np.cos(ang)
    feat[:, 1 + bands:1 + 2 * bands] = -np.sin(ang)
    deltas = np.abs(np.linspace(math.log(HY_TARGET) / HY_SLOW, math.log(HY_TARGET) / HY_FAST, D_HY, dtype=np.float32))
    dec = np.exp(-tn[:, None] * deltas[None, :]).astype(np.float32)
    lag = np.concatenate([np.arange(length), [0], np.arange(length - 1, 0, -1)])
    feat, dec = feat[lag], dec[lag]

    tl = 512
    hw = 64
    fp = feat.reshape(2 * length // tl, 2, tl // 2, 128)[..., :hw]
    feat = np.concatenate([fp[:, 0], fp[:, 1]], axis=-1).reshape(length, 128)
    pad2 = lambda m, r, c: jnp.zeros((r, c), F32).at[:m.shape[0], :m.shape[1]].set(m.astype(F32))
    bdiag = lambda m: jnp.zeros((128, 128), F32).at[:hw, :hw].set(pad2(m, hw, hw)).at[hw:, hw:].set(pad2(m, hw, hw))
    twice = lambda v: jnp.concatenate([pad2(v, v.shape[0], hw)] * 2, axis=1)
    w1p = bdiag(w1)
    w2p = bdiag(w2)
    w3h = pad2(w3, hw, 4 * D_HY)
    w3p = jnp.stack([jnp.concatenate([w3h, jnp.zeros_like(w3h)], axis=0),
                     jnp.concatenate([jnp.zeros_like(w3h), w3h], axis=0)])
    b1p = twice(b1[None])
    b2p = twice(b2[None])
    fqp = twice(freq)
    const = lambda i: (0, 0)
    return pl.pallas_call(
        functools.partial(_hyfilt_kernel, tl, length), grid=(2 * length // tl,),
        in_specs=[pl.BlockSpec((tl // 2, 128), lambda i: (i, 0)),
                  pl.BlockSpec((128, 128), const), pl.BlockSpec((1, 128), const),
                  pl.BlockSpec((128, 128), const), pl.BlockSpec((1, 128), const),
                  pl.BlockSpec((2, 128, 4 * D_HY), lambda i: (0, 0, 0)), pl.BlockSpec((2, 128), const),
                  pl.BlockSpec((tl, D_HY), lambda i: (i, 0))],
        out_specs=[pl.BlockSpec((tl, 2 * D_HY), lambda i: (i, 0)), pl.BlockSpec((1, 2 * D_HY), const)],
        out_shape=[jax.ShapeDtypeStruct((2 * length, 2 * D_HY), F32), jax.ShapeDtypeStruct((1, 2 * D_HY), F32)],
        compiler_params=_params(("arbitrary",)), name="hyena_filter_mlp",
    )(feat, w1p, b1p, w2p, b2p, w3p, fqp, dec)


@functools.lru_cache(maxsize=None)
def _fft_tables(length):
    n = 2 * length
    n1 = n // FFT_N2
    nh = n1 // 2
    k1 = np.arange(n1)[None, :, None].astype(np.float64)
    j = np.arange(nh)[None, None, :].astype(np.float64)
    n2 = np.arange(FFT_N2)[:, None, None].astype(np.float64)
    ang = -2.0 * np.pi * k1 * (FFT_N2 * j + n2) / n
    gf = np.concatenate([np.cos(ang), np.sin(ang)], axis=1)
    gi = np.concatenate([np.cos(ang).transpose(0, 2, 1), np.sin(ang).transpose(0, 2, 1)], axis=1)
    kk = np.arange(FFT_N2)[:, None] * np.arange(FFT_N2)[None, :]
    a2 = -2.0 * np.pi * kk / FFT_N2
    f2 = np.concatenate([np.cos(a2), np.sin(a2)], axis=0)
    cv = lambda m: np.asarray(m, np.float32).astype(BF16)
    return cv(gf), cv(gi), cv(f2), n1


def _stage1(n1, z_of, gf_ref, scr_ref, cols):
    def group(gidx, c):
        n2s = [gidx * FFT_UNROLL + u for u in range(FFT_UNROLL)]
        zs = [z_of(n2) for n2 in n2s]
        rs = [_dot(gf_ref[n2], z) for n2, z in zip(n2s, zs)]
        for n2, r in zip(n2s, rs):
            off = pl.multiple_of(n2 * (n1 + FFT_PAD), 8)
            if cols == 256:
                scr_ref[0, pl.ds(off, n1), :] = r[:n1, :128] - r[n1:, 128:]
                scr_ref[1, pl.ds(off, n1), :] = r[n1:, :128] + r[:n1, 128:]
            else:
                scr_ref[0, pl.ds(off, n1), :] = r[:n1]
                scr_ref[1, pl.ds(off, n1), :] = r[n1:]
        return c

    lax.fori_loop(0, FFT_N2 // FFT_UNROLL, group, 0)


def _stage2(n1, k1, f2_ref, scr_ref):
    ar = scr_ref[0, pl.ds(k1, FFT_N2, stride=n1 + FFT_PAD), :]
    ai = scr_ref[1, pl.ds(k1, FFT_N2, stride=n1 + FFT_PAD), :]
    r = _dot(f2_ref[...], jnp.concatenate([ar, ai], axis=1).astype(BF16))
    return r[:128, :128] - r[128:, 128:], r[128:, :128] + r[:128, 128:]


def _hyconv_kernel(n1, kg, gated, z_ref, gf_ref, f2_ref, kf_ref, gi_ref, *rest):
    if gated:
        x1_ref, b_ref, o_ref, scr_ref, tp_ref = rest
    else:
        o_ref, scr_ref, tp_ref = rest
    s = pl.program_id(2)
    nh = n1 // 2
    pitch = FFT_N2 + FFT_PAD

    @pl.when(s == 0)
    def _():
        for e in range(2):
            for j in range(nh):
                tp_ref[e, j * pitch:j * pitch + FFT_N2, :] = z_ref[0, e, j * FFT_N2:(j + 1) * FFT_N2, :].astype(F32)

        def z_of(n2):
            zr = tp_ref[0, pl.ds(n2, nh, stride=pitch), :]
            zi = tp_ref[1, pl.ds(n2, nh, stride=pitch), :]
            return jnp.concatenate([zr, zi], axis=1).astype(BF16)
        _stage1(n1, z_of, gf_ref, scr_ref, 256)

    def k1_group(gidx, c):
        idx = [gidx * FFT_UNROLL + u for u in range(FFT_UNROLL)]
        xs = [_stage2(n1, s * kg + i, f2_ref, scr_ref) for i in idx]
        ys = []
        for i, (xr, xi) in zip(idx, xs):
            off = pl.multiple_of(i * FFT_N2, FFT_N2)
            kr = kf_ref[0, 0, pl.ds(off, FFT_N2), :]
            ki = kf_ref[0, 1, pl.ds(off, FFT_N2), :]
            ys.append(jnp.concatenate([xr * kr - xi * ki, xr * ki + xi * kr], axis=1).astype(BF16))
        rs = [_dot(f2_ref[...], yc) for yc in ys]
        for i, r in zip(idx, rs):
            k1 = s * kg + i
            scr_ref[0, pl.ds(k1, FFT_N2, stride=n1 + FFT_PAD), :] = r[:128, :128] + r[128:, 128:]
            scr_ref[1, pl.ds(k1, FFT_N2, stride=n1 + FFT_PAD), :] = r[:128, 128:] - r[128:, :128]
        return c

    lax.fori_loop(0, kg // FFT_UNROLL, k1_group, 0)

    @pl.when(s == pl.num_programs(2) - 1)
    def _():
        def group(gidx, c):
            n2s = [gidx * FFT_UNROLL + u for u in range(FFT_UNROLL)]
            bcs = []
            for n2 in n2s:
                off = pl.multiple_of(n2 * (n1 + FFT_PAD), 8)
                bcs.append(jnp.concatenate([scr_ref[0, pl.ds(off, n1), :], scr_ref[1, pl.ds(off, n1), :]],
                                           axis=1).astype(BF16))
            rs = [_dot(gi_ref[n2], bc) for n2, bc in zip(n2s, bcs)]
            for n2, r in zip(n2s, rs):
                tp_ref[0, pl.ds(n2, nh, stride=pitch), :] = r[:nh, :128] + r[nh:, 128:]
                tp_ref[1, pl.ds(n2, nh, stride=pitch), :] = r[:nh, 128:] - r[nh:, :128]
            return c
        lax.fori_loop(0, FFT_N2 // FFT_UNROLL, group, 0)
        for e in range(2):
            for j in range(nh):
                rows = slice(j * FFT_N2, (j + 1) * FFT_N2)
                y = tp_ref[e, j * pitch:j * pitch + FFT_N2, :]
                if gated:
                    y = x1_ref[0, e, rows, :].astype(F32) * (y + z_ref[0, e, rows, :].astype(F32) * b_ref[...])
                o_ref[0, e, rows, :] = y.astype(BF16)


def _hyena_conv(z4, kf, order, length, x1_col0=None, bias=None):
    gf, gi, f2, n1 = _fft_tables(length)
    n = 2 * length
    kg = min(n1, 2 * FFT_UNROLL)
    npair = z4.shape[0]
    big = pl.Buffered(1)
    gated = bias is not None
    in_specs = [pl.BlockSpec((1, 2, length, 128), lambda p, c, s: (p, 0, 0, c)),
                pl.BlockSpec(gf.shape, lambda p, c, s: (0, 0, 0), pipeline_mode=big),
                pl.BlockSpec(f2.shape, lambda p, c, s: (0, 0)),
                pl.BlockSpec((1, 2, kg * FFT_N2, 128), lambda p, c, s: (order, 0, s, c)),
                pl.BlockSpec(gi.shape, lambda p, c, s: (0, 0, 0), pipeline_mode=big)]
    args = [z4, gf, f2, kf, gi]
    if gated:
        in_specs += [pl.BlockSpec((1, 2, length, 128), lambda p, c, s: (p, 0, 0, x1_col0 + c), pipeline_mode=big),
                     pl.BlockSpec((1, 128), lambda p, c, s: (0, c))]
        args += [z4, bias.reshape(1, D_HY)]
    return pl.pallas_call(
        functools.partial(_hyconv_kernel, n1, kg, gated), grid=(npair, 2, n1 // kg),
        in_specs=in_specs,
        out_specs=pl.BlockSpec((1, 2, length, 128), lambda p, c, s: (p, 0, 0, c),
                               pipeline_mode=pl.Buffered(1 if gated else 2)),
        out_shape=jax.ShapeDtypeStruct((npair, 2, length, 256), BF16),
        scratch_shapes=[pltpu.VMEM((2, FFT_N2 * (n1 + FFT_PAD), 128), F32),
                        pltpu.VMEM((2, (n1 // 2) * (FFT_N2 + FFT_PAD), 128), F32)],
        compiler_params=_params(("arbitrary", "arbitrary", "arbitrary"), FFTCONV_VMEM_LIMIT), name="hyena_fftconv",
    )(*args)


def _hyfft_kernel(n1, kg, z_ref, gf_ref, f2_ref, o_ref, scr_ref, tp_ref):
    s = pl.program_id(2)
    pitch = FFT_N2 + FFT_PAD

    @pl.when(s == 0)
    def _():
        for j in range(n1):
            tp_ref[j * pitch:j * pitch + FFT_N2, :] = z_ref[j * FFT_N2:(j + 1) * FFT_N2, :]
        _stage1(n1, lambda n2: tp_ref[pl.ds(n2, n1, stride=pitch), :].astype(BF16), gf_ref, scr_ref, 128)

    def k1_group(gidx, c):
        idx = [gidx * FFT_UNROLL + u for u in range(FFT_UNROLL)]
        xs = [_stage2(n1, s * kg + i, f2_ref, scr_ref) for i in idx]
        for i, (xr, xi) in zip(idx, xs):
            off = pl.multiple_of(i * FFT_N2, FFT_N2)
            o_ref[0, 0, pl.ds(off, FFT_N2), :] = xr
            o_ref[0, 1, pl.ds(off, FFT_N2), :] = xi
        return c

    lax.fori_loop(0, kg // FFT_UNROLL, k1_group, 0)


@functools.lru_cache(maxsize=None)
def _fft_table_full(length):
    n = 2 * length
    n1 = n // FFT_N2
    k1 = np.arange(n1)[None, :, None].astype(np.float64)
    j = np.arange(n1)[None, None, :].astype(np.float64)
    n2 = np.arange(FFT_N2)[:, None, None].astype(np.float64)
    ang = -2.0 * np.pi * k1 * (FFT_N2 * j + n2) / n
    return np.concatenate([np.cos(ang), np.sin(ang)], axis=1).astype(np.float32).astype(BF16)


def _filter_spectrum(k2, length):
    _, _, f2, n1 = _fft_tables(length)
    gf = _fft_table_full(length)
    n = 2 * length
    kg = min(n1, 16)
    nf = k2.shape[1] // D_HY
    big = pl.Buffered(1)
    return pl.pallas_call(
        functools.partial(_hyfft_kernel, n1, kg), grid=(nf, 2, n1 // kg),
        in_specs=[pl.BlockSpec((n, 128), lambda f, c, s: (0, 2 * f + c), pipeline_mode=big),
                  pl.BlockSpec(gf.shape, lambda f, c, s: (0, 0, 0), pipeline_mode=big),
                  pl.BlockSpec(f2.shape, lambda f, c, s: (0, 0))],
        out_specs=pl.BlockSpec((1, 2, kg * FFT_N2, 128), lambda f, c, s: (f, 0, s, c)),
        out_shape=jax.ShapeDtypeStruct((nf, 2, n, 256), F32),
        scratch_shapes=[pltpu.VMEM((2, FFT_N2 * (n1 + FFT_PAD), 128), F32),
                        pltpu.VMEM((n1 * (FFT_N2 + FFT_PAD), 128), F32)],
        compiler_params=_params(("arbitrary", "arbitrary", "arbitrary")), name="hyena_filter_fft",
    )(k2, gf, f2)


def _normalised(k2, ssq, extra_scale):
    return k2 * (lax.rsqrt(ssq + EPS) * extra_scale)


def _hy_ctx_kernel(lc, hv_ref, zh_ref, k2_ref, b_ref, fc_ref, ff_ref, ci_ref, o_ref):
    nf = 2 * lc
    hv = hv_ref[0].astype(F32)

    def conv(z, o):
        kf = _dot_exact_rhs(ff_ref[...], k2_ref[:, o * D_HY:(o + 1) * D_HY])
        zs = _dot(fc_ref[...], z.astype(BF16))
        zr, zi = zs[:nf], zs[nf:]
        kr, ki = kf[:nf], kf[nf:]
        ys = jnp.concatenate([zr * kr - zi * ki, zr * ki + zi * kr], axis=0).astype(BF16)
        return _dot(ci_ref[...], ys) * (1.0 / nf)

    v = hv[:, 0:D_HY]
    x1 = hv[:, D_HY:2 * D_HY]
    x2 = hv[:, 2 * D_HY:3 * D_HY]
    z1 = x1 * (conv(v, 0) + v * b_ref[0:1])
    p = x2 * (conv(z1, 1) + z1 * b_ref[1:2])
    o_ref[0] = (p * _silu(zh_ref[0].astype(F32))).astype(BF16)


def _dot_exact_rhs(a_bf16, b):
    bh, bm, bl = _split3(b)
    return _dot(a_bf16, bh) + _dot(a_bf16, bm) + _dot(a_bf16, bl)


@functools.lru_cache(maxsize=None)
def _ctx_dft_tables(lc):
    nf = 2 * lc
    k = np.arange(nf)[:, None].astype(np.float64)
    ang = -2.0 * np.pi * k * np.arange(nf)[None, :] / nf
    ff = np.concatenate([np.cos(ang), np.sin(ang)], axis=0)
    fc = ff[:, :lc]
    a2 = 2.0 * np.pi * np.arange(lc)[:, None] * np.arange(nf)[None, :] / nf
    ci = np.concatenate([np.cos(a2), -np.sin(a2)], axis=1)
    cv = lambda m: np.asarray(m, np.float32).astype(BF16)
    return cv(fc), cv(ff), cv(ci)


def _hy_ctx(hv, pf, k2c, hy_bias, nt, lc):
    nb = hv.shape[0]
    fc, ff, ci = _ctx_dft_tables(lc)
    c2 = lambda b: (0, 0)
    return pl.pallas_call(
        functools.partial(_hy_ctx_kernel, lc), grid=(nb,),
        in_specs=[pl.BlockSpec((1, lc, 768), lambda b: (b, nt, 0)),
                  pl.BlockSpec((1, lc, D_HY), lambda b: (b, nt, PF_ZH)),
                  pl.BlockSpec(k2c.shape, c2),
                  pl.BlockSpec((2, D_HY), c2),
                  pl.BlockSpec(fc.shape, c2), pl.BlockSpec(ff.shape, c2), pl.BlockSpec(ci.shape, c2)],
        out_specs=pl.BlockSpec((1, lc, D_HY), lambda b: (b, 0, 0)),
        out_shape=jax.ShapeDtypeStruct((nb, lc, D_HY), BF16),
        compiler_params=_params(("arbitrary",)), name="hyena_ctx",
    )(hv, pf, k2c, hy_bias, fc, ff, ci)


def _out_kernel(n_hy, na_ref, h0_ref, h1_ref, om_ref, zm_ref, w_ref, x_ref, mod_ref, g_ref, *rest):
    hy_refs, o_ref = rest[:n_hy], rest[n_hy]
    chunks = [slice(r, r + TT) for r in range(0, o_ref.shape[1], TT)]
    acts = []
    for rows in chunks:
        if n_hy == 1:
            hy = hy_refs[0][0, rows]
        else:
            y_ref, z_ref, x2_ref, zh_ref, b_ref = hy_refs
            z = z_ref[0, rows].astype(F32)
            hy = (x2_ref[0, rows].astype(F32) * (y_ref[0, rows].astype(F32) + z * b_ref[...])
                  * _silu(zh_ref[0, rows].astype(F32))).astype(BF16)
        hsum = h0_ref[0, 0, rows].astype(F32) + h1_ref[0, 0, rows].astype(F32)
        ml = (jax.nn.sigmoid(om_ref[0, rows].astype(F32)) * hsum * _silu(zm_ref[0, rows].astype(F32))).astype(BF16)
        acts.append((ml, hy))
    ys = [_dot(na_ref[0, rows], w_ref[0:D_NA]) + _dot(ml, w_ref[D_NA:D_NA + D_ML])
          + _dot(hy, w_ref[D_NA + D_ML:D_MODEL]) for rows, (ml, hy) in zip(chunks, acts)]
    for rows, y in zip(chunks, ys):
        ms = jnp.mean(y * y, axis=-1, keepdims=True)
        r = y * lax.rsqrt(ms + EPS) * g_ref[...]
        o_ref[0, rows] = x_ref[0, rows] + mod_ref[0][2:3] * r


def _out_projection(na_g, hml, pf, hy_args, w_out, res, res_t0, mod3, g_post, to, t0, ntiles, mod_row, out_rows,
                    prev=None):
    nb = na_g.shape[0]
    d = D_MODEL
    tok = lambda col: (lambda b, t: (b, t + t0, col))
    own = lambda col: (lambda b, t: (b, t, col))
    in_specs = [pl.BlockSpec((1, to, 512), tok(0)),
                pl.BlockSpec((1, 1, to, 256), lambda b, t: (b, 0, t + t0, 0)),
                pl.BlockSpec((1, 1, to, 256), lambda b, t: (b, 1, t + t0, 0)),
                pl.BlockSpec((1, to, 256), tok(PF_OM)), pl.BlockSpec((1, to, 256), tok(PF_ZM)),
                pl.BlockSpec((d, d), lambda b, t: (0, 0), pipeline_mode=pl.Buffered(1)),
                pl.BlockSpec((1, to, d), lambda b, t: (b, t + res_t0, 0)),
                pl.BlockSpec((1, 3, d), lambda b, t: (mod_row(b), 0, 0)),
                pl.BlockSpec((1, d), lambda b, t: (0, 0))]
    args = [na_g, hml, hml, pf, pf, w_out, res, mod3, g_post.reshape(1, d)]
    if len(hy_args) == 1:
        in_specs.append(pl.BlockSpec((1, to, D_HY), own(0)))
        args.append(hy_args[0])
    else:
        y2, z1, hv, pf_, bias1 = hy_args
        in_specs += [pl.BlockSpec((1, to, D_HY), own(0)), pl.BlockSpec((1, to, D_HY), own(0)),
                     pl.BlockSpec((1, to, D_HY), tok(2)), pl.BlockSpec((1, to, D_HY), tok(PF_ZH)),
                     pl.BlockSpec((1, D_HY), lambda b, t: (0, 0))]
        args += [y2, z1, hv, pf_, bias1.reshape(1, D_HY)]
    n_hy = len(hy_args)
    aliases = {}
    if prev is not None:
        in_specs.append(pl.BlockSpec(memory_space=pl.ANY))
        args.append(prev)
        aliases = {len(args) - 1: 0}
        kern = lambda *r: _out_kernel(n_hy, *r[:9 + n_hy], r[10 + n_hy])
    else:
        kern = functools.partial(_out_kernel, n_hy)
    return pl.pallas_call(
        kern, grid=(nb, ntiles), in_specs=in_specs,
        out_specs=pl.BlockSpec((1, to, d), tok(0)),
        out_shape=jax.ShapeDtypeStruct((nb, out_rows, d), F32),
        input_output_aliases=aliases,
        compiler_params=_params(("arbitrary", "arbitrary")), name="out_proj_norm_residual",
    )(*args)


@functools.lru_cache(maxsize=None)
def _rope_tables(length, lc):
    n = HEAD_DIM // 4
    t = np.arange(length)
    inv = (ROPE_BASE ** (-np.arange(n, dtype=np.float32) / n)).astype(np.float32)
    pos = np.stack([t // GRID_W, t % GRID_W], axis=-1).astype(np.float32)
    ang = (pos[:, :, None] * inv).astype(np.float32)
    cos_h = np.concatenate([np.cos(ang[:, 0]), np.cos(ang[:, 0]), np.cos(ang[:, 1]), np.cos(ang[:, 1])], axis=-1)
    sin_h = np.concatenate([-np.sin(ang[:, 0]), np.sin(ang[:, 0]), -np.sin(ang[:, 1]), np.sin(ang[:, 1])], axis=-1)
    cos_t = np.concatenate([np.tile(cos_h, (1, 2 * H_ML)), np.ones((lc, 512), np.float32)], axis=0)
    sin_t = np.concatenate([np.tile(sin_h, (1, 2 * H_ML)), np.zeros((lc, 512), np.float32)], axis=0)
    kscale = np.concatenate([np.ones(256, np.float32), np.full(256, HEAD_DIM ** -0.5, np.float32)])
    return (cos_t * kscale).astype(np.float32), (sin_t * kscale).astype(np.float32)


@functools.lru_cache(maxsize=None)
def _scan_constants():
    tril = np.tril(np.ones((TT, TT), np.float32))
    emat = np.zeros((128, D_ML), np.float32)
    for h in range(H_ML):
        emat[h, h * HEAD_DIM:(h + 1) * HEAD_DIM] = 1.0
    rh = np.arange(D_ML)[:, None] // HEAD_DIM
    chd = np.arange(D_ML)[None, :] // HEAD_DIM
    bmask = (rh == chd).astype(np.float32)
    tri = np.concatenate([tril, tril.T], axis=0)
    causal = np.stack([np.where(tril > 0, 0.0, -BIG), np.where(tril.T > 0, 0.0, -BIG)]).astype(np.float32)
    return tri.astype(BF16), emat.astype(BF16), bmask, causal


def _layer(x_lat, x_ctx, ctx_tile, mod3, nb, length, lc, g_pre, g_post, w_in, b_if, conv_ml, conv_hy, rpb,
           hf_w1, hf_b1, hf_w2, hf_b2, hf_w3, hf_freq, hy_bias, w_out, update_ctx):
    nt = length // TT
    ltot = length + lc
    tri, emat, bmask, causal = _scan_constants()
    cos_t, sin_t = _rope_tables(length, lc)

    c = lambda a, b: w_in[:, a:b]
    wna = jnp.concatenate([c(0, 512) * (HEAD_DIM ** -0.5 * LOG2E), c(512, 1536)], axis=1).astype(BF16)
    wpr = jnp.concatenate([c(1536, 2048), c(2560, 2816), c(2816, 3072), c(3072, 3328), c(4112, 4368)],
                          axis=1).astype(BF16)
    wlo = jnp.concatenate([c(2048, 2560), c(3344, 4112), c(3328, 3344), jnp.zeros((D_MODEL, 112), F32)],
                          axis=1).astype(BF16)
    bif_row = jnp.zeros((1, 128), F32).at[0, :4 * H_ML].set(b_if.reshape(-1))
    ona, pf, qk, hv, gates, gates_t = _projection(x_lat, x_ctx, ctx_tile, mod3, g_pre, wna, wpr, wlo, conv_ml,
                                                  conv_hy, bif_row, cos_t, sin_t, tri, nt)

    na_g = _attention(ona, pf, _na_bias_table(rpb), nt, update_ctx)

    hml = _mlstm_scan(qk, pf, gates, gates_t, emat, bmask, causal, nt)

    k2, ssq = _hyena_filters(length, hf_w1, hf_b1, hf_w2, hf_b2, hf_w3, hf_freq)
    kf = _filter_spectrum(_normalised(k2, ssq, 1.0 / (2 * length)), length)
    hv4 = hv.reshape(nb // 2, 2, ltot, 768)
    z1 = _hyena_conv(hv4, kf, 0, length, x1_col0=2, bias=hy_bias[0])
    y2 = _hyena_conv(z1, kf, 1, length).reshape(nb, length, D_HY)
    z1 = z1.reshape(nb, length, D_HY)
    hy_args = (y2, z1, hv, pf, hy_bias[1])
    to = 4 * TT
    if not update_ctx:
        return _out_projection(na_g, hml, pf, hy_args, w_out, x_lat, 0, mod3, g_post, to, 0, length // to,
                               lambda b: b, length)

    xn = _out_projection(na_g, hml, pf, hy_args, w_out, x_lat, 0, mod3, g_post, to, 0, length // to,
                         lambda b: b, ltot)
    k2c, ssq_c = _hyena_filters(lc, hf_w1, hf_b1, hf_w2, hf_b2, hf_w3, hf_freq)
    hyc_g = _hy_ctx(hv, pf, _normalised(k2c, ssq_c, 1.0), hy_bias, nt, lc)
    return _out_projection(na_g, hml, pf, (hyc_g,), w_out, x_ctx, ctx_tile, mod3, g_post, TT, nt, 1,
                           lambda b: nb, ltot, prev=xn)


def kernel(x, c, ctx, c_ctx, w_ada, b_ada, g_pre, g_post, w_in, b_if, conv_ml, conv_hy, rpb, hf_w1, hf_b1,
           hf_w2, hf_b2, hf_w3, hf_freq, hy_bias, w_out):
    nb, length, d = x.shape
    lc = ctx.shape[1]
    depth = w_in.shape[0]
    assert lc == TT and length % (4 * TT) == 0 and nb % 2 == 0 and nb % SCAN_NB == 0 and d == D_MODEL
    cc = jnp.zeros((16, d), F32).at[:nb].set(c).at[nb].set(c_ctx)
    x_lat, x_ctx, ctx_tile = x, ctx, 0
    for l in range(depth):
        mod3 = _modulation(cc, w_ada[l], b_ada[l]).reshape(16, 3, d)
        x_lat = _layer(x_lat, x_ctx, ctx_tile, mod3, nb, length, lc, g_pre[l], g_post[l], w_in[l], b_if[l],
                       conv_ml[l], conv_hy[l], rpb[l], hf_w1[l], hf_b1[l], hf_w2[l], hf_b2[l], hf_w3[l],
                       hf_freq[l], hy_bias[l], w_out[l].astype(BF16), l < depth - 1)
        x_ctx, ctx_tile = x_lat, length // TT
    return x_lat
```

```python
import functools
import math

import numpy as np
import jax
import jax.numpy as jnp
from jax import lax
from jax.experimental import pallas as pl
from jax.experimental.pallas import tpu as pltpu

F32 = jnp.float32
BF16 = jnp.bfloat16

D_MODEL = 1024
GRID_W = 64
HEAD_DIM = 64
D_NA = 512
D_ML = 256
D_HY = 256
H_NA = 8
H_ML = 4
NA_ROWS = 8
NA_COLS = 16
HY_EMB = 33
HY_FFN = 64
HY_TARGET = 1e-2
HY_FAST = 0.3
HY_SLOW = 1.5
ROPE_BASE = 10000.0
EPS = 1e-6

TT = 256
BAND_ROWS = TT // GRID_W
SCAN_NB = 8
PRE_ROWS = 64
FFT_N2 = 128
FFT_PAD = 8
FFT_UNROLL = 8
NEG = -1e30
BIG = 3e38
LOG2E = math.log2(math.e)
VMEM_LIMIT = 56 * 1024 * 1024
FFTCONV_VMEM_LIMIT = 59 * 1024 * 1024

PF_ZA = 0
PF_VM, PF_OM, PF_ZM, PF_ZH = 2, 3, 4, 5
LOC_QK, LOC_HY, LOC_GM, LOC_COLS = 0, 512, 1280, 1408
HALO = 8


def _dot(a, b):
    return jnp.dot(a, b, preferred_element_type=F32)


def _dot_nt(a, b):
    return lax.dot_general(a, b, (((1,), (1,)), ((), ())), preferred_element_type=F32)


def _dot_tn(a, b):
    return lax.dot_general(a, b, (((0,), (0,)), ((), ())), preferred_element_type=F32)


def _split3(x):
    h = x.astype(BF16)
    r = x - h.astype(F32)
    m = r.astype(BF16)
    l = (r - m.astype(F32)).astype(BF16)
    return h, m, l


def _dot_f32(a, b):
    ah, am, al = _split3(a)
    bh, bm, bl = _split3(b)
    return (_dot(ah, bh) + (_dot(ah, bm) + _dot(am, bh))
            + (_dot(ah, bl) + _dot(al, bh) + _dot(am, bm)))


def _dot_2x(a, b):
    ah, am, _ = _split3(a)
    bh, bm, _ = _split3(b)
    return _dot(ah, bh) + (_dot(ah, bm) + _dot(am, bh))


def _silu(x):
    return x * jax.nn.sigmoid(x)


def _params(sem, vmem=VMEM_LIMIT):
    return pltpu.CompilerParams(dimension_semantics=sem, vmem_limit_bytes=vmem)


def _mod_kernel(c_ref, w_ref, b_ref, o_ref):
    o_ref[...] = _dot_f32(_silu(c_ref[...]), w_ref[...]) + b_ref[...]


def _modulation(cc, w_ada, b_ada):
    n = w_ada.shape[1]
    bn = 512
    return pl.pallas_call(
        _mod_kernel, grid=(n // bn,),
        in_specs=[pl.BlockSpec(cc.shape, lambda j: (0, 0)),
                  pl.BlockSpec((w_ada.shape[0], bn), lambda j: (0, j)),
                  pl.BlockSpec((1, bn), lambda j: (0, j))],
        out_specs=pl.BlockSpec((cc.shape[0], bn), lambda j: (0, j)),
        out_shape=jax.ShapeDtypeStruct((cc.shape[0], n), F32),
        compiler_params=_params(("arbitrary",)), name="adaln_mod",
    )(cc, w_ada, b_ada.reshape(1, n))


def _proj_kernel(nt, x_ref, c_ref, xp_ref, xn_ref, mod_ref, g_ref, wna_ref, wpr_ref, wlo_ref, cml_ref, chy_ref,
                 bif_ref, cos_ref, sin_ref, tri_ref, ona_ref, opr_ref, oqk_ref, ohv_ref, og_ref, ogt_ref, loc_ref):
    t = pl.program_id(0)
    x = jnp.where(t == nt, c_ref[0], x_ref[0])
    xs = jnp.concatenate([xp_ref[0], x, xn_ref[0]], axis=0)
    ms = jnp.mean(xs * xs, axis=-1, keepdims=True)
    y = xs * lax.rsqrt(ms + EPS) * g_ref[...]
    mod = mod_ref[0]
    h = (y * (1.0 + mod[1:2]) + mod[0:1]).astype(BF16)
    hm = h[HALO:HALO + TT]
    loc_ref[...] = _dot(h, wlo_ref[...])
    ona_ref[0] = _dot(hm, wna_ref[...]).astype(BF16)

    g = loc_ref[HALO:HALO + TT, LOC_GM:LOC_COLS] + bif_ref[...]
    gl = lax.broadcasted_iota(jnp.int32, (1, 128), 1)
    is_f = jnp.logical_and((gl % 8) >= 4, gl < 16)
    lf = jnp.where(is_f, jnp.minimum(g, 0.0) - jnp.log1p(jnp.exp(-jnp.abs(g))), 0.0)
    hi, mid, lo = _split3(lf)
    packed = (hi.astype(F32) + pltpu.roll(mid.astype(F32), 16, 1) + pltpu.roll(lo.astype(F32), 32, 1)).astype(BF16)
    cs = _dot(tri_ref[...], packed)
    cs = cs + pltpu.roll(cs, 112, 1) + pltpu.roll(cs, 96, 1)
    bcum = jnp.where(gl < 8, cs[0:TT], cs[TT:2 * TT])
    a = g - pltpu.roll(bcum, 128 - H_ML, 1)
    at = jnp.where(is_f, 0.0, a).T[0:16]
    tlane = lax.broadcasted_iota(jnp.int32, (1, TT), 1)
    pre = suf = at
    shift = 1
    while shift < TT:
        pre = jnp.maximum(pre, jnp.where(tlane >= shift, pltpu.roll(pre, shift, 1), -BIG))
        suf = jnp.maximum(suf, jnp.where(tlane < TT - shift, pltpu.roll(suf, TT - shift, 1), -BIG))
        shift *= 2
    cm_t = jnp.where(lax.broadcasted_iota(jnp.int32, (16, 1), 0) < 8, pre, suf)
    cmax = jnp.concatenate([cm_t, jnp.zeros((128 - 16, TT), F32)], axis=0).T
    ab = jnp.where(is_f, bcum, a)
    og_ref[0, 0] = LOG2E * (jnp.where(gl < 8, ab, 0.0) + pltpu.roll(jnp.where(gl < H_ML, cmax, 0.0), 8, 1))
    og_ref[0, 1] = LOG2E * (pltpu.roll(jnp.where(jnp.logical_and(gl >= 8, gl < 16), ab, 0.0), 120, 1)
                            + jnp.where(jnp.logical_and(gl >= 8, gl < 8 + H_ML), cmax, 0.0))
    ogt_ref[0] = LOG2E * at
    opr_ref[0] = _dot(hm, wpr_ref[...]).astype(BF16)

    pm = jnp.where(jnp.logical_and(t != 0, t != nt), 1.0, 0.0)
    nm = jnp.where(jnp.logical_and(t != nt - 1, t != nt), 1.0, 0.0)
    row8 = lax.broadcasted_iota(jnp.int32, (8, 1), 0)
    nchunk = TT // PRE_ROWS

    def conv3(cols, w, c):
        r0 = HALO + c * PRE_ROWS
        u = loc_ref[r0:r0 + PRE_ROWS, cols]
        up = loc_ref[r0 - 1:r0, cols]
        un = loc_ref[r0 + PRE_ROWS:r0 + PRE_ROWS + 1, cols]
        if c == 0:
            up = up * pm
        if c == nchunk - 1:
            un = un * nm
        rp = pltpu.roll(u, 1, 0)
        rn = pltpu.roll(u, PRE_ROWS - 1, 0)
        prev = jnp.concatenate([jnp.where(row8 == 0, up, rp[0:8]), rp[8:]], axis=0)
        nxt = jnp.concatenate([rn[:PRE_ROWS - 8], jnp.where(row8 == 7, un, rn[PRE_ROWS - 8:])], axis=0)
        return prev * w[0:1] + u * w[1:2] + nxt * w[2:3]

    lane = lax.broadcasted_iota(jnp.int32, (1, 2 * D_ML), 1)
    first = (lane % 32) < 16
    for c in range(nchunk):
        rows = slice(c * PRE_ROWS, (c + 1) * PRE_ROWS)
        ohv_ref[0, rows] = conv3(slice(LOC_HY, LOC_GM), chy_ref[...], c).astype(BF16)
        x = _silu(conv3(slice(LOC_QK, LOC_HY), cml_ref[...], c))
        partner = jnp.where(first, pltpu.roll(x, 2 * D_ML - 16, 1), pltpu.roll(x, 16, 1))
        oqk_ref[0, rows] = (x * cos_ref[rows] + partner * sin_ref[rows]).astype(BF16)


def _projection(x_lat, x_ctx, ctx_tile, mod3, g_pre, wna, wpr, wlo, conv_ml, conv_hy, bif_row, cos_t, sin_t,
                tri, nt):
    nb, lat_rows, d = x_lat.shape
    ltot = (nt + 1) * TT
    ctx_row = nb
    last8 = lat_rows // HALO - 1
    r8 = TT // HALO

    def mod_idx(t, b):
        return (jnp.where(t == nt, ctx_row, b), 0, 0)

    const = lambda t, b: (0, 0)
    big = pl.Buffered(1)
    tok = lambda t, b: (b, t, 0)
    return pl.pallas_call(
        functools.partial(_proj_kernel, nt), grid=(nt + 1, nb),
        in_specs=[pl.BlockSpec((1, TT, d), lambda t, b: (b, jnp.minimum(t, nt - 1), 0)),
                  pl.BlockSpec((1, TT, d), lambda t, b: (b, ctx_tile, 0)),
                  pl.BlockSpec((1, HALO, d), lambda t, b: (b, jnp.clip(t * r8 - 1, 0, last8), 0)),
                  pl.BlockSpec((1, HALO, d), lambda t, b: (b, jnp.minimum((t + 1) * r8, last8), 0)),
                  pl.BlockSpec((1, 3, d), mod_idx),
                  pl.BlockSpec((1, d), const),
                  pl.BlockSpec(wna.shape, const, pipeline_mode=big),
                  pl.BlockSpec(wpr.shape, const, pipeline_mode=big),
                  pl.BlockSpec(wlo.shape, const, pipeline_mode=big),
                  pl.BlockSpec((3, 512), const), pl.BlockSpec((3, 768), const), pl.BlockSpec((1, 128), const),
                  pl.BlockSpec((TT, 512), lambda t, b: (t, 0)), pl.BlockSpec((TT, 512), lambda t, b: (t, 0)),
                  pl.BlockSpec((2 * TT, TT), const)],
        out_specs=[pl.BlockSpec((1, TT, wna.shape[1]), tok), pl.BlockSpec((1, TT, wpr.shape[1]), tok),
                   pl.BlockSpec((1, TT, 512), tok), pl.BlockSpec((1, TT, 768), tok),
                   pl.BlockSpec((1, 2, TT, 128), lambda t, b: (b, 0, t, 0)),
                   pl.BlockSpec((1, 16, TT), lambda t, b: (b, 0, t))],
        out_shape=[jax.ShapeDtypeStruct((nb, ltot, wna.shape[1]), BF16),
                   jax.ShapeDtypeStruct((nb, ltot, wpr.shape[1]), BF16),
                   jax.ShapeDtypeStruct((nb, ltot, 512), BF16),
                   jax.ShapeDtypeStruct((nb, ltot, 768), BF16),
                   jax.ShapeDtypeStruct((nb, 2, ltot, 128), F32),
                   jax.ShapeDtypeStruct((nb, 16, ltot), F32)],
        scratch_shapes=[pltpu.VMEM((TT + 2 * HALO, LOC_COLS), F32)],
        compiler_params=_params(("arbitrary", "arbitrary")), name="norm_in_proj_conv",
    )(x_lat, x_ctx, x_lat, x_lat, mod3, g_pre.reshape(1, d), wna, wpr, wlo, conv_ml, conv_hy, bif_row,
      cos_t, sin_t, tri)


def _na_kernel(q_ref, k0_ref, k1_ref, k2_ref, v0_ref, v1_ref, v2_ref, kc_ref, vc_ref, za_ref, tab_ref, o_ref):
    lane = lax.broadcasted_iota(jnp.int32, (1, 128), 1)
    k_refs = (k0_ref, k1_ref, k2_ref)
    v_refs = (v0_ref, v1_ref, v2_ref)

    def row_max(s):
        return jnp.max(jnp.maximum(s[:, :128], s[:, 128:]), axis=-1, keepdims=True)

    def head_mask(h):
        return (lane >= HEAD_DIM) if h % 2 else (lane < HEAD_DIM)

    def masked_q(h):
        sl = slice((h // 2) * 128, (h // 2 + 1) * 128)
        q2 = q_ref[0, :, sl]
        return jnp.where(head_mask(h), q2, jnp.zeros_like(q2))

    def score_block(h, qm, blk):
        sl = slice((h // 2) * 128, (h // 2 + 1) * 128)
        if blk == 0:
            return _dot_nt(qm, kc_ref[0, :, sl])
        return _dot_nt(qm, k_refs[blk - 1][0, :, sl]) + tab_ref[0, h, :, (blk - 1) * TT:blk * TT]

    def values(h, p):
        sl = slice((h // 2) * 128, (h // 2 + 1) * 128)
        hm = head_mask(h)
        v = jnp.concatenate([vc_ref[0, :, sl]] + [r[0, :, sl] for r in v_refs], axis=0)
        acc = _dot(p, jnp.where(hm, v, jnp.ones_like(v)))
        return jnp.where(hm, acc / pltpu.roll(acc, HEAD_DIM, 1), 0.0)

    outs = {}
    qm = masked_q(0)
    sc = [score_block(0, qm, blk) for blk in range(4)]
    for h in range(H_NA):
        m = row_max(jnp.maximum(jnp.maximum(sc[0], sc[1]), jnp.maximum(sc[2], sc[3])))
        qm = masked_q(h + 1) if h + 1 < H_NA else None
        nxt, ps = [], []
        for blk in range(4):
            if qm is not None:
                nxt.append(score_block(h + 1, qm, blk))
            ps.append(jnp.exp2(sc[blk] - m).astype(BF16))
        sc = nxt
        outs[h] = values(h, jnp.concatenate(ps, axis=1))
        if h % 2 == 1:
            sl = slice((h // 2) * 128, (h // 2 + 1) * 128)
            o_ref[0, :, sl] = ((outs[h - 1] + outs[h]) * _silu(za_ref[0, :, sl].astype(F32))).astype(BF16)


def _attention(ona, pf, table, nt, with_ctx_queries):
    nb, ltot, _ = ona.shape
    nj = nt + 1 if with_ctx_queries else nt

    def kv(col, dj):
        return lambda j, b: (b, jnp.clip(j + dj, 0, nt - 1), col)

    def pat(j, b):
        p = jnp.where(j == 0, 0, jnp.where(j == nt - 1, 2, jnp.where(j == nt, 3, 1)))
        return (p, 0, 0, 0)

    return pl.pallas_call(
        _na_kernel, grid=(nj, nb),
        in_specs=[pl.BlockSpec((1, TT, 512), lambda j, b: (b, j, 0)),
                  pl.BlockSpec((1, TT, 512), kv(1, -1)), pl.BlockSpec((1, TT, 512), kv(1, 0)),
                  pl.BlockSpec((1, TT, 512), kv(1, 1)),
                  pl.BlockSpec((1, TT, 512), kv(2, -1)), pl.BlockSpec((1, TT, 512), kv(2, 0)),
                  pl.BlockSpec((1, TT, 512), kv(2, 1)),
                  pl.BlockSpec((1, TT, 512), lambda j, b: (b, nt, 1)),
                  pl.BlockSpec((1, TT, 512), lambda j, b: (b, nt, 2)),
                  pl.BlockSpec((1, TT, 512), lambda j, b: (b, j, PF_ZA)),
                  pl.BlockSpec((1, H_NA, TT, 3 * TT), pat)],
        out_specs=pl.BlockSpec((1, TT, 512), lambda j, b: (b, j, 0)),
        out_shape=jax.ShapeDtypeStruct((nb, ltot, 512), BF16),
        compiler_params=_params(("arbitrary", "arbitrary")), name="nbr_attention",
    )(ona, ona, ona, ona, ona, ona, ona, ona, ona, pf, table)


def _na_bias_table(rpb):
    a = np.arange(TT) // GRID_W
    qc = np.arange(TT) % GRID_W
    kidx = np.arange(3 * TT)
    kr = (kidx // TT - 1) * BAND_ROWS + (kidx % TT) // GRID_W
    kcol = kidx % GRID_W
    qstart = np.clip(qc - NA_COLS // 2, 0, GRID_W - NA_COLS)
    col_ok = (kcol[None, :] >= qstart[:, None]) & (kcol[None, :] < qstart[:, None] + NA_COLS)
    coff = np.clip(kcol[None, :] - qc[:, None], -(NA_COLS - 1), NA_COLS - 1) + NA_COLS - 1
    roff = np.clip(kr[None, :] - a[:, None] + NA_ROWS - 1, 0, 2 * NA_ROWS - 2)
    half = NA_ROWS // 2
    row_ok = [
        (kr[None, :] >= 0) & (kr[None, :] < NA_ROWS) & (a[:, None] >= 0),
        (kr[None, :] >= a[:, None] - half) & (kr[None, :] < a[:, None] + half),
        (kr[None, :] >= BAND_ROWS - NA_ROWS) & (kr[None, :] < BAND_ROWS) & (a[:, None] >= 0),
    ]
    nr, ncol = 2 * NA_ROWS - 1, 2 * NA_COLS - 1
    onehot = (coff[:GRID_W, :GRID_W].reshape(1, -1) == np.arange(ncol)[:, None]).astype(np.float32)
    cmat = jnp.dot(rpb.astype(F32).reshape(H_NA * nr, ncol), onehot,
                   precision=lax.Precision.HIGHEST).reshape(H_NA, nr, GRID_W, GRID_W)
    nkr = 3 * BAND_ROWS
    blocks = [jnp.concatenate([cmat[:, roff[ai * GRID_W, kj * GRID_W]] for kj in range(nkr)], axis=-1)
              for ai in range(BAND_ROWS)]
    bias = jnp.concatenate(blocks, axis=1)
    bias = bias * LOG2E
    tabs = [jnp.where((r & col_ok)[None], bias, NEG) for r in row_ok]
    tabs.append(jnp.full_like(bias, NEG))
    return jnp.stack(tabs, axis=0)


def _expand(small, e_ref):
    sh = small.astype(BF16)
    sl = (small - sh.astype(F32)).astype(BF16)
    return _dot(sh, e_ref[...]) + _dot(sl, e_ref[...])


def _head_sums(x):
    lane = lax.broadcasted_iota(jnp.int32, (1, 128), 1)
    out = jnp.zeros((x.shape[0], 128), F32)
    for h in range(H_ML):
        t = x[:, (h // 2) * 128:(h // 2 + 1) * 128]
        sm = jnp.sum(jnp.where(lane // HEAD_DIM == h % 2, t, 0.0), axis=-1, keepdims=True)
        out = jnp.where(lane == h, sm, out)
    return out


def _scan_kernel(q_ref, k_ref, v_ref, g_ref, gt_ref, e_ref, bm_ref, cm_ref, o_ref, w_ref, n_ref, m_ref):
    s = pl.program_id(2)

    @pl.when(s == 0)
    def _():
        w_ref[...] = jnp.zeros_like(w_ref)
        n_ref[...] = jnp.zeros_like(n_ref)
        m_ref[...] = jnp.zeros_like(m_ref)

    lane_h = lax.broadcasted_iota(jnp.int32, (1, D_ML), 1) // HEAD_DIM
    gate_iota = lax.broadcasted_iota(jnp.int32, (1, 128), 1)
    gate_lanes = gate_iota < H_ML

    seqs = range(SCAN_NB)
    gate, kws, kwfs, vbs, mrows = [], [], [], [], []
    for i in seqs:
        gi = g_ref[i, 0]
        a = jnp.where(gate_lanes, gi, 0.0)
        bb = jnp.where(gate_lanes, pltpu.roll(gi, 128 - H_ML, 1), 0.0)
        cmax = jnp.where(gate_lanes, pltpu.roll(gi, 128 - 2 * H_ML, 1), 0.0)
        m_prev = m_ref[i]
        mt = jnp.maximum(m_prev, cmax)
        mc = jnp.maximum(m_prev, jnp.max(a, axis=0, keepdims=True))
        beta_s = jnp.exp2(m_prev - mt)
        small = jnp.concatenate([jnp.exp2(a - mc), beta_s,
                                 jnp.broadcast_to(jnp.exp2(m_prev - mc), (8, 128))], axis=0)
        ex = _expand(small, e_ref)
        gate.append((jnp.exp2(-bb - mt), beta_s, ex[TT:2 * TT], ex[2 * TT:2 * TT + 1],
                     jnp.min(bb, axis=0, keepdims=True) + mc))
        mrows.append(mt)
        kwf = k_ref[i].astype(F32) * ex[0:TT]
        kwfs.append(kwf)
        kws.append(kwf.astype(BF16))
        vbs.append(v_ref[i].astype(BF16))

    scores = [[_dot_nt(jnp.where(lane_h == h, q_ref[i], jnp.zeros_like(q_ref[i])), k_ref[i])
               for h in range(H_ML)] for i in seqs]
    inters = [_dot(q_ref[i], w_ref[i].astype(BF16)) for i in seqs]
    updates = [_dot_tn(kws[i], vbs[i]) for i in seqs]

    def masked(i, h):
        arg = (gt_ref[i, h:h + 1, :] - mrows[i][:, h:h + 1]) + cm_ref[0]
        sc = scores[i][h] * jnp.exp2(arg)
        return sc.astype(BF16), jnp.sum(sc, axis=-1, keepdims=True)

    order = [(i, h) for i in seqs for h in range(H_ML)]
    cur = masked(*order[0])
    intras, dens = {}, {}
    for n, (i, h) in enumerate(order):
        nxt = masked(*order[n + 1]) if n + 1 < len(order) else None
        scb, rs = cur
        part = _dot(scb, jnp.where(lane_h == h, vbs[i], jnp.zeros_like(vbs[i])))
        intras[i] = part if h == 0 else intras[i] + part
        dens[i] = jnp.where(gate_iota == h, rs, dens[i] if h else jnp.zeros((TT, 128), F32))
        cur = nxt

    for i in seqs:
        eb_s, beta_s, beta_t, decay, m_new = gate[i]
        num = beta_t * inters[i] + intras[i]
        den_s = beta_s * _head_sums(q_ref[i].astype(F32) * n_ref[i]) + dens[i]
        den = _expand(jnp.maximum(jnp.abs(den_s), eb_s), e_ref)
        o_ref[i, 0] = (num / den).astype(BF16)

        w_ref[i] = decay * w_ref[i] + bm_ref[...] * updates[i]
        n_ref[i] = decay * n_ref[i] + jnp.sum(kwfs[i], axis=0, keepdims=True)
        m_ref[i] = m_new


def _mlstm_scan(qk, pf, gates, gates_t, emat, bmask, causal, nt):
    nb, ltot, _ = qk.shape
    g = SCAN_NB

    def chunk(s, d):
        return jnp.where(s == 0, nt, jnp.where(d == 0, s - 1, nt - s))

    const = lambda b, d, s: (0, 0)
    return pl.pallas_call(
        _scan_kernel, grid=(nb // g, 2, nt + 1),
        in_specs=[pl.BlockSpec((g, TT, 256), lambda b, d, s: (b, chunk(s, d), 0)),
                  pl.BlockSpec((g, TT, 256), lambda b, d, s: (b, chunk(s, d), 1)),
                  pl.BlockSpec((g, TT, 256), lambda b, d, s: (b, chunk(s, d), PF_VM)),
                  pl.BlockSpec((g, 1, TT, 128), lambda b, d, s: (b, d, chunk(s, d), 0)),
                  pl.BlockSpec((g, 8, TT), lambda b, d, s: (b, d, chunk(s, d))),
                  pl.BlockSpec((128, 256), const), pl.BlockSpec((256, 256), const),
                  pl.BlockSpec((1, TT, TT), lambda b, d, s: (d, 0, 0))],
        out_specs=pl.BlockSpec((g, 1, TT, 256), lambda b, d, s: (b, d, chunk(s, d), 0)),
        out_shape=jax.ShapeDtypeStruct((nb, 2, ltot, 256), BF16),
        scratch_shapes=[pltpu.VMEM((g, D_ML, D_ML), F32), pltpu.VMEM((g, 1, D_ML), F32),
                        pltpu.VMEM((g, 1, 128), F32)],
        compiler_params=_params(("arbitrary", "arbitrary", "arbitrary")), name="mlstm_scan",
    )(qk, qk, pf, gates, gates_t, emat, bmask, causal)


def _hyfilt_kernel(tl, length, feat_ref, w1_ref, b1_ref, w2_ref, b2_ref, w3_ref, fq_ref, dec_ref, o_ref, ss_ref):
    i = pl.program_id(0)
    th = tl // 2
    a = jnp.sin(fq_ref[0:1] * (_dot_2x(feat_ref[...], w1_ref[...]) + b1_ref[...]))
    a = jnp.sin(fq_ref[1:2] * (_dot_2x(a, w2_ref[...]) + b2_ref[...]))
    part = None
    for half in range(2):
        dec = dec_ref[half * th:(half + 1) * th]
        f = _dot_2x(a, w3_ref[half]) * jnp.concatenate([dec, dec, dec, dec], axis=1)
        fwd = jnp.concatenate([f[:, 0:D_HY], f[:, 2 * D_HY:3 * D_HY]], axis=1)
        bwd = jnp.concatenate([f[:, D_HY:2 * D_HY], f[:, 3 * D_HY:4 * D_HY]], axis=1)
        row = lax.broadcasted_iota(jnp.int32, (th, 1), 0) + (i * tl + half * th)
        k2 = jnp.where(row < length, fwd, jnp.where(row > length, bwd, 0.0))
        o_ref[half * th:(half + 1) * th] = k2
        sq = jnp.sum(k2 * k2, axis=0, keepdims=True)
        part = sq if part is None else part + sq

    @pl.when(i == 0)
    def _():
        ss_ref[...] = jnp.zeros_like(ss_ref)

    ss_ref[...] += part


def _hyena_filters(length, w1, b1, w2, b2, w3, freq):
    t = np.arange(length, dtype=np.float32)
    tn = (t / np.float32(length - 1)).astype(np.float32)
    bands = (HY_EMB - 1) // 2
    fr = np.linspace(1e-4, bands - 1, bands, dtype=np.float32)
    ang = (np.float32(2.0 * math.pi / length) * t[:, None] * fr[None, :]).astype(np.float32)
    feat = np.zeros((length, 128), np.float32)
    feat[:, 0] = tn
    feat[:, 1:1 + bands] = np.cos(ang)
    feat[:, 1 + bands:1 + 2 * bands] = -np.sin(ang)
    deltas = np.abs(np.linspace(math.log(HY_TARGET) / HY_SLOW, math.log(HY_TARGET) / HY_FAST, D_HY, dtype=np.float32))
    dec = np.exp(-tn[:, None] * deltas[None, :]).astype(np.float32)
    lag = np.concatenate([np.arange(length), [0], np.arange(length - 1, 0, -1)])
    feat, dec = feat[lag], dec[lag]

    tl = 512
    hw = 64
    fp = feat.reshape(2 * length // tl, 2, tl // 2, 128)[..., :hw]
    feat = np.concatenate([fp[:, 0], fp[:, 1]], axis=-1).reshape(length, 128)
    pad2 = lambda m, r, c: jnp.zeros((r, c), F32).at[:m.shape[0], :m.shape[1]].set(m.astype(F32))
    bdiag = lambda m: jnp.zeros((128, 128), F32).at[:hw, :hw].set(pad2(m, hw, hw)).at[hw:, hw:].set(pad2(m, hw, hw))
    twice = lambda v: jnp.concatenate([pad2(v, v.shape[0], hw)] * 2, axis=1)
    w1p = bdiag(w1)
    w2p = bdiag(w2)
    w3h = pad2(w3, hw, 4 * D_HY)
    w3p = jnp.stack([jnp.concatenate([w3h, jnp.zeros_like(w3h)], axis=0),
                     jnp.concatenate([jnp.zeros_like(w3h), w3h], axis=0)])
    b1p = twice(b1[None])
    b2p = twice(b2[None])
    fqp = twice(freq)
    const = lambda i: (0, 0)
    return pl.pallas_call(
        functools.partial(_hyfilt_kernel, tl, length), grid=(2 * length // tl,),
        in_specs=[pl.BlockSpec((tl // 2, 128), lambda i: (i, 0)),
                  pl.BlockSpec((128, 128), const), pl.BlockSpec((1, 128), const),
                  pl.BlockSpec((128, 128), const), pl.BlockSpec((1, 128), const),
                  pl.BlockSpec((2, 128, 4 * D_HY), lambda i: (0, 0, 0)), pl.BlockSpec((2, 128), const),
                  pl.BlockSpec((tl, D_HY), lambda i: (i, 0))],
        out_specs=[pl.BlockSpec((tl, 2 * D_HY), lambda i: (i, 0)), pl.BlockSpec((1, 2 * D_HY), const)],
        out_shape=[jax.ShapeDtypeStruct((2 * length, 2 * D_HY), F32), jax.ShapeDtypeStruct((1, 2 * D_HY), F32)],
        compiler_params=_params(("arbitrary",)), name="hyena_filter_mlp",
    )(feat, w1p, b1p, w2p, b2p, w3p, fqp, dec)


@functools.lru_cache(maxsize=None)
def _fft_tables(length):
    n = 2 * length
    n1 = n // FFT_N2
    nh = n1 // 2
    k1 = np.arange(n1)[None, :, None].astype(np.float64)
    j = np.arange(nh)[None, None, :].astype(np.float64)
    n2 = np.arange(FFT_N2)[:, None, None].astype(np.float64)
    ang = -2.0 * np.pi * k1 * (FFT_N2 * j + n2) / n
    gf = np.concatenate([np.cos(ang), np.sin(ang)], axis=1)
    gi = np.concatenate([np.cos(ang).transpose(0, 2, 1), np.sin(ang).transpose(0, 2, 1)], axis=1)
    kk = np.arange(FFT_N2)[:, None] * np.arange(FFT_N2)[None, :]
    a2 = -2.0 * np.pi * kk / FFT_N2
    f2 = np.concatenate([np.cos(a2), np.sin(a2)], axis=0)
    cv = lambda m: np.asarray(m, np.float32).astype(BF16)
    return cv(gf), cv(gi), cv(f2), n1


def _stage1(n1, z_of, gf_ref, scr_ref, cols):
    def group(gidx, c):
        n2s = [gidx * FFT_UNROLL + u for u in range(FFT_UNROLL)]
        zs = [z_of(n2) for n2 in n2s]
        rs = [_dot(gf_ref[n2], z) for n2, z in zip(n2s, zs)]
        for n2, r in zip(n2s, rs):
            off = pl.multiple_of(n2 * (n1 + FFT_PAD), 8)
            if cols == 256:
                scr_ref[0, pl.ds(off, n1), :] = r[:n1, :128] - r[n1:, 128:]
                scr_ref[1, pl.ds(off, n1), :] = r[n1:, :128] + r[:n1, 128:]
            else:
                scr_ref[0, pl.ds(off, n1), :] = r[:n1]
                scr_ref[1, pl.ds(off, n1), :] = r[n1:]
        return c

    lax.fori_loop(0, FFT_N2 // FFT_UNROLL, group, 0)


def _stage2(n1, k1, f2_ref, scr_ref):
    ar = scr_ref[0, pl.ds(k1, FFT_N2, stride=n1 + FFT_PAD), :]
    ai = scr_ref[1, pl.ds(k1, FFT_N2, stride=n1 + FFT_PAD), :]
    r = _dot(f2_ref[...], jnp.concatenate([ar, ai], axis=1).astype(BF16))
    return r[:128, :128] - r[128:, 128:], r[128:, :128] + r[:128, 128:]


def _hyconv_kernel(n1, kg, gated, z_ref, gf_ref, f2_ref, kf_ref, gi_ref, *rest):
    if gated:
        x1_ref, b_ref, o_ref, scr_ref, tp_ref = rest
    else:
        o_ref, scr_ref, tp_ref = rest
    s = pl.program_id(2)
    nh = n1 // 2
    pitch = FFT_N2 + FFT_PAD

    @pl.when(s == 0)
    def _():
        for e in range(2):
            for j in range(nh):
                tp_ref[e, j * pitch:j * pitch + FFT_N2, :] = z_ref[0, e, j * FFT_N2:(j + 1) * FFT_N2, :].astype(F32)

        def z_of(n2):
            zr = tp_ref[0, pl.ds(n2, nh, stride=pitch), :]
            zi = tp_ref[1, pl.ds(n2, nh, stride=pitch), :]
            return jnp.concatenate([zr, zi], axis=1).astype(BF16)
        _stage1(n1, z_of, gf_ref, scr_ref, 256)

    def k1_group(gidx, c):
        idx = [gidx * FFT_UNROLL + u for u in range(FFT_UNROLL)]
        xs = [_stage2(n1, s * kg + i, f2_ref, scr_ref) for i in idx]
        ys = []
        for i, (xr, xi) in zip(idx, xs):
            off = pl.multiple_of(i * FFT_N2, FFT_N2)
            kr = kf_ref[0, 0, pl.ds(off, FFT_N2), :]
            ki = kf_ref[0, 1, pl.ds(off, FFT_N2), :]
            ys.append(jnp.concatenate([xr * kr - xi * ki, xr * ki + xi * kr], axis=1).astype(BF16))
        rs = [_dot(f2_ref[...], yc) for yc in ys]
        for i, r in zip(idx, rs):
            k1 = s * kg + i
            scr_ref[0, pl.ds(k1, FFT_N2, stride=n1 + FFT_PAD), :] = r[:128, :128] + r[128:, 128:]
            scr_ref[1, pl.ds(k1, FFT_N2, stride=n1 + FFT_PAD), :] = r[:128, 128:] - r[128:, :128]
        return c

    lax.fori_loop(0, kg // FFT_UNROLL, k1_group, 0)

    @pl.when(s == pl.num_programs(2) - 1)
    def _():
        def group(gidx, c):
            n2s = [gidx * FFT_UNROLL + u for u in range(FFT_UNROLL)]
            bcs = []
            for n2 in n2s:
                off = pl.multiple_of(n2 * (n1 + FFT_PAD), 8)
                bcs.append(jnp.concatenate([scr_ref[0, pl.ds(off, n1), :], scr_ref[1, pl.ds(off, n1), :]],
                                           axis=1).astype(BF16))
            rs = [_dot(gi_ref[n2], bc) for n2, bc in zip(n2s, bcs)]
            for n2, r in zip(n2s, rs):
                tp_ref[0, pl.ds(n2, nh, stride=pitch), :] = r[:nh, :128] + r[nh:, 128:]
                tp_ref[1, pl.ds(n2, nh, stride=pitch), :] = r[:nh, 128:] - r[nh:, :128]
            return c
        lax.fori_loop(0, FFT_N2 // FFT_UNROLL, group, 0)
        for e in range(2):
            for j in range(nh):
                rows = slice(j * FFT_N2, (j + 1) * FFT_N2)
                y = tp_ref[e, j * pitch:j * pitch + FFT_N2, :]
                if gated:
                    y = x1_ref[0, e, rows, :].astype(F32) * (y + z_ref[0, e, rows, :].astype(F32) * b_ref[...])
                o_ref[0, e, rows, :] = y.astype(BF16)


def _hyena_conv(z4, kf, order, length, x1_col0=None, bias=None):
    gf, gi, f2, n1 = _fft_tables(length)
    n = 2 * length
    kg = min(n1, 2 * FFT_UNROLL)
    npair = z4.shape[0]
    big = pl.Buffered(1)
    gated = bias is not None
    in_specs = [pl.BlockSpec((1, 2, length, 128), lambda p, c, s: (p, 0, 0, c)),
                pl.BlockSpec(gf.shape, lambda p, c, s: (0, 0, 0), pipeline_mode=big),
                pl.BlockSpec(f2.shape, lambda p, c, s: (0, 0)),
                pl.BlockSpec((1, 2, kg * FFT_N2, 128), lambda p, c, s: (order, 0, s, c)),
                pl.BlockSpec(gi.shape, lambda p, c, s: (0, 0, 0), pipeline_mode=big)]
    args = [z4, gf, f2, kf, gi]
    if gated:
        in_specs += [pl.BlockSpec((1, 2, length, 128), lambda p, c, s: (p, 0, 0, x1_col0 + c), pipeline_mode=big),
                     pl.BlockSpec((1, 128), lambda p, c, s: (0, c))]
        args += [z4, bias.reshape(1, D_HY)]
    return pl.pallas_call(
        functools.partial(_hyconv_kernel, n1, kg, gated), grid=(npair, 2, n1 // kg),
        in_specs=in_specs,
        out_specs=pl.BlockSpec((1, 2, length, 128), lambda p, c, s: (p, 0, 0, c),
                               pipeline_mode=pl.Buffered(1 if gated else 2)),
        out_shape=jax.ShapeDtypeStruct((npair, 2, length, 256), BF16),
        scratch_shapes=[pltpu.VMEM((2, FFT_N2 * (n1 + FFT_PAD), 128), F32),
                        pltpu.VMEM((2, (n1 // 2) * (FFT_N2 + FFT_PAD), 128), F32)],
        compiler_params=_params(("arbitrary", "arbitrary", "arbitrary"), FFTCONV_VMEM_LIMIT), name="hyena_fftconv",
    )(*args)


def _hyfft_kernel(n1, kg, z_ref, gf_ref, f2_ref, o_ref, scr_ref, tp_ref):
    s = pl.program_id(2)
    pitch = FFT_N2 + FFT_PAD

    @pl.when(s == 0)
    def _():
        for j in range(n1):
            tp_ref[j * pitch:j * pitch + FFT_N2, :] = z_ref[j * FFT_N2:(j + 1) * FFT_N2, :]
        _stage1(n1, lambda n2: tp_ref[pl.ds(n2, n1, stride=pitch), :].astype(BF16), gf_ref, scr_ref, 128)

    def k1_group(gidx, c):
        idx = [gidx * FFT_UNROLL + u for u in range(FFT_UNROLL)]
        xs = [_stage2(n1, s * kg + i, f2_ref, scr_ref) for i in idx]
        for i, (xr, xi) in zip(idx, xs):
            off = pl.multiple_of(i * FFT_N2, FFT_N2)
            o_ref[0, 0, pl.ds(off, FFT_N2), :] = xr
            o_ref[0, 1, pl.ds(off, FFT_N2), :] = xi
        return c

    lax.fori_loop(0, kg // FFT_UNROLL, k1_group, 0)


@functools.lru_cache(maxsize=None)
def _fft_table_full(length):
    n = 2 * length
    n1 = n // FFT_N2
    k1 = np.arange(n1)[None, :, None].astype(np.float64)
    j = np.arange(n1)[None, None, :].astype(np.float64)
    n2 = np.arange(FFT_N2)[:, None, None].astype(np.float64)
    ang = -2.0 * np.pi * k1 * (FFT_N2 * j + n2) / n
    return np.concatenate([np.cos(ang), np.sin(ang)], axis=1).astype(np.float32).astype(BF16)


def _filter_spectrum(k2, length):
    _, _, f2, n1 = _fft_tables(length)
    gf = _fft_table_full(length)
    n = 2 * length
    kg = min(n1, 16)
    nf = k2.shape[1] // D_HY
    big = pl.Buffered(1)
    return pl.pallas_call(
        functools.partial(_hyfft_kernel, n1, kg), grid=(nf, 2, n1 // kg),
        in_specs=[pl.BlockSpec((n, 128), lambda f, c, s: (0, 2 * f + c), pipeline_mode=big),
                  pl.BlockSpec(gf.shape, lambda f, c, s: (0, 0, 0), pipeline_mode=big),
                  pl.BlockSpec(f2.shape, lambda f, c, s: (0, 0))],
        out_specs=pl.BlockSpec((1, 2, kg * FFT_N2, 128), lambda f, c, s: (f, 0, s, c)),
        out_shape=jax.ShapeDtypeStruct((nf, 2, n, 256), F32),
        scratch_shapes=[pltpu.VMEM((2, FFT_N2 * (n1 + FFT_PAD), 128), F32),
                        pltpu.VMEM((n1 * (FFT_N2 + FFT_PAD), 128), F32)],
        compiler_params=_params(("arbitrary", "arbitrary", "arbitrary")), name="hyena_filter_fft",
    )(k2, gf, f2)


def _normalised(k2, ssq, extra_scale):
    return k2 * (lax.rsqrt(ssq + EPS) * extra_scale)


def _hy_ctx_kernel(lc, hv_ref, zh_ref, k2_ref, b_ref, fc_ref, ff_ref, ci_ref, o_ref):
    nf = 2 * lc
    hv = hv_ref[0].astype(F32)

    def conv(z, o):
        kf = _dot_exact_rhs(ff_ref[...], k2_ref[:, o * D_HY:(o + 1) * D_HY])
        zs = _dot(fc_ref[...], z.astype(BF16))
        zr, zi = zs[:nf], zs[nf:]
        kr, ki = kf[:nf], kf[nf:]
        ys = jnp.concatenate([zr * kr - zi * ki, zr * ki + zi * kr], axis=0).astype(BF16)
        return _dot(ci_ref[...], ys) * (1.0 / nf)

    v = hv[:, 0:D_HY]
    x1 = hv[:, D_HY:2 * D_HY]
    x2 = hv[:, 2 * D_HY:3 * D_HY]
    z1 = x1 * (conv(v, 0) + v * b_ref[0:1])
    p = x2 * (conv(z1, 1) + z1 * b_ref[1:2])
    o_ref[0] = (p * _silu(zh_ref[0].astype(F32))).astype(BF16)


def _dot_exact_rhs(a_bf16, b):
    bh, bm, bl = _split3(b)
    return _dot(a_bf16, bh) + _dot(a_bf16, bm) + _dot(a_bf16, bl)


@functools.lru_cache(maxsize=None)
def _ctx_dft_tables(lc):
    nf = 2 * lc
    k = np.arange(nf)[:, None].astype(np.float64)
    ang = -2.0 * np.pi * k * np.arange(nf)[None, :] / nf
    ff = np.concatenate([np.cos(ang), np.sin(ang)], axis=0)
    fc = ff[:, :lc]
    a2 = 2.0 * np.pi * np.arange(lc)[:, None] * np.arange(nf)[None, :] / nf
    ci = np.concatenate([np.cos(a2), -np.sin(a2)], axis=1)
    cv = lambda m: np.asarray(m, np.float32).astype(BF16)
    return cv(fc), cv(ff), cv(ci)


def _hy_ctx(hv, pf, k2c, hy_bias, nt, lc):
    nb = hv.shape[0]
    fc, ff, ci = _ctx_dft_tables(lc)
    c2 = lambda b: (0, 0)
    return pl.pallas_call(
        functools.partial(_hy_ctx_kernel, lc), grid=(nb,),
        in_specs=[pl.BlockSpec((1, lc, 768), lambda b: (b, nt, 0)),
                  pl.BlockSpec((1, lc, D_HY), lambda b: (b, nt, PF_ZH)),
                  pl.BlockSpec(k2c.shape, c2),
                  pl.BlockSpec((2, D_HY), c2),
                  pl.BlockSpec(fc.shape, c2), pl.BlockSpec(ff.shape, c2), pl.BlockSpec(ci.shape, c2)],
        out_specs=pl.BlockSpec((1, lc, D_HY), lambda b: (b, 0, 0)),
        out_shape=jax.ShapeDtypeStruct((nb, lc, D_HY), BF16),
        compiler_params=_params(("arbitrary",)), name="hyena_ctx",
    )(hv, pf, k2c, hy_bias, fc, ff, ci)


def _out_kernel(n_hy, ring, na_ref, h0_ref, h1_ref, om_ref, zm_ref, w_ref, x_ref, mod_ref, g_ref, *rest):
    hy_refs, o_ref = rest[:n_hy], rest[n_hy]
    if ring is not None:
        ntiles, to = ring
        xbuf, sem = rest[n_hy + 1], rest[n_hy + 2]
        lin = pl.program_id(0) * ntiles + pl.program_id(1)
        total = pl.num_programs(0) * ntiles

        def tile_copy(step):
            slot = step % 3
            src = x_ref.at[step // ntiles, pl.ds(pl.multiple_of((step % ntiles) * to, to), to), :]
            return pltpu.make_async_copy(src, xbuf.at[slot], sem.at[slot])

        @pl.when(lin == 0)
        def _():
            tile_copy(0).start()
            tile_copy(1).start()

        @pl.when(lin + 2 < total)
        def _():
            tile_copy(lin + 2).start()

        tile_copy(lin).wait()
        x_tile = xbuf.at[lin % 3]
    else:
        x_tile = x_ref.at[0]
    chunks = [slice(r, r + TT) for r in range(0, o_ref.shape[1], TT)]
    acts = []
    for rows in chunks:
        if n_hy == 1:
            hy = hy_refs[0][0, rows]
        else:
            y_ref, z_ref, x2_ref, zh_ref, b_ref = hy_refs
            z = z_ref[0, rows].astype(F32)
            hy = (x2_ref[0, rows].astype(F32) * (y_ref[0, rows].astype(F32) + z * b_ref[...])
                  * _silu(zh_ref[0, rows].astype(F32))).astype(BF16)
        hsum = h0_ref[0, 0, rows].astype(F32) + h1_ref[0, 0, rows].astype(F32)
        ml = (jax.nn.sigmoid(om_ref[0, rows].astype(F32)) * hsum * _silu(zm_ref[0, rows].astype(F32))).astype(BF16)
        acts.append((ml, hy))
    ys = [_dot(na_ref[0, rows], w_ref[0:D_NA]) + _dot(ml, w_ref[D_NA:D_NA + D_ML])
          + _dot(hy, w_ref[D_NA + D_ML:D_MODEL]) for rows, (ml, hy) in zip(chunks, acts)]
    for rows, y in zip(chunks, ys):
        ms = jnp.mean(y * y, axis=-1, keepdims=True)
        r = y * lax.rsqrt(ms + EPS) * g_ref[...]
        o_ref[0, rows] = x_tile[rows] + mod_ref[0][2:3] * r


def _out_projection(na_g, hml, pf, hy_args, w_out, res, res_t0, mod3, g_post, to, t0, ntiles, mod_row, out_rows,
                    prev=None):
    nb = na_g.shape[0]
    d = D_MODEL
    tok = lambda col: (lambda b, t: (b, t + t0, col))
    own = lambda col: (lambda b, t: (b, t, col))
    in_specs = [pl.BlockSpec((1, to, 512), tok(0)),
                pl.BlockSpec((1, 1, to, 256), lambda b, t: (b, 0, t + t0, 0)),
                pl.BlockSpec((1, 1, to, 256), lambda b, t: (b, 1, t + t0, 0)),
                pl.BlockSpec((1, to, 256), tok(PF_OM)), pl.BlockSpec((1, to, 256), tok(PF_ZM)),
                pl.BlockSpec((d, d), lambda b, t: (0, 0), pipeline_mode=pl.Buffered(1)),
                pl.BlockSpec((1, to, d), lambda b, t: (b, t + res_t0, 0)),
                pl.BlockSpec((1, 3, d), lambda b, t: (mod_row(b), 0, 0)),
                pl.BlockSpec((1, d), lambda b, t: (0, 0))]
    args = [na_g, hml, hml, pf, pf, w_out, res, mod3, g_post.reshape(1, d)]
    if len(hy_args) == 1:
        in_specs.append(pl.BlockSpec((1, to, D_HY), own(0)))
        args.append(hy_args[0])
    else:
        y2, z1, hv, pf_, bias1 = hy_args
        in_specs += [pl.BlockSpec((1, to, D_HY), own(0)), pl.BlockSpec((1, to, D_HY), own(0)),
                     pl.BlockSpec((1, to, D_HY), tok(2)), pl.BlockSpec((1, to, D_HY), tok(PF_ZH)),
                     pl.BlockSpec((1, D_HY), lambda b, t: (0, 0))]
        args += [y2, z1, hv, pf_, bias1.reshape(1, D_HY)]
    n_hy = len(hy_args)
    aliases = {}
    scratch = []
    if prev is not None:
        in_specs.append(pl.BlockSpec(memory_space=pl.ANY))
        args.append(prev)
        aliases = {len(args) - 1: 0}
        kern = lambda *r: _out_kernel(n_hy, None, *r[:9 + n_hy], r[10 + n_hy])
    else:
        assert res_t0 == 0 and nb * ntiles >= 2
        in_specs[6] = pl.BlockSpec(memory_space=pl.ANY)
        scratch = [pltpu.VMEM((3, to, d), F32), pltpu.SemaphoreType.DMA((3,))]
        kern = functools.partial(_out_kernel, n_hy, (ntiles, to))
    return pl.pallas_call(
        kern, grid=(nb, ntiles), in_specs=in_specs,
        out_specs=pl.BlockSpec((1, to, d), tok(0)),
        out_shape=jax.ShapeDtypeStruct((nb, out_rows, d), F32),
        input_output_aliases=aliases, scratch_shapes=scratch,
        compiler_params=_params(("arbitrary", "arbitrary")), name="out_proj_norm_residual",
    )(*args)


@functools.lru_cache(maxsize=None)
def _rope_tables(length, lc):
    n = HEAD_DIM // 4
    t = np.arange(length)
    inv = (ROPE_BASE ** (-np.arange(n, dtype=np.float32) / n)).astype(np.float32)
    pos = np.stack([t // GRID_W, t % GRID_W], axis=-1).astype(np.float32)
    ang = (pos[:, :, None] * inv).astype(np.float32)
    cos_h = np.concatenate([np.cos(ang[:, 0]), np.cos(ang[:, 0]), np.cos(ang[:, 1]), np.cos(ang[:, 1])], axis=-1)
    sin_h = np.concatenate([-np.sin(ang[:, 0]), np.sin(ang[:, 0]), -np.sin(ang[:, 1]), np.sin(ang[:, 1])], axis=-1)
    cos_t = np.concatenate([np.tile(cos_h, (1, 2 * H_ML)), np.ones((lc, 512), np.float32)], axis=0)
    sin_t = np.concatenate([np.tile(sin_h, (1, 2 * H_ML)), np.zeros((lc, 512), np.float32)], axis=0)
    kscale = np.concatenate([np.ones(256, np.float32), np.full(256, HEAD_DIM ** -0.5, np.float32)])
    return (cos_t * kscale).astype(np.float32), (sin_t * kscale).astype(np.float32)


@functools.lru_cache(maxsize=None)
def _scan_constants():
    tril = np.tril(np.ones((TT, TT), np.float32))
    emat = np.zeros((128, D_ML), np.float32)
    for h in range(H_ML):
        emat[h, h * HEAD_DIM:(h + 1) * HEAD_DIM] = 1.0
    rh = np.arange(D_ML)[:, None] // HEAD_DIM
    chd = np.arange(D_ML)[None, :] // HEAD_DIM
    bmask = (rh == chd).astype(np.float32)
    tri = np.concatenate([tril, tril.T], axis=0)
    causal = np.stack([np.where(tril > 0, 0.0, -BIG), np.where(tril.T > 0, 0.0, -BIG)]).astype(np.float32)
    return tri.astype(BF16), emat.astype(BF16), bmask, causal


def _layer(x_lat, x_ctx, ctx_tile, mod3, nb, length, lc, g_pre, g_post, w_in, b_if, conv_ml, conv_hy, rpb,
           hf_w1, hf_b1, hf_w2, hf_b2, hf_w3, hf_freq, hy_bias, w_out, update_ctx):
    nt = length // TT
    ltot = length + lc
    tri, emat, bmask, causal = _scan_constants()
    cos_t, sin_t = _rope_tables(length, lc)

    c = lambda a, b: w_in[:, a:b]
    wna = jnp.concatenate([c(0, 512) * (HEAD_DIM ** -0.5 * LOG2E), c(512, 1536)], axis=1).astype(BF16)
    wpr = jnp.concatenate([c(1536, 2048), c(2560, 2816), c(2816, 3072), c(3072, 3328), c(4112, 4368)],
                          axis=1).astype(BF16)
    wlo = jnp.concatenate([c(2048, 2560), c(3344, 4112), c(3328, 3344), jnp.zeros((D_MODEL, 112), F32)],
                          axis=1).astype(BF16)
    bif_row = jnp.zeros((1, 128), F32).at[0, :4 * H_ML].set(b_if.reshape(-1))
    ona, pf, qk, hv, gates, gates_t = _projection(x_lat, x_ctx, ctx_tile, mod3, g_pre, wna, wpr, wlo, conv_ml,
                                                  conv_hy, bif_row, cos_t, sin_t, tri, nt)

    na_g = _attention(ona, pf, _na_bias_table(rpb), nt, update_ctx)

    hml = _mlstm_scan(qk, pf, gates, gates_t, emat, bmask, causal, nt)

    k2, ssq = _hyena_filters(length, hf_w1, hf_b1, hf_w2, hf_b2, hf_w3, hf_freq)
    kf = _filter_spectrum(_normalised(k2, ssq, 1.0 / (2 * length)), length)
    hv4 = hv.reshape(nb // 2, 2, ltot, 768)
    z1 = _hyena_conv(hv4, kf, 0, length, x1_col0=2, bias=hy_bias[0])
    y2 = _hyena_conv(z1, kf, 1, length).reshape(nb, length, D_HY)
    z1 = z1.reshape(nb, length, D_HY)
    hy_args = (y2, z1, hv, pf, hy_bias[1])
    to = 4 * TT
    if not update_ctx:
        return _out_projection(na_g, hml, pf, hy_args, w_out, x_lat, 0, mod3, g_post, to, 0, length // to,
                               lambda b: b, length)

    xn = _out_projection(na_g, hml, pf, hy_args, w_out, x_lat, 0, mod3, g_post, to, 0, length // to,
                         lambda b: b, ltot)
    k2c, ssq_c = _hyena_filters(lc, hf_w1, hf_b1, hf_w2, hf_b2, hf_w3, hf_freq)
    hyc_g = _hy_ctx(hv, pf, _normalised(k2c, ssq_c, 1.0), hy_bias, nt, lc)
    return _out_projection(na_g, hml, pf, (hyc_g,), w_out, x_ctx, ctx_tile, mod3, g_post, TT, nt, 1,
                           lambda b: nb, ltot, prev=xn)


def kernel(x, c, ctx, c_ctx, w_ada, b_ada, g_pre, g_post, w_in, b_if, conv_ml, conv_hy, rpb, hf_w1, hf_b1,
           hf_w2, hf_b2, hf_w3, hf_freq, hy_bias, w_out):
    nb, length, d = x.shape
    lc = ctx.shape[1]
    depth = w_in.shape[0]
    assert lc == TT and length % (4 * TT) == 0 and nb % 2 == 0 and nb % SCAN_NB == 0 and d == D_MODEL
    cc = jnp.zeros((16, d), F32).at[:nb].set(c).at[nb].set(c_ctx)
    x_lat, x_ctx, ctx_tile = x, ctx, 0
    for l in range(depth):
        mod3 = _modulation(cc, w_ada[l], b_ada[l]).reshape(16, 3, d)
        x_lat = _layer(x_lat, x_ctx, ctx_tile, mod3, nb, length, lc, g_pre[l], g_post[l], w_in[l], b_if[l],
                       conv_ml[l], conv_hy[l], rpb[l], hf_w1[l], hf_b1[l], hf_w2[l], hf_b2[l], hf_w3[l],
                       hf_freq[l], hy_bias[l], w_out[l].astype(BF16), l < depth - 1)
        x_ctx, ctx_tile = x_lat, length // TT
    return x_lat
```
